```python
import math
import functools
import jax
import jax.numpy as jnp
from jax import lax
import numpy as np

D_MODEL = 1024
BATCH = 2
SEQ = 16384
DEPTH = 4

GRID_W = 64
CTX_LEN = 256

POOL_WIDTH = D_MODEL // 4
POOL_WINDOWS = (2, 4, 8, 16)
POOL_GROUPS = len(POOL_WINDOWS)
POOL_GROUP_DIM = POOL_WIDTH // POOL_GROUPS

HYENA_WIDTH = D_MODEL // 4
HYENA_SHORT = 3
HYENA_BANDS = 16
HYENA_EMB = 1 + 2 * HYENA_BANDS
HYENA_FILTER_HIDDEN = 64
HYENA_TARGET = 1e-2
HYENA_FAST_DECAY_PCT = 0.3
HYENA_SLOW_DECAY_PCT = 1.5

DIFF_HEADS = 4
DIFF_WIDTH = D_MODEL // 2
DIFF_V_DIM = DIFF_WIDTH // DIFF_HEADS
DIFF_HEAD_DIM = DIFF_V_DIM // 2
DIFF_QK_WIDTH = DIFF_HEADS * 2 * DIFF_HEAD_DIM
ROPE_FREQS = DIFF_HEAD_DIM // 4
ROPE_THETA = 10000.0
Q_BLOCK = 128

MIX_WIDTH = POOL_WIDTH + HYENA_WIDTH + DIFF_WIDTH
HYENA_OFF = POOL_WIDTH
ATT_OFF = POOL_WIDTH + 3 * HYENA_WIDTH
IN_WIDTH = ATT_OFF + 2 * DIFF_QK_WIDTH + DIFF_WIDTH

D_FF = ((8 * D_MODEL // 3 + 127) // 128) * 128
N_EXPERTS = 8
TOP_K = 2
D_FF_EXPERT = 7 * D_MODEL // 2
N_DENSE = (DEPTH + 1) // 2
N_MOE = DEPTH // 2
EPS = 1e-6

kernel_name = "hybrid_pool_hyena_diffattn_moe_dit"


def rmsnorm(x, g):
    xf = x.astype(jnp.float32)
    y = xf * lax.rsqrt(jnp.mean(xf * xf, axis=-1, keepdims=True) + EPS)
    return (y * g.astype(jnp.float32)).astype(x.dtype)


def modulate(x, shift, scale):
    return x * (1.0 + scale) + shift


def axial_rope_tables(L):
    rows = L // GRID_W
    row = jnp.repeat(jnp.arange(rows, dtype=jnp.float32), GRID_W)
    col = jnp.broadcast_to(jnp.arange(GRID_W, dtype=jnp.float32), (rows, GRID_W)).reshape(-1)
    inv_freq = jnp.power(ROPE_THETA, -jnp.arange(ROPE_FREQS, dtype=jnp.float32) / ROPE_FREQS)
    ang = jnp.stack([row, col], axis=-1)[:, :, None] * inv_freq
    return jnp.cos(ang), jnp.sin(ang)


def apply_axial_rope(x, cos, sin):
    shp = x.shape
    xr = x.astype(jnp.float32).reshape(shp[:-1] + (2, 2, ROPE_FREQS))
    a, b = xr[..., 0, :], xr[..., 1, :]
    c, s = cos[:, None, None], sin[:, None, None]
    out = jnp.stack([a * c - b * s, b * c + a * s], axis=-2)
    return out.reshape(shp).astype(x.dtype)


def attn_qkv(z_att, qk_g, rope):
    B, L, _ = z_att.shape
    q = z_att[..., :DIFF_QK_WIDTH].reshape(B, L, DIFF_HEADS, 2, DIFF_HEAD_DIM)
    k = z_att[..., DIFF_QK_WIDTH:2 * DIFF_QK_WIDTH].reshape(B, L, DIFF_HEADS, 2, DIFF_HEAD_DIM)
    v = z_att[..., 2 * DIFF_QK_WIDTH:].reshape(B, L, DIFF_HEADS, DIFF_V_DIM)
    q = rmsnorm(q, qk_g[0])
    k = rmsnorm(k, qk_g[1])
    if rope is not None:
        q = apply_axial_rope(q, rope[0], rope[1])
        k = apply_axial_rope(k, rope[0], rope[1])
    return q, k.transpose(0, 2, 3, 1, 4), v.transpose(0, 2, 1, 3)


def diff_lambda_value(lv, lam_init):
    lv = lv.astype(jnp.float32)
    return jnp.exp(jnp.sum(lv[0] * lv[1])) - jnp.exp(jnp.sum(lv[2] * lv[3])) + lam_init


def diff_attend(q, k, v, lam):
    s = jnp.einsum("bqhmd,bhmkd->bhmqk", q, k, preferred_element_type=jnp.float32) * (DIFF_HEAD_DIM ** -0.5)
    p = jax.nn.softmax(s, axis=-1)
    w = p[:, :, 0] - lam * p[:, :, 1]
    return jnp.einsum("bhqk,bhkd->bqhd", w.astype(v.dtype), v)


def diff_attend_blocks(q, k, v, lam):
    B, L = q.shape[:2]
    nb = L // Q_BLOCK
    qb = jnp.moveaxis(q.reshape((B, nb, Q_BLOCK) + q.shape[2:]), 1, 0)
    o = lax.map(lambda qblk: diff_attend(qblk, k, v, lam), qb)
    return jnp.moveaxis(o, 0, 1).reshape(B, L, DIFF_HEADS, DIFF_V_DIM)


def pool_mix(z, lin, scale):
    B, L, _ = z.shape
    zg = z.reshape(B, L, POOL_GROUPS, POOL_GROUP_DIM).astype(jnp.float32)
    cs = jnp.concatenate([jnp.zeros_like(zg[:, :1]), jnp.cumsum(zg, axis=1)], axis=1)
    t = jnp.arange(L)
    pooled = []
    for g, win in enumerate(POOL_WINDOWS):
        lo = jnp.clip(t - win // 2, 0, L)
        hi = jnp.clip(t + win // 2, 0, L)
        cnt = (hi - lo).astype(jnp.float32)[None, :, None]
        pooled.append((cs[:, hi, g] - cs[:, lo, g]) / cnt)
    d = (jnp.stack(pooled, axis=2) - zg).astype(z.dtype)
    y = jnp.einsum("blgc,gce->blge", d, lin).reshape(B, L, POOL_WIDTH)
    return y * scale


def short_conv(z, w, b):
    L = z.shape[1]
    pad = HYENA_SHORT // 2
    zp = jnp.pad(z, ((0, 0), (pad, HYENA_SHORT - 1 - pad), (0, 0)))
    y = b
    for j in range(HYENA_SHORT):
        y = y + zp[:, j:j + L] * w[j]
    return y


def hyena_filter(L, w1, b1, f1, w2, b2, f2, w3):
    f32 = jnp.float32
    t = jnp.linspace(0.0, 1.0, L, dtype=f32)[:, None]
    w = (2.0 * math.pi / L) * jnp.arange(L, dtype=f32)[:, None]
    bands = jnp.linspace(1e-4, HYENA_BANDS - 1, HYENA_BANDS, dtype=f32)[None, :]
    feat = jnp.concatenate([t, jnp.cos(bands * w), -jnp.sin(bands * w)], axis=-1)
    h = jnp.sin(f1.astype(f32) * (feat @ w1.astype(f32) + b1.astype(f32)))
    h = jnp.sin(f2.astype(f32) * (h @ w2.astype(f32) + b2.astype(f32)))
    h = (h @ w3.astype(f32)).reshape(L, 2, HYENA_WIDTH)
    max_decay = math.log(1.0 / HYENA_TARGET) / HYENA_FAST_DECAY_PCT
    min_decay = math.log(1.0 / HYENA_TARGET) / HYENA_SLOW_DECAY_PCT
    deltas = jnp.linspace(min_decay, max_decay, HYENA_WIDTH, dtype=f32)
    h = h * jnp.exp(-t * deltas)[:, None, :]
    k = jnp.concatenate([h[:, 0], jnp.zeros((1, HYENA_WIDTH), f32), jnp.flip(h[1:, 1], axis=0)], axis=0)
    return k / jnp.sum(jnp.abs(k), axis=0, keepdims=True)


def long_conv(u, k):
    L = u.shape[1]
    n = 2 * L
    uf = jnp.fft.rfft(u.astype(jnp.float32), n=n, axis=1)
    kf = jnp.fft.rfft(k, n=n, axis=0)
    return jnp.fft.irfft(uf * kf[None], n=n, axis=1)[:, :L].astype(u.dtype)


def hyena_mix(z, sw, sb, filt, bias):
    L = z.shape[1]
    x0, x1, v = jnp.split(short_conv(z, sw, sb), 3, axis=-1)
    v = v * x1
    y = long_conv(v, hyena_filter(L, *filt)) + v * bias
    return y * x0


def mix_merge(z, att, lam_init, pool_lin, pool_scale, sw, sb, filt, hy_bias, subln_g, w_out):
    B, L, _ = z.shape
    y_pool = pool_mix(z[..., :POOL_WIDTH], pool_lin, pool_scale)
    y_hy = hyena_mix(z[..., HYENA_OFF:ATT_OFF], sw, sb, filt, hy_bias)
    y_att = (rmsnorm(att, subln_g) * (1.0 - lam_init)).reshape(B, L, DIFF_WIDTH)
    return jnp.concatenate([y_pool, y_hy, y_att], axis=-1) @ w_out


def swiglu(u, w_gate, w_up, w_down):
    return (jax.nn.silu(u @ w_gate) * (u @ w_up)) @ w_down


def moe_swiglu(u, router_w, w_gate, w_up, w_down):
    shp = u.shape
    ut = u.reshape(-1, shp[-1])
    logits = jnp.matmul(ut, router_w, preferred_element_type=jnp.float32)
    top_val, top_idx = lax.top_k(logits, TOP_K)
    gate = jax.nn.softmax(top_val, axis=-1)
    comb = jnp.einsum("nk,nke->ne", gate, jax.nn.one_hot(top_idx, N_EXPERTS, dtype=jnp.float32))
    y = jnp.zeros(ut.shape, jnp.float32)
    for e in range(N_EXPERTS):
        he = jax.nn.silu(ut @ w_gate[e]) * (ut @ w_up[e])
        y = y + comb[:, e:e + 1] * (he @ w_down[e])
    return y.astype(u.dtype).reshape(shp)


def setup_inputs(seed: int = 0) -> dict:
    key = jax.random.key(seed)
    ks = iter(jax.random.split(key, 40))

    def nrm(shape, scale):
        return scale * jax.random.normal(next(ks), shape, jnp.float32)

    def gain(shape, scale):
        return 1.0 + nrm(shape, scale)

    D = D_MODEL
    return {
        "x": nrm((BATCH, SEQ, D), 1.0),
        "c": nrm((BATCH, D), 1.0),
        "ctx": nrm((BATCH, CTX_LEN, D), 1.0),
        "c_ctx": nrm((D,), 1.0),
        "mod_w": nrm((DEPTH, D, 6 * D), 0.5 * D ** -0.5),
        "mod_b": nrm((DEPTH, 6 * D), 0.01),
        "norm1_g": gain((DEPTH, D), 0.05),
        "norm2_g": gain((DEPTH, D), 0.05),
        "w_in": nrm((DEPTH, D, IN_WIDTH), D ** -0.5),
        "w_out": nrm((DEPTH, MIX_WIDTH, D), MIX_WIDTH ** -0.5),
        "pool_lin": nrm((DEPTH, POOL_GROUPS, POOL_GROUP_DIM, POOL_GROUP_DIM), POOL_GROUP_DIM ** -0.5),
        "pool_scale": gain((DEPTH, POOL_WIDTH), 0.1),
        "hy_short_w": nrm((DEPTH, HYENA_SHORT, 3 * HYENA_WIDTH), HYENA_SHORT ** -0.5),
        "hy_short_b": nrm((DEPTH, 3 * HYENA_WIDTH), 0.02),
        "hy_f_w1": nrm((DEPTH, HYENA_EMB, HYENA_FILTER_HIDDEN), HYENA_EMB ** -0.5),
        "hy_f_b1": nrm((DEPTH, HYENA_FILTER_HIDDEN), 0.1),
        "hy_f_freq1": gain((DEPTH, HYENA_FILTER_HIDDEN), 0.01),
        "hy_f_w2": nrm((DEPTH, HYENA_FILTER_HIDDEN, HYENA_FILTER_HIDDEN), HYENA_FILTER_HIDDEN ** -0.5),
        "hy_f_b2": nrm((DEPTH, HYENA_FILTER_HIDDEN), 0.1),
        "hy_f_freq2": gain((DEPTH, HYENA_FILTER_HIDDEN), 0.01),
        "hy_f_w3": nrm((DEPTH, HYENA_FILTER_HIDDEN, 2 * HYENA_WIDTH), HYENA_FILTER_HIDDEN ** -0.5),
        "hy_bias": nrm((DEPTH, HYENA_WIDTH), 1.0),
        "qk_norm_g": gain((DEPTH, 2, DIFF_HEAD_DIM), 0.05),
        "diff_lambda": nrm((DEPTH, 4, DIFF_HEAD_DIM), 0.1),
        "subln_g": gain((DEPTH, DIFF_V_DIM), 0.05),
        "ffn_w_gate": nrm((N_DENSE, D, D_FF), D ** -0.5),
        "ffn_w_up": nrm((N_DENSE, D, D_FF), D ** -0.5),
        "ffn_w_down": nrm((N_DENSE, D_FF, D), D_FF ** -0.5),
        "router_w": nrm((N_MOE, D, N_EXPERTS), D ** -0.5),
        "moe_w_gate": nrm((N_MOE, N_EXPERTS, D, D_FF_EXPERT), D ** -0.5),
        "moe_w_up": nrm((N_MOE, N_EXPERTS, D, D_FF_EXPERT), D ** -0.5),
        "moe_w_down": nrm((N_MOE, N_EXPERTS, D_FF_EXPERT, D), D_FF_EXPERT ** -0.5),
    }


def reference(x, c, ctx, c_ctx, mod_w, mod_b, norm1_g, norm2_g, w_in, w_out,
              pool_lin, pool_scale, hy_short_w, hy_short_b, hy_f_w1, hy_f_b1, hy_f_freq1,
              hy_f_w2, hy_f_b2, hy_f_freq2, hy_f_w3, hy_bias, qk_norm_g, diff_lambda, subln_g,
              ffn_w_gate, ffn_w_up, ffn_w_down, router_w, moe_w_gate, moe_w_up, moe_w_down):
    L = x.shape[1]
    rope = axial_rope_tables(L)
    silu_c = jax.nn.silu(c)[:, None, :]
    silu_cc = jax.nn.silu(c_ctx)
    h = ctx
    for i in range(DEPTH):
        last = i == DEPTH - 1
        m_lat = jnp.split(silu_c @ mod_w[i] + mod_b[i], 6, axis=-1)
        m_ctx = jnp.split(silu_cc @ mod_w[i] + mod_b[i], 6, axis=-1)
        lam_init = 0.8 - 0.6 * math.exp(-0.3 * i)
        lam = diff_lambda_value(diff_lambda[i], lam_init)
        filt = (hy_f_w1[i], hy_f_b1[i], hy_f_freq1[i], hy_f_w2[i], hy_f_b2[i], hy_f_freq2[i], hy_f_w3[i])
        merge = functools.partial(mix_merge, lam_init=lam_init, pool_lin=pool_lin[i], pool_scale=pool_scale[i],
                                  sw=hy_short_w[i], sb=hy_short_b[i], filt=filt, hy_bias=hy_bias[i],
                                  subln_g=subln_g[i], w_out=w_out[i])
        z_lat = modulate(rmsnorm(x, norm1_g[i]), m_lat[0], m_lat[1]) @ w_in[i]
        z_ctx = modulate(rmsnorm(h, norm1_g[i]), m_ctx[0], m_ctx[1]) @ w_in[i]
        q_l, k_l, v_l = attn_qkv(z_lat[..., ATT_OFF:], qk_norm_g[i], rope)
        q_c, k_c, v_c = attn_qkv(z_ctx[..., ATT_OFF:], qk_norm_g[i], None)
        att_lat = diff_attend_blocks(q_l, jnp.concatenate([k_c, k_l], axis=3),
                                     jnp.concatenate([v_c, v_l], axis=2), lam)
        x = x + m_lat[2] * merge(z_lat, att_lat)
        if not last:
            att_ctx = diff_attend(q_c, k_c, v_c, lam)
            h = h + m_ctx[2] * merge(z_ctx, att_ctx)
        if i % 2 == 0:
            j = i // 2
            ffn = functools.partial(swiglu, w_gate=ffn_w_gate[j], w_up=ffn_w_up[j], w_down=ffn_w_down[j])
        else:
            j = i // 2
            ffn = functools.partial(moe_swiglu, router_w=router_w[j], w_gate=moe_w_gate[j],
                                    w_up=moe_w_up[j], w_down=moe_w_down[j])
        x = x + m_lat[5] * ffn(modulate(rmsnorm(x, norm2_g[i]), m_lat[3], m_lat[4]))
        if not last:
            h = h + m_ctx[5] * ffn(modulate(rmsnorm(h, norm2_g[i]), m_ctx[3], m_ctx[4]))
    return x
```

```python
import functools
import math

import numpy as np
import jax
import jax.numpy as jnp
from jax import lax
from jax.experimental import pallas as pl
from jax.experimental.pallas import tpu as pltpu

F32 = jnp.float32
BF16 = jnp.bfloat16
EPS = 1e-6

GRID_W = 64
POOL_WINDOWS = (2, 4, 8, 16)
HEADS = 4
HEAD_DIM = 64
V_DIM = 128
ROPE_FREQS = 16
ROPE_THETA = 10000.0
HY_BANDS = 16
HY_TARGET = 1e-2
HY_FAST = 0.3
HY_SLOW = 1.5
N_EXPERTS = 8

ROW_TILE = 512
HALO = 32
FFT_N1 = 128
FFT_COLS = 2048
ATT_TQ = 512
ATT_TK = 512
MOE_TM = 512
MOE_FC = 512
VMEM_LIMIT = 56 * 1024 * 1024


def _cparams(sem):
    return pltpu.CompilerParams(dimension_semantics=sem, vmem_limit_bytes=VMEM_LIMIT)


def _dot(a, b):
    return jnp.dot(a, b, preferred_element_type=F32)


def _split(a):
    hi = a.astype(BF16)
    lo = (a - hi.astype(F32)).astype(BF16)
    return hi, lo


def _dot3(a, b):
    ah, al = _split(a)
    bh, bl = _split(b)
    return _dot(ah, bh) + _dot(ah, bl) + _dot(al, bh)


def _dot3c(ch, cl, d):
    dh, dl = _split(d)
    return _dot(ch, dh) + _dot(ch, dl) + _dot(cl, dh)


def _silu(x):
    return x / (1.0 + jnp.exp(-x))


def _normmod(x, g, shift, scale):
    ms = jnp.mean(x * x, axis=-1, keepdims=True)
    return (x * lax.rsqrt(ms + EPS) * g) * (1.0 + scale) + shift


def _cond_row(t, n_lat_tiles, tiles_per_batch, n_cond):
    return jnp.where(t >= n_lat_tiles, n_cond - 1, t // tiles_per_batch)


def _modvec_kernel(c_ref, w_ref, b_ref, o_ref):
    s = _silu(c_ref[...])
    o_ref[0, 0] = _dot3(s, w_ref[0]) + b_ref[0]


def _modvec(cond8, mod_w, mod_b):
    depth, d, six_d = mod_w.shape
    nchunk = six_d // d
    return pl.pallas_call(
        _modvec_kernel,
        grid=(depth, nchunk),
        in_specs=[
            pl.BlockSpec((8, d), lambda i, j: (0, 0)),
            pl.BlockSpec((1, d, d), lambda i, j: (i, 0, j)),
            pl.BlockSpec((1, 1, d), lambda i, j: (i, 0, j)),
        ],
        out_specs=pl.BlockSpec((1, 1, 8, d), lambda i, j: (i, j, 0, 0)),
        out_shape=jax.ShapeDtypeStruct((depth, nchunk, 8, d), F32),
        compiler_params=_cparams(("arbitrary", "arbitrary")),
    )(cond8, mod_w, mod_b.reshape(depth, 1, six_d))


def _in_proj_kernel(x_ref, g_ref, mod_ref, w_ref, zp_ref, zh_ref, za_ref, *, nlt, tpb, ncond, pw, hw):
    ci = _cond_row(pl.program_id(0), nlt, tpb, ncond)
    shift = mod_ref[0, pl.ds(ci, 1), :]
    scale = mod_ref[1, pl.ds(ci, 1), :]
    y = _normmod(x_ref[...], g_ref[...], shift, scale).astype(BF16)
    z = _dot(y, w_ref[...])
    zp_ref[...] = z[:, :pw].astype(BF16)
    zh_ref[...] = z[:, pw:pw + 3 * hw].astype(BF16)
    za_ref[...] = z[:, pw + 3 * hw:].astype(BF16)


def _in_proj(x, g, mod, w, *, nlt, tpb, ncond, pw, hw):
    n, d = x.shape
    wid = w.shape[1]
    aw = wid - pw - 3 * hw
    tm = ROW_TILE
    kern = functools.partial(_in_proj_kernel, nlt=nlt, tpb=tpb, ncond=ncond, pw=pw, hw=hw)
    return pl.pallas_call(
        kern,
        grid=(n // tm,),
        in_specs=[
            pl.BlockSpec((tm, d), lambda t: (t, 0)),
            pl.BlockSpec((1, d), lambda t: (0, 0)),
            pl.BlockSpec(mod.shape, lambda t: (0, 0, 0)),
            pl.BlockSpec((d, wid), lambda t: (0, 0)),
        ],
        out_specs=[
            pl.BlockSpec((tm, pw), lambda t: (t, 0)),
            pl.BlockSpec((tm, 3 * hw), lambda t: (t, 0)),
            pl.BlockSpec((tm, aw), lambda t: (t, 0)),
        ],
        out_shape=[
            jax.ShapeDtypeStruct((n, pw), BF16),
            jax.ShapeDtypeStruct((n, 3 * hw), BF16),
            jax.ShapeDtypeStruct((n, aw), BF16),
        ],
        compiler_params=_cparams(("arbitrary",)),
    )(x, g, mod, w)


def _prep_kernel(za_ref, cos_ref, sin_ref, g_ref, qt_ref, k_ref, vt_ref, *, qscale):
    tm = za_ref.shape[0]
    lane = lax.broadcasted_iota(jnp.int32, (tm, 2 * HEAD_DIM), 1)
    first = lane < HEAD_DIM
    apart = (lane % (2 * ROPE_FREQS)) < ROPE_FREQS
    cos = cos_ref[...]
    sin = sin_ref[...]
    qk_w = HEADS * 2 * HEAD_DIM

    def norm_rope(x, g):
        x2 = x * x
        s_all = jnp.sum(x2, axis=-1, keepdims=True)
        s_first = jnp.sum(jnp.where(first, x2, 0.0), axis=-1, keepdims=True)
        ms = jnp.where(first, s_first, s_all - s_first) * (1.0 / HEAD_DIM)
        xn = x * lax.rsqrt(ms + EPS) * g
        swapped = jnp.where(apart, pltpu.roll(xn, 2 * HEAD_DIM - ROPE_FREQS, 1), pltpu.roll(xn, ROPE_FREQS, 1))
        return xn * cos + swapped * sin

    for h in range(HEADS):
        lo = h * 2 * HEAD_DIM
        q = norm_rope(za_ref[:, lo:lo + 2 * HEAD_DIM].astype(F32), g_ref[0:1, :])
        qt_ref[h] = (q * qscale).T.astype(BF16)
        k = norm_rope(za_ref[:, qk_w + lo:qk_w + lo + 2 * HEAD_DIM].astype(F32), g_ref[1:2, :])
        k_ref[h] = k.astype(BF16)
        vlo = 2 * qk_w + h * V_DIM
        vt_ref[h] = za_ref[:, vlo:vlo + V_DIM].astype(F32).T.astype(BF16)


def _attn_prep(za, cos_t, sin_t, g2):
    n, aw = za.shape
    tm = ROW_TILE
    qscale = (HEAD_DIM ** -0.5) * math.log2(math.e)
    return pl.pallas_call(
        functools.partial(_prep_kernel, qscale=qscale),
        grid=(n // tm,),
        in_specs=[
            pl.BlockSpec((tm, aw), lambda t: (t, 0)),
            pl.BlockSpec((tm, 2 * HEAD_DIM), lambda t: (t, 0)),
            pl.BlockSpec((tm, 2 * HEAD_DIM), lambda t: (t, 0)),
            pl.BlockSpec((2, 2 * HEAD_DIM), lambda t: (0, 0)),
        ],
        out_specs=[
            pl.BlockSpec((HEADS, 2 * HEAD_DIM, tm), lambda t: (0, 0, t)),
            pl.BlockSpec((HEADS, tm, 2 * HEAD_DIM), lambda t: (0, t, 0)),
            pl.BlockSpec((HEADS, V_DIM, tm), lambda t: (0, 0, t)),
        ],
        out_shape=[
            jax.ShapeDtypeStruct((HEADS, 2 * HEAD_DIM, n), BF16),
            jax.ShapeDtypeStruct((HEADS, n, 2 * HEAD_DIM), BF16),
            jax.ShapeDtypeStruct((HEADS, V_DIM, n), BF16),
        ],
        compiler_params=_cparams(("arbitrary",)),
    )(za, cos_t, sin_t, g2)


def _attn_kernel(lam_ref, qt_ref, kc_ref, vtc_ref, *rest, tk, n_lat_chunks, lam_init):
    if n_lat_chunks:
        kl_ref, vtl_ref, g_ref, o_ref, m_ref, l_ref, acc_ref = rest
    else:
        g_ref, o_ref, m_ref, l_ref, acc_ref = rest
    qt = qt_ref[0]
    row = lax.broadcasted_iota(jnp.int32, qt.shape, 0)
    zero = jnp.zeros_like(qt)
    qmaps = (jnp.where(row < HEAD_DIM, qt, zero), jnp.where(row >= HEAD_DIM, qt, zero))

    def process(k_tile, vt_tile, first):
        for mi in range(2):
            s = _dot(k_tile, qmaps[mi])
            smax = jnp.max(s, axis=0, keepdims=True)
            if first:
                m_new = smax
            else:
                m_old = m_ref[mi]
                m_new = jnp.maximum(m_old, smax)
            p = jnp.exp2(s - m_new)
            psum = jnp.sum(p, axis=0, keepdims=True)
            pv = _dot(vt_tile, p.astype(BF16))
            if first:
                l_ref[mi] = psum
                acc_ref[mi] = pv
            else:
                alpha = jnp.exp2(m_old - m_new)
                l_ref[mi] = alpha * l_ref[mi] + psum
                acc_ref[mi] = alpha * acc_ref[mi] + pv
            m_ref[mi] = m_new

    process(kc_ref[0], vtc_ref[0], True)
    if n_lat_chunks:
        def body(j, carry):
            off = pl.multiple_of(j * tk, tk)
            process(kl_ref[0, pl.ds(off, tk), :], vtl_ref[0, :, pl.ds(off, tk)], False)
            return carry
        lax.fori_loop(0, n_lat_chunks, body, 0)

    lv = lam_ref[...]
    lam = (jnp.exp(jnp.sum(lv[0:1] * lv[1:2], axis=-1, keepdims=True))
           - jnp.exp(jnp.sum(lv[2:3] * lv[3:4], axis=-1, keepdims=True)) + lam_init)
    o = acc_ref[0] * (1.0 / l_ref[0]) - lam * (acc_ref[1] * (1.0 / l_ref[1]))
    ms = jnp.mean(o * o, axis=0, keepdims=True)
    y = o * lax.rsqrt(ms + EPS) * g_ref[...] * (1.0 - lam_init)
    o_ref[...] = y.T.astype(BF16)


def _attention(lam_p, qt, k12, vt, g_b, *, batch, seq, ctx_len, latent, lam_init):
    n = k12.shape[1]
    nl = batch * seq
    tq = ATT_TQ if latent else ctx_len
    nq = seq // tq if latent else 1
    tk = ATT_TK
    qbase = 0 if latent else nl // tq

    def qrow(b, iq):
        return (b * nq + iq) if latent else (qbase + b)

    in_specs = [
        pl.BlockSpec(lam_p.shape, lambda b, h, iq: (0, 0)),
        pl.BlockSpec((1, 2 * HEAD_DIM, tq), lambda b, h, iq: (h, 0, qrow(b, iq))),
        pl.BlockSpec((1, ctx_len, 2 * HEAD_DIM), lambda b, h, iq: (h, nl // ctx_len + b, 0)),
        pl.BlockSpec((1, V_DIM, ctx_len), lambda b, h, iq: (h, 0, nl // ctx_len + b)),
    ]
    args = [lam_p, qt, k12, vt]
    if latent:
        in_specs += [
            pl.BlockSpec((1, seq, 2 * HEAD_DIM), lambda b, h, iq: (h, b, 0)),
            pl.BlockSpec((1, V_DIM, seq), lambda b, h, iq: (h, 0, b)),
        ]
        args += [k12, vt]
    in_specs.append(pl.BlockSpec((V_DIM, tq), lambda b, h, iq: (0, 0)))
    args.append(g_b[:, :tq])
    nrows = nl if latent else batch * ctx_len
    kern = functools.partial(_attn_kernel, tk=tk, n_lat_chunks=(seq // tk if latent else 0), lam_init=lam_init)
    return pl.pallas_call(
        kern,
        grid=(batch, HEADS, nq),
        in_specs=in_specs,
        out_specs=pl.BlockSpec((tq, V_DIM), lambda b, h, iq: (b * nq + iq, h)),
        out_shape=jax.ShapeDtypeStruct((nrows, HEADS * V_DIM), BF16),
        scratch_shapes=[
            pltpu.VMEM((2, 1, tq), F32),
            pltpu.VMEM((2, 1, tq), F32),
            pltpu.VMEM((2, V_DIM, tq), F32),
        ],
        compiler_params=_cparams(("arbitrary", "arbitrary", "arbitrary")),
    )(*args)


def _halo_specs(width, *, row0, seq, tile):
    hb = tile // HALO

    def cur(b, i):
        return ((row0 + b * seq) // tile + i, 0)

    def prev(b, i):
        first = (row0 + b * seq) // HALO
        return (jnp.maximum(first + i * hb - 1, first), 0)

    def nxt(b, i):
        first = (row0 + b * seq) // HALO
        return (jnp.minimum(first + (i + 1) * hb, first + seq // HALO - 1), 0)

    return [pl.BlockSpec((HALO, width), prev), pl.BlockSpec((tile, width), cur), pl.BlockSpec((HALO, width), nxt)]


def _fill_ext(ext_ref, prev_ref, cur_ref, next_ref, *, seq, tile):
    pos0 = pl.program_id(1) * tile
    width = cur_ref.shape[1]
    hpos = lax.broadcasted_iota(jnp.int32, (HALO, width), 0)
    ext_ref[0:HALO, :] = jnp.where(pos0 - HALO + hpos >= 0, prev_ref[...].astype(F32), 0.0)
    ext_ref[HALO:HALO + tile, :] = cur_ref[...].astype(F32)
    ext_ref[HALO + tile:, :] = jnp.where(pos0 + tile + hpos < seq, next_ref[...].astype(F32), 0.0)


def _pool_kernel(prev_ref, cur_ref, next_ref, lin_ref, scale_ref, o_ref, ext_ref, s_ref, *, seq, tile):
    _fill_ext(ext_ref, prev_ref, cur_ref, next_ref, seq=seq, tile=tile)
    width = cur_ref.shape[1]
    gd = width // len(POOL_WINDOWS)
    n0 = tile + 2 * HALO
    s_ref[0, 0:n0 - 8, :] = ext_ref[0:n0 - 8, :] + ext_ref[1:n0 - 7, :]
    for k in range(1, len(POOL_WINDOWS)):
        step = 1 << k
        ln = n0 - 8 * (k + 1)
        s_ref[k, 0:ln, :] = s_ref[k - 1, 0:ln, :] + s_ref[k - 1, step:step + ln, :]
    lane = lax.broadcasted_iota(jnp.int32, (tile, width), 1)
    pos = pl.program_id(1) * tile + lax.broadcasted_iota(jnp.int32, (tile, width), 0)
    grp = lane // gd
    wsum = jnp.zeros((tile, width), F32)
    half = jnp.zeros((tile, width), jnp.int32)
    for k, win in enumerate(POOL_WINDOWS):
        start = HALO - win // 2
        wsum = jnp.where(grp == k, s_ref[k, start:start + tile, :], wsum)
        half = jnp.where(grp == k, win // 2, half)
    cnt = jnp.minimum(pos + half, seq) - jnp.maximum(pos - half, 0)
    z = ext_ref[HALO:HALO + tile, :]
    dlt = (wsum / cnt.astype(F32) - z).astype(BF16)
    o_ref[...] = (_dot(dlt, lin_ref[...]) * scale_ref[...]).astype(BF16)


def _pool(zp, lin_bd, scale, *, row0, batch, seq, out_rows, out_row0):
    width = zp.shape[1]
    tile = min(ROW_TILE, seq)
    kern = functools.partial(_pool_kernel, seq=seq, tile=tile)
    return pl.pallas_call(
        kern,
        grid=(batch, seq // tile),
        in_specs=_halo_specs(width, row0=row0, seq=seq, tile=tile) + [
            pl.BlockSpec((width, width), lambda b, i: (0, 0)),
            pl.BlockSpec((1, width), lambda b, i: (0, 0)),
        ],
        out_specs=pl.BlockSpec((tile, width), lambda b, i: ((out_row0 + b * seq) // tile + i, 0)),
        out_shape=jax.ShapeDtypeStruct((out_rows, width), BF16),
        scratch_shapes=[
            pltpu.VMEM((tile + 2 * HALO, width), F32),
            pltpu.VMEM((4, tile + 2 * HALO, width), F32),
        ],
        compiler_params=_cparams(("arbitrary", "arbitrary")),
    )(zp, zp, zp, lin_bd, scale)


def _hy_pre_kernel(prev_ref, cur_ref, next_ref, w_ref, b_ref, u_ref, x0_ref, ext_ref, *, seq, tile, hw):
    _fill_ext(ext_ref, prev_ref, cur_ref, next_ref, seq=seq, tile=tile)
    y = b_ref[...] + ext_ref[HALO - 1:HALO - 1 + tile, :] * w_ref[0:1, :]
    y = y + ext_ref[HALO:HALO + tile, :] * w_ref[1:2, :]
    y = y + ext_ref[HALO + 1:HALO + 1 + tile, :] * w_ref[2:3, :]
    x0_ref[...] = y[:, :hw]
    u_ref[...] = y[:, 2 * hw:] * y[:, hw:2 * hw]


def _hy_pre(zh, sw, sb, *, row0, batch, seq, out_rows, out_row0):
    width = zh.shape[1]
    hw = width // 3
    tile = min(ROW_TILE, seq)
    kern = functools.partial(_hy_pre_kernel, seq=seq, tile=tile, hw=hw)
    ospec = pl.BlockSpec((tile, hw), lambda b, i: ((out_row0 + b * seq) // tile + i, 0))
    return pl.pallas_call(
        kern,
        grid=(batch, seq // tile),
        in_specs=_halo_specs(width, row0=row0, seq=seq, tile=tile) + [
            pl.BlockSpec((3, width), lambda b, i: (0, 0)),
            pl.BlockSpec((1, width), lambda b, i: (0, 0)),
        ],
        out_specs=[ospec, ospec],
        out_shape=[jax.ShapeDtypeStruct((out_rows, hw), F32), jax.ShapeDtypeStruct((out_rows, hw), F32)],
        scratch_shapes=[pltpu.VMEM((tile + 2 * HALO, width), F32)],
        compiler_params=_cparams(("arbitrary", "arbitrary")),
    )(zh, zh, zh, sw, sb)


def _filter_kernel(feat_ref, w1_ref, b1_ref, f1_ref, w2_ref, b2_ref, f2_ref, w3_ref, dl_ref,
                   hf_ref, hb_ref, asum_ref, *, tile, hw):
    feat = feat_ref[...]
    h = jnp.sin(f1_ref[...] * (_dot3(feat, w1_ref[...]) + b1_ref[...]))
    h = jnp.sin(f2_ref[...] * (_dot3(h, w2_ref[...]) + b2_ref[...]))
    h = _dot3(h, w3_ref[...])
    dec = jnp.exp(-feat[:, 0:1] * dl_ref[...])
    pos = pl.program_id(0) * tile + lax.broadcasted_iota(jnp.int32, (tile, hw), 0)
    hf = h[:, :hw] * dec
    hb = jnp.where(pos == 0, 0.0, h[:, hw:] * dec)
    hf_ref[...] = hf
    hb_ref[...] = hb
    part = jnp.sum(jnp.abs(hf) + jnp.abs(hb), axis=0, keepdims=True)

    @pl.when(pl.program_id(0) == 0)
    def _():
        asum_ref[...] = part

    @pl.when(pl.program_id(0) != 0)
    def _():
        asum_ref[...] = asum_ref[...] + part


def _hy_filter(seq, w1, b1, f1, w2, b2, f2, w3, hw):
    t = jnp.linspace(0.0, 1.0, seq, dtype=F32)[:, None]
    w = (2.0 * math.pi / seq) * jnp.arange(seq, dtype=F32)[:, None]
    bands = jnp.linspace(1e-4, HY_BANDS - 1, HY_BANDS, dtype=F32)[None, :]
    emb = 1 + 2 * HY_BANDS
    embp = ((emb + 7) // 8) * 8
    feat = jnp.concatenate([t, jnp.cos(bands * w), -jnp.sin(bands * w), jnp.zeros((seq, embp - emb), F32)], axis=-1)
    w1p = jnp.concatenate([w1, jnp.zeros((embp - emb, w1.shape[1]), F32)], axis=0)
    max_decay = math.log(1.0 / HY_TARGET) / HY_FAST
    min_decay = math.log(1.0 / HY_TARGET) / HY_SLOW
    deltas = jnp.linspace(min_decay, max_decay, hw, dtype=F32)[None, :]
    tile = min(ROW_TILE, seq)
    hid = w1.shape[1]
    full = lambda shape: pl.BlockSpec(shape, lambda i: (0,) * len(shape))
    return pl.pallas_call(
        functools.partial(_filter_kernel, tile=tile, hw=hw),
        grid=(seq // tile,),
        in_specs=[
            pl.BlockSpec((tile, embp), lambda i: (i, 0)),
            full((embp, hid)), full((1, hid)), full((1, hid)),
            full((hid, hid)), full((1, hid)), full((1, hid)),
            full((hid, 2 * hw)), full((1, hw)),
        ],
        out_specs=[
            pl.BlockSpec((tile, hw), lambda i: (i, 0)),
            pl.BlockSpec((tile, hw), lambda i: (i, 0)),
            pl.BlockSpec((1, hw), lambda i: (0, 0)),
        ],
        out_shape=[
            jax.ShapeDtypeStruct((seq, hw), F32),
            jax.ShapeDtypeStruct((seq, hw), F32),
            jax.ShapeDtypeStruct((1, hw), F32),
        ],
        compiler_params=_cparams(("arbitrary",)),
    )(feat, w1p, b1[None, :], f1[None, :], w2, b2[None, :], f2[None, :], w3, deltas)


def _np_split(a):
    a32 = jnp.asarray(a, F32)
    hi = a32.astype(BF16)
    lo = (a32 - hi.astype(F32)).astype(BF16)
    return hi, lo


def _dft_consts(n2):
    n1 = FFT_N1
    n = n1 * n2
    half = n1 // 2
    a1 = -2.0 * np.pi * np.outer(np.arange(n1), np.arange(n1)) / n1
    f1r, f1i = np.cos(a1), np.sin(a1)
    f1_data = np.block([[f1r[:, :half], -f1i[:, :half]], [f1i[:, :half], f1r[:, :half]]])
    f1_real = np.concatenate([f1r, f1i], axis=0)
    g1r, g1i = f1r / n, -f1i / n
    g1 = np.block([[g1r[:half], -g1i[:half]], [g1i[:half], g1r[:half]]])
    a2 = -2.0 * np.pi * np.outer(np.arange(n2), np.arange(n2)) / n2
    f2r, f2i = np.cos(a2), np.sin(a2)
    f2 = np.block([[f2r, -f2i], [f2i, f2r]])
    g2 = np.block([[f2r, f2i], [-f2i, f2r]])
    at = -2.0 * np.pi * np.outer(np.arange(n1), np.arange(n2)) / n
    twr = jnp.broadcast_to(jnp.asarray(np.cos(at), F32)[:, :, None], (n1, n2, 128))
    twi = jnp.broadcast_to(jnp.asarray(np.sin(at), F32)[:, :, None], (n1, n2, 128))
    return dict(f1_data=_np_split(f1_data), f1_real=_np_split(f1_real), g1=_np_split(g1),
                f2=_np_split(f2), g2=_np_split(g2), twr=twr, twi=twi)


def _fft1_kernel(z_ref, fh_ref, fl_ref, ar_ref, ai_ref):
    a = _dot3c(fh_ref[...], fl_ref[...], z_ref[...])
    ar_ref[...] = a[:FFT_N1]
    ai_ref[...] = a[FFT_N1:]


def _fft_first(zview, fmat, ncols):
    cb = min(FFT_COLS, ncols)
    fh, fl = fmat
    cspec = pl.BlockSpec(fh.shape, lambda j: (0, 0))
    ospec = pl.BlockSpec((FFT_N1, cb), lambda j: (0, j))
    return pl.pallas_call(
        _fft1_kernel,
        grid=(ncols // cb,),
        in_specs=[pl.BlockSpec((FFT_N1, cb), lambda j: (0, j)), cspec, cspec],
        out_specs=[ospec, ospec],
        out_shape=[jax.ShapeDtypeStruct((FFT_N1, ncols), F32)] * 2,
        compiler_params=_cparams(("arbitrary",)),
    )(zview, fh, fl)


def _fftmid_kernel(ar_ref, ai_ref, twr_ref, twi_ref, f2h_ref, f2l_ref, *rest, filter_only, n2):
    width = ar_ref.shape[2]
    reps = width // 128
    twr = jnp.concatenate([twr_ref[0]] * reps, axis=1)
    twi = jnp.concatenate([twi_ref[0]] * reps, axis=1)
    ar, ai = ar_ref[0], ai_ref[0]
    z = jnp.concatenate([ar * twr - ai * twi, ar * twi + ai * twr], axis=0)
    x = _dot3c(f2h_ref[...], f2l_ref[...], z)
    xr, xi = x[:n2], x[n2:]
    if filter_only:
        kr_out, ki_out = rest
        kr_out[0] = xr
        ki_out[0] = xi
        return
    kr_ref, ki_ref, g2h_ref, g2l_ref, br_out, bi_out = rest
    kr, ki = kr_ref[0], ki_ref[0]
    y = jnp.concatenate([xr * kr - xi * ki, xr * ki + xi * kr], axis=0)
    w = _dot3c(g2h_ref[...], g2l_ref[...], y)
    wr, wi = w[:n2], w[n2:]
    br_out[0] = wr * twr + wi * twi
    bi_out[0] = wi * twr - wr * twi


def _fft_mid(ar, ai, consts, n2, width, spectrum=None):
    a3r = ar.reshape(FFT_N1, n2, width)
    a3i = ai.reshape(FFT_N1, n2, width)
    blk = pl.BlockSpec((1, n2, width), lambda k: (k, 0, 0))
    twspec = pl.BlockSpec((1, n2, 128), lambda k: (k, 0, 0))
    cspec = pl.BlockSpec((2 * n2, 2 * n2), lambda k: (0, 0))
    in_specs = [blk, blk, twspec, twspec, cspec, cspec]
    args = [a3r, a3i, consts["twr"], consts["twi"], *consts["f2"]]
    if spectrum is not None:
        in_specs += [blk, blk, cspec, cspec]
        args += [spectrum[0], spectrum[1], *consts["g2"]]
    return pl.pallas_call(
        functools.partial(_fftmid_kernel, filter_only=spectrum is None, n2=n2),
        grid=(FFT_N1,),
        in_specs=in_specs,
        out_specs=[blk, blk],
        out_shape=[jax.ShapeDtypeStruct((FFT_N1, n2, width), F32)] * 2,
        compiler_params=_cparams(("arbitrary",)),
    )(*args)


def _fftlast_kernel(br_ref, bi_ref, gh_ref, gl_ref, u_ref, x0_ref, bias_ref, invn_ref, o_ref):
    b = jnp.concatenate([br_ref[...], bi_ref[...]], axis=0)
    y = _dot3c(gh_ref[...], gl_ref[...], b)
    o_ref[...] = ((y * invn_ref[...] + u_ref[...] * bias_ref[...]) * x0_ref[...]).astype(BF16)


def _fft_last(br, bi, gmat, uview, x0view, bias_t, invn_t, ncols):
    cb = min(FFT_COLS, ncols)
    gh, gl = gmat
    cspec = pl.BlockSpec(gh.shape, lambda j: (0, 0))
    dspec = pl.BlockSpec((FFT_N1, cb), lambda j: (0, j))
    vspec = pl.BlockSpec((1, cb), lambda j: (0, 0))
    return pl.pallas_call(
        _fftlast_kernel,
        grid=(ncols // cb,),
        in_specs=[dspec, dspec, cspec, cspec, dspec, dspec, vspec, vspec],
        out_specs=dspec,
        out_shape=jax.ShapeDtypeStruct((FFT_N1, ncols), BF16),
        compiler_params=_cparams(("arbitrary",)),
    )(br, bi, gh, gl, uview, x0view, bias_t, invn_t)


def _dft_small_consts(seq):
    n = 2 * seq
    a = -2.0 * np.pi * np.outer(np.arange(n), np.arange(n)) / n
    fr, fi = np.cos(a), np.sin(a)
    f_data = np.block([[fr[:, :seq], -fi[:, :seq]], [fi[:, :seq], fr[:, :seq]]])
    f_real = np.concatenate([fr, fi], axis=0)
    gr, gi = fr / n, -fi / n
    g = np.block([[gr[:seq], -gi[:seq]], [gi[:seq], gr[:seq]]])
    return _np_split(f_data), _np_split(f_real), _np_split(g)


def _hy_ctx_kernel(u_ref, x0_ref, taps_ref, fdh_ref, fdl_ref, frh_ref, frl_ref, gh_ref, gl_ref,
                   bias_ref, invn_ref, o_ref):
    n = taps_ref.shape[0]
    z = u_ref[...]
    a = _dot3c(fdh_ref[...], fdl_ref[...], z)
    k = _dot3c(frh_ref[...], frl_ref[...], taps_ref[...])
    ar, ai, kr, ki = a[:n], a[n:], k[:n], k[n:]
    y = jnp.concatenate([ar * kr - ai * ki, ar * ki + ai * kr], axis=0)
    conv = _dot3c(gh_ref[...], gl_ref[...], y)
    o_ref[...] = ((conv * invn_ref[...] + z * bias_ref[...]) * x0_ref[...]).astype(BF16)


def _hy_ctx(u, x0, taps, bias, invn, small):
    rows, hw = u.shape
    full = lambda a: pl.BlockSpec(a.shape, lambda i: (0,) * a.ndim)
    args = [u, x0, taps, *small[0], *small[1], *small[2], bias, invn]
    return pl.pallas_call(
        _hy_ctx_kernel,
        grid=(1,),
        in_specs=[full(a) for a in args],
        out_specs=pl.BlockSpec((rows, hw), lambda i: (0, 0)),
        out_shape=jax.ShapeDtypeStruct((rows, hw), BF16),
        compiler_params=_cparams(("arbitrary",)),
    )(*args)


def _out_proj_kernel(x_ref, yp_ref, yh_ref, ya_ref, w_ref, g_ref, mod_ref, *rest, nlt, tpb, ncond, pw, hw, moe):
    if moe:
        rw_ref, xo_ref, u_ref, route_ref = rest
    else:
        xo_ref, u_ref = rest
    ci = _cond_row(pl.program_id(0), nlt, tpb, ncond)
    mix = (_dot(yp_ref[...], w_ref[0:pw, :]) + _dot(yh_ref[...], w_ref[pw:pw + hw, :])
           + _dot(ya_ref[...], w_ref[pw + hw:, :]))
    x = x_ref[...] + mod_ref[2, pl.ds(ci, 1), :] * mix
    xo_ref[...] = x
    un = _normmod(x, g_ref[...], mod_ref[3, pl.ds(ci, 1), :], mod_ref[4, pl.ds(ci, 1), :])
    u_ref[...] = un.astype(BF16)
    if moe:
        logits = _dot3(un, rw_ref[...])
        lane = lax.broadcasted_iota(jnp.int32, logits.shape, 1)
        neg = jnp.float32(-jnp.inf)
        lg = jnp.where(lane < N_EXPERTS, logits, neg)
        t1 = jnp.max(lg, axis=-1, keepdims=True)
        i1 = jnp.min(jnp.where(lg == t1, lane, 128), axis=-1, keepdims=True)
        lg2 = jnp.where(lane == i1, neg, lg)
        t2 = jnp.max(lg2, axis=-1, keepdims=True)
        i2 = jnp.min(jnp.where(lg2 == t2, lane, 128), axis=-1, keepdims=True)
        e2 = jnp.exp(t2 - t1)
        g1 = 1.0 / (1.0 + e2)
        g2 = e2 / (1.0 + e2)
        route_ref[...] = jnp.where(lane == 0, i1.astype(F32), jnp.where(lane == 1, i2.astype(F32),
                                   jnp.where(lane == 2, g1, jnp.where(lane == 3, g2, 0.0))))


def _out_proj(x, yp, yh, ya, w, g, mod, rw, *, nlt, tpb, ncond):
    n, d = x.shape
    pw, hw, aw = yp.shape[1], yh.shape[1], ya.shape[1]
    tm = ROW_TILE
    moe = rw is not None
    row = lambda width: pl.BlockSpec((tm, width), lambda t: (t, 0))
    in_specs = [row(d), row(pw), row(hw), row(aw),
                pl.BlockSpec(w.shape, lambda t: (0, 0)),
                pl.BlockSpec((1, d), lambda t: (0, 0)),
                pl.BlockSpec(mod.shape, lambda t: (0, 0, 0))]
    args = [x, yp, yh, ya, w, g, mod]
    out_specs = [row(d), row(d)]
    out_shape = [jax.ShapeDtypeStruct((n, d), F32), jax.ShapeDtypeStruct((n, d), BF16)]
    if moe:
        in_specs.append(pl.BlockSpec(rw.shape, lambda t: (0, 0)))
        args.append(rw)
        out_specs.append(row(128))
        out_shape.append(jax.ShapeDtypeStruct((n, 128), F32))
    kern = functools.partial(_out_proj_kernel, nlt=nlt, tpb=tpb, ncond=ncond, pw=pw, hw=hw, moe=moe)
    return pl.pallas_call(
        kern, grid=(n // tm,), in_specs=in_specs, out_specs=out_specs, out_shape=out_shape,
        compiler_params=_cparams(("arbitrary",)),
    )(*args)


def _ffn_kernel(x_ref, u_ref, mod_ref, wg_ref, wu_ref, wd_ref, o_ref, *, nlt, tpb, ncond, nchunk):
    ci = _cond_row(pl.program_id(0), nlt, tpb, ncond)
    u = u_ref[...]
    ff = wg_ref.shape[1]
    fc = ff // nchunk
    y = jnp.zeros(x_ref.shape, F32)
    for c in range(nchunk):
        gate = _dot(u, wg_ref[:, c * fc:(c + 1) * fc])
        up = _dot(u, wu_ref[:, c * fc:(c + 1) * fc])
        y = y + _dot((_silu(gate) * up).astype(BF16), wd_ref[c * fc:(c + 1) * fc, :])
    o_ref[...] = x_ref[...] + mod_ref[5, pl.ds(ci, 1), :] * y


def _ffn(x, u, mod, wg, wu, wd, *, nlt, tpb, ncond, out_rows):
    n, d = x.shape
    ff = wg.shape[1]
    tm = ROW_TILE
    nchunk = 2 if (ff // 2) % 128 == 0 else 1
    row = lambda dt: pl.BlockSpec((tm, d), lambda t: (t, 0))
    const = lambda shape: pl.BlockSpec(shape, lambda t: (0,) * len(shape), pipeline_mode=pl.Buffered(1))
    kern = functools.partial(_ffn_kernel, nlt=nlt, tpb=tpb, ncond=ncond, nchunk=nchunk)
    return pl.pallas_call(
        kern,
        grid=(out_rows // tm,),
        in_specs=[row(F32), row(BF16), pl.BlockSpec(mod.shape, lambda t: (0, 0, 0)),
                  const((d, ff)), const((d, ff)), const((ff, d))],
        out_specs=row(F32),
        out_shape=jax.ShapeDtypeStruct((out_rows, d), F32),
        compiler_params=_cparams(("arbitrary",)),
    )(x, u, mod, wg, wu, wd)


def _moe_kernel(te_ref, tv_ref, u_ref, wg_ref, wu_ref, wd_ref, o_ref, acc_ref):
    t = pl.program_id(0)
    f = pl.program_id(1)

    @pl.when(tv_ref[t] > 0)
    def _():
        u = u_ref[...]
        h = (_silu(_dot(u, wg_ref[0])) * _dot(u, wu_ref[0])).astype(BF16)
        y = _dot(h, wd_ref[0])

        @pl.when(f == 0)
        def _():
            acc_ref[...] = y

        @pl.when(f != 0)
        def _():
            acc_ref[...] = acc_ref[...] + y

        @pl.when(f == pl.num_programs(1) - 1)
        def _():
            o_ref[...] = acc_ref[...].astype(BF16)


def _moe_experts(ug, tile_expert, tile_valid, wg, wu, wd):
    p, d = ug.shape
    ff = wg.shape[2]
    tm, fc = MOE_TM, MOE_FC
    grid_spec = pltpu.PrefetchScalarGridSpec(
        num_scalar_prefetch=2,
        grid=(p // tm, ff // fc),
        in_specs=[
            pl.BlockSpec((tm, d), lambda t, f, te, tv: (t, 0)),
            pl.BlockSpec((1, d, fc), lambda t, f, te, tv: (te[t], 0, f)),
            pl.BlockSpec((1, d, fc), lambda t, f, te, tv: (te[t], 0, f)),
            pl.BlockSpec((1, fc, d), lambda t, f, te, tv: (te[t], f, 0)),
        ],
        out_specs=pl.BlockSpec((tm, d), lambda t, f, te, tv: (t, 0)),
        scratch_shapes=[pltpu.VMEM((tm, d), F32)],
    )
    return pl.pallas_call(
        _moe_kernel, grid_spec=grid_spec,
        out_shape=jax.ShapeDtypeStruct((p, d), BF16),
        compiler_params=_cparams(("arbitrary", "arbitrary")),
    )(tile_expert, tile_valid, ug, wg, wu, wd)


def _moe_combine_kernel(x_ref, ya_ref, yb_ref, route_ref, mod_ref, o_ref, *, nlt, tpb, ncond):
    ci = _cond_row(pl.program_id(0), nlt, tpb, ncond)
    r = route_ref[...]
    lane = lax.broadcasted_iota(jnp.int32, r.shape, 1)
    g1 = jnp.sum(jnp.where(lane == 2, r, 0.0), axis=-1, keepdims=True)
    g2 = jnp.sum(jnp.where(lane == 3, r, 0.0), axis=-1, keepdims=True)
    y = g1 * ya_ref[...].astype(F32) + g2 * yb_ref[...].astype(F32)
    o_ref[...] = x_ref[...] + mod_ref[5, pl.ds(ci, 1), :] * y


def _moe_combine(x, ya, yb, route, mod, *, nlt, tpb, ncond, out_rows):
    n, d = x.shape
    tm = ROW_TILE
    row = lambda width: pl.BlockSpec((tm, width), lambda t: (t, 0))
    kern = functools.partial(_moe_combine_kernel, nlt=nlt, tpb=tpb, ncond=ncond)
    return pl.pallas_call(
        kern,
        grid=(out_rows // tm,),
        in_specs=[row(d), row(d), row(d), row(128), pl.BlockSpec(mod.shape, lambda t: (0, 0, 0))],
        out_specs=row(d),
        out_shape=jax.ShapeDtypeStruct((out_rows, d), F32),
        compiler_params=_cparams(("arbitrary",)),
    )(x, ya, yb, route, mod)


def _moe_layer(x, u, route, mod, wg, wu, wd, *, nlt, tpb, ncond, out_rows):
    n, d = x.shape
    tm = MOE_TM
    experts = jnp.concatenate([route[:, 0], route[:, 1]]).astype(jnp.int32)
    onehot = (experts[:, None] == jnp.arange(N_EXPERTS, dtype=jnp.int32)[None, :]).astype(jnp.int32)
    rank = jnp.sum(onehot * (jnp.cumsum(onehot, axis=0) - 1), axis=1)
    counts = jnp.sum(onehot, axis=0)
    padded = ((counts + tm - 1) // tm) * tm
    ends = jnp.cumsum(padded)
    starts = ends - padded
    dest = starts[experts] + rank
    p = 2 * n + N_EXPERTS * tm
    tokens = jnp.concatenate([jnp.arange(n, dtype=jnp.int32)] * 2)
    src = jnp.zeros((p,), jnp.int32).at[dest].set(tokens)
    tile_start = jnp.arange(p // tm, dtype=jnp.int32) * tm
    tile_expert = jnp.minimum(jnp.searchsorted(ends, tile_start, side="right"), N_EXPERTS - 1).astype(jnp.int32)
    tile_valid = (tile_start < ends[-1]).astype(jnp.int32)
    ys = _moe_experts(jnp.take(u, src, axis=0), tile_expert, tile_valid, wg, wu, wd)
    ya = jnp.take(ys, dest[:n], axis=0)
    yb = jnp.take(ys, dest[n:], axis=0)
    return _moe_combine(x, ya, yb, route, mod, nlt=nlt, tpb=tpb, ncond=ncond, out_rows=out_rows)


def _rope_tables(seq, n_ctx_rows, batch):
    rows = seq // GRID_W
    row = jnp.repeat(jnp.arange(rows, dtype=F32), GRID_W)
    col = jnp.broadcast_to(jnp.arange(GRID_W, dtype=F32), (rows, GRID_W)).reshape(-1)
    inv_freq = jnp.power(ROPE_THETA, -jnp.arange(ROPE_FREQS, dtype=F32) / ROPE_FREQS)
    ar = row[:, None] * inv_freq
    ac = col[:, None] * inv_freq
    cos = jnp.concatenate([jnp.cos(ar), jnp.cos(ar), jnp.cos(ac), jnp.cos(ac)], axis=-1)
    sin = jnp.concatenate([-jnp.sin(ar), jnp.sin(ar), -jnp.sin(ac), jnp.sin(ac)], axis=-1)
    cos = jnp.tile(jnp.concatenate([cos, cos], axis=-1), (batch, 1))
    sin = jnp.tile(jnp.concatenate([sin, sin], axis=-1), (batch, 1))
    cos = jnp.concatenate([cos, jnp.ones((n_ctx_rows, 2 * HEAD_DIM), F32)], axis=0)
    sin = jnp.concatenate([sin, jnp.zeros((n_ctx_rows, 2 * HEAD_DIM), F32)], axis=0)
    return cos, sin


def kernel(x, c, ctx, c_ctx, mod_w, mod_b, norm1_g, norm2_g, w_in, w_out, pool_lin, pool_scale, hy_short_w, hy_short_b, hy_f_w1, hy_f_b1, hy_f_freq1, hy_f_w2, hy_f_b2, hy_f_freq2, hy_f_w3, hy_bias, qk_norm_g, diff_lambda, subln_g, ffn_w_gate, ffn_w_up, ffn_w_down, router_w, moe_w_gate, moe_w_up, moe_w_down):
    batch, seq, d = x.shape
    ctx_len = ctx.shape[1]
    depth = mod_w.shape[0]
    pw = pool_scale.shape[1]
    hw = hy_bias.shape[1]
    nl, nc = batch * seq, batch * ctx_len
    n = nl + nc
    tm = ROW_TILE
    assert seq % tm == 0 and nc % tm == 0 and seq % (FFT_N1 // 2) == 0 and seq % GRID_W == 0
    assert d == HEADS * 2 * V_DIM and pw == hw and batch == 2
    n2 = 2 * seq // FFT_N1
    ncols = n2 * hw
    assert (n * hw) % ncols == 0
    tiles = dict(nlt=nl // tm, tpb=seq // tm, ncond=batch + 1)

    xs = jnp.concatenate([x.reshape(nl, d), ctx.reshape(nc, d)], axis=0)
    cond8 = jnp.concatenate([c, c_ctx[None, :], jnp.zeros((8 - batch - 1, d), F32)], axis=0)
    mods = _modvec(cond8, mod_w, mod_b)
    cos_t, sin_t = _rope_tables(seq, nc, batch)
    consts = _dft_consts(n2)
    small = _dft_small_consts(ctx_len)
    eye = jnp.eye(len(POOL_WINDOWS), dtype=F32)

    for i in range(depth):
        last = i == depth - 1
        lam_init = 0.8 - 0.6 * math.exp(-0.3 * i)
        mod = mods[i]
        zp, zh, za = _in_proj(xs, norm1_g[i][None, :], mod, w_in[i].astype(BF16), pw=pw, hw=hw, **tiles)

        g2 = jnp.concatenate([qk_norm_g[i], qk_norm_g[i]], axis=-1)
        qt, k12, vt = _attn_prep(za, cos_t, sin_t, g2)
        g_b = jnp.broadcast_to(subln_g[i][:, None], (V_DIM, ATT_TQ))
        att_kw = dict(batch=batch, seq=seq, ctx_len=ctx_len, lam_init=lam_init)
        ya_lat = _attention(diff_lambda[i], qt, k12, vt, g_b, latent=True, **att_kw)
        ya_ctx = _attention(diff_lambda[i], qt, k12, vt, g_b, latent=False, **att_kw)
        ya = jnp.concatenate([ya_lat, ya_ctx], axis=0)

        gd = pw // len(POOL_WINDOWS)
        lin_bd = (eye[:, None, :, None] * pool_lin[i][:, :, None, :]).reshape(pw, pw).astype(BF16)
        pscale = pool_scale[i][None, :]
        yp = jnp.concatenate([
            _pool(zp, lin_bd, pscale, row0=0, batch=batch, seq=seq, out_rows=nl, out_row0=0),
            _pool(zp, lin_bd, pscale, row0=nl, batch=batch, seq=ctx_len, out_rows=nc, out_row0=0)], axis=0)

        sw, sb = hy_short_w[i], hy_short_b[i][None, :]
        u_l, x0_l = _hy_pre(zh, sw, sb, row0=0, batch=batch, seq=seq, out_rows=n, out_row0=0)
        u_c, x0_c = _hy_pre(zh, sw, sb, row0=nl, batch=batch, seq=ctx_len, out_rows=nc, out_row0=0)
        filt = (hy_f_w1[i], hy_f_b1[i], hy_f_freq1[i], hy_f_w2[i], hy_f_b2[i], hy_f_freq2[i], hy_f_w3[i])
        bias = hy_bias[i][None, :]
        hf, hb, asum = _hy_filter(seq, *filt, hw)
        taps = jnp.concatenate([hf, jnp.roll(jnp.flip(hb, axis=0), 1, axis=0)], axis=0)
        kr1, ki1 = _fft_first(taps.reshape(FFT_N1, ncols), consts["f1_real"], ncols)
        spectrum = _fft_mid(kr1, ki1, consts, n2, hw)
        ar, ai = _fft_first(u_l.reshape(-1, ncols), consts["f1_data"], ncols)
        br, bi = _fft_mid(ar, ai, consts, n2, hw, spectrum=spectrum)
        reps = min(FFT_COLS, ncols) // hw
        yh_lat = _fft_last(br.reshape(FFT_N1, ncols), bi.reshape(FFT_N1, ncols), consts["g1"],
                           u_l.reshape(-1, ncols), x0_l.reshape(-1, ncols),
                           jnp.tile(bias, (1, reps)), jnp.tile(1.0 / asum, (1, reps)), ncols)
        hf_c, hb_c, asum_c = _hy_filter(ctx_len, *filt, hw)
        taps_c = jnp.concatenate([hf_c, jnp.roll(jnp.flip(hb_c, axis=0), 1, axis=0)], axis=0)
        yh_ctx = _hy_ctx(u_c, x0_c, taps_c, bias, 1.0 / asum_c, small)
        yh = jnp.concatenate([yh_lat.reshape(nl, hw), yh_ctx], axis=0)

        j = i // 2
        moe = i % 2 == 1
        rw = None
        if moe:
            rw = jnp.concatenate([router_w[j], jnp.zeros((d, 128 - N_EXPERTS), F32)], axis=1)
        outs = _out_proj(xs, yp, yh, ya, w_out[i].astype(BF16), norm2_g[i][None, :], mod, rw, **tiles)
        out_rows = nl if last else n
        if moe:
            xs, u, route = outs
            xs = _moe_layer(xs, u, route, mod, moe_w_gate[j].astype(BF16), moe_w_up[j].astype(BF16),
                            moe_w_down[j].astype(BF16), out_rows=out_rows, **tiles)
        else:
            xs, u = outs
            xs = _ffn(xs, u, mod, ffn_w_gate[j].astype(BF16), ffn_w_up[j].astype(BF16),
                      ffn_w_down[j].astype(BF16), out_rows=out_rows, **tiles)
    return xs[:nl].reshape(batch, seq, d)
```

```python
import functools
import math

import numpy as np
import jax
import jax.numpy as jnp
from jax import lax
from jax.experimental import pallas as pl
from jax.experimental.pallas import tpu as pltpu

F32 = jnp.float32
BF16 = jnp.bfloat16
EPS = 1e-6

GRID_W = 64
POOL_WINDOWS = (2, 4, 8, 16)
HEADS = 4
HEAD_DIM = 64
V_DIM = 128
ROPE_FREQS = 16
ROPE_THETA = 10000.0
HY_BANDS = 16
HY_TARGET = 1e-2
HY_FAST = 0.3
HY_SLOW = 1.5
N_EXPERTS = 8

ROW_TILE = 512
HALO = 32
FFT_N1 = 128
FFT_COLS = 2048
ATT_TQ = 512
ATT_TK = 512
ATT_UNROLL = 4
QSCALE = (HEAD_DIM ** -0.5) * math.log2(math.e)
ATT_DIRECT_MAX = 100.0
MOE_TM = 512
MOE_FC = 512
VMEM_LIMIT = 56 * 1024 * 1024


def _cparams(sem):
    return pltpu.CompilerParams(dimension_semantics=sem, vmem_limit_bytes=VMEM_LIMIT)


def _dot(a, b):
    return jnp.dot(a, b, preferred_element_type=F32)


def _split(a):
    hi = a.astype(BF16)
    lo = (a - hi.astype(F32)).astype(BF16)
    return hi, lo


def _dot3(a, b):
    ah, al = _split(a)
    bh, bl = _split(b)
    return _dot(ah, bh) + _dot(ah, bl) + _dot(al, bh)


def _dot3c(ch, cl, d):
    dh, dl = _split(d)
    return _dot(ch, dh) + _dot(ch, dl) + _dot(cl, dh)


def _silu(x):
    return x / (1.0 + jnp.exp(-x))


def _normmod(x, g, shift, scale):
    ms = jnp.mean(x * x, axis=-1, keepdims=True)
    return (x * lax.rsqrt(ms + EPS) * g) * (1.0 + scale) + shift


def _cond_row(t, n_lat_tiles, tiles_per_batch, n_cond):
    return jnp.where(t >= n_lat_tiles, n_cond - 1, t // tiles_per_batch)


def _modvec_kernel(c_ref, w_ref, b_ref, o_ref):
    s = _silu(c_ref[...])
    o_ref[0, 0] = _dot3(s, w_ref[0]) + b_ref[0]


def _modvec(cond8, mod_w, mod_b):
    depth, d, six_d = mod_w.shape
    nchunk = six_d // d
    return pl.pallas_call(
        _modvec_kernel,
        grid=(depth, nchunk),
        in_specs=[
            pl.BlockSpec((8, d), lambda i, j: (0, 0)),
            pl.BlockSpec((1, d, d), lambda i, j: (i, 0, j)),
            pl.BlockSpec((1, 1, d), lambda i, j: (i, 0, j)),
        ],
        out_specs=pl.BlockSpec((1, 1, 8, d), lambda i, j: (i, j, 0, 0)),
        out_shape=jax.ShapeDtypeStruct((depth, nchunk, 8, d), F32),
        compiler_params=_cparams(("arbitrary", "arbitrary")),
    )(cond8, mod_w, mod_b.reshape(depth, 1, six_d))


def _in_proj_kernel(x_ref, g_ref, mod_ref, w_ref, zp_ref, zh_ref, za_ref, *, nlt, tpb, ncond, pw, hw):
    ci = _cond_row(pl.program_id(0), nlt, tpb, ncond)
    shift = mod_ref[0, pl.ds(ci, 1), :]
    scale = mod_ref[1, pl.ds(ci, 1), :]
    y = _normmod(x_ref[...], g_ref[...], shift, scale).astype(BF16)
    z = _dot(y, w_ref[...])
    zp_ref[...] = z[:, :pw].astype(BF16)
    zh_ref[...] = z[:, pw:pw + 3 * hw].astype(BF16)
    za_ref[...] = z[:, pw + 3 * hw:].astype(BF16)


def _in_proj(x, g, mod, w, *, nlt, tpb, ncond, pw, hw):
    n, d = x.shape
    wid = w.shape[1]
    aw = wid - pw - 3 * hw
    tm = ROW_TILE
    kern = functools.partial(_in_proj_kernel, nlt=nlt, tpb=tpb, ncond=ncond, pw=pw, hw=hw)
    return pl.pallas_call(
        kern,
        grid=(n // tm,),
        in_specs=[
            pl.BlockSpec((tm, d), lambda t: (t, 0)),
            pl.BlockSpec((1, d), lambda t: (0, 0)),
            pl.BlockSpec(mod.shape, lambda t: (0, 0, 0)),
            pl.BlockSpec((d, wid), lambda t: (0, 0)),
        ],
        out_specs=[
            pl.BlockSpec((tm, pw), lambda t: (t, 0)),
            pl.BlockSpec((tm, 3 * hw), lambda t: (t, 0)),
            pl.BlockSpec((tm, aw), lambda t: (t, 0)),
        ],
        out_shape=[
            jax.ShapeDtypeStruct((n, pw), BF16),
            jax.ShapeDtypeStruct((n, 3 * hw), BF16),
            jax.ShapeDtypeStruct((n, aw), BF16),
        ],
        compiler_params=_cparams(("arbitrary",)),
    )(x, g, mod, w)


def _prep_kernel(za_ref, cos_ref, sin_ref, g_ref, qt_ref, k_ref, vt_ref, *, qscale):
    tm = za_ref.shape[0]
    lane = lax.broadcasted_iota(jnp.int32, (tm, 2 * HEAD_DIM), 1)
    first = lane < HEAD_DIM
    apart = (lane % (2 * ROPE_FREQS)) < ROPE_FREQS
    cos = cos_ref[...]
    sin = sin_ref[...]
    qk_w = HEADS * 2 * HEAD_DIM

    def norm_rope(x, g):
        x2 = x * x
        s_all = jnp.sum(x2, axis=-1, keepdims=True)
        s_first = jnp.sum(jnp.where(first, x2, 0.0), axis=-1, keepdims=True)
        ms = jnp.where(first, s_first, s_all - s_first) * (1.0 / HEAD_DIM)
        xn = x * lax.rsqrt(ms + EPS) * g
        swapped = jnp.where(apart, pltpu.roll(xn, 2 * HEAD_DIM - ROPE_FREQS, 1), pltpu.roll(xn, ROPE_FREQS, 1))
        return xn * cos + swapped * sin

    for h in range(HEADS):
        lo = h * 2 * HEAD_DIM
        q = norm_rope(za_ref[:, lo:lo + 2 * HEAD_DIM].astype(F32), g_ref[0:1, :])
        qt_ref[h] = (q * qscale).T.astype(BF16)
        k = norm_rope(za_ref[:, qk_w + lo:qk_w + lo + 2 * HEAD_DIM].astype(F32), g_ref[1:2, :])
        k_ref[h] = k.astype(BF16)
        vlo = 2 * qk_w + h * V_DIM
        vt_ref[h] = za_ref[:, vlo:vlo + V_DIM].astype(F32).T.astype(BF16)


def _attn_prep(za, cos_t, sin_t, g2):
    n, aw = za.shape
    tm = ROW_TILE
    return pl.pallas_call(
        functools.partial(_prep_kernel, qscale=QSCALE),
        grid=(n // tm,),
        in_specs=[
            pl.BlockSpec((tm, aw), lambda t: (t, 0)),
            pl.BlockSpec((tm, 2 * HEAD_DIM), lambda t: (t, 0)),
            pl.BlockSpec((tm, 2 * HEAD_DIM), lambda t: (t, 0)),
            pl.BlockSpec((2, 2 * HEAD_DIM), lambda t: (0, 0)),
        ],
        out_specs=[
            pl.BlockSpec((HEADS, 2 * HEAD_DIM, tm), lambda t: (0, 0, t)),
            pl.BlockSpec((HEADS, tm, 2 * HEAD_DIM), lambda t: (0, t, 0)),
            pl.BlockSpec((HEADS, V_DIM, tm), lambda t: (0, 0, t)),
        ],
        out_shape=[
            jax.ShapeDtypeStruct((HEADS, 2 * HEAD_DIM, n), BF16),
            jax.ShapeDtypeStruct((HEADS, n, 2 * HEAD_DIM), BF16),
            jax.ShapeDtypeStruct((HEADS, V_DIM, n), BF16),
        ],
        compiler_params=_cparams(("arbitrary",)),
    )(za, cos_t, sin_t, g2)


def _attn_kernel(lam_ref, qt_ref, kc_ref, vtc_ref, *rest, tk, n_lat_chunks, lam_init):
    if n_lat_chunks:
        kl_ref, vtl_ref, g_ref, o_ref, m_ref, l_ref, acc_ref = rest
    else:
        g_ref, o_ref, m_ref, l_ref, acc_ref = rest
    qt = qt_ref[0]
    row = lax.broadcasted_iota(jnp.int32, qt.shape, 0)
    zero = jnp.zeros_like(qt)
    qmaps = (jnp.where(row < HEAD_DIM, qt, zero), jnp.where(row >= HEAD_DIM, qt, zero))

    def process(k_tile, vt_tile, first):
        for mi in range(2):
            s = _dot(k_tile, qmaps[mi])
            smax = jnp.max(s, axis=0, keepdims=True)
            if first:
                m_new = smax
            else:
                m_old = m_ref[mi]
                m_new = jnp.maximum(m_old, smax)
            p = jnp.exp2(s - m_new)
            psum = jnp.sum(p, axis=0, keepdims=True)
            pv = _dot(vt_tile, p.astype(BF16))
            if first:
                l_ref[mi] = psum
                acc_ref[mi] = pv
            else:
                alpha = jnp.exp2(m_old - m_new)
                l_ref[mi] = alpha * l_ref[mi] + psum
                acc_ref[mi] = alpha * acc_ref[mi] + pv
            m_ref[mi] = m_new

    process(kc_ref[0], vtc_ref[0], True)
    if n_lat_chunks:
        def body(j, carry):
            off = pl.multiple_of(j * tk, tk)
            process(kl_ref[0, pl.ds(off, tk), :], vtl_ref[0, :, pl.ds(off, tk)], False)
            return carry
        lax.fori_loop(0, n_lat_chunks, body, 0)

    lv = lam_ref[...]
    lam = (jnp.exp(jnp.sum(lv[0:1] * lv[1:2], axis=-1, keepdims=True))
           - jnp.exp(jnp.sum(lv[2:3] * lv[3:4], axis=-1, keepdims=True)) + lam_init)
    o = acc_ref[0] * (1.0 / l_ref[0]) - lam * (acc_ref[1] * (1.0 / l_ref[1]))
    ms = jnp.mean(o * o, axis=0, keepdims=True)
    y = o * lax.rsqrt(ms + EPS) * g_ref[...] * (1.0 - lam_init)
    o_ref[...] = y.T.astype(BF16)


def _attn_direct_kernel(lam_ref, qt_ref, kc_ref, vtc_ref, *rest, tk, n_lat_chunks, lam_init):
    if n_lat_chunks:
        kl_ref, vtl_ref, g_ref, o_ref, s_ref, p_ref, l_ref, acc_ref = rest
    else:
        g_ref, o_ref, s_ref, p_ref, l_ref, acc_ref = rest
    qt = qt_ref[0]
    tq = qt.shape[1]
    row = lax.broadcasted_iota(jnp.int32, qt.shape, 0)
    zero = jnp.zeros_like(qt)
    qmaps = (jnp.where(row < HEAD_DIM, qt, zero), jnp.where(row >= HEAD_DIM, qt, zero))

    def piece_rows(j):
        if isinstance(j, int):
            return pl.ds((j - 1) * tk, tk)
        return pl.ds(pl.multiple_of((j - 1) * tk, tk), tk)

    def k_piece(j):
        return kl_ref[0, piece_rows(j), :]

    def vt_piece(j):
        return vtc_ref[0] if isinstance(j, int) and j == 0 else vtl_ref[0, :, piece_rows(j)]

    def step(j, par):
        stage_a(k_piece(j + 1), 1 - par)
        stage_b(par)
        stage_c(vt_piece(j - 1), 1 - par)

    def stage_a(k_tile, slot):
        for mi in range(2):
            s_ref[slot, mi] = _dot(k_tile, qmaps[mi])

    def stage_b(slot):
        for mi in range(2):
            p = jnp.exp2(s_ref[slot, mi])
            l_ref[mi] = l_ref[mi] + jnp.sum(p.reshape(tk // 8, 8, tq), axis=0)
            p_ref[slot, mi] = p.astype(BF16)

    def stage_c(vt_tile, slot):
        for mi in range(2):
            acc_ref[mi] = acc_ref[mi] + _dot(vt_tile, p_ref[slot, mi])

    l_ref[...] = jnp.zeros_like(l_ref)
    acc_ref[...] = jnp.zeros_like(acc_ref)
    stage_a(kc_ref[0], 0)
    if n_lat_chunks == 0:
        stage_b(0)
        stage_c(vtc_ref[0], 0)
    else:
        stage_a(k_piece(1), 1)
        stage_b(0)
        for j in range(1, ATT_UNROLL):
            step(j, j % 2)

        def body(jj, carry):
            j = ATT_UNROLL * (jj + 1)
            for r in range(ATT_UNROLL):
                step(j + r, r % 2)
            return carry

        lax.fori_loop(0, n_lat_chunks // ATT_UNROLL - 1, body, 0)
        stage_b(0)
        stage_c(vt_piece(n_lat_chunks - 1), 1)
        stage_c(vt_piece(n_lat_chunks), 0)

    lv = lam_ref[...]
    lam = (jnp.exp(jnp.sum(lv[0:1] * lv[1:2], axis=-1, keepdims=True))
           - jnp.exp(jnp.sum(lv[2:3] * lv[3:4], axis=-1, keepdims=True)) + lam_init)
    l1 = jnp.sum(l_ref[0], axis=0, keepdims=True)
    l2 = jnp.sum(l_ref[1], axis=0, keepdims=True)
    o = acc_ref[0] * (1.0 / l1) - lam * (acc_ref[1] * (1.0 / l2))
    ms = jnp.mean(o * o, axis=0, keepdims=True)
    y = o * lax.rsqrt(ms + EPS) * g_ref[...] * (1.0 - lam_init)
    o_ref[...] = y.T.astype(BF16)


def _attention(lam_p, qt, k12, vt, g_b, *, batch, seq, ctx_len, latent, lam_init, direct):
    n = k12.shape[1]
    nl = batch * seq
    tq = ATT_TQ if latent else ctx_len
    nq = seq // tq if latent else 1
    tk = ctx_len if direct else ATT_TK
    assert seq % (ATT_UNROLL * tk) == 0
    qbase = 0 if latent else nl // tq

    def qrow(b, iq):
        return (b * nq + iq) if latent else (qbase + b)

    in_specs = [
        pl.BlockSpec(lam_p.shape, lambda b, h, iq: (0, 0)),
        pl.BlockSpec((1, 2 * HEAD_DIM, tq), lambda b, h, iq: (h, 0, qrow(b, iq))),
        pl.BlockSpec((1, ctx_len, 2 * HEAD_DIM), lambda b, h, iq: (h, nl // ctx_len + b, 0)),
        pl.BlockSpec((1, V_DIM, ctx_len), lambda b, h, iq: (h, 0, nl // ctx_len + b)),
    ]
    args = [lam_p, qt, k12, vt]
    if latent:
        in_specs += [
            pl.BlockSpec((1, seq, 2 * HEAD_DIM), lambda b, h, iq: (h, b, 0)),
            pl.BlockSpec((1, V_DIM, seq), lambda b, h, iq: (h, 0, b)),
        ]
        args += [k12, vt]
    in_specs.append(pl.BlockSpec((V_DIM, tq), lambda b, h, iq: (0, 0)))
    args.append(g_b[:, :tq])
    nrows = nl if latent else batch * ctx_len
    body = _attn_direct_kernel if direct else _attn_kernel
    kern = functools.partial(body, tk=tk, n_lat_chunks=(seq // tk if latent else 0), lam_init=lam_init)
    if direct:
        scratch = [pltpu.VMEM((2, 2, tk, tq), F32), pltpu.VMEM((2, 2, tk, tq), BF16),
                   pltpu.VMEM((2, 8, tq), F32), pltpu.VMEM((2, V_DIM, tq), F32)]
    else:
        scratch = [pltpu.VMEM((2, 1, tq), F32), pltpu.VMEM((2, 1, tq), F32), pltpu.VMEM((2, V_DIM, tq), F32)]
    return pl.pallas_call(
        kern,
        grid=(batch, HEADS, nq),
        in_specs=in_specs,
        out_specs=pl.BlockSpec((tq, V_DIM), lambda b, h, iq: (b * nq + iq, h)),
        out_shape=jax.ShapeDtypeStruct((nrows, HEADS * V_DIM), BF16),
        scratch_shapes=scratch,
        compiler_params=_cparams(("arbitrary", "arbitrary", "arbitrary")),
    )(*args)


def _halo_specs(width, *, row0, seq, tile):
    hb = tile // HALO

    def cur(b, i):
        return ((row0 + b * seq) // tile + i, 0)

    def prev(b, i):
        first = (row0 + b * seq) // HALO
        return (jnp.maximum(first + i * hb - 1, first), 0)

    def nxt(b, i):
        first = (row0 + b * seq) // HALO
        return (jnp.minimum(first + (i + 1) * hb, first + seq // HALO - 1), 0)

    return [pl.BlockSpec((HALO, width), prev), pl.BlockSpec((tile, width), cur), pl.BlockSpec((HALO, width), nxt)]


def _fill_ext(ext_ref, prev_ref, cur_ref, next_ref, *, seq, tile):
    pos0 = pl.program_id(1) * tile
    width = cur_ref.shape[1]
    hpos = lax.broadcasted_iota(jnp.int32, (HALO, width), 0)
    ext_ref[0:HALO, :] = jnp.where(pos0 - HALO + hpos >= 0, prev_ref[...].astype(F32), 0.0)
    ext_ref[HALO:HALO + tile, :] = cur_ref[...].astype(F32)
    ext_ref[HALO + tile:, :] = jnp.where(pos0 + tile + hpos < seq, next_ref[...].astype(F32), 0.0)


def _pool_kernel(prev_ref, cur_ref, next_ref, lin_ref, scale_ref, o_ref, ext_ref, s_ref, *, seq, tile):
    _fill_ext(ext_ref, prev_ref, cur_ref, next_ref, seq=seq, tile=tile)
    width = cur_ref.shape[1]
    gd = width // len(POOL_WINDOWS)
    n0 = tile + 2 * HALO
    s_ref[0, 0:n0 - 8, :] = ext_ref[0:n0 - 8, :] + ext_ref[1:n0 - 7, :]
    for k in range(1, len(POOL_WINDOWS)):
        step = 1 << k
        ln = n0 - 8 * (k + 1)
        s_ref[k, 0:ln, :] = s_ref[k - 1, 0:ln, :] + s_ref[k - 1, step:step + ln, :]
    lane = lax.broadcasted_iota(jnp.int32, (tile, width), 1)
    pos = pl.program_id(1) * tile + lax.broadcasted_iota(jnp.int32, (tile, width), 0)
    grp = lane // gd
    wsum = jnp.zeros((tile, width), F32)
    half = jnp.zeros((tile, width), jnp.int32)
    for k, win in enumerate(POOL_WINDOWS):
        start = HALO - win // 2
        wsum = jnp.where(grp == k, s_ref[k, start:start + tile, :], wsum)
        half = jnp.where(grp == k, win // 2, half)
    cnt = jnp.minimum(pos + half, seq) - jnp.maximum(pos - half, 0)
    z = ext_ref[HALO:HALO + tile, :]
    dlt = (wsum / cnt.astype(F32) - z).astype(BF16)
    o_ref[...] = (_dot(dlt, lin_ref[...]) * scale_ref[...]).astype(BF16)


def _pool(zp, lin_bd, scale, *, row0, batch, seq, out_rows, out_row0):
    width = zp.shape[1]
    tile = min(ROW_TILE, seq)
    kern = functools.partial(_pool_kernel, seq=seq, tile=tile)
    return pl.pallas_call(
        kern,
        grid=(batch, seq // tile),
        in_specs=_halo_specs(width, row0=row0, seq=seq, tile=tile) + [
            pl.BlockSpec((width, width), lambda b, i: (0, 0)),
            pl.BlockSpec((1, width), lambda b, i: (0, 0)),
        ],
        out_specs=pl.BlockSpec((tile, width), lambda b, i: ((out_row0 + b * seq) // tile + i, 0)),
        out_shape=jax.ShapeDtypeStruct((out_rows, width), BF16),
        scratch_shapes=[
            pltpu.VMEM((tile + 2 * HALO, width), F32),
            pltpu.VMEM((4, tile + 2 * HALO, width), F32),
        ],
        compiler_params=_cparams(("arbitrary", "arbitrary")),
    )(zp, zp, zp, lin_bd, scale)


def _hy_pre_kernel(prev_ref, cur_ref, next_ref, w_ref, b_ref, u_ref, x0_ref, ext_ref, *, seq, tile, hw):
    _fill_ext(ext_ref, prev_ref, cur_ref, next_ref, seq=seq, tile=tile)
    y = b_ref[...] + ext_ref[HALO - 1:HALO - 1 + tile, :] * w_ref[0:1, :]
    y = y + ext_ref[HALO:HALO + tile, :] * w_ref[1:2, :]
    y = y + ext_ref[HALO + 1:HALO + 1 + tile, :] * w_ref[2:3, :]
    x0_ref[...] = y[:, :hw]
    u_ref[...] = y[:, 2 * hw:] * y[:, hw:2 * hw]


def _hy_pre(zh, sw, sb, *, row0, batch, seq, out_rows, out_row0):
    width = zh.shape[1]
    hw = width // 3
    tile = min(ROW_TILE, seq)
    kern = functools.partial(_hy_pre_kernel, seq=seq, tile=tile, hw=hw)
    ospec = pl.BlockSpec((tile, hw), lambda b, i: ((out_row0 + b * seq) // tile + i, 0))
    return pl.pallas_call(
        kern,
        grid=(batch, seq // tile),
        in_specs=_halo_specs(width, row0=row0, seq=seq, tile=tile) + [
            pl.BlockSpec((3, width), lambda b, i: (0, 0)),
            pl.BlockSpec((1, width), lambda b, i: (0, 0)),
        ],
        out_specs=[ospec, ospec],
        out_shape=[jax.ShapeDtypeStruct((out_rows, hw), F32), jax.ShapeDtypeStruct((out_rows, hw), F32)],
        scratch_shapes=[pltpu.VMEM((tile + 2 * HALO, width), F32)],
        compiler_params=_cparams(("arbitrary", "arbitrary")),
    )(zh, zh, zh, sw, sb)


def _filter_kernel(feat_ref, w1_ref, b1_ref, f1_ref, w2_ref, b2_ref, f2_ref, w3_ref, dl_ref,
                   hf_ref, hb_ref, asum_ref, *, tile, hw):
    feat = feat_ref[...]
    h = jnp.sin(f1_ref[...] * (_dot3(feat, w1_ref[...]) + b1_ref[...]))
    h = jnp.sin(f2_ref[...] * (_dot3(h, w2_ref[...]) + b2_ref[...]))
    h = _dot3(h, w3_ref[...])
    dec = jnp.exp(-feat[:, 0:1] * dl_ref[...])
    pos = pl.program_id(0) * tile + lax.broadcasted_iota(jnp.int32, (tile, hw), 0)
    hf = h[:, :hw] * dec
    hb = jnp.where(pos == 0, 0.0, h[:, hw:] * dec)
    hf_ref[...] = hf
    hb_ref[...] = hb
    part = jnp.sum(jnp.abs(hf) + jnp.abs(hb), axis=0, keepdims=True)

    @pl.when(pl.program_id(0) == 0)
    def _():
        asum_ref[...] = part

    @pl.when(pl.program_id(0) != 0)
    def _():
        asum_ref[...] = asum_ref[...] + part


def _hy_filter(seq, w1, b1, f1, w2, b2, f2, w3, hw):
    t = jnp.linspace(0.0, 1.0, seq, dtype=F32)[:, None]
    w = (2.0 * math.pi / seq) * jnp.arange(seq, dtype=F32)[:, None]
    bands = jnp.linspace(1e-4, HY_BANDS - 1, HY_BANDS, dtype=F32)[None, :]
    emb = 1 + 2 * HY_BANDS
    embp = ((emb + 7) // 8) * 8
    feat = jnp.concatenate([t, jnp.cos(bands * w), -jnp.sin(bands * w), jnp.zeros((seq, embp - emb), F32)], axis=-1)
    w1p = jnp.concatenate([w1, jnp.zeros((embp - emb, w1.shape[1]), F32)], axis=0)
    max_decay = math.log(1.0 / HY_TARGET) / HY_FAST
    min_decay = math.log(1.0 / HY_TARGET) / HY_SLOW
    deltas = jnp.linspace(min_decay, max_decay, hw, dtype=F32)[None, :]
    tile = min(ROW_TILE, seq)
    hid = w1.shape[1]
    full = lambda shape: pl.BlockSpec(shape, lambda i: (0,) * len(shape))
    return pl.pallas_call(
        functools.partial(_filter_kernel, tile=tile, hw=hw),
        grid=(seq // tile,),
        in_specs=[
            pl.BlockSpec((tile, embp), lambda i: (i, 0)),
            full((embp, hid)), full((1, hid)), full((1, hid)),
            full((hid, hid)), full((1, hid)), full((1, hid)),
            full((hid, 2 * hw)), full((1, hw)),
        ],
        out_specs=[
            pl.BlockSpec((tile, hw), lambda i: (i, 0)),
            pl.BlockSpec((tile, hw), lambda i: (i, 0)),
            pl.BlockSpec((1, hw), lambda i: (0, 0)),
        ],
        out_shape=[
            jax.ShapeDtypeStruct((seq, hw), F32),
            jax.ShapeDtypeStruct((seq, hw), F32),
            jax.ShapeDtypeStruct((1, hw), F32),
        ],
        compiler_params=_cparams(("arbitrary",)),
    )(feat, w1p, b1[None, :], f1[None, :], w2, b2[None, :], f2[None, :], w3, deltas)


def _np_split(a):
    a32 = jnp.asarray(a, F32)
    hi = a32.astype(BF16)
    lo = (a32 - hi.astype(F32)).astype(BF16)
    return hi, lo


def _dft_consts(n2):
    n1 = FFT_N1
    n = n1 * n2
    half = n1 // 2
    a1 = -2.0 * np.pi * np.outer(np.arange(n1), np.arange(n1)) / n1
    f1r, f1i = np.cos(a1), np.sin(a1)
    f1_data = np.block([[f1r[:, :half], -f1i[:, :half]], [f1i[:, :half], f1r[:, :half]]])
    f1_real = np.concatenate([f1r, f1i], axis=0)
    g1r, g1i = f1r / n, -f1i / n
    g1 = np.block([[g1r[:half], -g1i[:half]], [g1i[:half], g1r[:half]]])
    a2 = -2.0 * np.pi * np.outer(np.arange(n2), np.arange(n2)) / n2
    f2r, f2i = np.cos(a2), np.sin(a2)
    f2 = np.block([[f2r, -f2i], [f2i, f2r]])
    g2 = np.block([[f2r, f2i], [-f2i, f2r]])
    at = -2.0 * np.pi * np.outer(np.arange(n1), np.arange(n2)) / n
    twr = jnp.broadcast_to(jnp.asarray(np.cos(at), F32)[:, :, None], (n1, n2, 128))
    twi = jnp.broadcast_to(jnp.asarray(np.sin(at), F32)[:, :, None], (n1, n2, 128))
    return dict(f1_data=_np_split(f1_data), f1_real=_np_split(f1_real), g1=_np_split(g1),
                f2=_np_split(f2), g2=_np_split(g2), twr=twr, twi=twi)


def _fft1_kernel(z_ref, fh_ref, fl_ref, ar_ref, ai_ref):
    a = _dot3c(fh_ref[...], fl_ref[...], z_ref[...])
    ar_ref[...] = a[:FFT_N1]
    ai_ref[...] = a[FFT_N1:]


def _fft_first(zview, fmat, ncols):
    cb = min(FFT_COLS, ncols)
    fh, fl = fmat
    cspec = pl.BlockSpec(fh.shape, lambda j: (0, 0))
    ospec = pl.BlockSpec((FFT_N1, cb), lambda j: (0, j))
    return pl.pallas_call(
        _fft1_kernel,
        grid=(ncols // cb,),
        in_specs=[pl.BlockSpec((FFT_N1, cb), lambda j: (0, j)), cspec, cspec],
        out_specs=[ospec, ospec],
        out_shape=[jax.ShapeDtypeStruct((FFT_N1, ncols), F32)] * 2,
        compiler_params=_cparams(("arbitrary",)),
    )(zview, fh, fl)


def _fftmid_kernel(ar_ref, ai_ref, twr_ref, twi_ref, f2h_ref, f2l_ref, *rest, filter_only, n2):
    width = ar_ref.shape[2]
    reps = width // 128
    twr = jnp.concatenate([twr_ref[0]] * reps, axis=1)
    twi = jnp.concatenate([twi_ref[0]] * reps, axis=1)
    ar, ai = ar_ref[0], ai_ref[0]
    z = jnp.concatenate([ar * twr - ai * twi, ar * twi + ai * twr], axis=0)
    x = _dot3c(f2h_ref[...], f2l_ref[...], z)
    xr, xi = x[:n2], x[n2:]
    if filter_only:
        kr_out, ki_out = rest
        kr_out[0] = xr
        ki_out[0] = xi
        return
    kr_ref, ki_ref, g2h_ref, g2l_ref, br_out, bi_out = rest
    kr, ki = kr_ref[0], ki_ref[0]
    y = jnp.concatenate([xr * kr - xi * ki, xr * ki + xi * kr], axis=0)
    w = _dot3c(g2h_ref[...], g2l_ref[...], y)
    wr, wi = w[:n2], w[n2:]
    br_out[0] = wr * twr + wi * twi
    bi_out[0] = wi * twr - wr * twi


def _fft_mid(ar, ai, consts, n2, width, spectrum=None):
    a3r = ar.reshape(FFT_N1, n2, width)
    a3i = ai.reshape(FFT_N1, n2, width)
    blk = pl.BlockSpec((1, n2, width), lambda k: (k, 0, 0))
    twspec = pl.BlockSpec((1, n2, 128), lambda k: (k, 0, 0))
    cspec = pl.BlockSpec((2 * n2, 2 * n2), lambda k: (0, 0))
    in_specs = [blk, blk, twspec, twspec, cspec, cspec]
    args = [a3r, a3i, consts["twr"], consts["twi"], *consts["f2"]]
    if spectrum is not None:
        in_specs += [blk, blk, cspec, cspec]
        args += [spectrum[0], spectrum[1], *consts["g2"]]
    return pl.pallas_call(
        functools.partial(_fftmid_kernel, filter_only=spectrum is None, n2=n2),
        grid=(FFT_N1,),
        in_specs=in_specs,
        out_specs=[blk, blk],
        out_shape=[jax.ShapeDtypeStruct((FFT_N1, n2, width), F32)] * 2,
        compiler_params=_cparams(("arbitrary",)),
    )(*args)


def _fftlast_kernel(br_ref, bi_ref, gh_ref, gl_ref, u_ref, x0_ref, bias_ref, invn_ref, o_ref):
    b = jnp.concatenate([br_ref[...], bi_ref[...]], axis=0)
    y = _dot3c(gh_ref[...], gl_ref[...], b)
    o_ref[...] = ((y * invn_ref[...] + u_ref[...] * bias_ref[...]) * x0_ref[...]).astype(BF16)


def _fft_last(br, bi, gmat, uview, x0view, bias_t, invn_t, ncols):
    cb = min(FFT_COLS, ncols)
    gh, gl = gmat
    cspec = pl.BlockSpec(gh.shape, lambda j: (0, 0))
    dspec = pl.BlockSpec((FFT_N1, cb), lambda j: (0, j))
    vspec = pl.BlockSpec((1, cb), lambda j: (0, 0))
    return pl.pallas_call(
        _fftlast_kernel,
        grid=(ncols // cb,),
        in_specs=[dspec, dspec, cspec, cspec, dspec, dspec, vspec, vspec],
        out_specs=dspec,
        out_shape=jax.ShapeDtypeStruct((FFT_N1, ncols), BF16),
        compiler_params=_cparams(("arbitrary",)),
    )(br, bi, gh, gl, uview, x0view, bias_t, invn_t)


def _dft_small_consts(seq):
    n = 2 * seq
    a = -2.0 * np.pi * np.outer(np.arange(n), np.arange(n)) / n
    fr, fi = np.cos(a), np.sin(a)
    f_data = np.block([[fr[:, :seq], -fi[:, :seq]], [fi[:, :seq], fr[:, :seq]]])
    f_real = np.concatenate([fr, fi], axis=0)
    gr, gi = fr / n, -fi / n
    g = np.block([[gr[:seq], -gi[:seq]], [gi[:seq], gr[:seq]]])
    return _np_split(f_data), _np_split(f_real), _np_split(g)


def _hy_ctx_kernel(u_ref, x0_ref, taps_ref, fdh_ref, fdl_ref, frh_ref, frl_ref, gh_ref, gl_ref,
                   bias_ref, invn_ref, o_ref):
    n = taps_ref.shape[0]
    z = u_ref[...]
    a = _dot3c(fdh_ref[...], fdl_ref[...], z)
    k = _dot3c(frh_ref[...], frl_ref[...], taps_ref[...])
    ar, ai, kr, ki = a[:n], a[n:], k[:n], k[n:]
    y = jnp.concatenate([ar * kr - ai * ki, ar * ki + ai * kr], axis=0)
    conv = _dot3c(gh_ref[...], gl_ref[...], y)
    o_ref[...] = ((conv * invn_ref[...] + z * bias_ref[...]) * x0_ref[...]).astype(BF16)


def _hy_ctx(u, x0, taps, bias, invn, small):
    rows, hw = u.shape
    full = lambda a: pl.BlockSpec(a.shape, lambda i: (0,) * a.ndim)
    args = [u, x0, taps, *small[0], *small[1], *small[2], bias, invn]
    return pl.pallas_call(
        _hy_ctx_kernel,
        grid=(1,),
        in_specs=[full(a) for a in args],
        out_specs=pl.BlockSpec((rows, hw), lambda i: (0, 0)),
        out_shape=jax.ShapeDtypeStruct((rows, hw), BF16),
        compiler_params=_cparams(("arbitrary",)),
    )(*args)


def _out_proj_kernel(x_ref, yp_ref, yh_ref, ya_ref, w_ref, g_ref, mod_ref, *rest, nlt, tpb, ncond, pw, hw, moe):
    if moe:
        rw_ref, xo_ref, u_ref, route_ref = rest
    else:
        xo_ref, u_ref = rest
    ci = _cond_row(pl.program_id(0), nlt, tpb, ncond)
    mix = (_dot(yp_ref[...], w_ref[0:pw, :]) + _dot(yh_ref[...], w_ref[pw:pw + hw, :])
           + _dot(ya_ref[...], w_ref[pw + hw:, :]))
    x = x_ref[...] + mod_ref[2, pl.ds(ci, 1), :] * mix
    xo_ref[...] = x
    un = _normmod(x, g_ref[...], mod_ref[3, pl.ds(ci, 1), :], mod_ref[4, pl.ds(ci, 1), :])
    u_ref[...] = un.astype(BF16)
    if moe:
        logits = _dot3(un, rw_ref[...])
        lane = lax.broadcasted_iota(jnp.int32, logits.shape, 1)
        neg = jnp.float32(-jnp.inf)
        lg = jnp.where(lane < N_EXPERTS, logits, neg)
        t1 = jnp.max(lg, axis=-1, keepdims=True)
        i1 = jnp.min(jnp.where(lg == t1, lane, 128), axis=-1, keepdims=True)
        lg2 = jnp.where(lane == i1, neg, lg)
        t2 = jnp.max(lg2, axis=-1, keepdims=True)
        i2 = jnp.min(jnp.where(lg2 == t2, lane, 128), axis=-1, keepdims=True)
        e2 = jnp.exp(t2 - t1)
        g1 = 1.0 / (1.0 + e2)
        g2 = e2 / (1.0 + e2)
        route_ref[...] = jnp.where(lane == 0, i1.astype(F32), jnp.where(lane == 1, i2.astype(F32),
                                   jnp.where(lane == 2, g1, jnp.where(lane == 3, g2, 0.0))))


def _out_proj(x, yp, yh, ya, w, g, mod, rw, *, nlt, tpb, ncond):
    n, d = x.shape
    pw, hw, aw = yp.shape[1], yh.shape[1], ya.shape[1]
    tm = ROW_TILE
    moe = rw is not None
    row = lambda width: pl.BlockSpec((tm, width), lambda t: (t, 0))
    in_specs = [row(d), row(pw), row(hw), row(aw),
                pl.BlockSpec(w.shape, lambda t: (0, 0)),
                pl.BlockSpec((1, d), lambda t: (0, 0)),
                pl.BlockSpec(mod.shape, lambda t: (0, 0, 0))]
    args = [x, yp, yh, ya, w, g, mod]
    out_specs = [row(d), row(d)]
    out_shape = [jax.ShapeDtypeStruct((n, d), F32), jax.ShapeDtypeStruct((n, d), BF16)]
    if moe:
        in_specs.append(pl.BlockSpec(rw.shape, lambda t: (0, 0)))
        args.append(rw)
        out_specs.append(row(128))
        out_shape.append(jax.ShapeDtypeStruct((n, 128), F32))
    kern = functools.partial(_out_proj_kernel, nlt=nlt, tpb=tpb, ncond=ncond, pw=pw, hw=hw, moe=moe)
    return pl.pallas_call(
        kern, grid=(n // tm,), in_specs=in_specs, out_specs=out_specs, out_shape=out_shape,
        compiler_params=_cparams(("arbitrary",)),
    )(*args)


def _ffn_kernel(x_ref, u_ref, mod_ref, wg_ref, wu_ref, wd_ref, o_ref, *, nlt, tpb, ncond, nchunk):
    ci = _cond_row(pl.program_id(0), nlt, tpb, ncond)
    u = u_ref[...]
    ff = wg_ref.shape[1]
    fc = ff // nchunk
    y = jnp.zeros(x_ref.shape, F32)
    for c in range(nchunk):
        gate = _dot(u, wg_ref[:, c * fc:(c + 1) * fc])
        up = _dot(u, wu_ref[:, c * fc:(c + 1) * fc])
        y = y + _dot((_silu(gate) * up).astype(BF16), wd_ref[c * fc:(c + 1) * fc, :])
    o_ref[...] = x_ref[...] + mod_ref[5, pl.ds(ci, 1), :] * y


def _ffn(x, u, mod, wg, wu, wd, *, nlt, tpb, ncond, out_rows):
    n, d = x.shape
    ff = wg.shape[1]
    tm = ROW_TILE
    nchunk = 2 if (ff // 2) % 128 == 0 else 1
    row = lambda dt: pl.BlockSpec((tm, d), lambda t: (t, 0))
    const = lambda shape: pl.BlockSpec(shape, lambda t: (0,) * len(shape), pipeline_mode=pl.Buffered(1))
    kern = functools.partial(_ffn_kernel, nlt=nlt, tpb=tpb, ncond=ncond, nchunk=nchunk)
    return pl.pallas_call(
        kern,
        grid=(out_rows // tm,),
        in_specs=[row(F32), row(BF16), pl.BlockSpec(mod.shape, lambda t: (0, 0, 0)),
                  const((d, ff)), const((d, ff)), const((ff, d))],
        out_specs=row(F32),
        out_shape=jax.ShapeDtypeStruct((out_rows, d), F32),
        compiler_params=_cparams(("arbitrary",)),
    )(x, u, mod, wg, wu, wd)


def _moe_kernel(te_ref, tv_ref, u_ref, wg_ref, wu_ref, wd_ref, o_ref, acc_ref):
    t = pl.program_id(0)
    f = pl.program_id(1)

    @pl.when(tv_ref[t] > 0)
    def _():
        u = u_ref[...]
        h = (_silu(_dot(u, wg_ref[0])) * _dot(u, wu_ref[0])).astype(BF16)
        y = _dot(h, wd_ref[0])

        @pl.when(f == 0)
        def _():
            acc_ref[...] = y

        @pl.when(f != 0)
        def _():
            acc_ref[...] = acc_ref[...] + y

        @pl.when(f == pl.num_programs(1) - 1)
        def _():
            o_ref[...] = acc_ref[...].astype(BF16)


def _moe_experts(ug, tile_expert, tile_valid, wg, wu, wd):
    p, d = ug.shape
    ff = wg.shape[2]
    tm, fc = MOE_TM, MOE_FC
    grid_spec = pltpu.PrefetchScalarGridSpec(
        num_scalar_prefetch=2,
        grid=(p // tm, ff // fc),
        in_specs=[
            pl.BlockSpec((tm, d), lambda t, f, te, tv: (t, 0)),
            pl.BlockSpec((1, d, fc), lambda t, f, te, tv: (te[t], 0, f)),
            pl.BlockSpec((1, d, fc), lambda t, f, te, tv: (te[t], 0, f)),
            pl.BlockSpec((1, fc, d), lambda t, f, te, tv: (te[t], f, 0)),
        ],
        out_specs=pl.BlockSpec((tm, d), lambda t, f, te, tv: (t, 0)),
        scratch_shapes=[pltpu.VMEM((tm, d), F32)],
    )
    return pl.pallas_call(
        _moe_kernel, grid_spec=grid_spec,
        out_shape=jax.ShapeDtypeStruct((p, d), BF16),
        compiler_params=_cparams(("arbitrary", "arbitrary")),
    )(tile_expert, tile_valid, ug, wg, wu, wd)


def _moe_combine_kernel(x_ref, ya_ref, yb_ref, route_ref, mod_ref, o_ref, *, nlt, tpb, ncond):
    ci = _cond_row(pl.program_id(0), nlt, tpb, ncond)
    r = route_ref[...]
    lane = lax.broadcasted_iota(jnp.int32, r.shape, 1)
    g1 = jnp.sum(jnp.where(lane == 2, r, 0.0), axis=-1, keepdims=True)
    g2 = jnp.sum(jnp.where(lane == 3, r, 0.0), axis=-1, keepdims=True)
    y = g1 * ya_ref[...].astype(F32) + g2 * yb_ref[...].astype(F32)
    o_ref[...] = x_ref[...] + mod_ref[5, pl.ds(ci, 1), :] * y


def _moe_combine(x, ya, yb, route, mod, *, nlt, tpb, ncond, out_rows):
    n, d = x.shape
    tm = ROW_TILE
    row = lambda width: pl.BlockSpec((tm, width), lambda t: (t, 0))
    kern = functools.partial(_moe_combine_kernel, nlt=nlt, tpb=tpb, ncond=ncond)
    return pl.pallas_call(
        kern,
        grid=(out_rows // tm,),
        in_specs=[row(d), row(d), row(d), row(128), pl.BlockSpec(mod.shape, lambda t: (0, 0, 0))],
        out_specs=row(d),
        out_shape=jax.ShapeDtypeStruct((out_rows, d), F32),
        compiler_params=_cparams(("arbitrary",)),
    )(x, ya, yb, route, mod)


def _moe_layer(x, u, route, mod, wg, wu, wd, *, nlt, tpb, ncond, out_rows):
    n, d = x.shape
    tm = MOE_TM
    experts = jnp.concatenate([route[:, 0], route[:, 1]]).astype(jnp.int32)
    onehot = (experts[:, None] == jnp.arange(N_EXPERTS, dtype=jnp.int32)[None, :]).astype(jnp.int32)
    rank = jnp.sum(onehot * (jnp.cumsum(onehot, axis=0) - 1), axis=1)
    counts = jnp.sum(onehot, axis=0)
    padded = ((counts + tm - 1) // tm) * tm
    ends = jnp.cumsum(padded)
    starts = ends - padded
    dest = starts[experts] + rank
    p = 2 * n + N_EXPERTS * tm
    tokens = jnp.concatenate([jnp.arange(n, dtype=jnp.int32)] * 2)
    src = jnp.zeros((p,), jnp.int32).at[dest].set(tokens)
    tile_start = jnp.arange(p // tm, dtype=jnp.int32) * tm
    tile_expert = jnp.minimum(jnp.searchsorted(ends, tile_start, side="right"), N_EXPERTS - 1).astype(jnp.int32)
    tile_valid = (tile_start < ends[-1]).astype(jnp.int32)
    ys = _moe_experts(jnp.take(u, src, axis=0), tile_expert, tile_valid, wg, wu, wd)
    ya = jnp.take(ys, dest[:n], axis=0)
    yb = jnp.take(ys, dest[n:], axis=0)
    return _moe_combine(x, ya, yb, route, mod, nlt=nlt, tpb=tpb, ncond=ncond, out_rows=out_rows)


def _rope_tables(seq, n_ctx_rows, batch):
    rows = seq // GRID_W
    row = jnp.repeat(jnp.arange(rows, dtype=F32), GRID_W)
    col = jnp.broadcast_to(jnp.arange(GRID_W, dtype=F32), (rows, GRID_W)).reshape(-1)
    inv_freq = jnp.power(ROPE_THETA, -jnp.arange(ROPE_FREQS, dtype=F32) / ROPE_FREQS)
    ar = row[:, None] * inv_freq
    ac = col[:, None] * inv_freq
    cos = jnp.concatenate([jnp.cos(ar), jnp.cos(ar), jnp.cos(ac), jnp.cos(ac)], axis=-1)
    sin = jnp.concatenate([-jnp.sin(ar), jnp.sin(ar), -jnp.sin(ac), jnp.sin(ac)], axis=-1)
    cos = jnp.tile(jnp.concatenate([cos, cos], axis=-1), (batch, 1))
    sin = jnp.tile(jnp.concatenate([sin, sin], axis=-1), (batch, 1))
    cos = jnp.concatenate([cos, jnp.ones((n_ctx_rows, 2 * HEAD_DIM), F32)], axis=0)
    sin = jnp.concatenate([sin, jnp.zeros((n_ctx_rows, 2 * HEAD_DIM), F32)], axis=0)
    return cos, sin


def kernel(x, c, ctx, c_ctx, mod_w, mod_b, norm1_g, norm2_g, w_in, w_out, pool_lin, pool_scale, hy_short_w, hy_short_b, hy_f_w1, hy_f_b1, hy_f_freq1, hy_f_w2, hy_f_b2, hy_f_freq2, hy_f_w3, hy_bias, qk_norm_g, diff_lambda, subln_g, ffn_w_gate, ffn_w_up, ffn_w_down, router_w, moe_w_gate, moe_w_up, moe_w_down):
    batch, seq, d = x.shape
    ctx_len = ctx.shape[1]
    depth = mod_w.shape[0]
    pw = pool_scale.shape[1]
    hw = hy_bias.shape[1]
    nl, nc = batch * seq, batch * ctx_len
    n = nl + nc
    tm = ROW_TILE
    assert seq % tm == 0 and nc % tm == 0 and seq % (FFT_N1 // 2) == 0 and seq % GRID_W == 0
    assert d == HEADS * 2 * V_DIM and pw == hw and batch == 2
    n2 = 2 * seq // FFT_N1
    ncols = n2 * hw
    assert (n * hw) % ncols == 0
    tiles = dict(nlt=nl // tm, tpb=seq // tm, ncond=batch + 1)

    xs = jnp.concatenate([x.reshape(nl, d), ctx.reshape(nc, d)], axis=0)
    cond8 = jnp.concatenate([c, c_ctx[None, :], jnp.zeros((8 - batch - 1, d), F32)], axis=0)
    mods = _modvec(cond8, mod_w, mod_b)
    cos_t, sin_t = _rope_tables(seq, nc, batch)
    consts = _dft_consts(n2)
    small = _dft_small_consts(ctx_len)
    eye = jnp.eye(len(POOL_WINDOWS), dtype=F32)

    for i in range(depth):
        last = i == depth - 1
        lam_init = 0.8 - 0.6 * math.exp(-0.3 * i)
        mod = mods[i]
        zp, zh, za = _in_proj(xs, norm1_g[i][None, :], mod, w_in[i].astype(BF16), pw=pw, hw=hw, **tiles)

        g2 = jnp.concatenate([qk_norm_g[i], qk_norm_g[i]], axis=-1)
        qt, k12, vt = _attn_prep(za, cos_t, sin_t, g2)
        g_b = jnp.broadcast_to(subln_g[i][:, None], (V_DIM, ATT_TQ))
        att_kw = dict(batch=batch, seq=seq, ctx_len=ctx_len, lam_init=lam_init)

        def attend(direct, lam_p=diff_lambda[i], qt=qt, k12=k12, vt=vt, g_b=g_b, att_kw=att_kw):
            return jnp.concatenate([
                _attention(lam_p, qt, k12, vt, g_b, latent=True, direct=direct, **att_kw),
                _attention(lam_p, qt, k12, vt, g_b, latent=False, direct=direct, **att_kw)], axis=0)

        bound = (HEAD_DIM * QSCALE) * jnp.max(jnp.abs(qk_norm_g[i][0])) * jnp.max(jnp.abs(qk_norm_g[i][1]))
        ya = lax.cond(bound * 1.02 < ATT_DIRECT_MAX, lambda: attend(True), lambda: attend(False))

        gd = pw // len(POOL_WINDOWS)
        lin_bd = (eye[:, None, :, None] * pool_lin[i][:, :, None, :]).reshape(pw, pw).astype(BF16)
        pscale = pool_scale[i][None, :]
        yp = jnp.concatenate([
            _pool(zp, lin_bd, pscale, row0=0, batch=batch, seq=seq, out_rows=nl, out_row0=0),
            _pool(zp, lin_bd, pscale, row0=nl, batch=batch, seq=ctx_len, out_rows=nc, out_row0=0)], axis=0)

        sw, sb = hy_short_w[i], hy_short_b[i][None, :]
        u_l, x0_l = _hy_pre(zh, sw, sb, row0=0, batch=batch, seq=seq, out_rows=n, out_row0=0)
        u_c, x0_c = _hy_pre(zh, sw, sb, row0=nl, batch=batch, seq=ctx_len, out_rows=nc, out_row0=0)
        filt = (hy_f_w1[i], hy_f_b1[i], hy_f_freq1[i], hy_f_w2[i], hy_f_b2[i], hy_f_freq2[i], hy_f_w3[i])
        bias = hy_bias[i][None, :]
        hf, hb, asum = _hy_filter(seq, *filt, hw)
        taps = jnp.concatenate([hf, jnp.roll(jnp.flip(hb, axis=0), 1, axis=0)], axis=0)
        kr1, ki1 = _fft_first(taps.reshape(FFT_N1, ncols), consts["f1_real"], ncols)
        spectrum = _fft_mid(kr1, ki1, consts, n2, hw)
        ar, ai = _fft_first(u_l.reshape(-1, ncols), consts["f1_data"], ncols)
        br, bi = _fft_mid(ar, ai, consts, n2, hw, spectrum=spectrum)
        reps = min(FFT_COLS, ncols) // hw
        yh_lat = _fft_last(br.reshape(FFT_N1, ncols), bi.reshape(FFT_N1, ncols), consts["g1"],
                           u_l.reshape(-1, ncols), x0_l.reshape(-1, ncols),
                           jnp.tile(bias, (1, reps)), jnp.tile(1.0 / asum, (1, reps)), ncols)
        hf_c, hb_c, asum_c = _hy_filter(ctx_len, *filt, hw)
        taps_c = jnp.concatenate([hf_c, jnp.roll(jnp.flip(hb_c, axis=0), 1, axis=0)], axis=0)
        yh_ctx = _hy_ctx(u_c, x0_c, taps_c, bias, 1.0 / asum_c, small)
        yh = jnp.concatenate([yh_lat.reshape(nl, hw), yh_ctx], axis=0)

        j = i // 2
        moe = i % 2 == 1
        rw = None
        if moe:
            rw = jnp.concatenate([router_w[j], jnp.zeros((d, 128 - N_EXPERTS), F32)], axis=1)
        outs = _out_proj(xs, yp, yh, ya, w_out[i].astype(BF16), norm2_g[i][None, :], mod, rw, **tiles)
        out_rows = nl if last else n
        if moe:
            xs, u, route = outs
            xs = _moe_layer(xs, u, route, mod, moe_w_gate[j].astype(BF16), moe_w_up[j].astype(BF16),
                            moe_w_down[j].astype(BF16), out_rows=out_rows, **tiles)
        else:
            xs, u = outs
            xs = _ffn(xs, u, mod, ffn_w_gate[j].astype(BF16), ffn_w_up[j].astype(BF16),
                      ffn_w_down[j].astype(BF16), out_rows=out_rows, **tiles)
    return xs[:nl].reshape(batch, seq, d)
```

```python
import functools
import math

import numpy as np
import jax
import jax.numpy as jnp
from jax import lax
from jax.experimental import pallas as pl
from jax.experimental.pallas import tpu as pltpu

F32 = jnp.float32
BF16 = jnp.bfloat16
EPS = 1e-6

GRID_W = 64
POOL_WINDOWS = (2, 4, 8, 16)
HEADS = 4
HEAD_DIM = 64
V_DIM = 128
ROPE_FREQS = 16
ROPE_THETA = 10000.0
HY_BANDS = 16
HY_TARGET = 1e-2
HY_FAST = 0.3
HY_SLOW = 1.5
N_EXPERTS = 8

ROW_TILE = 512
HALO = 32
FFT_N1 = 128
FFT_COLS = 2048
ATT_TQ = 512
ATT_TK = 512
ATT_UNROLL = 8
QSCALE = (HEAD_DIM ** -0.5) * math.log2(math.e)
ATT_DIRECT_MAX = 100.0
MOE_TM = 512
MOE_FC = 1792
VMEM_LIMIT = 56 * 1024 * 1024


def _cparams(sem):
    return pltpu.CompilerParams(dimension_semantics=sem, vmem_limit_bytes=VMEM_LIMIT)


def _dot(a, b):
    return jnp.dot(a, b, preferred_element_type=F32)


def _split(a):
    hi = a.astype(BF16)
    lo = (a - hi.astype(F32)).astype(BF16)
    return hi, lo


def _dot3(a, b):
    ah, al = _split(a)
    bh, bl = _split(b)
    return _dot(ah, bh) + _dot(ah, bl) + _dot(al, bh)


def _dot3c(ch, cl, d):
    dh, dl = _split(d)
    return _dot(ch, dh) + _dot(ch, dl) + _dot(cl, dh)


def _silu(x):
    return x / (1.0 + jnp.exp(-x))


def _normmod(x, g, shift, scale):
    ms = jnp.mean(x * x, axis=-1, keepdims=True)
    return (x * lax.rsqrt(ms + EPS) * g) * (1.0 + scale) + shift


def _cond_row(t, n_lat_tiles, tiles_per_batch, n_cond):
    return jnp.where(t >= n_lat_tiles, n_cond - 1, t // tiles_per_batch)


def _modvec_kernel(c_ref, w_ref, b_ref, o_ref):
    s = _silu(c_ref[...])
    o_ref[0, 0] = _dot3(s, w_ref[0]) + b_ref[0]


def _modvec(cond8, mod_w, mod_b):
    depth, d, six_d = mod_w.shape
    nchunk = six_d // d
    return pl.pallas_call(
        _modvec_kernel,
        grid=(depth, nchunk),
        in_specs=[
            pl.BlockSpec((8, d), lambda i, j: (0, 0)),
            pl.BlockSpec((1, d, d), lambda i, j: (i, 0, j)),
            pl.BlockSpec((1, 1, d), lambda i, j: (i, 0, j)),
        ],
        out_specs=pl.BlockSpec((1, 1, 8, d), lambda i, j: (i, j, 0, 0)),
        out_shape=jax.ShapeDtypeStruct((depth, nchunk, 8, d), F32),
        compiler_params=_cparams(("arbitrary", "arbitrary")),
    )(cond8, mod_w, mod_b.reshape(depth, 1, six_d))


def _in_proj_kernel(x_ref, g_ref, mod_ref, w_ref, zp_ref, zh_ref, za_ref, *, nlt, tpb, ncond, pw, hw):
    ci = _cond_row(pl.program_id(0), nlt, tpb, ncond)
    shift = mod_ref[0, pl.ds(ci, 1), :]
    scale = mod_ref[1, pl.ds(ci, 1), :]
    y = _normmod(x_ref[...], g_ref[...], shift, scale).astype(BF16)
    z = _dot(y, w_ref[...])
    zp_ref[...] = z[:, :pw].astype(BF16)
    zh_ref[...] = z[:, pw:pw + 3 * hw].astype(BF16)
    za_ref[...] = z[:, pw + 3 * hw:].astype(BF16)


def _in_proj(x, g, mod, w, *, nlt, tpb, ncond, pw, hw):
    n, d = x.shape
    wid = w.shape[1]
    aw = wid - pw - 3 * hw
    tm = ROW_TILE
    kern = functools.partial(_in_proj_kernel, nlt=nlt, tpb=tpb, ncond=ncond, pw=pw, hw=hw)
    return pl.pallas_call(
        kern,
        grid=(n // tm,),
        in_specs=[
            pl.BlockSpec((tm, d), lambda t: (t, 0)),
            pl.BlockSpec((1, d), lambda t: (0, 0)),
            pl.BlockSpec(mod.shape, lambda t: (0, 0, 0)),
            pl.BlockSpec((d, wid), lambda t: (0, 0)),
        ],
        out_specs=[
            pl.BlockSpec((tm, pw), lambda t: (t, 0)),
            pl.BlockSpec((tm, 3 * hw), lambda t: (t, 0)),
            pl.BlockSpec((tm, aw), lambda t: (t, 0)),
        ],
        out_shape=[
            jax.ShapeDtypeStruct((n, pw), BF16),
            jax.ShapeDtypeStruct((n, 3 * hw), BF16),
            jax.ShapeDtypeStruct((n, aw), BF16),
        ],
        compiler_params=_cparams(("arbitrary",)),
    )(x, g, mod, w)


def _prep_kernel(za_ref, cos_ref, sin_ref, g_ref, qt_ref, k_ref, vt_ref, *, qscale):
    tm = za_ref.shape[0]
    lane = lax.broadcasted_iota(jnp.int32, (tm, 2 * HEAD_DIM), 1)
    first = lane < HEAD_DIM
    apart = (lane % (2 * ROPE_FREQS)) < ROPE_FREQS
    cos = cos_ref[...]
    sin = sin_ref[...]
    qk_w = HEADS * 2 * HEAD_DIM

    def norm_rope(x, g):
        x2 = x * x
        s_all = jnp.sum(x2, axis=-1, keepdims=True)
        s_first = jnp.sum(jnp.where(first, x2, 0.0), axis=-1, keepdims=True)
        ms = jnp.where(first, s_first, s_all - s_first) * (1.0 / HEAD_DIM)
        xn = x * lax.rsqrt(ms + EPS) * g
        swapped = jnp.where(apart, pltpu.roll(xn, 2 * HEAD_DIM - ROPE_FREQS, 1), pltpu.roll(xn, ROPE_FREQS, 1))
        return xn * cos + swapped * sin

    for h in range(HEADS):
        lo = h * 2 * HEAD_DIM
        q = norm_rope(za_ref[:, lo:lo + 2 * HEAD_DIM].astype(F32), g_ref[0:1, :])
        qt_ref[h] = (q * qscale).T.astype(BF16)
        k = norm_rope(za_ref[:, qk_w + lo:qk_w + lo + 2 * HEAD_DIM].astype(F32), g_ref[1:2, :])
        k_ref[h] = k.astype(BF16)
        vlo = 2 * qk_w + h * V_DIM
        vt_ref[h] = za_ref[:, vlo:vlo + V_DIM].astype(F32).T.astype(BF16)


def _attn_prep(za, cos_t, sin_t, g2):
    n, aw = za.shape
    tm = ROW_TILE
    return pl.pallas_call(
        functools.partial(_prep_kernel, qscale=QSCALE),
        grid=(n // tm,),
        in_specs=[
            pl.BlockSpec((tm, aw), lambda t: (t, 0)),
            pl.BlockSpec((tm, 2 * HEAD_DIM), lambda t: (t, 0)),
            pl.BlockSpec((tm, 2 * HEAD_DIM), lambda t: (t, 0)),
            pl.BlockSpec((2, 2 * HEAD_DIM), lambda t: (0, 0)),
        ],
        out_specs=[
            pl.BlockSpec((HEADS, 2 * HEAD_DIM, tm), lambda t: (0, 0, t)),
            pl.BlockSpec((HEADS, tm, 2 * HEAD_DIM), lambda t: (0, t, 0)),
            pl.BlockSpec((HEADS, V_DIM, tm), lambda t: (0, 0, t)),
        ],
        out_shape=[
            jax.ShapeDtypeStruct((HEADS, 2 * HEAD_DIM, n), BF16),
            jax.ShapeDtypeStruct((HEADS, n, 2 * HEAD_DIM), BF16),
            jax.ShapeDtypeStruct((HEADS, V_DIM, n), BF16),
        ],
        compiler_params=_cparams(("arbitrary",)),
    )(za, cos_t, sin_t, g2)


def _attn_kernel(lam_ref, qt_ref, kc_ref, vtc_ref, *rest, tk, n_lat_chunks, lam_init):
    if n_lat_chunks:
        kl_ref, vtl_ref, g_ref, o_ref, m_ref, l_ref, acc_ref = rest
    else:
        g_ref, o_ref, m_ref, l_ref, acc_ref = rest
    qt = qt_ref[0]
    row = lax.broadcasted_iota(jnp.int32, qt.shape, 0)
    zero = jnp.zeros_like(qt)
    qmaps = (jnp.where(row < HEAD_DIM, qt, zero), jnp.where(row >= HEAD_DIM, qt, zero))

    def process(k_tile, vt_tile, first):
        for mi in range(2):
            s = _dot(k_tile, qmaps[mi])
            smax = jnp.max(s, axis=0, keepdims=True)
            if first:
                m_new = smax
            else:
                m_old = m_ref[mi]
                m_new = jnp.maximum(m_old, smax)
            p = jnp.exp2(s - m_new)
            psum = jnp.sum(p, axis=0, keepdims=True)
            pv = _dot(vt_tile, p.astype(BF16))
            if first:
                l_ref[mi] = psum
                acc_ref[mi] = pv
            else:
                alpha = jnp.exp2(m_old - m_new)
                l_ref[mi] = alpha * l_ref[mi] + psum
                acc_ref[mi] = alpha * acc_ref[mi] + pv
            m_ref[mi] = m_new

    process(kc_ref[0], vtc_ref[0], True)
    if n_lat_chunks:
        def body(j, carry):
            off = pl.multiple_of(j * tk, tk)
            process(kl_ref[0, pl.ds(off, tk), :], vtl_ref[0, :, pl.ds(off, tk)], False)
            return carry
        lax.fori_loop(0, n_lat_chunks, body, 0)

    lv = lam_ref[...]
    lam = (jnp.exp(jnp.sum(lv[0:1] * lv[1:2], axis=-1, keepdims=True))
           - jnp.exp(jnp.sum(lv[2:3] * lv[3:4], axis=-1, keepdims=True)) + lam_init)
    o = acc_ref[0] * (1.0 / l_ref[0]) - lam * (acc_ref[1] * (1.0 / l_ref[1]))
    ms = jnp.mean(o * o, axis=0, keepdims=True)
    y = o * lax.rsqrt(ms + EPS) * g_ref[...] * (1.0 - lam_init)
    o_ref[...] = y.T.astype(BF16)


def _attn_direct_kernel(lam_ref, qt_ref, kc_ref, vtc_ref, *rest, tk, n_lat_chunks, lam_init):
    if n_lat_chunks:
        kl_ref, vtl_ref, g_ref, o_ref, s_ref, p_ref, l_ref, acc_ref = rest
    else:
        g_ref, o_ref, s_ref, p_ref, l_ref, acc_ref = rest
    qt = qt_ref[0]
    tq = qt.shape[1]
    row = lax.broadcasted_iota(jnp.int32, qt.shape, 0)
    zero = jnp.zeros_like(qt)
    qmaps = (jnp.where(row < HEAD_DIM, qt, zero), jnp.where(row >= HEAD_DIM, qt, zero))

    def piece_rows(j):
        if isinstance(j, int):
            return pl.ds((j - 1) * tk, tk)
        return pl.ds(pl.multiple_of((j - 1) * tk, tk), tk)

    def k_piece(j):
        return kl_ref[0, piece_rows(j), :]

    def vt_piece(j):
        return vtc_ref[0] if isinstance(j, int) and j == 0 else vtl_ref[0, :, piece_rows(j)]

    def step(j, par):
        stage_a(k_piece(j + 1), 1 - par)
        stage_b(par)
        stage_c(vt_piece(j - 1), 1 - par)

    def stage_a(k_tile, slot):
        for mi in range(2):
            s_ref[slot, mi] = _dot(k_tile, qmaps[mi])

    def stage_b(slot):
        for mi in range(2):
            p = jnp.exp2(s_ref[slot, mi])
            l_ref[mi] = l_ref[mi] + jnp.sum(p.reshape(tk // 8, 8, tq), axis=0)
            p_ref[slot, mi] = p.astype(BF16)

    def stage_c(vt_tile, slot):
        for mi in range(2):
            acc_ref[mi] = acc_ref[mi] + _dot(vt_tile, p_ref[slot, mi])

    l_ref[...] = jnp.zeros_like(l_ref)
    acc_ref[...] = jnp.zeros_like(acc_ref)
    stage_a(kc_ref[0], 0)
    if n_lat_chunks == 0:
        stage_b(0)
        stage_c(vtc_ref[0], 0)
    else:
        stage_a(k_piece(1), 1)
        stage_b(0)
        for j in range(1, ATT_UNROLL):
            step(j, j % 2)

        def body(jj, carry):
            j = ATT_UNROLL * (jj + 1)
            for r in range(ATT_UNROLL):
                step(j + r, r % 2)
            return carry

        lax.fori_loop(0, n_lat_chunks // ATT_UNROLL - 1, body, 0)
        stage_b(0)
        stage_c(vt_piece(n_lat_chunks - 1), 1)
        stage_c(vt_piece(n_lat_chunks), 0)

    lv = lam_ref[...]
    lam = (jnp.exp(jnp.sum(lv[0:1] * lv[1:2], axis=-1, keepdims=True))
           - jnp.exp(jnp.sum(lv[2:3] * lv[3:4], axis=-1, keepdims=True)) + lam_init)
    l1 = jnp.sum(l_ref[0], axis=0, keepdims=True)
    l2 = jnp.sum(l_ref[1], axis=0, keepdims=True)
    o = acc_ref[0] * (1.0 / l1) - lam * (acc_ref[1] * (1.0 / l2))
    ms = jnp.mean(o * o, axis=0, keepdims=True)
    y = o * lax.rsqrt(ms + EPS) * g_ref[...] * (1.0 - lam_init)
    o_ref[...] = y.T.astype(BF16)


def _attention(lam_p, qt, k12, vt, g_b, *, batch, seq, ctx_len, latent, lam_init, direct):
    n = k12.shape[1]
    nl = batch * seq
    tq = ATT_TQ if latent else ctx_len
    nq = seq // tq if latent else 1
    tk = ctx_len if direct else ATT_TK
    assert seq % ((ATT_UNROLL if direct else 1) * tk) == 0
    qbase = 0 if latent else nl // tq

    def qrow(b, iq):
        return (b * nq + iq) if latent else (qbase + b)

    in_specs = [
        pl.BlockSpec(lam_p.shape, lambda b, h, iq: (0, 0)),
        pl.BlockSpec((1, 2 * HEAD_DIM, tq), lambda b, h, iq: (h, 0, qrow(b, iq))),
        pl.BlockSpec((1, ctx_len, 2 * HEAD_DIM), lambda b, h, iq: (h, nl // ctx_len + b, 0)),
        pl.BlockSpec((1, V_DIM, ctx_len), lambda b, h, iq: (h, 0, nl // ctx_len + b)),
    ]
    args = [lam_p, qt, k12, vt]
    if latent:
        in_specs += [
            pl.BlockSpec((1, seq, 2 * HEAD_DIM), lambda b, h, iq: (h, b, 0)),
            pl.BlockSpec((1, V_DIM, seq), lambda b, h, iq: (h, 0, b)),
        ]
        args += [k12, vt]
    in_specs.append(pl.BlockSpec((V_DIM, tq), lambda b, h, iq: (0, 0)))
    args.append(g_b[:, :tq])
    nrows = nl if latent else batch * ctx_len
    body = _attn_direct_kernel if direct else _attn_kernel
    kern = functools.partial(body, tk=tk, n_lat_chunks=(seq // tk if latent else 0), lam_init=lam_init)
    if direct:
        scratch = [pltpu.VMEM((2, 2, tk, tq), F32), pltpu.VMEM((2, 2, tk, tq), BF16),
                   pltpu.VMEM((2, 8, tq), F32), pltpu.VMEM((2, V_DIM, tq), F32)]
    else:
        scratch = [pltpu.VMEM((2, 1, tq), F32), pltpu.VMEM((2, 1, tq), F32), pltpu.VMEM((2, V_DIM, tq), F32)]
    return pl.pallas_call(
        kern,
        grid=(batch, HEADS, nq),
        in_specs=in_specs,
        out_specs=pl.BlockSpec((tq, V_DIM), lambda b, h, iq: (b * nq + iq, h)),
        out_shape=jax.ShapeDtypeStruct((nrows, HEADS * V_DIM), BF16),
        scratch_shapes=scratch,
        compiler_params=_cparams(("arbitrary", "arbitrary", "arbitrary")),
    )(*args)


def _halo_specs(width, *, row0, seq, tile):
    hb = tile // HALO

    def cur(b, i):
        return ((row0 + b * seq) // tile + i, 0)

    def prev(b, i):
        first = (row0 + b * seq) // HALO
        return (jnp.maximum(first + i * hb - 1, first), 0)

    def nxt(b, i):
        first = (row0 + b * seq) // HALO
        return (jnp.minimum(first + (i + 1) * hb, first + seq // HALO - 1), 0)

    return [pl.BlockSpec((HALO, width), prev), pl.BlockSpec((tile, width), cur), pl.BlockSpec((HALO, width), nxt)]


def _fill_ext(ext_ref, prev_ref, cur_ref, next_ref, *, seq, tile):
    pos0 = pl.program_id(1) * tile
    width = cur_ref.shape[1]
    hpos = lax.broadcasted_iota(jnp.int32, (HALO, width), 0)
    ext_ref[0:HALO, :] = jnp.where(pos0 - HALO + hpos >= 0, prev_ref[...].astype(F32), 0.0)
    ext_ref[HALO:HALO + tile, :] = cur_ref[...].astype(F32)
    ext_ref[HALO + tile:, :] = jnp.where(pos0 + tile + hpos < seq, next_ref[...].astype(F32), 0.0)


def _pool_kernel(prev_ref, cur_ref, next_ref, lin_ref, scale_ref, o_ref, ext_ref, s_ref, *, seq, tile):
    _fill_ext(ext_ref, prev_ref, cur_ref, next_ref, seq=seq, tile=tile)
    width = cur_ref.shape[1]
    gd = width // len(POOL_WINDOWS)
    n0 = tile + 2 * HALO
    s_ref[0, 0:n0 - 8, :] = ext_ref[0:n0 - 8, :] + ext_ref[1:n0 - 7, :]
    for k in range(1, len(POOL_WINDOWS)):
        step = 1 << k
        ln = n0 - 8 * (k + 1)
        s_ref[k, 0:ln, :] = s_ref[k - 1, 0:ln, :] + s_ref[k - 1, step:step + ln, :]
    lane = lax.broadcasted_iota(jnp.int32, (tile, width), 1)
    pos = pl.program_id(1) * tile + lax.broadcasted_iota(jnp.int32, (tile, width), 0)
    grp = lane // gd
    wsum = jnp.zeros((tile, width), F32)
    half = jnp.zeros((tile, width), jnp.int32)
    for k, win in enumerate(POOL_WINDOWS):
        start = HALO - win // 2
        wsum = jnp.where(grp == k, s_ref[k, start:start + tile, :], wsum)
        half = jnp.where(grp == k, win // 2, half)
    cnt = jnp.minimum(pos + half, seq) - jnp.maximum(pos - half, 0)
    z = ext_ref[HALO:HALO + tile, :]
    dlt = (wsum / cnt.astype(F32) - z).astype(BF16)
    o_ref[...] = (_dot(dlt, lin_ref[...]) * scale_ref[...]).astype(BF16)


def _pool(zp, lin_bd, scale, *, row0, batch, seq, out_rows, out_row0):
    width = zp.shape[1]
    tile = min(ROW_TILE, seq)
    kern = functools.partial(_pool_kernel, seq=seq, tile=tile)
    return pl.pallas_call(
        kern,
        grid=(batch, seq // tile),
        in_specs=_halo_specs(width, row0=row0, seq=seq, tile=tile) + [
            pl.BlockSpec((width, width), lambda b, i: (0, 0)),
            pl.BlockSpec((1, width), lambda b, i: (0, 0)),
        ],
        out_specs=pl.BlockSpec((tile, width), lambda b, i: ((out_row0 + b * seq) // tile + i, 0)),
        out_shape=jax.ShapeDtypeStruct((out_rows, width), BF16),
        scratch_shapes=[
            pltpu.VMEM((tile + 2 * HALO, width), F32),
            pltpu.VMEM((4, tile + 2 * HALO, width), F32),
        ],
        compiler_params=_cparams(("arbitrary", "arbitrary")),
    )(zp, zp, zp, lin_bd, scale)


def _hy_pre_kernel(prev_ref, cur_ref, next_ref, w_ref, b_ref, u_ref, x0_ref, ext_ref, *, seq, tile, hw):
    _fill_ext(ext_ref, prev_ref, cur_ref, next_ref, seq=seq, tile=tile)
    y = b_ref[...] + ext_ref[HALO - 1:HALO - 1 + tile, :] * w_ref[0:1, :]
    y = y + ext_ref[HALO:HALO + tile, :] * w_ref[1:2, :]
    y = y + ext_ref[HALO + 1:HALO + 1 + tile, :] * w_ref[2:3, :]
    x0_ref[...] = y[:, :hw]
    u_ref[...] = y[:, 2 * hw:] * y[:, hw:2 * hw]


def _hy_pre(zh, sw, sb, *, row0, batch, seq, out_rows, out_row0):
    width = zh.shape[1]
    hw = width // 3
    tile = min(ROW_TILE, seq)
    kern = functools.partial(_hy_pre_kernel, seq=seq, tile=tile, hw=hw)
    ospec = pl.BlockSpec((tile, hw), lambda b, i: ((out_row0 + b * seq) // tile + i, 0))
    return pl.pallas_call(
        kern,
        grid=(batch, seq // tile),
        in_specs=_halo_specs(width, row0=row0, seq=seq, tile=tile) + [
            pl.BlockSpec((3, width), lambda b, i: (0, 0)),
            pl.BlockSpec((1, width), lambda b, i: (0, 0)),
        ],
        out_specs=[ospec, ospec],
        out_shape=[jax.ShapeDtypeStruct((out_rows, hw), F32), jax.ShapeDtypeStruct((out_rows, hw), F32)],
        scratch_shapes=[pltpu.VMEM((tile + 2 * HALO, width), F32)],
        compiler_params=_cparams(("arbitrary", "arbitrary")),
    )(zh, zh, zh, sw, sb)


def _filter_kernel(feat_ref, w1_ref, b1_ref, f1_ref, w2_ref, b2_ref, f2_ref, w3_ref, dl_ref,
                   taps_ref, asum_ref, *, tile, hw, seq):
    feat = feat_ref[...]
    h = jnp.sin(f1_ref[...] * (_dot3(feat, w1_ref[...]) + b1_ref[...]))
    h = jnp.sin(f2_ref[...] * (_dot3(h, w2_ref[...]) + b2_ref[...]))
    h = _dot3(h, w3_ref[...])
    dec = jnp.exp(-feat[:, 0:1] * dl_ref[...])
    pos = pl.program_id(0) * tile + lax.broadcasted_iota(jnp.int32, (tile, hw), 0)
    taps = jnp.where(pos == seq, 0.0, h * dec)
    taps_ref[...] = taps
    part = jnp.sum(jnp.abs(taps), axis=0, keepdims=True)

    @pl.when(pl.program_id(0) == 0)
    def _():
        asum_ref[...] = part

    @pl.when(pl.program_id(0) != 0)
    def _():
        asum_ref[...] = asum_ref[...] + part


def _hy_filter(seq, w1, b1, f1, w2, b2, f2, w3, hw):
    t = jnp.linspace(0.0, 1.0, seq, dtype=F32)[:, None]
    w = (2.0 * math.pi / seq) * jnp.arange(seq, dtype=F32)[:, None]
    bands = jnp.linspace(1e-4, HY_BANDS - 1, HY_BANDS, dtype=F32)[None, :]
    emb = 1 + 2 * HY_BANDS
    embp = ((emb + 7) // 8) * 8
    feat = jnp.concatenate([t, jnp.cos(bands * w), -jnp.sin(bands * w), jnp.zeros((seq, embp - emb), F32)], axis=-1)
    w1p = jnp.concatenate([w1, jnp.zeros((embp - emb, w1.shape[1]), F32)], axis=0)
    max_decay = math.log(1.0 / HY_TARGET) / HY_FAST
    min_decay = math.log(1.0 / HY_TARGET) / HY_SLOW
    deltas = jnp.linspace(min_decay, max_decay, hw, dtype=F32)[None, :]
    feat = jnp.concatenate([feat, feat[0:1], jnp.flip(feat[1:], axis=0)], axis=0)
    tile = min(ROW_TILE, seq)
    nfwd = seq // tile
    hid = w1.shape[1]
    full = lambda shape: pl.BlockSpec(shape, lambda i: (0,) * len(shape))
    return pl.pallas_call(
        functools.partial(_filter_kernel, tile=tile, hw=hw, seq=seq),
        grid=(2 * nfwd,),
        in_specs=[
            pl.BlockSpec((tile, embp), lambda i: (i, 0)),
            full((embp, hid)), full((1, hid)), full((1, hid)),
            full((hid, hid)), full((1, hid)), full((1, hid)),
            pl.BlockSpec((hid, hw), lambda i: (0, i // nfwd)), full((1, hw)),
        ],
        out_specs=[
            pl.BlockSpec((tile, hw), lambda i: (i, 0)),
            pl.BlockSpec((1, hw), lambda i: (0, 0)),
        ],
        out_shape=[
            jax.ShapeDtypeStruct((2 * seq, hw), F32),
            jax.ShapeDtypeStruct((1, hw), F32),
        ],
        compiler_params=_cparams(("arbitrary",)),
    )(feat, w1p, b1[None, :], f1[None, :], w2, b2[None, :], f2[None, :], w3, deltas)


def _np_split(a):
    a32 = jnp.asarray(a, F32)
    hi = a32.astype(BF16)
    lo = (a32 - hi.astype(F32)).astype(BF16)
    return hi, lo


def _dft_consts(n2):
    n1 = FFT_N1
    n = n1 * n2
    half = n1 // 2
    a1 = -2.0 * np.pi * np.outer(np.arange(n1), np.arange(n1)) / n1
    f1r, f1i = np.cos(a1), np.sin(a1)
    f1_data = np.block([[f1r[:, :half], -f1i[:, :half]], [f1i[:, :half], f1r[:, :half]]])
    f1_real = np.concatenate([f1r, f1i], axis=0)
    g1r, g1i = f1r / n, -f1i / n
    g1 = np.block([[g1r[:half], -g1i[:half]], [g1i[:half], g1r[:half]]])
    a2 = -2.0 * np.pi * np.outer(np.arange(n2), np.arange(n2)) / n2
    f2r, f2i = np.cos(a2), np.sin(a2)
    f2 = np.block([[f2r, -f2i], [f2i, f2r]])
    g2 = np.block([[f2r, f2i], [-f2i, f2r]])
    at = -2.0 * np.pi * np.outer(np.arange(n1), np.arange(n2)) / n
    twr = jnp.broadcast_to(jnp.asarray(np.cos(at), F32)[:, :, None], (n1, n2, 128))
    twi = jnp.broadcast_to(jnp.asarray(np.sin(at), F32)[:, :, None], (n1, n2, 128))
    return dict(f1_data=_np_split(f1_data), f1_real=_np_split(f1_real), g1=_np_split(g1),
                f2=_np_split(f2), g2=_np_split(g2), twr=twr, twi=twi)


def _fft1_kernel(z_ref, fh_ref, fl_ref, ar_ref, ai_ref):
    a = _dot3c(fh_ref[...], fl_ref[...], z_ref[...])
    ar_ref[...] = a[:FFT_N1]
    ai_ref[...] = a[FFT_N1:]


def _fft_first(zview, fmat, ncols):
    cb = min(FFT_COLS, ncols)
    fh, fl = fmat
    cspec = pl.BlockSpec(fh.shape, lambda j: (0, 0))
    ospec = pl.BlockSpec((FFT_N1, cb), lambda j: (0, j))
    return pl.pallas_call(
        _fft1_kernel,
        grid=(ncols // cb,),
        in_specs=[pl.BlockSpec((FFT_N1, cb), lambda j: (0, j)), cspec, cspec],
        out_specs=[ospec, ospec],
        out_shape=[jax.ShapeDtypeStruct((FFT_N1, ncols), F32)] * 2,
        compiler_params=_cparams(("arbitrary",)),
    )(zview, fh, fl)


def _fftmid_kernel(ar_ref, ai_ref, twr_ref, twi_ref, f2h_ref, f2l_ref, *rest, filter_only, n2):
    width = ar_ref.shape[2]
    reps = width // 128
    twr = jnp.concatenate([twr_ref[0]] * reps, axis=1)
    twi = jnp.concatenate([twi_ref[0]] * reps, axis=1)
    ar, ai = ar_ref[0], ai_ref[0]
    z = jnp.concatenate([ar * twr - ai * twi, ar * twi + ai * twr], axis=0)
    x = _dot3c(f2h_ref[...], f2l_ref[...], z)
    xr, xi = x[:n2], x[n2:]
    if filter_only:
        kr_out, ki_out = rest
        kr_out[0] = xr
        ki_out[0] = xi
        return
    kr_ref, ki_ref, g2h_ref, g2l_ref, br_out, bi_out = rest
    kr, ki = kr_ref[0], ki_ref[0]
    y = jnp.concatenate([xr * kr - xi * ki, xr * ki + xi * kr], axis=0)
    w = _dot3c(g2h_ref[...], g2l_ref[...], y)
    wr, wi = w[:n2], w[n2:]
    br_out[0] = wr * twr + wi * twi
    bi_out[0] = wi * twr - wr * twi


def _fft_mid(ar, ai, consts, n2, width, spectrum=None):
    a3r = ar.reshape(FFT_N1, n2, width)
    a3i = ai.reshape(FFT_N1, n2, width)
    blk = pl.BlockSpec((1, n2, width), lambda k: (k, 0, 0))
    twspec = pl.BlockSpec((1, n2, 128), lambda k: (k, 0, 0))
    cspec = pl.BlockSpec((2 * n2, 2 * n2), lambda k: (0, 0))
    in_specs = [blk, blk, twspec, twspec, cspec, cspec]
    args = [a3r, a3i, consts["twr"], consts["twi"], *consts["f2"]]
    if spectrum is not None:
        in_specs += [blk, blk, cspec, cspec]
        args += [spectrum[0], spectrum[1], *consts["g2"]]
    return pl.pallas_call(
        functools.partial(_fftmid_kernel, filter_only=spectrum is None, n2=n2),
        grid=(FFT_N1,),
        in_specs=in_specs,
        out_specs=[blk, blk],
        out_shape=[jax.ShapeDtypeStruct((FFT_N1, n2, width), F32)] * 2,
        compiler_params=_cparams(("arbitrary",)),
    )(*args)


def _fftlast_kernel(br_ref, bi_ref, gh_ref, gl_ref, u_ref, x0_ref, bias_ref, invn_ref, o_ref):
    b = jnp.concatenate([br_ref[...], bi_ref[...]], axis=0)
    y = _dot3c(gh_ref[...], gl_ref[...], b)
    o_ref[...] = ((y * invn_ref[...] + u_ref[...] * bias_ref[...]) * x0_ref[...]).astype(BF16)


def _fft_last(br, bi, gmat, uview, x0view, bias_t, invn_t, ncols):
    cb = min(FFT_COLS, ncols)
    gh, gl = gmat
    cspec = pl.BlockSpec(gh.shape, lambda j: (0, 0))
    dspec = pl.BlockSpec((FFT_N1, cb), lambda j: (0, j))
    vspec = pl.BlockSpec((1, cb), lambda j: (0, 0))
    return pl.pallas_call(
        _fftlast_kernel,
        grid=(ncols // cb,),
        in_specs=[dspec, dspec, cspec, cspec, dspec, dspec, vspec, vspec],
        out_specs=dspec,
        out_shape=jax.ShapeDtypeStruct((FFT_N1, ncols), BF16),
        compiler_params=_cparams(("arbitrary",)),
    )(br, bi, gh, gl, uview, x0view, bias_t, invn_t)


def _dft_small_consts(seq):
    n = 2 * seq
    a = -2.0 * np.pi * np.outer(np.arange(n), np.arange(n)) / n
    fr, fi = np.cos(a), np.sin(a)
    f_data = np.block([[fr[:, :seq], -fi[:, :seq]], [fi[:, :seq], fr[:, :seq]]])
    f_real = np.concatenate([fr, fi], axis=0)
    gr, gi = fr / n, -fi / n
    g = np.block([[gr[:seq], -gi[:seq]], [gi[:seq], gr[:seq]]])
    return _np_split(f_data), _np_split(f_real), _np_split(g)


def _hy_ctx_kernel(u_ref, x0_ref, taps_ref, fdh_ref, fdl_ref, frh_ref, frl_ref, gh_ref, gl_ref,
                   bias_ref, invn_ref, o_ref):
    n = taps_ref.shape[0]
    z = u_ref[...]
    a = _dot3c(fdh_ref[...], fdl_ref[...], z)
    k = _dot3c(frh_ref[...], frl_ref[...], taps_ref[...])
    ar, ai, kr, ki = a[:n], a[n:], k[:n], k[n:]
    y = jnp.concatenate([ar * kr - ai * ki, ar * ki + ai * kr], axis=0)
    conv = _dot3c(gh_ref[...], gl_ref[...], y)
    o_ref[...] = ((conv * invn_ref[...] + z * bias_ref[...]) * x0_ref[...]).astype(BF16)


def _hy_ctx(u, x0, taps, bias, invn, small):
    rows, hw = u.shape
    full = lambda a: pl.BlockSpec(a.shape, lambda i: (0,) * a.ndim)
    args = [u, x0, taps, *small[0], *small[1], *small[2], bias, invn]
    return pl.pallas_call(
        _hy_ctx_kernel,
        grid=(1,),
        in_specs=[full(a) for a in args],
        out_specs=pl.BlockSpec((rows, hw), lambda i: (0, 0)),
        out_shape=jax.ShapeDtypeStruct((rows, hw), BF16),
        compiler_params=_cparams(("arbitrary",)),
    )(*args)


def _out_proj_kernel(x_ref, ypl_ref, yhl_ref, yal_ref, ypc_ref, yhc_ref, yac_ref, w_ref, g_ref, mod_ref, *rest,
                     nlt, tpb, ncond, pw, hw, moe):
    if moe:
        rw_ref, xo_ref, u_ref, route_ref = rest
    else:
        xo_ref, u_ref = rest
    ci = _cond_row(pl.program_id(0), nlt, tpb, ncond)
    is_ctx = pl.program_id(0) >= nlt
    yp = jnp.where(is_ctx, ypc_ref[...], ypl_ref[...])
    yh = jnp.where(is_ctx, yhc_ref[...], yhl_ref[...])
    ya = jnp.where(is_ctx, yac_ref[...], yal_ref[...])
    mix = _dot(yp, w_ref[0:pw, :]) + _dot(yh, w_ref[pw:pw + hw, :]) + _dot(ya, w_ref[pw + hw:, :])
    x = x_ref[...] + mod_ref[2, pl.ds(ci, 1), :] * mix
    xo_ref[...] = x
    un = _normmod(x, g_ref[...], mod_ref[3, pl.ds(ci, 1), :], mod_ref[4, pl.ds(ci, 1), :])
    u_ref[...] = un.astype(BF16)
    if moe:
        logits = _dot3(un, rw_ref[...])
        lane = lax.broadcasted_iota(jnp.int32, logits.shape, 1)
        neg = jnp.float32(-jnp.inf)
        lg = jnp.where(lane < N_EXPERTS, logits, neg)
        t1 = jnp.max(lg, axis=-1, keepdims=True)
        i1 = jnp.min(jnp.where(lg == t1, lane, 128), axis=-1, keepdims=True)
        lg2 = jnp.where(lane == i1, neg, lg)
        t2 = jnp.max(lg2, axis=-1, keepdims=True)
        i2 = jnp.min(jnp.where(lg2 == t2, lane, 128), axis=-1, keepdims=True)
        e2 = jnp.exp(t2 - t1)
        g1 = 1.0 / (1.0 + e2)
        g2 = e2 / (1.0 + e2)
        route_ref[...] = jnp.where(lane == 0, i1.astype(F32), jnp.where(lane == 1, i2.astype(F32),
                                   jnp.where(lane == 2, g1, jnp.where(lane == 3, g2, 0.0))))


def _out_proj(x, lat, ctx, w, g, mod, rw, *, nlt, tpb, ncond):
    n, d = x.shape
    pw, hw, aw = (a.shape[1] for a in lat)
    tm = ROW_TILE
    moe = rw is not None
    row = lambda width: pl.BlockSpec((tm, width), lambda t: (t, 0))
    lrow = lambda width: pl.BlockSpec((tm, width), lambda t: (jnp.minimum(t, nlt - 1), 0))
    crow = lambda width: pl.BlockSpec((tm, width), lambda t: (jnp.maximum(t - nlt, 0), 0))
    in_specs = [row(d), lrow(pw), lrow(hw), lrow(aw), crow(pw), crow(hw), crow(aw),
                pl.BlockSpec(w.shape, lambda t: (0, 0)),
                pl.BlockSpec((1, d), lambda t: (0, 0)),
                pl.BlockSpec(mod.shape, lambda t: (0, 0, 0))]
    args = [x, *lat, *ctx, w, g, mod]
    out_specs = [row(d), row(d)]
    out_shape = [jax.ShapeDtypeStruct((n, d), F32), jax.ShapeDtypeStruct((n, d), BF16)]
    if moe:
        in_specs.append(pl.BlockSpec(rw.shape, lambda t: (0, 0)))
        args.append(rw)
        out_specs.append(row(128))
        out_shape.append(jax.ShapeDtypeStruct((n, 128), F32))
    kern = functools.partial(_out_proj_kernel, nlt=nlt, tpb=tpb, ncond=ncond, pw=pw, hw=hw, moe=moe)
    return pl.pallas_call(
        kern, grid=(n // tm,), in_specs=in_specs, out_specs=out_specs, out_shape=out_shape,
        compiler_params=_cparams(("arbitrary",)),
    )(*args)


def _ffn_kernel(x_ref, u_ref, mod_ref, wg_ref, wu_ref, wd_ref, o_ref, *, nlt, tpb, ncond, nchunk):
    ci = _cond_row(pl.program_id(0), nlt, tpb, ncond)
    u = u_ref[...]
    ff = wg_ref.shape[1]
    fc = ff // nchunk
    y = jnp.zeros(x_ref.shape, F32)
    for c in range(nchunk):
        gate = _dot(u, wg_ref[:, c * fc:(c + 1) * fc])
        up = _dot(u, wu_ref[:, c * fc:(c + 1) * fc])
        y = y + _dot((_silu(gate) * up).astype(BF16), wd_ref[c * fc:(c + 1) * fc, :])
    o_ref[...] = x_ref[...] + mod_ref[5, pl.ds(ci, 1), :] * y


def _ffn(x, u, mod, wg, wu, wd, *, nlt, tpb, ncond, out_rows):
    n, d = x.shape
    ff = wg.shape[1]
    tm = ROW_TILE
    nchunk = 2 if (ff // 2) % 128 == 0 else 1
    row = lambda dt: pl.BlockSpec((tm, d), lambda t: (t, 0))
    const = lambda shape: pl.BlockSpec(shape, lambda t: (0,) * len(shape), pipeline_mode=pl.Buffered(1))
    kern = functools.partial(_ffn_kernel, nlt=nlt, tpb=tpb, ncond=ncond, nchunk=nchunk)
    return pl.pallas_call(
        kern,
        grid=(out_rows // tm,),
        in_specs=[row(F32), row(BF16), pl.BlockSpec(mod.shape, lambda t: (0, 0, 0)),
                  const((d, ff)), const((d, ff)), const((ff, d))],
        out_specs=row(F32),
        out_shape=jax.ShapeDtypeStruct((out_rows, d), F32),
        compiler_params=_cparams(("arbitrary",)),
    )(x, u, mod, wg, wu, wd)


def _moe_kernel(te_ref, tv_ref, u_ref, wg_ref, wu_ref, wd_ref, o_ref, acc_ref):
    t = pl.program_id(0)
    f = pl.program_id(1)

    @pl.when(tv_ref[t] > 0)
    def _():
        u = u_ref[...]
        h = (_silu(_dot(u, wg_ref[0, 0])) * _dot(u, wu_ref[0, 0])).astype(BF16)
        y = _dot(h, wd_ref[0, 0])

        @pl.when(f == 0)
        def _():
            acc_ref[...] = y

        @pl.when(f != 0)
        def _():
            acc_ref[...] = acc_ref[...] + y

        @pl.when(f == pl.num_programs(1) - 1)
        def _():
            o_ref[...] = acc_ref[...].astype(BF16)


def _moe_experts(ug, tile_expert, tile_valid, wg, wu, wd, j):
    p, d = ug.shape
    ff = wg.shape[3]
    tm, fc = MOE_TM, MOE_FC
    grid_spec = pltpu.PrefetchScalarGridSpec(
        num_scalar_prefetch=2,
        grid=(p // tm, ff // fc),
        in_specs=[
            pl.BlockSpec((tm, d), lambda t, f, te, tv: (t, 0)),
            pl.BlockSpec((1, 1, d, fc), lambda t, f, te, tv: (j, te[t], 0, f)),
            pl.BlockSpec((1, 1, d, fc), lambda t, f, te, tv: (j, te[t], 0, f)),
            pl.BlockSpec((1, 1, fc, d), lambda t, f, te, tv: (j, te[t], f, 0)),
        ],
        out_specs=pl.BlockSpec((tm, d), lambda t, f, te, tv: (t, 0)),
        scratch_shapes=[pltpu.VMEM((tm, d), F32)],
    )
    return pl.pallas_call(
        _moe_kernel, grid_spec=grid_spec,
        out_shape=jax.ShapeDtypeStruct((p, d), BF16),
        compiler_params=_cparams(("arbitrary", "arbitrary")),
    )(tile_expert, tile_valid, ug, wg, wu, wd)


def _moe_combine_kernel(x_ref, ya_ref, yb_ref, route_ref, mod_ref, o_ref, *, nlt, tpb, ncond):
    ci = _cond_row(pl.program_id(0), nlt, tpb, ncond)
    r = route_ref[...]
    lane = lax.broadcasted_iota(jnp.int32, r.shape, 1)
    g1 = jnp.sum(jnp.where(lane == 2, r, 0.0), axis=-1, keepdims=True)
    g2 = jnp.sum(jnp.where(lane == 3, r, 0.0), axis=-1, keepdims=True)
    y = g1 * ya_ref[...].astype(F32) + g2 * yb_ref[...].astype(F32)
    o_ref[...] = x_ref[...] + mod_ref[5, pl.ds(ci, 1), :] * y


def _moe_combine(x, ya, yb, route, mod, *, nlt, tpb, ncond, out_rows):
    n, d = x.shape
    tm = ROW_TILE
    row = lambda width: pl.BlockSpec((tm, width), lambda t: (t, 0))
    kern = functools.partial(_moe_combine_kernel, nlt=nlt, tpb=tpb, ncond=ncond)
    return pl.pallas_call(
        kern,
        grid=(out_rows // tm,),
        in_specs=[row(d), row(d), row(d), row(128), pl.BlockSpec(mod.shape, lambda t: (0, 0, 0))],
        out_specs=row(d),
        out_shape=jax.ShapeDtypeStruct((out_rows, d), F32),
        compiler_params=_cparams(("arbitrary",)),
    )(x, ya, yb, route, mod)


def _moe_layer(x, u, route, mod, wg, wu, wd, j, *, nlt, tpb, ncond, out_rows):
    n, d = x.shape
    tm = MOE_TM
    experts = jnp.concatenate([route[:, 0], route[:, 1]]).astype(jnp.int32)
    onehot = (experts[:, None] == jnp.arange(N_EXPERTS, dtype=jnp.int32)[None, :]).astype(jnp.int32)
    rank = jnp.sum(onehot * (jnp.cumsum(onehot, axis=0) - 1), axis=1)
    counts = jnp.sum(onehot, axis=0)
    padded = ((counts + tm - 1) // tm) * tm
    ends = jnp.cumsum(padded)
    starts = ends - padded
    dest = starts[experts] + rank
    p = 2 * n + N_EXPERTS * tm
    tokens = jnp.concatenate([jnp.arange(n, dtype=jnp.int32)] * 2)
    inb = dict(mode="promise_in_bounds")
    src = jnp.zeros((p,), jnp.int32).at[dest].set(tokens, unique_indices=True, **inb)
    tile_start = jnp.arange(p // tm, dtype=jnp.int32) * tm
    tile_expert = jnp.minimum(jnp.searchsorted(ends, tile_start, side="right"), N_EXPERTS - 1).astype(jnp.int32)
    tile_valid = (tile_start < ends[-1]).astype(jnp.int32)
    ys = _moe_experts(u.at[src].get(**inb), tile_expert, tile_valid, wg, wu, wd, j)
    ya = ys.at[dest[:n]].get(**inb)
    yb = ys.at[dest[n:]].get(**inb)
    return _moe_combine(x, ya, yb, route, mod, nlt=nlt, tpb=tpb, ncond=ncond, out_rows=out_rows)


def _rope_tables(seq, n_ctx_rows, batch):
    rows = seq // GRID_W
    row = jnp.repeat(jnp.arange(rows, dtype=F32), GRID_W)
    col = jnp.broadcast_to(jnp.arange(GRID_W, dtype=F32), (rows, GRID_W)).reshape(-1)
    inv_freq = jnp.power(ROPE_THETA, -jnp.arange(ROPE_FREQS, dtype=F32) / ROPE_FREQS)
    ar = row[:, None] * inv_freq
    ac = col[:, None] * inv_freq
    cos = jnp.concatenate([jnp.cos(ar), jnp.cos(ar), jnp.cos(ac), jnp.cos(ac)], axis=-1)
    sin = jnp.concatenate([-jnp.sin(ar), jnp.sin(ar), -jnp.sin(ac), jnp.sin(ac)], axis=-1)
    cos = jnp.tile(jnp.concatenate([cos, cos], axis=-1), (batch, 1))
    sin = jnp.tile(jnp.concatenate([sin, sin], axis=-1), (batch, 1))
    cos = jnp.concatenate([cos, jnp.ones((n_ctx_rows, 2 * HEAD_DIM), F32)], axis=0)
    sin = jnp.concatenate([sin, jnp.zeros((n_ctx_rows, 2 * HEAD_DIM), F32)], axis=0)
    return cos, sin


def kernel(x, c, ctx, c_ctx, mod_w, mod_b, norm1_g, norm2_g, w_in, w_out, pool_lin, pool_scale, hy_short_w, hy_short_b, hy_f_w1, hy_f_b1, hy_f_freq1, hy_f_w2, hy_f_b2, hy_f_freq2, hy_f_w3, hy_bias, qk_norm_g, diff_lambda, subln_g, ffn_w_gate, ffn_w_up, ffn_w_down, router_w, moe_w_gate, moe_w_up, moe_w_down):
    batch, seq, d = x.shape
    ctx_len = ctx.shape[1]
    depth = mod_w.shape[0]
    pw = pool_scale.shape[1]
    hw = hy_bias.shape[1]
    nl, nc = batch * seq, batch * ctx_len
    n = nl + nc
    tm = ROW_TILE
    assert seq % tm == 0 and nc % tm == 0 and seq % (FFT_N1 // 2) == 0 and seq % GRID_W == 0
    assert d == HEADS * 2 * V_DIM and pw == hw and batch == 2
    n2 = 2 * seq // FFT_N1
    ncols = n2 * hw
    assert (n * hw) % ncols == 0
    tiles = dict(nlt=nl // tm, tpb=seq // tm, ncond=batch + 1)

    xs = jnp.concatenate([x.reshape(nl, d), ctx.reshape(nc, d)], axis=0)
    cond8 = jnp.concatenate([c, c_ctx[None, :], jnp.zeros((8 - batch - 1, d), F32)], axis=0)
    mods = _modvec(cond8, mod_w, mod_b)
    cos_t, sin_t = _rope_tables(seq, nc, batch)
    consts = _dft_consts(n2)
    small = _dft_small_consts(ctx_len)
    eye = jnp.eye(len(POOL_WINDOWS), dtype=F32)
    moe_wg, moe_wu, moe_wd = moe_w_gate.astype(BF16), moe_w_up.astype(BF16), moe_w_down.astype(BF16)

    for i in range(depth):
        last = i == depth - 1
        lam_init = 0.8 - 0.6 * math.exp(-0.3 * i)
        mod = mods[i]
        zp, zh, za = _in_proj(xs, norm1_g[i][None, :], mod, w_in[i].astype(BF16), pw=pw, hw=hw, **tiles)

        g2 = jnp.concatenate([qk_norm_g[i], qk_norm_g[i]], axis=-1)
        qt, k12, vt = _attn_prep(za, cos_t, sin_t, g2)
        g_b = jnp.broadcast_to(subln_g[i][:, None], (V_DIM, ATT_TQ))
        att_kw = dict(batch=batch, seq=seq, ctx_len=ctx_len, lam_init=lam_init)

        def attend(direct, lam_p=diff_lambda[i], qt=qt, k12=k12, vt=vt, g_b=g_b, att_kw=att_kw):
            return (_attention(lam_p, qt, k12, vt, g_b, latent=True, direct=direct, **att_kw),
                    _attention(lam_p, qt, k12, vt, g_b, latent=False, direct=direct, **att_kw))

        bound = (HEAD_DIM * QSCALE) * jnp.max(jnp.abs(qk_norm_g[i][0])) * jnp.max(jnp.abs(qk_norm_g[i][1]))
        ya_l, ya_c = lax.cond(bound * 1.02 < ATT_DIRECT_MAX, lambda: attend(True), lambda: attend(False))

        lin_bd = (eye[:, None, :, None] * pool_lin[i][:, :, None, :]).reshape(pw, pw).astype(BF16)
        pscale = pool_scale[i][None, :]
        yp_l = _pool(zp, lin_bd, pscale, row0=0, batch=batch, seq=seq, out_rows=nl, out_row0=0)
        yp_c = _pool(zp, lin_bd, pscale, row0=nl, batch=batch, seq=ctx_len, out_rows=nc, out_row0=0)

        sw, sb = hy_short_w[i], hy_short_b[i][None, :]
        u_l, x0_l = _hy_pre(zh, sw, sb, row0=0, batch=batch, seq=seq, out_rows=n, out_row0=0)
        u_c, x0_c = _hy_pre(zh, sw, sb, row0=nl, batch=batch, seq=ctx_len, out_rows=nc, out_row0=0)
        filt = (hy_f_w1[i], hy_f_b1[i], hy_f_freq1[i], hy_f_w2[i], hy_f_b2[i], hy_f_freq2[i], hy_f_w3[i])
        bias = hy_bias[i][None, :]
        taps, asum = _hy_filter(seq, *filt, hw)
        kr1, ki1 = _fft_first(taps.reshape(FFT_N1, ncols), consts["f1_real"], ncols)
        spectrum = _fft_mid(kr1, ki1, consts, n2, hw)
        ar, ai = _fft_first(u_l.reshape(-1, ncols), consts["f1_data"], ncols)
        br, bi = _fft_mid(ar, ai, consts, n2, hw, spectrum=spectrum)
        reps = min(FFT_COLS, ncols) // hw
        yh_lat = _fft_last(br.reshape(FFT_N1, ncols), bi.reshape(FFT_N1, ncols), consts["g1"],
                           u_l.reshape(-1, ncols), x0_l.reshape(-1, ncols),
                           jnp.tile(bias, (1, reps)), jnp.tile(1.0 / asum, (1, reps)), ncols)
        taps_c, asum_c = _hy_filter(ctx_len, *filt, hw)
        yh_ctx = _hy_ctx(u_c, x0_c, taps_c, bias, 1.0 / asum_c, small)

        j = i // 2
        moe = i % 2 == 1
        rw = None
        if moe:
            rw = jnp.concatenate([router_w[j], jnp.zeros((d, 128 - N_EXPERTS), F32)], axis=1)
        outs = _out_proj(xs, (yp_l, yh_lat.reshape(nl, hw), ya_l), (yp_c, yh_ctx, ya_c), w_out[i].astype(BF16),
                         norm2_g[i][None, :], mod, rw, **tiles)
        out_rows = nl if last else n
        if moe:
            xs, u, route = outs
            xs = _moe_layer(xs, u, route, mod, moe_wg, moe_wu, moe_wd, j, out_rows=out_rows, **tiles)
        else:
            xs, u = outs
            xs = _ffn(xs, u, mod, ffn_w_gate[j].astype(BF16), ffn_w_up[j].astype(BF16),
                      ffn_w_down[j].astype(BF16), out_rows=out_rows, **tiles)
    return xs[:nl].reshape(batch, seq, d)
```

```python
import functools
import math

import numpy as np
import jax
import jax.numpy as jnp
from jax import lax
from jax.experimental import pallas as pl
from jax.experimental.pallas import tpu as pltpu

F32 = jnp.float32
BF16 = jnp.bfloat16
EPS = 1e-6

GRID_W = 64
POOL_WINDOWS = (2, 4, 8, 16)
HEADS = 4
HEAD_DIM = 64
V_DIM = 128
ROPE_FREQS = 16
ROPE_THETA = 10000.0
HY_BANDS = 16
HY_TARGET = 1e-2
HY_FAST = 0.3
HY_SLOW = 1.5
N_EXPERTS = 8

ROW_TILE = 512
HALO = 32
FFT_N1 = 128
FFT_COLS = 2048
ATT_TQ = 1024
ATT_TK = 512
ATT_UNROLL = 8
QSCALE = (HEAD_DIM ** -0.5) * math.log2(math.e)
ATT_DIRECT_MAX = 100.0
MOE_TM = 512
MOE_FC = 1792
VMEM_LIMIT = 56 * 1024 * 1024


def _cparams(sem):
    return pltpu.CompilerParams(dimension_semantics=sem, vmem_limit_bytes=VMEM_LIMIT)


def _dot(a, b):
    return jnp.dot(a, b, preferred_element_type=F32)


def _split(a):
    hi = a.astype(BF16)
    lo = (a - hi.astype(F32)).astype(BF16)
    return hi, lo


def _dot3(a, b):
    ah, al = _split(a)
    bh, bl = _split(b)
    return _dot(ah, bh) + _dot(ah, bl) + _dot(al, bh)


def _dot3c(ch, cl, d):
    dh, dl = _split(d)
    return _dot(ch, dh) + _dot(ch, dl) + _dot(cl, dh)


def _silu(x):
    return x / (1.0 + jnp.exp(-x))


def _normmod(x, g, shift, scale):
    ms = jnp.mean(x * x, axis=-1, keepdims=True)
    return (x * lax.rsqrt(ms + EPS) * g) * (1.0 + scale) + shift


def _cond_row(t, n_lat_tiles, tiles_per_batch, n_cond):
    return jnp.where(t >= n_lat_tiles, n_cond - 1, t // tiles_per_batch)


def _cast_kernel(w_ref, o_ref):
    o_ref[...] = w_ref[...].astype(BF16)


def _cast_bf16(w):
    shape = w.shape
    w2 = w.reshape(-1, shape[-1])
    rows, cols = w2.shape
    tile = ROW_TILE
    out = pl.pallas_call(
        _cast_kernel,
        grid=(rows // tile,),
        in_specs=[pl.BlockSpec((tile, cols), lambda t: (t, 0))],
        out_specs=pl.BlockSpec((tile, cols), lambda t: (t, 0)),
        out_shape=jax.ShapeDtypeStruct((rows, cols), BF16),
        compiler_params=_cparams(("arbitrary",)),
    )(w2)
    return out.reshape(shape)


def _modvec_kernel(c_ref, w_ref, b_ref, o_ref):
    s = _silu(c_ref[...])
    o_ref[0, 0] = _dot3(s, w_ref[0]) + b_ref[0]


def _modvec(cond8, mod_w, mod_b):
    depth, d, six_d = mod_w.shape
    nchunk = six_d // d
    return pl.pallas_call(
        _modvec_kernel,
        grid=(depth, nchunk),
        in_specs=[
            pl.BlockSpec((8, d), lambda i, j: (0, 0)),
            pl.BlockSpec((1, d, d), lambda i, j: (i, 0, j)),
            pl.BlockSpec((1, 1, d), lambda i, j: (i, 0, j)),
        ],
        out_specs=pl.BlockSpec((1, 1, 8, d), lambda i, j: (i, j, 0, 0)),
        out_shape=jax.ShapeDtypeStruct((depth, nchunk, 8, d), F32),
        compiler_params=_cparams(("arbitrary", "arbitrary")),
    )(cond8, mod_w, mod_b.reshape(depth, 1, six_d))


def _in_proj_kernel(x_ref, g_ref, mod_ref, w_ref, cos_ref, sin_ref, gqk_ref,
                    zp_ref, zh_ref, qt_ref, k_ref, vt_ref, *, nlt, tpb, ncond, pw, hw):
    ci = _cond_row(pl.program_id(0), nlt, tpb, ncond)
    shift = mod_ref[0, pl.ds(ci, 1), :]
    scale = mod_ref[1, pl.ds(ci, 1), :]
    y = _normmod(x_ref[...], g_ref[...], shift, scale).astype(BF16)
    z = _dot(y, w_ref[...])
    zp_ref[...] = z[:, :pw].astype(BF16)
    zh_ref[...] = z[:, pw:pw + 3 * hw].astype(BF16)

    tm = z.shape[0]
    att = pw + 3 * hw
    lane = lax.broadcasted_iota(jnp.int32, (tm, 2 * HEAD_DIM), 1)
    first = lane < HEAD_DIM
    apart = (lane % (2 * ROPE_FREQS)) < ROPE_FREQS
    cos = cos_ref[...]
    sin = sin_ref[...]
    qk_w = HEADS * 2 * HEAD_DIM

    def norm_rope(v, g):
        v2 = v * v
        s_all = jnp.sum(v2, axis=-1, keepdims=True)
        s_first = jnp.sum(jnp.where(first, v2, 0.0), axis=-1, keepdims=True)
        ms = jnp.where(first, s_first, s_all - s_first) * (1.0 / HEAD_DIM)
        vn = v * lax.rsqrt(ms + EPS) * g
        swapped = jnp.where(apart, pltpu.roll(vn, 2 * HEAD_DIM - ROPE_FREQS, 1), pltpu.roll(vn, ROPE_FREQS, 1))
        return vn * cos + swapped * sin

    for h in range(HEADS):
        lo = att + h * 2 * HEAD_DIM
        q = norm_rope(z[:, lo:lo + 2 * HEAD_DIM], gqk_ref[0:1, :])
        qt_ref[h] = (q * QSCALE).T.astype(BF16)
        k = norm_rope(z[:, qk_w + lo:qk_w + lo + 2 * HEAD_DIM], gqk_ref[1:2, :])
        k_ref[h] = k.astype(BF16)
        vlo = att + 2 * qk_w + h * V_DIM
        vt_ref[h] = z[:, vlo:vlo + V_DIM].T.astype(BF16)


def _in_proj(x, g, mod, w, cos_t, sin_t, gqk, *, nlt, tpb, ncond, pw, hw):
    n, d = x.shape
    wid = w.shape[1]
    tm = ROW_TILE
    kern = functools.partial(_in_proj_kernel, nlt=nlt, tpb=tpb, ncond=ncond, pw=pw, hw=hw)
    return pl.pallas_call(
        kern,
        grid=(n // tm,),
        in_specs=[
            pl.BlockSpec((tm, d), lambda t: (t, 0)),
            pl.BlockSpec((1, d), lambda t: (0, 0)),
            pl.BlockSpec(mod.shape, lambda t: (0, 0, 0)),
            pl.BlockSpec((d, wid), lambda t: (0, 0)),
            pl.BlockSpec((tm, 2 * HEAD_DIM), lambda t: (t, 0)),
            pl.BlockSpec((tm, 2 * HEAD_DIM), lambda t: (t, 0)),
            pl.BlockSpec((2, 2 * HEAD_DIM), lambda t: (0, 0)),
        ],
        out_specs=[
            pl.BlockSpec((tm, pw), lambda t: (t, 0)),
            pl.BlockSpec((tm, 3 * hw), lambda t: (t, 0)),
            pl.BlockSpec((HEADS, 2 * HEAD_DIM, tm), lambda t: (0, 0, t)),
            pl.BlockSpec((HEADS, tm, 2 * HEAD_DIM), lambda t: (0, t, 0)),
            pl.BlockSpec((HEADS, V_DIM, tm), lambda t: (0, 0, t)),
        ],
        out_shape=[
            jax.ShapeDtypeStruct((n, pw), BF16),
            jax.ShapeDtypeStruct((n, 3 * hw), BF16),
            jax.ShapeDtypeStruct((HEADS, 2 * HEAD_DIM, n), BF16),
            jax.ShapeDtypeStruct((HEADS, n, 2 * HEAD_DIM), BF16),
            jax.ShapeDtypeStruct((HEADS, V_DIM, n), BF16),
        ],
        compiler_params=_cparams(("arbitrary",)),
    )(x, g, mod, w, cos_t, sin_t, gqk)


def _attn_kernel(lam_ref, qt_ref, kc_ref, vtc_ref, *rest, tk, n_lat_chunks, lam_init):
    if n_lat_chunks:
        kl_ref, vtl_ref, g_ref, o_ref, m_ref, l_ref, acc_ref = rest
    else:
        g_ref, o_ref, m_ref, l_ref, acc_ref = rest
    qt = qt_ref[0]
    row = lax.broadcasted_iota(jnp.int32, qt.shape, 0)
    zero = jnp.zeros_like(qt)
    qmaps = (jnp.where(row < HEAD_DIM, qt, zero), jnp.where(row >= HEAD_DIM, qt, zero))

    def process(k_tile, vt_tile, first):
        for mi in range(2):
            s = _dot(k_tile, qmaps[mi])
            smax = jnp.max(s, axis=0, keepdims=True)
            if first:
                m_new = smax
            else:
                m_old = m_ref[mi]
                m_new = jnp.maximum(m_old, smax)
            p = jnp.exp2(s - m_new)
            psum = jnp.sum(p, axis=0, keepdims=True)
            pv = _dot(vt_tile, p.astype(BF16))
            if first:
                l_ref[mi] = psum
                acc_ref[mi] = pv
            else:
                alpha = jnp.exp2(m_old - m_new)
                l_ref[mi] = alpha * l_ref[mi] + psum
                acc_ref[mi] = alpha * acc_ref[mi] + pv
            m_ref[mi] = m_new

    process(kc_ref[0], vtc_ref[0], True)
    if n_lat_chunks:
        def body(j, carry):
            off = pl.multiple_of(j * tk, tk)
            process(kl_ref[0, pl.ds(off, tk), :], vtl_ref[0, :, pl.ds(off, tk)], False)
            return carry
        lax.fori_loop(0, n_lat_chunks, body, 0)

    lv = lam_ref[...]
    lam = (jnp.exp(jnp.sum(lv[0:1] * lv[1:2], axis=-1, keepdims=True))
           - jnp.exp(jnp.sum(lv[2:3] * lv[3:4], axis=-1, keepdims=True)) + lam_init)
    o = acc_ref[0] * (1.0 / l_ref[0]) - lam * (acc_ref[1] * (1.0 / l_ref[1]))
    ms = jnp.mean(o * o, axis=0, keepdims=True)
    y = o * lax.rsqrt(ms + EPS) * g_ref[...] * (1.0 - lam_init)
    o_ref[...] = y.T.astype(BF16)


def _attn_direct_kernel(lam_ref, qt_ref, kc_ref, vtc_ref, *rest, tk, n_lat_chunks, lam_init):
    if n_lat_chunks:
        kl_ref, vtl_ref, g_ref, o_ref, s_ref, p_ref, l_ref, acc_ref = rest
    else:
        g_ref, o_ref, s_ref, p_ref, l_ref, acc_ref = rest
    qt = qt_ref[0]
    tq = qt.shape[1]
    row = lax.broadcasted_iota(jnp.int32, qt.shape, 0)
    zero = jnp.zeros_like(qt)
    qmaps = (jnp.where(row < HEAD_DIM, qt, zero), jnp.where(row >= HEAD_DIM, qt, zero))

    def piece_rows(j):
        if isinstance(j, int):
            return pl.ds((j - 1) * tk, tk)
        return pl.ds(pl.multiple_of((j - 1) * tk, tk), tk)

    def k_piece(j):
        return kl_ref[0, piece_rows(j), :]

    def vt_piece(j):
        return vtc_ref[0] if isinstance(j, int) and j == 0 else vtl_ref[0, :, piece_rows(j)]

    def step(j, par):
        stage_a(k_piece(j + 1), 1 - par)
        stage_b(par)
        stage_c(vt_piece(j - 1), 1 - par)

    def stage_a(k_tile, slot):
        for mi in range(2):
            s_ref[slot, mi] = _dot(k_tile, qmaps[mi])

    def stage_b(slot):
        for mi in range(2):
            p = jnp.exp2(s_ref[slot, mi])
            l_ref[mi] = l_ref[mi] + jnp.sum(p.reshape(tk // 8, 8, tq), axis=0)
            p_ref[slot, mi] = p.astype(BF16)

    def stage_c(vt_tile, slot):
        for mi in range(2):
            acc_ref[mi] = acc_ref[mi] + _dot(vt_tile, p_ref[slot, mi])

    l_ref[...] = jnp.zeros_like(l_ref)
    acc_ref[...] = jnp.zeros_like(acc_ref)
    stage_a(kc_ref[0], 0)
    if n_lat_chunks == 0:
        stage_b(0)
        stage_c(vtc_ref[0], 0)
    else:
        stage_a(k_piece(1), 1)
        stage_b(0)
        for j in range(1, ATT_UNROLL):
            step(j, j % 2)

        def body(jj, carry):
            j = ATT_UNROLL * (jj + 1)
            for r in range(ATT_UNROLL):
                step(j + r, r % 2)
            return carry

        lax.fori_loop(0, n_lat_chunks // ATT_UNROLL - 1, body, 0)
        stage_b(0)
        stage_c(vt_piece(n_lat_chunks - 1), 1)
        stage_c(vt_piece(n_lat_chunks), 0)

    lv = lam_ref[...]
    lam = (jnp.exp(jnp.sum(lv[0:1] * lv[1:2], axis=-1, keepdims=True))
           - jnp.exp(jnp.sum(lv[2:3] * lv[3:4], axis=-1, keepdims=True)) + lam_init)
    l1 = jnp.sum(l_ref[0], axis=0, keepdims=True)
    l2 = jnp.sum(l_ref[1], axis=0, keepdims=True)
    o = acc_ref[0] * (1.0 / l1) - lam * (acc_ref[1] * (1.0 / l2))
    ms = jnp.mean(o * o, axis=0, keepdims=True)
    y = o * lax.rsqrt(ms + EPS) * g_ref[...] * (1.0 - lam_init)
    o_ref[...] = y.T.astype(BF16)


def _attention(lam_p, qt, k12, vt, g_b, *, batch, seq, ctx_len, latent, lam_init, direct):
    n = k12.shape[1]
    nl = batch * seq
    tq = ATT_TQ if latent else ctx_len
    nq = seq // tq if latent else 1
    tk = ctx_len if direct else ATT_TK
    assert seq % ((ATT_UNROLL if direct else 1) * tk) == 0
    qbase = 0 if latent else nl // tq

    def qrow(b, iq):
        return (b * nq + iq) if latent else (qbase + b)

    in_specs = [
        pl.BlockSpec(lam_p.shape, lambda b, h, iq: (0, 0)),
        pl.BlockSpec((1, 2 * HEAD_DIM, tq), lambda b, h, iq: (h, 0, qrow(b, iq))),
        pl.BlockSpec((1, ctx_len, 2 * HEAD_DIM), lambda b, h, iq: (h, nl // ctx_len + b, 0)),
        pl.BlockSpec((1, V_DIM, ctx_len), lambda b, h, iq: (h, 0, nl // ctx_len + b)),
    ]
    args = [lam_p, qt, k12, vt]
    if latent:
        in_specs += [
            pl.BlockSpec((1, seq, 2 * HEAD_DIM), lambda b, h, iq: (h, b, 0)),
            pl.BlockSpec((1, V_DIM, seq), lambda b, h, iq: (h, 0, b)),
        ]
        args += [k12, vt]
    in_specs.append(pl.BlockSpec((V_DIM, tq), lambda b, h, iq: (0, 0)))
    args.append(g_b[:, :tq])
    nrows = nl if latent else batch * ctx_len
    body = _attn_direct_kernel if direct else _attn_kernel
    kern = functools.partial(body, tk=tk, n_lat_chunks=(seq // tk if latent else 0), lam_init=lam_init)
    if direct:
        scratch = [pltpu.VMEM((2, 2, tk, tq), F32), pltpu.VMEM((2, 2, tk, tq), BF16),
                   pltpu.VMEM((2, 8, tq), F32), pltpu.VMEM((2, V_DIM, tq), F32)]
    else:
        scratch = [pltpu.VMEM((2, 1, tq), F32), pltpu.VMEM((2, 1, tq), F32), pltpu.VMEM((2, V_DIM, tq), F32)]
    return pl.pallas_call(
        kern,
        grid=(batch, HEADS, nq),
        in_specs=in_specs,
        out_specs=pl.BlockSpec((tq, V_DIM), lambda b, h, iq: (b * nq + iq, h)),
        out_shape=jax.ShapeDtypeStruct((nrows, HEADS * V_DIM), BF16),
        scratch_shapes=scratch,
        compiler_params=_cparams(("arbitrary", "arbitrary", "arbitrary")),
    )(*args)


def _halo_specs(width, *, row0, seq, tile):
    hb = tile // HALO

    def cur(b, i):
        return ((row0 + b * seq) // tile + i, 0)

    def prev(b, i):
        first = (row0 + b * seq) // HALO
        return (jnp.maximum(first + i * hb - 1, first), 0)

    def nxt(b, i):
        first = (row0 + b * seq) // HALO
        return (jnp.minimum(first + (i + 1) * hb, first + seq // HALO - 1), 0)

    return [pl.BlockSpec((HALO, width), prev), pl.BlockSpec((tile, width), cur), pl.BlockSpec((HALO, width), nxt)]


def _fill_ext(ext_ref, prev_ref, cur_ref, next_ref, *, seq, tile):
    pos0 = pl.program_id(1) * tile
    width = cur_ref.shape[1]
    hpos = lax.broadcasted_iota(jnp.int32, (HALO, width), 0)
    ext_ref[0:HALO, :] = jnp.where(pos0 - HALO + hpos >= 0, prev_ref[...].astype(F32), 0.0)
    ext_ref[HALO:HALO + tile, :] = cur_ref[...].astype(F32)
    ext_ref[HALO + tile:, :] = jnp.where(pos0 + tile + hpos < seq, next_ref[...].astype(F32), 0.0)


def _pool_kernel(prev_ref, cur_ref, next_ref, lin_ref, scale_ref, o_ref, ext_ref, s_ref, *, seq, tile):
    _fill_ext(ext_ref, prev_ref, cur_ref, next_ref, seq=seq, tile=tile)
    width = cur_ref.shape[1]
    gd = width // len(POOL_WINDOWS)
    n0 = tile + 2 * HALO
    s_ref[0, 0:n0 - 8, :] = ext_ref[0:n0 - 8, :] + ext_ref[1:n0 - 7, :]
    for k in range(1, len(POOL_WINDOWS)):
        step = 1 << k
        ln = n0 - 8 * (k + 1)
        s_ref[k, 0:ln, :] = s_ref[k - 1, 0:ln, :] + s_ref[k - 1, step:step + ln, :]
    lane = lax.broadcasted_iota(jnp.int32, (tile, width), 1)
    pos = pl.program_id(1) * tile + lax.broadcasted_iota(jnp.int32, (tile, width), 0)
    grp = lane // gd
    wsum = jnp.zeros((tile, width), F32)
    half = jnp.zeros((tile, width), jnp.int32)
    for k, win in enumerate(POOL_WINDOWS):
        start = HALO - win // 2
        wsum = jnp.where(grp == k, s_ref[k, start:start + tile, :], wsum)
        half = jnp.where(grp == k, win // 2, half)
    cnt = jnp.minimum(pos + half, seq) - jnp.maximum(pos - half, 0)
    z = ext_ref[HALO:HALO + tile, :]
    dlt = (wsum / cnt.astype(F32) - z).astype(BF16)
    o_ref[...] = (_dot(dlt, lin_ref[...]) * scale_ref[...]).astype(BF16)


def _pool(zp, lin_bd, scale, *, row0, batch, seq, out_rows, out_row0):
    width = zp.shape[1]
    tile = min(ROW_TILE, seq)
    kern = functools.partial(_pool_kernel, seq=seq, tile=tile)
    return pl.pallas_call(
        kern,
        grid=(batch, seq // tile),
        in_specs=_halo_specs(width, row0=row0, seq=seq, tile=tile) + [
            pl.BlockSpec((width, width), lambda b, i: (0, 0)),
            pl.BlockSpec((1, width), lambda b, i: (0, 0)),
        ],
        out_specs=pl.BlockSpec((tile, width), lambda b, i: ((out_row0 + b * seq) // tile + i, 0)),
        out_shape=jax.ShapeDtypeStruct((out_rows, width), BF16),
        scratch_shapes=[
            pltpu.VMEM((tile + 2 * HALO, width), F32),
            pltpu.VMEM((4, tile + 2 * HALO, width), F32),
        ],
        compiler_params=_cparams(("arbitrary", "arbitrary")),
    )(zp, zp, zp, lin_bd, scale)


def _hy_pre_kernel(prev_ref, cur_ref, next_ref, w_ref, b_ref, u_ref, x0_ref, ext_ref, *, seq, tile, hw):
    _fill_ext(ext_ref, prev_ref, cur_ref, next_ref, seq=seq, tile=tile)
    y = b_ref[...] + ext_ref[HALO - 1:HALO - 1 + tile, :] * w_ref[0:1, :]
    y = y + ext_ref[HALO:HALO + tile, :] * w_ref[1:2, :]
    y = y + ext_ref[HALO + 1:HALO + 1 + tile, :] * w_ref[2:3, :]
    x0_ref[...] = y[:, :hw]
    u_ref[...] = y[:, 2 * hw:] * y[:, hw:2 * hw]


def _hy_pre(zh, sw, sb, *, row0, batch, seq, out_rows, out_row0):
    width = zh.shape[1]
    hw = width // 3
    tile = min(ROW_TILE, seq)
    kern = functools.partial(_hy_pre_kernel, seq=seq, tile=tile, hw=hw)
    ospec = pl.BlockSpec((tile, hw), lambda b, i: ((out_row0 + b * seq) // tile + i, 0))
    return pl.pallas_call(
        kern,
        grid=(batch, seq // tile),
        in_specs=_halo_specs(width, row0=row0, seq=seq, tile=tile) + [
            pl.BlockSpec((3, width), lambda b, i: (0, 0)),
            pl.BlockSpec((1, width), lambda b, i: (0, 0)),
        ],
        out_specs=[ospec, ospec],
        out_shape=[jax.ShapeDtypeStruct((out_rows, hw), F32), jax.ShapeDtypeStruct((out_rows, hw), F32)],
        scratch_shapes=[pltpu.VMEM((tile + 2 * HALO, width), F32)],
        compiler_params=_cparams(("arbitrary", "arbitrary")),
    )(zh, zh, zh, sw, sb)


def _filter_kernel(feat_ref, w1_ref, b1_ref, f1_ref, w2_ref, b2_ref, f2_ref, w3_ref, dl_ref,
                   taps_ref, asum_ref, *, tile, hw, seq):
    feat = feat_ref[...]
    h = jnp.sin(f1_ref[...] * (_dot3(feat, w1_ref[...]) + b1_ref[...]))
    h = jnp.sin(f2_ref[...] * (_dot3(h, w2_ref[...]) + b2_ref[...]))
    h = _dot3(h, w3_ref[...])
    dec = jnp.exp(-feat[:, 0:1] * dl_ref[...])
    pos = pl.program_id(0) * tile + lax.broadcasted_iota(jnp.int32, (tile, hw), 0)
    taps = jnp.where(pos == seq, 0.0, h * dec)
    taps_ref[...] = taps
    part = jnp.sum(jnp.abs(taps), axis=0, keepdims=True)

    @pl.when(pl.program_id(0) == 0)
    def _():
        asum_ref[...] = part

    @pl.when(pl.program_id(0) != 0)
    def _():
        asum_ref[...] = asum_ref[...] + part


def _hy_filter(seq, w1, b1, f1, w2, b2, f2, w3, hw):
    t = jnp.linspace(0.0, 1.0, seq, dtype=F32)[:, None]
    w = (2.0 * math.pi / seq) * jnp.arange(seq, dtype=F32)[:, None]
    bands = jnp.linspace(1e-4, HY_BANDS - 1, HY_BANDS, dtype=F32)[None, :]
    emb = 1 + 2 * HY_BANDS
    embp = ((emb + 7) // 8) * 8
    feat = jnp.concatenate([t, jnp.cos(bands * w), -jnp.sin(bands * w), jnp.zeros((seq, embp - emb), F32)], axis=-1)
    w1p = jnp.concatenate([w1, jnp.zeros((embp - emb, w1.shape[1]), F32)], axis=0)
    max_decay = math.log(1.0 / HY_TARGET) / HY_FAST
    min_decay = math.log(1.0 / HY_TARGET) / HY_SLOW
    deltas = jnp.linspace(min_decay, max_decay, hw, dtype=F32)[None, :]
    feat = jnp.concatenate([feat, feat[0:1], jnp.flip(feat[1:], axis=0)], axis=0)
    tile = min(ROW_TILE, seq)
    nfwd = seq // tile
    hid = w1.shape[1]
    full = lambda shape: pl.BlockSpec(shape, lambda i: (0,) * len(shape))
    return pl.pallas_call(
        functools.partial(_filter_kernel, tile=tile, hw=hw, seq=seq),
        grid=(2 * nfwd,),
        in_specs=[
            pl.BlockSpec((tile, embp), lambda i: (i, 0)),
            full((embp, hid)), full((1, hid)), full((1, hid)),
            full((hid, hid)), full((1, hid)), full((1, hid)),
            pl.BlockSpec((hid, hw), lambda i: (0, i // nfwd)), full((1, hw)),
        ],
        out_specs=[
            pl.BlockSpec((tile, hw), lambda i: (i, 0)),
            pl.BlockSpec((1, hw), lambda i: (0, 0)),
        ],
        out_shape=[
            jax.ShapeDtypeStruct((2 * seq, hw), F32),
            jax.ShapeDtypeStruct((1, hw), F32),
        ],
        compiler_params=_cparams(("arbitrary",)),
    )(feat, w1p, b1[None, :], f1[None, :], w2, b2[None, :], f2[None, :], w3, deltas)


def _np_split(a):
    a32 = jnp.asarray(a, F32)
    hi = a32.astype(BF16)
    lo = (a32 - hi.astype(F32)).astype(BF16)
    return hi, lo


def _dft_consts(n2):
    n1 = FFT_N1
    n = n1 * n2
    half = n1 // 2
    a1 = -2.0 * np.pi * np.outer(np.arange(n1), np.arange(n1)) / n1
    f1r, f1i = np.cos(a1), np.sin(a1)
    f1_data = np.block([[f1r[:, :half], -f1i[:, :half]], [f1i[:, :half], f1r[:, :half]]])
    f1_real = np.concatenate([f1r, f1i], axis=0)
    g1r, g1i = f1r / n, -f1i / n
    g1 = np.block([[g1r[:half], -g1i[:half]], [g1i[:half], g1r[:half]]])
    a2 = -2.0 * np.pi * np.outer(np.arange(n2), np.arange(n2)) / n2
    f2r, f2i = np.cos(a2), np.sin(a2)
    f2 = np.block([[f2r, -f2i], [f2i, f2r]])
    g2 = np.block([[f2r, f2i], [-f2i, f2r]])
    at = -2.0 * np.pi * np.outer(np.arange(n1), np.arange(n2)) / n
    twr = jnp.broadcast_to(jnp.asarray(np.cos(at), F32)[:, :, None], (n1, n2, 128))
    twi = jnp.broadcast_to(jnp.asarray(np.sin(at), F32)[:, :, None], (n1, n2, 128))
    return dict(f1_data=_np_split(f1_data), f1_real=_np_split(f1_real), g1=_np_split(g1),
                f2=_np_split(f2), g2=_np_split(g2), twr=twr, twi=twi)


def _fft1_kernel(z_ref, fh_ref, fl_ref, ar_ref, ai_ref):
    a = _dot3c(fh_ref[...], fl_ref[...], z_ref[...])
    ar_ref[...] = a[:FFT_N1]
    ai_ref[...] = a[FFT_N1:]


def _fft_first(zview, fmat, ncols):
    cb = min(FFT_COLS, ncols)
    fh, fl = fmat
    cspec = pl.BlockSpec(fh.shape, lambda j: (0, 0))
    ospec = pl.BlockSpec((FFT_N1, cb), lambda j: (0, j))
    return pl.pallas_call(
        _fft1_kernel,
        grid=(ncols // cb,),
        in_specs=[pl.BlockSpec((FFT_N1, cb), lambda j: (0, j)), cspec, cspec],
        out_specs=[ospec, ospec],
        out_shape=[jax.ShapeDtypeStruct((FFT_N1, ncols), F32)] * 2,
        compiler_params=_cparams(("arbitrary",)),
    )(zview, fh, fl)


def _fftmid_kernel(ar_ref, ai_ref, twr_ref, twi_ref, f2h_ref, f2l_ref, *rest, filter_only, n2):
    width = ar_ref.shape[2]
    reps = width // 128
    twr = jnp.concatenate([twr_ref[0]] * reps, axis=1)
    twi = jnp.concatenate([twi_ref[0]] * reps, axis=1)
    ar, ai = ar_ref[0], ai_ref[0]
    z = jnp.concatenate([ar * twr - ai * twi, ar * twi + ai * twr], axis=0)
    x = _dot3c(f2h_ref[...], f2l_ref[...], z)
    xr, xi = x[:n2], x[n2:]
    if filter_only:
        kr_out, ki_out = rest
        kr_out[0] = xr
        ki_out[0] = xi
        return
    kr_ref, ki_ref, g2h_ref, g2l_ref, br_out, bi_out = rest
    kr, ki = kr_ref[0], ki_ref[0]
    y = jnp.concatenate([xr * kr - xi * ki, xr * ki + xi * kr], axis=0)
    w = _dot3c(g2h_ref[...], g2l_ref[...], y)
    wr, wi = w[:n2], w[n2:]
    br_out[0] = wr * twr + wi * twi
    bi_out[0] = wi * twr - wr * twi


def _fft_mid(ar, ai, consts, n2, width, spectrum=None):
    a3r = ar.reshape(FFT_N1, n2, width)
    a3i = ai.reshape(FFT_N1, n2, width)
    blk = pl.BlockSpec((1, n2, width), lambda k: (k, 0, 0))
    twspec = pl.BlockSpec((1, n2, 128), lambda k: (k, 0, 0))
    cspec = pl.BlockSpec((2 * n2, 2 * n2), lambda k: (0, 0))
    in_specs = [blk, blk, twspec, twspec, cspec, cspec]
    args = [a3r, a3i, consts["twr"], consts["twi"], *consts["f2"]]
    if spectrum is not None:
        in_specs += [blk, blk, cspec, cspec]
        args += [spectrum[0], spectrum[1], *consts["g2"]]
    return pl.pallas_call(
        functools.partial(_fftmid_kernel, filter_only=spectrum is None, n2=n2),
        grid=(FFT_N1,),
        in_specs=in_specs,
        out_specs=[blk, blk],
        out_shape=[jax.ShapeDtypeStruct((FFT_N1, n2, width), F32)] * 2,
        compiler_params=_cparams(("arbitrary",)),
    )(*args)


def _fftlast_kernel(br_ref, bi_ref, gh_ref, gl_ref, u_ref, x0_ref, bias_ref, invn_ref, o_ref):
    b = jnp.concatenate([br_ref[...], bi_ref[...]], axis=0)
    y = _dot3c(gh_ref[...], gl_ref[...], b)
    o_ref[...] = ((y * invn_ref[...] + u_ref[...] * bias_ref[...]) * x0_ref[...]).astype(BF16)


def _fft_last(br, bi, gmat, uview, x0view, bias_t, invn_t, ncols):
    cb = min(FFT_COLS, ncols)
    gh, gl = gmat
    cspec = pl.BlockSpec(gh.shape, lambda j: (0, 0))
    dspec = pl.BlockSpec((FFT_N1, cb), lambda j: (0, j))
    vspec = pl.BlockSpec((1, cb), lambda j: (0, 0))
    return pl.pallas_call(
        _fftlast_kernel,
        grid=(ncols // cb,),
        in_specs=[dspec, dspec, cspec, cspec, dspec, dspec, vspec, vspec],
        out_specs=dspec,
        out_shape=jax.ShapeDtypeStruct((FFT_N1, ncols), BF16),
        compiler_params=_cparams(("arbitrary",)),
    )(br, bi, gh, gl, uview, x0view, bias_t, invn_t)


def _dft_small_consts(seq):
    n = 2 * seq
    a = -2.0 * np.pi * np.outer(np.arange(n), np.arange(n)) / n
    fr, fi = np.cos(a), np.sin(a)
    f_data = np.block([[fr[:, :seq], -fi[:, :seq]], [fi[:, :seq], fr[:, :seq]]])
    f_real = np.concatenate([fr, fi], axis=0)
    gr, gi = fr / n, -fi / n
    g = np.block([[gr[:seq], -gi[:seq]], [gi[:seq], gr[:seq]]])
    return _np_split(f_data), _np_split(f_real), _np_split(g)


def _hy_ctx_kernel(u_ref, x0_ref, taps_ref, fdh_ref, fdl_ref, frh_ref, frl_ref, gh_ref, gl_ref,
                   bias_ref, invn_ref, o_ref):
    n = taps_ref.shape[0]
    z = u_ref[...]
    a = _dot3c(fdh_ref[...], fdl_ref[...], z)
    k = _dot3c(frh_ref[...], frl_ref[...], taps_ref[...])
    ar, ai, kr, ki = a[:n], a[n:], k[:n], k[n:]
    y = jnp.concatenate([ar * kr - ai * ki, ar * ki + ai * kr], axis=0)
    conv = _dot3c(gh_ref[...], gl_ref[...], y)
    o_ref[...] = ((conv * invn_ref[...] + z * bias_ref[...]) * x0_ref[...]).astype(BF16)


def _hy_ctx(u, x0, taps, bias, invn, small):
    rows, hw = u.shape
    full = lambda a: pl.BlockSpec(a.shape, lambda i: (0,) * a.ndim)
    args = [u, x0, taps, *small[0], *small[1], *small[2], bias, invn]
    return pl.pallas_call(
        _hy_ctx_kernel,
        grid=(1,),
        in_specs=[full(a) for a in args],
        out_specs=pl.BlockSpec((rows, hw), lambda i: (0, 0)),
        out_shape=jax.ShapeDtypeStruct((rows, hw), BF16),
        compiler_params=_cparams(("arbitrary",)),
    )(*args)


def _out_proj_kernel(x_ref, ypl_ref, yhl_ref, yal_ref, ypc_ref, yhc_ref, yac_ref, w_ref, g_ref, mod_ref, *rest,
                     nlt, tpb, ncond, pw, hw, moe):
    if moe:
        rw_ref, xo_ref, u_ref, route_ref = rest
    else:
        xo_ref, u_ref = rest
    ci = _cond_row(pl.program_id(0), nlt, tpb, ncond)
    is_ctx = pl.program_id(0) >= nlt
    yp = jnp.where(is_ctx, ypc_ref[...], ypl_ref[...])
    yh = jnp.where(is_ctx, yhc_ref[...], yhl_ref[...])
    ya = jnp.where(is_ctx, yac_ref[...], yal_ref[...])
    mix = _dot(yp, w_ref[0:pw, :]) + _dot(yh, w_ref[pw:pw + hw, :]) + _dot(ya, w_ref[pw + hw:, :])
    x = x_ref[...] + mod_ref[2, pl.ds(ci, 1), :] * mix
    xo_ref[...] = x
    un = _normmod(x, g_ref[...], mod_ref[3, pl.ds(ci, 1), :], mod_ref[4, pl.ds(ci, 1), :])
    u_ref[...] = un.astype(BF16)
    if moe:
        logits = _dot3(un, rw_ref[...])
        lane = lax.broadcasted_iota(jnp.int32, logits.shape, 1)
        neg = jnp.float32(-jnp.inf)
        lg = jnp.where(lane < N_EXPERTS, logits, neg)
        t1 = jnp.max(lg, axis=-1, keepdims=True)
        i1 = jnp.min(jnp.where(lg == t1, lane, 128), axis=-1, keepdims=True)
        lg2 = jnp.where(lane == i1, neg, lg)
        t2 = jnp.max(lg2, axis=-1, keepdims=True)
        i2 = jnp.min(jnp.where(lg2 == t2, lane, 128), axis=-1, keepdims=True)
        e2 = jnp.exp(t2 - t1)
        g1 = 1.0 / (1.0 + e2)
        g2 = e2 / (1.0 + e2)
        route_ref[...] = jnp.where(lane == 0, i1.astype(F32), jnp.where(lane == 1, i2.astype(F32),
                                   jnp.where(lane == 2, g1, jnp.where(lane == 3, g2, 0.0))))


def _out_proj(x, lat, ctx, w, g, mod, rw, *, nlt, tpb, ncond):
    n, d = x.shape
    pw, hw, aw = (a.shape[1] for a in lat)
    tm = ROW_TILE
    moe = rw is not None
    row = lambda width: pl.BlockSpec((tm, width), lambda t: (t, 0))
    lrow = lambda width: pl.BlockSpec((tm, width), lambda t: (jnp.minimum(t, nlt - 1), 0))
    crow = lambda width: pl.BlockSpec((tm, width), lambda t: (jnp.maximum(t - nlt, 0), 0))
    in_specs = [row(d), lrow(pw), lrow(hw), lrow(aw), crow(pw), crow(hw), crow(aw),
                pl.BlockSpec(w.shape, lambda t: (0, 0)),
                pl.BlockSpec((1, d), lambda t: (0, 0)),
                pl.BlockSpec(mod.shape, lambda t: (0, 0, 0))]
    args = [x, *lat, *ctx, w, g, mod]
    out_specs = [row(d), row(d)]
    out_shape = [jax.ShapeDtypeStruct((n, d), F32), jax.ShapeDtypeStruct((n, d), BF16)]
    if moe:
        in_specs.append(pl.BlockSpec(rw.shape, lambda t: (0, 0)))
        args.append(rw)
        out_specs.append(row(128))
        out_shape.append(jax.ShapeDtypeStruct((n, 128), F32))
    kern = functools.partial(_out_proj_kernel, nlt=nlt, tpb=tpb, ncond=ncond, pw=pw, hw=hw, moe=moe)
    return pl.pallas_call(
        kern, grid=(n // tm,), in_specs=in_specs, out_specs=out_specs, out_shape=out_shape,
        compiler_params=_cparams(("arbitrary",)),
    )(*args)


def _ffn_kernel(x_ref, u_ref, mod_ref, wg_ref, wu_ref, wd_ref, o_ref, *, nlt, tpb, ncond, nchunk):
    ci = _cond_row(pl.program_id(0), nlt, tpb, ncond)
    u = u_ref[...]
    ff = wg_ref.shape[1]
    fc = ff // nchunk
    y = jnp.zeros(x_ref.shape, F32)
    for c in range(nchunk):
        gate = _dot(u, wg_ref[:, c * fc:(c + 1) * fc])
        up = _dot(u, wu_ref[:, c * fc:(c + 1) * fc])
        y = y + _dot((_silu(gate) * up).astype(BF16), wd_ref[c * fc:(c + 1) * fc, :])
    o_ref[...] = x_ref[...] + mod_ref[5, pl.ds(ci, 1), :] * y


def _ffn(x, u, mod, wg, wu, wd, *, nlt, tpb, ncond, out_rows):
    n, d = x.shape
    ff = wg.shape[1]
    tm = ROW_TILE
    nchunk = 2 if (ff // 2) % 128 == 0 else 1
    row = lambda dt: pl.BlockSpec((tm, d), lambda t: (t, 0))
    const = lambda shape: pl.BlockSpec(shape, lambda t: (0,) * len(shape), pipeline_mode=pl.Buffered(1))
    kern = functools.partial(_ffn_kernel, nlt=nlt, tpb=tpb, ncond=ncond, nchunk=nchunk)
    return pl.pallas_call(
        kern,
        grid=(out_rows // tm,),
        in_specs=[row(F32), row(BF16), pl.BlockSpec(mod.shape, lambda t: (0, 0, 0)),
                  const((d, ff)), const((d, ff)), const((ff, d))],
        out_specs=row(F32),
        out_shape=jax.ShapeDtypeStruct((out_rows, d), F32),
        compiler_params=_cparams(("arbitrary",)),
    )(x, u, mod, wg, wu, wd)


def _moe_kernel(te_ref, tv_ref, u_ref, wg_ref, wu_ref, wd_ref, o_ref, acc_ref):
    t = pl.program_id(0)
    f = pl.program_id(1)

    @pl.when(tv_ref[t] > 0)
    def _():
        u = u_ref[...]
        h = (_silu(_dot(u, wg_ref[0, 0])) * _dot(u, wu_ref[0, 0])).astype(BF16)
        y = _dot(h, wd_ref[0, 0])

        @pl.when(f == 0)
        def _():
            acc_ref[...] = y

        @pl.when(f != 0)
        def _():
            acc_ref[...] = acc_ref[...] + y

        @pl.when(f == pl.num_programs(1) - 1)
        def _():
            o_ref[...] = acc_ref[...].astype(BF16)


def _moe_experts(ug, tile_expert, tile_valid, wg, wu, wd, j):
    p, d = ug.shape
    ff = wg.shape[3]
    tm, fc = MOE_TM, MOE_FC
    grid_spec = pltpu.PrefetchScalarGridSpec(
        num_scalar_prefetch=2,
        grid=(p // tm, ff // fc),
        in_specs=[
            pl.BlockSpec((tm, d), lambda t, f, te, tv: (t, 0)),
            pl.BlockSpec((1, 1, d, fc), lambda t, f, te, tv: (j, te[t], 0, f)),
            pl.BlockSpec((1, 1, d, fc), lambda t, f, te, tv: (j, te[t], 0, f)),
            pl.BlockSpec((1, 1, fc, d), lambda t, f, te, tv: (j, te[t], f, 0)),
        ],
        out_specs=pl.BlockSpec((tm, d), lambda t, f, te, tv: (t, 0)),
        scratch_shapes=[pltpu.VMEM((tm, d), F32)],
    )
    return pl.pallas_call(
        _moe_kernel, grid_spec=grid_spec,
        out_shape=jax.ShapeDtypeStruct((p, d), BF16),
        compiler_params=_cparams(("arbitrary", "arbitrary")),
    )(tile_expert, tile_valid, ug, wg, wu, wd)


def _moe_combine_kernel(x_ref, ya_ref, yb_ref, route_ref, mod_ref, o_ref, *, nlt, tpb, ncond):
    ci = _cond_row(pl.program_id(0), nlt, tpb, ncond)
    r = route_ref[...]
    lane = lax.broadcasted_iota(jnp.int32, r.shape, 1)
    g1 = jnp.sum(jnp.where(lane == 2, r, 0.0), axis=-1, keepdims=True)
    g2 = jnp.sum(jnp.where(lane == 3, r, 0.0), axis=-1, keepdims=True)
    y = g1 * ya_ref[...].astype(F32) + g2 * yb_ref[...].astype(F32)
    o_ref[...] = x_ref[...] + mod_ref[5, pl.ds(ci, 1), :] * y


def _moe_combine(x, ya, yb, route, mod, *, nlt, tpb, ncond, out_rows):
    n, d = x.shape
    tm = ROW_TILE
    row = lambda width: pl.BlockSpec((tm, width), lambda t: (t, 0))
    kern = functools.partial(_moe_combine_kernel, nlt=nlt, tpb=tpb, ncond=ncond)
    return pl.pallas_call(
        kern,
        grid=(out_rows // tm,),
        in_specs=[row(d), row(d), row(d), row(128), pl.BlockSpec(mod.shape, lambda t: (0, 0, 0))],
        out_specs=row(d),
        out_shape=jax.ShapeDtypeStruct((out_rows, d), F32),
        compiler_params=_cparams(("arbitrary",)),
    )(x, ya, yb, route, mod)


def _moe_layer(x, u, route, mod, wg, wu, wd, j, *, nlt, tpb, ncond, out_rows):
    n, d = x.shape
    tm = MOE_TM
    experts = jnp.concatenate([route[:, 0], route[:, 1]]).astype(jnp.int32)
    onehot = (experts[:, None] == jnp.arange(N_EXPERTS, dtype=jnp.int32)[None, :]).astype(jnp.int32)
    rank = jnp.sum(onehot * (jnp.cumsum(onehot, axis=0) - 1), axis=1)
    counts = jnp.sum(onehot, axis=0)
    padded = ((counts + tm - 1) // tm) * tm
    ends = jnp.cumsum(padded)
    starts = ends - padded
    dest = starts[experts] + rank
    p = 2 * n + N_EXPERTS * tm
    tokens = jnp.concatenate([jnp.arange(n, dtype=jnp.int32)] * 2)
    inb = dict(mode="promise_in_bounds")
    src = jnp.zeros((p,), jnp.int32).at[dest].set(tokens, unique_indices=True, **inb)
    tile_start = jnp.arange(p // tm, dtype=jnp.int32) * tm
    tile_expert = jnp.minimum(jnp.searchsorted(ends, tile_start, side="right"), N_EXPERTS - 1).astype(jnp.int32)
    tile_valid = (tile_start < ends[-1]).astype(jnp.int32)
    ys = _moe_experts(u.at[src].get(**inb), tile_expert, tile_valid, wg, wu, wd, j)
    ya = ys.at[dest[:n]].get(**inb)
    yb = ys.at[dest[n:]].get(**inb)
    return _moe_combine(x, ya, yb, route, mod, nlt=nlt, tpb=tpb, ncond=ncond, out_rows=out_rows)


def _rope_tables(seq, n_ctx_rows, batch):
    rows = seq // GRID_W
    row = jnp.repeat(jnp.arange(rows, dtype=F32), GRID_W)
    col = jnp.broadcast_to(jnp.arange(GRID_W, dtype=F32), (rows, GRID_W)).reshape(-1)
    inv_freq = jnp.power(ROPE_THETA, -jnp.arange(ROPE_FREQS, dtype=F32) / ROPE_FREQS)
    ar = row[:, None] * inv_freq
    ac = col[:, None] * inv_freq
    cos = jnp.concatenate([jnp.cos(ar), jnp.cos(ar), jnp.cos(ac), jnp.cos(ac)], axis=-1)
    sin = jnp.concatenate([-jnp.sin(ar), jnp.sin(ar), -jnp.sin(ac), jnp.sin(ac)], axis=-1)
    cos = jnp.tile(jnp.concatenate([cos, cos], axis=-1), (batch, 1))
    sin = jnp.tile(jnp.concatenate([sin, sin], axis=-1), (batch, 1))
    cos = jnp.concatenate([cos, jnp.ones((n_ctx_rows, 2 * HEAD_DIM), F32)], axis=0)
    sin = jnp.concatenate([sin, jnp.zeros((n_ctx_rows, 2 * HEAD_DIM), F32)], axis=0)
    return cos, sin


def kernel(x, c, ctx, c_ctx, mod_w, mod_b, norm1_g, norm2_g, w_in, w_out, pool_lin, pool_scale, hy_short_w, hy_short_b, hy_f_w1, hy_f_b1, hy_f_freq1, hy_f_w2, hy_f_b2, hy_f_freq2, hy_f_w3, hy_bias, qk_norm_g, diff_lambda, subln_g, ffn_w_gate, ffn_w_up, ffn_w_down, router_w, moe_w_gate, moe_w_up, moe_w_down):
    batch, seq, d = x.shape
    ctx_len = ctx.shape[1]
    depth = mod_w.shape[0]
    pw = pool_scale.shape[1]
    hw = hy_bias.shape[1]
    nl, nc = batch * seq, batch * ctx_len
    n = nl + nc
    tm = ROW_TILE
    assert seq % tm == 0 and nc % tm == 0 and seq % (FFT_N1 // 2) == 0 and seq % GRID_W == 0
    assert d == HEADS * 2 * V_DIM and pw == hw and batch == 2
    n2 = 2 * seq // FFT_N1
    ncols = n2 * hw
    assert (n * hw) % ncols == 0
    tiles = dict(nlt=nl // tm, tpb=seq // tm, ncond=batch + 1)

    xs = jnp.concatenate([x.reshape(nl, d), ctx.reshape(nc, d)], axis=0)
    cond8 = jnp.concatenate([c, c_ctx[None, :], jnp.zeros((8 - batch - 1, d), F32)], axis=0)
    mods = _modvec(cond8, mod_w, mod_b)
    cos_t, sin_t = _rope_tables(seq, nc, batch)
    consts = _dft_consts(n2)
    small = _dft_small_consts(ctx_len)
    eye = jnp.eye(len(POOL_WINDOWS), dtype=F32)
    moe_wg, moe_wu, moe_wd = _cast_bf16(moe_w_gate), _cast_bf16(moe_w_up), _cast_bf16(moe_w_down)

    for i in range(depth):
        last = i == depth - 1
        lam_init = 0.8 - 0.6 * math.exp(-0.3 * i)
        mod = mods[i]
        gqk = jnp.concatenate([qk_norm_g[i], qk_norm_g[i]], axis=-1)
        zp, zh, qt, k12, vt = _in_proj(xs, norm1_g[i][None, :], mod, w_in[i].astype(BF16), cos_t, sin_t, gqk,
                                       pw=pw, hw=hw, **tiles)

        g_b = jnp.broadcast_to(subln_g[i][:, None], (V_DIM, ATT_TQ))
        att_kw = dict(batch=batch, seq=seq, ctx_len=ctx_len, lam_init=lam_init)

        def attend(direct, lam_p=diff_lambda[i], qt=qt, k12=k12, vt=vt, g_b=g_b, att_kw=att_kw):
            return (_attention(lam_p, qt, k12, vt, g_b, latent=True, direct=direct, **att_kw),
                    _attention(lam_p, qt, k12, vt, g_b, latent=False, direct=direct, **att_kw))

        bound = (HEAD_DIM * QSCALE) * jnp.max(jnp.abs(qk_norm_g[i][0])) * jnp.max(jnp.abs(qk_norm_g[i][1]))
        ya_l, ya_c = lax.cond(bound * 1.02 < ATT_DIRECT_MAX, lambda: attend(True), lambda: attend(False))

        lin_bd = (eye[:, None, :, None] * pool_lin[i][:, :, None, :]).reshape(pw, pw).astype(BF16)
        pscale = pool_scale[i][None, :]
        yp_l = _pool(zp, lin_bd, pscale, row0=0, batch=batch, seq=seq, out_rows=nl, out_row0=0)
        yp_c = _pool(zp, lin_bd, pscale, row0=nl, batch=batch, seq=ctx_len, out_rows=nc, out_row0=0)

        sw, sb = hy_short_w[i], hy_short_b[i][None, :]
        u_l, x0_l = _hy_pre(zh, sw, sb, row0=0, batch=batch, seq=seq, out_rows=n, out_row0=0)
        u_c, x0_c = _hy_pre(zh, sw, sb, row0=nl, batch=batch, seq=ctx_len, out_rows=nc, out_row0=0)
        filt = (hy_f_w1[i], hy_f_b1[i], hy_f_freq1[i], hy_f_w2[i], hy_f_b2[i], hy_f_freq2[i], hy_f_w3[i])
        bias = hy_bias[i][None, :]
        taps, asum = _hy_filter(seq, *filt, hw)
        kr1, ki1 = _fft_first(taps.reshape(FFT_N1, ncols), consts["f1_real"], ncols)
        spectrum = _fft_mid(kr1, ki1, consts, n2, hw)
        ar, ai = _fft_first(u_l.reshape(-1, ncols), consts["f1_data"], ncols)
        br, bi = _fft_mid(ar, ai, consts, n2, hw, spectrum=spectrum)
        reps = min(FFT_COLS, ncols) // hw
        yh_lat = _fft_last(br.reshape(FFT_N1, ncols), bi.reshape(FFT_N1, ncols), consts["g1"],
                           u_l.reshape(-1, ncols), x0_l.reshape(-1, ncols),
                           jnp.tile(bias, (1, reps)), jnp.tile(1.0 / asum, (1, reps)), ncols)
        taps_c, asum_c = _hy_filter(ctx_len, *filt, hw)
        yh_ctx = _hy_ctx(u_c, x0_c, taps_c, bias, 1.0 / asum_c, small)

        j = i // 2
        moe = i % 2 == 1
        rw = None
        if moe:
            rw = jnp.concatenate([router_w[j], jnp.zeros((d, 128 - N_EXPERTS), F32)], axis=1)
        outs = _out_proj(xs, (yp_l, yh_lat.reshape(nl, hw), ya_l), (yp_c, yh_ctx, ya_c), w_out[i].astype(BF16),
                         norm2_g[i][None, :], mod, rw, **tiles)
        out_rows = nl if last else n
        if moe:
            xs, u, route = outs
            xs = _moe_layer(xs, u, route, mod, moe_wg, moe_wu, moe_wd, j, out_rows=out_rows, **tiles)
        else:
            xs, u = outs
            xs = _ffn(xs, u, mod, ffn_w_gate[j].astype(BF16), ffn_w_up[j].astype(BF16),
                      ffn_w_down[j].astype(BF16), out_rows=out_rows, **tiles)
    return xs[:nl].reshape(batch, seq, d)
```

```python
import functools
import math

import numpy as np
import jax
import jax.numpy as jnp
from jax import lax
from jax.experimental import pallas as pl
from jax.experimental.pallas import tpu as pltpu

F32 = jnp.float32
BF16 = jnp.bfloat16
EPS = 1e-6

GRID_W = 64
POOL_WINDOWS = (2, 4, 8, 16)
HEADS = 4
HEAD_DIM = 64
V_DIM = 128
ROPE_FREQS = 16
ROPE_THETA = 10000.0
HY_BANDS = 16
HY_TARGET = 1e-2
HY_FAST = 0.3
HY_SLOW = 1.5
N_EXPERTS = 8

ROW_TILE = 512
HALO = 32
FFT_N1 = 128
FFT_COLS = 2048
FFT_KB = 4
ATT_TQ = 1024
ATT_TK = 512
ATT_UNROLL = 8
QSCALE = (HEAD_DIM ** -0.5) * math.log2(math.e)
ATT_DIRECT_MAX = 100.0
MOE_TM = 512
MOE_FC = 1792
VMEM_LIMIT = 56 * 1024 * 1024


def _cparams(sem):
    return pltpu.CompilerParams(dimension_semantics=sem, vmem_limit_bytes=VMEM_LIMIT)


def _dot(a, b):
    return jnp.dot(a, b, preferred_element_type=F32)


def _split(a):
    hi = a.astype(BF16)
    lo = (a - hi.astype(F32)).astype(BF16)
    return hi, lo


def _dot3(a, b):
    ah, al = _split(a)
    bh, bl = _split(b)
    return _dot(ah, bh) + _dot(ah, bl) + _dot(al, bh)


def _dot3c(ch, cl, d):
    dh, dl = _split(d)
    return _dot(ch, dh) + _dot(ch, dl) + _dot(cl, dh)


def _silu(x):
    return x / (1.0 + jnp.exp(-x))


def _normmod(x, g, shift, scale):
    ms = jnp.mean(x * x, axis=-1, keepdims=True)
    return (x * lax.rsqrt(ms + EPS) * g) * (1.0 + scale) + shift


def _cond_row(t, n_lat_tiles, tiles_per_batch, n_cond):
    return jnp.where(t >= n_lat_tiles, n_cond - 1, t // tiles_per_batch)


def _cast_kernel(w_ref, o_ref):
    o_ref[...] = w_ref[...].astype(BF16)


def _cast_bf16(w):
    shape = w.shape
    w2 = w.reshape(-1, shape[-1])
    rows, cols = w2.shape
    tile = ROW_TILE
    out = pl.pallas_call(
        _cast_kernel,
        grid=(rows // tile,),
        in_specs=[pl.BlockSpec((tile, cols), lambda t: (t, 0))],
        out_specs=pl.BlockSpec((tile, cols), lambda t: (t, 0)),
        out_shape=jax.ShapeDtypeStruct((rows, cols), BF16),
        compiler_params=_cparams(("arbitrary",)),
    )(w2)
    return out.reshape(shape)


def _modvec_kernel(c_ref, w_ref, b_ref, o_ref):
    s = _silu(c_ref[...])
    o_ref[0, 0] = _dot3(s, w_ref[0]) + b_ref[0]


def _modvec(cond8, mod_w, mod_b):
    depth, d, six_d = mod_w.shape
    nchunk = six_d // d
    return pl.pallas_call(
        _modvec_kernel,
        grid=(depth, nchunk),
        in_specs=[
            pl.BlockSpec((8, d), lambda i, j: (0, 0)),
            pl.BlockSpec((1, d, d), lambda i, j: (i, 0, j)),
            pl.BlockSpec((1, 1, d), lambda i, j: (i, 0, j)),
        ],
        out_specs=pl.BlockSpec((1, 1, 8, d), lambda i, j: (i, j, 0, 0)),
        out_shape=jax.ShapeDtypeStruct((depth, nchunk, 8, d), F32),
        compiler_params=_cparams(("arbitrary", "arbitrary")),
    )(cond8, mod_w, mod_b.reshape(depth, 1, six_d))


def _in_proj_kernel(x_ref, g_ref, mod_ref, w_ref, cos_ref, sin_ref, gqk_ref,
                    zp_ref, zh_ref, qt_ref, k_ref, vt_ref, *, nlt, tpb, ncond, pw, hw):
    ci = _cond_row(pl.program_id(0), nlt, tpb, ncond)
    shift = mod_ref[0, pl.ds(ci, 1), :]
    scale = mod_ref[1, pl.ds(ci, 1), :]
    y = _normmod(x_ref[...], g_ref[...], shift, scale).astype(BF16)
    z = _dot(y, w_ref[...])
    zp_ref[...] = z[:, :pw].astype(BF16)
    zh_ref[...] = z[:, pw:pw + 3 * hw].astype(BF16)

    tm = z.shape[0]
    att = pw + 3 * hw
    lane = lax.broadcasted_iota(jnp.int32, (tm, 2 * HEAD_DIM), 1)
    first = lane < HEAD_DIM
    apart = (lane % (2 * ROPE_FREQS)) < ROPE_FREQS
    cos = cos_ref[...]
    sin = sin_ref[...]
    qk_w = HEADS * 2 * HEAD_DIM

    def norm_rope(v, g):
        v2 = v * v
        s_all = jnp.sum(v2, axis=-1, keepdims=True)
        s_first = jnp.sum(jnp.where(first, v2, 0.0), axis=-1, keepdims=True)
        ms = jnp.where(first, s_first, s_all - s_first) * (1.0 / HEAD_DIM)
        vn = v * lax.rsqrt(ms + EPS) * g
        swapped = jnp.where(apart, pltpu.roll(vn, 2 * HEAD_DIM - ROPE_FREQS, 1), pltpu.roll(vn, ROPE_FREQS, 1))
        return vn * cos + swapped * sin

    for h in range(HEADS):
        lo = att + h * 2 * HEAD_DIM
        q = norm_rope(z[:, lo:lo + 2 * HEAD_DIM], gqk_ref[0:1, :])
        qt_ref[h] = (q * QSCALE).T.astype(BF16)
        k = norm_rope(z[:, qk_w + lo:qk_w + lo + 2 * HEAD_DIM], gqk_ref[1:2, :])
        k_ref[h] = k.astype(BF16)
        vlo = att + 2 * qk_w + h * V_DIM
        vt_ref[h] = z[:, vlo:vlo + V_DIM].T.astype(BF16)


def _in_proj(x, g, mod, w, cos_t, sin_t, gqk, *, nlt, tpb, ncond, pw, hw):
    n, d = x.shape
    wid = w.shape[1]
    tm = ROW_TILE
    kern = functools.partial(_in_proj_kernel, nlt=nlt, tpb=tpb, ncond=ncond, pw=pw, hw=hw)
    return pl.pallas_call(
        kern,
        grid=(n // tm,),
        in_specs=[
            pl.BlockSpec((tm, d), lambda t: (t, 0)),
            pl.BlockSpec((1, d), lambda t: (0, 0)),
            pl.BlockSpec(mod.shape, lambda t: (0, 0, 0)),
            pl.BlockSpec((d, wid), lambda t: (0, 0)),
            pl.BlockSpec((tm, 2 * HEAD_DIM), lambda t: (t, 0)),
            pl.BlockSpec((tm, 2 * HEAD_DIM), lambda t: (t, 0)),
            pl.BlockSpec((2, 2 * HEAD_DIM), lambda t: (0, 0)),
        ],
        out_specs=[
            pl.BlockSpec((tm, pw), lambda t: (t, 0)),
            pl.BlockSpec((tm, 3 * hw), lambda t: (t, 0)),
            pl.BlockSpec((HEADS, 2 * HEAD_DIM, tm), lambda t: (0, 0, t)),
            pl.BlockSpec((HEADS, tm, 2 * HEAD_DIM), lambda t: (0, t, 0)),
            pl.BlockSpec((HEADS, V_DIM, tm), lambda t: (0, 0, t)),
        ],
        out_shape=[
            jax.ShapeDtypeStruct((n, pw), BF16),
            jax.ShapeDtypeStruct((n, 3 * hw), BF16),
            jax.ShapeDtypeStruct((HEADS, 2 * HEAD_DIM, n), BF16),
            jax.ShapeDtypeStruct((HEADS, n, 2 * HEAD_DIM), BF16),
            jax.ShapeDtypeStruct((HEADS, V_DIM, n), BF16),
        ],
        compiler_params=_cparams(("arbitrary",)),
    )(x, g, mod, w, cos_t, sin_t, gqk)


def _attn_kernel(lam_ref, qt_ref, kc_ref, vtc_ref, *rest, tk, n_lat_chunks, lam_init):
    if n_lat_chunks:
        kl_ref, vtl_ref, g_ref, o_ref, m_ref, l_ref, acc_ref = rest
    else:
        g_ref, o_ref, m_ref, l_ref, acc_ref = rest
    qt = qt_ref[0]
    row = lax.broadcasted_iota(jnp.int32, qt.shape, 0)
    zero = jnp.zeros_like(qt)
    qmaps = (jnp.where(row < HEAD_DIM, qt, zero), jnp.where(row >= HEAD_DIM, qt, zero))

    def process(k_tile, vt_tile, first):
        for mi in range(2):
            s = _dot(k_tile, qmaps[mi])
            smax = jnp.max(s, axis=0, keepdims=True)
            if first:
                m_new = smax
            else:
                m_old = m_ref[mi]
                m_new = jnp.maximum(m_old, smax)
            p = jnp.exp2(s - m_new)
            psum = jnp.sum(p, axis=0, keepdims=True)
            pv = _dot(vt_tile, p.astype(BF16))
            if first:
                l_ref[mi] = psum
                acc_ref[mi] = pv
            else:
                alpha = jnp.exp2(m_old - m_new)
                l_ref[mi] = alpha * l_ref[mi] + psum
                acc_ref[mi] = alpha * acc_ref[mi] + pv
            m_ref[mi] = m_new

    process(kc_ref[0], vtc_ref[0], True)
    if n_lat_chunks:
        def body(j, carry):
            off = pl.multiple_of(j * tk, tk)
            process(kl_ref[0, pl.ds(off, tk), :], vtl_ref[0, :, pl.ds(off, tk)], False)
            return carry
        lax.fori_loop(0, n_lat_chunks, body, 0)

    lv = lam_ref[...]
    lam = (jnp.exp(jnp.sum(lv[0:1] * lv[1:2], axis=-1, keepdims=True))
           - jnp.exp(jnp.sum(lv[2:3] * lv[3:4], axis=-1, keepdims=True)) + lam_init)
    o = acc_ref[0] * (1.0 / l_ref[0]) - lam * (acc_ref[1] * (1.0 / l_ref[1]))
    ms = jnp.mean(o * o, axis=0, keepdims=True)
    y = o * lax.rsqrt(ms + EPS) * g_ref[...] * (1.0 - lam_init)
    o_ref[...] = y.T.astype(BF16)


def _attn_direct_kernel(lam_ref, qt_ref, kc_ref, vtc_ref, *rest, tk, n_lat_chunks, lam_init):
    if n_lat_chunks:
        kl_ref, vtl_ref, g_ref, o_ref, s_ref, p_ref, l_ref, acc_ref = rest
    else:
        g_ref, o_ref, s_ref, p_ref, l_ref, acc_ref = rest
    qt = qt_ref[0]
    tq = qt.shape[1]
    row = lax.broadcasted_iota(jnp.int32, qt.shape, 0)
    zero = jnp.zeros_like(qt)
    qmaps = (jnp.where(row < HEAD_DIM, qt, zero), jnp.where(row >= HEAD_DIM, qt, zero))

    def piece_rows(j):
        if isinstance(j, int):
            return pl.ds((j - 1) * tk, tk)
        return pl.ds(pl.multiple_of((j - 1) * tk, tk), tk)

    def k_piece(j):
        return kl_ref[0, piece_rows(j), :]

    def vt_piece(j):
        return vtc_ref[0] if isinstance(j, int) and j == 0 else vtl_ref[0, :, piece_rows(j)]

    def step(j, par):
        stage_a(k_piece(j + 1), 1 - par)
        stage_b(par)
        stage_c(vt_piece(j - 1), 1 - par)

    def stage_a(k_tile, slot):
        for mi in range(2):
            s_ref[slot, mi] = _dot(k_tile, qmaps[mi])

    def stage_b(slot):
        for mi in range(2):
            p = jnp.exp2(s_ref[slot, mi].astype(BF16))
            l_ref[mi] = l_ref[mi] + jnp.sum(p.astype(F32).reshape(tk // 8, 8, tq), axis=0)
            p_ref[slot, mi] = p

    def stage_c(vt_tile, slot):
        for mi in range(2):
            acc_ref[mi] = acc_ref[mi] + _dot(vt_tile, p_ref[slot, mi])

    l_ref[...] = jnp.zeros_like(l_ref)
    acc_ref[...] = jnp.zeros_like(acc_ref)
    stage_a(kc_ref[0], 0)
    if n_lat_chunks == 0:
        stage_b(0)
        stage_c(vtc_ref[0], 0)
    else:
        stage_a(k_piece(1), 1)
        stage_b(0)
        for j in range(1, ATT_UNROLL):
            step(j, j % 2)

        def body(jj, carry):
            j = ATT_UNROLL * (jj + 1)
            for r in range(ATT_UNROLL):
                step(j + r, r % 2)
            return carry

        lax.fori_loop(0, n_lat_chunks // ATT_UNROLL - 1, body, 0)
        stage_b(0)
        stage_c(vt_piece(n_lat_chunks - 1), 1)
        stage_c(vt_piece(n_lat_chunks), 0)

    lv = lam_ref[...]
    lam = (jnp.exp(jnp.sum(lv[0:1] * lv[1:2], axis=-1, keepdims=True))
           - jnp.exp(jnp.sum(lv[2:3] * lv[3:4], axis=-1, keepdims=True)) + lam_init)
    l1 = jnp.sum(l_ref[0], axis=0, keepdims=True)
    l2 = jnp.sum(l_ref[1], axis=0, keepdims=True)
    o = acc_ref[0] * (1.0 / l1) - lam * (acc_ref[1] * (1.0 / l2))
    ms = jnp.mean(o * o, axis=0, keepdims=True)
    y = o * lax.rsqrt(ms + EPS) * g_ref[...] * (1.0 - lam_init)
    o_ref[...] = y.T.astype(BF16)


def _attention(lam_p, qt, k12, vt, g_b, *, batch, seq, ctx_len, latent, lam_init, direct):
    n = k12.shape[1]
    nl = batch * seq
    tq = ATT_TQ if latent else ctx_len
    nq = seq // tq if latent else 1
    tk = ctx_len if direct else ATT_TK
    assert seq % ((ATT_UNROLL if direct else 1) * tk) == 0
    qbase = 0 if latent else nl // tq

    def qrow(b, iq):
        return (b * nq + iq) if latent else (qbase + b)

    in_specs = [
        pl.BlockSpec(lam_p.shape, lambda b, h, iq: (0, 0)),
        pl.BlockSpec((1, 2 * HEAD_DIM, tq), lambda b, h, iq: (h, 0, qrow(b, iq))),
        pl.BlockSpec((1, ctx_len, 2 * HEAD_DIM), lambda b, h, iq: (h, nl // ctx_len + b, 0)),
        pl.BlockSpec((1, V_DIM, ctx_len), lambda b, h, iq: (h, 0, nl // ctx_len + b)),
    ]
    args = [lam_p, qt, k12, vt]
    if latent:
        in_specs += [
            pl.BlockSpec((1, seq, 2 * HEAD_DIM), lambda b, h, iq: (h, b, 0)),
            pl.BlockSpec((1, V_DIM, seq), lambda b, h, iq: (h, 0, b)),
        ]
        args += [k12, vt]
    in_specs.append(pl.BlockSpec((V_DIM, tq), lambda b, h, iq: (0, 0)))
    args.append(g_b[:, :tq])
    nrows = nl if latent else batch * ctx_len
    body = _attn_direct_kernel if direct else _attn_kernel
    kern = functools.partial(body, tk=tk, n_lat_chunks=(seq // tk if latent else 0), lam_init=lam_init)
    if direct:
        scratch = [pltpu.VMEM((2, 2, tk, tq), F32), pltpu.VMEM((2, 2, tk, tq), BF16),
                   pltpu.VMEM((2, 8, tq), F32), pltpu.VMEM((2, V_DIM, tq), F32)]
    else:
        scratch = [pltpu.VMEM((2, 1, tq), F32), pltpu.VMEM((2, 1, tq), F32), pltpu.VMEM((2, V_DIM, tq), F32)]
    return pl.pallas_call(
        kern,
        grid=(batch, HEADS, nq),
        in_specs=in_specs,
        out_specs=pl.BlockSpec((tq, V_DIM), lambda b, h, iq: (b * nq + iq, h)),
        out_shape=jax.ShapeDtypeStruct((nrows, HEADS * V_DIM), BF16),
        scratch_shapes=scratch,
        compiler_params=_cparams(("arbitrary", "arbitrary", "arbitrary")),
    )(*args)


def _halo_specs(width, *, row0, seq, tile):
    hb = tile // HALO

    def cur(b, i):
        return ((row0 + b * seq) // tile + i, 0)

    def prev(b, i):
        first = (row0 + b * seq) // HALO
        return (jnp.maximum(first + i * hb - 1, first), 0)

    def nxt(b, i):
        first = (row0 + b * seq) // HALO
        return (jnp.minimum(first + (i + 1) * hb, first + seq // HALO - 1), 0)

    return [pl.BlockSpec((HALO, width), prev), pl.BlockSpec((tile, width), cur), pl.BlockSpec((HALO, width), nxt)]


def _fill_ext(ext_ref, prev_ref, cur_ref, next_ref, *, seq, tile):
    pos0 = pl.program_id(1) * tile
    width = cur_ref.shape[1]
    hpos = lax.broadcasted_iota(jnp.int32, (HALO, width), 0)
    ext_ref[0:HALO, :] = jnp.where(pos0 - HALO + hpos >= 0, prev_ref[...].astype(F32), 0.0)
    ext_ref[HALO:HALO + tile, :] = cur_ref[...].astype(F32)
    ext_ref[HALO + tile:, :] = jnp.where(pos0 + tile + hpos < seq, next_ref[...].astype(F32), 0.0)


def _pool_kernel(prev_ref, cur_ref, next_ref, lin_ref, scale_ref, o_ref, ext_ref, s_ref, *, seq, tile):
    _fill_ext(ext_ref, prev_ref, cur_ref, next_ref, seq=seq, tile=tile)
    width = cur_ref.shape[1]
    gd = width // len(POOL_WINDOWS)
    n0 = tile + 2 * HALO
    s_ref[0, 0:n0 - 8, :] = ext_ref[0:n0 - 8, :] + ext_ref[1:n0 - 7, :]
    for k in range(1, len(POOL_WINDOWS)):
        step = 1 << k
        ln = n0 - 8 * (k + 1)
        s_ref[k, 0:ln, :] = s_ref[k - 1, 0:ln, :] + s_ref[k - 1, step:step + ln, :]
    lane = lax.broadcasted_iota(jnp.int32, (tile, width), 1)
    pos = pl.program_id(1) * tile + lax.broadcasted_iota(jnp.int32, (tile, width), 0)
    grp = lane // gd
    wsum = jnp.zeros((tile, width), F32)
    half = jnp.zeros((tile, width), jnp.int32)
    for k, win in enumerate(POOL_WINDOWS):
        start = HALO - win // 2
        wsum = jnp.where(grp == k, s_ref[k, start:start + tile, :], wsum)
        half = jnp.where(grp == k, win // 2, half)
    cnt = jnp.minimum(pos + half, seq) - jnp.maximum(pos - half, 0)
    z = ext_ref[HALO:HALO + tile, :]
    dlt = (wsum / cnt.astype(F32) - z).astype(BF16)
    o_ref[...] = (_dot(dlt, lin_ref[...]) * scale_ref[...]).astype(BF16)


def _pool(zp, lin_bd, scale, *, row0, batch, seq, out_rows, out_row0):
    width = zp.shape[1]
    tile = min(ROW_TILE, seq)
    kern = functools.partial(_pool_kernel, seq=seq, tile=tile)
    return pl.pallas_call(
        kern,
        grid=(batch, seq // tile),
        in_specs=_halo_specs(width, row0=row0, seq=seq, tile=tile) + [
            pl.BlockSpec((width, width), lambda b, i: (0, 0)),
            pl.BlockSpec((1, width), lambda b, i: (0, 0)),
        ],
        out_specs=pl.BlockSpec((tile, width), lambda b, i: ((out_row0 + b * seq) // tile + i, 0)),
        out_shape=jax.ShapeDtypeStruct((out_rows, width), BF16),
        scratch_shapes=[
            pltpu.VMEM((tile + 2 * HALO, width), F32),
            pltpu.VMEM((4, tile + 2 * HALO, width), F32),
        ],
        compiler_params=_cparams(("arbitrary", "arbitrary")),
    )(zp, zp, zp, lin_bd, scale)


def _hy_pre_kernel(prev_ref, cur_ref, next_ref, w_ref, b_ref, u_ref, x0_ref, ext_ref, *, seq, tile, hw):
    _fill_ext(ext_ref, prev_ref, cur_ref, next_ref, seq=seq, tile=tile)
    y = b_ref[...] + ext_ref[HALO - 1:HALO - 1 + tile, :] * w_ref[0:1, :]
    y = y + ext_ref[HALO:HALO + tile, :] * w_ref[1:2, :]
    y = y + ext_ref[HALO + 1:HALO + 1 + tile, :] * w_ref[2:3, :]
    x0_ref[...] = y[:, :hw]
    u_ref[...] = y[:, 2 * hw:] * y[:, hw:2 * hw]


def _hy_pre(zh, sw, sb, *, row0, batch, seq, out_rows, out_row0):
    width = zh.shape[1]
    hw = width // 3
    tile = min(ROW_TILE, seq)
    kern = functools.partial(_hy_pre_kernel, seq=seq, tile=tile, hw=hw)
    ospec = pl.BlockSpec((tile, hw), lambda b, i: ((out_row0 + b * seq) // tile + i, 0))
    return pl.pallas_call(
        kern,
        grid=(batch, seq // tile),
        in_specs=_halo_specs(width, row0=row0, seq=seq, tile=tile) + [
            pl.BlockSpec((3, width), lambda b, i: (0, 0)),
            pl.BlockSpec((1, width), lambda b, i: (0, 0)),
        ],
        out_specs=[ospec, ospec],
        out_shape=[jax.ShapeDtypeStruct((out_rows, hw), F32), jax.ShapeDtypeStruct((out_rows, hw), F32)],
        scratch_shapes=[pltpu.VMEM((tile + 2 * HALO, width), F32)],
        compiler_params=_cparams(("arbitrary", "arbitrary")),
    )(zh, zh, zh, sw, sb)


def _filter_kernel(feat_ref, w1_ref, b1_ref, f1_ref, w2_ref, b2_ref, f2_ref, w3_ref, dl_ref,
                   taps_ref, asum_ref, *, tile, hw, seq):
    feat = feat_ref[...]
    h = jnp.sin(f1_ref[...] * (_dot3(feat, w1_ref[...]) + b1_ref[...]))
    h = jnp.sin(f2_ref[...] * (_dot3(h, w2_ref[...]) + b2_ref[...]))
    h = _dot3(h, w3_ref[...])
    dec = jnp.exp(-feat[:, 0:1] * dl_ref[...])
    pos = pl.program_id(0) * tile + lax.broadcasted_iota(jnp.int32, (tile, hw), 0)
    taps = jnp.where(pos == seq, 0.0, h * dec)
    taps_ref[...] = taps
    part = jnp.sum(jnp.abs(taps), axis=0, keepdims=True)

    @pl.when(pl.program_id(0) == 0)
    def _():
        asum_ref[...] = part

    @pl.when(pl.program_id(0) != 0)
    def _():
        asum_ref[...] = asum_ref[...] + part


def _hy_filter(seq, w1, b1, f1, w2, b2, f2, w3, hw):
    t = jnp.linspace(0.0, 1.0, seq, dtype=F32)[:, None]
    w = (2.0 * math.pi / seq) * jnp.arange(seq, dtype=F32)[:, None]
    bands = jnp.linspace(1e-4, HY_BANDS - 1, HY_BANDS, dtype=F32)[None, :]
    emb = 1 + 2 * HY_BANDS
    embp = ((emb + 7) // 8) * 8
    feat = jnp.concatenate([t, jnp.cos(bands * w), -jnp.sin(bands * w), jnp.zeros((seq, embp - emb), F32)], axis=-1)
    w1p = jnp.concatenate([w1, jnp.zeros((embp - emb, w1.shape[1]), F32)], axis=0)
    max_decay = math.log(1.0 / HY_TARGET) / HY_FAST
    min_decay = math.log(1.0 / HY_TARGET) / HY_SLOW
    deltas = jnp.linspace(min_decay, max_decay, hw, dtype=F32)[None, :]
    feat = jnp.concatenate([feat, feat[0:1], jnp.flip(feat[1:], axis=0)], axis=0)
    tile = min(ROW_TILE, seq)
    nfwd = seq // tile
    hid = w1.shape[1]
    full = lambda shape: pl.BlockSpec(shape, lambda i: (0,) * len(shape))
    return pl.pallas_call(
        functools.partial(_filter_kernel, tile=tile, hw=hw, seq=seq),
        grid=(2 * nfwd,),
        in_specs=[
            pl.BlockSpec((tile, embp), lambda i: (i, 0)),
            full((embp, hid)), full((1, hid)), full((1, hid)),
            full((hid, hid)), full((1, hid)), full((1, hid)),
            pl.BlockSpec((hid, hw), lambda i: (0, i // nfwd)), full((1, hw)),
        ],
        out_specs=[
            pl.BlockSpec((tile, hw), lambda i: (i, 0)),
            pl.BlockSpec((1, hw), lambda i: (0, 0)),
        ],
        out_shape=[
            jax.ShapeDtypeStruct((2 * seq, hw), F32),
            jax.ShapeDtypeStruct((1, hw), F32),
        ],
        compiler_params=_cparams(("arbitrary",)),
    )(feat, w1p, b1[None, :], f1[None, :], w2, b2[None, :], f2[None, :], w3, deltas)


def _np_split(a):
    a32 = jnp.asarray(a, F32)
    hi = a32.astype(BF16)
    lo = (a32 - hi.astype(F32)).astype(BF16)
    return hi, lo


def _dft_consts(n2):
    n1 = FFT_N1
    n = n1 * n2
    half = n1 // 2
    a1 = -2.0 * np.pi * np.outer(np.arange(n1), np.arange(n1)) / n1
    f1r, f1i = np.cos(a1), np.sin(a1)
    f1_data = np.block([[f1r[:, :half], -f1i[:, :half]], [f1i[:, :half], f1r[:, :half]]])
    f1_real = np.concatenate([f1r, f1i], axis=0)
    g1r, g1i = f1r / n, -f1i / n
    g1 = np.block([[g1r[:half], -g1i[:half]], [g1i[:half], g1r[:half]]])
    a2 = -2.0 * np.pi * np.outer(np.arange(n2), np.arange(n2)) / n2
    f2r, f2i = np.cos(a2), np.sin(a2)
    f2 = np.block([[f2r, -f2i], [f2i, f2r]])
    g2 = np.block([[f2r, f2i], [-f2i, f2r]])
    at = -2.0 * np.pi * np.outer(np.arange(n1), np.arange(n2)) / n
    twr = jnp.broadcast_to(jnp.asarray(np.cos(at), F32)[:, :, None], (n1, n2, 128))
    twi = jnp.broadcast_to(jnp.asarray(np.sin(at), F32)[:, :, None], (n1, n2, 128))
    return dict(f1_data=_np_split(f1_data), f1_real=_np_split(f1_real), g1=_np_split(g1),
                f2=_np_split(f2), g2=_np_split(g2), twr=twr, twi=twi)


def _fft1_kernel(z_ref, fh_ref, fl_ref, ar_ref, ai_ref):
    a = _dot3c(fh_ref[...], fl_ref[...], z_ref[...])
    ar_ref[...] = a[:FFT_N1]
    ai_ref[...] = a[FFT_N1:]


def _fft_first(zview, fmat, ncols):
    cb = min(FFT_COLS, ncols)
    fh, fl = fmat
    cspec = pl.BlockSpec(fh.shape, lambda j: (0, 0))
    ospec = pl.BlockSpec((FFT_N1, cb), lambda j: (0, j))
    return pl.pallas_call(
        _fft1_kernel,
        grid=(ncols // cb,),
        in_specs=[pl.BlockSpec((FFT_N1, cb), lambda j: (0, j)), cspec, cspec],
        out_specs=[ospec, ospec],
        out_shape=[jax.ShapeDtypeStruct((FFT_N1, ncols), F32)] * 2,
        compiler_params=_cparams(("arbitrary",)),
    )(zview, fh, fl)


def _fftmid_kernel(ar_ref, ai_ref, twr_ref, twi_ref, f2h_ref, f2l_ref, *rest, filter_only, n2):
    kb, _, width = ar_ref.shape
    reps = width // 128
    lanes = lambda parts: jnp.concatenate(parts, axis=1)
    twr = lanes([t for j in range(kb) for t in [twr_ref[j]] * reps])
    twi = lanes([t for j in range(kb) for t in [twi_ref[j]] * reps])
    ar = lanes([ar_ref[j] for j in range(kb)])
    ai = lanes([ai_ref[j] for j in range(kb)])
    z = jnp.concatenate([ar * twr - ai * twi, ar * twi + ai * twr], axis=0)
    x = _dot3c(f2h_ref[...], f2l_ref[...], z)
    xr, xi = x[:n2], x[n2:]
    if filter_only:
        kr_out, ki_out = rest
        for j in range(kb):
            kr_out[j] = xr[:, j * width:(j + 1) * width]
            ki_out[j] = xi[:, j * width:(j + 1) * width]
        return
    kr_ref, ki_ref, g2h_ref, g2l_ref, br_out, bi_out = rest
    kr = lanes([kr_ref[j] for j in range(kb)])
    ki = lanes([ki_ref[j] for j in range(kb)])
    y = jnp.concatenate([xr * kr - xi * ki, xr * ki + xi * kr], axis=0)
    w = _dot3c(g2h_ref[...], g2l_ref[...], y)
    wr, wi = w[:n2], w[n2:]
    br = wr * twr + wi * twi
    bi = wi * twr - wr * twi
    for j in range(kb):
        br_out[j] = br[:, j * width:(j + 1) * width]
        bi_out[j] = bi[:, j * width:(j + 1) * width]


def _fft_mid(ar, ai, consts, n2, width, spectrum=None):
    a3r = ar.reshape(FFT_N1, n2, width)
    a3i = ai.reshape(FFT_N1, n2, width)
    blk = pl.BlockSpec((FFT_KB, n2, width), lambda k: (k, 0, 0))
    twspec = pl.BlockSpec((FFT_KB, n2, 128), lambda k: (k, 0, 0))
    cspec = pl.BlockSpec((2 * n2, 2 * n2), lambda k: (0, 0))
    in_specs = [blk, blk, twspec, twspec, cspec, cspec]
    args = [a3r, a3i, consts["twr"], consts["twi"], *consts["f2"]]
    if spectrum is not None:
        in_specs += [blk, blk, cspec, cspec]
        args += [spectrum[0], spectrum[1], *consts["g2"]]
    return pl.pallas_call(
        functools.partial(_fftmid_kernel, filter_only=spectrum is None, n2=n2),
        grid=(FFT_N1 // FFT_KB,),
        in_specs=in_specs,
        out_specs=[blk, blk],
        out_shape=[jax.ShapeDtypeStruct((FFT_N1, n2, width), F32)] * 2,
        compiler_params=_cparams(("arbitrary",)),
    )(*args)


def _fftlast_kernel(br_ref, bi_ref, gh_ref, gl_ref, u_ref, x0_ref, bias_ref, invn_ref, o_ref):
    b = jnp.concatenate([br_ref[...], bi_ref[...]], axis=0)
    y = _dot3c(gh_ref[...], gl_ref[...], b)
    o_ref[...] = ((y * invn_ref[...] + u_ref[...] * bias_ref[...]) * x0_ref[...]).astype(BF16)


def _fft_last(br, bi, gmat, uview, x0view, bias_t, invn_t, ncols):
    cb = min(FFT_COLS, ncols)
    gh, gl = gmat
    cspec = pl.BlockSpec(gh.shape, lambda j: (0, 0))
    dspec = pl.BlockSpec((FFT_N1, cb), lambda j: (0, j))
    vspec = pl.BlockSpec((1, cb), lambda j: (0, 0))
    return pl.pallas_call(
        _fftlast_kernel,
        grid=(ncols // cb,),
        in_specs=[dspec, dspec, cspec, cspec, dspec, dspec, vspec, vspec],
        out_specs=dspec,
        out_shape=jax.ShapeDtypeStruct((FFT_N1, ncols), BF16),
        compiler_params=_cparams(("arbitrary",)),
    )(br, bi, gh, gl, uview, x0view, bias_t, invn_t)


def _dft_small_consts(seq):
    n = 2 * seq
    a = -2.0 * np.pi * np.outer(np.arange(n), np.arange(n)) / n
    fr, fi = np.cos(a), np.sin(a)
    f_data = np.block([[fr[:, :seq], -fi[:, :seq]], [fi[:, :seq], fr[:, :seq]]])
    f_real = np.concatenate([fr, fi], axis=0)
    gr, gi = fr / n, -fi / n
    g = np.block([[gr[:seq], -gi[:seq]], [gi[:seq], gr[:seq]]])
    return _np_split(f_data), _np_split(f_real), _np_split(g)


def _hy_ctx_kernel(u_ref, x0_ref, taps_ref, fdh_ref, fdl_ref, frh_ref, frl_ref, gh_ref, gl_ref,
                   bias_ref, invn_ref, o_ref):
    n = taps_ref.shape[0]
    z = u_ref[...]
    a = _dot3c(fdh_ref[...], fdl_ref[...], z)
    k = _dot3c(frh_ref[...], frl_ref[...], taps_ref[...])
    ar, ai, kr, ki = a[:n], a[n:], k[:n], k[n:]
    y = jnp.concatenate([ar * kr - ai * ki, ar * ki + ai * kr], axis=0)
    conv = _dot3c(gh_ref[...], gl_ref[...], y)
    o_ref[...] = ((conv * invn_ref[...] + z * bias_ref[...]) * x0_ref[...]).astype(BF16)


def _hy_ctx(u, x0, taps, bias, invn, small):
    rows, hw = u.shape
    full = lambda a: pl.BlockSpec(a.shape, lambda i: (0,) * a.ndim)
    args = [u, x0, taps, *small[0], *small[1], *small[2], bias, invn]
    return pl.pallas_call(
        _hy_ctx_kernel,
        grid=(1,),
        in_specs=[full(a) for a in args],
        out_specs=pl.BlockSpec((rows, hw), lambda i: (0, 0)),
        out_shape=jax.ShapeDtypeStruct((rows, hw), BF16),
        compiler_params=_cparams(("arbitrary",)),
    )(*args)


def _out_proj_kernel(x_ref, ypl_ref, yhl_ref, yal_ref, ypc_ref, yhc_ref, yac_ref, w_ref, g_ref, mod_ref, *rest,
                     nlt, tpb, ncond, pw, hw, moe):
    if moe:
        rw_ref, xo_ref, u_ref, route_ref = rest
    else:
        xo_ref, u_ref = rest
    ci = _cond_row(pl.program_id(0), nlt, tpb, ncond)
    is_ctx = pl.program_id(0) >= nlt
    yp = jnp.where(is_ctx, ypc_ref[...], ypl_ref[...])
    yh = jnp.where(is_ctx, yhc_ref[...], yhl_ref[...])
    ya = jnp.where(is_ctx, yac_ref[...], yal_ref[...])
    mix = _dot(yp, w_ref[0:pw, :]) + _dot(yh, w_ref[pw:pw + hw, :]) + _dot(ya, w_ref[pw + hw:, :])
    x = x_ref[...] + mod_ref[2, pl.ds(ci, 1), :] * mix
    xo_ref[...] = x
    un = _normmod(x, g_ref[...], mod_ref[3, pl.ds(ci, 1), :], mod_ref[4, pl.ds(ci, 1), :])
    u_ref[...] = un.astype(BF16)
    if moe:
        logits = _dot3(un, rw_ref[...])
        lane = lax.broadcasted_iota(jnp.int32, logits.shape, 1)
        neg = jnp.float32(-jnp.inf)
        lg = jnp.where(lane < N_EXPERTS, logits, neg)
        t1 = jnp.max(lg, axis=-1, keepdims=True)
        i1 = jnp.min(jnp.where(lg == t1, lane, 128), axis=-1, keepdims=True)
        lg2 = jnp.where(lane == i1, neg, lg)
        t2 = jnp.max(lg2, axis=-1, keepdims=True)
        i2 = jnp.min(jnp.where(lg2 == t2, lane, 128), axis=-1, keepdims=True)
        e2 = jnp.exp(t2 - t1)
        g1 = 1.0 / (1.0 + e2)
        g2 = e2 / (1.0 + e2)
        route_ref[...] = jnp.where(lane == 0, i1.astype(F32), jnp.where(lane == 1, i2.astype(F32),
                                   jnp.where(lane == 2, g1, jnp.where(lane == 3, g2, 0.0))))


def _out_proj(x, lat, ctx, w, g, mod, rw, *, nlt, tpb, ncond):
    n, d = x.shape
    pw, hw, aw = (a.shape[1] for a in lat)
    tm = ROW_TILE
    moe = rw is not None
    row = lambda width: pl.BlockSpec((tm, width), lambda t: (t, 0))
    lrow = lambda width: pl.BlockSpec((tm, width), lambda t: (jnp.minimum(t, nlt - 1), 0))
    crow = lambda width: pl.BlockSpec((tm, width), lambda t: (jnp.maximum(t - nlt, 0), 0))
    in_specs = [row(d), lrow(pw), lrow(hw), lrow(aw), crow(pw), crow(hw), crow(aw),
                pl.BlockSpec(w.shape, lambda t: (0, 0)),
                pl.BlockSpec((1, d), lambda t: (0, 0)),
                pl.BlockSpec(mod.shape, lambda t: (0, 0, 0))]
    args = [x, *lat, *ctx, w, g, mod]
    out_specs = [row(d), row(d)]
    out_shape = [jax.ShapeDtypeStruct((n, d), F32), jax.ShapeDtypeStruct((n, d), BF16)]
    if moe:
        in_specs.append(pl.BlockSpec(rw.shape, lambda t: (0, 0)))
        args.append(rw)
        out_specs.append(row(128))
        out_shape.append(jax.ShapeDtypeStruct((n, 128), F32))
    kern = functools.partial(_out_proj_kernel, nlt=nlt, tpb=tpb, ncond=ncond, pw=pw, hw=hw, moe=moe)
    return pl.pallas_call(
        kern, grid=(n // tm,), in_specs=in_specs, out_specs=out_specs, out_shape=out_shape,
        compiler_params=_cparams(("arbitrary",)),
    )(*args)


def _ffn_kernel(x_ref, u_ref, mod_ref, wg_ref, wu_ref, wd_ref, o_ref, *, nlt, tpb, ncond, nchunk):
    ci = _cond_row(pl.program_id(0), nlt, tpb, ncond)
    u = u_ref[...]
    ff = wg_ref.shape[1]
    fc = ff // nchunk
    y = jnp.zeros(x_ref.shape, F32)
    for c in range(nchunk):
        gate = _dot(u, wg_ref[:, c * fc:(c + 1) * fc])
        up = _dot(u, wu_ref[:, c * fc:(c + 1) * fc])
        y = y + _dot((_silu(gate) * up).astype(BF16), wd_ref[c * fc:(c + 1) * fc, :])
    o_ref[...] = x_ref[...] + mod_ref[5, pl.ds(ci, 1), :] * y


def _ffn(x, u, mod, wg, wu, wd, *, nlt, tpb, ncond, out_rows):
    n, d = x.shape
    ff = wg.shape[1]
    tm = ROW_TILE
    nchunk = 2 if (ff // 2) % 128 == 0 else 1
    row = lambda dt: pl.BlockSpec((tm, d), lambda t: (t, 0))
    const = lambda shape: pl.BlockSpec(shape, lambda t: (0,) * len(shape), pipeline_mode=pl.Buffered(1))
    kern = functools.partial(_ffn_kernel, nlt=nlt, tpb=tpb, ncond=ncond, nchunk=nchunk)
    return pl.pallas_call(
        kern,
        grid=(out_rows // tm,),
        in_specs=[row(F32), row(BF16), pl.BlockSpec(mod.shape, lambda t: (0, 0, 0)),
                  const((d, ff)), const((d, ff)), const((ff, d))],
        out_specs=row(F32),
        out_shape=jax.ShapeDtypeStruct((out_rows, d), F32),
        compiler_params=_cparams(("arbitrary",)),
    )(x, u, mod, wg, wu, wd)


def _moe_kernel(te_ref, tv_ref, u_ref, wg_ref, wu_ref, wd_ref, o_ref, acc_ref):
    t = pl.program_id(0)
    f = pl.program_id(1)

    @pl.when(tv_ref[t] > 0)
    def _():
        u = u_ref[...]
        h = (_silu(_dot(u, wg_ref[0, 0])) * _dot(u, wu_ref[0, 0])).astype(BF16)
        y = _dot(h, wd_ref[0, 0])

        @pl.when(f == 0)
        def _():
            acc_ref[...] = y

        @pl.when(f != 0)
        def _():
            acc_ref[...] = acc_ref[...] + y

        @pl.when(f == pl.num_programs(1) - 1)
        def _():
            o_ref[...] = acc_ref[...].astype(BF16)


def _moe_experts(ug, tile_expert, tile_valid, wg, wu, wd, j):
    p, d = ug.shape
    ff = wg.shape[3]
    tm, fc = MOE_TM, MOE_FC
    grid_spec = pltpu.PrefetchScalarGridSpec(
        num_scalar_prefetch=2,
        grid=(p // tm, ff // fc),
        in_specs=[
            pl.BlockSpec((tm, d), lambda t, f, te, tv: (t, 0)),
            pl.BlockSpec((1, 1, d, fc), lambda t, f, te, tv: (j, te[t], 0, f)),
            pl.BlockSpec((1, 1, d, fc), lambda t, f, te, tv: (j, te[t], 0, f)),
            pl.BlockSpec((1, 1, fc, d), lambda t, f, te, tv: (j, te[t], f, 0)),
        ],
        out_specs=pl.BlockSpec((tm, d), lambda t, f, te, tv: (t, 0)),
        scratch_shapes=[pltpu.VMEM((tm, d), F32)],
    )
    return pl.pallas_call(
        _moe_kernel, grid_spec=grid_spec,
        out_shape=jax.ShapeDtypeStruct((p, d), BF16),
        compiler_params=_cparams(("arbitrary", "arbitrary")),
    )(tile_expert, tile_valid, ug, wg, wu, wd)


def _moe_combine_kernel(x_ref, ya_ref, yb_ref, route_ref, mod_ref, o_ref, *, nlt, tpb, ncond):
    ci = _cond_row(pl.program_id(0), nlt, tpb, ncond)
    r = route_ref[...]
    lane = lax.broadcasted_iota(jnp.int32, r.shape, 1)
    g1 = jnp.sum(jnp.where(lane == 2, r, 0.0), axis=-1, keepdims=True)
    g2 = jnp.sum(jnp.where(lane == 3, r, 0.0), axis=-1, keepdims=True)
    y = g1 * ya_ref[...].astype(F32) + g2 * yb_ref[...].astype(F32)
    o_ref[...] = x_ref[...] + mod_ref[5, pl.ds(ci, 1), :] * y


def _moe_combine(x, ya, yb, route, mod, *, nlt, tpb, ncond, out_rows):
    n, d = x.shape
    tm = ROW_TILE
    row = lambda width: pl.BlockSpec((tm, width), lambda t: (t, 0))
    kern = functools.partial(_moe_combine_kernel, nlt=nlt, tpb=tpb, ncond=ncond)
    return pl.pallas_call(
        kern,
        grid=(out_rows // tm,),
        in_specs=[row(d), row(d), row(d), row(128), pl.BlockSpec(mod.shape, lambda t: (0, 0, 0))],
        out_specs=row(d),
        out_shape=jax.ShapeDtypeStruct((out_rows, d), F32),
        compiler_params=_cparams(("arbitrary",)),
    )(x, ya, yb, route, mod)


def _moe_layer(x, u, route, mod, wg, wu, wd, j, *, nlt, tpb, ncond, out_rows):
    n, d = x.shape
    tm = MOE_TM
    experts = jnp.concatenate([route[:, 0], route[:, 1]]).astype(jnp.int32)
    onehot = (experts[:, None] == jnp.arange(N_EXPERTS, dtype=jnp.int32)[None, :]).astype(jnp.int32)
    rank = jnp.sum(onehot * (jnp.cumsum(onehot, axis=0) - 1), axis=1)
    counts = jnp.sum(onehot, axis=0)
    padded = ((counts + tm - 1) // tm) * tm
    ends = jnp.cumsum(padded)
    starts = ends - padded
    dest = starts[experts] + rank
    p = 2 * n + N_EXPERTS * tm
    tokens = jnp.concatenate([jnp.arange(n, dtype=jnp.int32)] * 2)
    inb = dict(mode="promise_in_bounds")
    src = jnp.zeros((p,), jnp.int32).at[dest].set(tokens, unique_indices=True, **inb)
    tile_start = jnp.arange(p // tm, dtype=jnp.int32) * tm
    tile_expert = jnp.minimum(jnp.searchsorted(ends, tile_start, side="right"), N_EXPERTS - 1).astype(jnp.int32)
    tile_valid = (tile_start < ends[-1]).astype(jnp.int32)
    ys = _moe_experts(u.at[src].get(**inb), tile_expert, tile_valid, wg, wu, wd, j)
    ya = ys.at[dest[:n]].get(**inb)
    yb = ys.at[dest[n:]].get(**inb)
    return _moe_combine(x, ya, yb, route, mod, nlt=nlt, tpb=tpb, ncond=ncond, out_rows=out_rows)


def _rope_tables(seq, n_ctx_rows, batch):
    rows = seq // GRID_W
    row = jnp.repeat(jnp.arange(rows, dtype=F32), GRID_W)
    col = jnp.broadcast_to(jnp.arange(GRID_W, dtype=F32), (rows, GRID_W)).reshape(-1)
    inv_freq = jnp.power(ROPE_THETA, -jnp.arange(ROPE_FREQS, dtype=F32) / ROPE_FREQS)
    ar = row[:, None] * inv_freq
    ac = col[:, None] * inv_freq
    cos = jnp.concatenate([jnp.cos(ar), jnp.cos(ar), jnp.cos(ac), jnp.cos(ac)], axis=-1)
    sin = jnp.concatenate([-jnp.sin(ar), jnp.sin(ar), -jnp.sin(ac), jnp.sin(ac)], axis=-1)
    cos = jnp.tile(jnp.concatenate([cos, cos], axis=-1), (batch, 1))
    sin = jnp.tile(jnp.concatenate([sin, sin], axis=-1), (batch, 1))
    cos = jnp.concatenate([cos, jnp.ones((n_ctx_rows, 2 * HEAD_DIM), F32)], axis=0)
    sin = jnp.concatenate([sin, jnp.zeros((n_ctx_rows, 2 * HEAD_DIM), F32)], axis=0)
    return cos, sin


def kernel(x, c, ctx, c_ctx, mod_w, mod_b, norm1_g, norm2_g, w_in, w_out, pool_lin, pool_scale, hy_short_w, hy_short_b, hy_f_w1, hy_f_b1, hy_f_freq1, hy_f_w2, hy_f_b2, hy_f_freq2, hy_f_w3, hy_bias, qk_norm_g, diff_lambda, subln_g, ffn_w_gate, ffn_w_up, ffn_w_down, router_w, moe_w_gate, moe_w_up, moe_w_down):
    batch, seq, d = x.shape
    ctx_len = ctx.shape[1]
    depth = mod_w.shape[0]
    pw = pool_scale.shape[1]
    hw = hy_bias.shape[1]
    nl, nc = batch * seq, batch * ctx_len
    n = nl + nc
    tm = ROW_TILE
    assert seq % tm == 0 and nc % tm == 0 and seq % (FFT_N1 // 2) == 0 and seq % GRID_W == 0
    assert d == HEADS * 2 * V_DIM and pw == hw and batch == 2
    n2 = 2 * seq // FFT_N1
    ncols = n2 * hw
    assert (n * hw) % ncols == 0
    tiles = dict(nlt=nl // tm, tpb=seq // tm, ncond=batch + 1)

    xs = jnp.concatenate([x.reshape(nl, d), ctx.reshape(nc, d)], axis=0)
    cond8 = jnp.concatenate([c, c_ctx[None, :], jnp.zeros((8 - batch - 1, d), F32)], axis=0)
    mods = _modvec(cond8, mod_w, mod_b)
    cos_t, sin_t = _rope_tables(seq, nc, batch)
    consts = _dft_consts(n2)
    small = _dft_small_consts(ctx_len)
    eye = jnp.eye(len(POOL_WINDOWS), dtype=F32)
    moe_wg, moe_wu, moe_wd = _cast_bf16(moe_w_gate), _cast_bf16(moe_w_up), _cast_bf16(moe_w_down)

    for i in range(depth):
        last = i == depth - 1
        lam_init = 0.8 - 0.6 * math.exp(-0.3 * i)
        mod = mods[i]
        gqk = jnp.concatenate([qk_norm_g[i], qk_norm_g[i]], axis=-1)
        zp, zh, qt, k12, vt = _in_proj(xs, norm1_g[i][None, :], mod, w_in[i].astype(BF16), cos_t, sin_t, gqk,
                                       pw=pw, hw=hw, **tiles)

        g_b = jnp.broadcast_to(subln_g[i][:, None], (V_DIM, ATT_TQ))
        att_kw = dict(batch=batch, seq=seq, ctx_len=ctx_len, lam_init=lam_init)

        def attend(direct, lam_p=diff_lambda[i], qt=qt, k12=k12, vt=vt, g_b=g_b, att_kw=att_kw):
            return (_attention(lam_p, qt, k12, vt, g_b, latent=True, direct=direct, **att_kw),
                    _attention(lam_p, qt, k12, vt, g_b, latent=False, direct=direct, **att_kw))

        bound = (HEAD_DIM * QSCALE) * jnp.max(jnp.abs(qk_norm_g[i][0])) * jnp.max(jnp.abs(qk_norm_g[i][1]))
        ya_l, ya_c = lax.cond(bound * 1.02 < ATT_DIRECT_MAX, lambda: attend(True), lambda: attend(False))

        lin_bd = (eye[:, None, :, None] * pool_lin[i][:, :, None, :]).reshape(pw, pw).astype(BF16)
        pscale = pool_scale[i][None, :]
        yp_l = _pool(zp, lin_bd, pscale, row0=0, batch=batch, seq=seq, out_rows=nl, out_row0=0)
        yp_c = _pool(zp, lin_bd, pscale, row0=nl, batch=batch, seq=ctx_len, out_rows=nc, out_row0=0)

        sw, sb = hy_short_w[i], hy_short_b[i][None, :]
        u_l, x0_l = _hy_pre(zh, sw, sb, row0=0, batch=batch, seq=seq, out_rows=n, out_row0=0)
        u_c, x0_c = _hy_pre(zh, sw, sb, row0=nl, batch=batch, seq=ctx_len, out_rows=nc, out_row0=0)
        filt = (hy_f_w1[i], hy_f_b1[i], hy_f_freq1[i], hy_f_w2[i], hy_f_b2[i], hy_f_freq2[i], hy_f_w3[i])
        bias = hy_bias[i][None, :]
        taps, asum = _hy_filter(seq, *filt, hw)
        kr1, ki1 = _fft_first(taps.reshape(FFT_N1, ncols), consts["f1_real"], ncols)
        spectrum = _fft_mid(kr1, ki1, consts, n2, hw)
        ar, ai = _fft_first(u_l.reshape(-1, ncols), consts["f1_data"], ncols)
        br, bi = _fft_mid(ar, ai, consts, n2, hw, spectrum=spectrum)
        reps = min(FFT_COLS, ncols) // hw
        yh_lat = _fft_last(br.reshape(FFT_N1, ncols), bi.reshape(FFT_N1, ncols), consts["g1"],
                           u_l.reshape(-1, ncols), x0_l.reshape(-1, ncols),
                           jnp.tile(bias, (1, reps)), jnp.tile(1.0 / asum, (1, reps)), ncols)
        taps_c, asum_c = _hy_filter(ctx_len, *filt, hw)
        yh_ctx = _hy_ctx(u_c, x0_c, taps_c, bias, 1.0 / asum_c, small)

        j = i // 2
        moe = i % 2 == 1
        rw = None
        if moe:
            rw = jnp.concatenate([router_w[j], jnp.zeros((d, 128 - N_EXPERTS), F32)], axis=1)
        outs = _out_proj(xs, (yp_l, yh_lat.reshape(nl, hw), ya_l), (yp_c, yh_ctx, ya_c), w_out[i].astype(BF16),
                         norm2_g[i][None, :], mod, rw, **tiles)
        out_rows = nl if last else n
        if moe:
            xs, u, route = outs
            xs = _moe_layer(xs, u, route, mod, moe_wg, moe_wu, moe_wd, j, out_rows=out_rows, **tiles)
        else:
            xs, u = outs
            xs = _ffn(xs, u, mod, ffn_w_gate[j].astype(BF16), ffn_w_up[j].astype(BF16),
                      ffn_w_down[j].astype(BF16), out_rows=out_rows, **tiles)
    return xs[:nl].reshape(batch, seq, d)
```

```python
import functools
import math

import numpy as np
import jax
import jax.numpy as jnp
from jax import lax
from jax.experimental import pallas as pl
from jax.experimental.pallas import tpu as pltpu

F32 = jnp.float32
BF16 = jnp.bfloat16
EPS = 1e-6

GRID_W = 64
POOL_WINDOWS = (2, 4, 8, 16)
HEADS = 4
HEAD_DIM = 64
V_DIM = 128
ROPE_FREQS = 16
ROPE_THETA = 10000.0
HY_BANDS = 16
HY_TARGET = 1e-2
HY_FAST = 0.3
HY_SLOW = 1.5
N_EXPERTS = 8

ROW_TILE = 512
HALO = 32
FFT_N1 = 128
FFT_COLS = 2048
FFT_KB = 4
ATT_TQ = 1024
ATT_TK = 512
ATT_UNROLL = 8
QSCALE = (HEAD_DIM ** -0.5) * math.log2(math.e)
ATT_DIRECT_MAX = 100.0
CAST_STEPS = 64
MOE_TM = 512
MOE_FC = 1792
VMEM_LIMIT = 56 * 1024 * 1024


def _cparams(sem):
    return pltpu.CompilerParams(dimension_semantics=sem, vmem_limit_bytes=VMEM_LIMIT)


def _dot(a, b):
    return jnp.dot(a, b, preferred_element_type=F32)


def _split(a):
    hi = a.astype(BF16)
    lo = (a - hi.astype(F32)).astype(BF16)
    return hi, lo


def _dot3(a, b):
    ah, al = _split(a)
    bh, bl = _split(b)
    return _dot(ah, bh) + _dot(ah, bl) + _dot(al, bh)


def _dot3c(ch, cl, d):
    dh, dl = _split(d)
    return _dot(ch, dh) + _dot(ch, dl) + _dot(cl, dh)


def _silu(x):
    return x / (1.0 + jnp.exp(-x))


def _normmod(x, g, shift, scale):
    ms = jnp.mean(x * x, axis=-1, keepdims=True)
    return (x * lax.rsqrt(ms + EPS) * g) * (1.0 + scale) + shift


def _cond_row(t, n_lat_tiles, tiles_per_batch, n_cond):
    return jnp.where(t >= n_lat_tiles, n_cond - 1, t // tiles_per_batch)


def _cast_kernel(*refs):
    half = len(refs) // 2
    for w_ref, o_ref in zip(refs[:half], refs[half:]):
        o_ref[...] = w_ref[...].astype(BF16)


def _cast_bf16(*ws):
    flat = [w.reshape(-1, w.shape[-1]) for w in ws]
    steps = CAST_STEPS
    specs = [pl.BlockSpec((w.shape[0] // steps, w.shape[1]), lambda t: (t, 0)) for w in flat]
    outs = pl.pallas_call(
        _cast_kernel,
        grid=(steps,),
        in_specs=specs,
        out_specs=specs,
        out_shape=[jax.ShapeDtypeStruct(w.shape, BF16) for w in flat],
        compiler_params=_cparams(("arbitrary",)),
    )(*flat)
    return [o.reshape(w.shape) for o, w in zip(outs, ws)]


def _modvec_kernel(c_ref, w_ref, b_ref, o_ref):
    s = _silu(c_ref[...])
    o_ref[0, 0] = _dot3(s, w_ref[0]) + b_ref[0]


def _modvec(cond8, mod_w, mod_b):
    depth, d, six_d = mod_w.shape
    nchunk = six_d // d
    return pl.pallas_call(
        _modvec_kernel,
        grid=(depth, nchunk),
        in_specs=[
            pl.BlockSpec((8, d), lambda i, j: (0, 0)),
            pl.BlockSpec((1, d, d), lambda i, j: (i, 0, j)),
            pl.BlockSpec((1, 1, d), lambda i, j: (i, 0, j)),
        ],
        out_specs=pl.BlockSpec((1, 1, 8, d), lambda i, j: (i, j, 0, 0)),
        out_shape=jax.ShapeDtypeStruct((depth, nchunk, 8, d), F32),
        compiler_params=_cparams(("arbitrary", "arbitrary")),
    )(cond8, mod_w, mod_b.reshape(depth, 1, six_d))


def _in_proj_kernel(x_ref, g_ref, mod_ref, w_ref, cos_ref, sin_ref, gqk_ref,
                    zp_ref, zh_ref, qt_ref, k_ref, vt_ref, zatt_ref, *, nlt, tpb, ncond, pw, hw, ntiles):
    t = pl.program_id(0)

    @pl.when(t == 0)
    def _():
        zatt_ref[...] = jnp.zeros_like(zatt_ref)

    tm = x_ref.shape[0]
    att = pw + 3 * hw
    lane = lax.broadcasted_iota(jnp.int32, (tm, 2 * HEAD_DIM), 1)
    first = lane < HEAD_DIM
    apart = (lane % (2 * ROPE_FREQS)) < ROPE_FREQS
    cos = cos_ref[...]
    sin = sin_ref[...]
    qk_w = HEADS * 2 * HEAD_DIM

    def norm_rope(v, g):
        v2 = v * v
        s_all = jnp.sum(v2, axis=-1, keepdims=True)
        s_first = jnp.sum(jnp.where(first, v2, 0.0), axis=-1, keepdims=True)
        ms = jnp.where(first, s_first, s_all - s_first) * (1.0 / HEAD_DIM)
        vn = v * lax.rsqrt(ms + EPS) * g
        swapped = jnp.where(apart, pltpu.roll(vn, 2 * HEAD_DIM - ROPE_FREQS, 1), pltpu.roll(vn, ROPE_FREQS, 1))
        return vn * cos + swapped * sin

    for h in range(HEADS):
        lo = h * 2 * HEAD_DIM
        q = norm_rope(zatt_ref[:, lo:lo + 2 * HEAD_DIM], gqk_ref[0:1, :])
        qt_ref[h] = (q * QSCALE).T.astype(BF16)
        k = norm_rope(zatt_ref[:, qk_w + lo:qk_w + lo + 2 * HEAD_DIM], gqk_ref[1:2, :])
        k_ref[h] = k.astype(BF16)
        vlo = 2 * qk_w + h * V_DIM
        vt_ref[h] = zatt_ref[:, vlo:vlo + V_DIM].T.astype(BF16)

    ci = _cond_row(jnp.minimum(t, ntiles - 1), nlt, tpb, ncond)
    shift = mod_ref[0, pl.ds(ci, 1), :]
    scale = mod_ref[1, pl.ds(ci, 1), :]
    y = _normmod(x_ref[...], g_ref[...], shift, scale).astype(BF16)
    z = _dot(y, w_ref[...])
    zp_ref[...] = z[:, :pw].astype(BF16)
    zh_ref[...] = z[:, pw:att].astype(BF16)
    zatt_ref[...] = z[:, att:]


def _in_proj(x, g, mod, w, cos_t, sin_t, gqk, *, nlt, tpb, ncond, pw, hw):
    n, d = x.shape
    wid = w.shape[1]
    tm = ROW_TILE
    nt = n // tm
    kern = functools.partial(_in_proj_kernel, nlt=nlt, tpb=tpb, ncond=ncond, pw=pw, hw=hw, ntiles=nt)
    cur = lambda t: jnp.minimum(t, nt - 1)
    prv = lambda t: jnp.maximum(t - 1, 0)
    return pl.pallas_call(
        kern,
        grid=(nt + 1,),
        in_specs=[
            pl.BlockSpec((tm, d), lambda t: (cur(t), 0)),
            pl.BlockSpec((1, d), lambda t: (0, 0)),
            pl.BlockSpec(mod.shape, lambda t: (0, 0, 0)),
            pl.BlockSpec((d, wid), lambda t: (0, 0)),
            pl.BlockSpec((tm, 2 * HEAD_DIM), lambda t: (prv(t), 0)),
            pl.BlockSpec((tm, 2 * HEAD_DIM), lambda t: (prv(t), 0)),
            pl.BlockSpec((2, 2 * HEAD_DIM), lambda t: (0, 0)),
        ],
        out_specs=[
            pl.BlockSpec((tm, pw), lambda t: (cur(t), 0)),
            pl.BlockSpec((tm, 3 * hw), lambda t: (cur(t), 0)),
            pl.BlockSpec((HEADS, 2 * HEAD_DIM, tm), lambda t: (0, 0, prv(t))),
            pl.BlockSpec((HEADS, tm, 2 * HEAD_DIM), lambda t: (0, prv(t), 0)),
            pl.BlockSpec((HEADS, V_DIM, tm), lambda t: (0, 0, prv(t))),
        ],
        scratch_shapes=[pltpu.VMEM((tm, wid - pw - 3 * hw), F32)],
        out_shape=[
            jax.ShapeDtypeStruct((n, pw), BF16),
            jax.ShapeDtypeStruct((n, 3 * hw), BF16),
            jax.ShapeDtypeStruct((HEADS, 2 * HEAD_DIM, n), BF16),
            jax.ShapeDtypeStruct((HEADS, n, 2 * HEAD_DIM), BF16),
            jax.ShapeDtypeStruct((HEADS, V_DIM, n), BF16),
        ],
        compiler_params=_cparams(("arbitrary",)),
    )(x, g, mod, w, cos_t, sin_t, gqk)


def _attn_kernel(lam_ref, qt_ref, kc_ref, vtc_ref, *rest, tk, n_lat_chunks, lam_init):
    if n_lat_chunks:
        kl_ref, vtl_ref, g_ref, o_ref, m_ref, l_ref, acc_ref = rest
    else:
        g_ref, o_ref, m_ref, l_ref, acc_ref = rest
    qt = qt_ref[0]
    row = lax.broadcasted_iota(jnp.int32, qt.shape, 0)
    zero = jnp.zeros_like(qt)
    qmaps = (jnp.where(row < HEAD_DIM, qt, zero), jnp.where(row >= HEAD_DIM, qt, zero))

    def process(k_tile, vt_tile, first):
        for mi in range(2):
            s = _dot(k_tile, qmaps[mi])
            smax = jnp.max(s, axis=0, keepdims=True)
            if first:
                m_new = smax
            else:
                m_old = m_ref[mi]
                m_new = jnp.maximum(m_old, smax)
            p = jnp.exp2(s - m_new)
            psum = jnp.sum(p, axis=0, keepdims=True)
            pv = _dot(vt_tile, p.astype(BF16))
            if first:
                l_ref[mi] = psum
                acc_ref[mi] = pv
            else:
                alpha = jnp.exp2(m_old - m_new)
                l_ref[mi] = alpha * l_ref[mi] + psum
                acc_ref[mi] = alpha * acc_ref[mi] + pv
            m_ref[mi] = m_new

    process(kc_ref[0], vtc_ref[0], True)
    if n_lat_chunks:
        def body(j, carry):
            off = pl.multiple_of(j * tk, tk)
            process(kl_ref[0, pl.ds(off, tk), :], vtl_ref[0, :, pl.ds(off, tk)], False)
            return carry
        lax.fori_loop(0, n_lat_chunks, body, 0)

    lv = lam_ref[...]
    lam = (jnp.exp(jnp.sum(lv[0:1] * lv[1:2], axis=-1, keepdims=True))
           - jnp.exp(jnp.sum(lv[2:3] * lv[3:4], axis=-1, keepdims=True)) + lam_init)
    o = acc_ref[0] * (1.0 / l_ref[0]) - lam * (acc_ref[1] * (1.0 / l_ref[1]))
    ms = jnp.mean(o * o, axis=0, keepdims=True)
    y = o * lax.rsqrt(ms + EPS) * g_ref[...] * (1.0 - lam_init)
    o_ref[...] = y.T.astype(BF16)


def _attn_direct_kernel(lam_ref, qt_ref, kc_ref, vtc_ref, *rest, tk, n_lat_chunks, lam_init):
    if n_lat_chunks:
        kl_ref, vtl_ref, g_ref, o_ref, s_ref, p_ref, l_ref, acc_ref = rest
    else:
        g_ref, o_ref, s_ref, p_ref, l_ref, acc_ref = rest
    qt = qt_ref[0]
    tq = qt.shape[1]
    row = lax.broadcasted_iota(jnp.int32, qt.shape, 0)
    zero = jnp.zeros_like(qt)
    qmaps = (jnp.where(row < HEAD_DIM, qt, zero), jnp.where(row >= HEAD_DIM, qt, zero))

    def piece_rows(j):
        if isinstance(j, int):
            return pl.ds((j - 1) * tk, tk)
        return pl.ds(pl.multiple_of((j - 1) * tk, tk), tk)

    def k_piece(j):
        return kl_ref[0, piece_rows(j), :]

    def vt_piece(j):
        return vtc_ref[0] if isinstance(j, int) and j == 0 else vtl_ref[0, :, piece_rows(j)]

    def step(j, par):
        stage_a(k_piece(j + 1), 1 - par)
        stage_b(par)
        stage_c(vt_piece(j - 1), 1 - par)

    def stage_a(k_tile, slot):
        for mi in range(2):
            s_ref[slot, mi] = _dot(k_tile, qmaps[mi])

    def stage_b(slot):
        for mi in range(2):
            p = jnp.exp2(s_ref[slot, mi])
            l_ref[mi] = l_ref[mi] + jnp.sum(p.reshape(tk // 8, 8, tq), axis=0)
            p_ref[slot, mi] = p.astype(BF16)

    def stage_c(vt_tile, slot):
        for mi in range(2):
            acc_ref[mi] = acc_ref[mi] + _dot(vt_tile, p_ref[slot, mi])

    l_ref[...] = jnp.zeros_like(l_ref)
    acc_ref[...] = jnp.zeros_like(acc_ref)
    stage_a(kc_ref[0], 0)
    if n_lat_chunks == 0:
        stage_b(0)
        stage_c(vtc_ref[0], 0)
    else:
        stage_a(k_piece(1), 1)
        stage_b(0)
        for j in range(1, ATT_UNROLL):
            step(j, j % 2)

        def body(jj, carry):
            j = ATT_UNROLL * (jj + 1)
            for r in range(ATT_UNROLL):
                step(j + r, r % 2)
            return carry

        lax.fori_loop(0, n_lat_chunks // ATT_UNROLL - 1, body, 0)
        stage_b(0)
        stage_c(vt_piece(n_lat_chunks - 1), 1)
        stage_c(vt_piece(n_lat_chunks), 0)

    lv = lam_ref[...]
    lam = (jnp.exp(jnp.sum(lv[0:1] * lv[1:2], axis=-1, keepdims=True))
           - jnp.exp(jnp.sum(lv[2:3] * lv[3:4], axis=-1, keepdims=True)) + lam_init)
    l1 = jnp.sum(l_ref[0], axis=0, keepdims=True)
    l2 = jnp.sum(l_ref[1], axis=0, keepdims=True)
    o = acc_ref[0] * (1.0 / l1) - lam * (acc_ref[1] * (1.0 / l2))
    ms = jnp.mean(o * o, axis=0, keepdims=True)
    y = o * lax.rsqrt(ms + EPS) * g_ref[...] * (1.0 - lam_init)
    o_ref[...] = y.T.astype(BF16)


def _attention(lam_p, qt, k12, vt, g_b, *, batch, seq, ctx_len, latent, lam_init, direct):
    n = k12.shape[1]
    nl = batch * seq
    tq = ATT_TQ if latent else ctx_len
    nq = seq // tq if latent else 1
    tk = ctx_len if direct else ATT_TK
    assert seq % ((ATT_UNROLL if direct else 1) * tk) == 0
    qbase = 0 if latent else nl // tq

    def qrow(b, iq):
        return (b * nq + iq) if latent else (qbase + b)

    in_specs = [
        pl.BlockSpec(lam_p.shape, lambda b, h, iq: (0, 0)),
        pl.BlockSpec((1, 2 * HEAD_DIM, tq), lambda b, h, iq: (h, 0, qrow(b, iq))),
        pl.BlockSpec((1, ctx_len, 2 * HEAD_DIM), lambda b, h, iq: (h, nl // ctx_len + b, 0)),
        pl.BlockSpec((1, V_DIM, ctx_len), lambda b, h, iq: (h, 0, nl // ctx_len + b)),
    ]
    args = [lam_p, qt, k12, vt]
    if latent:
        in_specs += [
            pl.BlockSpec((1, seq, 2 * HEAD_DIM), lambda b, h, iq: (h, b, 0)),
            pl.BlockSpec((1, V_DIM, seq), lambda b, h, iq: (h, 0, b)),
        ]
        args += [k12, vt]
    in_specs.append(pl.BlockSpec((V_DIM, tq), lambda b, h, iq: (0, 0)))
    args.append(g_b[:, :tq])
    nrows = nl if latent else batch * ctx_len
    body = _attn_direct_kernel if direct else _attn_kernel
    kern = functools.partial(body, tk=tk, n_lat_chunks=(seq // tk if latent else 0), lam_init=lam_init)
    if direct:
        scratch = [pltpu.VMEM((2, 2, tk, tq), F32), pltpu.VMEM((2, 2, tk, tq), BF16),
                   pltpu.VMEM((2, 8, tq), F32), pltpu.VMEM((2, V_DIM, tq), F32)]
    else:
        scratch = [pltpu.VMEM((2, 1, tq), F32), pltpu.VMEM((2, 1, tq), F32), pltpu.VMEM((2, V_DIM, tq), F32)]
    return pl.pallas_call(
        kern,
        grid=(batch, HEADS, nq),
        in_specs=in_specs,
        out_specs=pl.BlockSpec((tq, V_DIM), lambda b, h, iq: (b * nq + iq, h)),
        out_shape=jax.ShapeDtypeStruct((nrows, HEADS * V_DIM), BF16),
        scratch_shapes=scratch,
        compiler_params=_cparams(("arbitrary", "arbitrary", "arbitrary")),
    )(*args)


def _halo_specs(width, *, row0, seq, tile):
    hb = tile // HALO

    def cur(b, i):
        return ((row0 + b * seq) // tile + i, 0)

    def prev(b, i):
        first = (row0 + b * seq) // HALO
        return (jnp.maximum(first + i * hb - 1, first), 0)

    def nxt(b, i):
        first = (row0 + b * seq) // HALO
        return (jnp.minimum(first + (i + 1) * hb, first + seq // HALO - 1), 0)

    return [pl.BlockSpec((HALO, width), prev), pl.BlockSpec((tile, width), cur), pl.BlockSpec((HALO, width), nxt)]


def _fill_ext(ext_ref, prev_ref, cur_ref, next_ref, *, seq, tile):
    pos0 = pl.program_id(1) * tile
    width = cur_ref.shape[1]
    hpos = lax.broadcasted_iota(jnp.int32, (HALO, width), 0)
    ext_ref[0:HALO, :] = jnp.where(pos0 - HALO + hpos >= 0, prev_ref[...].astype(F32), 0.0)
    ext_ref[HALO:HALO + tile, :] = cur_ref[...].astype(F32)
    ext_ref[HALO + tile:, :] = jnp.where(pos0 + tile + hpos < seq, next_ref[...].astype(F32), 0.0)


def _pool_kernel(prev_ref, cur_ref, next_ref, lin_ref, scale_ref, o_ref, ext_ref, s_ref, *, seq, tile):
    _fill_ext(ext_ref, prev_ref, cur_ref, next_ref, seq=seq, tile=tile)
    width = cur_ref.shape[1]
    gd = width // len(POOL_WINDOWS)
    n0 = tile + 2 * HALO
    s_ref[0, 0:n0 - 8, :] = ext_ref[0:n0 - 8, :] + ext_ref[1:n0 - 7, :]
    for k in range(1, len(POOL_WINDOWS)):
        step = 1 << k
        ln = n0 - 8 * (k + 1)
        s_ref[k, 0:ln, :] = s_ref[k - 1, 0:ln, :] + s_ref[k - 1, step:step + ln, :]
    lane = lax.broadcasted_iota(jnp.int32, (tile, width), 1)
    pos = pl.program_id(1) * tile + lax.broadcasted_iota(jnp.int32, (tile, width), 0)
    grp = lane // gd
    wsum = jnp.zeros((tile, width), F32)
    half = jnp.zeros((tile, width), jnp.int32)
    for k, win in enumerate(POOL_WINDOWS):
        start = HALO - win // 2
        wsum = jnp.where(grp == k, s_ref[k, start:start + tile, :], wsum)
        half = jnp.where(grp == k, win // 2, half)
    cnt = jnp.minimum(pos + half, seq) - jnp.maximum(pos - half, 0)
    z = ext_ref[HALO:HALO + tile, :]
    dlt = (wsum / cnt.astype(F32) - z).astype(BF16)
    o_ref[...] = (_dot(dlt, lin_ref[...]) * scale_ref[...]).astype(BF16)


def _pool(zp, lin_bd, scale, *, row0, batch, seq, out_rows, out_row0):
    width = zp.shape[1]
    tile = min(ROW_TILE, seq)
    kern = functools.partial(_pool_kernel, seq=seq, tile=tile)
    return pl.pallas_call(
        kern,
        grid=(batch, seq // tile),
        in_specs=_halo_specs(width, row0=row0, seq=seq, tile=tile) + [
            pl.BlockSpec((width, width), lambda b, i: (0, 0)),
            pl.BlockSpec((1, width), lambda b, i: (0, 0)),
        ],
        out_specs=pl.BlockSpec((tile, width), lambda b, i: ((out_row0 + b * seq) // tile + i, 0)),
        out_shape=jax.ShapeDtypeStruct((out_rows, width), BF16),
        scratch_shapes=[
            pltpu.VMEM((tile + 2 * HALO, width), F32),
            pltpu.VMEM((4, tile + 2 * HALO, width), F32),
        ],
        compiler_params=_cparams(("arbitrary", "arbitrary")),
    )(zp, zp, zp, lin_bd, scale)


def _hy_pre_kernel(prev_ref, cur_ref, next_ref, w_ref, b_ref, u_ref, x0_ref, ext_ref, *, seq, tile, hw):
    _fill_ext(ext_ref, prev_ref, cur_ref, next_ref, seq=seq, tile=tile)
    y = b_ref[...] + ext_ref[HALO - 1:HALO - 1 + tile, :] * w_ref[0:1, :]
    y = y + ext_ref[HALO:HALO + tile, :] * w_ref[1:2, :]
    y = y + ext_ref[HALO + 1:HALO + 1 + tile, :] * w_ref[2:3, :]
    x0_ref[...] = y[:, :hw]
    u_ref[...] = y[:, 2 * hw:] * y[:, hw:2 * hw]


def _hy_pre(zh, sw, sb, *, row0, batch, seq, out_rows, out_row0):
    width = zh.shape[1]
    hw = width // 3
    tile = min(ROW_TILE, seq)
    kern = functools.partial(_hy_pre_kernel, seq=seq, tile=tile, hw=hw)
    ospec = pl.BlockSpec((tile, hw), lambda b, i: ((out_row0 + b * seq) // tile + i, 0))
    return pl.pallas_call(
        kern,
        grid=(batch, seq // tile),
        in_specs=_halo_specs(width, row0=row0, seq=seq, tile=tile) + [
            pl.BlockSpec((3, width), lambda b, i: (0, 0)),
            pl.BlockSpec((1, width), lambda b, i: (0, 0)),
        ],
        out_specs=[ospec, ospec],
        out_shape=[jax.ShapeDtypeStruct((out_rows, hw), F32), jax.ShapeDtypeStruct((out_rows, hw), F32)],
        scratch_shapes=[pltpu.VMEM((tile + 2 * HALO, width), F32)],
        compiler_params=_cparams(("arbitrary", "arbitrary")),
    )(zh, zh, zh, sw, sb)


def _filter_kernel(feat_ref, w1_ref, b1_ref, f1_ref, w2_ref, b2_ref, f2_ref, w3_ref, dl_ref,
                   taps_ref, asum_ref, *, tile, hw, seq):
    feat = feat_ref[...]
    h = jnp.sin(f1_ref[...] * (_dot3(feat, w1_ref[...]) + b1_ref[...]))
    h = jnp.sin(f2_ref[...] * (_dot3(h, w2_ref[...]) + b2_ref[...]))
    h = _dot3(h, w3_ref[...])
    dec = jnp.exp(-feat[:, 0:1] * dl_ref[...])
    pos = pl.program_id(0) * tile + lax.broadcasted_iota(jnp.int32, (tile, hw), 0)
    taps = jnp.where(pos == seq, 0.0, h * dec)
    taps_ref[...] = taps
    part = jnp.sum(jnp.abs(taps), axis=0, keepdims=True)

    @pl.when(pl.program_id(0) == 0)
    def _():
        asum_ref[...] = part

    @pl.when(pl.program_id(0) != 0)
    def _():
        asum_ref[...] = asum_ref[...] + part


def _hy_filter(seq, w1, b1, f1, w2, b2, f2, w3, hw):
    t = jnp.linspace(0.0, 1.0, seq, dtype=F32)[:, None]
    w = (2.0 * math.pi / seq) * jnp.arange(seq, dtype=F32)[:, None]
    bands = jnp.linspace(1e-4, HY_BANDS - 1, HY_BANDS, dtype=F32)[None, :]
    emb = 1 + 2 * HY_BANDS
    embp = ((emb + 7) // 8) * 8
    feat = jnp.concatenate([t, jnp.cos(bands * w), -jnp.sin(bands * w), jnp.zeros((seq, embp - emb), F32)], axis=-1)
    w1p = jnp.concatenate([w1, jnp.zeros((embp - emb, w1.shape[1]), F32)], axis=0)
    max_decay = math.log(1.0 / HY_TARGET) / HY_FAST
    min_decay = math.log(1.0 / HY_TARGET) / HY_SLOW
    deltas = jnp.linspace(min_decay, max_decay, hw, dtype=F32)[None, :]
    feat = jnp.concatenate([feat, feat[0:1], jnp.flip(feat[1:], axis=0)], axis=0)
    tile = min(ROW_TILE, seq)
    nfwd = seq // tile
    hid = w1.shape[1]
    full = lambda shape: pl.BlockSpec(shape, lambda i: (0,) * len(shape))
    return pl.pallas_call(
        functools.partial(_filter_kernel, tile=tile, hw=hw, seq=seq),
        grid=(2 * nfwd,),
        in_specs=[
            pl.BlockSpec((tile, embp), lambda i: (i, 0)),
            full((embp, hid)), full((1, hid)), full((1, hid)),
            full((hid, hid)), full((1, hid)), full((1, hid)),
            pl.BlockSpec((hid, hw), lambda i: (0, i // nfwd)), full((1, hw)),
        ],
        out_specs=[
            pl.BlockSpec((tile, hw), lambda i: (i, 0)),
            pl.BlockSpec((1, hw), lambda i: (0, 0)),
        ],
        out_shape=[
            jax.ShapeDtypeStruct((2 * seq, hw), F32),
            jax.ShapeDtypeStruct((1, hw), F32),
        ],
        compiler_params=_cparams(("arbitrary",)),
    )(feat, w1p, b1[None, :], f1[None, :], w2, b2[None, :], f2[None, :], w3, deltas)


def _np_split(a):
    a32 = jnp.asarray(a, F32)
    hi = a32.astype(BF16)
    lo = (a32 - hi.astype(F32)).astype(BF16)
    return hi, lo


def _dft_consts(n2):
    n1 = FFT_N1
    n = n1 * n2
    half = n1 // 2
    a1 = -2.0 * np.pi * np.outer(np.arange(n1), np.arange(n1)) / n1
    f1r, f1i = np.cos(a1), np.sin(a1)
    f1_data = np.block([[f1r[:, :half], -f1i[:, :half]], [f1i[:, :half], f1r[:, :half]]])
    f1_real = np.concatenate([f1r, f1i], axis=0)
    g1r, g1i = f1r / n, -f1i / n
    g1 = np.block([[g1r[:half], -g1i[:half]], [g1i[:half], g1r[:half]]])
    a2 = -2.0 * np.pi * np.outer(np.arange(n2), np.arange(n2)) / n2
    f2r, f2i = np.cos(a2), np.sin(a2)
    f2 = np.block([[f2r, -f2i], [f2i, f2r]])
    g2 = np.block([[f2r, f2i], [-f2i, f2r]])
    at = -2.0 * np.pi * np.outer(np.arange(n1), np.arange(n2)) / n
    twr = jnp.broadcast_to(jnp.asarray(np.cos(at), F32)[:, :, None], (n1, n2, 128))
    twi = jnp.broadcast_to(jnp.asarray(np.sin(at), F32)[:, :, None], (n1, n2, 128))
    return dict(f1_data=_np_split(f1_data), f1_real=_np_split(f1_real), g1=_np_split(g1),
                f2=_np_split(f2), g2=_np_split(g2), twr=twr, twi=twi)


def _fft1_kernel(z_ref, fh_ref, fl_ref, ar_ref, ai_ref):
    a = _dot3c(fh_ref[...], fl_ref[...], z_ref[...])
    ar_ref[...] = a[:FFT_N1]
    ai_ref[...] = a[FFT_N1:]


def _fft_first(zview, fmat, ncols):
    cb = min(FFT_COLS, ncols)
    fh, fl = fmat
    cspec = pl.BlockSpec(fh.shape, lambda j: (0, 0))
    ospec = pl.BlockSpec((FFT_N1, cb), lambda j: (0, j))
    return pl.pallas_call(
        _fft1_kernel,
        grid=(ncols // cb,),
        in_specs=[pl.BlockSpec((FFT_N1, cb), lambda j: (0, j)), cspec, cspec],
        out_specs=[ospec, ospec],
        out_shape=[jax.ShapeDtypeStruct((FFT_N1, ncols), F32)] * 2,
        compiler_params=_cparams(("arbitrary",)),
    )(zview, fh, fl)


def _fftmid_kernel(ar_ref, ai_ref, twr_ref, twi_ref, f2h_ref, f2l_ref, *rest, filter_only, n2):
    kb, _, width = ar_ref.shape
    reps = width // 128
    lanes = lambda parts: jnp.concatenate(parts, axis=1)
    twr = lanes([t for j in range(kb) for t in [twr_ref[j]] * reps])
    twi = lanes([t for j in range(kb) for t in [twi_ref[j]] * reps])
    ar = lanes([ar_ref[j] for j in range(kb)])
    ai = lanes([ai_ref[j] for j in range(kb)])
    z = jnp.concatenate([ar * twr - ai * twi, ar * twi + ai * twr], axis=0)
    x = _dot3c(f2h_ref[...], f2l_ref[...], z)
    xr, xi = x[:n2], x[n2:]
    if filter_only:
        kr_out, ki_out = rest
        for j in range(kb):
            kr_out[j] = xr[:, j * width:(j + 1) * width]
            ki_out[j] = xi[:, j * width:(j + 1) * width]
        return
    kr_ref, ki_ref, g2h_ref, g2l_ref, br_out, bi_out = rest
    kr = lanes([kr_ref[j] for j in range(kb)])
    ki = lanes([ki_ref[j] for j in range(kb)])
    y = jnp.concatenate([xr * kr - xi * ki, xr * ki + xi * kr], axis=0)
    w = _dot3c(g2h_ref[...], g2l_ref[...], y)
    wr, wi = w[:n2], w[n2:]
    br = wr * twr + wi * twi
    bi = wi * twr - wr * twi
    for j in range(kb):
        br_out[j] = br[:, j * width:(j + 1) * width]
        bi_out[j] = bi[:, j * width:(j + 1) * width]


def _fft_mid(ar, ai, consts, n2, width, spectrum=None):
    a3r = ar.reshape(FFT_N1, n2, width)
    a3i = ai.reshape(FFT_N1, n2, width)
    blk = pl.BlockSpec((FFT_KB, n2, width), lambda k: (k, 0, 0))
    twspec = pl.BlockSpec((FFT_KB, n2, 128), lambda k: (k, 0, 0))
    cspec = pl.BlockSpec((2 * n2, 2 * n2), lambda k: (0, 0))
    in_specs = [blk, blk, twspec, twspec, cspec, cspec]
    args = [a3r, a3i, consts["twr"], consts["twi"], *consts["f2"]]
    if spectrum is not None:
        in_specs += [blk, blk, cspec, cspec]
        args += [spectrum[0], spectrum[1], *consts["g2"]]
    return pl.pallas_call(
        functools.partial(_fftmid_kernel, filter_only=spectrum is None, n2=n2),
        grid=(FFT_N1 // FFT_KB,),
        in_specs=in_specs,
        out_specs=[blk, blk],
        out_shape=[jax.ShapeDtypeStruct((FFT_N1, n2, width), F32)] * 2,
        compiler_params=_cparams(("arbitrary",)),
    )(*args)


def _fftlast_kernel(br_ref, bi_ref, gh_ref, gl_ref, u_ref, x0_ref, bias_ref, invn_ref, o_ref):
    b = jnp.concatenate([br_ref[...], bi_ref[...]], axis=0)
    y = _dot3c(gh_ref[...], gl_ref[...], b)
    o_ref[...] = ((y * invn_ref[...] + u_ref[...] * bias_ref[...]) * x0_ref[...]).astype(BF16)


def _fft_last(br, bi, gmat, uview, x0view, bias_t, invn_t, ncols):
    cb = min(FFT_COLS, ncols)
    gh, gl = gmat
    cspec = pl.BlockSpec(gh.shape, lambda j: (0, 0))
    dspec = pl.BlockSpec((FFT_N1, cb), lambda j: (0, j))
    vspec = pl.BlockSpec((1, cb), lambda j: (0, 0))
    return pl.pallas_call(
        _fftlast_kernel,
        grid=(ncols // cb,),
        in_specs=[dspec, dspec, cspec, cspec, dspec, dspec, vspec, vspec],
        out_specs=dspec,
        out_shape=jax.ShapeDtypeStruct((FFT_N1, ncols), BF16),
        compiler_params=_cparams(("arbitrary",)),
    )(br, bi, gh, gl, uview, x0view, bias_t, invn_t)


def _dft_small_consts(seq):
    n = 2 * seq
    a = -2.0 * np.pi * np.outer(np.arange(n), np.arange(n)) / n
    fr, fi = np.cos(a), np.sin(a)
    f_data = np.block([[fr[:, :seq], -fi[:, :seq]], [fi[:, :seq], fr[:, :seq]]])
    f_real = np.concatenate([fr, fi], axis=0)
    gr, gi = fr / n, -fi / n
    g = np.block([[gr[:seq], -gi[:seq]], [gi[:seq], gr[:seq]]])
    return _np_split(f_data), _np_split(f_real), _np_split(g)


def _hy_ctx_kernel(u_ref, x0_ref, taps_ref, fdh_ref, fdl_ref, frh_ref, frl_ref, gh_ref, gl_ref,
                   bias_ref, invn_ref, o_ref):
    n = taps_ref.shape[0]
    z = u_ref[...]
    a = _dot3c(fdh_ref[...], fdl_ref[...], z)
    k = _dot3c(frh_ref[...], frl_ref[...], taps_ref[...])
    ar, ai, kr, ki = a[:n], a[n:], k[:n], k[n:]
    y = jnp.concatenate([ar * kr - ai * ki, ar * ki + ai * kr], axis=0)
    conv = _dot3c(gh_ref[...], gl_ref[...], y)
    o_ref[...] = ((conv * invn_ref[...] + z * bias_ref[...]) * x0_ref[...]).astype(BF16)


def _hy_ctx(u, x0, taps, bias, invn, small):
    rows, hw = u.shape
    full = lambda a: pl.BlockSpec(a.shape, lambda i: (0,) * a.ndim)
    args = [u, x0, taps, *small[0], *small[1], *small[2], bias, invn]
    return pl.pallas_call(
        _hy_ctx_kernel,
        grid=(1,),
        in_specs=[full(a) for a in args],
        out_specs=pl.BlockSpec((rows, hw), lambda i: (0, 0)),
        out_shape=jax.ShapeDtypeStruct((rows, hw), BF16),
        compiler_params=_cparams(("arbitrary",)),
    )(*args)


def _out_proj_kernel(x_ref, ypl_ref, yhl_ref, yal_ref, ypc_ref, yhc_ref, yac_ref, w_ref, g_ref, mod_ref, *rest,
                     nlt, tpb, ncond, pw, hw, moe):
    if moe:
        rw_ref, xo_ref, u_ref, route_ref = rest
    else:
        xo_ref, u_ref = rest
    ci = _cond_row(pl.program_id(0), nlt, tpb, ncond)
    is_ctx = pl.program_id(0) >= nlt
    yp = jnp.where(is_ctx, ypc_ref[...], ypl_ref[...])
    yh = jnp.where(is_ctx, yhc_ref[...], yhl_ref[...])
    ya = jnp.where(is_ctx, yac_ref[...], yal_ref[...])
    mix = _dot(yp, w_ref[0:pw, :]) + _dot(yh, w_ref[pw:pw + hw, :]) + _dot(ya, w_ref[pw + hw:, :])
    x = x_ref[...] + mod_ref[2, pl.ds(ci, 1), :] * mix
    xo_ref[...] = x
    un = _normmod(x, g_ref[...], mod_ref[3, pl.ds(ci, 1), :], mod_ref[4, pl.ds(ci, 1), :])
    u_ref[...] = un.astype(BF16)
    if moe:
        logits = _dot3(un, rw_ref[...])
        lane = lax.broadcasted_iota(jnp.int32, logits.shape, 1)
        neg = jnp.float32(-jnp.inf)
        lg = jnp.where(lane < N_EXPERTS, logits, neg)
        t1 = jnp.max(lg, axis=-1, keepdims=True)
        i1 = jnp.min(jnp.where(lg == t1, lane, 128), axis=-1, keepdims=True)
        lg2 = jnp.where(lane == i1, neg, lg)
        t2 = jnp.max(lg2, axis=-1, keepdims=True)
        i2 = jnp.min(jnp.where(lg2 == t2, lane, 128), axis=-1, keepdims=True)
        e2 = jnp.exp(t2 - t1)
        g1 = 1.0 / (1.0 + e2)
        g2 = e2 / (1.0 + e2)
        route_ref[...] = jnp.where(lane == 0, i1.astype(F32), jnp.where(lane == 1, i2.astype(F32),
                                   jnp.where(lane == 2, g1, jnp.where(lane == 3, g2, 0.0))))


def _out_proj(x, lat, ctx, w, g, mod, rw, *, nlt, tpb, ncond):
    n, d = x.shape
    pw, hw, aw = (a.shape[1] for a in lat)
    tm = ROW_TILE
    moe = rw is not None
    row = lambda width: pl.BlockSpec((tm, width), lambda t: (t, 0))
    lrow = lambda width: pl.BlockSpec((tm, width), lambda t: (jnp.minimum(t, nlt - 1), 0))
    crow = lambda width: pl.BlockSpec((tm, width), lambda t: (jnp.maximum(t - nlt, 0), 0))
    in_specs = [row(d), lrow(pw), lrow(hw), lrow(aw), crow(pw), crow(hw), crow(aw),
                pl.BlockSpec(w.shape, lambda t: (0, 0)),
                pl.BlockSpec((1, d), lambda t: (0, 0)),
                pl.BlockSpec(mod.shape, lambda t: (0, 0, 0))]
    args = [x, *lat, *ctx, w, g, mod]
    out_specs = [row(d), row(d)]
    out_shape = [jax.ShapeDtypeStruct((n, d), F32), jax.ShapeDtypeStruct((n, d), BF16)]
    if moe:
        in_specs.append(pl.BlockSpec(rw.shape, lambda t: (0, 0)))
        args.append(rw)
        out_specs.append(row(128))
        out_shape.append(jax.ShapeDtypeStruct((n, 128), F32))
    kern = functools.partial(_out_proj_kernel, nlt=nlt, tpb=tpb, ncond=ncond, pw=pw, hw=hw, moe=moe)
    return pl.pallas_call(
        kern, grid=(n // tm,), in_specs=in_specs, out_specs=out_specs, out_shape=out_shape,
        compiler_params=_cparams(("arbitrary",)),
    )(*args)


def _ffn_kernel(x_ref, u_ref, mod_ref, wg_ref, wu_ref, wd_ref, o_ref, *, nlt, tpb, ncond, nchunk):
    ci = _cond_row(pl.program_id(0), nlt, tpb, ncond)
    u = u_ref[...]
    ff = wg_ref.shape[1]
    fc = ff // nchunk
    y = jnp.zeros(x_ref.shape, F32)
    for c in range(nchunk):
        gate = _dot(u, wg_ref[:, c * fc:(c + 1) * fc])
        up = _dot(u, wu_ref[:, c * fc:(c + 1) * fc])
        y = y + _dot((_silu(gate) * up).astype(BF16), wd_ref[c * fc:(c + 1) * fc, :])
    o_ref[...] = x_ref[...] + mod_ref[5, pl.ds(ci, 1), :] * y


def _ffn(x, u, mod, wg, wu, wd, *, nlt, tpb, ncond, out_rows):
    n, d = x.shape
    ff = wg.shape[1]
    tm = ROW_TILE
    nchunk = 2 if (ff // 2) % 128 == 0 else 1
    row = lambda dt: pl.BlockSpec((tm, d), lambda t: (t, 0))
    const = lambda shape: pl.BlockSpec(shape, lambda t: (0,) * len(shape), pipeline_mode=pl.Buffered(1))
    kern = functools.partial(_ffn_kernel, nlt=nlt, tpb=tpb, ncond=ncond, nchunk=nchunk)
    return pl.pallas_call(
        kern,
        grid=(out_rows // tm,),
        in_specs=[row(F32), row(BF16), pl.BlockSpec(mod.shape, lambda t: (0, 0, 0)),
                  const((d, ff)), const((d, ff)), const((ff, d))],
        out_specs=row(F32),
        out_shape=jax.ShapeDtypeStruct((out_rows, d), F32),
        compiler_params=_cparams(("arbitrary",)),
    )(x, u, mod, wg, wu, wd)


def _moe_kernel(te_ref, tv_ref, u_ref, wg_ref, wu_ref, wd_ref, o_ref, acc_ref):
    t = pl.program_id(0)
    f = pl.program_id(1)

    @pl.when(tv_ref[t] > 0)
    def _():
        u = u_ref[...]
        h = (_silu(_dot(u, wg_ref[0, 0])) * _dot(u, wu_ref[0, 0])).astype(BF16)
        y = _dot(h, wd_ref[0, 0])

        @pl.when(f == 0)
        def _():
            acc_ref[...] = y

        @pl.when(f != 0)
        def _():
            acc_ref[...] = acc_ref[...] + y

        @pl.when(f == pl.num_programs(1) - 1)
        def _():
            o_ref[...] = acc_ref[...].astype(BF16)


def _moe_experts(ug, tile_expert, tile_valid, wg, wu, wd, j):
    p, d = ug.shape
    ff = wg.shape[3]
    tm, fc = MOE_TM, MOE_FC
    grid_spec = pltpu.PrefetchScalarGridSpec(
        num_scalar_prefetch=2,
        grid=(p // tm, ff // fc),
        in_specs=[
            pl.BlockSpec((tm, d), lambda t, f, te, tv: (t, 0)),
            pl.BlockSpec((1, 1, d, fc), lambda t, f, te, tv: (j, te[t], 0, f)),
            pl.BlockSpec((1, 1, d, fc), lambda t, f, te, tv: (j, te[t], 0, f)),
            pl.BlockSpec((1, 1, fc, d), lambda t, f, te, tv: (j, te[t], f, 0)),
        ],
        out_specs=pl.BlockSpec((tm, d), lambda t, f, te, tv: (t, 0)),
        scratch_shapes=[pltpu.VMEM((tm, d), F32)],
    )
    return pl.pallas_call(
        _moe_kernel, grid_spec=grid_spec,
        out_shape=jax.ShapeDtypeStruct((p, d), BF16),
        compiler_params=_cparams(("arbitrary", "arbitrary")),
    )(tile_expert, tile_valid, ug, wg, wu, wd)


def _moe_combine_kernel(x_ref, ya_ref, yb_ref, route_ref, mod_ref, o_ref, *, nlt, tpb, ncond):
    ci = _cond_row(pl.program_id(0), nlt, tpb, ncond)
    r = route_ref[...]
    lane = lax.broadcasted_iota(jnp.int32, r.shape, 1)
    g1 = jnp.sum(jnp.where(lane == 2, r, 0.0), axis=-1, keepdims=True)
    g2 = jnp.sum(jnp.where(lane == 3, r, 0.0), axis=-1, keepdims=True)
    y = g1 * ya_ref[...].astype(F32) + g2 * yb_ref[...].astype(F32)
    o_ref[...] = x_ref[...] + mod_ref[5, pl.ds(ci, 1), :] * y


def _moe_combine(x, ya, yb, route, mod, *, nlt, tpb, ncond, out_rows):
    n, d = x.shape
    tm = ROW_TILE
    row = lambda width: pl.BlockSpec((tm, width), lambda t: (t, 0))
    kern = functools.partial(_moe_combine_kernel, nlt=nlt, tpb=tpb, ncond=ncond)
    return pl.pallas_call(
        kern,
        grid=(out_rows // tm,),
        in_specs=[row(d), row(d), row(d), row(128), pl.BlockSpec(mod.shape, lambda t: (0, 0, 0))],
        out_specs=row(d),
        out_shape=jax.ShapeDtypeStruct((out_rows, d), F32),
        compiler_params=_cparams(("arbitrary",)),
    )(x, ya, yb, route, mod)


def _moe_layer(x, u, route, mod, wg, wu, wd, j, *, nlt, tpb, ncond, out_rows):
    n, d = x.shape
    tm = MOE_TM
    experts = jnp.concatenate([route[:, 0], route[:, 1]]).astype(jnp.int32)
    onehot = (experts[:, None] == jnp.arange(N_EXPERTS, dtype=jnp.int32)[None, :]).astype(jnp.int32)
    rank = jnp.sum(onehot * (jnp.cumsum(onehot, axis=0) - 1), axis=1)
    counts = jnp.sum(onehot, axis=0)
    padded = ((counts + tm - 1) // tm) * tm
    ends = jnp.cumsum(padded)
    starts = ends - padded
    dest = starts[experts] + rank
    p = 2 * n + N_EXPERTS * tm
    tokens = jnp.concatenate([jnp.arange(n, dtype=jnp.int32)] * 2)
    inb = dict(mode="promise_in_bounds")
    src = jnp.zeros((p,), jnp.int32).at[dest].set(tokens, unique_indices=True, **inb)
    tile_start = jnp.arange(p // tm, dtype=jnp.int32) * tm
    tile_expert = jnp.minimum(jnp.searchsorted(ends, tile_start, side="right"), N_EXPERTS - 1).astype(jnp.int32)
    tile_valid = (tile_start < ends[-1]).astype(jnp.int32)
    ys = _moe_experts(u.at[src].get(**inb), tile_expert, tile_valid, wg, wu, wd, j)
    ya = ys.at[dest[:n]].get(**inb)
    yb = ys.at[dest[n:]].get(**inb)
    return _moe_combine(x, ya, yb, route, mod, nlt=nlt, tpb=tpb, ncond=ncond, out_rows=out_rows)


def _rope_tables(seq, n_ctx_rows, batch):
    rows = seq // GRID_W
    row = jnp.repeat(jnp.arange(rows, dtype=F32), GRID_W)
    col = jnp.broadcast_to(jnp.arange(GRID_W, dtype=F32), (rows, GRID_W)).reshape(-1)
    inv_freq = jnp.power(ROPE_THETA, -jnp.arange(ROPE_FREQS, dtype=F32) / ROPE_FREQS)
    ar = row[:, None] * inv_freq
    ac = col[:, None] * inv_freq
    cos = jnp.concatenate([jnp.cos(ar), jnp.cos(ar), jnp.cos(ac), jnp.cos(ac)], axis=-1)
    sin = jnp.concatenate([-jnp.sin(ar), jnp.sin(ar), -jnp.sin(ac), jnp.sin(ac)], axis=-1)
    cos = jnp.tile(jnp.concatenate([cos, cos], axis=-1), (batch, 1))
    sin = jnp.tile(jnp.concatenate([sin, sin], axis=-1), (batch, 1))
    cos = jnp.concatenate([cos, jnp.ones((n_ctx_rows, 2 * HEAD_DIM), F32)], axis=0)
    sin = jnp.concatenate([sin, jnp.zeros((n_ctx_rows, 2 * HEAD_DIM), F32)], axis=0)
    return cos, sin


def kernel(x, c, ctx, c_ctx, mod_w, mod_b, norm1_g, norm2_g, w_in, w_out, pool_lin, pool_scale, hy_short_w, hy_short_b, hy_f_w1, hy_f_b1, hy_f_freq1, hy_f_w2, hy_f_b2, hy_f_freq2, hy_f_w3, hy_bias, qk_norm_g, diff_lambda, subln_g, ffn_w_gate, ffn_w_up, ffn_w_down, router_w, moe_w_gate, moe_w_up, moe_w_down):
    batch, seq, d = x.shape
    ctx_len = ctx.shape[1]
    depth = mod_w.shape[0]
    pw = pool_scale.shape[1]
    hw = hy_bias.shape[1]
    nl, nc = batch * seq, batch * ctx_len
    n = nl + nc
    tm = ROW_TILE
    assert seq % tm == 0 and nc % tm == 0 and seq % (FFT_N1 // 2) == 0 and seq % GRID_W == 0
    assert d == HEADS * 2 * V_DIM and pw == hw and batch == 2
    n2 = 2 * seq // FFT_N1
    ncols = n2 * hw
    assert (n * hw) % ncols == 0
    tiles = dict(nlt=nl // tm, tpb=seq // tm, ncond=batch + 1)

    xs = jnp.concatenate([x.reshape(nl, d), ctx.reshape(nc, d)], axis=0)
    cond8 = jnp.concatenate([c, c_ctx[None, :], jnp.zeros((8 - batch - 1, d), F32)], axis=0)
    mods = _modvec(cond8, mod_w, mod_b)
    cos_t, sin_t = _rope_tables(seq, nc, batch)
    consts = _dft_consts(n2)
    small = _dft_small_consts(ctx_len)
    eye = jnp.eye(len(POOL_WINDOWS), dtype=F32)
    moe_wg, moe_wu, moe_wd = _cast_bf16(moe_w_gate, moe_w_up, moe_w_down)

    for i in range(depth):
        last = i == depth - 1
        lam_init = 0.8 - 0.6 * math.exp(-0.3 * i)
        mod = mods[i]
        gqk = jnp.concatenate([qk_norm_g[i], qk_norm_g[i]], axis=-1)
        zp, zh, qt, k12, vt = _in_proj(xs, norm1_g[i][None, :], mod, w_in[i].astype(BF16), cos_t, sin_t, gqk,
                                       pw=pw, hw=hw, **tiles)

        g_b = jnp.broadcast_to(subln_g[i][:, None], (V_DIM, ATT_TQ))
        att_kw = dict(batch=batch, seq=seq, ctx_len=ctx_len, lam_init=lam_init)

        def attend(direct, lam_p=diff_lambda[i], qt=qt, k12=k12, vt=vt, g_b=g_b, att_kw=att_kw):
            return (_attention(lam_p, qt, k12, vt, g_b, latent=True, direct=direct, **att_kw),
                    _attention(lam_p, qt, k12, vt, g_b, latent=False, direct=direct, **att_kw))

        bound = (HEAD_DIM * QSCALE) * jnp.max(jnp.abs(qk_norm_g[i][0])) * jnp.max(jnp.abs(qk_norm_g[i][1]))
        ya_l, ya_c = lax.cond(bound * 1.02 < ATT_DIRECT_MAX, lambda: attend(True), lambda: attend(False))

        lin_bd = (eye[:, None, :, None] * pool_lin[i][:, :, None, :]).reshape(pw, pw).astype(BF16)
        pscale = pool_scale[i][None, :]
        yp_l = _pool(zp, lin_bd, pscale, row0=0, batch=batch, seq=seq, out_rows=nl, out_row0=0)
        yp_c = _pool(zp, lin_bd, pscale, row0=nl, batch=batch, seq=ctx_len, out_rows=nc, out_row0=0)

        sw, sb = hy_short_w[i], hy_short_b[i][None, :]
        u_l, x0_l = _hy_pre(zh, sw, sb, row0=0, batch=batch, seq=seq, out_rows=n, out_row0=0)
        u_c, x0_c = _hy_pre(zh, sw, sb, row0=nl, batch=batch, seq=ctx_len, out_rows=nc, out_row0=0)
        filt = (hy_f_w1[i], hy_f_b1[i], hy_f_freq1[i], hy_f_w2[i], hy_f_b2[i], hy_f_freq2[i], hy_f_w3[i])
        bias = hy_bias[i][None, :]
        taps, asum = _hy_filter(seq, *filt, hw)
        kr1, ki1 = _fft_first(taps.reshape(FFT_N1, ncols), consts["f1_real"], ncols)
        spectrum = _fft_mid(kr1, ki1, consts, n2, hw)
        ar, ai = _fft_first(u_l.reshape(-1, ncols), consts["f1_data"], ncols)
        br, bi = _fft_mid(ar, ai, consts, n2, hw, spectrum=spectrum)
        reps = min(FFT_COLS, ncols) // hw
        yh_lat = _fft_last(br.reshape(FFT_N1, ncols), bi.reshape(FFT_N1, ncols), consts["g1"],
                           u_l.reshape(-1, ncols), x0_l.reshape(-1, ncols),
                           jnp.tile(bias, (1, reps)), jnp.tile(1.0 / asum, (1, reps)), ncols)
        taps_c, asum_c = _hy_filter(ctx_len, *filt, hw)
        yh_ctx = _hy_ctx(u_c, x0_c, taps_c, bias, 1.0 / asum_c, small)

        j = i // 2
        moe = i % 2 == 1
        rw = None
        if moe:
            rw = jnp.concatenate([router_w[j], jnp.zeros((d, 128 - N_EXPERTS), F32)], axis=1)
        outs = _out_proj(xs, (yp_l, yh_lat.reshape(nl, hw), ya_l), (yp_c, yh_ctx, ya_c), w_out[i].astype(BF16),
                         norm2_g[i][None, :], mod, rw, **tiles)
        out_rows = nl if last else n
        if moe:
            xs, u, route = outs
            xs = _moe_layer(xs, u, route, mod, moe_wg, moe_wu, moe_wd, j, out_rows=out_rows, **tiles)
        else:
            xs, u = outs
            xs = _ffn(xs, u, mod, ffn_w_gate[j].astype(BF16), ffn_w_up[j].astype(BF16),
                      ffn_w_down[j].astype(BF16), out_rows=out_rows, **tiles)
    return xs[:nl].reshape(batch, seq, d)
```

```python
import functools
import math

import numpy as np
import jax
import jax.numpy as jnp
from jax import lax
from jax.experimental import pallas as pl
from jax.experimental.pallas import tpu as pltpu

F32 = jnp.float32
BF16 = jnp.bfloat16
EPS = 1e-6

GRID_W = 64
POOL_WINDOWS = (2, 4, 8, 16)
HEADS = 4
HEAD_DIM = 64
V_DIM = 128
ROPE_FREQS = 16
ROPE_THETA = 10000.0
HY_BANDS = 16
HY_TARGET = 1e-2
HY_FAST = 0.3
HY_SLOW = 1.5
N_EXPERTS = 8

ROW_TILE = 512
HALO = 32
FFT_N1 = 128
FFT_COLS = 2048
FFT_KB = 4
ATT_TQ = 1024
ATT_TK = 512
ATT_UNROLL = 16
QSCALE = (HEAD_DIM ** -0.5) * math.log2(math.e)
ATT_DIRECT_MAX = 100.0
CAST_STEPS = 64
MOE_TM = 512
MOE_FC = 1792
VMEM_LIMIT = 56 * 1024 * 1024


def _cparams(sem):
    return pltpu.CompilerParams(dimension_semantics=sem, vmem_limit_bytes=VMEM_LIMIT)


def _dot(a, b):
    return jnp.dot(a, b, preferred_element_type=F32)


def _split(a):
    hi = a.astype(BF16)
    lo = (a - hi.astype(F32)).astype(BF16)
    return hi, lo


def _dot3(a, b):
    ah, al = _split(a)
    bh, bl = _split(b)
    return _dot(ah, bh) + _dot(ah, bl) + _dot(al, bh)


def _dot3c(ch, cl, d):
    dh, dl = _split(d)
    return _dot(ch, dh) + _dot(ch, dl) + _dot(cl, dh)


def _silu(x):
    return x / (1.0 + jnp.exp(-x))


def _normmod(x, g, shift, scale):
    ms = jnp.mean(x * x, axis=-1, keepdims=True)
    return (x * lax.rsqrt(ms + EPS) * g) * (1.0 + scale) + shift


def _cond_row(t, n_lat_tiles, tiles_per_batch, n_cond):
    return jnp.where(t >= n_lat_tiles, n_cond - 1, t // tiles_per_batch)


def _cast_kernel(*refs):
    half = len(refs) // 2
    for w_ref, o_ref in zip(refs[:half], refs[half:]):
        o_ref[...] = w_ref[...].astype(BF16)


def _cast_bf16(*ws):
    flat = [w.reshape(-1, w.shape[-1]) for w in ws]
    steps = CAST_STEPS
    specs = [pl.BlockSpec((w.shape[0] // steps, w.shape[1]), lambda t: (t, 0)) for w in flat]
    outs = pl.pallas_call(
        _cast_kernel,
        grid=(steps,),
        in_specs=specs,
        out_specs=specs,
        out_shape=[jax.ShapeDtypeStruct(w.shape, BF16) for w in flat],
        compiler_params=_cparams(("arbitrary",)),
    )(*flat)
    return [o.reshape(w.shape) for o, w in zip(outs, ws)]


def _modvec_kernel(c_ref, w_ref, b_ref, o_ref):
    s = _silu(c_ref[...])
    o_ref[0, 0] = _dot3(s, w_ref[0]) + b_ref[0]


def _modvec(cond8, mod_w, mod_b):
    depth, d, six_d = mod_w.shape
    nchunk = six_d // d
    return pl.pallas_call(
        _modvec_kernel,
        grid=(depth, nchunk),
        in_specs=[
            pl.BlockSpec((8, d), lambda i, j: (0, 0)),
            pl.BlockSpec((1, d, d), lambda i, j: (i, 0, j)),
            pl.BlockSpec((1, 1, d), lambda i, j: (i, 0, j)),
        ],
        out_specs=pl.BlockSpec((1, 1, 8, d), lambda i, j: (i, j, 0, 0)),
        out_shape=jax.ShapeDtypeStruct((depth, nchunk, 8, d), F32),
        compiler_params=_cparams(("arbitrary", "arbitrary")),
    )(cond8, mod_w, mod_b.reshape(depth, 1, six_d))


def _in_proj_kernel(x_ref, g_ref, mod_ref, w_ref, cos_ref, sin_ref, gqk_ref,
                    zp_ref, zh_ref, qt_ref, k_ref, vt_ref, zatt_ref, *, nlt, tpb, ncond, pw, hw, ntiles):
    t = pl.program_id(0)

    @pl.when(t == 0)
    def _():
        zatt_ref[...] = jnp.zeros_like(zatt_ref)

    tm = x_ref.shape[0]
    att = pw + 3 * hw
    lane = lax.broadcasted_iota(jnp.int32, (tm, 2 * HEAD_DIM), 1)
    first = lane < HEAD_DIM
    apart = (lane % (2 * ROPE_FREQS)) < ROPE_FREQS
    cos = cos_ref[...]
    sin = sin_ref[...]
    qk_w = HEADS * 2 * HEAD_DIM

    def norm_rope(v, g):
        v2 = v * v
        s_all = jnp.sum(v2, axis=-1, keepdims=True)
        s_first = jnp.sum(jnp.where(first, v2, 0.0), axis=-1, keepdims=True)
        ms = jnp.where(first, s_first, s_all - s_first) * (1.0 / HEAD_DIM)
        vn = v * lax.rsqrt(ms + EPS) * g
        swapped = jnp.where(apart, pltpu.roll(vn, 2 * HEAD_DIM - ROPE_FREQS, 1), pltpu.roll(vn, ROPE_FREQS, 1))
        return vn * cos + swapped * sin

    for h in range(HEADS):
        lo = h * 2 * HEAD_DIM
        q = norm_rope(zatt_ref[:, lo:lo + 2 * HEAD_DIM], gqk_ref[0:1, :])
        qt_ref[h] = (q * QSCALE).T.astype(BF16)
        k = norm_rope(zatt_ref[:, qk_w + lo:qk_w + lo + 2 * HEAD_DIM], gqk_ref[1:2, :])
        k_ref[h] = k.astype(BF16)
        vlo = 2 * qk_w + h * V_DIM
        vt_ref[h] = zatt_ref[:, vlo:vlo + V_DIM].T.astype(BF16)

    ci = _cond_row(jnp.minimum(t, ntiles - 1), nlt, tpb, ncond)
    shift = mod_ref[0, pl.ds(ci, 1), :]
    scale = mod_ref[1, pl.ds(ci, 1), :]
    y = _normmod(x_ref[...], g_ref[...], shift, scale).astype(BF16)
    z = _dot(y, w_ref[...])
    zp_ref[...] = z[:, :pw].astype(BF16)
    zh_ref[...] = z[:, pw:att].astype(BF16)
    zatt_ref[...] = z[:, att:]


def _in_proj(x, g, mod, w, cos_t, sin_t, gqk, *, nlt, tpb, ncond, pw, hw):
    n, d = x.shape
    wid = w.shape[1]
    tm = ROW_TILE
    nt = n // tm
    kern = functools.partial(_in_proj_kernel, nlt=nlt, tpb=tpb, ncond=ncond, pw=pw, hw=hw, ntiles=nt)
    cur = lambda t: jnp.minimum(t, nt - 1)
    prv = lambda t: jnp.maximum(t - 1, 0)
    return pl.pallas_call(
        kern,
        grid=(nt + 1,),
        in_specs=[
            pl.BlockSpec((tm, d), lambda t: (cur(t), 0)),
            pl.BlockSpec((1, d), lambda t: (0, 0)),
            pl.BlockSpec(mod.shape, lambda t: (0, 0, 0)),
            pl.BlockSpec((d, wid), lambda t: (0, 0)),
            pl.BlockSpec((tm, 2 * HEAD_DIM), lambda t: (prv(t), 0)),
            pl.BlockSpec((tm, 2 * HEAD_DIM), lambda t: (prv(t), 0)),
            pl.BlockSpec((2, 2 * HEAD_DIM), lambda t: (0, 0)),
        ],
        out_specs=[
            pl.BlockSpec((tm, pw), lambda t: (cur(t), 0)),
            pl.BlockSpec((tm, 3 * hw), lambda t: (cur(t), 0)),
            pl.BlockSpec((HEADS, 2 * HEAD_DIM, tm), lambda t: (0, 0, prv(t))),
            pl.BlockSpec((HEADS, tm, 2 * HEAD_DIM), lambda t: (0, prv(t), 0)),
            pl.BlockSpec((HEADS, V_DIM, tm), lambda t: (0, 0, prv(t))),
        ],
        scratch_shapes=[pltpu.VMEM((tm, wid - pw - 3 * hw), F32)],
        out_shape=[
            jax.ShapeDtypeStruct((n, pw), BF16),
            jax.ShapeDtypeStruct((n, 3 * hw), BF16),
            jax.ShapeDtypeStruct((HEADS, 2 * HEAD_DIM, n), BF16),
            jax.ShapeDtypeStruct((HEADS, n, 2 * HEAD_DIM), BF16),
            jax.ShapeDtypeStruct((HEADS, V_DIM, n), BF16),
        ],
        compiler_params=_cparams(("arbitrary",)),
    )(x, g, mod, w, cos_t, sin_t, gqk)


def _attn_kernel(lam_ref, qt_ref, kc_ref, vtc_ref, *rest, tk, n_lat_chunks, lam_init):
    if n_lat_chunks:
        kl_ref, vtl_ref, g_ref, o_ref, m_ref, l_ref, acc_ref = rest
    else:
        g_ref, o_ref, m_ref, l_ref, acc_ref = rest
    qt = qt_ref[0]
    row = lax.broadcasted_iota(jnp.int32, qt.shape, 0)
    zero = jnp.zeros_like(qt)
    qmaps = (jnp.where(row < HEAD_DIM, qt, zero), jnp.where(row >= HEAD_DIM, qt, zero))

    def process(k_tile, vt_tile, first):
        for mi in range(2):
            s = _dot(k_tile, qmaps[mi])
            smax = jnp.max(s, axis=0, keepdims=True)
            if first:
                m_new = smax
            else:
                m_old = m_ref[mi]
                m_new = jnp.maximum(m_old, smax)
            p = jnp.exp2(s - m_new)
            psum = jnp.sum(p, axis=0, keepdims=True)
            pv = _dot(vt_tile, p.astype(BF16))
            if first:
                l_ref[mi] = psum
                acc_ref[mi] = pv
            else:
                alpha = jnp.exp2(m_old - m_new)
                l_ref[mi] = alpha * l_ref[mi] + psum
                acc_ref[mi] = alpha * acc_ref[mi] + pv
            m_ref[mi] = m_new

    process(kc_ref[0], vtc_ref[0], True)
    if n_lat_chunks:
        def body(j, carry):
            off = pl.multiple_of(j * tk, tk)
            process(kl_ref[0, pl.ds(off, tk), :], vtl_ref[0, :, pl.ds(off, tk)], False)
            return carry
        lax.fori_loop(0, n_lat_chunks, body, 0)

    lv = lam_ref[...]
    lam = (jnp.exp(jnp.sum(lv[0:1] * lv[1:2], axis=-1, keepdims=True))
           - jnp.exp(jnp.sum(lv[2:3] * lv[3:4], axis=-1, keepdims=True)) + lam_init)
    o = acc_ref[0] * (1.0 / l_ref[0]) - lam * (acc_ref[1] * (1.0 / l_ref[1]))
    ms = jnp.mean(o * o, axis=0, keepdims=True)
    y = o * lax.rsqrt(ms + EPS) * g_ref[...] * (1.0 - lam_init)
    o_ref[...] = y.T.astype(BF16)


def _attn_direct_kernel(lam_ref, qt_ref, kc_ref, vtc_ref, *rest, tk, n_lat_chunks, lam_init):
    if n_lat_chunks:
        kl_ref, vtl_ref, g_ref, o_ref, s_ref, p_ref, l_ref, acc_ref = rest
    else:
        g_ref, o_ref, s_ref, p_ref, l_ref, acc_ref = rest
    qt = qt_ref[0]
    tq = qt.shape[1]
    row = lax.broadcasted_iota(jnp.int32, qt.shape, 0)
    zero = jnp.zeros_like(qt)
    qmaps = (jnp.where(row < HEAD_DIM, qt, zero), jnp.where(row >= HEAD_DIM, qt, zero))

    def piece_rows(j):
        if isinstance(j, int):
            return pl.ds((j - 1) * tk, tk)
        return pl.ds(pl.multiple_of((j - 1) * tk, tk), tk)

    def k_piece(j):
        return kl_ref[0, piece_rows(j), :]

    def vt_piece(j):
        return vtc_ref[0] if isinstance(j, int) and j == 0 else vtl_ref[0, :, piece_rows(j)]

    def step(j, par):
        stage_a(k_piece(j + 1), 1 - par)
        stage_b(par)
        stage_c(vt_piece(j - 1), 1 - par)

    def stage_a(k_tile, slot):
        for mi in range(2):
            s_ref[slot, mi] = _dot(k_tile, qmaps[mi])

    def stage_b(slot):
        for mi in range(2):
            p = jnp.exp2(s_ref[slot, mi])
            l_ref[mi] = l_ref[mi] + jnp.sum(p.reshape(tk // 8, 8, tq), axis=0)
            p_ref[slot, mi] = p.astype(BF16)

    def stage_c(vt_tile, slot):
        for mi in range(2):
            acc_ref[mi] = acc_ref[mi] + _dot(vt_tile, p_ref[slot, mi])

    l_ref[...] = jnp.zeros_like(l_ref)
    acc_ref[...] = jnp.zeros_like(acc_ref)
    stage_a(kc_ref[0], 0)
    if n_lat_chunks == 0:
        stage_b(0)
        stage_c(vtc_ref[0], 0)
    else:
        stage_a(k_piece(1), 1)
        stage_b(0)
        for j in range(1, ATT_UNROLL):
            step(j, j % 2)

        def body(jj, carry):
            j = ATT_UNROLL * (jj + 1)
            for r in range(ATT_UNROLL):
                step(j + r, r % 2)
            return carry

        lax.fori_loop(0, n_lat_chunks // ATT_UNROLL - 1, body, 0)
        stage_b(0)
        stage_c(vt_piece(n_lat_chunks - 1), 1)
        stage_c(vt_piece(n_lat_chunks), 0)

    lv = lam_ref[...]
    lam = (jnp.exp(jnp.sum(lv[0:1] * lv[1:2], axis=-1, keepdims=True))
           - jnp.exp(jnp.sum(lv[2:3] * lv[3:4], axis=-1, keepdims=True)) + lam_init)
    l1 = jnp.sum(l_ref[0], axis=0, keepdims=True)
    l2 = jnp.sum(l_ref[1], axis=0, keepdims=True)
    o = acc_ref[0] * (1.0 / l1) - lam * (acc_ref[1] * (1.0 / l2))
    ms = jnp.mean(o * o, axis=0, keepdims=True)
    y = o * lax.rsqrt(ms + EPS) * g_ref[...] * (1.0 - lam_init)
    o_ref[...] = y.T.astype(BF16)


def _attention(lam_p, qt, k12, vt, g_b, *, batch, seq, ctx_len, latent, lam_init, direct):
    n = k12.shape[1]
    nl = batch * seq
    tq = ATT_TQ if latent else ctx_len
    nq = seq // tq if latent else 1
    tk = ctx_len if direct else ATT_TK
    assert seq % ((ATT_UNROLL if direct else 1) * tk) == 0
    qbase = 0 if latent else nl // tq

    def qrow(b, iq):
        return (b * nq + iq) if latent else (qbase + b)

    in_specs = [
        pl.BlockSpec(lam_p.shape, lambda b, h, iq: (0, 0)),
        pl.BlockSpec((1, 2 * HEAD_DIM, tq), lambda b, h, iq: (h, 0, qrow(b, iq))),
        pl.BlockSpec((1, ctx_len, 2 * HEAD_DIM), lambda b, h, iq: (h, nl // ctx_len + b, 0)),
        pl.BlockSpec((1, V_DIM, ctx_len), lambda b, h, iq: (h, 0, nl // ctx_len + b)),
    ]
    args = [lam_p, qt, k12, vt]
    if latent:
        in_specs += [
            pl.BlockSpec((1, seq, 2 * HEAD_DIM), lambda b, h, iq: (h, b, 0)),
            pl.BlockSpec((1, V_DIM, seq), lambda b, h, iq: (h, 0, b)),
        ]
        args += [k12, vt]
    in_specs.append(pl.BlockSpec((V_DIM, tq), lambda b, h, iq: (0, 0)))
    args.append(g_b[:, :tq])
    nrows = nl if latent else batch * ctx_len
    body = _attn_direct_kernel if direct else _attn_kernel
    kern = functools.partial(body, tk=tk, n_lat_chunks=(seq // tk if latent else 0), lam_init=lam_init)
    if direct:
        scratch = [pltpu.VMEM((2, 2, tk, tq), F32), pltpu.VMEM((2, 2, tk, tq), BF16),
                   pltpu.VMEM((2, 8, tq), F32), pltpu.VMEM((2, V_DIM, tq), F32)]
    else:
        scratch = [pltpu.VMEM((2, 1, tq), F32), pltpu.VMEM((2, 1, tq), F32), pltpu.VMEM((2, V_DIM, tq), F32)]
    return pl.pallas_call(
        kern,
        grid=(batch, HEADS, nq),
        in_specs=in_specs,
        out_specs=pl.BlockSpec((tq, V_DIM), lambda b, h, iq: (b * nq + iq, h)),
        out_shape=jax.ShapeDtypeStruct((nrows, HEADS * V_DIM), BF16),
        scratch_shapes=scratch,
        compiler_params=_cparams(("arbitrary", "arbitrary", "arbitrary")),
    )(*args)


def _halo_specs(width, *, row0, seq, tile):
    hb = tile // HALO

    def cur(b, i):
        return ((row0 + b * seq) // tile + i, 0)

    def prev(b, i):
        first = (row0 + b * seq) // HALO
        return (jnp.maximum(first + i * hb - 1, first), 0)

    def nxt(b, i):
        first = (row0 + b * seq) // HALO
        return (jnp.minimum(first + (i + 1) * hb, first + seq // HALO - 1), 0)

    return [pl.BlockSpec((HALO, width), prev), pl.BlockSpec((tile, width), cur), pl.BlockSpec((HALO, width), nxt)]


def _fill_ext(ext_ref, prev_ref, cur_ref, next_ref, *, seq, tile):
    pos0 = pl.program_id(1) * tile
    width = cur_ref.shape[1]
    hpos = lax.broadcasted_iota(jnp.int32, (HALO, width), 0)
    ext_ref[0:HALO, :] = jnp.where(pos0 - HALO + hpos >= 0, prev_ref[...].astype(F32), 0.0)
    ext_ref[HALO:HALO + tile, :] = cur_ref[...].astype(F32)
    ext_ref[HALO + tile:, :] = jnp.where(pos0 + tile + hpos < seq, next_ref[...].astype(F32), 0.0)


def _pool_kernel(prev_ref, cur_ref, next_ref, lin_ref, scale_ref, o_ref, ext_ref, s_ref, *, seq, tile):
    _fill_ext(ext_ref, prev_ref, cur_ref, next_ref, seq=seq, tile=tile)
    width = cur_ref.shape[1]
    gd = width // len(POOL_WINDOWS)
    n0 = tile + 2 * HALO
    s_ref[0, 0:n0 - 8, :] = ext_ref[0:n0 - 8, :] + ext_ref[1:n0 - 7, :]
    for k in range(1, len(POOL_WINDOWS)):
        step = 1 << k
        ln = n0 - 8 * (k + 1)
        s_ref[k, 0:ln, :] = s_ref[k - 1, 0:ln, :] + s_ref[k - 1, step:step + ln, :]
    lane = lax.broadcasted_iota(jnp.int32, (tile, width), 1)
    pos = pl.program_id(1) * tile + lax.broadcasted_iota(jnp.int32, (tile, width), 0)
    grp = lane // gd
    wsum = jnp.zeros((tile, width), F32)
    half = jnp.zeros((tile, width), jnp.int32)
    for k, win in enumerate(POOL_WINDOWS):
        start = HALO - win // 2
        wsum = jnp.where(grp == k, s_ref[k, start:start + tile, :], wsum)
        half = jnp.where(grp == k, win // 2, half)
    cnt = jnp.minimum(pos + half, seq) - jnp.maximum(pos - half, 0)
    z = ext_ref[HALO:HALO + tile, :]
    dlt = (wsum / cnt.astype(F32) - z).astype(BF16)
    o_ref[...] = (_dot(dlt, lin_ref[...]) * scale_ref[...]).astype(BF16)


def _pool(zp, lin_bd, scale, *, row0, batch, seq, out_rows, out_row0):
    width = zp.shape[1]
    tile = min(ROW_TILE, seq)
    kern = functools.partial(_pool_kernel, seq=seq, tile=tile)
    return pl.pallas_call(
        kern,
        grid=(batch, seq // tile),
        in_specs=_halo_specs(width, row0=row0, seq=seq, tile=tile) + [
            pl.BlockSpec((width, width), lambda b, i: (0, 0)),
            pl.BlockSpec((1, width), lambda b, i: (0, 0)),
        ],
        out_specs=pl.BlockSpec((tile, width), lambda b, i: ((out_row0 + b * seq) // tile + i, 0)),
        out_shape=jax.ShapeDtypeStruct((out_rows, width), BF16),
        scratch_shapes=[
            pltpu.VMEM((tile + 2 * HALO, width), F32),
            pltpu.VMEM((4, tile + 2 * HALO, width), F32),
        ],
        compiler_params=_cparams(("arbitrary", "arbitrary")),
    )(zp, zp, zp, lin_bd, scale)


def _hy_pre_kernel(prev_ref, cur_ref, next_ref, w_ref, b_ref, u_ref, x0_ref, ext_ref, *, seq, tile, hw):
    _fill_ext(ext_ref, prev_ref, cur_ref, next_ref, seq=seq, tile=tile)
    y = b_ref[...] + ext_ref[HALO - 1:HALO - 1 + tile, :] * w_ref[0:1, :]
    y = y + ext_ref[HALO:HALO + tile, :] * w_ref[1:2, :]
    y = y + ext_ref[HALO + 1:HALO + 1 + tile, :] * w_ref[2:3, :]
    x0_ref[...] = y[:, :hw]
    u_ref[...] = y[:, 2 * hw:] * y[:, hw:2 * hw]


def _hy_pre(zh, sw, sb, *, row0, batch, seq, out_rows, out_row0):
    width = zh.shape[1]
    hw = width // 3
    tile = min(ROW_TILE, seq)
    kern = functools.partial(_hy_pre_kernel, seq=seq, tile=tile, hw=hw)
    ospec = pl.BlockSpec((tile, hw), lambda b, i: ((out_row0 + b * seq) // tile + i, 0))
    return pl.pallas_call(
        kern,
        grid=(batch, seq // tile),
        in_specs=_halo_specs(width, row0=row0, seq=seq, tile=tile) + [
            pl.BlockSpec((3, width), lambda b, i: (0, 0)),
            pl.BlockSpec((1, width), lambda b, i: (0, 0)),
        ],
        out_specs=[ospec, ospec],
        out_shape=[jax.ShapeDtypeStruct((out_rows, hw), F32), jax.ShapeDtypeStruct((out_rows, hw), F32)],
        scratch_shapes=[pltpu.VMEM((tile + 2 * HALO, width), F32)],
        compiler_params=_cparams(("arbitrary", "arbitrary")),
    )(zh, zh, zh, sw, sb)


def _filter_kernel(feat_ref, w1_ref, b1_ref, f1_ref, w2_ref, b2_ref, f2_ref, w3_ref, dl_ref,
                   taps_ref, asum_ref, *, tile, hw, seq, embp):
    feat = feat_ref[...]
    h = jnp.sin(f1_ref[...] * (_dot3(feat, w1_ref[...]) + b1_ref[...]))
    h = jnp.sin(f2_ref[...] * (_dot3(h, w2_ref[...]) + b2_ref[...]))
    h = _dot3(h, w3_ref[...])
    odd = lax.broadcasted_iota(jnp.int32, (tile, 2 * hw), 1) >= hw
    tcol = jnp.where(odd, feat[:, embp:embp + 1], feat[:, 0:1])
    dec = jnp.exp(-tcol * dl_ref[...])
    row = pl.program_id(0) * tile + lax.broadcasted_iota(jnp.int32, (tile, 2 * hw), 0)
    pos = 2 * row + odd.astype(jnp.int32)
    taps = jnp.where(pos == seq, 0.0, h * dec)
    taps_ref[...] = taps
    part = jnp.sum(jnp.abs(taps), axis=0, keepdims=True)

    @pl.when(pl.program_id(0) == 0)
    def _():
        asum_ref[...] = part

    @pl.when(pl.program_id(0) != 0)
    def _():
        asum_ref[...] = asum_ref[...] + part


def _hy_filter(seq, w1, b1, f1, w2, b2, f2, w3, hw):
    t = jnp.linspace(0.0, 1.0, seq, dtype=F32)[:, None]
    w = (2.0 * math.pi / seq) * jnp.arange(seq, dtype=F32)[:, None]
    bands = jnp.linspace(1e-4, HY_BANDS - 1, HY_BANDS, dtype=F32)[None, :]
    emb = 1 + 2 * HY_BANDS
    embp = ((emb + 7) // 8) * 8
    feat = jnp.concatenate([t, jnp.cos(bands * w), -jnp.sin(bands * w), jnp.zeros((seq, embp - emb), F32)], axis=-1)
    w1p = jnp.concatenate([w1, jnp.zeros((embp - emb, w1.shape[1]), F32)], axis=0)
    max_decay = math.log(1.0 / HY_TARGET) / HY_FAST
    min_decay = math.log(1.0 / HY_TARGET) / HY_SLOW
    deltas = jnp.linspace(min_decay, max_decay, hw, dtype=F32)[None, :]
    feat = jnp.concatenate([feat, feat[0:1], jnp.flip(feat[1:], axis=0)], axis=0)
    feat = feat.reshape(seq, 2 * embp)
    hid = w1.shape[1]

    def pair(m):
        z = jnp.zeros_like(m)
        return jnp.concatenate([jnp.concatenate([m, z], axis=1), jnp.concatenate([z, m], axis=1)], axis=0)

    twice = lambda v: jnp.concatenate([v, v])[None, :]
    w3p = jnp.concatenate([pair(w3[:, :hw]), pair(w3[:, hw:])], axis=1)
    tile = min(ROW_TILE, seq) // 2
    nfwd = seq // (2 * tile)
    full = lambda shape: pl.BlockSpec(shape, lambda i: (0,) * len(shape))
    taps, asum = pl.pallas_call(
        functools.partial(_filter_kernel, tile=tile, hw=hw, seq=seq, embp=embp),
        grid=(2 * nfwd,),
        in_specs=[
            pl.BlockSpec((tile, 2 * embp), lambda i: (i, 0)),
            full((2 * embp, 2 * hid)), full((1, 2 * hid)), full((1, 2 * hid)),
            full((2 * hid, 2 * hid)), full((1, 2 * hid)), full((1, 2 * hid)),
            pl.BlockSpec((2 * hid, 2 * hw), lambda i: (0, i // nfwd)), full((1, 2 * hw)),
        ],
        out_specs=[
            pl.BlockSpec((tile, 2 * hw), lambda i: (i, 0)),
            pl.BlockSpec((1, 2 * hw), lambda i: (0, 0)),
        ],
        out_shape=[
            jax.ShapeDtypeStruct((seq, 2 * hw), F32),
            jax.ShapeDtypeStruct((1, 2 * hw), F32),
        ],
        compiler_params=_cparams(("arbitrary",)),
    )(feat, pair(w1p), twice(b1), twice(f1), pair(w2), twice(b2), twice(f2), w3p,
      jnp.concatenate([deltas, deltas], axis=1))
    return taps.reshape(2 * seq, hw), asum[:, :hw] + asum[:, hw:]


def _np_split(a):
    a32 = jnp.asarray(a, F32)
    hi = a32.astype(BF16)
    lo = (a32 - hi.astype(F32)).astype(BF16)
    return hi, lo


def _dft_consts(n2):
    n1 = FFT_N1
    n = n1 * n2
    half = n1 // 2
    a1 = -2.0 * np.pi * np.outer(np.arange(n1), np.arange(n1)) / n1
    f1r, f1i = np.cos(a1), np.sin(a1)
    f1_data = np.block([[f1r[:, :half], -f1i[:, :half]], [f1i[:, :half], f1r[:, :half]]])
    f1_real = np.concatenate([f1r, f1i], axis=0)
    g1r, g1i = f1r / n, -f1i / n
    g1 = np.block([[g1r[:half], -g1i[:half]], [g1i[:half], g1r[:half]]])
    a2 = -2.0 * np.pi * np.outer(np.arange(n2), np.arange(n2)) / n2
    f2r, f2i = np.cos(a2), np.sin(a2)
    f2 = np.block([[f2r, -f2i], [f2i, f2r]])
    g2 = np.block([[f2r, f2i], [-f2i, f2r]])
    at = -2.0 * np.pi * np.outer(np.arange(n1), np.arange(n2)) / n
    twr = jnp.broadcast_to(jnp.asarray(np.cos(at), F32)[:, :, None], (n1, n2, 128))
    twi = jnp.broadcast_to(jnp.asarray(np.sin(at), F32)[:, :, None], (n1, n2, 128))
    return dict(f1_data=_np_split(f1_data), f1_real=_np_split(f1_real), g1=_np_split(g1),
                f2=_np_split(f2), g2=_np_split(g2), twr=twr, twi=twi)


def _fft1_kernel(z_ref, fh_ref, fl_ref, ar_ref, ai_ref):
    a = _dot3c(fh_ref[...], fl_ref[...], z_ref[...])
    ar_ref[...] = a[:FFT_N1]
    ai_ref[...] = a[FFT_N1:]


def _fft_first(zview, fmat, ncols):
    cb = min(FFT_COLS, ncols)
    fh, fl = fmat
    cspec = pl.BlockSpec(fh.shape, lambda j: (0, 0))
    ospec = pl.BlockSpec((FFT_N1, cb), lambda j: (0, j))
    return pl.pallas_call(
        _fft1_kernel,
        grid=(ncols // cb,),
        in_specs=[pl.BlockSpec((FFT_N1, cb), lambda j: (0, j)), cspec, cspec],
        out_specs=[ospec, ospec],
        out_shape=[jax.ShapeDtypeStruct((FFT_N1, ncols), F32)] * 2,
        compiler_params=_cparams(("arbitrary",)),
    )(zview, fh, fl)


def _fftmid_kernel(ar_ref, ai_ref, twr_ref, twi_ref, f2h_ref, f2l_ref, *rest, filter_only, n2):
    kb, _, width = ar_ref.shape
    reps = width // 128
    lanes = lambda parts: jnp.concatenate(parts, axis=1)
    twr = lanes([t for j in range(kb) for t in [twr_ref[j]] * reps])
    twi = lanes([t for j in range(kb) for t in [twi_ref[j]] * reps])
    ar = lanes([ar_ref[j] for j in range(kb)])
    ai = lanes([ai_ref[j] for j in range(kb)])
    z = jnp.concatenate([ar * twr - ai * twi, ar * twi + ai * twr], axis=0)
    x = _dot3c(f2h_ref[...], f2l_ref[...], z)
    xr, xi = x[:n2], x[n2:]
    if filter_only:
        kr_out, ki_out = rest
        for j in range(kb):
            kr_out[j] = xr[:, j * width:(j + 1) * width]
            ki_out[j] = xi[:, j * width:(j + 1) * width]
        return
    kr_ref, ki_ref, g2h_ref, g2l_ref, br_out, bi_out = rest
    kr = lanes([kr_ref[j] for j in range(kb)])
    ki = lanes([ki_ref[j] for j in range(kb)])
    y = jnp.concatenate([xr * kr - xi * ki, xr * ki + xi * kr], axis=0)
    w = _dot3c(g2h_ref[...], g2l_ref[...], y)
    wr, wi = w[:n2], w[n2:]
    br = wr * twr + wi * twi
    bi = wi * twr - wr * twi
    for j in range(kb):
        br_out[j] = br[:, j * width:(j + 1) * width]
        bi_out[j] = bi[:, j * width:(j + 1) * width]


def _fft_mid(ar, ai, consts, n2, width, spectrum=None):
    a3r = ar.reshape(FFT_N1, n2, width)
    a3i = ai.reshape(FFT_N1, n2, width)
    blk = pl.BlockSpec((FFT_KB, n2, width), lambda k: (k, 0, 0))
    twspec = pl.BlockSpec((FFT_KB, n2, 128), lambda k: (k, 0, 0))
    cspec = pl.BlockSpec((2 * n2, 2 * n2), lambda k: (0, 0))
    in_specs = [blk, blk, twspec, twspec, cspec, cspec]
    args = [a3r, a3i, consts["twr"], consts["twi"], *consts["f2"]]
    if spectrum is not None:
        in_specs += [blk, blk, cspec, cspec]
        args += [spectrum[0], spectrum[1], *consts["g2"]]
    return pl.pallas_call(
        functools.partial(_fftmid_kernel, filter_only=spectrum is None, n2=n2),
        grid=(FFT_N1 // FFT_KB,),
        in_specs=in_specs,
        out_specs=[blk, blk],
        out_shape=[jax.ShapeDtypeStruct((FFT_N1, n2, width), F32)] * 2,
        compiler_params=_cparams(("arbitrary",)),
    )(*args)


def _fftlast_kernel(br_ref, bi_ref, gh_ref, gl_ref, u_ref, x0_ref, bias_ref, invn_ref, o_ref):
    b = jnp.concatenate([br_ref[...], bi_ref[...]], axis=0)
    y = _dot3c(gh_ref[...], gl_ref[...], b)
    o_ref[...] = ((y * invn_ref[...] + u_ref[...] * bias_ref[...]) * x0_ref[...]).astype(BF16)


def _fft_last(br, bi, gmat, uview, x0view, bias_t, invn_t, ncols):
    cb = min(FFT_COLS, ncols)
    gh, gl = gmat
    cspec = pl.BlockSpec(gh.shape, lambda j: (0, 0))
    dspec = pl.BlockSpec((FFT_N1, cb), lambda j: (0, j))
    vspec = pl.BlockSpec((1, cb), lambda j: (0, 0))
    return pl.pallas_call(
        _fftlast_kernel,
        grid=(ncols // cb,),
        in_specs=[dspec, dspec, cspec, cspec, dspec, dspec, vspec, vspec],
        out_specs=dspec,
        out_shape=jax.ShapeDtypeStruct((FFT_N1, ncols), BF16),
        compiler_params=_cparams(("arbitrary",)),
    )(br, bi, gh, gl, uview, x0view, bias_t, invn_t)


def _dft_small_consts(seq):
    n = 2 * seq
    a = -2.0 * np.pi * np.outer(np.arange(n), np.arange(n)) / n
    fr, fi = np.cos(a), np.sin(a)
    f_data = np.block([[fr[:, :seq], -fi[:, :seq]], [fi[:, :seq], fr[:, :seq]]])
    f_real = np.concatenate([fr, fi], axis=0)
    gr, gi = fr / n, -fi / n
    g = np.block([[gr[:seq], -gi[:seq]], [gi[:seq], gr[:seq]]])
    return _np_split(f_data), _np_split(f_real), _np_split(g)


def _hy_ctx_kernel(u_ref, x0_ref, taps_ref, fdh_ref, fdl_ref, frh_ref, frl_ref, gh_ref, gl_ref,
                   bias_ref, invn_ref, o_ref):
    n = taps_ref.shape[0]
    z = u_ref[...]
    a = _dot3c(fdh_ref[...], fdl_ref[...], z)
    k = _dot3c(frh_ref[...], frl_ref[...], taps_ref[...])
    ar, ai, kr, ki = a[:n], a[n:], k[:n], k[n:]
    y = jnp.concatenate([ar * kr - ai * ki, ar * ki + ai * kr], axis=0)
    conv = _dot3c(gh_ref[...], gl_ref[...], y)
    o_ref[...] = ((conv * invn_ref[...] + z * bias_ref[...]) * x0_ref[...]).astype(BF16)


def _hy_ctx(u, x0, taps, bias, invn, small):
    rows, hw = u.shape
    full = lambda a: pl.BlockSpec(a.shape, lambda i: (0,) * a.ndim)
    args = [u, x0, taps, *small[0], *small[1], *small[2], bias, invn]
    return pl.pallas_call(
        _hy_ctx_kernel,
        grid=(1,),
        in_specs=[full(a) for a in args],
        out_specs=pl.BlockSpec((rows, hw), lambda i: (0, 0)),
        out_shape=jax.ShapeDtypeStruct((rows, hw), BF16),
        compiler_params=_cparams(("arbitrary",)),
    )(*args)


def _out_proj_kernel(x_ref, ypl_ref, yhl_ref, yal_ref, ypc_ref, yhc_ref, yac_ref, w_ref, g_ref, mod_ref, *rest,
                     nlt, tpb, ncond, pw, hw, moe, ntiles):
    if moe:
        rw_ref, xo_ref, u_ref, route_ref, un_ref = rest
    else:
        xo_ref, u_ref = rest
    t = jnp.minimum(pl.program_id(0), ntiles - 1)
    if moe:
        @pl.when(pl.program_id(0) == 0)
        def _():
            un_ref[...] = jnp.zeros_like(un_ref)

        logits = _dot3(un_ref[...], rw_ref[...])
    ci = _cond_row(t, nlt, tpb, ncond)
    is_ctx = t >= nlt
    yp = jnp.where(is_ctx, ypc_ref[...], ypl_ref[...])
    yh = jnp.where(is_ctx, yhc_ref[...], yhl_ref[...])
    ya = jnp.where(is_ctx, yac_ref[...], yal_ref[...])
    mix = _dot(yp, w_ref[0:pw, :]) + _dot(yh, w_ref[pw:pw + hw, :]) + _dot(ya, w_ref[pw + hw:, :])
    x = x_ref[...] + mod_ref[2, pl.ds(ci, 1), :] * mix
    xo_ref[...] = x
    un = _normmod(x, g_ref[...], mod_ref[3, pl.ds(ci, 1), :], mod_ref[4, pl.ds(ci, 1), :])
    u_ref[...] = un.astype(BF16)
    if moe:
        un_ref[...] = un
        lane = lax.broadcasted_iota(jnp.int32, logits.shape, 1)
        neg = jnp.float32(-jnp.inf)
        lg = jnp.where(lane < N_EXPERTS, logits, neg)
        t1 = jnp.max(lg, axis=-1, keepdims=True)
        i1 = jnp.min(jnp.where(lg == t1, lane, 128), axis=-1, keepdims=True)
        lg2 = jnp.where(lane == i1, neg, lg)
        t2 = jnp.max(lg2, axis=-1, keepdims=True)
        i2 = jnp.min(jnp.where(lg2 == t2, lane, 128), axis=-1, keepdims=True)
        e2 = jnp.exp(t2 - t1)
        g1 = 1.0 / (1.0 + e2)
        g2 = e2 / (1.0 + e2)
        route_ref[...] = jnp.where(lane == 0, i1.astype(F32), jnp.where(lane == 1, i2.astype(F32),
                                   jnp.where(lane == 2, g1, jnp.where(lane == 3, g2, 0.0))))


def _out_proj(x, lat, ctx, w, g, mod, rw, *, nlt, tpb, ncond):
    n, d = x.shape
    pw, hw, aw = (a.shape[1] for a in lat)
    tm = ROW_TILE
    moe = rw is not None
    nt = n // tm
    cur = lambda t: jnp.minimum(t, nt - 1)
    row = lambda width: pl.BlockSpec((tm, width), lambda t: (cur(t), 0))
    lrow = lambda width: pl.BlockSpec((tm, width), lambda t: (jnp.minimum(t, nlt - 1), 0))
    crow = lambda width: pl.BlockSpec((tm, width), lambda t: (jnp.maximum(cur(t) - nlt, 0), 0))
    in_specs = [row(d), lrow(pw), lrow(hw), lrow(aw), crow(pw), crow(hw), crow(aw),
                pl.BlockSpec(w.shape, lambda t: (0, 0)),
                pl.BlockSpec((1, d), lambda t: (0, 0)),
                pl.BlockSpec(mod.shape, lambda t: (0, 0, 0))]
    args = [x, *lat, *ctx, w, g, mod]
    out_specs = [row(d), row(d)]
    out_shape = [jax.ShapeDtypeStruct((n, d), F32), jax.ShapeDtypeStruct((n, d), BF16)]
    scratch = []
    if moe:
        in_specs.append(pl.BlockSpec(rw.shape, lambda t: (0, 0)))
        args.append(rw)
        out_specs.append(pl.BlockSpec((tm, 128), lambda t: (jnp.maximum(t - 1, 0), 0)))
        out_shape.append(jax.ShapeDtypeStruct((n, 128), F32))
        scratch.append(pltpu.VMEM((tm, d), F32))
    kern = functools.partial(_out_proj_kernel, nlt=nlt, tpb=tpb, ncond=ncond, pw=pw, hw=hw, moe=moe, ntiles=nt)
    return pl.pallas_call(
        kern, grid=(nt + 1 if moe else nt,), in_specs=in_specs, out_specs=out_specs, out_shape=out_shape,
        scratch_shapes=scratch,
        compiler_params=_cparams(("arbitrary",)),
    )(*args)


def _ffn_kernel(x_ref, u_ref, mod_ref, wg_ref, wu_ref, wd_ref, o_ref, *, nlt, tpb, ncond, nchunk):
    ci = _cond_row(pl.program_id(0), nlt, tpb, ncond)
    u = u_ref[...]
    ff = wg_ref.shape[1]
    fc = ff // nchunk
    y = jnp.zeros(x_ref.shape, F32)
    for c in range(nchunk):
        gate = _dot(u, wg_ref[:, c * fc:(c + 1) * fc])
        up = _dot(u, wu_ref[:, c * fc:(c + 1) * fc])
        y = y + _dot((_silu(gate) * up).astype(BF16), wd_ref[c * fc:(c + 1) * fc, :])
    o_ref[...] = x_ref[...] + mod_ref[5, pl.ds(ci, 1), :] * y


def _ffn(x, u, mod, wg, wu, wd, *, nlt, tpb, ncond, out_rows):
    n, d = x.shape
    ff = wg.shape[1]
    tm = ROW_TILE
    nchunk = 2 if (ff // 2) % 128 == 0 else 1
    row = lambda dt: pl.BlockSpec((tm, d), lambda t: (t, 0))
    const = lambda shape: pl.BlockSpec(shape, lambda t: (0,) * len(shape), pipeline_mode=pl.Buffered(1))
    kern = functools.partial(_ffn_kernel, nlt=nlt, tpb=tpb, ncond=ncond, nchunk=nchunk)
    return pl.pallas_call(
        kern,
        grid=(out_rows // tm,),
        in_specs=[row(F32), row(BF16), pl.BlockSpec(mod.shape, lambda t: (0, 0, 0)),
                  const((d, ff)), const((d, ff)), const((ff, d))],
        out_specs=row(F32),
        out_shape=jax.ShapeDtypeStruct((out_rows, d), F32),
        compiler_params=_cparams(("arbitrary",)),
    )(x, u, mod, wg, wu, wd)


def _moe_kernel(te_ref, tv_ref, u_ref, wg_ref, wu_ref, wd_ref, o_ref, acc_ref):
    t = pl.program_id(0)
    f = pl.program_id(1)

    @pl.when(tv_ref[t] > 0)
    def _():
        u = u_ref[...]
        h = (_silu(_dot(u, wg_ref[0, 0])) * _dot(u, wu_ref[0, 0])).astype(BF16)
        y = _dot(h, wd_ref[0, 0])

        @pl.when(f == 0)
        def _():
            acc_ref[...] = y

        @pl.when(f != 0)
        def _():
            acc_ref[...] = acc_ref[...] + y

        @pl.when(f == pl.num_programs(1) - 1)
        def _():
            o_ref[...] = acc_ref[...].astype(BF16)


def _moe_experts(ug, tile_expert, tile_valid, wg, wu, wd, j):
    p, d = ug.shape
    ff = wg.shape[3]
    tm, fc = MOE_TM, MOE_FC
    grid_spec = pltpu.PrefetchScalarGridSpec(
        num_scalar_prefetch=2,
        grid=(p // tm, ff // fc),
        in_specs=[
            pl.BlockSpec((tm, d), lambda t, f, te, tv: (t, 0)),
            pl.BlockSpec((1, 1, d, fc), lambda t, f, te, tv: (j, te[t], 0, f)),
            pl.BlockSpec((1, 1, d, fc), lambda t, f, te, tv: (j, te[t], 0, f)),
            pl.BlockSpec((1, 1, fc, d), lambda t, f, te, tv: (j, te[t], f, 0)),
        ],
        out_specs=pl.BlockSpec((tm, d), lambda t, f, te, tv: (t, 0)),
        scratch_shapes=[pltpu.VMEM((tm, d), F32)],
    )
    return pl.pallas_call(
        _moe_kernel, grid_spec=grid_spec,
        out_shape=jax.ShapeDtypeStruct((p, d), BF16),
        compiler_params=_cparams(("arbitrary", "arbitrary")),
    )(tile_expert, tile_valid, ug, wg, wu, wd)


def _moe_combine_kernel(x_ref, ya_ref, yb_ref, route_ref, mod_ref, o_ref, *, nlt, tpb, ncond):
    ci = _cond_row(pl.program_id(0), nlt, tpb, ncond)
    r = route_ref[...]
    lane = lax.broadcasted_iota(jnp.int32, r.shape, 1)
    g1 = jnp.sum(jnp.where(lane == 2, r, 0.0), axis=-1, keepdims=True)
    g2 = jnp.sum(jnp.where(lane == 3, r, 0.0), axis=-1, keepdims=True)
    y = g1 * ya_ref[...].astype(F32) + g2 * yb_ref[...].astype(F32)
    o_ref[...] = x_ref[...] + mod_ref[5, pl.ds(ci, 1), :] * y


def _moe_combine(x, ya, yb, route, mod, *, nlt, tpb, ncond, out_rows):
    n, d = x.shape
    tm = ROW_TILE
    row = lambda width: pl.BlockSpec((tm, width), lambda t: (t, 0))
    kern = functools.partial(_moe_combine_kernel, nlt=nlt, tpb=tpb, ncond=ncond)
    return pl.pallas_call(
        kern,
        grid=(out_rows // tm,),
        in_specs=[row(d), row(d), row(d), row(128), pl.BlockSpec(mod.shape, lambda t: (0, 0, 0))],
        out_specs=row(d),
        out_shape=jax.ShapeDtypeStruct((out_rows, d), F32),
        compiler_params=_cparams(("arbitrary",)),
    )(x, ya, yb, route, mod)


def _moe_layer(x, u, route, mod, wg, wu, wd, j, *, nlt, tpb, ncond, out_rows):
    n, d = x.shape
    tm = MOE_TM
    experts = jnp.concatenate([route[:, 0], route[:, 1]]).astype(jnp.int32)
    onehot = (experts[:, None] == jnp.arange(N_EXPERTS, dtype=jnp.int32)[None, :]).astype(jnp.int32)
    rank = jnp.sum(onehot * (jnp.cumsum(onehot, axis=0) - 1), axis=1)
    counts = jnp.sum(onehot, axis=0)
    padded = ((counts + tm - 1) // tm) * tm
    ends = jnp.cumsum(padded)
    starts = ends - padded
    dest = starts[experts] + rank
    p = 2 * n + N_EXPERTS * tm
    tokens = jnp.concatenate([jnp.arange(n, dtype=jnp.int32)] * 2)
    inb = dict(mode="promise_in_bounds")
    src = jnp.zeros((p,), jnp.int32).at[dest].set(tokens, unique_indices=True, **inb)
    tile_start = jnp.arange(p // tm, dtype=jnp.int32) * tm
    tile_expert = jnp.minimum(jnp.searchsorted(ends, tile_start, side="right"), N_EXPERTS - 1).astype(jnp.int32)
    tile_valid = (tile_start < ends[-1]).astype(jnp.int32)
    ys = _moe_experts(u.at[src].get(**inb), tile_expert, tile_valid, wg, wu, wd, j)
    ya = ys.at[dest[:n]].get(**inb)
    yb = ys.at[dest[n:]].get(**inb)
    return _moe_combine(x, ya, yb, route, mod, nlt=nlt, tpb=tpb, ncond=ncond, out_rows=out_rows)


def _rope_tables(seq, n_ctx_rows, batch):
    rows = seq // GRID_W
    row = jnp.repeat(jnp.arange(rows, dtype=F32), GRID_W)
    col = jnp.broadcast_to(jnp.arange(GRID_W, dtype=F32), (rows, GRID_W)).reshape(-1)
    inv_freq = jnp.power(ROPE_THETA, -jnp.arange(ROPE_FREQS, dtype=F32) / ROPE_FREQS)
    ar = row[:, None] * inv_freq
    ac = col[:, None] * inv_freq
    cos = jnp.concatenate([jnp.cos(ar), jnp.cos(ar), jnp.cos(ac), jnp.cos(ac)], axis=-1)
    sin = jnp.concatenate([-jnp.sin(ar), jnp.sin(ar), -jnp.sin(ac), jnp.sin(ac)], axis=-1)
    cos = jnp.tile(jnp.concatenate([cos, cos], axis=-1), (batch, 1))
    sin = jnp.tile(jnp.concatenate([sin, sin], axis=-1), (batch, 1))
    cos = jnp.concatenate([cos, jnp.ones((n_ctx_rows, 2 * HEAD_DIM), F32)], axis=0)
    sin = jnp.concatenate([sin, jnp.zeros((n_ctx_rows, 2 * HEAD_DIM), F32)], axis=0)
    return cos, sin


def kernel(x, c, ctx, c_ctx, mod_w, mod_b, norm1_g, norm2_g, w_in, w_out, pool_lin, pool_scale, hy_short_w, hy_short_b, hy_f_w1, hy_f_b1, hy_f_freq1, hy_f_w2, hy_f_b2, hy_f_freq2, hy_f_w3, hy_bias, qk_norm_g, diff_lambda, subln_g, ffn_w_gate, ffn_w_up, ffn_w_down, router_w, moe_w_gate, moe_w_up, moe_w_down):
    batch, seq, d = x.shape
    ctx_len = ctx.shape[1]
    depth = mod_w.shape[0]
    pw = pool_scale.shape[1]
    hw = hy_bias.shape[1]
    nl, nc = batch * seq, batch * ctx_len
    n = nl + nc
    tm = ROW_TILE
    assert seq % tm == 0 and nc % tm == 0 and seq % (FFT_N1 // 2) == 0 and seq % GRID_W == 0
    assert d == HEADS * 2 * V_DIM and pw == hw and batch == 2
    n2 = 2 * seq // FFT_N1
    ncols = n2 * hw
    assert (n * hw) % ncols == 0
    tiles = dict(nlt=nl // tm, tpb=seq // tm, ncond=batch + 1)

    xs = jnp.concatenate([x.reshape(nl, d), ctx.reshape(nc, d)], axis=0)
    cond8 = jnp.concatenate([c, c_ctx[None, :], jnp.zeros((8 - batch - 1, d), F32)], axis=0)
    mods = _modvec(cond8, mod_w, mod_b)
    cos_t, sin_t = _rope_tables(seq, nc, batch)
    consts = _dft_consts(n2)
    small = _dft_small_consts(ctx_len)
    eye = jnp.eye(len(POOL_WINDOWS), dtype=F32)
    moe_wg, moe_wu, moe_wd = _cast_bf16(moe_w_gate, moe_w_up, moe_w_down)

    for i in range(depth):
        last = i == depth - 1
        lam_init = 0.8 - 0.6 * math.exp(-0.3 * i)
        mod = mods[i]
        gqk = jnp.concatenate([qk_norm_g[i], qk_norm_g[i]], axis=-1)
        zp, zh, qt, k12, vt = _in_proj(xs, norm1_g[i][None, :], mod, w_in[i].astype(BF16), cos_t, sin_t, gqk,
                                       pw=pw, hw=hw, **tiles)

        g_b = jnp.broadcast_to(subln_g[i][:, None], (V_DIM, ATT_TQ))
        att_kw = dict(batch=batch, seq=seq, ctx_len=ctx_len, lam_init=lam_init)

        def attend(direct, lam_p=diff_lambda[i], qt=qt, k12=k12, vt=vt, g_b=g_b, att_kw=att_kw):
            return (_attention(lam_p, qt, k12, vt, g_b, latent=True, direct=direct, **att_kw),
                    _attention(lam_p, qt, k12, vt, g_b, latent=False, direct=direct, **att_kw))

        bound = (HEAD_DIM * QSCALE) * jnp.max(jnp.abs(qk_norm_g[i][0])) * jnp.max(jnp.abs(qk_norm_g[i][1]))
        ya_l, ya_c = lax.cond(bound * 1.02 < ATT_DIRECT_MAX, lambda: attend(True), lambda: attend(False))

        lin_bd = (eye[:, None, :, None] * pool_lin[i][:, :, None, :]).reshape(pw, pw).astype(BF16)
        pscale = pool_scale[i][None, :]
        yp_l = _pool(zp, lin_bd, pscale, row0=0, batch=batch, seq=seq, out_rows=nl, out_row0=0)
        yp_c = _pool(zp, lin_bd, pscale, row0=nl, batch=batch, seq=ctx_len, out_rows=nc, out_row0=0)

        sw, sb = hy_short_w[i], hy_short_b[i][None, :]
        u_l, x0_l = _hy_pre(zh, sw, sb, row0=0, batch=batch, seq=seq, out_rows=n, out_row0=0)
        u_c, x0_c = _hy_pre(zh, sw, sb, row0=nl, batch=batch, seq=ctx_len, out_rows=nc, out_row0=0)
        filt = (hy_f_w1[i], hy_f_b1[i], hy_f_freq1[i], hy_f_w2[i], hy_f_b2[i], hy_f_freq2[i], hy_f_w3[i])
        bias = hy_bias[i][None, :]
        taps, asum = _hy_filter(seq, *filt, hw)
        kr1, ki1 = _fft_first(taps.reshape(FFT_N1, ncols), consts["f1_real"], ncols)
        spectrum = _fft_mid(kr1, ki1, consts, n2, hw)
        ar, ai = _fft_first(u_l.reshape(-1, ncols), consts["f1_data"], ncols)
        br, bi = _fft_mid(ar, ai, consts, n2, hw, spectrum=spectrum)
        reps = min(FFT_COLS, ncols) // hw
        yh_lat = _fft_last(br.reshape(FFT_N1, ncols), bi.reshape(FFT_N1, ncols), consts["g1"],
                           u_l.reshape(-1, ncols), x0_l.reshape(-1, ncols),
                           jnp.tile(bias, (1, reps)), jnp.tile(1.0 / asum, (1, reps)), ncols)
        taps_c, asum_c = _hy_filter(ctx_len, *filt, hw)
        yh_ctx = _hy_ctx(u_c, x0_c, taps_c, bias, 1.0 / asum_c, small)

        j = i // 2
        moe = i % 2 == 1
        rw = None
        if moe:
            rw = jnp.concatenate([router_w[j], jnp.zeros((d, 128 - N_EXPERTS), F32)], axis=1)
        outs = _out_proj(xs, (yp_l, yh_lat.reshape(nl, hw), ya_l), (yp_c, yh_ctx, ya_c), w_out[i].astype(BF16),
                         norm2_g[i][None, :], mod, rw, **tiles)
        out_rows = nl if last else n
        if moe:
            xs, u, route = outs
            xs = _moe_layer(xs, u, route, mod, moe_wg, moe_wu, moe_wd, j, out_rows=out_rows, **tiles)
        else:
            xs, u = outs
            xs = _ffn(xs, u, mod, ffn_w_gate[j].astype(BF16), ffn_w_up[j].astype(BF16),
                      ffn_w_down[j].astype(BF16), out_rows=out_rows, **tiles)
    return xs[:nl].reshape(batch, seq, d)
```

```python
import functools
import math

import numpy as np
import jax
import jax.numpy as jnp
from jax import lax
from jax.experimental import pallas as pl
from jax.experimental.pallas import tpu as pltpu

F32 = jnp.float32
BF16 = jnp.bfloat16
EPS = 1e-6

GRID_W = 64
POOL_WINDOWS = (2, 4, 8, 16)
HEADS = 4
HEAD_DIM = 64
V_DIM = 128
ROPE_FREQS = 16
ROPE_THETA = 10000.0
HY_BANDS = 16
HY_TARGET = 1e-2
HY_FAST = 0.3
HY_SLOW = 1.5
N_EXPERTS = 8

ROW_TILE = 512
HALO = 32
FFT_N1 = 128
FFT_COLS = 2048
FFT_KB = 4
ATT_TQ = 1024
ATT_TK = 512
ATT_UNROLL = 16
QSCALE = (HEAD_DIM ** -0.5) * math.log2(math.e)
ATT_DIRECT_MAX = 100.0
CAST_STEPS = 64
MOE_TM = 512
MOE_FC = 1792
VMEM_LIMIT = 56 * 1024 * 1024


def _cparams(sem):
    return pltpu.CompilerParams(dimension_semantics=sem, vmem_limit_bytes=VMEM_LIMIT)


def _dot(a, b):
    return jnp.dot(a, b, preferred_element_type=F32)


def _split(a):
    hi = a.astype(BF16)
    lo = (a - hi.astype(F32)).astype(BF16)
    return hi, lo


def _dot3(a, b):
    ah, al = _split(a)
    bh, bl = _split(b)
    return _dot(ah, bh) + _dot(ah, bl) + _dot(al, bh)


def _dot3c(ch, cl, d):
    dh, dl = _split(d)
    return _dot(ch, dh) + _dot(ch, dl) + _dot(cl, dh)


def _silu(x):
    return x / (1.0 + jnp.exp(-x))


def _normmod(x, g, shift, scale):
    ms = jnp.mean(x * x, axis=-1, keepdims=True)
    return (x * lax.rsqrt(ms + EPS) * g) * (1.0 + scale) + shift


def _cond_row(t, n_lat_tiles, tiles_per_batch, n_cond):
    return jnp.where(t >= n_lat_tiles, n_cond - 1, t // tiles_per_batch)


def _cast_kernel(*refs):
    half = len(refs) // 2
    for w_ref, o_ref in zip(refs[:half], refs[half:]):
        o_ref[...] = w_ref[...].astype(BF16)


def _cast_bf16(*ws):
    flat = [w.reshape(-1, w.shape[-1]) for w in ws]
    steps = CAST_STEPS
    specs = [pl.BlockSpec((w.shape[0] // steps, w.shape[1]), lambda t: (t, 0)) for w in flat]
    outs = pl.pallas_call(
        _cast_kernel,
        grid=(steps,),
        in_specs=specs,
        out_specs=specs,
        out_shape=[jax.ShapeDtypeStruct(w.shape, BF16) for w in flat],
        compiler_params=_cparams(("arbitrary",)),
    )(*flat)
    return [o.reshape(w.shape) for o, w in zip(outs, ws)]


def _modvec_kernel(c_ref, w_ref, b_ref, o_ref):
    s = _silu(c_ref[...])
    o_ref[0, 0] = _dot3(s, w_ref[0]) + b_ref[0]


def _modvec(cond8, mod_w, mod_b):
    depth, d, six_d = mod_w.shape
    nchunk = six_d // d
    return pl.pallas_call(
        _modvec_kernel,
        grid=(depth, nchunk),
        in_specs=[
            pl.BlockSpec((8, d), lambda i, j: (0, 0)),
            pl.BlockSpec((1, d, d), lambda i, j: (i, 0, j)),
            pl.BlockSpec((1, 1, d), lambda i, j: (i, 0, j)),
        ],
        out_specs=pl.BlockSpec((1, 1, 8, d), lambda i, j: (i, j, 0, 0)),
        out_shape=jax.ShapeDtypeStruct((depth, nchunk, 8, d), F32),
        compiler_params=_cparams(("arbitrary", "arbitrary")),
    )(cond8, mod_w, mod_b.reshape(depth, 1, six_d))


def _in_proj_kernel(x_ref, g_ref, mod_ref, w_ref, cos_ref, sin_ref, gqk_ref,
                    zp_ref, zh_ref, qt_ref, k_ref, vt_ref, zatt_ref, *, nlt, tpb, ncond, pw, hw, ntiles):
    t = pl.program_id(0)

    @pl.when(t == 0)
    def _():
        zatt_ref[...] = jnp.zeros_like(zatt_ref)

    tm = x_ref.shape[0]
    att = pw + 3 * hw
    lane = lax.broadcasted_iota(jnp.int32, (tm, 2 * HEAD_DIM), 1)
    first = lane < HEAD_DIM
    apart = (lane % (2 * ROPE_FREQS)) < ROPE_FREQS
    cos = cos_ref[...]
    sin = sin_ref[...]
    qk_w = HEADS * 2 * HEAD_DIM

    def norm_rope(v, g):
        v2 = v * v
        s_all = jnp.sum(v2, axis=-1, keepdims=True)
        s_first = jnp.sum(jnp.where(first, v2, 0.0), axis=-1, keepdims=True)
        ms = jnp.where(first, s_first, s_all - s_first) * (1.0 / HEAD_DIM)
        vn = v * lax.rsqrt(ms + EPS) * g
        swapped = jnp.where(apart, pltpu.roll(vn, 2 * HEAD_DIM - ROPE_FREQS, 1), pltpu.roll(vn, ROPE_FREQS, 1))
        return vn * cos + swapped * sin

    for h in range(HEADS):
        lo = h * 2 * HEAD_DIM
        q = norm_rope(zatt_ref[:, lo:lo + 2 * HEAD_DIM], gqk_ref[0:1, :])
        qt_ref[h] = (q * QSCALE).T.astype(BF16)
        k = norm_rope(zatt_ref[:, qk_w + lo:qk_w + lo + 2 * HEAD_DIM], gqk_ref[1:2, :])
        k_ref[h] = k.astype(BF16)
        vlo = 2 * qk_w + h * V_DIM
        vt_ref[h] = zatt_ref[:, vlo:vlo + V_DIM].T.astype(BF16)

    ci = _cond_row(jnp.minimum(t, ntiles - 1), nlt, tpb, ncond)
    shift = mod_ref[0, pl.ds(ci, 1), :]
    scale = mod_ref[1, pl.ds(ci, 1), :]
    y = _normmod(x_ref[...], g_ref[...], shift, scale).astype(BF16)
    z = _dot(y, w_ref[...])
    zp_ref[...] = z[:, :pw].astype(BF16)
    zh_ref[...] = z[:, pw:att].astype(BF16)
    zatt_ref[...] = z[:, att:]


def _in_proj(x, g, mod, w, cos_t, sin_t, gqk, *, nlt, tpb, ncond, pw, hw):
    n, d = x.shape
    wid = w.shape[1]
    tm = ROW_TILE
    nt = n // tm
    kern = functools.partial(_in_proj_kernel, nlt=nlt, tpb=tpb, ncond=ncond, pw=pw, hw=hw, ntiles=nt)
    cur = lambda t: jnp.minimum(t, nt - 1)
    prv = lambda t: jnp.maximum(t - 1, 0)
    return pl.pallas_call(
        kern,
        grid=(nt + 1,),
        in_specs=[
            pl.BlockSpec((tm, d), lambda t: (cur(t), 0)),
            pl.BlockSpec((1, d), lambda t: (0, 0)),
            pl.BlockSpec(mod.shape, lambda t: (0, 0, 0)),
            pl.BlockSpec((d, wid), lambda t: (0, 0)),
            pl.BlockSpec((tm, 2 * HEAD_DIM), lambda t: (prv(t), 0)),
            pl.BlockSpec((tm, 2 * HEAD_DIM), lambda t: (prv(t), 0)),
            pl.BlockSpec((2, 2 * HEAD_DIM), lambda t: (0, 0)),
        ],
        out_specs=[
            pl.BlockSpec((tm, pw), lambda t: (cur(t), 0)),
            pl.BlockSpec((tm, 3 * hw), lambda t: (cur(t), 0)),
            pl.BlockSpec((HEADS, 2 * HEAD_DIM, tm), lambda t: (0, 0, prv(t))),
            pl.BlockSpec((HEADS, tm, 2 * HEAD_DIM), lambda t: (0, prv(t), 0)),
            pl.BlockSpec((HEADS, V_DIM, tm), lambda t: (0, 0, prv(t))),
        ],
        scratch_shapes=[pltpu.VMEM((tm, wid - pw - 3 * hw), F32)],
        out_shape=[
            jax.ShapeDtypeStruct((n, pw), BF16),
            jax.ShapeDtypeStruct((n, 3 * hw), BF16),
            jax.ShapeDtypeStruct((HEADS, 2 * HEAD_DIM, n), BF16),
            jax.ShapeDtypeStruct((HEADS, n, 2 * HEAD_DIM), BF16),
            jax.ShapeDtypeStruct((HEADS, V_DIM, n), BF16),
        ],
        compiler_params=_cparams(("arbitrary",)),
    )(x, g, mod, w, cos_t, sin_t, gqk)


def _attn_kernel(lam_ref, qt_ref, kc_ref, vtc_ref, *rest, tk, n_lat_chunks, lam_init):
    if n_lat_chunks:
        kl_ref, vtl_ref, g_ref, o_ref, m_ref, l_ref, acc_ref = rest
    else:
        g_ref, o_ref, m_ref, l_ref, acc_ref = rest
    qt = qt_ref[0]
    row = lax.broadcasted_iota(jnp.int32, qt.shape, 0)
    zero = jnp.zeros_like(qt)
    qmaps = (jnp.where(row < HEAD_DIM, qt, zero), jnp.where(row >= HEAD_DIM, qt, zero))

    def process(k_tile, vt_tile, first):
        for mi in range(2):
            s = _dot(k_tile, qmaps[mi])
            smax = jnp.max(s, axis=0, keepdims=True)
            if first:
                m_new = smax
            else:
                m_old = m_ref[mi]
                m_new = jnp.maximum(m_old, smax)
            p = jnp.exp2(s - m_new)
            psum = jnp.sum(p, axis=0, keepdims=True)
            pv = _dot(vt_tile, p.astype(BF16))
            if first:
                l_ref[mi] = psum
                acc_ref[mi] = pv
            else:
                alpha = jnp.exp2(m_old - m_new)
                l_ref[mi] = alpha * l_ref[mi] + psum
                acc_ref[mi] = alpha * acc_ref[mi] + pv
            m_ref[mi] = m_new

    process(kc_ref[0], vtc_ref[0], True)
    if n_lat_chunks:
        def body(j, carry):
            off = pl.multiple_of(j * tk, tk)
            process(kl_ref[0, pl.ds(off, tk), :], vtl_ref[0, :, pl.ds(off, tk)], False)
            return carry
        lax.fori_loop(0, n_lat_chunks, body, 0)

    lv = lam_ref[...]
    lam = (jnp.exp(jnp.sum(lv[0:1] * lv[1:2], axis=-1, keepdims=True))
           - jnp.exp(jnp.sum(lv[2:3] * lv[3:4], axis=-1, keepdims=True)) + lam_init)
    o = acc_ref[0] * (1.0 / l_ref[0]) - lam * (acc_ref[1] * (1.0 / l_ref[1]))
    ms = jnp.mean(o * o, axis=0, keepdims=True)
    y = o * lax.rsqrt(ms + EPS) * g_ref[...] * (1.0 - lam_init)
    o_ref[...] = y.T.astype(BF16)


def _attn_direct_kernel(lam_ref, qt_ref, kc_ref, vtc_ref, *rest, tk, n_lat_chunks, lam_init):
    if n_lat_chunks:
        kl_ref, vtl_ref, g_ref, o_ref, s_ref, p_ref, l_ref, acc_ref = rest
    else:
        g_ref, o_ref, s_ref, p_ref, l_ref, acc_ref = rest
    qt = qt_ref[0]
    tq = qt.shape[1]
    row = lax.broadcasted_iota(jnp.int32, qt.shape, 0)
    zero = jnp.zeros_like(qt)
    qmaps = (jnp.where(row < HEAD_DIM, qt, zero), jnp.where(row >= HEAD_DIM, qt, zero))

    def piece_rows(j):
        if isinstance(j, int):
            return pl.ds((j - 1) * tk, tk)
        return pl.ds(pl.multiple_of((j - 1) * tk, tk), tk)

    def k_piece(j):
        return kl_ref[0, piece_rows(j), :]

    def vt_piece(j):
        return vtc_ref[0] if isinstance(j, int) and j == 0 else vtl_ref[0, :, piece_rows(j)]

    def step(j, par):
        stage_a(k_piece(j + 1), 1 - par)
        stage_b(par)
        stage_c(vt_piece(j - 1), 1 - par)

    def stage_a(k_tile, slot):
        for mi in range(2):
            s_ref[slot, mi] = _dot(k_tile, qmaps[mi])

    def stage_b(slot):
        for mi in range(2):
            p = jnp.exp2(s_ref[slot, mi])
            l_ref[mi] = l_ref[mi] + jnp.sum(p.reshape(tk // 8, 8, tq), axis=0)
            p_ref[slot, mi] = p.astype(BF16)

    def stage_c(vt_tile, slot):
        for mi in range(2):
            acc_ref[mi] = acc_ref[mi] + _dot(vt_tile, p_ref[slot, mi])

    l_ref[...] = jnp.zeros_like(l_ref)
    acc_ref[...] = jnp.zeros_like(acc_ref)
    stage_a(kc_ref[0], 0)
    def stage_ab(k_tile, slot):
        for mi in range(2):
            p = jnp.exp2(_dot(k_tile, qmaps[mi]))
            l_ref[mi] = l_ref[mi] + jnp.sum(p.reshape(tk // 8, 8, tq), axis=0)
            p_ref[slot, mi] = p.astype(BF16)

    def step2(j, par):
        stage_ab(k_piece(j), par)
        stage_c(vt_piece(j - 1), 1 - par)

    if n_lat_chunks == 0:
        stage_b(0)
        stage_c(vtc_ref[0], 0)
    else:
        stage_b(0)
        for j in range(1, ATT_UNROLL + 1):
            step2(j, j % 2)

        def body(jj, carry):
            j = ATT_UNROLL * (jj + 1) + 1
            for r in range(ATT_UNROLL):
                step2(j + r, (1 + r) % 2)
            return carry

        lax.fori_loop(0, n_lat_chunks // ATT_UNROLL - 1, body, 0)
        stage_c(vt_piece(n_lat_chunks), 0)

    lv = lam_ref[...]
    lam = (jnp.exp(jnp.sum(lv[0:1] * lv[1:2], axis=-1, keepdims=True))
           - jnp.exp(jnp.sum(lv[2:3] * lv[3:4], axis=-1, keepdims=True)) + lam_init)
    l1 = jnp.sum(l_ref[0], axis=0, keepdims=True)
    l2 = jnp.sum(l_ref[1], axis=0, keepdims=True)
    o = acc_ref[0] * (1.0 / l1) - lam * (acc_ref[1] * (1.0 / l2))
    ms = jnp.mean(o * o, axis=0, keepdims=True)
    y = o * lax.rsqrt(ms + EPS) * g_ref[...] * (1.0 - lam_init)
    o_ref[...] = y.T.astype(BF16)


def _attention(lam_p, qt, k12, vt, g_b, *, batch, seq, ctx_len, latent, lam_init, direct):
    n = k12.shape[1]
    nl = batch * seq
    tq = ATT_TQ if latent else ctx_len
    nq = seq // tq if latent else 1
    tk = ctx_len if direct else ATT_TK
    assert seq % ((ATT_UNROLL if direct else 1) * tk) == 0
    qbase = 0 if latent else nl // tq

    def qrow(b, iq):
        return (b * nq + iq) if latent else (qbase + b)

    in_specs = [
        pl.BlockSpec(lam_p.shape, lambda b, h, iq: (0, 0)),
        pl.BlockSpec((1, 2 * HEAD_DIM, tq), lambda b, h, iq: (h, 0, qrow(b, iq))),
        pl.BlockSpec((1, ctx_len, 2 * HEAD_DIM), lambda b, h, iq: (h, nl // ctx_len + b, 0)),
        pl.BlockSpec((1, V_DIM, ctx_len), lambda b, h, iq: (h, 0, nl // ctx_len + b)),
    ]
    args = [lam_p, qt, k12, vt]
    if latent:
        in_specs += [
            pl.BlockSpec((1, seq, 2 * HEAD_DIM), lambda b, h, iq: (h, b, 0)),
            pl.BlockSpec((1, V_DIM, seq), lambda b, h, iq: (h, 0, b)),
        ]
        args += [k12, vt]
    in_specs.append(pl.BlockSpec((V_DIM, tq), lambda b, h, iq: (0, 0)))
    args.append(g_b[:, :tq])
    nrows = nl if latent else batch * ctx_len
    body = _attn_direct_kernel if direct else _attn_kernel
    kern = functools.partial(body, tk=tk, n_lat_chunks=(seq // tk if latent else 0), lam_init=lam_init)
    if direct:
        scratch = [pltpu.VMEM((2, 2, tk, tq), F32), pltpu.VMEM((2, 2, tk, tq), BF16),
                   pltpu.VMEM((2, 8, tq), F32), pltpu.VMEM((2, V_DIM, tq), F32)]
    else:
        scratch = [pltpu.VMEM((2, 1, tq), F32), pltpu.VMEM((2, 1, tq), F32), pltpu.VMEM((2, V_DIM, tq), F32)]
    return pl.pallas_call(
        kern,
        grid=(batch, HEADS, nq),
        in_specs=in_specs,
        out_specs=pl.BlockSpec((tq, V_DIM), lambda b, h, iq: (b * nq + iq, h)),
        out_shape=jax.ShapeDtypeStruct((nrows, HEADS * V_DIM), BF16),
        scratch_shapes=scratch,
        compiler_params=_cparams(("arbitrary", "arbitrary", "arbitrary")),
    )(*args)


def _halo_specs(width, *, row0, seq, tile):
    hb = tile // HALO

    def cur(b, i):
        return ((row0 + b * seq) // tile + i, 0)

    def prev(b, i):
        first = (row0 + b * seq) // HALO
        return (jnp.maximum(first + i * hb - 1, first), 0)

    def nxt(b, i):
        first = (row0 + b * seq) // HALO
        return (jnp.minimum(first + (i + 1) * hb, first + seq // HALO - 1), 0)

    return [pl.BlockSpec((HALO, width), prev), pl.BlockSpec((tile, width), cur), pl.BlockSpec((HALO, width), nxt)]


def _fill_ext(ext_ref, prev_ref, cur_ref, next_ref, *, seq, tile):
    pos0 = pl.program_id(1) * tile
    width = cur_ref.shape[1]
    hpos = lax.broadcasted_iota(jnp.int32, (HALO, width), 0)
    ext_ref[0:HALO, :] = jnp.where(pos0 - HALO + hpos >= 0, prev_ref[...].astype(F32), 0.0)
    ext_ref[HALO:HALO + tile, :] = cur_ref[...].astype(F32)
    ext_ref[HALO + tile:, :] = jnp.where(pos0 + tile + hpos < seq, next_ref[...].astype(F32), 0.0)


def _pool_kernel(prev_ref, cur_ref, next_ref, lin_ref, scale_ref, o_ref, ext_ref, s_ref, *, seq, tile):
    _fill_ext(ext_ref, prev_ref, cur_ref, next_ref, seq=seq, tile=tile)
    width = cur_ref.shape[1]
    gd = width // len(POOL_WINDOWS)
    n0 = tile + 2 * HALO
    s_ref[0, 0:n0 - 8, :] = ext_ref[0:n0 - 8, :] + ext_ref[1:n0 - 7, :]
    for k in range(1, len(POOL_WINDOWS)):
        step = 1 << k
        ln = n0 - 8 * (k + 1)
        s_ref[k, 0:ln, :] = s_ref[k - 1, 0:ln, :] + s_ref[k - 1, step:step + ln, :]
    lane = lax.broadcasted_iota(jnp.int32, (tile, width), 1)
    pos = pl.program_id(1) * tile + lax.broadcasted_iota(jnp.int32, (tile, width), 0)
    grp = lane // gd
    wsum = jnp.zeros((tile, width), F32)
    half = jnp.zeros((tile, width), jnp.int32)
    for k, win in enumerate(POOL_WINDOWS):
        start = HALO - win // 2
        wsum = jnp.where(grp == k, s_ref[k, start:start + tile, :], wsum)
        half = jnp.where(grp == k, win // 2, half)
    cnt = jnp.minimum(pos + half, seq) - jnp.maximum(pos - half, 0)
    z = ext_ref[HALO:HALO + tile, :]
    dlt = (wsum / cnt.astype(F32) - z).astype(BF16)
    o_ref[...] = (_dot(dlt, lin_ref[...]) * scale_ref[...]).astype(BF16)


def _pool(zp, lin_bd, scale, *, row0, batch, seq, out_rows, out_row0):
    width = zp.shape[1]
    tile = min(ROW_TILE, seq)
    kern = functools.partial(_pool_kernel, seq=seq, tile=tile)
    return pl.pallas_call(
        kern,
        grid=(batch, seq // tile),
        in_specs=_halo_specs(width, row0=row0, seq=seq, tile=tile) + [
            pl.BlockSpec((width, width), lambda b, i: (0, 0)),
            pl.BlockSpec((1, width), lambda b, i: (0, 0)),
        ],
        out_specs=pl.BlockSpec((tile, width), lambda b, i: ((out_row0 + b * seq) // tile + i, 0)),
        out_shape=jax.ShapeDtypeStruct((out_rows, width), BF16),
        scratch_shapes=[
            pltpu.VMEM((tile + 2 * HALO, width), F32),
            pltpu.VMEM((4, tile + 2 * HALO, width), F32),
        ],
        compiler_params=_cparams(("arbitrary", "arbitrary")),
    )(zp, zp, zp, lin_bd, scale)


def _hy_pre_kernel(prev_ref, cur_ref, next_ref, w_ref, b_ref, u_ref, x0_ref, ext_ref, *, seq, tile, hw):
    _fill_ext(ext_ref, prev_ref, cur_ref, next_ref, seq=seq, tile=tile)
    y = b_ref[...] + ext_ref[HALO - 1:HALO - 1 + tile, :] * w_ref[0:1, :]
    y = y + ext_ref[HALO:HALO + tile, :] * w_ref[1:2, :]
    y = y + ext_ref[HALO + 1:HALO + 1 + tile, :] * w_ref[2:3, :]
    x0_ref[...] = y[:, :hw]
    u_ref[...] = y[:, 2 * hw:] * y[:, hw:2 * hw]


def _hy_pre(zh, sw, sb, *, row0, batch, seq, out_rows, out_row0):
    width = zh.shape[1]
    hw = width // 3
    tile = min(ROW_TILE, seq)
    kern = functools.partial(_hy_pre_kernel, seq=seq, tile=tile, hw=hw)
    ospec = pl.BlockSpec((tile, hw), lambda b, i: ((out_row0 + b * seq) // tile + i, 0))
    return pl.pallas_call(
        kern,
        grid=(batch, seq // tile),
        in_specs=_halo_specs(width, row0=row0, seq=seq, tile=tile) + [
            pl.BlockSpec((3, width), lambda b, i: (0, 0)),
            pl.BlockSpec((1, width), lambda b, i: (0, 0)),
        ],
        out_specs=[ospec, ospec],
        out_shape=[jax.ShapeDtypeStruct((out_rows, hw), F32), jax.ShapeDtypeStruct((out_rows, hw), F32)],
        scratch_shapes=[pltpu.VMEM((tile + 2 * HALO, width), F32)],
        compiler_params=_cparams(("arbitrary", "arbitrary")),
    )(zh, zh, zh, sw, sb)


def _filter_kernel(feat_ref, w1_ref, b1_ref, f1_ref, w2_ref, b2_ref, f2_ref, w3_ref, dl_ref,
                   taps_ref, asum_ref, *, tile, hw, seq, embp):
    feat = feat_ref[...]
    h = jnp.sin(f1_ref[...] * (_dot3(feat, w1_ref[...]) + b1_ref[...]))
    h = jnp.sin(f2_ref[...] * (_dot3(h, w2_ref[...]) + b2_ref[...]))
    h = _dot3(h, w3_ref[...])
    odd = lax.broadcasted_iota(jnp.int32, (tile, 2 * hw), 1) >= hw
    tcol = jnp.where(odd, feat[:, embp:embp + 1], feat[:, 0:1])
    dec = jnp.exp(-tcol * dl_ref[...])
    row = pl.program_id(0) * tile + lax.broadcasted_iota(jnp.int32, (tile, 2 * hw), 0)
    pos = 2 * row + odd.astype(jnp.int32)
    taps = jnp.where(pos == seq, 0.0, h * dec)
    taps_ref[...] = taps
    part = jnp.sum(jnp.abs(taps), axis=0, keepdims=True)

    @pl.when(pl.program_id(0) == 0)
    def _():
        asum_ref[...] = part

    @pl.when(pl.program_id(0) != 0)
    def _():
        asum_ref[...] = asum_ref[...] + part


def _hy_filter(seq, w1, b1, f1, w2, b2, f2, w3, hw):
    t = jnp.linspace(0.0, 1.0, seq, dtype=F32)[:, None]
    w = (2.0 * math.pi / seq) * jnp.arange(seq, dtype=F32)[:, None]
    bands = jnp.linspace(1e-4, HY_BANDS - 1, HY_BANDS, dtype=F32)[None, :]
    emb = 1 + 2 * HY_BANDS
    embp = ((emb + 7) // 8) * 8
    feat = jnp.concatenate([t, jnp.cos(bands * w), -jnp.sin(bands * w), jnp.zeros((seq, embp - emb), F32)], axis=-1)
    w1p = jnp.concatenate([w1, jnp.zeros((embp - emb, w1.shape[1]), F32)], axis=0)
    max_decay = math.log(1.0 / HY_TARGET) / HY_FAST
    min_decay = math.log(1.0 / HY_TARGET) / HY_SLOW
    deltas = jnp.linspace(min_decay, max_decay, hw, dtype=F32)[None, :]
    feat = jnp.concatenate([feat, feat[0:1], jnp.flip(feat[1:], axis=0)], axis=0)
    feat = feat.reshape(seq, 2 * embp)
    hid = w1.shape[1]

    def pair(m):
        z = jnp.zeros_like(m)
        return jnp.concatenate([jnp.concatenate([m, z], axis=1), jnp.concatenate([z, m], axis=1)], axis=0)

    twice = lambda v: jnp.concatenate([v, v])[None, :]
    w3p = jnp.concatenate([pair(w3[:, :hw]), pair(w3[:, hw:])], axis=1)
    tile = min(ROW_TILE, seq) // 2
    nfwd = seq // (2 * tile)
    full = lambda shape: pl.BlockSpec(shape, lambda i: (0,) * len(shape))
    taps, asum = pl.pallas_call(
        functools.partial(_filter_kernel, tile=tile, hw=hw, seq=seq, embp=embp),
        grid=(2 * nfwd,),
        in_specs=[
            pl.BlockSpec((tile, 2 * embp), lambda i: (i, 0)),
            full((2 * embp, 2 * hid)), full((1, 2 * hid)), full((1, 2 * hid)),
            full((2 * hid, 2 * hid)), full((1, 2 * hid)), full((1, 2 * hid)),
            pl.BlockSpec((2 * hid, 2 * hw), lambda i: (0, i // nfwd)), full((1, 2 * hw)),
        ],
        out_specs=[
            pl.BlockSpec((tile, 2 * hw), lambda i: (i, 0)),
            pl.BlockSpec((1, 2 * hw), lambda i: (0, 0)),
        ],
        out_shape=[
            jax.ShapeDtypeStruct((seq, 2 * hw), F32),
            jax.ShapeDtypeStruct((1, 2 * hw), F32),
        ],
        compiler_params=_cparams(("arbitrary",)),
    )(feat, pair(w1p), twice(b1), twice(f1), pair(w2), twice(b2), twice(f2), w3p,
      jnp.concatenate([deltas, deltas], axis=1))
    return taps.reshape(2 * seq, hw), asum[:, :hw] + asum[:, hw:]


def _np_split(a):
    a32 = jnp.asarray(a, F32)
    hi = a32.astype(BF16)
    lo = (a32 - hi.astype(F32)).astype(BF16)
    return hi, lo


def _dft_consts(n2):
    n1 = FFT_N1
    n = n1 * n2
    half = n1 // 2
    a1 = -2.0 * np.pi * np.outer(np.arange(n1), np.arange(n1)) / n1
    f1r, f1i = np.cos(a1), np.sin(a1)
    f1_data = np.block([[f1r[:, :half], -f1i[:, :half]], [f1i[:, :half], f1r[:, :half]]])
    f1_real = np.concatenate([f1r, f1i], axis=0)
    g1r, g1i = f1r / n, -f1i / n
    g1 = np.block([[g1r[:half], -g1i[:half]], [g1i[:half], g1r[:half]]])
    a2 = -2.0 * np.pi * np.outer(np.arange(n2), np.arange(n2)) / n2
    f2r, f2i = np.cos(a2), np.sin(a2)
    f2 = np.block([[f2r, -f2i], [f2i, f2r]])
    g2 = np.block([[f2r, f2i], [-f2i, f2r]])
    at = -2.0 * np.pi * np.outer(np.arange(n1), np.arange(n2)) / n
    twr = jnp.broadcast_to(jnp.asarray(np.cos(at), F32)[:, :, None], (n1, n2, 128))
    twi = jnp.broadcast_to(jnp.asarray(np.sin(at), F32)[:, :, None], (n1, n2, 128))
    return dict(f1_data=_np_split(f1_data), f1_real=_np_split(f1_real), g1=_np_split(g1),
                f2=_np_split(f2), g2=_np_split(g2), twr=twr, twi=twi)


def _fft1_kernel(z_ref, fh_ref, fl_ref, ar_ref, ai_ref):
    a = _dot3c(fh_ref[...], fl_ref[...], z_ref[...])
    ar_ref[...] = a[:FFT_N1]
    ai_ref[...] = a[FFT_N1:]


def _fft_first(zview, fmat, ncols):
    cb = min(FFT_COLS, ncols)
    fh, fl = fmat
    cspec = pl.BlockSpec(fh.shape, lambda j: (0, 0))
    ospec = pl.BlockSpec((FFT_N1, cb), lambda j: (0, j))
    return pl.pallas_call(
        _fft1_kernel,
        grid=(ncols // cb,),
        in_specs=[pl.BlockSpec((FFT_N1, cb), lambda j: (0, j)), cspec, cspec],
        out_specs=[ospec, ospec],
        out_shape=[jax.ShapeDtypeStruct((FFT_N1, ncols), F32)] * 2,
        compiler_params=_cparams(("arbitrary",)),
    )(zview, fh, fl)


def _fftmid_kernel(ar_ref, ai_ref, twr_ref, twi_ref, f2h_ref, f2l_ref, *rest, filter_only, n2):
    kb, _, width = ar_ref.shape
    reps = width // 128
    lanes = lambda parts: jnp.concatenate(parts, axis=1)
    twr = lanes([t for j in range(kb) for t in [twr_ref[j]] * reps])
    twi = lanes([t for j in range(kb) for t in [twi_ref[j]] * reps])
    ar = lanes([ar_ref[j] for j in range(kb)])
    ai = lanes([ai_ref[j] for j in range(kb)])
    z = jnp.concatenate([ar * twr - ai * twi, ar * twi + ai * twr], axis=0)
    x = _dot3c(f2h_ref[...], f2l_ref[...], z)
    xr, xi = x[:n2], x[n2:]
    if filter_only:
        kr_out, ki_out = rest
        for j in range(kb):
            kr_out[j] = xr[:, j * width:(j + 1) * width]
            ki_out[j] = xi[:, j * width:(j + 1) * width]
        return
    kr_ref, ki_ref, g2h_ref, g2l_ref, br_out, bi_out = rest
    kr = lanes([kr_ref[j] for j in range(kb)])
    ki = lanes([ki_ref[j] for j in range(kb)])
    y = jnp.concatenate([xr * kr - xi * ki, xr * ki + xi * kr], axis=0)
    w = _dot3c(g2h_ref[...], g2l_ref[...], y)
    wr, wi = w[:n2], w[n2:]
    br = wr * twr + wi * twi
    bi = wi * twr - wr * twi
    for j in range(kb):
        br_out[j] = br[:, j * width:(j + 1) * width]
        bi_out[j] = bi[:, j * width:(j + 1) * width]


def _fft_mid(ar, ai, consts, n2, width, spectrum=None):
    a3r = ar.reshape(FFT_N1, n2, width)
    a3i = ai.reshape(FFT_N1, n2, width)
    blk = pl.BlockSpec((FFT_KB, n2, width), lambda k: (k, 0, 0))
    twspec = pl.BlockSpec((FFT_KB, n2, 128), lambda k: (k, 0, 0))
    cspec = pl.BlockSpec((2 * n2, 2 * n2), lambda k: (0, 0))
    in_specs = [blk, blk, twspec, twspec, cspec, cspec]
    args = [a3r, a3i, consts["twr"], consts["twi"], *consts["f2"]]
    if spectrum is not None:
        in_specs += [blk, blk, cspec, cspec]
        args += [spectrum[0], spectrum[1], *consts["g2"]]
    return pl.pallas_call(
        functools.partial(_fftmid_kernel, filter_only=spectrum is None, n2=n2),
        grid=(FFT_N1 // FFT_KB,),
        in_specs=in_specs,
        out_specs=[blk, blk],
        out_shape=[jax.ShapeDtypeStruct((FFT_N1, n2, width), F32)] * 2,
        compiler_params=_cparams(("arbitrary",)),
    )(*args)


def _fftlast_kernel(br_ref, bi_ref, gh_ref, gl_ref, u_ref, x0_ref, bias_ref, invn_ref, o_ref):
    b = jnp.concatenate([br_ref[...], bi_ref[...]], axis=0)
    y = _dot3c(gh_ref[...], gl_ref[...], b)
    o_ref[...] = ((y * invn_ref[...] + u_ref[...] * bias_ref[...]) * x0_ref[...]).astype(BF16)


def _fft_last(br, bi, gmat, uview, x0view, bias_t, invn_t, ncols):
    cb = min(FFT_COLS, ncols)
    gh, gl = gmat
    cspec = pl.BlockSpec(gh.shape, lambda j: (0, 0))
    dspec = pl.BlockSpec((FFT_N1, cb), lambda j: (0, j))
    vspec = pl.BlockSpec((1, cb), lambda j: (0, 0))
    return pl.pallas_call(
        _fftlast_kernel,
        grid=(ncols // cb,),
        in_specs=[dspec, dspec, cspec, cspec, dspec, dspec, vspec, vspec],
        out_specs=dspec,
        out_shape=jax.ShapeDtypeStruct((FFT_N1, ncols), BF16),
        compiler_params=_cparams(("arbitrary",)),
    )(br, bi, gh, gl, uview, x0view, bias_t, invn_t)


def _dft_small_consts(seq):
    n = 2 * seq
    a = -2.0 * np.pi * np.outer(np.arange(n), np.arange(n)) / n
    fr, fi = np.cos(a), np.sin(a)
    f_data = np.block([[fr[:, :seq], -fi[:, :seq]], [fi[:, :seq], fr[:, :seq]]])
    f_real = np.concatenate([fr, fi], axis=0)
    gr, gi = fr / n, -fi / n
    g = np.block([[gr[:seq], -gi[:seq]], [gi[:seq], gr[:seq]]])
    return _np_split(f_data), _np_split(f_real), _np_split(g)


def _hy_ctx_kernel(u_ref, x0_ref, taps_ref, fdh_ref, fdl_ref, frh_ref, frl_ref, gh_ref, gl_ref,
                   bias_ref, invn_ref, o_ref):
    n = taps_ref.shape[0]
    z = u_ref[...]
    a = _dot3c(fdh_ref[...], fdl_ref[...], z)
    k = _dot3c(frh_ref[...], frl_ref[...], taps_ref[...])
    ar, ai, kr, ki = a[:n], a[n:], k[:n], k[n:]
    y = jnp.concatenate([ar * kr - ai * ki, ar * ki + ai * kr], axis=0)
    conv = _dot3c(gh_ref[...], gl_ref[...], y)
    o_ref[...] = ((conv * invn_ref[...] + z * bias_ref[...]) * x0_ref[...]).astype(BF16)


def _hy_ctx(u, x0, taps, bias, invn, small):
    rows, hw = u.shape
    full = lambda a: pl.BlockSpec(a.shape, lambda i: (0,) * a.ndim)
    args = [u, x0, taps, *small[0], *small[1], *small[2], bias, invn]
    return pl.pallas_call(
        _hy_ctx_kernel,
        grid=(1,),
        in_specs=[full(a) for a in args],
        out_specs=pl.BlockSpec((rows, hw), lambda i: (0, 0)),
        out_shape=jax.ShapeDtypeStruct((rows, hw), BF16),
        compiler_params=_cparams(("arbitrary",)),
    )(*args)


def _out_proj_kernel(x_ref, ypl_ref, yhl_ref, yal_ref, ypc_ref, yhc_ref, yac_ref, w_ref, g_ref, mod_ref, *rest,
                     nlt, tpb, ncond, pw, hw, moe, ntiles):
    if moe:
        rw_ref, xo_ref, u_ref, route_ref, un_ref = rest
    else:
        xo_ref, u_ref = rest
    t = jnp.minimum(pl.program_id(0), ntiles - 1)
    if moe:
        @pl.when(pl.program_id(0) == 0)
        def _():
            un_ref[...] = jnp.zeros_like(un_ref)

        logits = _dot3(un_ref[...], rw_ref[...])
    ci = _cond_row(t, nlt, tpb, ncond)
    is_ctx = t >= nlt
    yp = jnp.where(is_ctx, ypc_ref[...], ypl_ref[...])
    yh = jnp.where(is_ctx, yhc_ref[...], yhl_ref[...])
    ya = jnp.where(is_ctx, yac_ref[...], yal_ref[...])
    mix = _dot(yp, w_ref[0:pw, :]) + _dot(yh, w_ref[pw:pw + hw, :]) + _dot(ya, w_ref[pw + hw:, :])
    x = x_ref[...] + mod_ref[2, pl.ds(ci, 1), :] * mix
    xo_ref[...] = x
    un = _normmod(x, g_ref[...], mod_ref[3, pl.ds(ci, 1), :], mod_ref[4, pl.ds(ci, 1), :])
    u_ref[...] = un.astype(BF16)
    if moe:
        un_ref[...] = un
        lane = lax.broadcasted_iota(jnp.int32, logits.shape, 1)
        neg = jnp.float32(-jnp.inf)
        lg = jnp.where(lane < N_EXPERTS, logits, neg)
        t1 = jnp.max(lg, axis=-1, keepdims=True)
        i1 = jnp.min(jnp.where(lg == t1, lane, 128), axis=-1, keepdims=True)
        lg2 = jnp.where(lane == i1, neg, lg)
        t2 = jnp.max(lg2, axis=-1, keepdims=True)
        i2 = jnp.min(jnp.where(lg2 == t2, lane, 128), axis=-1, keepdims=True)
        e2 = jnp.exp(t2 - t1)
        g1 = 1.0 / (1.0 + e2)
        g2 = e2 / (1.0 + e2)
        route_ref[...] = jnp.where(lane == 0, i1.astype(F32), jnp.where(lane == 1, i2.astype(F32),
                                   jnp.where(lane == 2, g1, jnp.where(lane == 3, g2, 0.0))))


def _out_proj(x, lat, ctx, w, g, mod, rw, *, nlt, tpb, ncond):
    n, d = x.shape
    pw, hw, aw = (a.shape[1] for a in lat)
    tm = ROW_TILE
    moe = rw is not None
    nt = n // tm
    cur = lambda t: jnp.minimum(t, nt - 1)
    row = lambda width: pl.BlockSpec((tm, width), lambda t: (cur(t), 0))
    lrow = lambda width: pl.BlockSpec((tm, width), lambda t: (jnp.minimum(t, nlt - 1), 0))
    crow = lambda width: pl.BlockSpec((tm, width), lambda t: (jnp.maximum(cur(t) - nlt, 0), 0))
    in_specs = [row(d), lrow(pw), lrow(hw), lrow(aw), crow(pw), crow(hw), crow(aw),
                pl.BlockSpec(w.shape, lambda t: (0, 0)),
                pl.BlockSpec((1, d), lambda t: (0, 0)),
                pl.BlockSpec(mod.shape, lambda t: (0, 0, 0))]
    args = [x, *lat, *ctx, w, g, mod]
    out_specs = [row(d), row(d)]
    out_shape = [jax.ShapeDtypeStruct((n, d), F32), jax.ShapeDtypeStruct((n, d), BF16)]
    scratch = []
    if moe:
        in_specs.append(pl.BlockSpec(rw.shape, lambda t: (0, 0)))
        args.append(rw)
        out_specs.append(pl.BlockSpec((tm, 128), lambda t: (jnp.maximum(t - 1, 0), 0)))
        out_shape.append(jax.ShapeDtypeStruct((n, 128), F32))
        scratch.append(pltpu.VMEM((tm, d), F32))
    kern = functools.partial(_out_proj_kernel, nlt=nlt, tpb=tpb, ncond=ncond, pw=pw, hw=hw, moe=moe, ntiles=nt)
    return pl.pallas_call(
        kern, grid=(nt + 1 if moe else nt,), in_specs=in_specs, out_specs=out_specs, out_shape=out_shape,
        scratch_shapes=scratch,
        compiler_params=_cparams(("arbitrary",)),
    )(*args)


def _ffn_kernel(x_ref, u_ref, mod_ref, wg_ref, wu_ref, wd_ref, o_ref, *, nlt, tpb, ncond, nchunk):
    ci = _cond_row(pl.program_id(0), nlt, tpb, ncond)
    u = u_ref[...]
    ff = wg_ref.shape[1]
    fc = ff // nchunk
    y = jnp.zeros(x_ref.shape, F32)
    for c in range(nchunk):
        gate = _dot(u, wg_ref[:, c * fc:(c + 1) * fc])
        up = _dot(u, wu_ref[:, c * fc:(c + 1) * fc])
        y = y + _dot((_silu(gate) * up).astype(BF16), wd_ref[c * fc:(c + 1) * fc, :])
    o_ref[...] = x_ref[...] + mod_ref[5, pl.ds(ci, 1), :] * y


def _ffn(x, u, mod, wg, wu, wd, *, nlt, tpb, ncond, out_rows):
    n, d = x.shape
    ff = wg.shape[1]
    tm = ROW_TILE
    nchunk = 2 if (ff // 2) % 128 == 0 else 1
    row = lambda dt: pl.BlockSpec((tm, d), lambda t: (t, 0))
    const = lambda shape: pl.BlockSpec(shape, lambda t: (0,) * len(shape), pipeline_mode=pl.Buffered(1))
    kern = functools.partial(_ffn_kernel, nlt=nlt, tpb=tpb, ncond=ncond, nchunk=nchunk)
    return pl.pallas_call(
        kern,
        grid=(out_rows // tm,),
        in_specs=[row(F32), row(BF16), pl.BlockSpec(mod.shape, lambda t: (0, 0, 0)),
                  const((d, ff)), const((d, ff)), const((ff, d))],
        out_specs=row(F32),
        out_shape=jax.ShapeDtypeStruct((out_rows, d), F32),
        compiler_params=_cparams(("arbitrary",)),
    )(x, u, mod, wg, wu, wd)


def _moe_kernel(te_ref, tv_ref, u_ref, wg_ref, wu_ref, wd_ref, o_ref, acc_ref):
    t = pl.program_id(0)
    f = pl.program_id(1)

    @pl.when(tv_ref[t] > 0)
    def _():
        u = u_ref[...]
        h = (_silu(_dot(u, wg_ref[0, 0])) * _dot(u, wu_ref[0, 0])).astype(BF16)
        y = _dot(h, wd_ref[0, 0])

        @pl.when(f == 0)
        def _():
            acc_ref[...] = y

        @pl.when(f != 0)
        def _():
            acc_ref[...] = acc_ref[...] + y

        @pl.when(f == pl.num_programs(1) - 1)
        def _():
            o_ref[...] = acc_ref[...].astype(BF16)


def _moe_experts(ug, tile_expert, tile_valid, wg, wu, wd, j):
    p, d = ug.shape
    ff = wg.shape[3]
    tm, fc = MOE_TM, MOE_FC
    grid_spec = pltpu.PrefetchScalarGridSpec(
        num_scalar_prefetch=2,
        grid=(p // tm, ff // fc),
        in_specs=[
            pl.BlockSpec((tm, d), lambda t, f, te, tv: (t, 0)),
            pl.BlockSpec((1, 1, d, fc), lambda t, f, te, tv: (j, te[t], 0, f)),
            pl.BlockSpec((1, 1, d, fc), lambda t, f, te, tv: (j, te[t], 0, f)),
            pl.BlockSpec((1, 1, fc, d), lambda t, f, te, tv: (j, te[t], f, 0)),
        ],
        out_specs=pl.BlockSpec((tm, d), lambda t, f, te, tv: (t, 0)),
        scratch_shapes=[pltpu.VMEM((tm, d), F32)],
    )
    return pl.pallas_call(
        _moe_kernel, grid_spec=grid_spec,
        out_shape=jax.ShapeDtypeStruct((p, d), BF16),
        compiler_params=_cparams(("arbitrary", "arbitrary")),
    )(tile_expert, tile_valid, ug, wg, wu, wd)


def _moe_combine_kernel(x_ref, ya_ref, yb_ref, route_ref, mod_ref, o_ref, *, nlt, tpb, ncond):
    ci = _cond_row(pl.program_id(0), nlt, tpb, ncond)
    r = route_ref[...]
    lane = lax.broadcasted_iota(jnp.int32, r.shape, 1)
    g1 = jnp.sum(jnp.where(lane == 2, r, 0.0), axis=-1, keepdims=True)
    g2 = jnp.sum(jnp.where(lane == 3, r, 0.0), axis=-1, keepdims=True)
    y = g1 * ya_ref[...].astype(F32) + g2 * yb_ref[...].astype(F32)
    o_ref[...] = x_ref[...] + mod_ref[5, pl.ds(ci, 1), :] * y


def _moe_combine(x, ya, yb, route, mod, *, nlt, tpb, ncond, out_rows):
    n, d = x.shape
    tm = ROW_TILE
    row = lambda width: pl.BlockSpec((tm, width), lambda t: (t, 0))
    kern = functools.partial(_moe_combine_kernel, nlt=nlt, tpb=tpb, ncond=ncond)
    return pl.pallas_call(
        kern,
        grid=(out_rows // tm,),
        in_specs=[row(d), row(d), row(d), row(128), pl.BlockSpec(mod.shape, lambda t: (0, 0, 0))],
        out_specs=row(d),
        out_shape=jax.ShapeDtypeStruct((out_rows, d), F32),
        compiler_params=_cparams(("arbitrary",)),
    )(x, ya, yb, route, mod)


def _moe_layer(x, u, route, mod, wg, wu, wd, j, *, nlt, tpb, ncond, out_rows):
    n, d = x.shape
    tm = MOE_TM
    experts = jnp.concatenate([route[:, 0], route[:, 1]]).astype(jnp.int32)
    onehot = (experts[:, None] == jnp.arange(N_EXPERTS, dtype=jnp.int32)[None, :]).astype(jnp.int32)
    rank = jnp.sum(onehot * (jnp.cumsum(onehot, axis=0) - 1), axis=1)
    counts = jnp.sum(onehot, axis=0)
    padded = ((counts + tm - 1) // tm) * tm
    ends = jnp.cumsum(padded)
    starts = ends - padded
    dest = starts[experts] + rank
    p = 2 * n + N_EXPERTS * tm
    tokens = jnp.concatenate([jnp.arange(n, dtype=jnp.int32)] * 2)
    inb = dict(mode="promise_in_bounds")
    src = jnp.zeros((p,), jnp.int32).at[dest].set(tokens, unique_indices=True, **inb)
    tile_start = jnp.arange(p // tm, dtype=jnp.int32) * tm
    tile_expert = jnp.minimum(jnp.searchsorted(ends, tile_start, side="right"), N_EXPERTS - 1).astype(jnp.int32)
    tile_valid = (tile_start < ends[-1]).astype(jnp.int32)
    ys = _moe_experts(u.at[src].get(**inb), tile_expert, tile_valid, wg, wu, wd, j)
    ya = ys.at[dest[:n]].get(**inb)
    yb = ys.at[dest[n:]].get(**inb)
    return _moe_combine(x, ya, yb, route, mod, nlt=nlt, tpb=tpb, ncond=ncond, out_rows=out_rows)


def _rope_tables(seq, n_ctx_rows, batch):
    rows = seq // GRID_W
    row = jnp.repeat(jnp.arange(rows, dtype=F32), GRID_W)
    col = jnp.broadcast_to(jnp.arange(GRID_W, dtype=F32), (rows, GRID_W)).reshape(-1)
    inv_freq = jnp.power(ROPE_THETA, -jnp.arange(ROPE_FREQS, dtype=F32) / ROPE_FREQS)
    ar = row[:, None] * inv_freq
    ac = col[:, None] * inv_freq
    cos = jnp.concatenate([jnp.cos(ar), jnp.cos(ar), jnp.cos(ac), jnp.cos(ac)], axis=-1)
    sin = jnp.concatenate([-jnp.sin(ar), jnp.sin(ar), -jnp.sin(ac), jnp.sin(ac)], axis=-1)
    cos = jnp.tile(jnp.concatenate([cos, cos], axis=-1), (batch, 1))
    sin = jnp.tile(jnp.concatenate([sin, sin], axis=-1), (batch, 1))
    cos = jnp.concatenate([cos, jnp.ones((n_ctx_rows, 2 * HEAD_DIM), F32)], axis=0)
    sin = jnp.concatenate([sin, jnp.zeros((n_ctx_rows, 2 * HEAD_DIM), F32)], axis=0)
    return cos, sin


def kernel(x, c, ctx, c_ctx, mod_w, mod_b, norm1_g, norm2_g, w_in, w_out, pool_lin, pool_scale, hy_short_w, hy_short_b, hy_f_w1, hy_f_b1, hy_f_freq1, hy_f_w2, hy_f_b2, hy_f_freq2, hy_f_w3, hy_bias, qk_norm_g, diff_lambda, subln_g, ffn_w_gate, ffn_w_up, ffn_w_down, router_w, moe_w_gate, moe_w_up, moe_w_down):
    batch, seq, d = x.shape
    ctx_len = ctx.shape[1]
    depth = mod_w.shape[0]
    pw = pool_scale.shape[1]
    hw = hy_bias.shape[1]
    nl, nc = batch * seq, batch * ctx_len
    n = nl + nc
    tm = ROW_TILE
    assert seq % tm == 0 and nc % tm == 0 and seq % (FFT_N1 // 2) == 0 and seq % GRID_W == 0
    assert d == HEADS * 2 * V_DIM and pw == hw and batch == 2
    n2 = 2 * seq // FFT_N1
    ncols = n2 * hw
    assert (n * hw) % ncols == 0
    tiles = dict(nlt=nl // tm, tpb=seq // tm, ncond=batch + 1)

    xs = jnp.concatenate([x.reshape(nl, d), ctx.reshape(nc, d)], axis=0)
    cond8 = jnp.concatenate([c, c_ctx[None, :], jnp.zeros((8 - batch - 1, d), F32)], axis=0)
    mods = _modvec(cond8, mod_w, mod_b)
    cos_t, sin_t = _rope_tables(seq, nc, batch)
    consts = _dft_consts(n2)
    small = _dft_small_consts(ctx_len)
    eye = jnp.eye(len(POOL_WINDOWS), dtype=F32)
    moe_wg, moe_wu, moe_wd = _cast_bf16(moe_w_gate, moe_w_up, moe_w_down)

    for i in range(depth):
        last = i == depth - 1
        lam_init = 0.8 - 0.6 * math.exp(-0.3 * i)
        mod = mods[i]
        gqk = jnp.concatenate([qk_norm_g[i], qk_norm_g[i]], axis=-1)
        zp, zh, qt, k12, vt = _in_proj(xs, norm1_g[i][None, :], mod, w_in[i].astype(BF16), cos_t, sin_t, gqk,
                                       pw=pw, hw=hw, **tiles)

        g_b = jnp.broadcast_to(subln_g[i][:, None], (V_DIM, ATT_TQ))
        att_kw = dict(batch=batch, seq=seq, ctx_len=ctx_len, lam_init=lam_init)

        def attend(direct, lam_p=diff_lambda[i], qt=qt, k12=k12, vt=vt, g_b=g_b, att_kw=att_kw):
            return (_attention(lam_p, qt, k12, vt, g_b, latent=True, direct=direct, **att_kw),
                    _attention(lam_p, qt, k12, vt, g_b, latent=False, direct=direct, **att_kw))

        bound = (HEAD_DIM * QSCALE) * jnp.max(jnp.abs(qk_norm_g[i][0])) * jnp.max(jnp.abs(qk_norm_g[i][1]))
        ya_l, ya_c = lax.cond(bound * 1.02 < ATT_DIRECT_MAX, lambda: attend(True), lambda: attend(False))

        lin_bd = (eye[:, None, :, None] * pool_lin[i][:, :, None, :]).reshape(pw, pw).astype(BF16)
        pscale = pool_scale[i][None, :]
        yp_l = _pool(zp, lin_bd, pscale, row0=0, batch=batch, seq=seq, out_rows=nl, out_row0=0)
        yp_c = _pool(zp, lin_bd, pscale, row0=nl, batch=batch, seq=ctx_len, out_rows=nc, out_row0=0)

        sw, sb = hy_short_w[i], hy_short_b[i][None, :]
        u_l, x0_l = _hy_pre(zh, sw, sb, row0=0, batch=batch, seq=seq, out_rows=n, out_row0=0)
        u_c, x0_c = _hy_pre(zh, sw, sb, row0=nl, batch=batch, seq=ctx_len, out_rows=nc, out_row0=0)
        filt = (hy_f_w1[i], hy_f_b1[i], hy_f_freq1[i], hy_f_w2[i], hy_f_b2[i], hy_f_freq2[i], hy_f_w3[i])
        bias = hy_bias[i][None, :]
        taps, asum = _hy_filter(seq, *filt, hw)
        kr1, ki1 = _fft_first(taps.reshape(FFT_N1, ncols), consts["f1_real"], ncols)
        spectrum = _fft_mid(kr1, ki1, consts, n2, hw)
        ar, ai = _fft_first(u_l.reshape(-1, ncols), consts["f1_data"], ncols)
        br, bi = _fft_mid(ar, ai, consts, n2, hw, spectrum=spectrum)
        reps = min(FFT_COLS, ncols) // hw
        yh_lat = _fft_last(br.reshape(FFT_N1, ncols), bi.reshape(FFT_N1, ncols), consts["g1"],
                           u_l.reshape(-1, ncols), x0_l.reshape(-1, ncols),
                           jnp.tile(bias, (1, reps)), jnp.tile(1.0 / asum, (1, reps)), ncols)
        taps_c, asum_c = _hy_filter(ctx_len, *filt, hw)
        yh_ctx = _hy_ctx(u_c, x0_c, taps_c, bias, 1.0 / asum_c, small)

        j = i // 2
        moe = i % 2 == 1
        rw = None
        if moe:
            rw = jnp.concatenate([router_w[j], jnp.zeros((d, 128 - N_EXPERTS), F32)], axis=1)
        outs = _out_proj(xs, (yp_l, yh_lat.reshape(nl, hw), ya_l), (yp_c, yh_ctx, ya_c), w_out[i].astype(BF16),
                         norm2_g[i][None, :], mod, rw, **tiles)
        out_rows = nl if last else n
        if moe:
            xs, u, route = outs
            xs = _moe_layer(xs, u, route, mod, moe_wg, moe_wu, moe_wd, j, out_rows=out_rows, **tiles)
        else:
            xs, u = outs
            xs = _ffn(xs, u, mod, ffn_w_gate[j].astype(BF16), ffn_w_up[j].astype(BF16),
                      ffn_w_down[j].astype(BF16), out_rows=out_rows, **tiles)
    return xs[:nl].reshape(batch, seq, d)
```

```python
import functools
import math

import numpy as np
import jax
import jax.numpy as jnp
from jax import lax
from jax.experimental import pallas as pl
from jax.experimental.pallas import tpu as pltpu

F32 = jnp.float32
BF16 = jnp.bfloat16
EPS = 1e-6

GRID_W = 64
POOL_WINDOWS = (2, 4, 8, 16)
HEADS = 4
HEAD_DIM = 64
V_DIM = 128
ROPE_FREQS = 16
ROPE_THETA = 10000.0
HY_BANDS = 16
HY_TARGET = 1e-2
HY_FAST = 0.3
HY_SLOW = 1.5
N_EXPERTS = 8

ROW_TILE = 512
HALO = 32
FFT_N1 = 128
FFT_COLS = 2048
FFT_KB = 4
ATT_TQ = 1024
ATT_TK = 512
ATT_UNROLL = 16
ATT_GROUP = 8
QSCALE = (HEAD_DIM ** -0.5) * math.log2(math.e)
ATT_DIRECT_MAX = 100.0
CAST_STEPS = 64
MOE_TM = 512
MOE_FC = 1792
VMEM_LIMIT = 56 * 1024 * 1024


def _cparams(sem):
    return pltpu.CompilerParams(dimension_semantics=sem, vmem_limit_bytes=VMEM_LIMIT)


def _dot(a, b):
    return jnp.dot(a, b, preferred_element_type=F32)


def _split(a):
    hi = a.astype(BF16)
    lo = (a - hi.astype(F32)).astype(BF16)
    return hi, lo


def _dot3(a, b):
    ah, al = _split(a)
    bh, bl = _split(b)
    return _dot(ah, bh) + _dot(ah, bl) + _dot(al, bh)


def _dot3c(ch, cl, d):
    dh, dl = _split(d)
    return _dot(ch, dh) + _dot(ch, dl) + _dot(cl, dh)


def _silu(x):
    return x / (1.0 + jnp.exp(-x))


def _normmod(x, g, shift, scale):
    ms = jnp.mean(x * x, axis=-1, keepdims=True)
    return (x * lax.rsqrt(ms + EPS) * g) * (1.0 + scale) + shift


def _cond_row(t, n_lat_tiles, tiles_per_batch, n_cond):
    return jnp.where(t >= n_lat_tiles, n_cond - 1, t // tiles_per_batch)


def _cast_kernel(*refs):
    half = len(refs) // 2
    for w_ref, o_ref in zip(refs[:half], refs[half:]):
        o_ref[...] = w_ref[...].astype(BF16)


def _cast_bf16(*ws):
    flat = [w.reshape(-1, w.shape[-1]) for w in ws]
    steps = CAST_STEPS
    specs = [pl.BlockSpec((w.shape[0] // steps, w.shape[1]), lambda t: (t, 0)) for w in flat]
    outs = pl.pallas_call(
        _cast_kernel,
        grid=(steps,),
        in_specs=specs,
        out_specs=specs,
        out_shape=[jax.ShapeDtypeStruct(w.shape, BF16) for w in flat],
        compiler_params=_cparams(("arbitrary",)),
    )(*flat)
    return [o.reshape(w.shape) for o, w in zip(outs, ws)]


def _modvec_kernel(c_ref, w_ref, b_ref, o_ref):
    s = _silu(c_ref[...])
    o_ref[0, 0] = _dot3(s, w_ref[0]) + b_ref[0]


def _modvec(cond8, mod_w, mod_b):
    depth, d, six_d = mod_w.shape
    nchunk = six_d // d
    return pl.pallas_call(
        _modvec_kernel,
        grid=(depth, nchunk),
        in_specs=[
            pl.BlockSpec((8, d), lambda i, j: (0, 0)),
            pl.BlockSpec((1, d, d), lambda i, j: (i, 0, j)),
            pl.BlockSpec((1, 1, d), lambda i, j: (i, 0, j)),
        ],
        out_specs=pl.BlockSpec((1, 1, 8, d), lambda i, j: (i, j, 0, 0)),
        out_shape=jax.ShapeDtypeStruct((depth, nchunk, 8, d), F32),
        compiler_params=_cparams(("arbitrary", "arbitrary")),
    )(cond8, mod_w, mod_b.reshape(depth, 1, six_d))


def _in_proj_kernel(x_ref, g_ref, mod_ref, w_ref, cos_ref, sin_ref, gqk_ref,
                    zp_ref, zh_ref, qt_ref, k_ref, vt_ref, zatt_ref, *, nlt, tpb, ncond, pw, hw, ntiles):
    t = pl.program_id(0)

    @pl.when(t == 0)
    def _():
        zatt_ref[...] = jnp.zeros_like(zatt_ref)

    tm = x_ref.shape[0]
    att = pw + 3 * hw
    lane = lax.broadcasted_iota(jnp.int32, (tm, 2 * HEAD_DIM), 1)
    first = lane < HEAD_DIM
    apart = (lane % (2 * ROPE_FREQS)) < ROPE_FREQS
    cos = cos_ref[...]
    sin = sin_ref[...]
    qk_w = HEADS * 2 * HEAD_DIM

    def norm_rope(v, g):
        v2 = v * v
        s_all = jnp.sum(v2, axis=-1, keepdims=True)
        s_first = jnp.sum(jnp.where(first, v2, 0.0), axis=-1, keepdims=True)
        ms = jnp.where(first, s_first, s_all - s_first) * (1.0 / HEAD_DIM)
        vn = v * lax.rsqrt(ms + EPS) * g
        swapped = jnp.where(apart, pltpu.roll(vn, 2 * HEAD_DIM - ROPE_FREQS, 1), pltpu.roll(vn, ROPE_FREQS, 1))
        return vn * cos + swapped * sin

    for h in range(HEADS):
        lo = h * 2 * HEAD_DIM
        q = norm_rope(zatt_ref[:, lo:lo + 2 * HEAD_DIM], gqk_ref[0:1, :])
        qt_ref[h] = (q * QSCALE).T.astype(BF16)
        k = norm_rope(zatt_ref[:, qk_w + lo:qk_w + lo + 2 * HEAD_DIM], gqk_ref[1:2, :])
        k_ref[h] = k.astype(BF16)
        vlo = 2 * qk_w + h * V_DIM
        vt_ref[h] = zatt_ref[:, vlo:vlo + V_DIM].T.astype(BF16)

    ci = _cond_row(jnp.minimum(t, ntiles - 1), nlt, tpb, ncond)
    shift = mod_ref[0, pl.ds(ci, 1), :]
    scale = mod_ref[1, pl.ds(ci, 1), :]
    y = _normmod(x_ref[...], g_ref[...], shift, scale).astype(BF16)
    z = _dot(y, w_ref[...])
    zp_ref[...] = z[:, :pw].astype(BF16)
    zh_ref[...] = z[:, pw:att].astype(BF16)
    zatt_ref[...] = z[:, att:]


def _in_proj(x, g, mod, w, cos_t, sin_t, gqk, *, nlt, tpb, ncond, pw, hw):
    n, d = x.shape
    wid = w.shape[1]
    tm = ROW_TILE
    nt = n // tm
    kern = functools.partial(_in_proj_kernel, nlt=nlt, tpb=tpb, ncond=ncond, pw=pw, hw=hw, ntiles=nt)
    cur = lambda t: jnp.minimum(t, nt - 1)
    prv = lambda t: jnp.maximum(t - 1, 0)
    return pl.pallas_call(
        kern,
        grid=(nt + 1,),
        in_specs=[
            pl.BlockSpec((tm, d), lambda t: (cur(t), 0)),
            pl.BlockSpec((1, d), lambda t: (0, 0)),
            pl.BlockSpec(mod.shape, lambda t: (0, 0, 0)),
            pl.BlockSpec((d, wid), lambda t: (0, 0)),
            pl.BlockSpec((tm, 2 * HEAD_DIM), lambda t: (prv(t), 0)),
            pl.BlockSpec((tm, 2 * HEAD_DIM), lambda t: (prv(t), 0)),
            pl.BlockSpec((2, 2 * HEAD_DIM), lambda t: (0, 0)),
        ],
        out_specs=[
            pl.BlockSpec((tm, pw), lambda t: (cur(t), 0)),
            pl.BlockSpec((tm, 3 * hw), lambda t: (cur(t), 0)),
            pl.BlockSpec((HEADS, 2 * HEAD_DIM, tm), lambda t: (0, 0, prv(t))),
            pl.BlockSpec((HEADS, tm, 2 * HEAD_DIM), lambda t: (0, prv(t), 0)),
            pl.BlockSpec((HEADS, V_DIM, tm), lambda t: (0, 0, prv(t))),
        ],
        scratch_shapes=[pltpu.VMEM((tm, wid - pw - 3 * hw), F32)],
        out_shape=[
            jax.ShapeDtypeStruct((n, pw), BF16),
            jax.ShapeDtypeStruct((n, 3 * hw), BF16),
            jax.ShapeDtypeStruct((HEADS, 2 * HEAD_DIM, n), BF16),
            jax.ShapeDtypeStruct((HEADS, n, 2 * HEAD_DIM), BF16),
            jax.ShapeDtypeStruct((HEADS, V_DIM, n), BF16),
        ],
        compiler_params=_cparams(("arbitrary",)),
    )(x, g, mod, w, cos_t, sin_t, gqk)


def _attn_kernel(lam_ref, qt_ref, kc_ref, vtc_ref, *rest, tk, n_lat_chunks, lam_init):
    if n_lat_chunks:
        kl_ref, vtl_ref, g_ref, o_ref, m_ref, l_ref, acc_ref = rest
    else:
        g_ref, o_ref, m_ref, l_ref, acc_ref = rest
    qt = qt_ref[0]
    row = lax.broadcasted_iota(jnp.int32, qt.shape, 0)
    zero = jnp.zeros_like(qt)
    qmaps = (jnp.where(row < HEAD_DIM, qt, zero), jnp.where(row >= HEAD_DIM, qt, zero))

    def process(k_tile, vt_tile, first):
        for mi in range(2):
            s = _dot(k_tile, qmaps[mi])
            smax = jnp.max(s, axis=0, keepdims=True)
            if first:
                m_new = smax
            else:
                m_old = m_ref[mi]
                m_new = jnp.maximum(m_old, smax)
            p = jnp.exp2(s - m_new)
            psum = jnp.sum(p, axis=0, keepdims=True)
            pv = _dot(vt_tile, p.astype(BF16))
            if first:
                l_ref[mi] = psum
                acc_ref[mi] = pv
            else:
                alpha = jnp.exp2(m_old - m_new)
                l_ref[mi] = alpha * l_ref[mi] + psum
                acc_ref[mi] = alpha * acc_ref[mi] + pv
            m_ref[mi] = m_new

    process(kc_ref[0], vtc_ref[0], True)
    if n_lat_chunks:
        def body(j, carry):
            off = pl.multiple_of(j * tk, tk)
            process(kl_ref[0, pl.ds(off, tk), :], vtl_ref[0, :, pl.ds(off, tk)], False)
            return carry
        lax.fori_loop(0, n_lat_chunks, body, 0)

    lv = lam_ref[...]
    lam = (jnp.exp(jnp.sum(lv[0:1] * lv[1:2], axis=-1, keepdims=True))
           - jnp.exp(jnp.sum(lv[2:3] * lv[3:4], axis=-1, keepdims=True)) + lam_init)
    o = acc_ref[0] * (1.0 / l_ref[0]) - lam * (acc_ref[1] * (1.0 / l_ref[1]))
    ms = jnp.mean(o * o, axis=0, keepdims=True)
    y = o * lax.rsqrt(ms + EPS) * g_ref[...] * (1.0 - lam_init)
    o_ref[...] = y.T.astype(BF16)


def _attn_direct_kernel(lam_ref, qt_ref, kc_ref, vtc_ref, *rest, tk, n_lat_chunks, lam_init):
    if n_lat_chunks:
        kl_ref, vtl_ref, g_ref, o_ref, p_ref, l_ref, acc_ref = rest
    else:
        g_ref, o_ref, p_ref, l_ref, acc_ref = rest
    qt = qt_ref[0]
    tq = qt.shape[1]
    row = lax.broadcasted_iota(jnp.int32, qt.shape, 0)
    zero = jnp.zeros_like(qt)
    qmaps = (jnp.where(row < HEAD_DIM, qt, zero), jnp.where(row >= HEAD_DIM, qt, zero))
    grp = ATT_GROUP

    def keys(first_piece, count):
        if isinstance(first_piece, int):
            return pl.ds((first_piece - 1) * tk, count * tk)
        return pl.ds(pl.multiple_of((first_piece - 1) * tk, tk), count * tk)

    def stage_ab(k_tile, pset, r):
        for mi in range(2):
            p = jnp.exp2(_dot(k_tile, qmaps[mi]))
            l_ref[mi] = l_ref[mi] + jnp.sum(p.reshape(tk // 8, 8, tq), axis=0)
            p_ref[pset, mi, r * tk:(r + 1) * tk, :] = p.astype(BF16)

    def stage_c(vt_tile, pset, lo, count):
        for mi in range(2):
            acc_ref[mi] = acc_ref[mi] + _dot(vt_tile, p_ref[pset, mi, lo * tk:(lo + count) * tk, :])

    def group_ab(g, pset):
        k_tile = kl_ref[0, keys(grp * g, grp), :]
        for mi in range(2):
            p = jnp.exp2(_dot(k_tile, qmaps[mi]))
            l_ref[mi] = l_ref[mi] + jnp.sum(p.reshape(grp * tk // 8, 8, tq), axis=0)
            p_ref[pset, mi] = p.astype(BF16)

    def group_c(g, pset):
        stage_c(vtl_ref[0, :, keys(grp * g, grp)], pset, 0, grp)

    l_ref[...] = jnp.zeros_like(l_ref)
    acc_ref[...] = jnp.zeros_like(acc_ref)
    stage_ab(kc_ref[0], 0, 0)
    if n_lat_chunks == 0:
        stage_c(vtc_ref[0], 0, 0, 1)
    else:
        n_groups = n_lat_chunks // grp
        per_step = ATT_UNROLL // grp
        for r in range(1, grp):
            stage_ab(kl_ref[0, keys(r, 1), :], 0, r)
        group_ab(1, 1)
        stage_c(vtc_ref[0], 0, 0, 1)
        stage_c(vtl_ref[0, :, keys(1, grp - 1)], 0, 1, grp - 1)
        for g in range(2, per_step):
            group_ab(g, g % 2)
            group_c(g - 1, (g - 1) % 2)

        def body(jj, carry):
            g0 = per_step * (jj + 1)
            for q in range(per_step):
                group_ab(g0 + q, q % 2)
                group_c(g0 + q - 1, (q + 1) % 2)
            return carry

        lax.fori_loop(0, n_groups // per_step - 1, body, 0)
        stage_ab(kl_ref[0, keys(n_lat_chunks, 1), :], 0, 0)
        group_c(n_groups - 1, 1)
        stage_c(vtl_ref[0, :, keys(n_lat_chunks, 1)], 0, 0, 1)

    lv = lam_ref[...]
    lam = (jnp.exp(jnp.sum(lv[0:1] * lv[1:2], axis=-1, keepdims=True))
           - jnp.exp(jnp.sum(lv[2:3] * lv[3:4], axis=-1, keepdims=True)) + lam_init)
    l1 = jnp.sum(l_ref[0], axis=0, keepdims=True)
    l2 = jnp.sum(l_ref[1], axis=0, keepdims=True)
    o = acc_ref[0] * (1.0 / l1) - lam * (acc_ref[1] * (1.0 / l2))
    ms = jnp.mean(o * o, axis=0, keepdims=True)
    y = o * lax.rsqrt(ms + EPS) * g_ref[...] * (1.0 - lam_init)
    o_ref[...] = y.T.astype(BF16)


def _attention(lam_p, qt, k12, vt, g_b, *, batch, seq, ctx_len, latent, lam_init, direct):
    n = k12.shape[1]
    nl = batch * seq
    tq = ATT_TQ if latent else ctx_len
    nq = seq // tq if latent else 1
    tk = ctx_len if direct else ATT_TK
    assert seq % ((ATT_UNROLL if direct else 1) * tk) == 0
    qbase = 0 if latent else nl // tq

    def qrow(b, iq):
        return (b * nq + iq) if latent else (qbase + b)

    in_specs = [
        pl.BlockSpec(lam_p.shape, lambda b, h, iq: (0, 0)),
        pl.BlockSpec((1, 2 * HEAD_DIM, tq), lambda b, h, iq: (h, 0, qrow(b, iq))),
        pl.BlockSpec((1, ctx_len, 2 * HEAD_DIM), lambda b, h, iq: (h, nl // ctx_len + b, 0)),
        pl.BlockSpec((1, V_DIM, ctx_len), lambda b, h, iq: (h, 0, nl // ctx_len + b)),
    ]
    args = [lam_p, qt, k12, vt]
    if latent:
        in_specs += [
            pl.BlockSpec((1, seq, 2 * HEAD_DIM), lambda b, h, iq: (h, b, 0)),
            pl.BlockSpec((1, V_DIM, seq), lambda b, h, iq: (h, 0, b)),
        ]
        args += [k12, vt]
    in_specs.append(pl.BlockSpec((V_DIM, tq), lambda b, h, iq: (0, 0)))
    args.append(g_b[:, :tq])
    nrows = nl if latent else batch * ctx_len
    body = _attn_direct_kernel if direct else _attn_kernel
    kern = functools.partial(body, tk=tk, n_lat_chunks=(seq // tk if latent else 0), lam_init=lam_init)
    if direct:
        scratch = [pltpu.VMEM((2, 2, ATT_GROUP * tk, tq), BF16),
                   pltpu.VMEM((2, 8, tq), F32), pltpu.VMEM((2, V_DIM, tq), F32)]
    else:
        scratch = [pltpu.VMEM((2, 1, tq), F32), pltpu.VMEM((2, 1, tq), F32), pltpu.VMEM((2, V_DIM, tq), F32)]
    return pl.pallas_call(
        kern,
        grid=(batch, HEADS, nq),
        in_specs=in_specs,
        out_specs=pl.BlockSpec((tq, V_DIM), lambda b, h, iq: (b * nq + iq, h)),
        out_shape=jax.ShapeDtypeStruct((nrows, HEADS * V_DIM), BF16),
        scratch_shapes=scratch,
        compiler_params=_cparams(("arbitrary", "arbitrary", "arbitrary")),
    )(*args)


def _halo_specs(width, *, row0, seq, tile):
    hb = tile // HALO

    def cur(b, i):
        return ((row0 + b * seq) // tile + i, 0)

    def prev(b, i):
        first = (row0 + b * seq) // HALO
        return (jnp.maximum(first + i * hb - 1, first), 0)

    def nxt(b, i):
        first = (row0 + b * seq) // HALO
        return (jnp.minimum(first + (i + 1) * hb, first + seq // HALO - 1), 0)

    return [pl.BlockSpec((HALO, width), prev), pl.BlockSpec((tile, width), cur), pl.BlockSpec((HALO, width), nxt)]


def _fill_ext(ext_ref, prev_ref, cur_ref, next_ref, *, seq, tile):
    pos0 = pl.program_id(1) * tile
    width = cur_ref.shape[1]
    hpos = lax.broadcasted_iota(jnp.int32, (HALO, width), 0)
    ext_ref[0:HALO, :] = jnp.where(pos0 - HALO + hpos >= 0, prev_ref[...].astype(F32), 0.0)
    ext_ref[HALO:HALO + tile, :] = cur_ref[...].astype(F32)
    ext_ref[HALO + tile:, :] = jnp.where(pos0 + tile + hpos < seq, next_ref[...].astype(F32), 0.0)


def _pool_kernel(prev_ref, cur_ref, next_ref, lin_ref, scale_ref, o_ref, ext_ref, s_ref, *, seq, tile):
    _fill_ext(ext_ref, prev_ref, cur_ref, next_ref, seq=seq, tile=tile)
    width = cur_ref.shape[1]
    gd = width // len(POOL_WINDOWS)
    n0 = tile + 2 * HALO
    s_ref[0, 0:n0 - 8, :] = ext_ref[0:n0 - 8, :] + ext_ref[1:n0 - 7, :]
    for k in range(1, len(POOL_WINDOWS)):
        step = 1 << k
        ln = n0 - 8 * (k + 1)
        s_ref[k, 0:ln, :] = s_ref[k - 1, 0:ln, :] + s_ref[k - 1, step:step + ln, :]
    lane = lax.broadcasted_iota(jnp.int32, (tile, width), 1)
    pos = pl.program_id(1) * tile + lax.broadcasted_iota(jnp.int32, (tile, width), 0)
    grp = lane // gd
    wsum = jnp.zeros((tile, width), F32)
    half = jnp.zeros((tile, width), jnp.int32)
    for k, win in enumerate(POOL_WINDOWS):
        start = HALO - win // 2
        wsum = jnp.where(grp == k, s_ref[k, start:start + tile, :], wsum)
        half = jnp.where(grp == k, win // 2, half)
    cnt = jnp.minimum(pos + half, seq) - jnp.maximum(pos - half, 0)
    z = ext_ref[HALO:HALO + tile, :]
    dlt = (wsum / cnt.astype(F32) - z).astype(BF16)
    o_ref[...] = (_dot(dlt, lin_ref[...]) * scale_ref[...]).astype(BF16)


def _pool(zp, lin_bd, scale, *, row0, batch, seq, out_rows, out_row0):
    width = zp.shape[1]
    tile = min(ROW_TILE, seq)
    kern = functools.partial(_pool_kernel, seq=seq, tile=tile)
    return pl.pallas_call(
        kern,
        grid=(batch, seq // tile),
        in_specs=_halo_specs(width, row0=row0, seq=seq, tile=tile) + [
            pl.BlockSpec((width, width), lambda b, i: (0, 0)),
            pl.BlockSpec((1, width), lambda b, i: (0, 0)),
        ],
        out_specs=pl.BlockSpec((tile, width), lambda b, i: ((out_row0 + b * seq) // tile + i, 0)),
        out_shape=jax.ShapeDtypeStruct((out_rows, width), BF16),
        scratch_shapes=[
            pltpu.VMEM((tile + 2 * HALO, width), F32),
            pltpu.VMEM((4, tile + 2 * HALO, width), F32),
        ],
        compiler_params=_cparams(("arbitrary", "arbitrary")),
    )(zp, zp, zp, lin_bd, scale)


def _hy_pre_kernel(prev_ref, cur_ref, next_ref, w_ref, b_ref, u_ref, x0_ref, ext_ref, *, seq, tile, hw):
    _fill_ext(ext_ref, prev_ref, cur_ref, next_ref, seq=seq, tile=tile)
    y = b_ref[...] + ext_ref[HALO - 1:HALO - 1 + tile, :] * w_ref[0:1, :]
    y = y + ext_ref[HALO:HALO + tile, :] * w_ref[1:2, :]
    y = y + ext_ref[HALO + 1:HALO + 1 + tile, :] * w_ref[2:3, :]
    x0_ref[...] = y[:, :hw]
    u_ref[...] = y[:, 2 * hw:] * y[:, hw:2 * hw]


def _hy_pre(zh, sw, sb, *, row0, batch, seq, out_rows, out_row0):
    width = zh.shape[1]
    hw = width // 3
    tile = min(ROW_TILE, seq)
    kern = functools.partial(_hy_pre_kernel, seq=seq, tile=tile, hw=hw)
    ospec = pl.BlockSpec((tile, hw), lambda b, i: ((out_row0 + b * seq) // tile + i, 0))
    return pl.pallas_call(
        kern,
        grid=(batch, seq // tile),
        in_specs=_halo_specs(width, row0=row0, seq=seq, tile=tile) + [
            pl.BlockSpec((3, width), lambda b, i: (0, 0)),
            pl.BlockSpec((1, width), lambda b, i: (0, 0)),
        ],
        out_specs=[ospec, ospec],
        out_shape=[jax.ShapeDtypeStruct((out_rows, hw), F32), jax.ShapeDtypeStruct((out_rows, hw), F32)],
        scratch_shapes=[pltpu.VMEM((tile + 2 * HALO, width), F32)],
        compiler_params=_cparams(("arbitrary", "arbitrary")),
    )(zh, zh, zh, sw, sb)


def _filter_kernel(feat_ref, w1_ref, b1_ref, f1_ref, w2_ref, b2_ref, f2_ref, w3_ref, dl_ref,
                   taps_ref, asum_ref, *, tile, hw, seq, embp):
    feat = feat_ref[...]
    h = jnp.sin(f1_ref[...] * (_dot3(feat, w1_ref[...]) + b1_ref[...]))
    h = jnp.sin(f2_ref[...] * (_dot3(h, w2_ref[...]) + b2_ref[...]))
    h = _dot3(h, w3_ref[...])
    odd = lax.broadcasted_iota(jnp.int32, (tile, 2 * hw), 1) >= hw
    tcol = jnp.where(odd, feat[:, embp:embp + 1], feat[:, 0:1])
    dec = jnp.exp(-tcol * dl_ref[...])
    row = pl.program_id(0) * tile + lax.broadcasted_iota(jnp.int32, (tile, 2 * hw), 0)
    pos = 2 * row + odd.astype(jnp.int32)
    taps = jnp.where(pos == seq, 0.0, h * dec)
    taps_ref[...] = taps
    part = jnp.sum(jnp.abs(taps), axis=0, keepdims=True)

    @pl.when(pl.program_id(0) == 0)
    def _():
        asum_ref[...] = part

    @pl.when(pl.program_id(0) != 0)
    def _():
        asum_ref[...] = asum_ref[...] + part


def _hy_filter(seq, w1, b1, f1, w2, b2, f2, w3, hw):
    ar = jnp.arange(seq, dtype=jnp.int32)
    pos = jnp.concatenate([ar, (seq - ar) % seq]).astype(F32)[:, None]
    t = pos * (1.0 / (seq - 1))
    w = (2.0 * math.pi / seq) * pos
    bands = jnp.linspace(1e-4, HY_BANDS - 1, HY_BANDS, dtype=F32)[None, :]
    emb = 1 + 2 * HY_BANDS
    embp = ((emb + 7) // 8) * 8
    feat = jnp.concatenate([t, jnp.cos(bands * w), -jnp.sin(bands * w), jnp.zeros((2 * seq, embp - emb), F32)], axis=-1)
    w1p = jnp.concatenate([w1, jnp.zeros((embp - emb, w1.shape[1]), F32)], axis=0)
    max_decay = math.log(1.0 / HY_TARGET) / HY_FAST
    min_decay = math.log(1.0 / HY_TARGET) / HY_SLOW
    deltas = jnp.linspace(min_decay, max_decay, hw, dtype=F32)[None, :]
    feat = feat.reshape(seq, 2 * embp)
    hid = w1.shape[1]

    def pair(m):
        z = jnp.zeros_like(m)
        return jnp.concatenate([jnp.concatenate([m, z], axis=1), jnp.concatenate([z, m], axis=1)], axis=0)

    twice = lambda v: jnp.concatenate([v, v])[None, :]
    w3p = jnp.concatenate([pair(w3[:, :hw]), pair(w3[:, hw:])], axis=1)
    tile = min(ROW_TILE, seq) // 2
    nfwd = seq // (2 * tile)
    full = lambda shape: pl.BlockSpec(shape, lambda i: (0,) * len(shape))
    taps, asum = pl.pallas_call(
        functools.partial(_filter_kernel, tile=tile, hw=hw, seq=seq, embp=embp),
        grid=(2 * nfwd,),
        in_specs=[
            pl.BlockSpec((tile, 2 * embp), lambda i: (i, 0)),
            full((2 * embp, 2 * hid)), full((1, 2 * hid)), full((1, 2 * hid)),
            full((2 * hid, 2 * hid)), full((1, 2 * hid)), full((1, 2 * hid)),
            pl.BlockSpec((2 * hid, 2 * hw), lambda i: (0, i // nfwd)), full((1, 2 * hw)),
        ],
        out_specs=[
            pl.BlockSpec((tile, 2 * hw), lambda i: (i, 0)),
            pl.BlockSpec((1, 2 * hw), lambda i: (0, 0)),
        ],
        out_shape=[
            jax.ShapeDtypeStruct((seq, 2 * hw), F32),
            jax.ShapeDtypeStruct((1, 2 * hw), F32),
        ],
        compiler_params=_cparams(("arbitrary",)),
    )(feat, pair(w1p), twice(b1), twice(f1), pair(w2), twice(b2), twice(f2), w3p,
      jnp.concatenate([deltas, deltas], axis=1))
    return taps.reshape(2 * seq, hw), asum[:, :hw] + asum[:, hw:]


def _np_split(a):
    a32 = jnp.asarray(a, F32)
    hi = a32.astype(BF16)
    lo = (a32 - hi.astype(F32)).astype(BF16)
    return hi, lo


def _dft_consts(n2):
    n1 = FFT_N1
    n = n1 * n2
    half = n1 // 2
    a1 = -2.0 * np.pi * np.outer(np.arange(n1), np.arange(n1)) / n1
    f1r, f1i = np.cos(a1), np.sin(a1)
    f1_data = np.block([[f1r[:, :half], -f1i[:, :half]], [f1i[:, :half], f1r[:, :half]]])
    f1_real = np.concatenate([f1r, f1i], axis=0)
    g1r, g1i = f1r / n, -f1i / n
    g1 = np.block([[g1r[:half], -g1i[:half]], [g1i[:half], g1r[:half]]])
    a2 = -2.0 * np.pi * np.outer(np.arange(n2), np.arange(n2)) / n2
    f2r, f2i = np.cos(a2), np.sin(a2)
    f2 = np.block([[f2r, -f2i], [f2i, f2r]])
    g2 = np.block([[f2r, f2i], [-f2i, f2r]])
    at = -2.0 * np.pi * np.outer(np.arange(n1), np.arange(n2)) / n
    twr = jnp.broadcast_to(jnp.asarray(np.cos(at), F32)[:, :, None], (n1, n2, 128))
    twi = jnp.broadcast_to(jnp.asarray(np.sin(at), F32)[:, :, None], (n1, n2, 128))
    return dict(f1_data=_np_split(f1_data), f1_real=_np_split(f1_real), g1=_np_split(g1),
                f2=_np_split(f2), g2=_np_split(g2), twr=twr, twi=twi)


def _fft1_kernel(z_ref, fh_ref, fl_ref, ar_ref, ai_ref):
    a = _dot3c(fh_ref[...], fl_ref[...], z_ref[...])
    ar_ref[...] = a[:FFT_N1]
    ai_ref[...] = a[FFT_N1:]


def _fft_first(zview, fmat, ncols):
    cb = min(FFT_COLS, ncols)
    fh, fl = fmat
    cspec = pl.BlockSpec(fh.shape, lambda j: (0, 0))
    ospec = pl.BlockSpec((FFT_N1, cb), lambda j: (0, j))
    return pl.pallas_call(
        _fft1_kernel,
        grid=(ncols // cb,),
        in_specs=[pl.BlockSpec((FFT_N1, cb), lambda j: (0, j)), cspec, cspec],
        out_specs=[ospec, ospec],
        out_shape=[jax.ShapeDtypeStruct((FFT_N1, ncols), F32)] * 2,
        compiler_params=_cparams(("arbitrary",)),
    )(zview, fh, fl)


def _fftmid_kernel(ar_ref, ai_ref, twr_ref, twi_ref, f2h_ref, f2l_ref, *rest, filter_only, n2):
    kb, _, width = ar_ref.shape
    reps = width // 128
    lanes = lambda parts: jnp.concatenate(parts, axis=1)
    twr = lanes([t for j in range(kb) for t in [twr_ref[j]] * reps])
    twi = lanes([t for j in range(kb) for t in [twi_ref[j]] * reps])
    ar = lanes([ar_ref[j] for j in range(kb)])
    ai = lanes([ai_ref[j] for j in range(kb)])
    z = jnp.concatenate([ar * twr - ai * twi, ar * twi + ai * twr], axis=0)
    x = _dot3c(f2h_ref[...], f2l_ref[...], z)
    xr, xi = x[:n2], x[n2:]
    if filter_only:
        kr_out, ki_out = rest
        for j in range(kb):
            kr_out[j] = xr[:, j * width:(j + 1) * width]
            ki_out[j] = xi[:, j * width:(j + 1) * width]
        return
    kr_ref, ki_ref, g2h_ref, g2l_ref, br_out, bi_out = rest
    kr = lanes([kr_ref[j] for j in range(kb)])
    ki = lanes([ki_ref[j] for j in range(kb)])
    y = jnp.concatenate([xr * kr - xi * ki, xr * ki + xi * kr], axis=0)
    w = _dot3c(g2h_ref[...], g2l_ref[...], y)
    wr, wi = w[:n2], w[n2:]
    br = wr * twr + wi * twi
    bi = wi * twr - wr * twi
    for j in range(kb):
        br_out[j] = br[:, j * width:(j + 1) * width]
        bi_out[j] = bi[:, j * width:(j + 1) * width]


def _fft_mid(ar, ai, consts, n2, width, spectrum=None):
    a3r = ar.reshape(FFT_N1, n2, width)
    a3i = ai.reshape(FFT_N1, n2, width)
    blk = pl.BlockSpec((FFT_KB, n2, width), lambda k: (k, 0, 0))
    twspec = pl.BlockSpec((FFT_KB, n2, 128), lambda k: (k, 0, 0))
    cspec = pl.BlockSpec((2 * n2, 2 * n2), lambda k: (0, 0))
    in_specs = [blk, blk, twspec, twspec, cspec, cspec]
    args = [a3r, a3i, consts["twr"], consts["twi"], *consts["f2"]]
    if spectrum is not None:
        in_specs += [blk, blk, cspec, cspec]
        args += [spectrum[0], spectrum[1], *consts["g2"]]
    return pl.pallas_call(
        functools.partial(_fftmid_kernel, filter_only=spectrum is None, n2=n2),
        grid=(FFT_N1 // FFT_KB,),
        in_specs=in_specs,
        out_specs=[blk, blk],
        out_shape=[jax.ShapeDtypeStruct((FFT_N1, n2, width), F32)] * 2,
        compiler_params=_cparams(("arbitrary",)),
    )(*args)


def _fftlast_kernel(br_ref, bi_ref, gh_ref, gl_ref, u_ref, x0_ref, bias_ref, invn_ref, o_ref):
    b = jnp.concatenate([br_ref[...], bi_ref[...]], axis=0)
    y = _dot3c(gh_ref[...], gl_ref[...], b)
    o_ref[...] = ((y * invn_ref[...] + u_ref[...] * bias_ref[...]) * x0_ref[...]).astype(BF16)


def _fft_last(br, bi, gmat, uview, x0view, bias_t, invn_t, ncols):
    cb = min(FFT_COLS, ncols)
    gh, gl = gmat
    cspec = pl.BlockSpec(gh.shape, lambda j: (0, 0))
    dspec = pl.BlockSpec((FFT_N1, cb), lambda j: (0, j))
    vspec = pl.BlockSpec((1, cb), lambda j: (0, 0))
    return pl.pallas_call(
        _fftlast_kernel,
        grid=(ncols // cb,),
        in_specs=[dspec, dspec, cspec, cspec, dspec, dspec, vspec, vspec],
        out_specs=dspec,
        out_shape=jax.ShapeDtypeStruct((FFT_N1, ncols), BF16),
        compiler_params=_cparams(("arbitrary",)),
    )(br, bi, gh, gl, uview, x0view, bias_t, invn_t)


def _dft_small_consts(seq):
    n = 2 * seq
    a = -2.0 * np.pi * np.outer(np.arange(n), np.arange(n)) / n
    fr, fi = np.cos(a), np.sin(a)
    f_data = np.block([[fr[:, :seq], -fi[:, :seq]], [fi[:, :seq], fr[:, :seq]]])
    f_real = np.concatenate([fr, fi], axis=0)
    gr, gi = fr / n, -fi / n
    g = np.block([[gr[:seq], -gi[:seq]], [gi[:seq], gr[:seq]]])
    return _np_split(f_data), _np_split(f_real), _np_split(g)


def _hy_ctx_kernel(u_ref, x0_ref, taps_ref, fdh_ref, fdl_ref, frh_ref, frl_ref, gh_ref, gl_ref,
                   bias_ref, invn_ref, o_ref):
    n = taps_ref.shape[0]
    z = u_ref[...]
    a = _dot3c(fdh_ref[...], fdl_ref[...], z)
    k = _dot3c(frh_ref[...], frl_ref[...], taps_ref[...])
    ar, ai, kr, ki = a[:n], a[n:], k[:n], k[n:]
    y = jnp.concatenate([ar * kr - ai * ki, ar * ki + ai * kr], axis=0)
    conv = _dot3c(gh_ref[...], gl_ref[...], y)
    o_ref[...] = ((conv * invn_ref[...] + z * bias_ref[...]) * x0_ref[...]).astype(BF16)


def _hy_ctx(u, x0, taps, bias, invn, small):
    rows, hw = u.shape
    full = lambda a: pl.BlockSpec(a.shape, lambda i: (0,) * a.ndim)
    args = [u, x0, taps, *small[0], *small[1], *small[2], bias, invn]
    return pl.pallas_call(
        _hy_ctx_kernel,
        grid=(1,),
        in_specs=[full(a) for a in args],
        out_specs=pl.BlockSpec((rows, hw), lambda i: (0, 0)),
        out_shape=jax.ShapeDtypeStruct((rows, hw), BF16),
        compiler_params=_cparams(("arbitrary",)),
    )(*args)


def _out_proj_kernel(x_ref, ypl_ref, yhl_ref, yal_ref, ypc_ref, yhc_ref, yac_ref, w_ref, g_ref, mod_ref, *rest,
                     nlt, tpb, ncond, pw, hw, moe, ntiles):
    if moe:
        rw_ref, xo_ref, u_ref, route_ref, un_ref = rest
    else:
        xo_ref, u_ref = rest
    t = jnp.minimum(pl.program_id(0), ntiles - 1)
    if moe:
        @pl.when(pl.program_id(0) == 0)
        def _():
            un_ref[...] = jnp.zeros_like(un_ref)

        logits = _dot3(un_ref[...], rw_ref[...])
    ci = _cond_row(t, nlt, tpb, ncond)
    is_ctx = t >= nlt
    yp = jnp.where(is_ctx, ypc_ref[...], ypl_ref[...])
    yh = jnp.where(is_ctx, yhc_ref[...], yhl_ref[...])
    ya = jnp.where(is_ctx, yac_ref[...], yal_ref[...])
    mix = _dot(yp, w_ref[0:pw, :]) + _dot(yh, w_ref[pw:pw + hw, :]) + _dot(ya, w_ref[pw + hw:, :])
    x = x_ref[...] + mod_ref[2, pl.ds(ci, 1), :] * mix
    xo_ref[...] = x
    un = _normmod(x, g_ref[...], mod_ref[3, pl.ds(ci, 1), :], mod_ref[4, pl.ds(ci, 1), :])
    u_ref[...] = un.astype(BF16)
    if moe:
        un_ref[...] = un
        lane = lax.broadcasted_iota(jnp.int32, logits.shape, 1)
        neg = jnp.float32(-jnp.inf)
        lg = jnp.where(lane < N_EXPERTS, logits, neg)
        t1 = jnp.max(lg, axis=-1, keepdims=True)
        i1 = jnp.min(jnp.where(lg == t1, lane, 128), axis=-1, keepdims=True)
        lg2 = jnp.where(lane == i1, neg, lg)
        t2 = jnp.max(lg2, axis=-1, keepdims=True)
        i2 = jnp.min(jnp.where(lg2 == t2, lane, 128), axis=-1, keepdims=True)
        e2 = jnp.exp(t2 - t1)
        g1 = 1.0 / (1.0 + e2)
        g2 = e2 / (1.0 + e2)
        route_ref[...] = jnp.where(lane == 0, i1.astype(F32), jnp.where(lane == 1, i2.astype(F32),
                                   jnp.where(lane == 2, g1, jnp.where(lane == 3, g2, 0.0))))


def _out_proj(x, lat, ctx, w, g, mod, rw, *, nlt, tpb, ncond):
    n, d = x.shape
    pw, hw, aw = (a.shape[1] for a in lat)
    tm = ROW_TILE
    moe = rw is not None
    nt = n // tm
    cur = lambda t: jnp.minimum(t, nt - 1)
    row = lambda width: pl.BlockSpec((tm, width), lambda t: (cur(t), 0))
    lrow = lambda width: pl.BlockSpec((tm, width), lambda t: (jnp.minimum(t, nlt - 1), 0))
    crow = lambda width: pl.BlockSpec((tm, width), lambda t: (jnp.maximum(cur(t) - nlt, 0), 0))
    in_specs = [row(d), lrow(pw), lrow(hw), lrow(aw), crow(pw), crow(hw), crow(aw),
                pl.BlockSpec(w.shape, lambda t: (0, 0)),
                pl.BlockSpec((1, d), lambda t: (0, 0)),
                pl.BlockSpec(mod.shape, lambda t: (0, 0, 0))]
    args = [x, *lat, *ctx, w, g, mod]
    out_specs = [row(d), row(d)]
    out_shape = [jax.ShapeDtypeStruct((n, d), F32), jax.ShapeDtypeStruct((n, d), BF16)]
    scratch = []
    if moe:
        in_specs.append(pl.BlockSpec(rw.shape, lambda t: (0, 0)))
        args.append(rw)
        out_specs.append(pl.BlockSpec((tm, 128), lambda t: (jnp.maximum(t - 1, 0), 0)))
        out_shape.append(jax.ShapeDtypeStruct((n, 128), F32))
        scratch.append(pltpu.VMEM((tm, d), F32))
    kern = functools.partial(_out_proj_kernel, nlt=nlt, tpb=tpb, ncond=ncond, pw=pw, hw=hw, moe=moe, ntiles=nt)
    return pl.pallas_call(
        kern, grid=(nt + 1 if moe else nt,), in_specs=in_specs, out_specs=out_specs, out_shape=out_shape,
        scratch_shapes=scratch,
        compiler_params=_cparams(("arbitrary",)),
    )(*args)


def _ffn_kernel(x_ref, u_ref, mod_ref, wg_ref, wu_ref, wd_ref, o_ref, *, nlt, tpb, ncond, nchunk):
    ci = _cond_row(pl.program_id(0), nlt, tpb, ncond)
    u = u_ref[...]
    ff = wg_ref.shape[1]
    fc = ff // nchunk
    y = jnp.zeros(x_ref.shape, F32)
    for c in range(nchunk):
        gate = _dot(u, wg_ref[:, c * fc:(c + 1) * fc])
        up = _dot(u, wu_ref[:, c * fc:(c + 1) * fc])
        y = y + _dot((_silu(gate) * up).astype(BF16), wd_ref[c * fc:(c + 1) * fc, :])
    o_ref[...] = x_ref[...] + mod_ref[5, pl.ds(ci, 1), :] * y


def _ffn(x, u, mod, wg, wu, wd, *, nlt, tpb, ncond, out_rows):
    n, d = x.shape
    ff = wg.shape[1]
    tm = ROW_TILE
    nchunk = 2 if (ff // 2) % 128 == 0 else 1
    row = lambda dt: pl.BlockSpec((tm, d), lambda t: (t, 0))
    const = lambda shape: pl.BlockSpec(shape, lambda t: (0,) * len(shape), pipeline_mode=pl.Buffered(1))
    kern = functools.partial(_ffn_kernel, nlt=nlt, tpb=tpb, ncond=ncond, nchunk=nchunk)
    return pl.pallas_call(
        kern,
        grid=(out_rows // tm,),
        in_specs=[row(F32), row(BF16), pl.BlockSpec(mod.shape, lambda t: (0, 0, 0)),
                  const((d, ff)), const((d, ff)), const((ff, d))],
        out_specs=row(F32),
        out_shape=jax.ShapeDtypeStruct((out_rows, d), F32),
        compiler_params=_cparams(("arbitrary",)),
    )(x, u, mod, wg, wu, wd)


def _moe_kernel(te_ref, tv_ref, u_ref, wg_ref, wu_ref, wd_ref, *rest):
    o_ref, acc_ref = rest[-2:]
    t = pl.program_id(0)
    f = pl.program_id(1)

    @pl.when(tv_ref[t] > 0)
    def _():
        u = u_ref[...]
        h = (_silu(_dot(u, wg_ref[0, 0])) * _dot(u, wu_ref[0, 0])).astype(BF16)
        y = _dot(h, wd_ref[0, 0])

        @pl.when(f == 0)
        def _():
            acc_ref[...] = y

        @pl.when(f != 0)
        def _():
            acc_ref[...] = acc_ref[...] + y

        @pl.when(f == pl.num_programs(1) - 1)
        def _():
            o_ref[...] = acc_ref[...].astype(BF16)


def _moe_experts(ug, tile_expert, tile_valid, wg, wu, wd, j, *, tile0, total_rows, prev=None):
    p, d = ug.shape
    ff = wg.shape[3]
    tm, fc = MOE_TM, MOE_FC
    in_specs = [
        pl.BlockSpec((tm, d), lambda t, f, te, tv: (t, 0)),
        pl.BlockSpec((1, 1, d, fc), lambda t, f, te, tv: (j, te[t], 0, f)),
        pl.BlockSpec((1, 1, d, fc), lambda t, f, te, tv: (j, te[t], 0, f)),
        pl.BlockSpec((1, 1, fc, d), lambda t, f, te, tv: (j, te[t], f, 0)),
    ]
    args = [tile_expert, tile_valid, ug, wg, wu, wd]
    aliases = {}
    if prev is not None:
        in_specs.append(pl.BlockSpec(memory_space=pl.ANY))
        args.append(prev)
        aliases = {len(args) - 1: 0}
    grid_spec = pltpu.PrefetchScalarGridSpec(
        num_scalar_prefetch=2,
        grid=(p // tm, ff // fc),
        in_specs=in_specs,
        out_specs=pl.BlockSpec((tm, d), lambda t, f, te, tv: (t + tile0, 0)),
        scratch_shapes=[pltpu.VMEM((tm, d), F32)],
    )
    return pl.pallas_call(
        _moe_kernel, grid_spec=grid_spec,
        out_shape=jax.ShapeDtypeStruct((total_rows, d), BF16),
        input_output_aliases=aliases,
        compiler_params=_cparams(("arbitrary", "arbitrary")),
    )(*args)


def _moe_combine_kernel(x_ref, ya_ref, yb_ref, route_ref, mod_ref, o_ref, *, nlt, tpb, ncond):
    ci = _cond_row(pl.program_id(0), nlt, tpb, ncond)
    r = route_ref[...]
    lane = lax.broadcasted_iota(jnp.int32, r.shape, 1)
    g1 = jnp.sum(jnp.where(lane == 2, r, 0.0), axis=-1, keepdims=True)
    g2 = jnp.sum(jnp.where(lane == 3, r, 0.0), axis=-1, keepdims=True)
    y = g1 * ya_ref[...].astype(F32) + g2 * yb_ref[...].astype(F32)
    o_ref[...] = x_ref[...] + mod_ref[5, pl.ds(ci, 1), :] * y


def _moe_combine(x, ya, yb, route, mod, *, nlt, tpb, ncond, out_rows):
    n, d = x.shape
    tm = ROW_TILE
    row = lambda width: pl.BlockSpec((tm, width), lambda t: (t, 0))
    kern = functools.partial(_moe_combine_kernel, nlt=nlt, tpb=tpb, ncond=ncond)
    return pl.pallas_call(
        kern,
        grid=(out_rows // tm,),
        in_specs=[row(d), row(d), row(d), row(128), pl.BlockSpec(mod.shape, lambda t: (0, 0, 0))],
        out_specs=row(d),
        out_shape=jax.ShapeDtypeStruct((out_rows, d), F32),
        compiler_params=_cparams(("arbitrary",)),
    )(x, ya, yb, route, mod)


def _moe_layer(x, u, route, mod, wg, wu, wd, j, *, nlt, tpb, ncond, out_rows):
    n, d = x.shape
    tm = MOE_TM
    experts = jnp.concatenate([route[:, 0], route[:, 1]]).astype(jnp.int32)
    onehot = (experts[:, None] == jnp.arange(N_EXPERTS, dtype=jnp.int32)[None, :]).astype(jnp.int32)
    rank = jnp.sum(onehot * (jnp.cumsum(onehot, axis=0) - 1), axis=1)
    counts = jnp.sum(onehot, axis=0)
    padded = ((counts + tm - 1) // tm) * tm
    ends = jnp.cumsum(padded)
    starts = ends - padded
    dest = starts[experts] + rank
    p = 2 * n + N_EXPERTS * tm
    tokens = jnp.concatenate([jnp.arange(n, dtype=jnp.int32)] * 2)
    inb = dict(mode="promise_in_bounds")
    src = jnp.zeros((p,), jnp.int32).at[dest].set(tokens, unique_indices=True, **inb)
    tile_start = jnp.arange(p // tm, dtype=jnp.int32) * tm
    tile_expert = jnp.minimum(jnp.searchsorted(ends, tile_start, side="right"), N_EXPERTS - 1).astype(jnp.int32)
    tile_valid = (tile_start < ends[-1]).astype(jnp.int32)
    nt0 = (p // tm) // 2
    ys = None
    for lo, hi in ((0, nt0), (nt0, p // tm)):
        ug = u.at[src[lo * tm:hi * tm]].get(**inb)
        ys = _moe_experts(ug, tile_expert[lo:hi], tile_valid[lo:hi], wg, wu, wd, j,
                          tile0=lo, total_rows=p, prev=ys)
    ya = ys.at[dest[:n]].get(**inb)
    yb = ys.at[dest[n:]].get(**inb)
    return _moe_combine(x, ya, yb, route, mod, nlt=nlt, tpb=tpb, ncond=ncond, out_rows=out_rows)


def _rope_tables(seq, n_ctx_rows, batch):
    rows = seq // GRID_W
    row = jnp.repeat(jnp.arange(rows, dtype=F32), GRID_W)
    col = jnp.broadcast_to(jnp.arange(GRID_W, dtype=F32), (rows, GRID_W)).reshape(-1)
    inv_freq = jnp.power(ROPE_THETA, -jnp.arange(ROPE_FREQS, dtype=F32) / ROPE_FREQS)
    ar = row[:, None] * inv_freq
    ac = col[:, None] * inv_freq
    cos = jnp.concatenate([jnp.cos(ar), jnp.cos(ar), jnp.cos(ac), jnp.cos(ac)], axis=-1)
    sin = jnp.concatenate([-jnp.sin(ar), jnp.sin(ar), -jnp.sin(ac), jnp.sin(ac)], axis=-1)
    cos = jnp.tile(jnp.concatenate([cos, cos], axis=-1), (batch, 1))
    sin = jnp.tile(jnp.concatenate([sin, sin], axis=-1), (batch, 1))
    cos = jnp.concatenate([cos, jnp.ones((n_ctx_rows, 2 * HEAD_DIM), F32)], axis=0)
    sin = jnp.concatenate([sin, jnp.zeros((n_ctx_rows, 2 * HEAD_DIM), F32)], axis=0)
    return cos, sin


def kernel(x, c, ctx, c_ctx, mod_w, mod_b, norm1_g, norm2_g, w_in, w_out, pool_lin, pool_scale, hy_short_w, hy_short_b, hy_f_w1, hy_f_b1, hy_f_freq1, hy_f_w2, hy_f_b2, hy_f_freq2, hy_f_w3, hy_bias, qk_norm_g, diff_lambda, subln_g, ffn_w_gate, ffn_w_up, ffn_w_down, router_w, moe_w_gate, moe_w_up, moe_w_down):
    batch, seq, d = x.shape
    ctx_len = ctx.shape[1]
    depth = mod_w.shape[0]
    pw = pool_scale.shape[1]
    hw = hy_bias.shape[1]
    nl, nc = batch * seq, batch * ctx_len
    n = nl + nc
    tm = ROW_TILE
    assert seq % tm == 0 and nc % tm == 0 and seq % (FFT_N1 // 2) == 0 and seq % GRID_W == 0
    assert d == HEADS * 2 * V_DIM and pw == hw and batch == 2
    n2 = 2 * seq // FFT_N1
    ncols = n2 * hw
    assert (n * hw) % ncols == 0
    tiles = dict(nlt=nl // tm, tpb=seq // tm, ncond=batch + 1)

    xs = jnp.concatenate([x.reshape(nl, d), ctx.reshape(nc, d)], axis=0)
    cond8 = jnp.concatenate([c, c_ctx[None, :], jnp.zeros((8 - batch - 1, d), F32)], axis=0)
    mods = _modvec(cond8, mod_w, mod_b)
    cos_t, sin_t = _rope_tables(seq, nc, batch)
    consts = _dft_consts(n2)
    small = _dft_small_consts(ctx_len)
    eye = jnp.eye(len(POOL_WINDOWS), dtype=F32)
    moe_wg, moe_wu, moe_wd = _cast_bf16(moe_w_gate, moe_w_up, moe_w_down)

    for i in range(depth):
        last = i == depth - 1
        lam_init = 0.8 - 0.6 * math.exp(-0.3 * i)
        mod = mods[i]
        gqk = jnp.concatenate([qk_norm_g[i], qk_norm_g[i]], axis=-1)
        zp, zh, qt, k12, vt = _in_proj(xs, norm1_g[i][None, :], mod, w_in[i].astype(BF16), cos_t, sin_t, gqk,
                                       pw=pw, hw=hw, **tiles)

        g_b = jnp.broadcast_to(subln_g[i][:, None], (V_DIM, ATT_TQ))
        att_kw = dict(batch=batch, seq=seq, ctx_len=ctx_len, lam_init=lam_init)

        def attend(direct, lam_p=diff_lambda[i], qt=qt, k12=k12, vt=vt, g_b=g_b, att_kw=att_kw):
            return (_attention(lam_p, qt, k12, vt, g_b, latent=True, direct=direct, **att_kw),
                    _attention(lam_p, qt, k12, vt, g_b, latent=False, direct=direct, **att_kw))

        bound = (HEAD_DIM * QSCALE) * jnp.max(jnp.abs(qk_norm_g[i][0])) * jnp.max(jnp.abs(qk_norm_g[i][1]))
        ya_l, ya_c = lax.cond(bound * 1.02 < ATT_DIRECT_MAX, lambda: attend(True), lambda: attend(False))

        lin_bd = (eye[:, None, :, None] * pool_lin[i][:, :, None, :]).reshape(pw, pw).astype(BF16)
        pscale = pool_scale[i][None, :]
        yp_l = _pool(zp, lin_bd, pscale, row0=0, batch=batch, seq=seq, out_rows=nl, out_row0=0)
        yp_c = _pool(zp, lin_bd, pscale, row0=nl, batch=batch, seq=ctx_len, out_rows=nc, out_row0=0)

        sw, sb = hy_short_w[i], hy_short_b[i][None, :]
        u_l, x0_l = _hy_pre(zh, sw, sb, row0=0, batch=batch, seq=seq, out_rows=n, out_row0=0)
        u_c, x0_c = _hy_pre(zh, sw, sb, row0=nl, batch=batch, seq=ctx_len, out_rows=nc, out_row0=0)
        filt = (hy_f_w1[i], hy_f_b1[i], hy_f_freq1[i], hy_f_w2[i], hy_f_b2[i], hy_f_freq2[i], hy_f_w3[i])
        bias = hy_bias[i][None, :]
        taps, asum = _hy_filter(seq, *filt, hw)
        kr1, ki1 = _fft_first(taps.reshape(FFT_N1, ncols), consts["f1_real"], ncols)
        spectrum = _fft_mid(kr1, ki1, consts, n2, hw)
        ar, ai = _fft_first(u_l.reshape(-1, ncols), consts["f1_data"], ncols)
        br, bi = _fft_mid(ar, ai, consts, n2, hw, spectrum=spectrum)
        reps = min(FFT_COLS, ncols) // hw
        yh_lat = _fft_last(br.reshape(FFT_N1, ncols), bi.reshape(FFT_N1, ncols), consts["g1"],
                           u_l.reshape(-1, ncols), x0_l.reshape(-1, ncols),
                           jnp.tile(bias, (1, reps)), jnp.tile(1.0 / asum, (1, reps)), ncols)
        taps_c, asum_c = _hy_filter(ctx_len, *filt, hw)
        yh_ctx = _hy_ctx(u_c, x0_c, taps_c, bias, 1.0 / asum_c, small)

        j = i // 2
        moe = i % 2 == 1
        rw = None
        if moe:
            rw = jnp.concatenate([router_w[j], jnp.zeros((d, 128 - N_EXPERTS), F32)], axis=1)
        outs = _out_proj(xs, (yp_l, yh_lat.reshape(nl, hw), ya_l), (yp_c, yh_ctx, ya_c), w_out[i].astype(BF16),
                         norm2_g[i][None, :], mod, rw, **tiles)
        out_rows = nl if last else n
        if moe:
            xs, u, route = outs
            xs = _moe_layer(xs, u, route, mod, moe_wg, moe_wu, moe_wd, j, out_rows=out_rows, **tiles)
        else:
            xs, u = outs
            xs = _ffn(xs, u, mod, ffn_w_gate[j].astype(BF16), ffn_w_up[j].astype(BF16),
                      ffn_w_down[j].astype(BF16), out_rows=out_rows, **tiles)
    return xs[:nl].reshape(batch, seq, d)
```

```python
import functools
import math

import numpy as np
import jax
import jax.numpy as jnp
from jax import lax
from jax.experimental import pallas as pl
from jax.experimental.pallas import tpu as pltpu

F32 = jnp.float32
BF16 = jnp.bfloat16
EPS = 1e-6

GRID_W = 64
POOL_WINDOWS = (2, 4, 8, 16)
HEADS = 4
HEAD_DIM = 64
V_DIM = 128
ROPE_FREQS = 16
ROPE_THETA = 10000.0
HY_BANDS = 16
HY_TARGET = 1e-2
HY_FAST = 0.3
HY_SLOW = 1.5
N_EXPERTS = 8

LANES = 128
ROW_TILE = 512
HALO = 32
FFT_N1 = 128
FFT_COLS = 2048
FFT_KB = 4
ATT_TQ = 1024
ATT_TK = 512
ATT_UNROLL = 16
ATT_GROUP = 8
QSCALE = (HEAD_DIM ** -0.5) * math.log2(math.e)
ATT_DIRECT_MAX = 100.0
CAST_STEPS = 64
MOE_TM = 512
MOE_FC = 1792
VMEM_LIMIT = 56 * 1024 * 1024


def _cparams(sem):
    return pltpu.CompilerParams(dimension_semantics=sem, vmem_limit_bytes=VMEM_LIMIT)


def _dot(a, b):
    return jnp.dot(a, b, preferred_element_type=F32)


def _split(a):
    hi = a.astype(BF16)
    lo = (a - hi.astype(F32)).astype(BF16)
    return hi, lo


def _dot3(a, b):
    ah, al = _split(a)
    bh, bl = _split(b)
    return _dot(ah, bh) + _dot(ah, bl) + _dot(al, bh)


def _dot3c(ch, cl, d):
    dh, dl = _split(d)
    return _dot(ch, dh) + _dot(ch, dl) + _dot(cl, dh)


def _silu(x):
    return x / (1.0 + jnp.exp(-x))


def _normmod(x, g, shift, scale):
    ms = jnp.mean(x * x, axis=-1, keepdims=True)
    return (x * lax.rsqrt(ms + EPS) * g) * (1.0 + scale) + shift


def _cond_row(t, n_lat_tiles, tiles_per_batch, n_cond):
    return jnp.where(t >= n_lat_tiles, n_cond - 1, t // tiles_per_batch)


def _cast_kernel(*refs):
    half = len(refs) // 2
    for w_ref, o_ref in zip(refs[:half], refs[half:]):
        o_ref[...] = w_ref[...].astype(BF16)


def _cast_bf16(*ws):
    flat = [w.reshape(-1, w.shape[-1]) for w in ws]
    steps = CAST_STEPS
    specs = [pl.BlockSpec((w.shape[0] // steps, w.shape[1]), lambda t: (t, 0)) for w in flat]
    outs = pl.pallas_call(
        _cast_kernel,
        grid=(steps,),
        in_specs=specs,
        out_specs=specs,
        out_shape=[jax.ShapeDtypeStruct(w.shape, BF16) for w in flat],
        compiler_params=_cparams(("arbitrary",)),
    )(*flat)
    return [o.reshape(w.shape) for o, w in zip(outs, ws)]


def _modvec_kernel(c_ref, w_ref, b_ref, o_ref):
    s = _silu(c_ref[...])
    o_ref[0, 0] = _dot3(s, w_ref[0]) + b_ref[0]


def _modvec(cond8, mod_w, mod_b):
    depth, d, six_d = mod_w.shape
    nchunk = six_d // d
    return pl.pallas_call(
        _modvec_kernel,
        grid=(depth, nchunk),
        in_specs=[
            pl.BlockSpec((8, d), lambda i, j: (0, 0)),
            pl.BlockSpec((1, d, d), lambda i, j: (i, 0, j)),
            pl.BlockSpec((1, 1, d), lambda i, j: (i, 0, j)),
        ],
        out_specs=pl.BlockSpec((1, 1, 8, d), lambda i, j: (i, j, 0, 0)),
        out_shape=jax.ShapeDtypeStruct((depth, nchunk, 8, d), F32),
        compiler_params=_cparams(("arbitrary", "arbitrary")),
    )(cond8, mod_w, mod_b.reshape(depth, 1, six_d))


def _in_proj_kernel(x_ref, g_ref, mod_ref, w_ref, cos_ref, sin_ref, gqk_ref,
                    zp_ref, zh_ref, qt_ref, k_ref, vt_ref, zatt_ref, *, nlt, tpb, ncond, pw, hw, ntiles):
    t = pl.program_id(0)

    @pl.when(t == 0)
    def _():
        zatt_ref[...] = jnp.zeros_like(zatt_ref)

    tm = x_ref.shape[0]
    att = pw + 3 * hw
    lane = lax.broadcasted_iota(jnp.int32, (tm, 2 * HEAD_DIM), 1)
    first = lane < HEAD_DIM
    apart = (lane % (2 * ROPE_FREQS)) < ROPE_FREQS
    cos = cos_ref[...]
    sin = sin_ref[...]
    qk_w = HEADS * 2 * HEAD_DIM

    def norm_rope(v, g):
        v2 = v * v
        s_all = jnp.sum(v2, axis=-1, keepdims=True)
        s_first = jnp.sum(jnp.where(first, v2, 0.0), axis=-1, keepdims=True)
        ms = jnp.where(first, s_first, s_all - s_first) * (1.0 / HEAD_DIM)
        vn = v * lax.rsqrt(ms + EPS) * g
        swapped = jnp.where(apart, pltpu.roll(vn, 2 * HEAD_DIM - ROPE_FREQS, 1), pltpu.roll(vn, ROPE_FREQS, 1))
        return vn * cos + swapped * sin

    for h in range(HEADS):
        lo = h * 2 * HEAD_DIM
        q = norm_rope(zatt_ref[:, lo:lo + 2 * HEAD_DIM], gqk_ref[0:1, :])
        qt_ref[h] = (q * QSCALE).T.astype(BF16)
        k = norm_rope(zatt_ref[:, qk_w + lo:qk_w + lo + 2 * HEAD_DIM], gqk_ref[1:2, :])
        k_ref[h] = k.astype(BF16)
        vlo = 2 * qk_w + h * V_DIM
        vt_ref[h] = zatt_ref[:, vlo:vlo + V_DIM].T.astype(BF16)

    ci = _cond_row(jnp.minimum(t, ntiles - 1), nlt, tpb, ncond)
    shift = mod_ref[0, pl.ds(ci, 1), :]
    scale = mod_ref[1, pl.ds(ci, 1), :]
    y = _normmod(x_ref[...], g_ref[...], shift, scale).astype(BF16)
    z = _dot(y, w_ref[...])
    zp_ref[...] = z[:, :pw].astype(BF16)
    zh_ref[...] = z[:, pw:att].astype(BF16)
    zatt_ref[...] = z[:, att:]


def _in_proj(x, g, mod, w, cos_t, sin_t, gqk, *, nlt, tpb, ncond, pw, hw):
    n, d = x.shape
    wid = w.shape[1]
    tm = ROW_TILE
    nt = n // tm
    kern = functools.partial(_in_proj_kernel, nlt=nlt, tpb=tpb, ncond=ncond, pw=pw, hw=hw, ntiles=nt)
    cur = lambda t: jnp.minimum(t, nt - 1)
    prv = lambda t: jnp.maximum(t - 1, 0)
    return pl.pallas_call(
        kern,
        grid=(nt + 1,),
        in_specs=[
            pl.BlockSpec((tm, d), lambda t: (cur(t), 0)),
            pl.BlockSpec((1, d), lambda t: (0, 0)),
            pl.BlockSpec(mod.shape, lambda t: (0, 0, 0)),
            pl.BlockSpec((d, wid), lambda t: (0, 0)),
            pl.BlockSpec((tm, 2 * HEAD_DIM), lambda t: (prv(t), 0)),
            pl.BlockSpec((tm, 2 * HEAD_DIM), lambda t: (prv(t), 0)),
            pl.BlockSpec((2, 2 * HEAD_DIM), lambda t: (0, 0)),
        ],
        out_specs=[
            pl.BlockSpec((tm, pw), lambda t: (cur(t), 0)),
            pl.BlockSpec((tm, 3 * hw), lambda t: (cur(t), 0)),
            pl.BlockSpec((HEADS, 2 * HEAD_DIM, tm), lambda t: (0, 0, prv(t))),
            pl.BlockSpec((HEADS, tm, 2 * HEAD_DIM), lambda t: (0, prv(t), 0)),
            pl.BlockSpec((HEADS, V_DIM, tm), lambda t: (0, 0, prv(t))),
        ],
        scratch_shapes=[pltpu.VMEM((tm, wid - pw - 3 * hw), F32)],
        out_shape=[
            jax.ShapeDtypeStruct((n, pw), BF16),
            jax.ShapeDtypeStruct((n, 3 * hw), BF16),
            jax.ShapeDtypeStruct((HEADS, 2 * HEAD_DIM, n), BF16),
            jax.ShapeDtypeStruct((HEADS, n, 2 * HEAD_DIM), BF16),
            jax.ShapeDtypeStruct((HEADS, V_DIM, n), BF16),
        ],
        compiler_params=_cparams(("arbitrary",)),
    )(x, g, mod, w, cos_t, sin_t, gqk)


def _attn_kernel(lam_ref, qt_ref, kc_ref, vtc_ref, *rest, tk, n_lat_chunks, lam_init):
    if n_lat_chunks:
        kl_ref, vtl_ref, g_ref, o_ref, m_ref, l_ref, acc_ref = rest
    else:
        g_ref, o_ref, m_ref, l_ref, acc_ref = rest
    qt = qt_ref[0]
    row = lax.broadcasted_iota(jnp.int32, qt.shape, 0)
    zero = jnp.zeros_like(qt)
    qmaps = (jnp.where(row < HEAD_DIM, qt, zero), jnp.where(row >= HEAD_DIM, qt, zero))

    def process(k_tile, vt_tile, first):
        for mi in range(2):
            s = _dot(k_tile, qmaps[mi])
            smax = jnp.max(s, axis=0, keepdims=True)
            if first:
                m_new = smax
            else:
                m_old = m_ref[mi]
                m_new = jnp.maximum(m_old, smax)
            p = jnp.exp2(s - m_new)
            psum = jnp.sum(p, axis=0, keepdims=True)
            pv = _dot(vt_tile, p.astype(BF16))
            if first:
                l_ref[mi] = psum
                acc_ref[mi] = pv
            else:
                alpha = jnp.exp2(m_old - m_new)
                l_ref[mi] = alpha * l_ref[mi] + psum
                acc_ref[mi] = alpha * acc_ref[mi] + pv
            m_ref[mi] = m_new

    process(kc_ref[0], vtc_ref[0], True)
    if n_lat_chunks:
        def body(j, carry):
            off = pl.multiple_of(j * tk, tk)
            process(kl_ref[0, pl.ds(off, tk), :], vtl_ref[0, :, pl.ds(off, tk)], False)
            return carry
        lax.fori_loop(0, n_lat_chunks, body, 0)

    lv = lam_ref[...]
    lam = (jnp.exp(jnp.sum(lv[0:1] * lv[1:2], axis=-1, keepdims=True))
           - jnp.exp(jnp.sum(lv[2:3] * lv[3:4], axis=-1, keepdims=True)) + lam_init)
    o = acc_ref[0] * (1.0 / l_ref[0]) - lam * (acc_ref[1] * (1.0 / l_ref[1]))
    ms = jnp.mean(o * o, axis=0, keepdims=True)
    y = o * lax.rsqrt(ms + EPS) * g_ref[...] * (1.0 - lam_init)
    o_ref[...] = y.T.astype(BF16)


def _attn_direct_kernel(lam_ref, qt_ref, kc_ref, vtc_ref, *rest, tk, n_lat_chunks, lam_init):
    if n_lat_chunks:
        kl_ref, vtl_ref, g_ref, o_ref, p_ref, l_ref, acc_ref = rest
    else:
        g_ref, o_ref, p_ref, l_ref, acc_ref = rest
    qt = qt_ref[0]
    tq = qt.shape[1]
    row = lax.broadcasted_iota(jnp.int32, qt.shape, 0)
    zero = jnp.zeros_like(qt)
    qmaps = (jnp.where(row < HEAD_DIM, qt, zero), jnp.where(row >= HEAD_DIM, qt, zero))
    grp = ATT_GROUP

    def keys(first_piece, count):
        if isinstance(first_piece, int):
            return pl.ds((first_piece - 1) * tk, count * tk)
        return pl.ds(pl.multiple_of((first_piece - 1) * tk, tk), count * tk)

    def stage_ab(k_tile, pset, r):
        for mi in range(2):
            p = jnp.exp2(_dot(k_tile, qmaps[mi]))
            l_ref[mi] = l_ref[mi] + jnp.sum(p.reshape(tk // 8, 8, tq), axis=0)
            p_ref[pset, mi, r * tk:(r + 1) * tk, :] = p.astype(BF16)

    def stage_c(vt_tile, pset, lo, count):
        for mi in range(2):
            acc_ref[mi] = acc_ref[mi] + _dot(vt_tile, p_ref[pset, mi, lo * tk:(lo + count) * tk, :])

    def group_ab(g, pset):
        k_tile = kl_ref[0, keys(grp * g, grp), :]
        for mi in range(2):
            p = jnp.exp2(_dot(k_tile, qmaps[mi]))
            l_ref[mi] = l_ref[mi] + jnp.sum(p.reshape(grp * tk // 8, 8, tq), axis=0)
            p_ref[pset, mi] = p.astype(BF16)

    def group_c(g, pset):
        stage_c(vtl_ref[0, :, keys(grp * g, grp)], pset, 0, grp)

    l_ref[...] = jnp.zeros_like(l_ref)
    acc_ref[...] = jnp.zeros_like(acc_ref)
    stage_ab(kc_ref[0], 0, 0)
    if n_lat_chunks == 0:
        stage_c(vtc_ref[0], 0, 0, 1)
    else:
        n_groups = n_lat_chunks // grp
        per_step = ATT_UNROLL // grp
        for r in range(1, grp):
            stage_ab(kl_ref[0, keys(r, 1), :], 0, r)
        group_ab(1, 1)
        stage_c(vtc_ref[0], 0, 0, 1)
        stage_c(vtl_ref[0, :, keys(1, grp - 1)], 0, 1, grp - 1)
        for g in range(2, per_step):
            group_ab(g, g % 2)
            group_c(g - 1, (g - 1) % 2)

        def body(jj, carry):
            g0 = per_step * (jj + 1)
            for q in range(per_step):
                group_ab(g0 + q, q % 2)
                group_c(g0 + q - 1, (q + 1) % 2)
            return carry

        lax.fori_loop(0, n_groups // per_step - 1, body, 0)
        stage_ab(kl_ref[0, keys(n_lat_chunks, 1), :], 0, 0)
        group_c(n_groups - 1, 1)
        stage_c(vtl_ref[0, :, keys(n_lat_chunks, 1)], 0, 0, 1)

    lv = lam_ref[...]
    lam = (jnp.exp(jnp.sum(lv[0:1] * lv[1:2], axis=-1, keepdims=True))
           - jnp.exp(jnp.sum(lv[2:3] * lv[3:4], axis=-1, keepdims=True)) + lam_init)
    l1 = jnp.sum(l_ref[0], axis=0, keepdims=True)
    l2 = jnp.sum(l_ref[1], axis=0, keepdims=True)
    o = acc_ref[0] * (1.0 / l1) - lam * (acc_ref[1] * (1.0 / l2))
    ms = jnp.mean(o * o, axis=0, keepdims=True)
    y = o * lax.rsqrt(ms + EPS) * g_ref[...] * (1.0 - lam_init)
    o_ref[...] = y.T.astype(BF16)


def _attention(lam_p, qt, k12, vt, g_b, *, batch, seq, ctx_len, latent, lam_init, direct):
    n = k12.shape[1]
    nl = batch * seq
    tq = ATT_TQ if latent else ctx_len
    nq = seq // tq if latent else 1
    tk = ctx_len if direct else ATT_TK
    assert seq % ((ATT_UNROLL if direct else 1) * tk) == 0
    qbase = 0 if latent else nl // tq

    def qrow(b, iq):
        return (b * nq + iq) if latent else (qbase + b)

    in_specs = [
        pl.BlockSpec(lam_p.shape, lambda b, h, iq: (0, 0)),
        pl.BlockSpec((1, 2 * HEAD_DIM, tq), lambda b, h, iq: (h, 0, qrow(b, iq))),
        pl.BlockSpec((1, ctx_len, 2 * HEAD_DIM), lambda b, h, iq: (h, nl // ctx_len + b, 0)),
        pl.BlockSpec((1, V_DIM, ctx_len), lambda b, h, iq: (h, 0, nl // ctx_len + b)),
    ]
    args = [lam_p, qt, k12, vt]
    if latent:
        in_specs += [
            pl.BlockSpec((1, seq, 2 * HEAD_DIM), lambda b, h, iq: (h, b, 0)),
            pl.BlockSpec((1, V_DIM, seq), lambda b, h, iq: (h, 0, b)),
        ]
        args += [k12, vt]
    in_specs.append(pl.BlockSpec((V_DIM, tq), lambda b, h, iq: (0, 0)))
    args.append(g_b[:, :tq])
    nrows = nl if latent else batch * ctx_len
    body = _attn_direct_kernel if direct else _attn_kernel
    kern = functools.partial(body, tk=tk, n_lat_chunks=(seq // tk if latent else 0), lam_init=lam_init)
    if direct:
        scratch = [pltpu.VMEM((2, 2, ATT_GROUP * tk, tq), BF16),
                   pltpu.VMEM((2, 8, tq), F32), pltpu.VMEM((2, V_DIM, tq), F32)]
    else:
        scratch = [pltpu.VMEM((2, 1, tq), F32), pltpu.VMEM((2, 1, tq), F32), pltpu.VMEM((2, V_DIM, tq), F32)]
    return pl.pallas_call(
        kern,
        grid=(batch, HEADS, nq),
        in_specs=in_specs,
        out_specs=pl.BlockSpec((tq, V_DIM), lambda b, h, iq: (b * nq + iq, h)),
        out_shape=jax.ShapeDtypeStruct((nrows, HEADS * V_DIM), BF16),
        scratch_shapes=scratch,
        compiler_params=_cparams(("arbitrary", "arbitrary", "arbitrary")),
    )(*args)


def _halo_specs(width, *, row0, seq, tile):
    hb = tile // HALO

    def cur(b, i):
        return ((row0 + b * seq) // tile + i, 0)

    def prev(b, i):
        first = (row0 + b * seq) // HALO
        return (jnp.maximum(first + i * hb - 1, first), 0)

    def nxt(b, i):
        first = (row0 + b * seq) // HALO
        return (jnp.minimum(first + (i + 1) * hb, first + seq // HALO - 1), 0)

    return [pl.BlockSpec((HALO, width), prev), pl.BlockSpec((tile, width), cur), pl.BlockSpec((HALO, width), nxt)]


def _fill_ext(ext_ref, prev_ref, cur_ref, next_ref, *, seq, tile):
    pos0 = pl.program_id(1) * tile
    width = cur_ref.shape[1]
    hpos = lax.broadcasted_iota(jnp.int32, (HALO, width), 0)
    ext_ref[0:HALO, :] = jnp.where(pos0 - HALO + hpos >= 0, prev_ref[...].astype(F32), 0.0)
    ext_ref[HALO:HALO + tile, :] = cur_ref[...].astype(F32)
    ext_ref[HALO + tile:, :] = jnp.where(pos0 + tile + hpos < seq, next_ref[...].astype(F32), 0.0)


def _pool_kernel(prev_ref, cur_ref, next_ref, lin_ref, scale_ref, o_ref, ext_ref, s_ref, *, seq, tile):
    _fill_ext(ext_ref, prev_ref, cur_ref, next_ref, seq=seq, tile=tile)
    width = cur_ref.shape[1]
    gd = width // len(POOL_WINDOWS)
    n0 = tile + 2 * HALO
    s_ref[0, 0:n0 - 8, :] = ext_ref[0:n0 - 8, :] + ext_ref[1:n0 - 7, :]
    for k in range(1, len(POOL_WINDOWS)):
        step = 1 << k
        ln = n0 - 8 * (k + 1)
        s_ref[k, 0:ln, :] = s_ref[k - 1, 0:ln, :] + s_ref[k - 1, step:step + ln, :]
    lane = lax.broadcasted_iota(jnp.int32, (tile, width), 1)
    pos = pl.program_id(1) * tile + lax.broadcasted_iota(jnp.int32, (tile, width), 0)
    grp = lane // gd
    wsum = jnp.zeros((tile, width), F32)
    half = jnp.zeros((tile, width), jnp.int32)
    for k, win in enumerate(POOL_WINDOWS):
        start = HALO - win // 2
        wsum = jnp.where(grp == k, s_ref[k, start:start + tile, :], wsum)
        half = jnp.where(grp == k, win // 2, half)
    cnt = jnp.minimum(pos + half, seq) - jnp.maximum(pos - half, 0)
    z = ext_ref[HALO:HALO + tile, :]
    dlt = (wsum / cnt.astype(F32) - z).astype(BF16)
    o_ref[...] = (_dot(dlt, lin_ref[...]) * scale_ref[...]).astype(BF16)


def _pool(zp, lin_bd, scale, *, row0, batch, seq, out_rows, out_row0):
    width = zp.shape[1]
    tile = min(ROW_TILE, seq)
    kern = functools.partial(_pool_kernel, seq=seq, tile=tile)
    return pl.pallas_call(
        kern,
        grid=(batch, seq // tile),
        in_specs=_halo_specs(width, row0=row0, seq=seq, tile=tile) + [
            pl.BlockSpec((width, width), lambda b, i: (0, 0)),
            pl.BlockSpec((1, width), lambda b, i: (0, 0)),
        ],
        out_specs=pl.BlockSpec((tile, width), lambda b, i: ((out_row0 + b * seq) // tile + i, 0)),
        out_shape=jax.ShapeDtypeStruct((out_rows, width), BF16),
        scratch_shapes=[
            pltpu.VMEM((tile + 2 * HALO, width), F32),
            pltpu.VMEM((4, tile + 2 * HALO, width), F32),
        ],
        compiler_params=_cparams(("arbitrary", "arbitrary")),
    )(zp, zp, zp, lin_bd, scale)


def _hy_pre_kernel(prev_ref, cur_ref, next_ref, w_ref, b_ref, u_ref, x0_ref, ext_ref, *, seq, tile, hw):
    _fill_ext(ext_ref, prev_ref, cur_ref, next_ref, seq=seq, tile=tile)
    y = b_ref[...] + ext_ref[HALO - 1:HALO - 1 + tile, :] * w_ref[0:1, :]
    y = y + ext_ref[HALO:HALO + tile, :] * w_ref[1:2, :]
    y = y + ext_ref[HALO + 1:HALO + 1 + tile, :] * w_ref[2:3, :]
    x0_ref[...] = y[:, :hw]
    u_ref[...] = y[:, 2 * hw:] * y[:, hw:2 * hw]


def _hy_pre(zh, sw, sb, *, row0, batch, seq, out_rows, out_row0):
    width = zh.shape[1]
    hw = width // 3
    tile = min(ROW_TILE, seq)
    kern = functools.partial(_hy_pre_kernel, seq=seq, tile=tile, hw=hw)
    ospec = pl.BlockSpec((tile, hw), lambda b, i: ((out_row0 + b * seq) // tile + i, 0))
    return pl.pallas_call(
        kern,
        grid=(batch, seq // tile),
        in_specs=_halo_specs(width, row0=row0, seq=seq, tile=tile) + [
            pl.BlockSpec((3, width), lambda b, i: (0, 0)),
            pl.BlockSpec((1, width), lambda b, i: (0, 0)),
        ],
        out_specs=[ospec, ospec],
        out_shape=[jax.ShapeDtypeStruct((out_rows, hw), F32), jax.ShapeDtypeStruct((out_rows, hw), F32)],
        scratch_shapes=[pltpu.VMEM((tile + 2 * HALO, width), F32)],
        compiler_params=_cparams(("arbitrary", "arbitrary")),
    )(zh, zh, zh, sw, sb)


def _filter_kernel(feat_ref, w1_ref, b1_ref, f1_ref, w2_ref, b2_ref, f2_ref, w3_ref, dl_ref,
                   taps_ref, asum_ref, *, tile, hw, seq, embp):
    feat = feat_ref[...]
    h = jnp.sin(f1_ref[...] * (_dot3(feat, w1_ref[...]) + b1_ref[...]))
    h = jnp.sin(f2_ref[...] * (_dot3(h, w2_ref[...]) + b2_ref[...]))
    h = _dot3(h, w3_ref[...])
    odd = lax.broadcasted_iota(jnp.int32, (tile, 2 * hw), 1) >= hw
    tcol = jnp.where(odd, feat[:, embp:embp + 1], feat[:, 0:1])
    dec = jnp.exp(-tcol * dl_ref[...])
    row = pl.program_id(0) * tile + lax.broadcasted_iota(jnp.int32, (tile, 2 * hw), 0)
    pos = 2 * row + odd.astype(jnp.int32)
    taps = jnp.where(pos == seq, 0.0, h * dec)
    taps_ref[...] = taps
    part = jnp.sum(jnp.abs(taps), axis=0, keepdims=True)

    @pl.when(pl.program_id(0) == 0)
    def _():
        asum_ref[...] = part

    @pl.when(pl.program_id(0) != 0)
    def _():
        asum_ref[...] = asum_ref[...] + part


def _hy_filter(seq, w1, b1, f1, w2, b2, f2, w3, hw):
    ar = jnp.arange(seq, dtype=jnp.int32)
    pos = jnp.concatenate([ar, (seq - ar) % seq]).astype(F32)[:, None]
    t = pos * (1.0 / (seq - 1))
    w = (2.0 * math.pi / seq) * pos
    bands = jnp.linspace(1e-4, HY_BANDS - 1, HY_BANDS, dtype=F32)[None, :]
    emb = 1 + 2 * HY_BANDS
    embp = ((emb + 7) // 8) * 8
    feat = jnp.concatenate([t, jnp.cos(bands * w), -jnp.sin(bands * w), jnp.zeros((2 * seq, embp - emb), F32)], axis=-1)
    w1p = jnp.concatenate([w1, jnp.zeros((embp - emb, w1.shape[1]), F32)], axis=0)
    max_decay = math.log(1.0 / HY_TARGET) / HY_FAST
    min_decay = math.log(1.0 / HY_TARGET) / HY_SLOW
    deltas = jnp.linspace(min_decay, max_decay, hw, dtype=F32)[None, :]
    feat = feat.reshape(seq, 2 * embp)
    hid = w1.shape[1]

    def pair(m):
        z = jnp.zeros_like(m)
        return jnp.concatenate([jnp.concatenate([m, z], axis=1), jnp.concatenate([z, m], axis=1)], axis=0)

    twice = lambda v: jnp.concatenate([v, v])[None, :]
    w3p = jnp.concatenate([pair(w3[:, :hw]), pair(w3[:, hw:])], axis=1)
    tile = min(ROW_TILE, seq) // 2
    nfwd = seq // (2 * tile)
    full = lambda shape: pl.BlockSpec(shape, lambda i: (0,) * len(shape))
    taps, asum = pl.pallas_call(
        functools.partial(_filter_kernel, tile=tile, hw=hw, seq=seq, embp=embp),
        grid=(2 * nfwd,),
        in_specs=[
            pl.BlockSpec((tile, 2 * embp), lambda i: (i, 0)),
            full((2 * embp, 2 * hid)), full((1, 2 * hid)), full((1, 2 * hid)),
            full((2 * hid, 2 * hid)), full((1, 2 * hid)), full((1, 2 * hid)),
            pl.BlockSpec((2 * hid, 2 * hw), lambda i: (0, i // nfwd)), full((1, 2 * hw)),
        ],
        out_specs=[
            pl.BlockSpec((tile, 2 * hw), lambda i: (i, 0)),
            pl.BlockSpec((1, 2 * hw), lambda i: (0, 0)),
        ],
        out_shape=[
            jax.ShapeDtypeStruct((seq, 2 * hw), F32),
            jax.ShapeDtypeStruct((1, 2 * hw), F32),
        ],
        compiler_params=_cparams(("arbitrary",)),
    )(feat, pair(w1p), twice(b1), twice(f1), pair(w2), twice(b2), twice(f2), w3p,
      jnp.concatenate([deltas, deltas], axis=1))
    return taps.reshape(2 * seq, hw), asum[:, :hw] + asum[:, hw:]


def _np_split(a):
    a32 = jnp.asarray(a, F32)
    hi = a32.astype(BF16)
    lo = (a32 - hi.astype(F32)).astype(BF16)
    return hi, lo


def _dft_consts(n2):
    n1 = FFT_N1
    n = n1 * n2
    half = n1 // 2
    a1 = -2.0 * np.pi * np.outer(np.arange(n1), np.arange(n1)) / n1
    f1r, f1i = np.cos(a1), np.sin(a1)
    f1_data = np.block([[f1r[:, :half], -f1i[:, :half]], [f1i[:, :half], f1r[:, :half]]])
    f1_real = np.concatenate([f1r, f1i], axis=0)
    g1r, g1i = f1r / n, -f1i / n
    g1 = np.block([[g1r[:half], -g1i[:half]], [g1i[:half], g1r[:half]]])
    a2 = -2.0 * np.pi * np.outer(np.arange(n2), np.arange(n2)) / n2
    f2r, f2i = np.cos(a2), np.sin(a2)
    f2 = np.block([[f2r, -f2i], [f2i, f2r]])
    g2 = np.block([[f2r, f2i], [-f2i, f2r]])
    at = -2.0 * np.pi * np.outer(np.arange(n1), np.arange(n2)) / n
    twr = jnp.broadcast_to(jnp.asarray(np.cos(at), F32)[:, :, None], (n1, n2, LANES))
    twi = jnp.broadcast_to(jnp.asarray(np.sin(at), F32)[:, :, None], (n1, n2, LANES))
    return dict(f1_data=_np_split(f1_data), f1_real=_np_split(f1_real), g1=_np_split(g1),
                f2=_np_split(f2), g2=_np_split(g2), twr=twr, twi=twi)


def _fft1_kernel(z_ref, fh_ref, fl_ref, ar_ref, ai_ref):
    a = _dot3c(fh_ref[...], fl_ref[...], z_ref[...])
    ar_ref[...] = a[:FFT_N1]
    ai_ref[...] = a[FFT_N1:]


def _fft_first(zview, fmat, ncols):
    cb = min(FFT_COLS, ncols)
    fh, fl = fmat
    cspec = pl.BlockSpec(fh.shape, lambda j: (0, 0))
    ospec = pl.BlockSpec((FFT_N1, cb), lambda j: (0, j))
    return pl.pallas_call(
        _fft1_kernel,
        grid=(ncols // cb,),
        in_specs=[pl.BlockSpec((FFT_N1, cb), lambda j: (0, j)), cspec, cspec],
        out_specs=[ospec, ospec],
        out_shape=[jax.ShapeDtypeStruct((FFT_N1, ncols), F32)] * 2,
        compiler_params=_cparams(("arbitrary",)),
    )(zview, fh, fl)


def _fftmid_kernel(ar_ref, ai_ref, twr_ref, twi_ref, f2h_ref, f2l_ref, *rest, filter_only, n2):
    kb, _, width = ar_ref.shape
    reps = width // LANES
    lanes = lambda parts: jnp.concatenate(parts, axis=1)
    twr = lanes([t for j in range(kb) for t in [twr_ref[j]] * reps])
    twi = lanes([t for j in range(kb) for t in [twi_ref[j]] * reps])
    ar = lanes([ar_ref[j] for j in range(kb)])
    ai = lanes([ai_ref[j] for j in range(kb)])
    z = jnp.concatenate([ar * twr - ai * twi, ar * twi + ai * twr], axis=0)
    x = _dot3c(f2h_ref[...], f2l_ref[...], z)
    xr, xi = x[:n2], x[n2:]
    if filter_only:
        kr_out, ki_out = rest
        for j in range(kb):
            kr_out[j] = xr[:, j * width:(j + 1) * width]
            ki_out[j] = xi[:, j * width:(j + 1) * width]
        return
    kr_ref, ki_ref, g2h_ref, g2l_ref, br_out, bi_out = rest
    kr = lanes([kr_ref[j] for j in range(kb)])
    ki = lanes([ki_ref[j] for j in range(kb)])
    y = jnp.concatenate([xr * kr - xi * ki, xr * ki + xi * kr], axis=0)
    w = _dot3c(g2h_ref[...], g2l_ref[...], y)
    wr, wi = w[:n2], w[n2:]
    br = wr * twr + wi * twi
    bi = wi * twr - wr * twi
    for j in range(kb):
        br_out[j] = br[:, j * width:(j + 1) * width]
        bi_out[j] = bi[:, j * width:(j + 1) * width]


def _fft_mid(ar, ai, consts, n2, width, spectrum=None):
    a3r = ar.reshape(FFT_N1, n2, width)
    a3i = ai.reshape(FFT_N1, n2, width)
    blk = pl.BlockSpec((FFT_KB, n2, width), lambda k: (k, 0, 0))
    twspec = pl.BlockSpec((FFT_KB, n2, LANES), lambda k: (k, 0, 0))
    cspec = pl.BlockSpec((2 * n2, 2 * n2), lambda k: (0, 0))
    in_specs = [blk, blk, twspec, twspec, cspec, cspec]
    args = [a3r, a3i, consts["twr"], consts["twi"], *consts["f2"]]
    if spectrum is not None:
        in_specs += [blk, blk, cspec, cspec]
        args += [spectrum[0], spectrum[1], *consts["g2"]]
    return pl.pallas_call(
        functools.partial(_fftmid_kernel, filter_only=spectrum is None, n2=n2),
        grid=(FFT_N1 // FFT_KB,),
        in_specs=in_specs,
        out_specs=[blk, blk],
        out_shape=[jax.ShapeDtypeStruct((FFT_N1, n2, width), F32)] * 2,
        compiler_params=_cparams(("arbitrary",)),
    )(*args)


def _fftlast_kernel(br_ref, bi_ref, gh_ref, gl_ref, u_ref, x0_ref, bias_ref, invn_ref, o_ref):
    b = jnp.concatenate([br_ref[...], bi_ref[...]], axis=0)
    y = _dot3c(gh_ref[...], gl_ref[...], b)
    o_ref[...] = ((y * invn_ref[...] + u_ref[...] * bias_ref[...]) * x0_ref[...]).astype(BF16)


def _fft_last(br, bi, gmat, uview, x0view, bias_t, invn_t, ncols):
    cb = min(FFT_COLS, ncols)
    gh, gl = gmat
    cspec = pl.BlockSpec(gh.shape, lambda j: (0, 0))
    dspec = pl.BlockSpec((FFT_N1, cb), lambda j: (0, j))
    vspec = pl.BlockSpec((1, cb), lambda j: (0, 0))
    return pl.pallas_call(
        _fftlast_kernel,
        grid=(ncols // cb,),
        in_specs=[dspec, dspec, cspec, cspec, dspec, dspec, vspec, vspec],
        out_specs=dspec,
        out_shape=jax.ShapeDtypeStruct((FFT_N1, ncols), BF16),
        compiler_params=_cparams(("arbitrary",)),
    )(br, bi, gh, gl, uview, x0view, bias_t, invn_t)


def _dft_small_consts(seq):
    n = 2 * seq
    a = -2.0 * np.pi * np.outer(np.arange(n), np.arange(n)) / n
    fr, fi = np.cos(a), np.sin(a)
    f_data = np.block([[fr[:, :seq], -fi[:, :seq]], [fi[:, :seq], fr[:, :seq]]])
    f_real = np.concatenate([fr, fi], axis=0)
    gr, gi = fr / n, -fi / n
    g = np.block([[gr[:seq], -gi[:seq]], [gi[:seq], gr[:seq]]])
    return _np_split(f_data), _np_split(f_real), _np_split(g)


def _hy_ctx_kernel(u_ref, x0_ref, taps_ref, fdh_ref, fdl_ref, frh_ref, frl_ref, gh_ref, gl_ref,
                   bias_ref, invn_ref, o_ref):
    n = taps_ref.shape[0]
    z = u_ref[...]
    a = _dot3c(fdh_ref[...], fdl_ref[...], z)
    k = _dot3c(frh_ref[...], frl_ref[...], taps_ref[...])
    ar, ai, kr, ki = a[:n], a[n:], k[:n], k[n:]
    y = jnp.concatenate([ar * kr - ai * ki, ar * ki + ai * kr], axis=0)
    conv = _dot3c(gh_ref[...], gl_ref[...], y)
    o_ref[...] = ((conv * invn_ref[...] + z * bias_ref[...]) * x0_ref[...]).astype(BF16)


def _hy_ctx(u, x0, taps, bias, invn, small):
    rows, hw = u.shape
    full = lambda a: pl.BlockSpec(a.shape, lambda i: (0,) * a.ndim)
    args = [u, x0, taps, *small[0], *small[1], *small[2], bias, invn]
    return pl.pallas_call(
        _hy_ctx_kernel,
        grid=(1,),
        in_specs=[full(a) for a in args],
        out_specs=pl.BlockSpec((rows, hw), lambda i: (0, 0)),
        out_shape=jax.ShapeDtypeStruct((rows, hw), BF16),
        compiler_params=_cparams(("arbitrary",)),
    )(*args)


def _out_proj_kernel(x_ref, ypl_ref, yhl_ref, yal_ref, ypc_ref, yhc_ref, yac_ref, w_ref, g_ref, mod_ref, *rest,
                     nlt, tpb, ncond, pw, hw, moe, ntiles):
    if moe:
        rw_ref, xo_ref, u_ref, route_ref, un_ref = rest
    else:
        xo_ref, u_ref = rest
    t = jnp.minimum(pl.program_id(0), ntiles - 1)
    if moe:
        @pl.when(pl.program_id(0) == 0)
        def _():
            un_ref[...] = jnp.zeros_like(un_ref)

        logits = _dot3(un_ref[...], rw_ref[...])
    ci = _cond_row(t, nlt, tpb, ncond)
    is_ctx = t >= nlt
    yp = jnp.where(is_ctx, ypc_ref[...], ypl_ref[...])
    yh = jnp.where(is_ctx, yhc_ref[...], yhl_ref[...])
    ya = jnp.where(is_ctx, yac_ref[...], yal_ref[...])
    mix = _dot(yp, w_ref[0:pw, :]) + _dot(yh, w_ref[pw:pw + hw, :]) + _dot(ya, w_ref[pw + hw:, :])
    x = x_ref[...] + mod_ref[2, pl.ds(ci, 1), :] * mix
    xo_ref[...] = x
    un = _normmod(x, g_ref[...], mod_ref[3, pl.ds(ci, 1), :], mod_ref[4, pl.ds(ci, 1), :])
    u_ref[...] = un.astype(BF16)
    if moe:
        un_ref[...] = un
        lane = lax.broadcasted_iota(jnp.int32, logits.shape, 1)
        neg = jnp.float32(-jnp.inf)
        lg = jnp.where(lane < N_EXPERTS, logits, neg)
        t1 = jnp.max(lg, axis=-1, keepdims=True)
        i1 = jnp.min(jnp.where(lg == t1, lane, LANES), axis=-1, keepdims=True)
        lg2 = jnp.where(lane == i1, neg, lg)
        t2 = jnp.max(lg2, axis=-1, keepdims=True)
        i2 = jnp.min(jnp.where(lg2 == t2, lane, LANES), axis=-1, keepdims=True)
        e2 = jnp.exp(t2 - t1)
        g1 = 1.0 / (1.0 + e2)
        g2 = e2 / (1.0 + e2)
        route_ref[...] = jnp.where(lane == 0, i1.astype(F32), jnp.where(lane == 1, i2.astype(F32),
                                   jnp.where(lane == 2, g1, jnp.where(lane == 3, g2, 0.0))))


def _out_proj(x, lat, ctx, w, g, mod, rw, *, nlt, tpb, ncond):
    n, d = x.shape
    pw, hw, aw = (a.shape[1] for a in lat)
    tm = ROW_TILE
    moe = rw is not None
    nt = n // tm
    cur = lambda t: jnp.minimum(t, nt - 1)
    row = lambda width: pl.BlockSpec((tm, width), lambda t: (cur(t), 0))
    lrow = lambda width: pl.BlockSpec((tm, width), lambda t: (jnp.minimum(t, nlt - 1), 0))
    crow = lambda width: pl.BlockSpec((tm, width), lambda t: (jnp.maximum(cur(t) - nlt, 0), 0))
    in_specs = [row(d), lrow(pw), lrow(hw), lrow(aw), crow(pw), crow(hw), crow(aw),
                pl.BlockSpec(w.shape, lambda t: (0, 0)),
                pl.BlockSpec((1, d), lambda t: (0, 0)),
                pl.BlockSpec(mod.shape, lambda t: (0, 0, 0))]
    args = [x, *lat, *ctx, w, g, mod]
    out_specs = [row(d), row(d)]
    out_shape = [jax.ShapeDtypeStruct((n, d), F32), jax.ShapeDtypeStruct((n, d), BF16)]
    scratch = []
    if moe:
        in_specs.append(pl.BlockSpec(rw.shape, lambda t: (0, 0)))
        args.append(rw)
        out_specs.append(pl.BlockSpec((tm, LANES), lambda t: (jnp.maximum(t - 1, 0), 0)))
        out_shape.append(jax.ShapeDtypeStruct((n, LANES), F32))
        scratch.append(pltpu.VMEM((tm, d), F32))
    kern = functools.partial(_out_proj_kernel, nlt=nlt, tpb=tpb, ncond=ncond, pw=pw, hw=hw, moe=moe, ntiles=nt)
    return pl.pallas_call(
        kern, grid=(nt + 1 if moe else nt,), in_specs=in_specs, out_specs=out_specs, out_shape=out_shape,
        scratch_shapes=scratch,
        compiler_params=_cparams(("arbitrary",)),
    )(*args)


def _ffn_kernel(x_ref, u_ref, mod_ref, wg_ref, wu_ref, wd_ref, o_ref, *, nlt, tpb, ncond, nchunk):
    ci = _cond_row(pl.program_id(0), nlt, tpb, ncond)
    u = u_ref[...]
    ff = wg_ref.shape[1]
    fc = ff // nchunk
    y = jnp.zeros(x_ref.shape, F32)
    for c in range(nchunk):
        gate = _dot(u, wg_ref[:, c * fc:(c + 1) * fc])
        up = _dot(u, wu_ref[:, c * fc:(c + 1) * fc])
        y = y + _dot((_silu(gate) * up).astype(BF16), wd_ref[c * fc:(c + 1) * fc, :])
    o_ref[...] = x_ref[...] + mod_ref[5, pl.ds(ci, 1), :] * y


def _ffn(x, u, mod, wg, wu, wd, *, nlt, tpb, ncond, out_rows):
    n, d = x.shape
    ff = wg.shape[1]
    tm = ROW_TILE
    nchunk = 2 if (ff // 2) % LANES == 0 else 1
    row = lambda dt: pl.BlockSpec((tm, d), lambda t: (t, 0))
    const = lambda shape: pl.BlockSpec(shape, lambda t: (0,) * len(shape), pipeline_mode=pl.Buffered(1))
    kern = functools.partial(_ffn_kernel, nlt=nlt, tpb=tpb, ncond=ncond, nchunk=nchunk)
    return pl.pallas_call(
        kern,
        grid=(out_rows // tm,),
        in_specs=[row(F32), row(BF16), pl.BlockSpec(mod.shape, lambda t: (0, 0, 0)),
                  const((d, ff)), const((d, ff)), const((ff, d))],
        out_specs=row(F32),
        out_shape=jax.ShapeDtypeStruct((out_rows, d), F32),
        compiler_params=_cparams(("arbitrary",)),
    )(x, u, mod, wg, wu, wd)


def _moe_kernel(te_ref, tv_ref, u_ref, wg_ref, wu_ref, wd_ref, *rest):
    o_ref, acc_ref = rest[-2:]
    t = pl.program_id(0)
    f = pl.program_id(1)

    @pl.when(tv_ref[t] > 0)
    def _():
        u = u_ref[...]
        h = (_silu(_dot(u, wg_ref[0, 0])) * _dot(u, wu_ref[0, 0])).astype(BF16)
        y = _dot(h, wd_ref[0, 0])

        @pl.when(f == 0)
        def _():
            acc_ref[...] = y

        @pl.when(f != 0)
        def _():
            acc_ref[...] = acc_ref[...] + y

        @pl.when(f == pl.num_programs(1) - 1)
        def _():
            o_ref[...] = acc_ref[...].astype(BF16)


def _moe_experts(ug, tile_expert, tile_valid, wg, wu, wd, j, *, tile0, total_rows, prev=None):
    p, d = ug.shape
    ff = wg.shape[3]
    tm, fc = MOE_TM, MOE_FC
    in_specs = [
        pl.BlockSpec((tm, d), lambda t, f, te, tv: (t, 0)),
        pl.BlockSpec((1, 1, d, fc), lambda t, f, te, tv: (j, te[t], 0, f)),
        pl.BlockSpec((1, 1, d, fc), lambda t, f, te, tv: (j, te[t], 0, f)),
        pl.BlockSpec((1, 1, fc, d), lambda t, f, te, tv: (j, te[t], f, 0)),
    ]
    args = [tile_expert, tile_valid, ug, wg, wu, wd]
    aliases = {}
    if prev is not None:
        in_specs.append(pl.BlockSpec(memory_space=pl.ANY))
        args.append(prev)
        aliases = {len(args) - 1: 0}
    grid_spec = pltpu.PrefetchScalarGridSpec(
        num_scalar_prefetch=2,
        grid=(p // tm, ff // fc),
        in_specs=in_specs,
        out_specs=pl.BlockSpec((tm, d), lambda t, f, te, tv: (t + tile0, 0)),
        scratch_shapes=[pltpu.VMEM((tm, d), F32)],
    )
    return pl.pallas_call(
        _moe_kernel, grid_spec=grid_spec,
        out_shape=jax.ShapeDtypeStruct((total_rows, d), BF16),
        input_output_aliases=aliases,
        compiler_params=_cparams(("arbitrary", "arbitrary")),
    )(*args)


def _moe_combine_kernel(x_ref, ya_ref, yb_ref, route_ref, mod_ref, *rest, nlt, tpb, ncond, tile0):
    o_ref = rest[-1]
    ci = _cond_row(pl.program_id(0) + tile0, nlt, tpb, ncond)
    r = route_ref[...]
    lane = lax.broadcasted_iota(jnp.int32, r.shape, 1)
    g1 = jnp.sum(jnp.where(lane == 2, r, 0.0), axis=-1, keepdims=True)
    g2 = jnp.sum(jnp.where(lane == 3, r, 0.0), axis=-1, keepdims=True)
    y = g1 * ya_ref[...].astype(F32) + g2 * yb_ref[...].astype(F32)
    o_ref[...] = x_ref[...] + mod_ref[5, pl.ds(ci, 1), :] * y


def _moe_combine(x, ya, yb, route, mod, *, nlt, tpb, ncond, out_rows, tile0, prev=None):
    d = x.shape[1]
    tm = ROW_TILE
    glob = lambda width: pl.BlockSpec((tm, width), lambda t: (t + tile0, 0))
    loc = pl.BlockSpec((tm, d), lambda t: (t, 0))
    in_specs = [glob(d), loc, loc, glob(LANES), pl.BlockSpec(mod.shape, lambda t: (0, 0, 0))]
    args = [x, ya, yb, route, mod]
    aliases = {}
    if prev is not None:
        in_specs.append(pl.BlockSpec(memory_space=pl.ANY))
        args.append(prev)
        aliases = {len(args) - 1: 0}
    kern = functools.partial(_moe_combine_kernel, nlt=nlt, tpb=tpb, ncond=ncond, tile0=tile0)
    return pl.pallas_call(
        kern,
        grid=(ya.shape[0] // tm,),
        in_specs=in_specs,
        out_specs=glob(d),
        out_shape=jax.ShapeDtypeStruct((out_rows, d), F32),
        input_output_aliases=aliases,
        compiler_params=_cparams(("arbitrary",)),
    )(*args)


def _moe_layer(x, u, route, mod, wg, wu, wd, j, *, nlt, tpb, ncond, out_rows):
    n, d = x.shape
    tm = MOE_TM
    experts = jnp.concatenate([route[:, 0], route[:, 1]]).astype(jnp.int32)
    onehot = (experts[:, None] == jnp.arange(N_EXPERTS, dtype=jnp.int32)[None, :]).astype(jnp.int32)
    rank = jnp.sum(onehot * (jnp.cumsum(onehot, axis=0) - 1), axis=1)
    counts = jnp.sum(onehot, axis=0)
    padded = ((counts + tm - 1) // tm) * tm
    ends = jnp.cumsum(padded)
    starts = ends - padded
    dest = starts[experts] + rank
    p = 2 * n + N_EXPERTS * tm
    tokens = jnp.concatenate([jnp.arange(n, dtype=jnp.int32)] * 2)
    inb = dict(mode="promise_in_bounds")
    src = jnp.zeros((p,), jnp.int32).at[dest].set(tokens, unique_indices=True, **inb)
    tile_start = jnp.arange(p // tm, dtype=jnp.int32) * tm
    tile_expert = jnp.minimum(jnp.searchsorted(ends, tile_start, side="right"), N_EXPERTS - 1).astype(jnp.int32)
    tile_valid = (tile_start < ends[-1]).astype(jnp.int32)
    nt0 = (p // tm) // 2
    ys = None
    for lo, hi in ((0, nt0), (nt0, p // tm)):
        ug = u.at[src[lo * tm:hi * tm]].get(**inb)
        ys = _moe_experts(ug, tile_expert[lo:hi], tile_valid[lo:hi], wg, wu, wd, j,
                          tile0=lo, total_rows=p, prev=ys)
    rt = ROW_TILE
    nt_out = out_rows // rt
    out = None
    for lo, hi in ((0, nt_out // 2), (nt_out // 2, nt_out)):
        ya = ys.at[dest[lo * rt:hi * rt]].get(**inb)
        yb = ys.at[dest[n + lo * rt:n + hi * rt]].get(**inb)
        out = _moe_combine(x, ya, yb, route, mod, nlt=nlt, tpb=tpb, ncond=ncond, out_rows=out_rows,
                           tile0=lo, prev=out)
    return out


def _rope_tables(seq, n_ctx_rows, batch):
    rows = seq // GRID_W
    row = jnp.repeat(jnp.arange(rows, dtype=F32), GRID_W)
    col = jnp.broadcast_to(jnp.arange(GRID_W, dtype=F32), (rows, GRID_W)).reshape(-1)
    inv_freq = jnp.power(ROPE_THETA, -jnp.arange(ROPE_FREQS, dtype=F32) / ROPE_FREQS)
    ar = row[:, None] * inv_freq
    ac = col[:, None] * inv_freq
    cos = jnp.concatenate([jnp.cos(ar), jnp.cos(ar), jnp.cos(ac), jnp.cos(ac)], axis=-1)
    sin = jnp.concatenate([-jnp.sin(ar), jnp.sin(ar), -jnp.sin(ac), jnp.sin(ac)], axis=-1)
    cos = jnp.tile(jnp.concatenate([cos, cos], axis=-1), (batch, 1))
    sin = jnp.tile(jnp.concatenate([sin, sin], axis=-1), (batch, 1))
    cos = jnp.concatenate([cos, jnp.ones((n_ctx_rows, 2 * HEAD_DIM), F32)], axis=0)
    sin = jnp.concatenate([sin, jnp.zeros((n_ctx_rows, 2 * HEAD_DIM), F32)], axis=0)
    return cos, sin


def kernel(x, c, ctx, c_ctx, mod_w, mod_b, norm1_g, norm2_g, w_in, w_out, pool_lin, pool_scale, hy_short_w, hy_short_b, hy_f_w1, hy_f_b1, hy_f_freq1, hy_f_w2, hy_f_b2, hy_f_freq2, hy_f_w3, hy_bias, qk_norm_g, diff_lambda, subln_g, ffn_w_gate, ffn_w_up, ffn_w_down, router_w, moe_w_gate, moe_w_up, moe_w_down):
    batch, seq, d = x.shape
    ctx_len = ctx.shape[1]
    depth = mod_w.shape[0]
    pw = pool_scale.shape[1]
    hw = hy_bias.shape[1]
    nl, nc = batch * seq, batch * ctx_len
    n = nl + nc
    tm = ROW_TILE
    assert seq % tm == 0 and nc % tm == 0 and seq % (FFT_N1 // 2) == 0 and seq % GRID_W == 0
    assert d == HEADS * 2 * V_DIM and pw == hw and batch == 2
    n2 = 2 * seq // FFT_N1
    ncols = n2 * hw
    assert (n * hw) % ncols == 0
    tiles = dict(nlt=nl // tm, tpb=seq // tm, ncond=batch + 1)

    xs = jnp.concatenate([x.reshape(nl, d), ctx.reshape(nc, d)], axis=0)
    cond8 = jnp.concatenate([c, c_ctx[None, :], jnp.zeros((8 - batch - 1, d), F32)], axis=0)
    mods = _modvec(cond8, mod_w, mod_b)
    cos_t, sin_t = _rope_tables(seq, nc, batch)
    consts = _dft_consts(n2)
    small = _dft_small_consts(ctx_len)
    eye = jnp.eye(len(POOL_WINDOWS), dtype=F32)
    moe_wg, moe_wu, moe_wd = _cast_bf16(moe_w_gate, moe_w_up, moe_w_down)

    for i in range(depth):
        last = i == depth - 1
        lam_init = 0.8 - 0.6 * math.exp(-0.3 * i)
        mod = mods[i]
        gqk = jnp.concatenate([qk_norm_g[i], qk_norm_g[i]], axis=-1)
        zp, zh, qt, k12, vt = _in_proj(xs, norm1_g[i][None, :], mod, w_in[i].astype(BF16), cos_t, sin_t, gqk,
                                       pw=pw, hw=hw, **tiles)

        g_b = jnp.broadcast_to(subln_g[i][:, None], (V_DIM, ATT_TQ))
        att_kw = dict(batch=batch, seq=seq, ctx_len=ctx_len, lam_init=lam_init)

        def attend(direct, lam_p=diff_lambda[i], qt=qt, k12=k12, vt=vt, g_b=g_b, att_kw=att_kw):
            return (_attention(lam_p, qt, k12, vt, g_b, latent=True, direct=direct, **att_kw),
                    _attention(lam_p, qt, k12, vt, g_b, latent=False, direct=direct, **att_kw))

        bound = (HEAD_DIM * QSCALE) * jnp.max(jnp.abs(qk_norm_g[i][0])) * jnp.max(jnp.abs(qk_norm_g[i][1]))
        ya_l, ya_c = lax.cond(bound * 1.02 < ATT_DIRECT_MAX, lambda: attend(True), lambda: attend(False))

        lin_bd = (eye[:, None, :, None] * pool_lin[i][:, :, None, :]).reshape(pw, pw).astype(BF16)
        pscale = pool_scale[i][None, :]
        yp_l = _pool(zp, lin_bd, pscale, row0=0, batch=batch, seq=seq, out_rows=nl, out_row0=0)
        yp_c = _pool(zp, lin_bd, pscale, row0=nl, batch=batch, seq=ctx_len, out_rows=nc, out_row0=0)

        sw, sb = hy_short_w[i], hy_short_b[i][None, :]
        u_l, x0_l = _hy_pre(zh, sw, sb, row0=0, batch=batch, seq=seq, out_rows=n, out_row0=0)
        u_c, x0_c = _hy_pre(zh, sw, sb, row0=nl, batch=batch, seq=ctx_len, out_rows=nc, out_row0=0)
        filt = (hy_f_w1[i], hy_f_b1[i], hy_f_freq1[i], hy_f_w2[i], hy_f_b2[i], hy_f_freq2[i], hy_f_w3[i])
        bias = hy_bias[i][None, :]
        taps, asum = _hy_filter(seq, *filt, hw)
        kr1, ki1 = _fft_first(taps.reshape(FFT_N1, ncols), consts["f1_real"], ncols)
        spectrum = _fft_mid(kr1, ki1, consts, n2, hw)
        ar, ai = _fft_first(u_l.reshape(-1, ncols), consts["f1_data"], ncols)
        br, bi = _fft_mid(ar, ai, consts, n2, hw, spectrum=spectrum)
        reps = min(FFT_COLS, ncols) // hw
        yh_lat = _fft_last(br.reshape(FFT_N1, ncols), bi.reshape(FFT_N1, ncols), consts["g1"],
                           u_l.reshape(-1, ncols), x0_l.reshape(-1, ncols),
                           jnp.tile(bias, (1, reps)), jnp.tile(1.0 / asum, (1, reps)), ncols)
        taps_c, asum_c = _hy_filter(ctx_len, *filt, hw)
        yh_ctx = _hy_ctx(u_c, x0_c, taps_c, bias, 1.0 / asum_c, small)

        j = i // 2
        moe = i % 2 == 1
        rw = None
        if moe:
            rw = jnp.concatenate([router_w[j], jnp.zeros((d, LANES - N_EXPERTS), F32)], axis=1)
        outs = _out_proj(xs, (yp_l, yh_lat.reshape(nl, hw), ya_l), (yp_c, yh_ctx, ya_c), w_out[i].astype(BF16),
                         norm2_g[i][None, :], mod, rw, **tiles)
        out_rows = nl if last else n
        if moe:
            xs, u, route = outs
            xs = _moe_layer(xs, u, route, mod, moe_wg, moe_wu, moe_wd, j, out_rows=out_rows, **tiles)
        else:
            xs, u = outs
            xs = _ffn(xs, u, mod, ffn_w_gate[j].astype(BF16), ffn_w_up[j].astype(BF16),
                      ffn_w_down[j].astype(BF16), out_rows=out_rows, **tiles)
    return xs[:nl].reshape(batch, seq, d)
```

```python
import functools
import math

import numpy as np
import jax
import jax.numpy as jnp
from jax import lax
from jax.experimental import pallas as pl
from jax.experimental.pallas import tpu as pltpu

F32 = jnp.float32
BF16 = jnp.bfloat16
EPS = 1e-6

GRID_W = 64
POOL_WINDOWS = (2, 4, 8, 16)
HEADS = 4
HEAD_DIM = 64
V_DIM = 128
ROPE_FREQS = 16
ROPE_THETA = 10000.0
HY_BANDS = 16
HY_TARGET = 1e-2
HY_FAST = 0.3
HY_SLOW = 1.5
N_EXPERTS = 8

LANES = 128
ROW_TILE = 512
HALO = 32
FFT_N1 = 128
FFT_COLS = 2048
FFT_KB = 4
ATT_TQ = 1024
ATT_TK = 512
ATT_UNROLL = 16
ATT_GROUP = 8
QSCALE = (HEAD_DIM ** -0.5) * math.log2(math.e)
ATT_DIRECT_MAX = 100.0
FFN_CHUNKS = 2
CAST_STEPS = 64
MOE_TM = 512
MOE_FC = 1792
VMEM_LIMIT = 56 * 1024 * 1024


def _cparams(sem):
    return pltpu.CompilerParams(dimension_semantics=sem, vmem_limit_bytes=VMEM_LIMIT)


def _dot(a, b):
    return jnp.dot(a, b, preferred_element_type=F32)


def _split(a):
    hi = a.astype(BF16)
    lo = (a - hi.astype(F32)).astype(BF16)
    return hi, lo


def _dot3(a, b):
    ah, al = _split(a)
    bh, bl = _split(b)
    return _dot(ah, bh) + _dot(ah, bl) + _dot(al, bh)


def _dot3c(ch, cl, d):
    dh, dl = _split(d)
    return _dot(ch, dh) + _dot(ch, dl) + _dot(cl, dh)


def _silu(x):
    return x / (1.0 + jnp.exp(-x))


def _normmod(x, g, shift, scale):
    ms = jnp.mean(x * x, axis=-1, keepdims=True)
    return (x * lax.rsqrt(ms + EPS) * g) * (1.0 + scale) + shift


def _cond_row(t, n_lat_tiles, tiles_per_batch, n_cond):
    return jnp.where(t >= n_lat_tiles, n_cond - 1, t // tiles_per_batch)


def _cast_kernel(*refs):
    half = len(refs) // 2
    for w_ref, o_ref in zip(refs[:half], refs[half:]):
        o_ref[...] = w_ref[...].astype(BF16)


def _cast_bf16(*ws):
    flat = [w.reshape(-1, w.shape[-1]) for w in ws]
    steps = CAST_STEPS
    specs = [pl.BlockSpec((w.shape[0] // steps, w.shape[1]), lambda t: (t, 0)) for w in flat]
    outs = pl.pallas_call(
        _cast_kernel,
        grid=(steps,),
        in_specs=specs,
        out_specs=specs,
        out_shape=[jax.ShapeDtypeStruct(w.shape, BF16) for w in flat],
        compiler_params=_cparams(("arbitrary",)),
    )(*flat)
    return [o.reshape(w.shape) for o, w in zip(outs, ws)]


def _modvec_kernel(c_ref, w_ref, b_ref, o_ref):
    s = _silu(c_ref[...])
    o_ref[0, 0] = _dot3(s, w_ref[0]) + b_ref[0]


def _modvec(cond8, mod_w, mod_b):
    depth, d, six_d = mod_w.shape
    nchunk = six_d // d
    return pl.pallas_call(
        _modvec_kernel,
        grid=(depth, nchunk),
        in_specs=[
            pl.BlockSpec((8, d), lambda i, j: (0, 0)),
            pl.BlockSpec((1, d, d), lambda i, j: (i, 0, j)),
            pl.BlockSpec((1, 1, d), lambda i, j: (i, 0, j)),
        ],
        out_specs=pl.BlockSpec((1, 1, 8, d), lambda i, j: (i, j, 0, 0)),
        out_shape=jax.ShapeDtypeStruct((depth, nchunk, 8, d), F32),
        compiler_params=_cparams(("arbitrary", "arbitrary")),
    )(cond8, mod_w, mod_b.reshape(depth, 1, six_d))


def _in_proj_kernel(x_ref, g_ref, mod_ref, w_ref, cos_ref, sin_ref, gqk_ref,
                    zp_ref, zh_ref, qt_ref, k_ref, vt_ref, zatt_ref, *, nlt, tpb, ncond, pw, hw, ntiles):
    t = pl.program_id(0)

    @pl.when(t == 0)
    def _():
        zatt_ref[...] = jnp.zeros_like(zatt_ref)

    tm = x_ref.shape[0]
    att = pw + 3 * hw
    lane = lax.broadcasted_iota(jnp.int32, (tm, 2 * HEAD_DIM), 1)
    first = lane < HEAD_DIM
    apart = (lane % (2 * ROPE_FREQS)) < ROPE_FREQS
    cos = cos_ref[...]
    sin = sin_ref[...]
    qk_w = HEADS * 2 * HEAD_DIM

    def norm_rope(v, g):
        v2 = v * v
        s_all = jnp.sum(v2, axis=-1, keepdims=True)
        s_first = jnp.sum(jnp.where(first, v2, 0.0), axis=-1, keepdims=True)
        ms = jnp.where(first, s_first, s_all - s_first) * (1.0 / HEAD_DIM)
        vn = v * lax.rsqrt(ms + EPS) * g
        swapped = jnp.where(apart, pltpu.roll(vn, 2 * HEAD_DIM - ROPE_FREQS, 1), pltpu.roll(vn, ROPE_FREQS, 1))
        return vn * cos + swapped * sin

    for h in range(HEADS):
        lo = h * 2 * HEAD_DIM
        q = norm_rope(zatt_ref[:, lo:lo + 2 * HEAD_DIM], gqk_ref[0:1, :])
        qt_ref[h] = (q * QSCALE).T.astype(BF16)
        k = norm_rope(zatt_ref[:, qk_w + lo:qk_w + lo + 2 * HEAD_DIM], gqk_ref[1:2, :])
        k_ref[h] = k.astype(BF16)
        vlo = 2 * qk_w + h * V_DIM
        vt_ref[h] = zatt_ref[:, vlo:vlo + V_DIM].T.astype(BF16)

    ci = _cond_row(jnp.minimum(t, ntiles - 1), nlt, tpb, ncond)
    shift = mod_ref[0, pl.ds(ci, 1), :]
    scale = mod_ref[1, pl.ds(ci, 1), :]
    y = _normmod(x_ref[...], g_ref[...], shift, scale).astype(BF16)
    z = _dot(y, w_ref[...])
    zp_ref[...] = z[:, :pw].astype(BF16)
    zh_ref[...] = z[:, pw:att].astype(BF16)
    zatt_ref[...] = z[:, att:]


def _in_proj(x, g, mod, w, cos_t, sin_t, gqk, *, nlt, tpb, ncond, pw, hw):
    n, d = x.shape
    wid = w.shape[1]
    tm = ROW_TILE
    nt = n // tm
    kern = functools.partial(_in_proj_kernel, nlt=nlt, tpb=tpb, ncond=ncond, pw=pw, hw=hw, ntiles=nt)
    cur = lambda t: jnp.minimum(t, nt - 1)
    prv = lambda t: jnp.maximum(t - 1, 0)
    return pl.pallas_call(
        kern,
        grid=(nt + 1,),
        in_specs=[
            pl.BlockSpec((tm, d), lambda t: (cur(t), 0)),
            pl.BlockSpec((1, d), lambda t: (0, 0)),
            pl.BlockSpec(mod.shape, lambda t: (0, 0, 0)),
            pl.BlockSpec((d, wid), lambda t: (0, 0)),
            pl.BlockSpec((tm, 2 * HEAD_DIM), lambda t: (prv(t), 0)),
            pl.BlockSpec((tm, 2 * HEAD_DIM), lambda t: (prv(t), 0)),
            pl.BlockSpec((2, 2 * HEAD_DIM), lambda t: (0, 0)),
        ],
        out_specs=[
            pl.BlockSpec((tm, pw), lambda t: (cur(t), 0)),
            pl.BlockSpec((tm, 3 * hw), lambda t: (cur(t), 0)),
            pl.BlockSpec((HEADS, 2 * HEAD_DIM, tm), lambda t: (0, 0, prv(t))),
            pl.BlockSpec((HEADS, tm, 2 * HEAD_DIM), lambda t: (0, prv(t), 0)),
            pl.BlockSpec((HEADS, V_DIM, tm), lambda t: (0, 0, prv(t))),
        ],
        scratch_shapes=[pltpu.VMEM((tm, wid - pw - 3 * hw), F32)],
        out_shape=[
            jax.ShapeDtypeStruct((n, pw), BF16),
            jax.ShapeDtypeStruct((n, 3 * hw), BF16),
            jax.ShapeDtypeStruct((HEADS, 2 * HEAD_DIM, n), BF16),
            jax.ShapeDtypeStruct((HEADS, n, 2 * HEAD_DIM), BF16),
            jax.ShapeDtypeStruct((HEADS, V_DIM, n), BF16),
        ],
        compiler_params=_cparams(("arbitrary",)),
    )(x, g, mod, w, cos_t, sin_t, gqk)


def _attn_kernel(lam_ref, qt_ref, kc_ref, vtc_ref, *rest, tk, n_lat_chunks, lam_init):
    if n_lat_chunks:
        kl_ref, vtl_ref, g_ref, o_ref, m_ref, l_ref, acc_ref = rest
    else:
        g_ref, o_ref, m_ref, l_ref, acc_ref = rest
    qt = qt_ref[0]
    row = lax.broadcasted_iota(jnp.int32, qt.shape, 0)
    zero = jnp.zeros_like(qt)
    qmaps = (jnp.where(row < HEAD_DIM, qt, zero), jnp.where(row >= HEAD_DIM, qt, zero))

    def process(k_tile, vt_tile, first):
        for mi in range(2):
            s = _dot(k_tile, qmaps[mi])
            smax = jnp.max(s, axis=0, keepdims=True)
            if first:
                m_new = smax
            else:
                m_old = m_ref[mi]
                m_new = jnp.maximum(m_old, smax)
            p = jnp.exp2(s - m_new)
            psum = jnp.sum(p, axis=0, keepdims=True)
            pv = _dot(vt_tile, p.astype(BF16))
            if first:
                l_ref[mi] = psum
                acc_ref[mi] = pv
            else:
                alpha = jnp.exp2(m_old - m_new)
                l_ref[mi] = alpha * l_ref[mi] + psum
                acc_ref[mi] = alpha * acc_ref[mi] + pv
            m_ref[mi] = m_new

    process(kc_ref[0], vtc_ref[0], True)
    if n_lat_chunks:
        def body(j, carry):
            off = pl.multiple_of(j * tk, tk)
            process(kl_ref[0, pl.ds(off, tk), :], vtl_ref[0, :, pl.ds(off, tk)], False)
            return carry
        lax.fori_loop(0, n_lat_chunks, body, 0)

    lv = lam_ref[...]
    lam = (jnp.exp(jnp.sum(lv[0:1] * lv[1:2], axis=-1, keepdims=True))
           - jnp.exp(jnp.sum(lv[2:3] * lv[3:4], axis=-1, keepdims=True)) + lam_init)
    o = acc_ref[0] * (1.0 / l_ref[0]) - lam * (acc_ref[1] * (1.0 / l_ref[1]))
    ms = jnp.mean(o * o, axis=0, keepdims=True)
    y = o * lax.rsqrt(ms + EPS) * g_ref[...] * (1.0 - lam_init)
    o_ref[...] = y.T.astype(BF16)


def _attn_direct_kernel(lam_ref, qt_ref, kc_ref, vtc_ref, *rest, tk, n_lat_chunks, lam_init):
    if n_lat_chunks:
        kl_ref, vtl_ref, g_ref, o_ref, p_ref, l_ref, acc_ref = rest
    else:
        g_ref, o_ref, p_ref, l_ref, acc_ref = rest
    qt = qt_ref[0]
    tq = qt.shape[1]
    row = lax.broadcasted_iota(jnp.int32, qt.shape, 0)
    zero = jnp.zeros_like(qt)
    qmaps = (jnp.where(row < HEAD_DIM, qt, zero), jnp.where(row >= HEAD_DIM, qt, zero))
    grp = ATT_GROUP

    def keys(first_piece, count):
        if isinstance(first_piece, int):
            return pl.ds((first_piece - 1) * tk, count * tk)
        return pl.ds(pl.multiple_of((first_piece - 1) * tk, tk), count * tk)

    def stage_ab(k_tile, pset, r):
        for mi in range(2):
            p = jnp.exp2(_dot(k_tile, qmaps[mi]))
            l_ref[mi] = l_ref[mi] + jnp.sum(p.reshape(tk // 8, 8, tq), axis=0)
            p_ref[pset, mi, r * tk:(r + 1) * tk, :] = p.astype(BF16)

    def stage_c(vt_tile, pset, lo, count):
        for mi in range(2):
            acc_ref[mi] = acc_ref[mi] + _dot(vt_tile, p_ref[pset, mi, lo * tk:(lo + count) * tk, :])

    def group_ab(g, pset):
        k_tile = kl_ref[0, keys(grp * g, grp), :]
        for mi in range(2):
            p = jnp.exp2(_dot(k_tile, qmaps[mi]))
            l_ref[mi] = l_ref[mi] + jnp.sum(p.reshape(grp * tk // 8, 8, tq), axis=0)
            p_ref[pset, mi] = p.astype(BF16)

    def group_c(g, pset):
        stage_c(vtl_ref[0, :, keys(grp * g, grp)], pset, 0, grp)

    l_ref[...] = jnp.zeros_like(l_ref)
    acc_ref[...] = jnp.zeros_like(acc_ref)
    stage_ab(kc_ref[0], 0, 0)
    if n_lat_chunks == 0:
        stage_c(vtc_ref[0], 0, 0, 1)
    else:
        n_groups = n_lat_chunks // grp
        per_step = ATT_UNROLL // grp
        for r in range(1, grp):
            stage_ab(kl_ref[0, keys(r, 1), :], 0, r)
        group_ab(1, 1)
        stage_c(vtc_ref[0], 0, 0, 1)
        stage_c(vtl_ref[0, :, keys(1, grp - 1)], 0, 1, grp - 1)
        for g in range(2, per_step):
            group_ab(g, g % 2)
            group_c(g - 1, (g - 1) % 2)

        def body(jj, carry):
            g0 = per_step * (jj + 1)
            for q in range(per_step):
                group_ab(g0 + q, q % 2)
                group_c(g0 + q - 1, (q + 1) % 2)
            return carry

        lax.fori_loop(0, n_groups // per_step - 1, body, 0)
        stage_ab(kl_ref[0, keys(n_lat_chunks, 1), :], 0, 0)
        group_c(n_groups - 1, 1)
        stage_c(vtl_ref[0, :, keys(n_lat_chunks, 1)], 0, 0, 1)

    lv = lam_ref[...]
    lam = (jnp.exp(jnp.sum(lv[0:1] * lv[1:2], axis=-1, keepdims=True))
           - jnp.exp(jnp.sum(lv[2:3] * lv[3:4], axis=-1, keepdims=True)) + lam_init)
    l1 = jnp.sum(l_ref[0], axis=0, keepdims=True)
    l2 = jnp.sum(l_ref[1], axis=0, keepdims=True)
    o = acc_ref[0] * (1.0 / l1) - lam * (acc_ref[1] * (1.0 / l2))
    ms = jnp.mean(o * o, axis=0, keepdims=True)
    y = o * lax.rsqrt(ms + EPS) * g_ref[...] * (1.0 - lam_init)
    o_ref[...] = y.T.astype(BF16)


def _attention(lam_p, qt, k12, vt, g_b, *, batch, seq, ctx_len, latent, lam_init, direct):
    n = k12.shape[1]
    nl = batch * seq
    tq = ATT_TQ if latent else ctx_len
    nq = seq // tq if latent else 1
    tk = ctx_len if direct else ATT_TK
    assert seq % ((ATT_UNROLL if direct else 1) * tk) == 0
    qbase = 0 if latent else nl // tq

    def qrow(b, iq):
        return (b * nq + iq) if latent else (qbase + b)

    in_specs = [
        pl.BlockSpec(lam_p.shape, lambda b, h, iq: (0, 0)),
        pl.BlockSpec((1, 2 * HEAD_DIM, tq), lambda b, h, iq: (h, 0, qrow(b, iq))),
        pl.BlockSpec((1, ctx_len, 2 * HEAD_DIM), lambda b, h, iq: (h, nl // ctx_len + b, 0)),
        pl.BlockSpec((1, V_DIM, ctx_len), lambda b, h, iq: (h, 0, nl // ctx_len + b)),
    ]
    args = [lam_p, qt, k12, vt]
    if latent:
        in_specs += [
            pl.BlockSpec((1, seq, 2 * HEAD_DIM), lambda b, h, iq: (h, b, 0)),
            pl.BlockSpec((1, V_DIM, seq), lambda b, h, iq: (h, 0, b)),
        ]
        args += [k12, vt]
    in_specs.append(pl.BlockSpec((V_DIM, tq), lambda b, h, iq: (0, 0)))
    args.append(g_b[:, :tq])
    nrows = nl if latent else batch * ctx_len
    body = _attn_direct_kernel if direct else _attn_kernel
    kern = functools.partial(body, tk=tk, n_lat_chunks=(seq // tk if latent else 0), lam_init=lam_init)
    if direct:
        scratch = [pltpu.VMEM((2, 2, ATT_GROUP * tk, tq), BF16),
                   pltpu.VMEM((2, 8, tq), F32), pltpu.VMEM((2, V_DIM, tq), F32)]
    else:
        scratch = [pltpu.VMEM((2, 1, tq), F32), pltpu.VMEM((2, 1, tq), F32), pltpu.VMEM((2, V_DIM, tq), F32)]
    return pl.pallas_call(
        kern,
        grid=(batch, HEADS, nq),
        in_specs=in_specs,
        out_specs=pl.BlockSpec((tq, V_DIM), lambda b, h, iq: (b * nq + iq, h)),
        out_shape=jax.ShapeDtypeStruct((nrows, HEADS * V_DIM), BF16),
        scratch_shapes=scratch,
        compiler_params=_cparams(("arbitrary", "arbitrary", "arbitrary")),
    )(*args)


def _halo_specs(width, *, row0, seq, tile):
    hb = tile // HALO

    def cur(b, i):
        return ((row0 + b * seq) // tile + i, 0)

    def prev(b, i):
        first = (row0 + b * seq) // HALO
        return (jnp.maximum(first + i * hb - 1, first), 0)

    def nxt(b, i):
        first = (row0 + b * seq) // HALO
        return (jnp.minimum(first + (i + 1) * hb, first + seq // HALO - 1), 0)

    return [pl.BlockSpec((HALO, width), prev), pl.BlockSpec((tile, width), cur), pl.BlockSpec((HALO, width), nxt)]


def _fill_ext(ext_ref, prev_ref, cur_ref, next_ref, *, seq, tile):
    pos0 = pl.program_id(1) * tile
    width = cur_ref.shape[1]
    hpos = lax.broadcasted_iota(jnp.int32, (HALO, width), 0)
    ext_ref[0:HALO, :] = jnp.where(pos0 - HALO + hpos >= 0, prev_ref[...].astype(F32), 0.0)
    ext_ref[HALO:HALO + tile, :] = cur_ref[...].astype(F32)
    ext_ref[HALO + tile:, :] = jnp.where(pos0 + tile + hpos < seq, next_ref[...].astype(F32), 0.0)


def _pool_kernel(prev_ref, cur_ref, next_ref, lin_ref, scale_ref, o_ref, ext_ref, s_ref, *, seq, tile):
    _fill_ext(ext_ref, prev_ref, cur_ref, next_ref, seq=seq, tile=tile)
    width = cur_ref.shape[1]
    gd = width // len(POOL_WINDOWS)
    n0 = tile + 2 * HALO
    s_ref[0, 0:n0 - 8, :] = ext_ref[0:n0 - 8, :] + ext_ref[1:n0 - 7, :]
    for k in range(1, len(POOL_WINDOWS)):
        step = 1 << k
        ln = n0 - 8 * (k + 1)
        s_ref[k, 0:ln, :] = s_ref[k - 1, 0:ln, :] + s_ref[k - 1, step:step + ln, :]
    lane = lax.broadcasted_iota(jnp.int32, (tile, width), 1)
    pos = pl.program_id(1) * tile + lax.broadcasted_iota(jnp.int32, (tile, width), 0)
    grp = lane // gd
    wsum = jnp.zeros((tile, width), F32)
    half = jnp.zeros((tile, width), jnp.int32)
    for k, win in enumerate(POOL_WINDOWS):
        start = HALO - win // 2
        wsum = jnp.where(grp == k, s_ref[k, start:start + tile, :], wsum)
        half = jnp.where(grp == k, win // 2, half)
    cnt = jnp.minimum(pos + half, seq) - jnp.maximum(pos - half, 0)
    z = ext_ref[HALO:HALO + tile, :]
    dlt = (wsum / cnt.astype(F32) - z).astype(BF16)
    o_ref[...] = (_dot(dlt, lin_ref[...]) * scale_ref[...]).astype(BF16)


def _pool(zp, lin_bd, scale, *, row0, batch, seq, out_rows, out_row0):
    width = zp.shape[1]
    tile = min(ROW_TILE, seq)
    kern = functools.partial(_pool_kernel, seq=seq, tile=tile)
    return pl.pallas_call(
        kern,
        grid=(batch, seq // tile),
        in_specs=_halo_specs(width, row0=row0, seq=seq, tile=tile) + [
            pl.BlockSpec((width, width), lambda b, i: (0, 0)),
            pl.BlockSpec((1, width), lambda b, i: (0, 0)),
        ],
        out_specs=pl.BlockSpec((tile, width), lambda b, i: ((out_row0 + b * seq) // tile + i, 0)),
        out_shape=jax.ShapeDtypeStruct((out_rows, width), BF16),
        scratch_shapes=[
            pltpu.VMEM((tile + 2 * HALO, width), F32),
            pltpu.VMEM((4, tile + 2 * HALO, width), F32),
        ],
        compiler_params=_cparams(("arbitrary", "arbitrary")),
    )(zp, zp, zp, lin_bd, scale)


def _hy_pre_kernel(prev_ref, cur_ref, next_ref, w_ref, b_ref, u_ref, x0_ref, ext_ref, *, seq, tile, hw):
    _fill_ext(ext_ref, prev_ref, cur_ref, next_ref, seq=seq, tile=tile)
    y = b_ref[...] + ext_ref[HALO - 1:HALO - 1 + tile, :] * w_ref[0:1, :]
    y = y + ext_ref[HALO:HALO + tile, :] * w_ref[1:2, :]
    y = y + ext_ref[HALO + 1:HALO + 1 + tile, :] * w_ref[2:3, :]
    x0_ref[...] = y[:, :hw]
    u_ref[...] = y[:, 2 * hw:] * y[:, hw:2 * hw]


def _hy_pre(zh, sw, sb, *, row0, batch, seq, out_rows, out_row0):
    width = zh.shape[1]
    hw = width // 3
    tile = min(ROW_TILE, seq)
    kern = functools.partial(_hy_pre_kernel, seq=seq, tile=tile, hw=hw)
    ospec = pl.BlockSpec((tile, hw), lambda b, i: ((out_row0 + b * seq) // tile + i, 0))
    return pl.pallas_call(
        kern,
        grid=(batch, seq // tile),
        in_specs=_halo_specs(width, row0=row0, seq=seq, tile=tile) + [
            pl.BlockSpec((3, width), lambda b, i: (0, 0)),
            pl.BlockSpec((1, width), lambda b, i: (0, 0)),
        ],
        out_specs=[ospec, ospec],
        out_shape=[jax.ShapeDtypeStruct((out_rows, hw), F32), jax.ShapeDtypeStruct((out_rows, hw), F32)],
        scratch_shapes=[pltpu.VMEM((tile + 2 * HALO, width), F32)],
        compiler_params=_cparams(("arbitrary", "arbitrary")),
    )(zh, zh, zh, sw, sb)


def _filter_kernel(feat_ref, w1_ref, b1_ref, f1_ref, w2_ref, b2_ref, f2_ref, w3_ref, dl_ref,
                   taps_ref, asum_ref, *, tile, hw, seq, embp):
    feat = feat_ref[...]
    h = jnp.sin(f1_ref[...] * (_dot3(feat, w1_ref[...]) + b1_ref[...]))
    h = jnp.sin(f2_ref[...] * (_dot3(h, w2_ref[...]) + b2_ref[...]))
    h = _dot3(h, w3_ref[...])
    odd = lax.broadcasted_iota(jnp.int32, (tile, 2 * hw), 1) >= hw
    tcol = jnp.where(odd, feat[:, embp:embp + 1], feat[:, 0:1])
    dec = jnp.exp(-tcol * dl_ref[...])
    row = pl.program_id(0) * tile + lax.broadcasted_iota(jnp.int32, (tile, 2 * hw), 0)
    pos = 2 * row + odd.astype(jnp.int32)
    taps = jnp.where(pos == seq, 0.0, h * dec)
    taps_ref[...] = taps
    part = jnp.sum(jnp.abs(taps), axis=0, keepdims=True)

    @pl.when(pl.program_id(0) == 0)
    def _():
        asum_ref[...] = part

    @pl.when(pl.program_id(0) != 0)
    def _():
        asum_ref[...] = asum_ref[...] + part


def _hy_filter(seq, w1, b1, f1, w2, b2, f2, w3, hw):
    ar = jnp.arange(seq, dtype=jnp.int32)
    pos = jnp.concatenate([ar, (seq - ar) % seq]).astype(F32)[:, None]
    t = pos * (1.0 / (seq - 1))
    w = (2.0 * math.pi / seq) * pos
    bands = jnp.linspace(1e-4, HY_BANDS - 1, HY_BANDS, dtype=F32)[None, :]
    emb = 1 + 2 * HY_BANDS
    embp = ((emb + 7) // 8) * 8
    feat = jnp.concatenate([t, jnp.cos(bands * w), -jnp.sin(bands * w), jnp.zeros((2 * seq, embp - emb), F32)], axis=-1)
    w1p = jnp.concatenate([w1, jnp.zeros((embp - emb, w1.shape[1]), F32)], axis=0)
    max_decay = math.log(1.0 / HY_TARGET) / HY_FAST
    min_decay = math.log(1.0 / HY_TARGET) / HY_SLOW
    deltas = jnp.linspace(min_decay, max_decay, hw, dtype=F32)[None, :]
    feat = feat.reshape(seq, 2 * embp)
    hid = w1.shape[1]

    def pair(m):
        z = jnp.zeros_like(m)
        return jnp.concatenate([jnp.concatenate([m, z], axis=1), jnp.concatenate([z, m], axis=1)], axis=0)

    twice = lambda v: jnp.concatenate([v, v])[None, :]
    w3p = jnp.concatenate([pair(w3[:, :hw]), pair(w3[:, hw:])], axis=1)
    tile = min(ROW_TILE, seq) // 2
    nfwd = seq // (2 * tile)
    full = lambda shape: pl.BlockSpec(shape, lambda i: (0,) * len(shape))
    taps, asum = pl.pallas_call(
        functools.partial(_filter_kernel, tile=tile, hw=hw, seq=seq, embp=embp),
        grid=(2 * nfwd,),
        in_specs=[
            pl.BlockSpec((tile, 2 * embp), lambda i: (i, 0)),
            full((2 * embp, 2 * hid)), full((1, 2 * hid)), full((1, 2 * hid)),
            full((2 * hid, 2 * hid)), full((1, 2 * hid)), full((1, 2 * hid)),
            pl.BlockSpec((2 * hid, 2 * hw), lambda i: (0, i // nfwd)), full((1, 2 * hw)),
        ],
        out_specs=[
            pl.BlockSpec((tile, 2 * hw), lambda i: (i, 0)),
            pl.BlockSpec((1, 2 * hw), lambda i: (0, 0)),
        ],
        out_shape=[
            jax.ShapeDtypeStruct((seq, 2 * hw), F32),
            jax.ShapeDtypeStruct((1, 2 * hw), F32),
        ],
        compiler_params=_cparams(("arbitrary",)),
    )(feat, pair(w1p), twice(b1), twice(f1), pair(w2), twice(b2), twice(f2), w3p,
      jnp.concatenate([deltas, deltas], axis=1))
    return taps.reshape(2 * seq, hw), asum[:, :hw] + asum[:, hw:]


def _np_split(a):
    a32 = jnp.asarray(a, F32)
    hi = a32.astype(BF16)
    lo = (a32 - hi.astype(F32)).astype(BF16)
    return hi, lo


def _dft_consts(n2):
    n1 = FFT_N1
    n = n1 * n2
    half = n1 // 2
    a1 = -2.0 * np.pi * np.outer(np.arange(n1), np.arange(n1)) / n1
    f1r, f1i = np.cos(a1), np.sin(a1)
    f1_data = np.block([[f1r[:, :half], -f1i[:, :half]], [f1i[:, :half], f1r[:, :half]]])
    f1_real = np.concatenate([f1r, f1i], axis=0)
    g1r, g1i = f1r / n, -f1i / n
    g1 = np.block([[g1r[:half], -g1i[:half]], [g1i[:half], g1r[:half]]])
    a2 = -2.0 * np.pi * np.outer(np.arange(n2), np.arange(n2)) / n2
    f2r, f2i = np.cos(a2), np.sin(a2)
    f2 = np.block([[f2r, -f2i], [f2i, f2r]])
    g2 = np.block([[f2r, f2i], [-f2i, f2r]])
    at = -2.0 * np.pi * np.outer(np.arange(n1), np.arange(n2)) / n
    twr = jnp.broadcast_to(jnp.asarray(np.cos(at), F32)[:, :, None], (n1, n2, LANES))
    twi = jnp.broadcast_to(jnp.asarray(np.sin(at), F32)[:, :, None], (n1, n2, LANES))
    return dict(f1_data=_np_split(f1_data), f1_real=_np_split(f1_real), g1=_np_split(g1),
                f2=_np_split(f2), g2=_np_split(g2), twr=twr, twi=twi)


def _fft1_kernel(z_ref, fh_ref, fl_ref, ar_ref, ai_ref):
    a = _dot3c(fh_ref[...], fl_ref[...], z_ref[...])
    ar_ref[...] = a[:FFT_N1]
    ai_ref[...] = a[FFT_N1:]


def _fft_first(zview, fmat, ncols):
    cb = min(FFT_COLS, ncols)
    fh, fl = fmat
    cspec = pl.BlockSpec(fh.shape, lambda j: (0, 0))
    ospec = pl.BlockSpec((FFT_N1, cb), lambda j: (0, j))
    return pl.pallas_call(
        _fft1_kernel,
        grid=(ncols // cb,),
        in_specs=[pl.BlockSpec((FFT_N1, cb), lambda j: (0, j)), cspec, cspec],
        out_specs=[ospec, ospec],
        out_shape=[jax.ShapeDtypeStruct((FFT_N1, ncols), F32)] * 2,
        compiler_params=_cparams(("arbitrary",)),
    )(zview, fh, fl)


def _fftmid_kernel(ar_ref, ai_ref, twr_ref, twi_ref, f2h_ref, f2l_ref, *rest, filter_only, n2):
    kb, _, width = ar_ref.shape
    reps = width // LANES
    lanes = lambda parts: jnp.concatenate(parts, axis=1)
    twr = lanes([t for j in range(kb) for t in [twr_ref[j]] * reps])
    twi = lanes([t for j in range(kb) for t in [twi_ref[j]] * reps])
    ar = lanes([ar_ref[j] for j in range(kb)])
    ai = lanes([ai_ref[j] for j in range(kb)])
    z = jnp.concatenate([ar * twr - ai * twi, ar * twi + ai * twr], axis=0)
    x = _dot3c(f2h_ref[...], f2l_ref[...], z)
    xr, xi = x[:n2], x[n2:]
    if filter_only:
        kr_out, ki_out = rest
        for j in range(kb):
            kr_out[j] = xr[:, j * width:(j + 1) * width]
            ki_out[j] = xi[:, j * width:(j + 1) * width]
        return
    kr_ref, ki_ref, g2h_ref, g2l_ref, br_out, bi_out = rest
    kr = lanes([kr_ref[j] for j in range(kb)])
    ki = lanes([ki_ref[j] for j in range(kb)])
    y = jnp.concatenate([xr * kr - xi * ki, xr * ki + xi * kr], axis=0)
    w = _dot3c(g2h_ref[...], g2l_ref[...], y)
    wr, wi = w[:n2], w[n2:]
    br = wr * twr + wi * twi
    bi = wi * twr - wr * twi
    for j in range(kb):
        br_out[j] = br[:, j * width:(j + 1) * width]
        bi_out[j] = bi[:, j * width:(j + 1) * width]


def _fft_mid(ar, ai, consts, n2, width, spectrum=None):
    a3r = ar.reshape(FFT_N1, n2, width)
    a3i = ai.reshape(FFT_N1, n2, width)
    blk = pl.BlockSpec((FFT_KB, n2, width), lambda k: (k, 0, 0))
    twspec = pl.BlockSpec((FFT_KB, n2, LANES), lambda k: (k, 0, 0))
    cspec = pl.BlockSpec((2 * n2, 2 * n2), lambda k: (0, 0))
    in_specs = [blk, blk, twspec, twspec, cspec, cspec]
    args = [a3r, a3i, consts["twr"], consts["twi"], *consts["f2"]]
    if spectrum is not None:
        in_specs += [blk, blk, cspec, cspec]
        args += [spectrum[0], spectrum[1], *consts["g2"]]
    return pl.pallas_call(
        functools.partial(_fftmid_kernel, filter_only=spectrum is None, n2=n2),
        grid=(FFT_N1 // FFT_KB,),
        in_specs=in_specs,
        out_specs=[blk, blk],
        out_shape=[jax.ShapeDtypeStruct((FFT_N1, n2, width), F32)] * 2,
        compiler_params=_cparams(("arbitrary",)),
    )(*args)


def _fftlast_kernel(br_ref, bi_ref, gh_ref, gl_ref, u_ref, x0_ref, bias_ref, invn_ref, o_ref):
    b = jnp.concatenate([br_ref[...], bi_ref[...]], axis=0)
    y = _dot3c(gh_ref[...], gl_ref[...], b)
    o_ref[...] = ((y * invn_ref[...] + u_ref[...] * bias_ref[...]) * x0_ref[...]).astype(BF16)


def _fft_last(br, bi, gmat, uview, x0view, bias_t, invn_t, ncols):
    cb = min(FFT_COLS, ncols)
    gh, gl = gmat
    cspec = pl.BlockSpec(gh.shape, lambda j: (0, 0))
    dspec = pl.BlockSpec((FFT_N1, cb), lambda j: (0, j))
    vspec = pl.BlockSpec((1, cb), lambda j: (0, 0))
    return pl.pallas_call(
        _fftlast_kernel,
        grid=(ncols // cb,),
        in_specs=[dspec, dspec, cspec, cspec, dspec, dspec, vspec, vspec],
        out_specs=dspec,
        out_shape=jax.ShapeDtypeStruct((FFT_N1, ncols), BF16),
        compiler_params=_cparams(("arbitrary",)),
    )(br, bi, gh, gl, uview, x0view, bias_t, invn_t)


def _dft_small_consts(seq):
    n = 2 * seq
    a = -2.0 * np.pi * np.outer(np.arange(n), np.arange(n)) / n
    fr, fi = np.cos(a), np.sin(a)
    f_data = np.block([[fr[:, :seq], -fi[:, :seq]], [fi[:, :seq], fr[:, :seq]]])
    f_real = np.concatenate([fr, fi], axis=0)
    gr, gi = fr / n, -fi / n
    g = np.block([[gr[:seq], -gi[:seq]], [gi[:seq], gr[:seq]]])
    return _np_split(f_data), _np_split(f_real), _np_split(g)


def _hy_ctx_kernel(u_ref, x0_ref, taps_ref, fdh_ref, fdl_ref, frh_ref, frl_ref, gh_ref, gl_ref,
                   bias_ref, invn_ref, o_ref):
    n = taps_ref.shape[0]
    z = u_ref[...]
    a = _dot3c(fdh_ref[...], fdl_ref[...], z)
    k = _dot3c(frh_ref[...], frl_ref[...], taps_ref[...])
    ar, ai, kr, ki = a[:n], a[n:], k[:n], k[n:]
    y = jnp.concatenate([ar * kr - ai * ki, ar * ki + ai * kr], axis=0)
    conv = _dot3c(gh_ref[...], gl_ref[...], y)
    o_ref[...] = ((conv * invn_ref[...] + z * bias_ref[...]) * x0_ref[...]).astype(BF16)


def _hy_ctx(u, x0, taps, bias, invn, small):
    rows, hw = u.shape
    full = lambda a: pl.BlockSpec(a.shape, lambda i: (0,) * a.ndim)
    args = [u, x0, taps, *small[0], *small[1], *small[2], bias, invn]
    return pl.pallas_call(
        _hy_ctx_kernel,
        grid=(1,),
        in_specs=[full(a) for a in args],
        out_specs=pl.BlockSpec((rows, hw), lambda i: (0, 0)),
        out_shape=jax.ShapeDtypeStruct((rows, hw), BF16),
        compiler_params=_cparams(("arbitrary",)),
    )(*args)


def _out_proj_kernel(x_ref, ypl_ref, yhl_ref, yal_ref, ypc_ref, yhc_ref, yac_ref, w_ref, g_ref, mod_ref, *rest,
                     nlt, tpb, ncond, pw, hw, moe, ntiles):
    if moe:
        rw_ref, xo_ref, u_ref, route_ref, un_ref = rest
    else:
        wg_ref, wu_ref, wd_ref, xo_ref = rest
    t = jnp.minimum(pl.program_id(0), ntiles - 1)
    if moe:
        @pl.when(pl.program_id(0) == 0)
        def _():
            un_ref[...] = jnp.zeros_like(un_ref)

        logits = _dot3(un_ref[...], rw_ref[...])
    ci = _cond_row(t, nlt, tpb, ncond)
    is_ctx = t >= nlt
    yp = jnp.where(is_ctx, ypc_ref[...], ypl_ref[...])
    yh = jnp.where(is_ctx, yhc_ref[...], yhl_ref[...])
    ya = jnp.where(is_ctx, yac_ref[...], yal_ref[...])
    mix = _dot(yp, w_ref[0:pw, :]) + _dot(yh, w_ref[pw:pw + hw, :]) + _dot(ya, w_ref[pw + hw:, :])
    x = x_ref[...] + mod_ref[2, pl.ds(ci, 1), :] * mix
    un = _normmod(x, g_ref[...], mod_ref[3, pl.ds(ci, 1), :], mod_ref[4, pl.ds(ci, 1), :])
    if not moe:
        u = un.astype(BF16)
        ff = wg_ref.shape[1]
        fc = ff // FFN_CHUNKS
        y = jnp.zeros(x.shape, F32)
        for c in range(FFN_CHUNKS):
            gate = _dot(u, wg_ref[:, c * fc:(c + 1) * fc])
            up = _dot(u, wu_ref[:, c * fc:(c + 1) * fc])
            y = y + _dot((_silu(gate) * up).astype(BF16), wd_ref[c * fc:(c + 1) * fc, :])
        xo_ref[...] = x + mod_ref[5, pl.ds(ci, 1), :] * y
    else:
        xo_ref[...] = x
        u_ref[...] = un.astype(BF16)
        un_ref[...] = un
        lane = lax.broadcasted_iota(jnp.int32, logits.shape, 1)
        neg = jnp.float32(-jnp.inf)
        lg = jnp.where(lane < N_EXPERTS, logits, neg)
        t1 = jnp.max(lg, axis=-1, keepdims=True)
        i1 = jnp.min(jnp.where(lg == t1, lane, LANES), axis=-1, keepdims=True)
        lg2 = jnp.where(lane == i1, neg, lg)
        t2 = jnp.max(lg2, axis=-1, keepdims=True)
        i2 = jnp.min(jnp.where(lg2 == t2, lane, LANES), axis=-1, keepdims=True)
        e2 = jnp.exp(t2 - t1)
        g1 = 1.0 / (1.0 + e2)
        g2 = e2 / (1.0 + e2)
        route_ref[...] = jnp.where(lane == 0, i1.astype(F32), jnp.where(lane == 1, i2.astype(F32),
                                   jnp.where(lane == 2, g1, jnp.where(lane == 3, g2, 0.0))))


def _out_proj(x, lat, ctx, w, g, mod, rw, ffn_w, *, nlt, tpb, ncond, out_rows):
    n, d = x.shape
    pw, hw, aw = (a.shape[1] for a in lat)
    tm = ROW_TILE
    moe = rw is not None
    nt = (n if moe else out_rows) // tm
    cur = lambda t: jnp.minimum(t, nt - 1)
    row = lambda width: pl.BlockSpec((tm, width), lambda t: (cur(t), 0))
    lrow = lambda width: pl.BlockSpec((tm, width), lambda t: (jnp.minimum(t, nlt - 1), 0))
    crow = lambda width: pl.BlockSpec((tm, width), lambda t: (jnp.maximum(cur(t) - nlt, 0), 0))
    in_specs = [row(d), lrow(pw), lrow(hw), lrow(aw), crow(pw), crow(hw), crow(aw),
                pl.BlockSpec(w.shape, lambda t: (0, 0)),
                pl.BlockSpec((1, d), lambda t: (0, 0)),
                pl.BlockSpec(mod.shape, lambda t: (0, 0, 0))]
    args = [x, *lat, *ctx, w, g, mod]
    scratch = []
    if not moe:
        const = lambda a: pl.BlockSpec(a.shape, lambda t: (0, 0), pipeline_mode=pl.Buffered(1))
        in_specs += [const(a) for a in ffn_w]
        args += list(ffn_w)
        out_specs = [row(d)]
        out_shape = [jax.ShapeDtypeStruct((out_rows, d), F32)]
    else:
        out_specs = [row(d), row(d)]
        out_shape = [jax.ShapeDtypeStruct((n, d), F32), jax.ShapeDtypeStruct((n, d), BF16)]
        in_specs.append(pl.BlockSpec(rw.shape, lambda t: (0, 0)))
        args.append(rw)
        out_specs.append(pl.BlockSpec((tm, LANES), lambda t: (jnp.maximum(t - 1, 0), 0)))
        out_shape.append(jax.ShapeDtypeStruct((n, LANES), F32))
        scratch.append(pltpu.VMEM((tm, d), F32))
    kern = functools.partial(_out_proj_kernel, nlt=nlt, tpb=tpb, ncond=ncond, pw=pw, hw=hw, moe=moe, ntiles=nt)
    return pl.pallas_call(
        kern, grid=(nt + 1 if moe else nt,), in_specs=in_specs, out_specs=out_specs, out_shape=out_shape,
        scratch_shapes=scratch,
        compiler_params=_cparams(("arbitrary",)),
    )(*args)


def _moe_kernel(te_ref, tv_ref, u_ref, wg_ref, wu_ref, wd_ref, *rest):
    o_ref, acc_ref = rest[-2:]
    t = pl.program_id(0)
    f = pl.program_id(1)

    @pl.when(tv_ref[t] > 0)
    def _():
        u = u_ref[...]
        h = (_silu(_dot(u, wg_ref[0, 0])) * _dot(u, wu_ref[0, 0])).astype(BF16)
        y = _dot(h, wd_ref[0, 0])

        @pl.when(f == 0)
        def _():
            acc_ref[...] = y

        @pl.when(f != 0)
        def _():
            acc_ref[...] = acc_ref[...] + y

        @pl.when(f == pl.num_programs(1) - 1)
        def _():
            o_ref[...] = acc_ref[...].astype(BF16)


def _moe_experts(ug, tile_expert, tile_valid, wg, wu, wd, j, *, tile0, total_rows, prev=None):
    p, d = ug.shape
    ff = wg.shape[3]
    tm, fc = MOE_TM, MOE_FC
    in_specs = [
        pl.BlockSpec((tm, d), lambda t, f, te, tv: (t, 0)),
        pl.BlockSpec((1, 1, d, fc), lambda t, f, te, tv: (j, te[t], 0, f)),
        pl.BlockSpec((1, 1, d, fc), lambda t, f, te, tv: (j, te[t], 0, f)),
        pl.BlockSpec((1, 1, fc, d), lambda t, f, te, tv: (j, te[t], f, 0)),
    ]
    args = [tile_expert, tile_valid, ug, wg, wu, wd]
    aliases = {}
    if prev is not None:
        in_specs.append(pl.BlockSpec(memory_space=pl.ANY))
        args.append(prev)
        aliases = {len(args) - 1: 0}
    grid_spec = pltpu.PrefetchScalarGridSpec(
        num_scalar_prefetch=2,
        grid=(p // tm, ff // fc),
        in_specs=in_specs,
        out_specs=pl.BlockSpec((tm, d), lambda t, f, te, tv: (t + tile0, 0)),
        scratch_shapes=[pltpu.VMEM((tm, d), F32)],
    )
    return pl.pallas_call(
        _moe_kernel, grid_spec=grid_spec,
        out_shape=jax.ShapeDtypeStruct((total_rows, d), BF16),
        input_output_aliases=aliases,
        compiler_params=_cparams(("arbitrary", "arbitrary")),
    )(*args)


def _moe_combine_kernel(x_ref, ya_ref, yb_ref, route_ref, mod_ref, o_ref, *, nlt, tpb, ncond):
    ci = _cond_row(pl.program_id(0), nlt, tpb, ncond)
    r = route_ref[...]
    lane = lax.broadcasted_iota(jnp.int32, r.shape, 1)
    g1 = jnp.sum(jnp.where(lane == 2, r, 0.0), axis=-1, keepdims=True)
    g2 = jnp.sum(jnp.where(lane == 3, r, 0.0), axis=-1, keepdims=True)
    y = g1 * ya_ref[...].astype(F32) + g2 * yb_ref[...].astype(F32)
    o_ref[...] = x_ref[...] + mod_ref[5, pl.ds(ci, 1), :] * y


def _moe_combine(x, ya, yb, route, mod, *, nlt, tpb, ncond, out_rows):
    d = x.shape[1]
    tm = ROW_TILE
    row = lambda width: pl.BlockSpec((tm, width), lambda t: (t, 0))
    kern = functools.partial(_moe_combine_kernel, nlt=nlt, tpb=tpb, ncond=ncond)
    return pl.pallas_call(
        kern,
        grid=(out_rows // tm,),
        in_specs=[row(d), row(d), row(d), row(LANES), pl.BlockSpec(mod.shape, lambda t: (0, 0, 0))],
        out_specs=row(d),
        out_shape=jax.ShapeDtypeStruct((out_rows, d), F32),
        compiler_params=_cparams(("arbitrary",)),
    )(x, ya, yb, route, mod)


def _moe_layer(x, u, route, mod, wg, wu, wd, j, *, nlt, tpb, ncond, out_rows):
    n, d = x.shape
    tm = MOE_TM
    experts = jnp.concatenate([route[:, 0], route[:, 1]]).astype(jnp.int32)
    onehot = (experts[:, None] == jnp.arange(N_EXPERTS, dtype=jnp.int32)[None, :]).astype(jnp.int32)
    rank = jnp.sum(onehot * (jnp.cumsum(onehot, axis=0) - 1), axis=1)
    counts = jnp.sum(onehot, axis=0)
    padded = ((counts + tm - 1) // tm) * tm
    ends = jnp.cumsum(padded)
    starts = ends - padded
    dest = starts[experts] + rank
    p = 2 * n + N_EXPERTS * tm
    tokens = jnp.concatenate([jnp.arange(n, dtype=jnp.int32)] * 2)
    inb = dict(mode="promise_in_bounds")
    src = jnp.zeros((p,), jnp.int32).at[dest].set(tokens, unique_indices=True, **inb)
    tile_start = jnp.arange(p // tm, dtype=jnp.int32) * tm
    tile_expert = jnp.minimum(jnp.searchsorted(ends, tile_start, side="right"), N_EXPERTS - 1).astype(jnp.int32)
    tile_valid = (tile_start < ends[-1]).astype(jnp.int32)
    nt0 = (p // tm) // 2
    ys = None
    for lo, hi in ((0, nt0), (nt0, p // tm)):
        ug = u.at[src[lo * tm:hi * tm]].get(**inb)
        ys = _moe_experts(ug, tile_expert[lo:hi], tile_valid[lo:hi], wg, wu, wd, j,
                          tile0=lo, total_rows=p, prev=ys)
    ya = ys.at[dest[:out_rows]].get(**inb)
    yb = ys.at[dest[n:n + out_rows]].get(**inb)
    return _moe_combine(x, ya, yb, route, mod, nlt=nlt, tpb=tpb, ncond=ncond, out_rows=out_rows)


def _rope_tables(seq, n_ctx_rows, batch):
    rows = seq // GRID_W
    row = jnp.repeat(jnp.arange(rows, dtype=F32), GRID_W)
    col = jnp.broadcast_to(jnp.arange(GRID_W, dtype=F32), (rows, GRID_W)).reshape(-1)
    inv_freq = jnp.power(ROPE_THETA, -jnp.arange(ROPE_FREQS, dtype=F32) / ROPE_FREQS)
    ar = row[:, None] * inv_freq
    ac = col[:, None] * inv_freq
    cos = jnp.concatenate([jnp.cos(ar), jnp.cos(ar), jnp.cos(ac), jnp.cos(ac)], axis=-1)
    sin = jnp.concatenate([-jnp.sin(ar), jnp.sin(ar), -jnp.sin(ac), jnp.sin(ac)], axis=-1)
    cos = jnp.tile(jnp.concatenate([cos, cos], axis=-1), (batch, 1))
    sin = jnp.tile(jnp.concatenate([sin, sin], axis=-1), (batch, 1))
    cos = jnp.concatenate([cos, jnp.ones((n_ctx_rows, 2 * HEAD_DIM), F32)], axis=0)
    sin = jnp.concatenate([sin, jnp.zeros((n_ctx_rows, 2 * HEAD_DIM), F32)], axis=0)
    return cos, sin


def kernel(x, c, ctx, c_ctx, mod_w, mod_b, norm1_g, norm2_g, w_in, w_out, pool_lin, pool_scale, hy_short_w, hy_short_b, hy_f_w1, hy_f_b1, hy_f_freq1, hy_f_w2, hy_f_b2, hy_f_freq2, hy_f_w3, hy_bias, qk_norm_g, diff_lambda, subln_g, ffn_w_gate, ffn_w_up, ffn_w_down, router_w, moe_w_gate, moe_w_up, moe_w_down):
    batch, seq, d = x.shape
    ctx_len = ctx.shape[1]
    depth = mod_w.shape[0]
    pw = pool_scale.shape[1]
    hw = hy_bias.shape[1]
    nl, nc = batch * seq, batch * ctx_len
    n = nl + nc
    tm = ROW_TILE
    assert seq % tm == 0 and nc % tm == 0 and seq % (FFT_N1 // 2) == 0 and seq % GRID_W == 0
    assert d == HEADS * 2 * V_DIM and pw == hw and batch == 2
    n2 = 2 * seq // FFT_N1
    ncols = n2 * hw
    assert (n * hw) % ncols == 0
    tiles = dict(nlt=nl // tm, tpb=seq // tm, ncond=batch + 1)

    xs = jnp.concatenate([x.reshape(nl, d), ctx.reshape(nc, d)], axis=0)
    cond8 = jnp.concatenate([c, c_ctx[None, :], jnp.zeros((8 - batch - 1, d), F32)], axis=0)
    mods = _modvec(cond8, mod_w, mod_b)
    cos_t, sin_t = _rope_tables(seq, nc, batch)
    consts = _dft_consts(n2)
    small = _dft_small_consts(ctx_len)
    eye = jnp.eye(len(POOL_WINDOWS), dtype=F32)
    moe_wg, moe_wu, moe_wd = _cast_bf16(moe_w_gate, moe_w_up, moe_w_down)

    for i in range(depth):
        last = i == depth - 1
        lam_init = 0.8 - 0.6 * math.exp(-0.3 * i)
        mod = mods[i]
        gqk = jnp.concatenate([qk_norm_g[i], qk_norm_g[i]], axis=-1)
        zp, zh, qt, k12, vt = _in_proj(xs, norm1_g[i][None, :], mod, w_in[i].astype(BF16), cos_t, sin_t, gqk,
                                       pw=pw, hw=hw, **tiles)

        g_b = jnp.broadcast_to(subln_g[i][:, None], (V_DIM, ATT_TQ))
        att_kw = dict(batch=batch, seq=seq, ctx_len=ctx_len, lam_init=lam_init)

        def attend(direct, lam_p=diff_lambda[i], qt=qt, k12=k12, vt=vt, g_b=g_b, att_kw=att_kw):
            return (_attention(lam_p, qt, k12, vt, g_b, latent=True, direct=direct, **att_kw),
                    _attention(lam_p, qt, k12, vt, g_b, latent=False, direct=direct, **att_kw))

        bound = (HEAD_DIM * QSCALE) * jnp.max(jnp.abs(qk_norm_g[i][0])) * jnp.max(jnp.abs(qk_norm_g[i][1]))
        ya_l, ya_c = lax.cond(bound * 1.02 < ATT_DIRECT_MAX, lambda: attend(True), lambda: attend(False))

        lin_bd = (eye[:, None, :, None] * pool_lin[i][:, :, None, :]).reshape(pw, pw).astype(BF16)
        pscale = pool_scale[i][None, :]
        yp_l = _pool(zp, lin_bd, pscale, row0=0, batch=batch, seq=seq, out_rows=nl, out_row0=0)
        yp_c = _pool(zp, lin_bd, pscale, row0=nl, batch=batch, seq=ctx_len, out_rows=nc, out_row0=0)

        sw, sb = hy_short_w[i], hy_short_b[i][None, :]
        u_l, x0_l = _hy_pre(zh, sw, sb, row0=0, batch=batch, seq=seq, out_rows=n, out_row0=0)
        u_c, x0_c = _hy_pre(zh, sw, sb, row0=nl, batch=batch, seq=ctx_len, out_rows=nc, out_row0=0)
        filt = (hy_f_w1[i], hy_f_b1[i], hy_f_freq1[i], hy_f_w2[i], hy_f_b2[i], hy_f_freq2[i], hy_f_w3[i])
        bias = hy_bias[i][None, :]
        taps, asum = _hy_filter(seq, *filt, hw)
        kr1, ki1 = _fft_first(taps.reshape(FFT_N1, ncols), consts["f1_real"], ncols)
        spectrum = _fft_mid(kr1, ki1, consts, n2, hw)
        ar, ai = _fft_first(u_l.reshape(-1, ncols), consts["f1_data"], ncols)
        br, bi = _fft_mid(ar, ai, consts, n2, hw, spectrum=spectrum)
        reps = min(FFT_COLS, ncols) // hw
        yh_lat = _fft_last(br.reshape(FFT_N1, ncols), bi.reshape(FFT_N1, ncols), consts["g1"],
                           u_l.reshape(-1, ncols), x0_l.reshape(-1, ncols),
                           jnp.tile(bias, (1, reps)), jnp.tile(1.0 / asum, (1, reps)), ncols)
        taps_c, asum_c = _hy_filter(ctx_len, *filt, hw)
        yh_ctx = _hy_ctx(u_c, x0_c, taps_c, bias, 1.0 / asum_c, small)

        j = i // 2
        moe = i % 2 == 1
        rw = ffn_w = None
        if moe:
            rw = jnp.concatenate([router_w[j], jnp.zeros((d, LANES - N_EXPERTS), F32)], axis=1)
        else:
            ffn_w = (ffn_w_gate[j].astype(BF16), ffn_w_up[j].astype(BF16), ffn_w_down[j].astype(BF16))
        out_rows = nl if last else n
        outs = _out_proj(xs, (yp_l, yh_lat.reshape(nl, hw), ya_l), (yp_c, yh_ctx, ya_c), w_out[i].astype(BF16),
                         norm2_g[i][None, :], mod, rw, ffn_w, out_rows=out_rows, **tiles)
        if moe:
            xs, u, route = outs
            xs = _moe_layer(xs, u, route, mod, moe_wg, moe_wu, moe_wd, j, out_rows=out_rows, **tiles)
        else:
            xs, = outs
    return xs[:nl].reshape(batch, seq, d)
```

```python
import functools
import math

import numpy as np
import jax
import jax.numpy as jnp
from jax import lax
from jax.experimental import pallas as pl
from jax.experimental.pallas import tpu as pltpu

F32 = jnp.float32
BF16 = jnp.bfloat16
EPS = 1e-6

GRID_W = 64
POOL_WINDOWS = (2, 4, 8, 16)
HEADS = 4
HEAD_DIM = 64
V_DIM = 128
ROPE_FREQS = 16
ROPE_THETA = 10000.0
HY_BANDS = 16
HY_TARGET = 1e-2
HY_FAST = 0.3
HY_SLOW = 1.5
N_EXPERTS = 8

LANES = 128
ROW_TILE = 512
HALO = 32
FFT_N1 = 128
FFT_COLS = 2048
FFT_KB = 4
ATT_TQ = 1024
ATT_TK = 512
ATT_UNROLL = 16
ATT_GROUP = 8
QSCALE = (HEAD_DIM ** -0.5) * math.log2(math.e)
ATT_DIRECT_MAX = 100.0
FFN_CHUNKS = 2
CAST_STEPS = 64
MOE_TM = 512
MOE_FC = 1792
VMEM_LIMIT = 56 * 1024 * 1024


def _cparams(sem):
    return pltpu.CompilerParams(dimension_semantics=sem, vmem_limit_bytes=VMEM_LIMIT)


def _dot(a, b):
    return jnp.dot(a, b, preferred_element_type=F32)


def _split(a):
    hi = a.astype(BF16)
    lo = (a - hi.astype(F32)).astype(BF16)
    return hi, lo


def _dot3(a, b):
    ah, al = _split(a)
    bh, bl = _split(b)
    return _dot(ah, bh) + _dot(ah, bl) + _dot(al, bh)


def _dot3c(ch, cl, d):
    dh, dl = _split(d)
    return _dot(ch, dh) + _dot(ch, dl) + _dot(cl, dh)


def _silu(x):
    return x / (1.0 + jnp.exp(-x))


def _normmod(x, g, shift, scale):
    ms = jnp.mean(x * x, axis=-1, keepdims=True)
    return (x * lax.rsqrt(ms + EPS) * g) * (1.0 + scale) + shift


def _cond_row(t, n_lat_tiles, tiles_per_batch, n_cond):
    return jnp.where(t >= n_lat_tiles, n_cond - 1, t // tiles_per_batch)


def _cast_kernel(*refs):
    half = len(refs) // 2
    for w_ref, o_ref in zip(refs[:half], refs[half:]):
        o_ref[...] = w_ref[...].astype(BF16)


def _cast_bf16(*ws):
    flat = [w.reshape(-1, w.shape[-1]) for w in ws]
    steps = CAST_STEPS
    specs = [pl.BlockSpec((w.shape[0] // steps, w.shape[1]), lambda t: (t, 0)) for w in flat]
    outs = pl.pallas_call(
        _cast_kernel,
        grid=(steps,),
        in_specs=specs,
        out_specs=specs,
        out_shape=[jax.ShapeDtypeStruct(w.shape, BF16) for w in flat],
        compiler_params=_cparams(("arbitrary",)),
    )(*flat)
    return [o.reshape(w.shape) for o, w in zip(outs, ws)]


def _modvec_kernel(c_ref, w_ref, b_ref, o_ref):
    s = _silu(c_ref[...])
    o_ref[0, 0] = _dot3(s, w_ref[0]) + b_ref[0]


def _modvec(cond8, mod_w, mod_b):
    depth, d, six_d = mod_w.shape
    nchunk = six_d // d
    return pl.pallas_call(
        _modvec_kernel,
        grid=(depth, nchunk),
        in_specs=[
            pl.BlockSpec((8, d), lambda i, j: (0, 0)),
            pl.BlockSpec((1, d, d), lambda i, j: (i, 0, j)),
            pl.BlockSpec((1, 1, d), lambda i, j: (i, 0, j)),
        ],
        out_specs=pl.BlockSpec((1, 1, 8, d), lambda i, j: (i, j, 0, 0)),
        out_shape=jax.ShapeDtypeStruct((depth, nchunk, 8, d), F32),
        compiler_params=_cparams(("arbitrary", "arbitrary")),
    )(cond8, mod_w, mod_b.reshape(depth, 1, six_d))


def _in_proj_kernel(x_ref, g_ref, mod_ref, w_ref, cos_ref, sin_ref, gqk_ref,
                    zp_ref, zh_ref, qt_ref, k_ref, vt_ref, zatt_ref, *, nlt, tpb, ncond, pw, hw, ntiles):
    t = pl.program_id(0)

    @pl.when(t == 0)
    def _():
        zatt_ref[...] = jnp.zeros_like(zatt_ref)

    tm = x_ref.shape[0]
    att = pw + 3 * hw
    lane = lax.broadcasted_iota(jnp.int32, (tm, 2 * HEAD_DIM), 1)
    first = lane < HEAD_DIM
    apart = (lane % (2 * ROPE_FREQS)) < ROPE_FREQS
    cos = cos_ref[...]
    sin = sin_ref[...]
    qk_w = HEADS * 2 * HEAD_DIM

    def norm_rope(v, g):
        v2 = v * v
        s_all = jnp.sum(v2, axis=-1, keepdims=True)
        s_first = jnp.sum(jnp.where(first, v2, 0.0), axis=-1, keepdims=True)
        ms = jnp.where(first, s_first, s_all - s_first) * (1.0 / HEAD_DIM)
        vn = v * lax.rsqrt(ms + EPS) * g
        swapped = jnp.where(apart, pltpu.roll(vn, 2 * HEAD_DIM - ROPE_FREQS, 1), pltpu.roll(vn, ROPE_FREQS, 1))
        return vn * cos + swapped * sin

    for h in range(HEADS):
        lo = h * 2 * HEAD_DIM
        q = norm_rope(zatt_ref[:, lo:lo + 2 * HEAD_DIM], gqk_ref[0:1, :])
        qt_ref[h] = (q * QSCALE).T.astype(BF16)
        k = norm_rope(zatt_ref[:, qk_w + lo:qk_w + lo + 2 * HEAD_DIM], gqk_ref[1:2, :])
        k_ref[h] = k.astype(BF16)
        vlo = 2 * qk_w + h * V_DIM
        vt_ref[h] = zatt_ref[:, vlo:vlo + V_DIM].T.astype(BF16)

    ci = _cond_row(jnp.minimum(t, ntiles - 1), nlt, tpb, ncond)
    shift = mod_ref[0, pl.ds(ci, 1), :]
    scale = mod_ref[1, pl.ds(ci, 1), :]
    y = _normmod(x_ref[...], g_ref[...], shift, scale).astype(BF16)
    z = _dot(y, w_ref[...])
    zp_ref[...] = z[:, :pw].astype(BF16)
    zh_ref[...] = z[:, pw:att].astype(BF16)
    zatt_ref[...] = z[:, att:]


def _in_proj(x, g, mod, w, cos_t, sin_t, gqk, *, nlt, tpb, ncond, pw, hw):
    n, d = x.shape
    wid = w.shape[1]
    tm = ROW_TILE
    nt = n // tm
    kern = functools.partial(_in_proj_kernel, nlt=nlt, tpb=tpb, ncond=ncond, pw=pw, hw=hw, ntiles=nt)
    cur = lambda t: jnp.minimum(t, nt - 1)
    prv = lambda t: jnp.maximum(t - 1, 0)
    return pl.pallas_call(
        kern,
        grid=(nt + 1,),
        in_specs=[
            pl.BlockSpec((tm, d), lambda t: (cur(t), 0)),
            pl.BlockSpec((1, d), lambda t: (0, 0)),
            pl.BlockSpec(mod.shape, lambda t: (0, 0, 0)),
            pl.BlockSpec((d, wid), lambda t: (0, 0)),
            pl.BlockSpec((tm, 2 * HEAD_DIM), lambda t: (prv(t), 0)),
            pl.BlockSpec((tm, 2 * HEAD_DIM), lambda t: (prv(t), 0)),
            pl.BlockSpec((2, 2 * HEAD_DIM), lambda t: (0, 0)),
        ],
        out_specs=[
            pl.BlockSpec((tm, pw), lambda t: (cur(t), 0)),
            pl.BlockSpec((tm, 3 * hw), lambda t: (cur(t), 0)),
            pl.BlockSpec((HEADS, 2 * HEAD_DIM, tm), lambda t: (0, 0, prv(t))),
            pl.BlockSpec((HEADS, tm, 2 * HEAD_DIM), lambda t: (0, prv(t), 0)),
            pl.BlockSpec((HEADS, V_DIM, tm), lambda t: (0, 0, prv(t))),
        ],
        scratch_shapes=[pltpu.VMEM((tm, wid - pw - 3 * hw), F32)],
        out_shape=[
            jax.ShapeDtypeStruct((n, pw), BF16),
            jax.ShapeDtypeStruct((n, 3 * hw), BF16),
            jax.ShapeDtypeStruct((HEADS, 2 * HEAD_DIM, n), BF16),
            jax.ShapeDtypeStruct((HEADS, n, 2 * HEAD_DIM), BF16),
            jax.ShapeDtypeStruct((HEADS, V_DIM, n), BF16),
        ],
        compiler_params=_cparams(("arbitrary",)),
    )(x, g, mod, w, cos_t, sin_t, gqk)


def _attn_kernel(lam_ref, qt_ref, kc_ref, vtc_ref, *rest, tk, n_lat_chunks, lam_init):
    if n_lat_chunks:
        kl_ref, vtl_ref, g_ref, o_ref, m_ref, l_ref, acc_ref = rest
    else:
        g_ref, o_ref, m_ref, l_ref, acc_ref = rest
    qt = qt_ref[0]
    row = lax.broadcasted_iota(jnp.int32, qt.shape, 0)
    zero = jnp.zeros_like(qt)
    qmaps = (jnp.where(row < HEAD_DIM, qt, zero), jnp.where(row >= HEAD_DIM, qt, zero))

    def process(k_tile, vt_tile, first):
        for mi in range(2):
            s = _dot(k_tile, qmaps[mi])
            smax = jnp.max(s, axis=0, keepdims=True)
            if first:
                m_new = smax
            else:
                m_old = m_ref[mi]
                m_new = jnp.maximum(m_old, smax)
            p = jnp.exp2(s - m_new)
            psum = jnp.sum(p, axis=0, keepdims=True)
            pv = _dot(vt_tile, p.astype(BF16))
            if first:
                l_ref[mi] = psum
                acc_ref[mi] = pv
            else:
                alpha = jnp.exp2(m_old - m_new)
                l_ref[mi] = alpha * l_ref[mi] + psum
                acc_ref[mi] = alpha * acc_ref[mi] + pv
            m_ref[mi] = m_new

    process(kc_ref[0], vtc_ref[0], True)
    if n_lat_chunks:
        def body(j, carry):
            off = pl.multiple_of(j * tk, tk)
            process(kl_ref[0, pl.ds(off, tk), :], vtl_ref[0, :, pl.ds(off, tk)], False)
            return carry
        lax.fori_loop(0, n_lat_chunks, body, 0)

    lv = lam_ref[...]
    lam = (jnp.exp(jnp.sum(lv[0:1] * lv[1:2], axis=-1, keepdims=True))
           - jnp.exp(jnp.sum(lv[2:3] * lv[3:4], axis=-1, keepdims=True)) + lam_init)
    o = acc_ref[0] * (1.0 / l_ref[0]) - lam * (acc_ref[1] * (1.0 / l_ref[1]))
    ms = jnp.mean(o * o, axis=0, keepdims=True)
    y = o * lax.rsqrt(ms + EPS) * g_ref[...] * (1.0 - lam_init)
    o_ref[...] = y.T.astype(BF16)


def _attn_direct_kernel(lam_ref, qt_ref, kc_ref, vtc_ref, *rest, tk, n_lat_chunks, lam_init, n_cast=0):
    if n_lat_chunks:
        kl_ref, vtl_ref, g_ref = rest[:3]
        rest = rest[3:]
    else:
        g_ref = rest[0]
        rest = rest[1:]
    cast_in, (o_ref, *cast_out), (p_ref, l_ref, acc_ref) = rest[:n_cast], rest[n_cast:2 * n_cast + 1], rest[2 * n_cast + 1:]
    for w_ref, c_ref in zip(cast_in, cast_out):
        c_ref[...] = w_ref[...].astype(BF16)
    qt = qt_ref[0]
    tq = qt.shape[1]
    row = lax.broadcasted_iota(jnp.int32, qt.shape, 0)
    zero = jnp.zeros_like(qt)
    qmaps = (jnp.where(row < HEAD_DIM, qt, zero), jnp.where(row >= HEAD_DIM, qt, zero))
    grp = ATT_GROUP

    def keys(first_piece, count):
        if isinstance(first_piece, int):
            return pl.ds((first_piece - 1) * tk, count * tk)
        return pl.ds(pl.multiple_of((first_piece - 1) * tk, tk), count * tk)

    def stage_ab(k_tile, pset, r):
        for mi in range(2):
            p = jnp.exp2(_dot(k_tile, qmaps[mi]))
            l_ref[mi] = l_ref[mi] + jnp.sum(p.reshape(tk // 8, 8, tq), axis=0)
            p_ref[pset, mi, r * tk:(r + 1) * tk, :] = p.astype(BF16)

    def stage_c(vt_tile, pset, lo, count):
        for mi in range(2):
            acc_ref[mi] = acc_ref[mi] + _dot(vt_tile, p_ref[pset, mi, lo * tk:(lo + count) * tk, :])

    def group_ab(g, pset):
        k_tile = kl_ref[0, keys(grp * g, grp), :]
        for mi in range(2):
            p = jnp.exp2(_dot(k_tile, qmaps[mi]))
            l_ref[mi] = l_ref[mi] + jnp.sum(p.reshape(grp * tk // 8, 8, tq), axis=0)
            p_ref[pset, mi] = p.astype(BF16)

    def group_c(g, pset):
        stage_c(vtl_ref[0, :, keys(grp * g, grp)], pset, 0, grp)

    l_ref[...] = jnp.zeros_like(l_ref)
    acc_ref[...] = jnp.zeros_like(acc_ref)
    stage_ab(kc_ref[0], 0, 0)
    if n_lat_chunks == 0:
        stage_c(vtc_ref[0], 0, 0, 1)
    else:
        n_groups = n_lat_chunks // grp
        per_step = ATT_UNROLL // grp
        for r in range(1, grp):
            stage_ab(kl_ref[0, keys(r, 1), :], 0, r)
        group_ab(1, 1)
        stage_c(vtc_ref[0], 0, 0, 1)
        stage_c(vtl_ref[0, :, keys(1, grp - 1)], 0, 1, grp - 1)
        for g in range(2, per_step):
            group_ab(g, g % 2)
            group_c(g - 1, (g - 1) % 2)

        def body(jj, carry):
            g0 = per_step * (jj + 1)
            for q in range(per_step):
                group_ab(g0 + q, q % 2)
                group_c(g0 + q - 1, (q + 1) % 2)
            return carry

        lax.fori_loop(0, n_groups // per_step - 1, body, 0)
        stage_ab(kl_ref[0, keys(n_lat_chunks, 1), :], 0, 0)
        group_c(n_groups - 1, 1)
        stage_c(vtl_ref[0, :, keys(n_lat_chunks, 1)], 0, 0, 1)

    lv = lam_ref[...]
    lam = (jnp.exp(jnp.sum(lv[0:1] * lv[1:2], axis=-1, keepdims=True))
           - jnp.exp(jnp.sum(lv[2:3] * lv[3:4], axis=-1, keepdims=True)) + lam_init)
    l1 = jnp.sum(l_ref[0], axis=0, keepdims=True)
    l2 = jnp.sum(l_ref[1], axis=0, keepdims=True)
    o = acc_ref[0] * (1.0 / l1) - lam * (acc_ref[1] * (1.0 / l2))
    ms = jnp.mean(o * o, axis=0, keepdims=True)
    y = o * lax.rsqrt(ms + EPS) * g_ref[...] * (1.0 - lam_init)
    o_ref[...] = y.T.astype(BF16)


def _attention(lam_p, qt, k12, vt, g_b, *, batch, seq, ctx_len, latent, lam_init, direct, cast=()):
    n = k12.shape[1]
    nl = batch * seq
    tq = ATT_TQ if latent else ctx_len
    nq = seq // tq if latent else 1
    tk = ctx_len if direct else ATT_TK
    assert seq % ((ATT_UNROLL if direct else 1) * tk) == 0
    qbase = 0 if latent else nl // tq

    def qrow(b, iq):
        return (b * nq + iq) if latent else (qbase + b)

    in_specs = [
        pl.BlockSpec(lam_p.shape, lambda b, h, iq: (0, 0)),
        pl.BlockSpec((1, 2 * HEAD_DIM, tq), lambda b, h, iq: (h, 0, qrow(b, iq))),
        pl.BlockSpec((1, ctx_len, 2 * HEAD_DIM), lambda b, h, iq: (h, nl // ctx_len + b, 0)),
        pl.BlockSpec((1, V_DIM, ctx_len), lambda b, h, iq: (h, 0, nl // ctx_len + b)),
    ]
    args = [lam_p, qt, k12, vt]
    if latent:
        in_specs += [
            pl.BlockSpec((1, seq, 2 * HEAD_DIM), lambda b, h, iq: (h, b, 0)),
            pl.BlockSpec((1, V_DIM, seq), lambda b, h, iq: (h, 0, b)),
        ]
        args += [k12, vt]
    in_specs.append(pl.BlockSpec((V_DIM, tq), lambda b, h, iq: (0, 0)))
    args.append(g_b[:, :tq])
    nrows = nl if latent else batch * ctx_len
    body = _attn_direct_kernel if direct else _attn_kernel
    kw = dict(tk=tk, n_lat_chunks=(seq // tk if latent else 0), lam_init=lam_init)
    out_specs = [pl.BlockSpec((tq, V_DIM), lambda b, h, iq: (b * nq + iq, h))]
    out_shape = [jax.ShapeDtypeStruct((nrows, HEADS * V_DIM), BF16)]
    if cast:
        assert direct
        kw["n_cast"] = len(cast)
        steps = batch * HEADS * nq
        for w, slab, parts in cast:
            rows = w.shape[0] // parts // steps
            in_specs.append(pl.BlockSpec(
                (rows, w.shape[1]), lambda b, h, iq, slab=slab: (slab * steps + (b * HEADS + h) * nq + iq, 0)))
            args.append(w)
            out_specs.append(pl.BlockSpec((rows, w.shape[1]), lambda b, h, iq: ((b * HEADS + h) * nq + iq, 0)))
            out_shape.append(jax.ShapeDtypeStruct((w.shape[0] // parts, w.shape[1]), BF16))
    kern = functools.partial(body, **kw)
    if direct:
        scratch = [pltpu.VMEM((2, 2, ATT_GROUP * tk, tq), BF16),
                   pltpu.VMEM((2, 8, tq), F32), pltpu.VMEM((2, V_DIM, tq), F32)]
    else:
        scratch = [pltpu.VMEM((2, 1, tq), F32), pltpu.VMEM((2, 1, tq), F32), pltpu.VMEM((2, V_DIM, tq), F32)]
    outs = pl.pallas_call(
        kern,
        grid=(batch, HEADS, nq),
        in_specs=in_specs,
        out_specs=out_specs,
        out_shape=out_shape,
        scratch_shapes=scratch,
        compiler_params=_cparams(("arbitrary", "arbitrary", "arbitrary")),
    )(*args)
    return outs if cast else outs[0]


def _halo_specs(width, *, row0, seq, tile):
    hb = tile // HALO

    def cur(b, i):
        return ((row0 + b * seq) // tile + i, 0)

    def prev(b, i):
        first = (row0 + b * seq) // HALO
        return (jnp.maximum(first + i * hb - 1, first), 0)

    def nxt(b, i):
        first = (row0 + b * seq) // HALO
        return (jnp.minimum(first + (i + 1) * hb, first + seq // HALO - 1), 0)

    return [pl.BlockSpec((HALO, width), prev), pl.BlockSpec((tile, width), cur), pl.BlockSpec((HALO, width), nxt)]


def _fill_ext(ext_ref, prev_ref, cur_ref, next_ref, *, seq, tile):
    pos0 = pl.program_id(1) * tile
    width = cur_ref.shape[1]
    hpos = lax.broadcasted_iota(jnp.int32, (HALO, width), 0)
    ext_ref[0:HALO, :] = jnp.where(pos0 - HALO + hpos >= 0, prev_ref[...].astype(F32), 0.0)
    ext_ref[HALO:HALO + tile, :] = cur_ref[...].astype(F32)
    ext_ref[HALO + tile:, :] = jnp.where(pos0 + tile + hpos < seq, next_ref[...].astype(F32), 0.0)


def _pool_kernel(prev_ref, cur_ref, next_ref, lin_ref, scale_ref, o_ref, ext_ref, s_ref, *, seq, tile):
    _fill_ext(ext_ref, prev_ref, cur_ref, next_ref, seq=seq, tile=tile)
    width = cur_ref.shape[1]
    gd = width // len(POOL_WINDOWS)
    n0 = tile + 2 * HALO
    s_ref[0, 0:n0 - 8, :] = ext_ref[0:n0 - 8, :] + ext_ref[1:n0 - 7, :]
    for k in range(1, len(POOL_WINDOWS)):
        step = 1 << k
        ln = n0 - 8 * (k + 1)
        s_ref[k, 0:ln, :] = s_ref[k - 1, 0:ln, :] + s_ref[k - 1, step:step + ln, :]
    lane = lax.broadcasted_iota(jnp.int32, (tile, width), 1)
    pos = pl.program_id(1) * tile + lax.broadcasted_iota(jnp.int32, (tile, width), 0)
    grp = lane // gd
    wsum = jnp.zeros((tile, width), F32)
    half = jnp.zeros((tile, width), jnp.int32)
    for k, win in enumerate(POOL_WINDOWS):
        start = HALO - win // 2
        wsum = jnp.where(grp == k, s_ref[k, start:start + tile, :], wsum)
        half = jnp.where(grp == k, win // 2, half)
    cnt = jnp.minimum(pos + half, seq) - jnp.maximum(pos - half, 0)
    z = ext_ref[HALO:HALO + tile, :]
    dlt = (wsum / cnt.astype(F32) - z).astype(BF16)
    o_ref[...] = (_dot(dlt, lin_ref[...]) * scale_ref[...]).astype(BF16)


def _pool(zp, lin_bd, scale, *, row0, batch, seq, out_rows, out_row0):
    width = zp.shape[1]
    tile = min(ROW_TILE, seq)
    kern = functools.partial(_pool_kernel, seq=seq, tile=tile)
    return pl.pallas_call(
        kern,
        grid=(batch, seq // tile),
        in_specs=_halo_specs(width, row0=row0, seq=seq, tile=tile) + [
            pl.BlockSpec((width, width), lambda b, i: (0, 0)),
            pl.BlockSpec((1, width), lambda b, i: (0, 0)),
        ],
        out_specs=pl.BlockSpec((tile, width), lambda b, i: ((out_row0 + b * seq) // tile + i, 0)),
        out_shape=jax.ShapeDtypeStruct((out_rows, width), BF16),
        scratch_shapes=[
            pltpu.VMEM((tile + 2 * HALO, width), F32),
            pltpu.VMEM((4, tile + 2 * HALO, width), F32),
        ],
        compiler_params=_cparams(("arbitrary", "arbitrary")),
    )(zp, zp, zp, lin_bd, scale)


def _hy_pre_kernel(prev_ref, cur_ref, next_ref, w_ref, b_ref, u_ref, x0_ref, ext_ref, *, seq, tile, hw):
    _fill_ext(ext_ref, prev_ref, cur_ref, next_ref, seq=seq, tile=tile)
    y = b_ref[...] + ext_ref[HALO - 1:HALO - 1 + tile, :] * w_ref[0:1, :]
    y = y + ext_ref[HALO:HALO + tile, :] * w_ref[1:2, :]
    y = y + ext_ref[HALO + 1:HALO + 1 + tile, :] * w_ref[2:3, :]
    x0_ref[...] = y[:, :hw]
    u_ref[...] = y[:, 2 * hw:] * y[:, hw:2 * hw]


def _hy_pre(zh, sw, sb, *, row0, batch, seq, out_rows, out_row0):
    width = zh.shape[1]
    hw = width // 3
    tile = min(ROW_TILE, seq)
    kern = functools.partial(_hy_pre_kernel, seq=seq, tile=tile, hw=hw)
    ospec = pl.BlockSpec((tile, hw), lambda b, i: ((out_row0 + b * seq) // tile + i, 0))
    return pl.pallas_call(
        kern,
        grid=(batch, seq // tile),
        in_specs=_halo_specs(width, row0=row0, seq=seq, tile=tile) + [
            pl.BlockSpec((3, width), lambda b, i: (0, 0)),
            pl.BlockSpec((1, width), lambda b, i: (0, 0)),
        ],
        out_specs=[ospec, ospec],
        out_shape=[jax.ShapeDtypeStruct((out_rows, hw), F32), jax.ShapeDtypeStruct((out_rows, hw), F32)],
        scratch_shapes=[pltpu.VMEM((tile + 2 * HALO, width), F32)],
        compiler_params=_cparams(("arbitrary", "arbitrary")),
    )(zh, zh, zh, sw, sb)


def _filter_kernel(feat_ref, w1_ref, b1_ref, f1_ref, w2_ref, b2_ref, f2_ref, w3_ref, dl_ref,
                   taps_ref, asum_ref, *, tile, hw, seq, embp):
    feat = feat_ref[...]
    h = jnp.sin(f1_ref[...] * (_dot3(feat, w1_ref[...]) + b1_ref[...]))
    h = jnp.sin(f2_ref[...] * (_dot3(h, w2_ref[...]) + b2_ref[...]))
    h = _dot3(h, w3_ref[...])
    odd = lax.broadcasted_iota(jnp.int32, (tile, 2 * hw), 1) >= hw
    tcol = jnp.where(odd, feat[:, embp:embp + 1], feat[:, 0:1])
    dec = jnp.exp(-tcol * dl_ref[...])
    row = pl.program_id(0) * tile + lax.broadcasted_iota(jnp.int32, (tile, 2 * hw), 0)
    pos = 2 * row + odd.astype(jnp.int32)
    taps = jnp.where(pos == seq, 0.0, h * dec)
    taps_ref[...] = taps
    part = jnp.sum(jnp.abs(taps), axis=0, keepdims=True)

    @pl.when(pl.program_id(0) == 0)
    def _():
        asum_ref[...] = part

    @pl.when(pl.program_id(0) != 0)
    def _():
        asum_ref[...] = asum_ref[...] + part


def _hy_filter(seq, w1, b1, f1, w2, b2, f2, w3, hw):
    ar = jnp.arange(seq, dtype=jnp.int32)
    pos = jnp.concatenate([ar, (seq - ar) % seq]).astype(F32)[:, None]
    t = pos * (1.0 / (seq - 1))
    w = (2.0 * math.pi / seq) * pos
    bands = jnp.linspace(1e-4, HY_BANDS - 1, HY_BANDS, dtype=F32)[None, :]
    emb = 1 + 2 * HY_BANDS
    embp = ((emb + 7) // 8) * 8
    feat = jnp.concatenate([t, jnp.cos(bands * w), -jnp.sin(bands * w), jnp.zeros((2 * seq, embp - emb), F32)], axis=-1)
    w1p = jnp.concatenate([w1, jnp.zeros((embp - emb, w1.shape[1]), F32)], axis=0)
    max_decay = math.log(1.0 / HY_TARGET) / HY_FAST
    min_decay = math.log(1.0 / HY_TARGET) / HY_SLOW
    deltas = jnp.linspace(min_decay, max_decay, hw, dtype=F32)[None, :]
    feat = feat.reshape(seq, 2 * embp)
    hid = w1.shape[1]

    def pair(m):
        z = jnp.zeros_like(m)
        return jnp.concatenate([jnp.concatenate([m, z], axis=1), jnp.concatenate([z, m], axis=1)], axis=0)

    twice = lambda v: jnp.concatenate([v, v])[None, :]
    w3p = jnp.concatenate([pair(w3[:, :hw]), pair(w3[:, hw:])], axis=1)
    tile = min(ROW_TILE, seq) // 2
    nfwd = seq // (2 * tile)
    full = lambda shape: pl.BlockSpec(shape, lambda i: (0,) * len(shape))
    taps, asum = pl.pallas_call(
        functools.partial(_filter_kernel, tile=tile, hw=hw, seq=seq, embp=embp),
        grid=(2 * nfwd,),
        in_specs=[
            pl.BlockSpec((tile, 2 * embp), lambda i: (i, 0)),
            full((2 * embp, 2 * hid)), full((1, 2 * hid)), full((1, 2 * hid)),
            full((2 * hid, 2 * hid)), full((1, 2 * hid)), full((1, 2 * hid)),
            pl.BlockSpec((2 * hid, 2 * hw), lambda i: (0, i // nfwd)), full((1, 2 * hw)),
        ],
        out_specs=[
            pl.BlockSpec((tile, 2 * hw), lambda i: (i, 0)),
            pl.BlockSpec((1, 2 * hw), lambda i: (0, 0)),
        ],
        out_shape=[
            jax.ShapeDtypeStruct((seq, 2 * hw), F32),
            jax.ShapeDtypeStruct((1, 2 * hw), F32),
        ],
        compiler_params=_cparams(("arbitrary",)),
    )(feat, pair(w1p), twice(b1), twice(f1), pair(w2), twice(b2), twice(f2), w3p,
      jnp.concatenate([deltas, deltas], axis=1))
    return taps.reshape(2 * seq, hw), asum[:, :hw] + asum[:, hw:]


def _np_split(a):
    a32 = jnp.asarray(a, F32)
    hi = a32.astype(BF16)
    lo = (a32 - hi.astype(F32)).astype(BF16)
    return hi, lo


def _dft_consts(n2):
    n1 = FFT_N1
    n = n1 * n2
    half = n1 // 2
    a1 = -2.0 * np.pi * np.outer(np.arange(n1), np.arange(n1)) / n1
    f1r, f1i = np.cos(a1), np.sin(a1)
    f1_data = np.block([[f1r[:, :half], -f1i[:, :half]], [f1i[:, :half], f1r[:, :half]]])
    f1_real = np.concatenate([f1r, f1i], axis=0)
    g1r, g1i = f1r / n, -f1i / n
    g1 = np.block([[g1r[:half], -g1i[:half]], [g1i[:half], g1r[:half]]])
    a2 = -2.0 * np.pi * np.outer(np.arange(n2), np.arange(n2)) / n2
    f2r, f2i = np.cos(a2), np.sin(a2)
    f2 = np.block([[f2r, -f2i], [f2i, f2r]])
    g2 = np.block([[f2r, f2i], [-f2i, f2r]])
    at = -2.0 * np.pi * np.outer(np.arange(n1), np.arange(n2)) / n
    twr = jnp.broadcast_to(jnp.asarray(np.cos(at), F32)[:, :, None], (n1, n2, LANES))
    twi = jnp.broadcast_to(jnp.asarray(np.sin(at), F32)[:, :, None], (n1, n2, LANES))
    return dict(f1_data=_np_split(f1_data), f1_real=_np_split(f1_real), g1=_np_split(g1),
                f2=_np_split(f2), g2=_np_split(g2), twr=twr, twi=twi)


def _fft1_kernel(z_ref, fh_ref, fl_ref, ar_ref, ai_ref):
    a = _dot3c(fh_ref[...], fl_ref[...], z_ref[...])
    ar_ref[...] = a[:FFT_N1]
    ai_ref[...] = a[FFT_N1:]


def _fft_first(zview, fmat, ncols):
    cb = min(FFT_COLS, ncols)
    fh, fl = fmat
    cspec = pl.BlockSpec(fh.shape, lambda j: (0, 0))
    ospec = pl.BlockSpec((FFT_N1, cb), lambda j: (0, j))
    return pl.pallas_call(
        _fft1_kernel,
        grid=(ncols // cb,),
        in_specs=[pl.BlockSpec((FFT_N1, cb), lambda j: (0, j)), cspec, cspec],
        out_specs=[ospec, ospec],
        out_shape=[jax.ShapeDtypeStruct((FFT_N1, ncols), F32)] * 2,
        compiler_params=_cparams(("arbitrary",)),
    )(zview, fh, fl)


def _fftmid_kernel(ar_ref, ai_ref, twr_ref, twi_ref, f2h_ref, f2l_ref, *rest, filter_only, n2):
    kb, _, width = ar_ref.shape
    reps = width // LANES
    lanes = lambda parts: jnp.concatenate(parts, axis=1)
    twr = lanes([t for j in range(kb) for t in [twr_ref[j]] * reps])
    twi = lanes([t for j in range(kb) for t in [twi_ref[j]] * reps])
    ar = lanes([ar_ref[j] for j in range(kb)])
    ai = lanes([ai_ref[j] for j in range(kb)])
    z = jnp.concatenate([ar * twr - ai * twi, ar * twi + ai * twr], axis=0)
    x = _dot3c(f2h_ref[...], f2l_ref[...], z)
    xr, xi = x[:n2], x[n2:]
    if filter_only:
        kr_out, ki_out = rest
        for j in range(kb):
            kr_out[j] = xr[:, j * width:(j + 1) * width]
            ki_out[j] = xi[:, j * width:(j + 1) * width]
        return
    kr_ref, ki_ref, g2h_ref, g2l_ref, br_out, bi_out = rest
    kr = lanes([kr_ref[j] for j in range(kb)])
    ki = lanes([ki_ref[j] for j in range(kb)])
    y = jnp.concatenate([xr * kr - xi * ki, xr * ki + xi * kr], axis=0)
    w = _dot3c(g2h_ref[...], g2l_ref[...], y)
    wr, wi = w[:n2], w[n2:]
    br = wr * twr + wi * twi
    bi = wi * twr - wr * twi
    for j in range(kb):
        br_out[j] = br[:, j * width:(j + 1) * width]
        bi_out[j] = bi[:, j * width:(j + 1) * width]


def _fft_mid(ar, ai, consts, n2, width, spectrum=None):
    a3r = ar.reshape(FFT_N1, n2, width)
    a3i = ai.reshape(FFT_N1, n2, width)
    blk = pl.BlockSpec((FFT_KB, n2, width), lambda k: (k, 0, 0))
    twspec = pl.BlockSpec((FFT_KB, n2, LANES), lambda k: (k, 0, 0))
    cspec = pl.BlockSpec((2 * n2, 2 * n2), lambda k: (0, 0))
    in_specs = [blk, blk, twspec, twspec, cspec, cspec]
    args = [a3r, a3i, consts["twr"], consts["twi"], *consts["f2"]]
    if spectrum is not None:
        in_specs += [blk, blk, cspec, cspec]
        args += [spectrum[0], spectrum[1], *consts["g2"]]
    return pl.pallas_call(
        functools.partial(_fftmid_kernel, filter_only=spectrum is None, n2=n2),
        grid=(FFT_N1 // FFT_KB,),
        in_specs=in_specs,
        out_specs=[blk, blk],
        out_shape=[jax.ShapeDtypeStruct((FFT_N1, n2, width), F32)] * 2,
        compiler_params=_cparams(("arbitrary",)),
    )(*args)


def _fftlast_kernel(br_ref, bi_ref, gh_ref, gl_ref, u_ref, x0_ref, bias_ref, invn_ref, o_ref):
    b = jnp.concatenate([br_ref[...], bi_ref[...]], axis=0)
    y = _dot3c(gh_ref[...], gl_ref[...], b)
    o_ref[...] = ((y * invn_ref[...] + u_ref[...] * bias_ref[...]) * x0_ref[...]).astype(BF16)


def _fft_last(br, bi, gmat, uview, x0view, bias_t, invn_t, ncols):
    cb = min(FFT_COLS, ncols)
    gh, gl = gmat
    cspec = pl.BlockSpec(gh.shape, lambda j: (0, 0))
    dspec = pl.BlockSpec((FFT_N1, cb), lambda j: (0, j))
    vspec = pl.BlockSpec((1, cb), lambda j: (0, 0))
    return pl.pallas_call(
        _fftlast_kernel,
        grid=(ncols // cb,),
        in_specs=[dspec, dspec, cspec, cspec, dspec, dspec, vspec, vspec],
        out_specs=dspec,
        out_shape=jax.ShapeDtypeStruct((FFT_N1, ncols), BF16),
        compiler_params=_cparams(("arbitrary",)),
    )(br, bi, gh, gl, uview, x0view, bias_t, invn_t)


def _dft_small_consts(seq):
    n = 2 * seq
    a = -2.0 * np.pi * np.outer(np.arange(n), np.arange(n)) / n
    fr, fi = np.cos(a), np.sin(a)
    f_data = np.block([[fr[:, :seq], -fi[:, :seq]], [fi[:, :seq], fr[:, :seq]]])
    f_real = np.concatenate([fr, fi], axis=0)
    gr, gi = fr / n, -fi / n
    g = np.block([[gr[:seq], -gi[:seq]], [gi[:seq], gr[:seq]]])
    return _np_split(f_data), _np_split(f_real), _np_split(g)


def _hy_ctx_kernel(u_ref, x0_ref, taps_ref, fdh_ref, fdl_ref, frh_ref, frl_ref, gh_ref, gl_ref,
                   bias_ref, invn_ref, o_ref):
    n = taps_ref.shape[0]
    z = u_ref[...]
    a = _dot3c(fdh_ref[...], fdl_ref[...], z)
    k = _dot3c(frh_ref[...], frl_ref[...], taps_ref[...])
    ar, ai, kr, ki = a[:n], a[n:], k[:n], k[n:]
    y = jnp.concatenate([ar * kr - ai * ki, ar * ki + ai * kr], axis=0)
    conv = _dot3c(gh_ref[...], gl_ref[...], y)
    o_ref[...] = ((conv * invn_ref[...] + z * bias_ref[...]) * x0_ref[...]).astype(BF16)


def _hy_ctx(u, x0, taps, bias, invn, small):
    rows, hw = u.shape
    full = lambda a: pl.BlockSpec(a.shape, lambda i: (0,) * a.ndim)
    args = [u, x0, taps, *small[0], *small[1], *small[2], bias, invn]
    return pl.pallas_call(
        _hy_ctx_kernel,
        grid=(1,),
        in_specs=[full(a) for a in args],
        out_specs=pl.BlockSpec((rows, hw), lambda i: (0, 0)),
        out_shape=jax.ShapeDtypeStruct((rows, hw), BF16),
        compiler_params=_cparams(("arbitrary",)),
    )(*args)


def _out_proj_kernel(x_ref, ypl_ref, yhl_ref, yal_ref, ypc_ref, yhc_ref, yac_ref, w_ref, g_ref, mod_ref, *rest,
                     nlt, tpb, ncond, pw, hw, moe, ntiles):
    if moe:
        rw_ref, xo_ref, u_ref, route_ref, un_ref = rest
    else:
        wg_ref, wu_ref, wd_ref, xo_ref = rest
    t = jnp.minimum(pl.program_id(0), ntiles - 1)
    if moe:
        @pl.when(pl.program_id(0) == 0)
        def _():
            un_ref[...] = jnp.zeros_like(un_ref)

        logits = _dot3(un_ref[...], rw_ref[...])
    ci = _cond_row(t, nlt, tpb, ncond)
    is_ctx = t >= nlt
    yp = jnp.where(is_ctx, ypc_ref[...], ypl_ref[...])
    yh = jnp.where(is_ctx, yhc_ref[...], yhl_ref[...])
    ya = jnp.where(is_ctx, yac_ref[...], yal_ref[...])
    mix = _dot(yp, w_ref[0:pw, :]) + _dot(yh, w_ref[pw:pw + hw, :]) + _dot(ya, w_ref[pw + hw:, :])
    x = x_ref[...] + mod_ref[2, pl.ds(ci, 1), :] * mix
    un = _normmod(x, g_ref[...], mod_ref[3, pl.ds(ci, 1), :], mod_ref[4, pl.ds(ci, 1), :])
    if not moe:
        u = un.astype(BF16)
        ff = wg_ref.shape[1]
        fc = ff // FFN_CHUNKS
        y = jnp.zeros(x.shape, F32)
        for c in range(FFN_CHUNKS):
            gate = _dot(u, wg_ref[:, c * fc:(c + 1) * fc])
            up = _dot(u, wu_ref[:, c * fc:(c + 1) * fc])
            y = y + _dot((_silu(gate) * up).astype(BF16), wd_ref[c * fc:(c + 1) * fc, :])
        xo_ref[...] = x + mod_ref[5, pl.ds(ci, 1), :] * y
    else:
        xo_ref[...] = x
        u_ref[...] = un.astype(BF16)
        un_ref[...] = un
        lane = lax.broadcasted_iota(jnp.int32, logits.shape, 1)
        neg = jnp.float32(-jnp.inf)
        lg = jnp.where(lane < N_EXPERTS, logits, neg)
        t1 = jnp.max(lg, axis=-1, keepdims=True)
        i1 = jnp.min(jnp.where(lg == t1, lane, LANES), axis=-1, keepdims=True)
        lg2 = jnp.where(lane == i1, neg, lg)
        t2 = jnp.max(lg2, axis=-1, keepdims=True)
        i2 = jnp.min(jnp.where(lg2 == t2, lane, LANES), axis=-1, keepdims=True)
        e2 = jnp.exp(t2 - t1)
        g1 = 1.0 / (1.0 + e2)
        g2 = e2 / (1.0 + e2)
        route_ref[...] = jnp.where(lane == 0, i1.astype(F32), jnp.where(lane == 1, i2.astype(F32),
                                   jnp.where(lane == 2, g1, jnp.where(lane == 3, g2, 0.0))))


def _out_proj(x, lat, ctx, w, g, mod, rw, ffn_w, *, nlt, tpb, ncond, out_rows):
    n, d = x.shape
    pw, hw, aw = (a.shape[1] for a in lat)
    tm = ROW_TILE
    moe = rw is not None
    nt = (n if moe else out_rows) // tm
    cur = lambda t: jnp.minimum(t, nt - 1)
    row = lambda width: pl.BlockSpec((tm, width), lambda t: (cur(t), 0))
    lrow = lambda width: pl.BlockSpec((tm, width), lambda t: (jnp.minimum(t, nlt - 1), 0))
    crow = lambda width: pl.BlockSpec((tm, width), lambda t: (jnp.maximum(cur(t) - nlt, 0), 0))
    in_specs = [row(d), lrow(pw), lrow(hw), lrow(aw), crow(pw), crow(hw), crow(aw),
                pl.BlockSpec(w.shape, lambda t: (0, 0)),
                pl.BlockSpec((1, d), lambda t: (0, 0)),
                pl.BlockSpec(mod.shape, lambda t: (0, 0, 0))]
    args = [x, *lat, *ctx, w, g, mod]
    scratch = []
    if not moe:
        const = lambda a: pl.BlockSpec(a.shape, lambda t: (0, 0), pipeline_mode=pl.Buffered(1))
        in_specs += [const(a) for a in ffn_w]
        args += list(ffn_w)
        out_specs = [row(d)]
        out_shape = [jax.ShapeDtypeStruct((out_rows, d), F32)]
    else:
        out_specs = [row(d), row(d)]
        out_shape = [jax.ShapeDtypeStruct((n, d), F32), jax.ShapeDtypeStruct((n, d), BF16)]
        in_specs.append(pl.BlockSpec(rw.shape, lambda t: (0, 0)))
        args.append(rw)
        out_specs.append(pl.BlockSpec((tm, LANES), lambda t: (jnp.maximum(t - 1, 0), 0)))
        out_shape.append(jax.ShapeDtypeStruct((n, LANES), F32))
        scratch.append(pltpu.VMEM((tm, d), F32))
    kern = functools.partial(_out_proj_kernel, nlt=nlt, tpb=tpb, ncond=ncond, pw=pw, hw=hw, moe=moe, ntiles=nt)
    return pl.pallas_call(
        kern, grid=(nt + 1 if moe else nt,), in_specs=in_specs, out_specs=out_specs, out_shape=out_shape,
        scratch_shapes=scratch,
        compiler_params=_cparams(("arbitrary",)),
    )(*args)


def _moe_kernel(te_ref, tv_ref, u_ref, wg_ref, wu_ref, wd_ref, *rest):
    o_ref, acc_ref = rest[-2:]
    t = pl.program_id(0)
    f = pl.program_id(1)

    @pl.when(tv_ref[t] > 0)
    def _():
        u = u_ref[...]
        h = (_silu(_dot(u, wg_ref[0, 0])) * _dot(u, wu_ref[0, 0])).astype(BF16)
        y = _dot(h, wd_ref[0, 0])

        @pl.when(f == 0)
        def _():
            acc_ref[...] = y

        @pl.when(f != 0)
        def _():
            acc_ref[...] = acc_ref[...] + y

        @pl.when(f == pl.num_programs(1) - 1)
        def _():
            o_ref[...] = acc_ref[...].astype(BF16)


def _moe_experts(ug, tile_expert, tile_valid, wg, wu, wd, j, *, tile0, total_rows, prev=None):
    p, d = ug.shape
    ff = wg.shape[3]
    tm, fc = MOE_TM, MOE_FC
    in_specs = [
        pl.BlockSpec((tm, d), lambda t, f, te, tv: (t, 0)),
        pl.BlockSpec((1, 1, d, fc), lambda t, f, te, tv: (j, te[t], 0, f)),
        pl.BlockSpec((1, 1, d, fc), lambda t, f, te, tv: (j, te[t], 0, f)),
        pl.BlockSpec((1, 1, fc, d), lambda t, f, te, tv: (j, te[t], f, 0)),
    ]
    args = [tile_expert, tile_valid, ug, wg, wu, wd]
    aliases = {}
    if prev is not None:
        in_specs.append(pl.BlockSpec(memory_space=pl.ANY))
        args.append(prev)
        aliases = {len(args) - 1: 0}
    grid_spec = pltpu.PrefetchScalarGridSpec(
        num_scalar_prefetch=2,
        grid=(p // tm, ff // fc),
        in_specs=in_specs,
        out_specs=pl.BlockSpec((tm, d), lambda t, f, te, tv: (t + tile0, 0)),
        scratch_shapes=[pltpu.VMEM((tm, d), F32)],
    )
    return pl.pallas_call(
        _moe_kernel, grid_spec=grid_spec,
        out_shape=jax.ShapeDtypeStruct((total_rows, d), BF16),
        input_output_aliases=aliases,
        compiler_params=_cparams(("arbitrary", "arbitrary")),
    )(*args)


def _moe_combine_kernel(x_ref, ya_ref, yb_ref, route_ref, mod_ref, o_ref, *, nlt, tpb, ncond):
    ci = _cond_row(pl.program_id(0), nlt, tpb, ncond)
    r = route_ref[...]
    lane = lax.broadcasted_iota(jnp.int32, r.shape, 1)
    g1 = jnp.sum(jnp.where(lane == 2, r, 0.0), axis=-1, keepdims=True)
    g2 = jnp.sum(jnp.where(lane == 3, r, 0.0), axis=-1, keepdims=True)
    y = g1 * ya_ref[...].astype(F32) + g2 * yb_ref[...].astype(F32)
    o_ref[...] = x_ref[...] + mod_ref[5, pl.ds(ci, 1), :] * y


def _moe_combine(x, ya, yb, route, mod, *, nlt, tpb, ncond, out_rows):
    d = x.shape[1]
    tm = ROW_TILE
    row = lambda width: pl.BlockSpec((tm, width), lambda t: (t, 0))
    kern = functools.partial(_moe_combine_kernel, nlt=nlt, tpb=tpb, ncond=ncond)
    return pl.pallas_call(
        kern,
        grid=(out_rows // tm,),
        in_specs=[row(d), row(d), row(d), row(LANES), pl.BlockSpec(mod.shape, lambda t: (0, 0, 0))],
        out_specs=row(d),
        out_shape=jax.ShapeDtypeStruct((out_rows, d), F32),
        compiler_params=_cparams(("arbitrary",)),
    )(x, ya, yb, route, mod)


def _moe_layer(x, u, route, mod, wg, wu, wd, j, *, nlt, tpb, ncond, out_rows):
    n, d = x.shape
    tm = MOE_TM
    experts = jnp.concatenate([route[:, 0], route[:, 1]]).astype(jnp.int32)
    onehot = (experts[:, None] == jnp.arange(N_EXPERTS, dtype=jnp.int32)[None, :]).astype(jnp.int32)
    rank = jnp.sum(onehot * (jnp.cumsum(onehot, axis=0) - 1), axis=1)
    counts = jnp.sum(onehot, axis=0)
    padded = ((counts + tm - 1) // tm) * tm
    ends = jnp.cumsum(padded)
    starts = ends - padded
    dest = starts[experts] + rank
    p = 2 * n + N_EXPERTS * tm
    tokens = jnp.concatenate([jnp.arange(n, dtype=jnp.int32)] * 2)
    inb = dict(mode="promise_in_bounds")
    src = jnp.zeros((p,), jnp.int32).at[dest].set(tokens, unique_indices=True, **inb)
    tile_start = jnp.arange(p // tm, dtype=jnp.int32) * tm
    tile_expert = jnp.minimum(jnp.searchsorted(ends, tile_start, side="right"), N_EXPERTS - 1).astype(jnp.int32)
    tile_valid = (tile_start < ends[-1]).astype(jnp.int32)
    nt0 = (p // tm) // 2
    ys = None
    for lo, hi in ((0, nt0), (nt0, p // tm)):
        ug = u.at[src[lo * tm:hi * tm]].get(**inb)
        ys = _moe_experts(ug, tile_expert[lo:hi], tile_valid[lo:hi], wg, wu, wd, j,
                          tile0=lo, total_rows=p, prev=ys)
    ya = ys.at[dest[:out_rows]].get(**inb)
    yb = ys.at[dest[n:n + out_rows]].get(**inb)
    return _moe_combine(x, ya, yb, route, mod, nlt=nlt, tpb=tpb, ncond=ncond, out_rows=out_rows)


def _rope_tables(seq, n_ctx_rows, batch):
    rows = seq // GRID_W
    row = jnp.repeat(jnp.arange(rows, dtype=F32), GRID_W)
    col = jnp.broadcast_to(jnp.arange(GRID_W, dtype=F32), (rows, GRID_W)).reshape(-1)
    inv_freq = jnp.power(ROPE_THETA, -jnp.arange(ROPE_FREQS, dtype=F32) / ROPE_FREQS)
    ar = row[:, None] * inv_freq
    ac = col[:, None] * inv_freq
    cos = jnp.concatenate([jnp.cos(ar), jnp.cos(ar), jnp.cos(ac), jnp.cos(ac)], axis=-1)
    sin = jnp.concatenate([-jnp.sin(ar), jnp.sin(ar), -jnp.sin(ac), jnp.sin(ac)], axis=-1)
    cos = jnp.tile(jnp.concatenate([cos, cos], axis=-1), (batch, 1))
    sin = jnp.tile(jnp.concatenate([sin, sin], axis=-1), (batch, 1))
    cos = jnp.concatenate([cos, jnp.ones((n_ctx_rows, 2 * HEAD_DIM), F32)], axis=0)
    sin = jnp.concatenate([sin, jnp.zeros((n_ctx_rows, 2 * HEAD_DIM), F32)], axis=0)
    return cos, sin


def kernel(x, c, ctx, c_ctx, mod_w, mod_b, norm1_g, norm2_g, w_in, w_out, pool_lin, pool_scale, hy_short_w, hy_short_b, hy_f_w1, hy_f_b1, hy_f_freq1, hy_f_w2, hy_f_b2, hy_f_freq2, hy_f_w3, hy_bias, qk_norm_g, diff_lambda, subln_g, ffn_w_gate, ffn_w_up, ffn_w_down, router_w, moe_w_gate, moe_w_up, moe_w_down):
    batch, seq, d = x.shape
    ctx_len = ctx.shape[1]
    depth = mod_w.shape[0]
    pw = pool_scale.shape[1]
    hw = hy_bias.shape[1]
    nl, nc = batch * seq, batch * ctx_len
    n = nl + nc
    tm = ROW_TILE
    assert seq % tm == 0 and nc % tm == 0 and seq % (FFT_N1 // 2) == 0 and seq % GRID_W == 0
    assert d == HEADS * 2 * V_DIM and pw == hw and batch == 2
    n2 = 2 * seq // FFT_N1
    ncols = n2 * hw
    assert (n * hw) % ncols == 0
    tiles = dict(nlt=nl // tm, tpb=seq // tm, ncond=batch + 1)

    xs = jnp.concatenate([x.reshape(nl, d), ctx.reshape(nc, d)], axis=0)
    cond8 = jnp.concatenate([c, c_ctx[None, :], jnp.zeros((8 - batch - 1, d), F32)], axis=0)
    mods = _modvec(cond8, mod_w, mod_b)
    cos_t, sin_t = _rope_tables(seq, nc, batch)
    consts = _dft_consts(n2)
    small = _dft_small_consts(ctx_len)
    eye = jnp.eye(len(POOL_WINDOWS), dtype=F32)
    n_moe = moe_w_gate.shape[0]
    assert n_moe <= depth
    moe_w2d = [w.reshape(-1, w.shape[-1]) for w in (moe_w_gate, moe_w_up, moe_w_down)]
    moe_bf16 = {}

    for i in range(depth):
        last = i == depth - 1
        lam_init = 0.8 - 0.6 * math.exp(-0.3 * i)
        mod = mods[i]
        gqk = jnp.concatenate([qk_norm_g[i], qk_norm_g[i]], axis=-1)
        zp, zh, qt, k12, vt = _in_proj(xs, norm1_g[i][None, :], mod, w_in[i].astype(BF16), cos_t, sin_t, gqk,
                                       pw=pw, hw=hw, **tiles)

        g_b = jnp.broadcast_to(subln_g[i][:, None], (V_DIM, ATT_TQ))
        att_kw = dict(batch=batch, seq=seq, ctx_len=ctx_len, lam_init=lam_init)

        cast_j = i if i < n_moe else None

        def attend(direct, lam_p=diff_lambda[i], qt=qt, k12=k12, vt=vt, g_b=g_b, att_kw=att_kw, cast_j=cast_j):
            ctx_out = _attention(lam_p, qt, k12, vt, g_b, latent=False, direct=direct, **att_kw)
            if cast_j is None:
                return (_attention(lam_p, qt, k12, vt, g_b, latent=True, direct=direct, **att_kw), ctx_out)
            if direct:
                lat_out, *wb = _attention(lam_p, qt, k12, vt, g_b, latent=True, direct=True,
                                          cast=[(w, cast_j, n_moe) for w in moe_w2d], **att_kw)
            else:
                lat_out = _attention(lam_p, qt, k12, vt, g_b, latent=True, direct=False, **att_kw)
                wb = _cast_bf16(*[w[cast_j * (w.shape[0] // n_moe):(cast_j + 1) * (w.shape[0] // n_moe)]
                                  for w in moe_w2d])
            return (lat_out, ctx_out, *wb)

        bound = (HEAD_DIM * QSCALE) * jnp.max(jnp.abs(qk_norm_g[i][0])) * jnp.max(jnp.abs(qk_norm_g[i][1]))
        ya_l, ya_c, *wb = lax.cond(bound * 1.02 < ATT_DIRECT_MAX, lambda: attend(True), lambda: attend(False))
        if cast_j is not None:
            moe_bf16[cast_j] = tuple(w.reshape((1,) + src.shape[1:])
                                     for w, src in zip(wb, (moe_w_gate, moe_w_up, moe_w_down)))

        lin_bd = (eye[:, None, :, None] * pool_lin[i][:, :, None, :]).reshape(pw, pw).astype(BF16)
        pscale = pool_scale[i][None, :]
        yp_l = _pool(zp, lin_bd, pscale, row0=0, batch=batch, seq=seq, out_rows=nl, out_row0=0)
        yp_c = _pool(zp, lin_bd, pscale, row0=nl, batch=batch, seq=ctx_len, out_rows=nc, out_row0=0)

        sw, sb = hy_short_w[i], hy_short_b[i][None, :]
        u_l, x0_l = _hy_pre(zh, sw, sb, row0=0, batch=batch, seq=seq, out_rows=n, out_row0=0)
        u_c, x0_c = _hy_pre(zh, sw, sb, row0=nl, batch=batch, seq=ctx_len, out_rows=nc, out_row0=0)
        filt = (hy_f_w1[i], hy_f_b1[i], hy_f_freq1[i], hy_f_w2[i], hy_f_b2[i], hy_f_freq2[i], hy_f_w3[i])
        bias = hy_bias[i][None, :]
        taps, asum = _hy_filter(seq, *filt, hw)
        kr1, ki1 = _fft_first(taps.reshape(FFT_N1, ncols), consts["f1_real"], ncols)
        spectrum = _fft_mid(kr1, ki1, consts, n2, hw)
        ar, ai = _fft_first(u_l.reshape(-1, ncols), consts["f1_data"], ncols)
        br, bi = _fft_mid(ar, ai, consts, n2, hw, spectrum=spectrum)
        reps = min(FFT_COLS, ncols) // hw
        yh_lat = _fft_last(br.reshape(FFT_N1, ncols), bi.reshape(FFT_N1, ncols), consts["g1"],
                           u_l.reshape(-1, ncols), x0_l.reshape(-1, ncols),
                           jnp.tile(bias, (1, reps)), jnp.tile(1.0 / asum, (1, reps)), ncols)
        taps_c, asum_c = _hy_filter(ctx_len, *filt, hw)
        yh_ctx = _hy_ctx(u_c, x0_c, taps_c, bias, 1.0 / asum_c, small)

        j = i // 2
        moe = i % 2 == 1
        rw = ffn_w = None
        if moe:
            rw = jnp.concatenate([router_w[j], jnp.zeros((d, LANES - N_EXPERTS), F32)], axis=1)
        else:
            ffn_w = (ffn_w_gate[j].astype(BF16), ffn_w_up[j].astype(BF16), ffn_w_down[j].astype(BF16))
        out_rows = nl if last else n
        outs = _out_proj(xs, (yp_l, yh_lat.reshape(nl, hw), ya_l), (yp_c, yh_ctx, ya_c), w_out[i].astype(BF16),
                         norm2_g[i][None, :], mod, rw, ffn_w, out_rows=out_rows, **tiles)
        if moe:
            xs, u, route = outs
            xs = _moe_layer(xs, u, route, mod, *moe_bf16[j], 0, out_rows=out_rows, **tiles)
        else:
            xs, = outs
    return xs[:nl].reshape(batch, seq, d)
```

```python
import functools
import math

import numpy as np
import jax
import jax.numpy as jnp
from jax import lax
from jax.experimental import pallas as pl
from jax.experimental.pallas import tpu as pltpu

F32 = jnp.float32
BF16 = jnp.bfloat16
EPS = 1e-6

GRID_W = 64
POOL_WINDOWS = (2, 4, 8, 16)
HEADS = 4
HEAD_DIM = 64
V_DIM = 128
ROPE_FREQS = 16
ROPE_THETA = 10000.0
HY_BANDS = 16
HY_TARGET = 1e-2
HY_FAST = 0.3
HY_SLOW = 1.5
N_EXPERTS = 8

LANES = 128
ROW_TILE = 512
HALO = 32
FFT_N1 = 128
FFT_COLS = 2048
FFT_KB = 4
ATT_TQ = 1024
ATT_TK = 512
ATT_UNROLL = 16
ATT_GROUP = 8
QSCALE = (HEAD_DIM ** -0.5) * math.log2(math.e)
ATT_DIRECT_MAX = 100.0
FFN_CHUNKS = 2
CAST_STEPS = 64
MOE_TM = 512
MOE_FC = 1792
VMEM_LIMIT = 56 * 1024 * 1024


def _cparams(sem):
    return pltpu.CompilerParams(dimension_semantics=sem, vmem_limit_bytes=VMEM_LIMIT)


def _dot(a, b):
    return jnp.dot(a, b, preferred_element_type=F32)


def _split(a):
    hi = a.astype(BF16)
    lo = (a - hi.astype(F32)).astype(BF16)
    return hi, lo


def _dot3(a, b):
    ah, al = _split(a)
    bh, bl = _split(b)
    return _dot(ah, bh) + _dot(ah, bl) + _dot(al, bh)


def _dot3c(ch, cl, d):
    dh, dl = _split(d)
    return _dot(ch, dh) + _dot(ch, dl) + _dot(cl, dh)


def _silu(x):
    return x / (1.0 + jnp.exp(-x))


def _normmod(x, g, shift, scale):
    ms = jnp.mean(x * x, axis=-1, keepdims=True)
    return (x * lax.rsqrt(ms + EPS) * g) * (1.0 + scale) + shift


def _cond_row(t, n_lat_tiles, tiles_per_batch, n_cond):
    return jnp.where(t >= n_lat_tiles, n_cond - 1, t // tiles_per_batch)


def _cast_kernel(*refs):
    half = len(refs) // 2
    for w_ref, o_ref in zip(refs[:half], refs[half:]):
        o_ref[...] = w_ref[...].astype(BF16)


def _cast_bf16(*ws):
    flat = [w.reshape(-1, w.shape[-1]) for w in ws]
    steps = CAST_STEPS
    specs = [pl.BlockSpec((w.shape[0] // steps, w.shape[1]), lambda t: (t, 0)) for w in flat]
    outs = pl.pallas_call(
        _cast_kernel,
        grid=(steps,),
        in_specs=specs,
        out_specs=specs,
        out_shape=[jax.ShapeDtypeStruct(w.shape, BF16) for w in flat],
        compiler_params=_cparams(("arbitrary",)),
    )(*flat)
    return [o.reshape(w.shape) for o, w in zip(outs, ws)]


def _modvec_kernel(c_ref, w_ref, b_ref, o_ref):
    s = _silu(c_ref[...])
    o_ref[0, 0] = _dot3(s, w_ref[0]) + b_ref[0]


def _modvec(cond8, mod_w, mod_b):
    depth, d, six_d = mod_w.shape
    nchunk = six_d // d
    return pl.pallas_call(
        _modvec_kernel,
        grid=(depth, nchunk),
        in_specs=[
            pl.BlockSpec((8, d), lambda i, j: (0, 0)),
            pl.BlockSpec((1, d, d), lambda i, j: (i, 0, j)),
            pl.BlockSpec((1, 1, d), lambda i, j: (i, 0, j)),
        ],
        out_specs=pl.BlockSpec((1, 1, 8, d), lambda i, j: (i, j, 0, 0)),
        out_shape=jax.ShapeDtypeStruct((depth, nchunk, 8, d), F32),
        compiler_params=_cparams(("arbitrary", "arbitrary")),
    )(cond8, mod_w, mod_b.reshape(depth, 1, six_d))


def _in_proj_kernel(x_ref, g_ref, mod_ref, w_ref, cos_ref, sin_ref, gqk_ref,
                    zp_ref, zh_ref, qt_ref, k_ref, vt_ref, zatt_ref, *, nlt, tpb, ncond, pw, hw, ntiles):
    t = pl.program_id(0)

    @pl.when(t == 0)
    def _():
        zatt_ref[...] = jnp.zeros_like(zatt_ref)

    tm = x_ref.shape[0]
    att = pw + 3 * hw
    lane = lax.broadcasted_iota(jnp.int32, (tm, 2 * HEAD_DIM), 1)
    first = lane < HEAD_DIM
    apart = (lane % (2 * ROPE_FREQS)) < ROPE_FREQS
    cos = cos_ref[...]
    sin = sin_ref[...]
    qk_w = HEADS * 2 * HEAD_DIM

    def norm_rope(v, g):
        v2 = v * v
        s_all = jnp.sum(v2, axis=-1, keepdims=True)
        s_first = jnp.sum(jnp.where(first, v2, 0.0), axis=-1, keepdims=True)
        ms = jnp.where(first, s_first, s_all - s_first) * (1.0 / HEAD_DIM)
        vn = v * lax.rsqrt(ms + EPS) * g
        swapped = jnp.where(apart, pltpu.roll(vn, 2 * HEAD_DIM - ROPE_FREQS, 1), pltpu.roll(vn, ROPE_FREQS, 1))
        return vn * cos + swapped * sin

    for h in range(HEADS):
        lo = h * 2 * HEAD_DIM
        q = norm_rope(zatt_ref[:, lo:lo + 2 * HEAD_DIM], gqk_ref[0:1, :])
        qt_ref[h] = (q * QSCALE).T.astype(BF16)
        k = norm_rope(zatt_ref[:, qk_w + lo:qk_w + lo + 2 * HEAD_DIM], gqk_ref[1:2, :])
        k_ref[h] = k.astype(BF16)
        vlo = 2 * qk_w + h * V_DIM
        vt_ref[h] = zatt_ref[:, vlo:vlo + V_DIM].T.astype(BF16)

    ci = _cond_row(jnp.minimum(t, ntiles - 1), nlt, tpb, ncond)
    shift = mod_ref[0, pl.ds(ci, 1), :]
    scale = mod_ref[1, pl.ds(ci, 1), :]
    y = _normmod(x_ref[...], g_ref[...], shift, scale).astype(BF16)
    z = _dot(y, w_ref[...])
    zp_ref[...] = z[:, :pw].astype(BF16)
    zh_ref[...] = z[:, pw:att].astype(BF16)
    zatt_ref[...] = z[:, att:]


def _in_proj(x, g, mod, w, cos_t, sin_t, gqk, *, nlt, tpb, ncond, pw, hw):
    n, d = x.shape
    wid = w.shape[1]
    tm = ROW_TILE
    nt = n // tm
    kern = functools.partial(_in_proj_kernel, nlt=nlt, tpb=tpb, ncond=ncond, pw=pw, hw=hw, ntiles=nt)
    cur = lambda t: jnp.minimum(t, nt - 1)
    prv = lambda t: jnp.maximum(t - 1, 0)
    return pl.pallas_call(
        kern,
        grid=(nt + 1,),
        in_specs=[
            pl.BlockSpec((tm, d), lambda t: (cur(t), 0)),
            pl.BlockSpec((1, d), lambda t: (0, 0)),
            pl.BlockSpec(mod.shape, lambda t: (0, 0, 0)),
            pl.BlockSpec((d, wid), lambda t: (0, 0)),
            pl.BlockSpec((tm, 2 * HEAD_DIM), lambda t: (prv(t), 0)),
            pl.BlockSpec((tm, 2 * HEAD_DIM), lambda t: (prv(t), 0)),
            pl.BlockSpec((2, 2 * HEAD_DIM), lambda t: (0, 0)),
        ],
        out_specs=[
            pl.BlockSpec((tm, pw), lambda t: (cur(t), 0)),
            pl.BlockSpec((tm, 3 * hw), lambda t: (cur(t), 0)),
            pl.BlockSpec((HEADS, 2 * HEAD_DIM, tm), lambda t: (0, 0, prv(t))),
            pl.BlockSpec((HEADS, tm, 2 * HEAD_DIM), lambda t: (0, prv(t), 0)),
            pl.BlockSpec((HEADS, V_DIM, tm), lambda t: (0, 0, prv(t))),
        ],
        scratch_shapes=[pltpu.VMEM((tm, wid - pw - 3 * hw), F32)],
        out_shape=[
            jax.ShapeDtypeStruct((n, pw), BF16),
            jax.ShapeDtypeStruct((n, 3 * hw), BF16),
            jax.ShapeDtypeStruct((HEADS, 2 * HEAD_DIM, n), BF16),
            jax.ShapeDtypeStruct((HEADS, n, 2 * HEAD_DIM), BF16),
            jax.ShapeDtypeStruct((HEADS, V_DIM, n), BF16),
        ],
        compiler_params=_cparams(("arbitrary",)),
    )(x, g, mod, w, cos_t, sin_t, gqk)


def _attn_kernel(lam_ref, qt_ref, kc_ref, vtc_ref, *rest, tk, n_lat_chunks, lam_init):
    if n_lat_chunks:
        kl_ref, vtl_ref, g_ref, o_ref, m_ref, l_ref, acc_ref = rest
    else:
        g_ref, o_ref, m_ref, l_ref, acc_ref = rest
    qt = qt_ref[0]
    row = lax.broadcasted_iota(jnp.int32, qt.shape, 0)
    zero = jnp.zeros_like(qt)
    qmaps = (jnp.where(row < HEAD_DIM, qt, zero), jnp.where(row >= HEAD_DIM, qt, zero))

    def process(k_tile, vt_tile, first):
        for mi in range(2):
            s = _dot(k_tile, qmaps[mi])
            smax = jnp.max(s, axis=0, keepdims=True)
            if first:
                m_new = smax
            else:
                m_old = m_ref[mi]
                m_new = jnp.maximum(m_old, smax)
            p = jnp.exp2(s - m_new)
            psum = jnp.sum(p, axis=0, keepdims=True)
            pv = _dot(vt_tile, p.astype(BF16))
            if first:
                l_ref[mi] = psum
                acc_ref[mi] = pv
            else:
                alpha = jnp.exp2(m_old - m_new)
                l_ref[mi] = alpha * l_ref[mi] + psum
                acc_ref[mi] = alpha * acc_ref[mi] + pv
            m_ref[mi] = m_new

    process(kc_ref[0], vtc_ref[0], True)
    if n_lat_chunks:
        def body(j, carry):
            off = pl.multiple_of(j * tk, tk)
            process(kl_ref[0, pl.ds(off, tk), :], vtl_ref[0, :, pl.ds(off, tk)], False)
            return carry
        lax.fori_loop(0, n_lat_chunks, body, 0)

    lv = lam_ref[...]
    lam = (jnp.exp(jnp.sum(lv[0:1] * lv[1:2], axis=-1, keepdims=True))
           - jnp.exp(jnp.sum(lv[2:3] * lv[3:4], axis=-1, keepdims=True)) + lam_init)
    o = acc_ref[0] * (1.0 / l_ref[0]) - lam * (acc_ref[1] * (1.0 / l_ref[1]))
    ms = jnp.mean(o * o, axis=0, keepdims=True)
    y = o * lax.rsqrt(ms + EPS) * g_ref[...] * (1.0 - lam_init)
    o_ref[...] = y.T.astype(BF16)


def _attn_direct_kernel(lam_ref, qt_ref, kc_ref, vtc_ref, *rest, tk, n_lat_chunks, lam_init, n_cast=0):
    if n_lat_chunks:
        kl_ref, vtl_ref, g_ref = rest[:3]
        rest = rest[3:]
    else:
        g_ref = rest[0]
        rest = rest[1:]
    cast_in, (o_ref, *cast_out), (p_ref, l_ref, acc_ref) = rest[:n_cast], rest[n_cast:2 * n_cast + 1], rest[2 * n_cast + 1:]
    for w_ref, c_ref in zip(cast_in, cast_out):
        c_ref[...] = w_ref[...].astype(BF16)
    qt = qt_ref[0]
    tq = qt.shape[1]
    row = lax.broadcasted_iota(jnp.int32, qt.shape, 0)
    zero = jnp.zeros_like(qt)
    qmaps = (jnp.where(row < HEAD_DIM, qt, zero), jnp.where(row >= HEAD_DIM, qt, zero))
    grp = ATT_GROUP

    def keys(first_piece, count):
        if isinstance(first_piece, int):
            return pl.ds((first_piece - 1) * tk, count * tk)
        return pl.ds(pl.multiple_of((first_piece - 1) * tk, tk), count * tk)

    def stage_ab(k_tile, pset, r):
        for mi in range(2):
            p = jnp.exp2(_dot(k_tile, qmaps[mi]))
            l_ref[mi] = l_ref[mi] + jnp.sum(p.reshape(tk // 8, 8, tq), axis=0)
            p_ref[pset, mi, r * tk:(r + 1) * tk, :] = p.astype(BF16)

    def stage_c(vt_tile, pset, lo, count):
        for mi in range(2):
            acc_ref[mi] = acc_ref[mi] + _dot(vt_tile, p_ref[pset, mi, lo * tk:(lo + count) * tk, :])

    def group_ab(g, pset):
        k_tile = kl_ref[0, keys(grp * g, grp), :]
        for mi in range(2):
            p = jnp.exp2(_dot(k_tile, qmaps[mi]))
            l_ref[mi] = l_ref[mi] + jnp.sum(p.reshape(grp * tk // 8, 8, tq), axis=0)
            p_ref[pset, mi] = p.astype(BF16)

    def group_c(g, pset):
        stage_c(vtl_ref[0, :, keys(grp * g, grp)], pset, 0, grp)

    l_ref[...] = jnp.zeros_like(l_ref)
    acc_ref[...] = jnp.zeros_like(acc_ref)
    stage_ab(kc_ref[0], 0, 0)
    if n_lat_chunks == 0:
        stage_c(vtc_ref[0], 0, 0, 1)
    else:
        n_groups = n_lat_chunks // grp
        per_step = ATT_UNROLL // grp
        for r in range(1, grp):
            stage_ab(kl_ref[0, keys(r, 1), :], 0, r)
        group_ab(1, 1)
        stage_c(vtc_ref[0], 0, 0, 1)
        stage_c(vtl_ref[0, :, keys(1, grp - 1)], 0, 1, grp - 1)
        for g in range(2, per_step):
            group_ab(g, g % 2)
            group_c(g - 1, (g - 1) % 2)

        def body(jj, carry):
            g0 = per_step * (jj + 1)
            for q in range(per_step):
                group_ab(g0 + q, q % 2)
                group_c(g0 + q - 1, (q + 1) % 2)
            return carry

        lax.fori_loop(0, n_groups // per_step - 1, body, 0)
        stage_ab(kl_ref[0, keys(n_lat_chunks, 1), :], 0, 0)
        group_c(n_groups - 1, 1)
        stage_c(vtl_ref[0, :, keys(n_lat_chunks, 1)], 0, 0, 1)

    lv = lam_ref[...]
    lam = (jnp.exp(jnp.sum(lv[0:1] * lv[1:2], axis=-1, keepdims=True))
           - jnp.exp(jnp.sum(lv[2:3] * lv[3:4], axis=-1, keepdims=True)) + lam_init)
    l1 = jnp.sum(l_ref[0], axis=0, keepdims=True)
    l2 = jnp.sum(l_ref[1], axis=0, keepdims=True)
    o = acc_ref[0] * (1.0 / l1) - lam * (acc_ref[1] * (1.0 / l2))
    ms = jnp.mean(o * o, axis=0, keepdims=True)
    y = o * lax.rsqrt(ms + EPS) * g_ref[...] * (1.0 - lam_init)
    o_ref[...] = y.T.astype(BF16)


def _attention(lam_p, qt, k12, vt, g_b, *, batch, seq, ctx_len, latent, lam_init, direct, cast=()):
    n = k12.shape[1]
    nl = batch * seq
    tq = ATT_TQ if latent else ctx_len
    nq = seq // tq if latent else 1
    tk = ctx_len if direct else ATT_TK
    assert seq % ((ATT_UNROLL if direct else 1) * tk) == 0
    qbase = 0 if latent else nl // tq

    def qrow(b, iq):
        return (b * nq + iq) if latent else (qbase + b)

    in_specs = [
        pl.BlockSpec(lam_p.shape, lambda b, h, iq: (0, 0)),
        pl.BlockSpec((1, 2 * HEAD_DIM, tq), lambda b, h, iq: (h, 0, qrow(b, iq))),
        pl.BlockSpec((1, ctx_len, 2 * HEAD_DIM), lambda b, h, iq: (h, nl // ctx_len + b, 0)),
        pl.BlockSpec((1, V_DIM, ctx_len), lambda b, h, iq: (h, 0, nl // ctx_len + b)),
    ]
    args = [lam_p, qt, k12, vt]
    if latent:
        in_specs += [
            pl.BlockSpec((1, seq, 2 * HEAD_DIM), lambda b, h, iq: (h, b, 0)),
            pl.BlockSpec((1, V_DIM, seq), lambda b, h, iq: (h, 0, b)),
        ]
        args += [k12, vt]
    in_specs.append(pl.BlockSpec((V_DIM, tq), lambda b, h, iq: (0, 0)))
    args.append(g_b[:, :tq])
    nrows = nl if latent else batch * ctx_len
    body = _attn_direct_kernel if direct else _attn_kernel
    kw = dict(tk=tk, n_lat_chunks=(seq // tk if latent else 0), lam_init=lam_init)
    out_specs = [pl.BlockSpec((tq, V_DIM), lambda b, h, iq: (b * nq + iq, h))]
    out_shape = [jax.ShapeDtypeStruct((nrows, HEADS * V_DIM), BF16)]
    if cast:
        assert direct
        kw["n_cast"] = len(cast)
        steps = batch * HEADS * nq
        for w, slab, parts in cast:
            rows = w.shape[0] // parts // steps
            in_specs.append(pl.BlockSpec(
                (rows, w.shape[1]), lambda b, h, iq, slab=slab: (slab * steps + (b * HEADS + h) * nq + iq, 0)))
            args.append(w)
            out_specs.append(pl.BlockSpec((rows, w.shape[1]), lambda b, h, iq: ((b * HEADS + h) * nq + iq, 0)))
            out_shape.append(jax.ShapeDtypeStruct((w.shape[0] // parts, w.shape[1]), BF16))
    kern = functools.partial(body, **kw)
    if direct:
        scratch = [pltpu.VMEM((2, 2, ATT_GROUP * tk, tq), BF16),
                   pltpu.VMEM((2, 8, tq), F32), pltpu.VMEM((2, V_DIM, tq), F32)]
    else:
        scratch = [pltpu.VMEM((2, 1, tq), F32), pltpu.VMEM((2, 1, tq), F32), pltpu.VMEM((2, V_DIM, tq), F32)]
    outs = pl.pallas_call(
        kern,
        grid=(batch, HEADS, nq),
        in_specs=in_specs,
        out_specs=out_specs,
        out_shape=out_shape,
        scratch_shapes=scratch,
        compiler_params=_cparams(("arbitrary", "arbitrary", "arbitrary")),
    )(*args)
    return outs if cast else outs[0]


def _halo_specs(width, *, row0, seq, tile):
    hb = tile // HALO

    def cur(b, i):
        return ((row0 + b * seq) // tile + i, 0)

    def prev(b, i):
        first = (row0 + b * seq) // HALO
        return (jnp.maximum(first + i * hb - 1, first), 0)

    def nxt(b, i):
        first = (row0 + b * seq) // HALO
        return (jnp.minimum(first + (i + 1) * hb, first + seq // HALO - 1), 0)

    return [pl.BlockSpec((HALO, width), prev), pl.BlockSpec((tile, width), cur), pl.BlockSpec((HALO, width), nxt)]


def _fill_ext(ext_ref, prev_ref, cur_ref, next_ref, *, seq, tile):
    pos0 = pl.program_id(1) * tile
    width = cur_ref.shape[1]
    hpos = lax.broadcasted_iota(jnp.int32, (HALO, width), 0)
    ext_ref[0:HALO, :] = jnp.where(pos0 - HALO + hpos >= 0, prev_ref[...].astype(F32), 0.0)
    ext_ref[HALO:HALO + tile, :] = cur_ref[...].astype(F32)
    ext_ref[HALO + tile:, :] = jnp.where(pos0 + tile + hpos < seq, next_ref[...].astype(F32), 0.0)


def _pool_kernel(prev_ref, cur_ref, next_ref, lin_ref, scale_ref, o_ref, ext_ref, s_ref, *, seq, tile):
    _fill_ext(ext_ref, prev_ref, cur_ref, next_ref, seq=seq, tile=tile)
    width = cur_ref.shape[1]
    gd = width // len(POOL_WINDOWS)
    n0 = tile + 2 * HALO
    s_ref[0, 0:n0 - 8, :] = ext_ref[0:n0 - 8, :] + ext_ref[1:n0 - 7, :]
    for k in range(1, len(POOL_WINDOWS)):
        step = 1 << k
        ln = n0 - 8 * (k + 1)
        s_ref[k, 0:ln, :] = s_ref[k - 1, 0:ln, :] + s_ref[k - 1, step:step + ln, :]
    lane = lax.broadcasted_iota(jnp.int32, (tile, width), 1)
    pos = pl.program_id(1) * tile + lax.broadcasted_iota(jnp.int32, (tile, width), 0)
    grp = lane // gd
    wsum = jnp.zeros((tile, width), F32)
    half = jnp.zeros((tile, width), jnp.int32)
    for k, win in enumerate(POOL_WINDOWS):
        start = HALO - win // 2
        wsum = jnp.where(grp == k, s_ref[k, start:start + tile, :], wsum)
        half = jnp.where(grp == k, win // 2, half)
    cnt = jnp.minimum(pos + half, seq) - jnp.maximum(pos - half, 0)
    z = ext_ref[HALO:HALO + tile, :]
    dlt = (wsum / cnt.astype(F32) - z).astype(BF16)
    o_ref[...] = (_dot(dlt, lin_ref[...]) * scale_ref[...]).astype(BF16)


def _pool(zp, lin_bd, scale, *, row0, batch, seq, out_rows, out_row0):
    width = zp.shape[1]
    tile = min(ROW_TILE, seq)
    kern = functools.partial(_pool_kernel, seq=seq, tile=tile)
    return pl.pallas_call(
        kern,
        grid=(batch, seq // tile),
        in_specs=_halo_specs(width, row0=row0, seq=seq, tile=tile) + [
            pl.BlockSpec((width, width), lambda b, i: (0, 0)),
            pl.BlockSpec((1, width), lambda b, i: (0, 0)),
        ],
        out_specs=pl.BlockSpec((tile, width), lambda b, i: ((out_row0 + b * seq) // tile + i, 0)),
        out_shape=jax.ShapeDtypeStruct((out_rows, width), BF16),
        scratch_shapes=[
            pltpu.VMEM((tile + 2 * HALO, width), F32),
            pltpu.VMEM((4, tile + 2 * HALO, width), F32),
        ],
        compiler_params=_cparams(("arbitrary", "arbitrary")),
    )(zp, zp, zp, lin_bd, scale)


def _hy_pre_kernel(prev_ref, cur_ref, next_ref, w_ref, b_ref, u_ref, x0_ref, ext_ref, *, seq, tile, hw):
    _fill_ext(ext_ref, prev_ref, cur_ref, next_ref, seq=seq, tile=tile)
    y = b_ref[...] + ext_ref[HALO - 1:HALO - 1 + tile, :] * w_ref[0:1, :]
    y = y + ext_ref[HALO:HALO + tile, :] * w_ref[1:2, :]
    y = y + ext_ref[HALO + 1:HALO + 1 + tile, :] * w_ref[2:3, :]
    x0_ref[...] = y[:, :hw]
    u_ref[...] = y[:, 2 * hw:] * y[:, hw:2 * hw]


def _hy_pre(zh, sw, sb, *, row0, batch, seq, out_rows, out_row0):
    width = zh.shape[1]
    hw = width // 3
    tile = min(ROW_TILE, seq)
    kern = functools.partial(_hy_pre_kernel, seq=seq, tile=tile, hw=hw)
    ospec = pl.BlockSpec((tile, hw), lambda b, i: ((out_row0 + b * seq) // tile + i, 0))
    return pl.pallas_call(
        kern,
        grid=(batch, seq // tile),
        in_specs=_halo_specs(width, row0=row0, seq=seq, tile=tile) + [
            pl.BlockSpec((3, width), lambda b, i: (0, 0)),
            pl.BlockSpec((1, width), lambda b, i: (0, 0)),
        ],
        out_specs=[ospec, ospec],
        out_shape=[jax.ShapeDtypeStruct((out_rows, hw), F32), jax.ShapeDtypeStruct((out_rows, hw), F32)],
        scratch_shapes=[pltpu.VMEM((tile + 2 * HALO, width), F32)],
        compiler_params=_cparams(("arbitrary", "arbitrary")),
    )(zh, zh, zh, sw, sb)


def _filter_kernel(feat_ref, w1_ref, b1_ref, f1_ref, w2_ref, b2_ref, f2_ref, w3_ref, dl_ref,
                   taps_ref, asum_ref, *, tile, hw, seq, embp):
    feat = feat_ref[...]
    h = jnp.sin(f1_ref[...] * (_dot3(feat, w1_ref[...]) + b1_ref[...]))
    h = jnp.sin(f2_ref[...] * (_dot3(h, w2_ref[...]) + b2_ref[...]))
    h = _dot3(h, w3_ref[...])
    odd = lax.broadcasted_iota(jnp.int32, (tile, 2 * hw), 1) >= hw
    tcol = jnp.where(odd, feat[:, embp:embp + 1], feat[:, 0:1])
    dec = jnp.exp(-tcol * dl_ref[...])
    row = pl.program_id(0) * tile + lax.broadcasted_iota(jnp.int32, (tile, 2 * hw), 0)
    pos = 2 * row + odd.astype(jnp.int32)
    taps = jnp.where(pos == seq, 0.0, h * dec)
    taps_ref[...] = taps
    part = jnp.sum(jnp.abs(taps), axis=0, keepdims=True)

    @pl.when(pl.program_id(0) == 0)
    def _():
        asum_ref[...] = part

    @pl.when(pl.program_id(0) != 0)
    def _():
        asum_ref[...] = asum_ref[...] + part


def _hy_filter(seq, w1, b1, f1, w2, b2, f2, w3, hw):
    ar = jnp.arange(seq, dtype=jnp.int32)
    pos = jnp.concatenate([ar, (seq - ar) % seq]).astype(F32)[:, None]
    t = pos * (1.0 / (seq - 1))
    w = (2.0 * math.pi / seq) * pos
    bands = jnp.linspace(1e-4, HY_BANDS - 1, HY_BANDS, dtype=F32)[None, :]
    emb = 1 + 2 * HY_BANDS
    embp = ((emb + 7) // 8) * 8
    feat = jnp.concatenate([t, jnp.cos(bands * w), -jnp.sin(bands * w), jnp.zeros((2 * seq, embp - emb), F32)], axis=-1)
    w1p = jnp.concatenate([w1, jnp.zeros((embp - emb, w1.shape[1]), F32)], axis=0)
    max_decay = math.log(1.0 / HY_TARGET) / HY_FAST
    min_decay = math.log(1.0 / HY_TARGET) / HY_SLOW
    deltas = jnp.linspace(min_decay, max_decay, hw, dtype=F32)[None, :]
    feat = feat.reshape(seq, 2 * embp)
    hid = w1.shape[1]

    def pair(m):
        z = jnp.zeros_like(m)
        return jnp.concatenate([jnp.concatenate([m, z], axis=1), jnp.concatenate([z, m], axis=1)], axis=0)

    twice = lambda v: jnp.concatenate([v, v])[None, :]
    w3p = jnp.concatenate([pair(w3[:, :hw]), pair(w3[:, hw:])], axis=1)
    tile = min(ROW_TILE, seq) // 2
    nfwd = seq // (2 * tile)
    full = lambda shape: pl.BlockSpec(shape, lambda i: (0,) * len(shape))
    taps, asum = pl.pallas_call(
        functools.partial(_filter_kernel, tile=tile, hw=hw, seq=seq, embp=embp),
        grid=(2 * nfwd,),
        in_specs=[
            pl.BlockSpec((tile, 2 * embp), lambda i: (i, 0)),
            full((2 * embp, 2 * hid)), full((1, 2 * hid)), full((1, 2 * hid)),
            full((2 * hid, 2 * hid)), full((1, 2 * hid)), full((1, 2 * hid)),
            pl.BlockSpec((2 * hid, 2 * hw), lambda i: (0, i // nfwd)), full((1, 2 * hw)),
        ],
        out_specs=[
            pl.BlockSpec((tile, 2 * hw), lambda i: (i, 0)),
            pl.BlockSpec((1, 2 * hw), lambda i: (0, 0)),
        ],
        out_shape=[
            jax.ShapeDtypeStruct((seq, 2 * hw), F32),
            jax.ShapeDtypeStruct((1, 2 * hw), F32),
        ],
        compiler_params=_cparams(("arbitrary",)),
    )(feat, pair(w1p), twice(b1), twice(f1), pair(w2), twice(b2), twice(f2), w3p,
      jnp.concatenate([deltas, deltas], axis=1))
    return taps.reshape(2 * seq, hw), asum[:, :hw] + asum[:, hw:]


def _np_split(a):
    a32 = jnp.asarray(a, F32)
    hi = a32.astype(BF16)
    lo = (a32 - hi.astype(F32)).astype(BF16)
    return hi, lo


def _dft_consts(n2):
    n1 = FFT_N1
    n = n1 * n2
    half = n1 // 2
    a1 = -2.0 * np.pi * np.outer(np.arange(n1), np.arange(n1)) / n1
    f1r, f1i = np.cos(a1), np.sin(a1)
    f1_data = np.block([[f1r[:, :half], -f1i[:, :half]], [f1i[:, :half], f1r[:, :half]]])
    f1_real = np.concatenate([f1r, f1i], axis=0)
    g1r, g1i = f1r / n, -f1i / n
    g1 = np.block([[g1r[:half], -g1i[:half]], [g1i[:half], g1r[:half]]])
    a2 = -2.0 * np.pi * np.outer(np.arange(n2), np.arange(n2)) / n2
    f2r, f2i = np.cos(a2), np.sin(a2)
    f2 = np.block([[f2r, -f2i], [f2i, f2r]])
    g2 = np.block([[f2r, f2i], [-f2i, f2r]])
    at = -2.0 * np.pi * np.outer(np.arange(n1), np.arange(n2)) / n
    twr = jnp.broadcast_to(jnp.asarray(np.cos(at), F32)[:, :, None], (n1, n2, LANES))
    twi = jnp.broadcast_to(jnp.asarray(np.sin(at), F32)[:, :, None], (n1, n2, LANES))
    return dict(f1_data=_np_split(f1_data), f1_real=_np_split(f1_real), g1=_np_split(g1),
                f2=_np_split(f2), g2=_np_split(g2), twr=twr, twi=twi)


def _fft1_kernel(z_ref, fh_ref, fl_ref, ar_ref, ai_ref):
    a = _dot3c(fh_ref[...], fl_ref[...], z_ref[...])
    ar_ref[...] = a[:FFT_N1]
    ai_ref[...] = a[FFT_N1:]


def _fft_first(zview, fmat, ncols):
    cb = min(FFT_COLS, ncols)
    fh, fl = fmat
    cspec = pl.BlockSpec(fh.shape, lambda j: (0, 0))
    ospec = pl.BlockSpec((FFT_N1, cb), lambda j: (0, j))
    return pl.pallas_call(
        _fft1_kernel,
        grid=(ncols // cb,),
        in_specs=[pl.BlockSpec((FFT_N1, cb), lambda j: (0, j)), cspec, cspec],
        out_specs=[ospec, ospec],
        out_shape=[jax.ShapeDtypeStruct((FFT_N1, ncols), F32)] * 2,
        compiler_params=_cparams(("arbitrary",)),
    )(zview, fh, fl)


def _fftmid_kernel(ar_ref, ai_ref, twr_ref, twi_ref, f2h_ref, f2l_ref, *rest, filter_only, n2):
    kb, _, width = ar_ref.shape
    reps = width // LANES
    lanes = lambda parts: jnp.concatenate(parts, axis=1)
    twr = lanes([t for j in range(kb) for t in [twr_ref[j]] * reps])
    twi = lanes([t for j in range(kb) for t in [twi_ref[j]] * reps])
    ar = lanes([ar_ref[j] for j in range(kb)])
    ai = lanes([ai_ref[j] for j in range(kb)])
    z = jnp.concatenate([ar * twr - ai * twi, ar * twi + ai * twr], axis=0)
    x = _dot3c(f2h_ref[...], f2l_ref[...], z)
    xr, xi = x[:n2], x[n2:]
    if filter_only:
        kr_out, ki_out = rest
        for j in range(kb):
            kr_out[j] = xr[:, j * width:(j + 1) * width]
            ki_out[j] = xi[:, j * width:(j + 1) * width]
        return
    kr_ref, ki_ref, g2h_ref, g2l_ref, br_out, bi_out = rest
    kr = lanes([kr_ref[j] for j in range(kb)])
    ki = lanes([ki_ref[j] for j in range(kb)])
    y = jnp.concatenate([xr * kr - xi * ki, xr * ki + xi * kr], axis=0)
    w = _dot3c(g2h_ref[...], g2l_ref[...], y)
    wr, wi = w[:n2], w[n2:]
    br = wr * twr + wi * twi
    bi = wi * twr - wr * twi
    for j in range(kb):
        br_out[j] = br[:, j * width:(j + 1) * width]
        bi_out[j] = bi[:, j * width:(j + 1) * width]


def _fft_mid(ar, ai, consts, n2, width, spectrum=None):
    a3r = ar.reshape(FFT_N1, n2, width)
    a3i = ai.reshape(FFT_N1, n2, width)
    blk = pl.BlockSpec((FFT_KB, n2, width), lambda k: (k, 0, 0))
    twspec = pl.BlockSpec((FFT_KB, n2, LANES), lambda k: (k, 0, 0))
    cspec = pl.BlockSpec((2 * n2, 2 * n2), lambda k: (0, 0))
    in_specs = [blk, blk, twspec, twspec, cspec, cspec]
    args = [a3r, a3i, consts["twr"], consts["twi"], *consts["f2"]]
    if spectrum is not None:
        in_specs += [blk, blk, cspec, cspec]
        args += [spectrum[0], spectrum[1], *consts["g2"]]
    return pl.pallas_call(
        functools.partial(_fftmid_kernel, filter_only=spectrum is None, n2=n2),
        grid=(FFT_N1 // FFT_KB,),
        in_specs=in_specs,
        out_specs=[blk, blk],
        out_shape=[jax.ShapeDtypeStruct((FFT_N1, n2, width), F32)] * 2,
        compiler_params=_cparams(("arbitrary",)),
    )(*args)


def _fftlast_kernel(br_ref, bi_ref, gh_ref, gl_ref, u_ref, x0_ref, bias_ref, invn_ref, o_ref):
    b = jnp.concatenate([br_ref[...], bi_ref[...]], axis=0)
    y = _dot3c(gh_ref[...], gl_ref[...], b)
    o_ref[...] = ((y * invn_ref[...] + u_ref[...] * bias_ref[...]) * x0_ref[...]).astype(BF16)


def _fft_last(br, bi, gmat, uview, x0view, bias_t, invn_t, ncols):
    cb = min(FFT_COLS, ncols)
    gh, gl = gmat
    cspec = pl.BlockSpec(gh.shape, lambda j: (0, 0))
    dspec = pl.BlockSpec((FFT_N1, cb), lambda j: (0, j))
    vspec = pl.BlockSpec((1, cb), lambda j: (0, 0))
    return pl.pallas_call(
        _fftlast_kernel,
        grid=(ncols // cb,),
        in_specs=[dspec, dspec, cspec, cspec, dspec, dspec, vspec, vspec],
        out_specs=dspec,
        out_shape=jax.ShapeDtypeStruct((FFT_N1, ncols), BF16),
        compiler_params=_cparams(("arbitrary",)),
    )(br, bi, gh, gl, uview, x0view, bias_t, invn_t)


def _dft_small_consts(seq):
    n = 2 * seq
    a = -2.0 * np.pi * np.outer(np.arange(n), np.arange(n)) / n
    fr, fi = np.cos(a), np.sin(a)
    f_data = np.block([[fr[:, :seq], -fi[:, :seq]], [fi[:, :seq], fr[:, :seq]]])
    f_real = np.concatenate([fr, fi], axis=0)
    gr, gi = fr / n, -fi / n
    g = np.block([[gr[:seq], -gi[:seq]], [gi[:seq], gr[:seq]]])
    return _np_split(f_data), _np_split(f_real), _np_split(g)


def _hy_ctx_kernel(u_ref, x0_ref, taps_ref, fdh_ref, fdl_ref, frh_ref, frl_ref, gh_ref, gl_ref,
                   bias_ref, invn_ref, o_ref):
    n = taps_ref.shape[0]
    z = u_ref[...]
    a = _dot3c(fdh_ref[...], fdl_ref[...], z)
    k = _dot3c(frh_ref[...], frl_ref[...], taps_ref[...])
    ar, ai, kr, ki = a[:n], a[n:], k[:n], k[n:]
    y = jnp.concatenate([ar * kr - ai * ki, ar * ki + ai * kr], axis=0)
    conv = _dot3c(gh_ref[...], gl_ref[...], y)
    o_ref[...] = ((conv * invn_ref[...] + z * bias_ref[...]) * x0_ref[...]).astype(BF16)


def _hy_ctx(u, x0, taps, bias, invn, small):
    rows, hw = u.shape
    full = lambda a: pl.BlockSpec(a.shape, lambda i: (0,) * a.ndim)
    args = [u, x0, taps, *small[0], *small[1], *small[2], bias, invn]
    return pl.pallas_call(
        _hy_ctx_kernel,
        grid=(1,),
        in_specs=[full(a) for a in args],
        out_specs=pl.BlockSpec((rows, hw), lambda i: (0, 0)),
        out_shape=jax.ShapeDtypeStruct((rows, hw), BF16),
        compiler_params=_cparams(("arbitrary",)),
    )(*args)


def _out_proj_kernel(x_ref, ypl_ref, yhl_ref, yal_ref, ypc_ref, yhc_ref, yac_ref, w_ref, g_ref, mod_ref, *rest,
                     nlt, tpb, ncond, pw, hw, moe, ntiles):
    if moe:
        rw_ref, xo_ref, u_ref, route_ref, un_ref = rest
    else:
        wg_ref, wu_ref, wd_ref, xo_ref = rest
    t = jnp.minimum(pl.program_id(0), ntiles - 1)
    if moe:
        @pl.when(pl.program_id(0) == 0)
        def _():
            un_ref[...] = jnp.zeros_like(un_ref)

        logits = _dot3(un_ref[...], rw_ref[...])
    ci = _cond_row(t, nlt, tpb, ncond)
    is_ctx = t >= nlt
    yp = jnp.where(is_ctx, ypc_ref[...], ypl_ref[...])
    yh = jnp.where(is_ctx, yhc_ref[...], yhl_ref[...])
    ya = jnp.where(is_ctx, yac_ref[...], yal_ref[...])
    mix = _dot(yp, w_ref[0:pw, :]) + _dot(yh, w_ref[pw:pw + hw, :]) + _dot(ya, w_ref[pw + hw:, :])
    x = x_ref[...] + mod_ref[2, pl.ds(ci, 1), :] * mix
    un = _normmod(x, g_ref[...], mod_ref[3, pl.ds(ci, 1), :], mod_ref[4, pl.ds(ci, 1), :])
    if not moe:
        u = un.astype(BF16)
        ff = wg_ref.shape[1]
        fc = ff // FFN_CHUNKS
        y = jnp.zeros(x.shape, F32)
        for c in range(FFN_CHUNKS):
            gate = _dot(u, wg_ref[:, c * fc:(c + 1) * fc])
            up = _dot(u, wu_ref[:, c * fc:(c + 1) * fc])
            y = y + _dot((_silu(gate) * up).astype(BF16), wd_ref[c * fc:(c + 1) * fc, :])
        xo_ref[...] = x + mod_ref[5, pl.ds(ci, 1), :] * y
    else:
        xo_ref[...] = x
        u_ref[...] = un.astype(BF16)
        un_ref[...] = un
        lane = lax.broadcasted_iota(jnp.int32, logits.shape, 1)
        neg = jnp.float32(-jnp.inf)
        lg = jnp.where(lane < N_EXPERTS, logits, neg)
        t1 = jnp.max(lg, axis=-1, keepdims=True)
        i1 = jnp.min(jnp.where(lg == t1, lane, LANES), axis=-1, keepdims=True)
        lg2 = jnp.where(lane == i1, neg, lg)
        t2 = jnp.max(lg2, axis=-1, keepdims=True)
        i2 = jnp.min(jnp.where(lg2 == t2, lane, LANES), axis=-1, keepdims=True)
        e2 = jnp.exp(t2 - t1)
        g1 = 1.0 / (1.0 + e2)
        g2 = e2 / (1.0 + e2)
        route_ref[...] = jnp.where(lane == 0, i1.astype(F32), jnp.where(lane == 1, i2.astype(F32),
                                   jnp.where(lane == 2, g1, jnp.where(lane == 3, g2, 0.0))))


def _out_proj(x, lat, ctx, w, g, mod, rw, ffn_w, *, nlt, tpb, ncond, out_rows):
    n, d = x.shape
    pw, hw, aw = (a.shape[1] for a in lat)
    tm = ROW_TILE
    moe = rw is not None
    nt = (n if moe else out_rows) // tm
    cur = lambda t: jnp.minimum(t, nt - 1)
    row = lambda width: pl.BlockSpec((tm, width), lambda t: (cur(t), 0))
    lrow = lambda width: pl.BlockSpec((tm, width), lambda t: (jnp.minimum(t, nlt - 1), 0))
    crow = lambda width: pl.BlockSpec((tm, width), lambda t: (jnp.maximum(cur(t) - nlt, 0), 0))
    in_specs = [row(d), lrow(pw), lrow(hw), lrow(aw), crow(pw), crow(hw), crow(aw),
                pl.BlockSpec(w.shape, lambda t: (0, 0)),
                pl.BlockSpec((1, d), lambda t: (0, 0)),
                pl.BlockSpec(mod.shape, lambda t: (0, 0, 0))]
    args = [x, *lat, *ctx, w, g, mod]
    scratch = []
    if not moe:
        const = lambda a: pl.BlockSpec(a.shape, lambda t: (0, 0), pipeline_mode=pl.Buffered(1))
        in_specs += [const(a) for a in ffn_w]
        args += list(ffn_w)
        out_specs = [row(d)]
        out_shape = [jax.ShapeDtypeStruct((out_rows, d), F32)]
    else:
        out_specs = [row(d), row(d)]
        out_shape = [jax.ShapeDtypeStruct((n, d), F32), jax.ShapeDtypeStruct((n, d), BF16)]
        in_specs.append(pl.BlockSpec(rw.shape, lambda t: (0, 0)))
        args.append(rw)
        out_specs.append(pl.BlockSpec((tm, LANES), lambda t: (jnp.maximum(t - 1, 0), 0)))
        out_shape.append(jax.ShapeDtypeStruct((n, LANES), F32))
        scratch.append(pltpu.VMEM((tm, d), F32))
    kern = functools.partial(_out_proj_kernel, nlt=nlt, tpb=tpb, ncond=ncond, pw=pw, hw=hw, moe=moe, ntiles=nt)
    return pl.pallas_call(
        kern, grid=(nt + 1 if moe else nt,), in_specs=in_specs, out_specs=out_specs, out_shape=out_shape,
        scratch_shapes=scratch,
        compiler_params=_cparams(("arbitrary",)),
    )(*args)


def _moe_kernel(te_ref, tv_ref, u_ref, wg_ref, wu_ref, wd_ref, *rest):
    o_ref, acc_ref = rest[-2:]
    t = pl.program_id(0)
    f = pl.program_id(1)

    @pl.when(tv_ref[t] > 0)
    def _():
        u = u_ref[...]
        h = (_silu(_dot(u, wg_ref[0, 0])) * _dot(u, wu_ref[0, 0])).astype(BF16)
        y = _dot(h, wd_ref[0, 0])

        @pl.when(f == 0)
        def _():
            acc_ref[...] = y

        @pl.when(f != 0)
        def _():
            acc_ref[...] = acc_ref[...] + y

        @pl.when(f == pl.num_programs(1) - 1)
        def _():
            o_ref[...] = acc_ref[...].astype(BF16)


def _moe_experts(ug, tile_expert, tile_valid, wg, wu, wd, j, *, tile0, total_rows, prev=None):
    p, d = ug.shape
    ff = wg.shape[3]
    tm, fc = MOE_TM, MOE_FC
    in_specs = [
        pl.BlockSpec((tm, d), lambda t, f, te, tv: (t, 0)),
        pl.BlockSpec((1, 1, d, fc), lambda t, f, te, tv: (j, te[t], 0, f)),
        pl.BlockSpec((1, 1, d, fc), lambda t, f, te, tv: (j, te[t], 0, f)),
        pl.BlockSpec((1, 1, fc, d), lambda t, f, te, tv: (j, te[t], f, 0)),
    ]
    args = [tile_expert, tile_valid, ug, wg, wu, wd]
    aliases = {}
    if prev is not None:
        in_specs.append(pl.BlockSpec(memory_space=pl.ANY))
        args.append(prev)
        aliases = {len(args) - 1: 0}
    grid_spec = pltpu.PrefetchScalarGridSpec(
        num_scalar_prefetch=2,
        grid=(p // tm, ff // fc),
        in_specs=in_specs,
        out_specs=pl.BlockSpec((tm, d), lambda t, f, te, tv: (t + tile0, 0)),
        scratch_shapes=[pltpu.VMEM((tm, d), F32)],
    )
    return pl.pallas_call(
        _moe_kernel, grid_spec=grid_spec,
        out_shape=jax.ShapeDtypeStruct((total_rows, d), BF16),
        input_output_aliases=aliases,
        compiler_params=_cparams(("arbitrary", "arbitrary")),
    )(*args)


def _moe_combine_kernel(x_ref, ya_ref, yb_ref, route_ref, mod_ref, o_ref, *, nlt, tpb, ncond):
    ci = _cond_row(pl.program_id(0), nlt, tpb, ncond)
    r = route_ref[...]
    lane = lax.broadcasted_iota(jnp.int32, r.shape, 1)
    g1 = jnp.sum(jnp.where(lane == 2, r, 0.0), axis=-1, keepdims=True)
    g2 = jnp.sum(jnp.where(lane == 3, r, 0.0), axis=-1, keepdims=True)
    y = g1 * ya_ref[...].astype(F32) + g2 * yb_ref[...].astype(F32)
    o_ref[...] = x_ref[...] + mod_ref[5, pl.ds(ci, 1), :] * y


def _moe_combine(x, ya, yb, route, mod, *, nlt, tpb, ncond, out_rows):
    d = x.shape[1]
    tm = ROW_TILE
    row = lambda width: pl.BlockSpec((tm, width), lambda t: (t, 0))
    kern = functools.partial(_moe_combine_kernel, nlt=nlt, tpb=tpb, ncond=ncond)
    return pl.pallas_call(
        kern,
        grid=(out_rows // tm,),
        in_specs=[row(d), row(d), row(d), row(LANES), pl.BlockSpec(mod.shape, lambda t: (0, 0, 0))],
        out_specs=row(d),
        out_shape=jax.ShapeDtypeStruct((out_rows, d), F32),
        compiler_params=_cparams(("arbitrary",)),
    )(x, ya, yb, route, mod)


def _moe_layer(x, u, route, mod, wg, wu, wd, j, fill, *, nlt, tpb, ncond, out_rows):
    n, d = x.shape
    tm = MOE_TM
    experts = jnp.concatenate([route[:, 0], route[:, 1]]).astype(jnp.int32)
    onehot = (experts[:, None] == jnp.arange(N_EXPERTS, dtype=jnp.int32)[None, :]).astype(jnp.int32)
    rank = jnp.sum(onehot * (jnp.cumsum(onehot, axis=0) - 1), axis=1)
    counts = jnp.sum(onehot, axis=0)
    padded = ((counts + tm - 1) // tm) * tm
    ends = jnp.cumsum(padded)
    starts = ends - padded
    dest = starts[experts] + rank
    p = 2 * n + N_EXPERTS * tm
    tokens = jnp.concatenate([jnp.arange(n, dtype=jnp.int32)] * 2)
    inb = dict(mode="promise_in_bounds")
    src = jnp.zeros((p,), jnp.int32).at[dest].set(tokens, unique_indices=True, **inb)
    tile_start = jnp.arange(p // tm, dtype=jnp.int32) * tm
    tile_expert = jnp.minimum(jnp.searchsorted(ends, tile_start, side="right"), N_EXPERTS - 1).astype(jnp.int32)
    tile_valid = (tile_start < ends[-1]).astype(jnp.int32)
    nt0 = (p // tm) // 2
    ys = None
    for lo, hi in ((0, nt0), (nt0, p // tm)):
        ug = u.at[src[lo * tm:hi * tm]].get(**inb)
        if lo == 0:
            fill()
        ys = _moe_experts(ug, tile_expert[lo:hi], tile_valid[lo:hi], wg, wu, wd, j,
                          tile0=lo, total_rows=p, prev=ys)
    ya = ys.at[dest[:out_rows]].get(**inb)
    yb = ys.at[dest[n:n + out_rows]].get(**inb)
    return _moe_combine(x, ya, yb, route, mod, nlt=nlt, tpb=tpb, ncond=ncond, out_rows=out_rows)


def _rope_tables(seq, n_ctx_rows, batch):
    rows = seq // GRID_W
    row = jnp.repeat(jnp.arange(rows, dtype=F32), GRID_W)
    col = jnp.broadcast_to(jnp.arange(GRID_W, dtype=F32), (rows, GRID_W)).reshape(-1)
    inv_freq = jnp.power(ROPE_THETA, -jnp.arange(ROPE_FREQS, dtype=F32) / ROPE_FREQS)
    ar = row[:, None] * inv_freq
    ac = col[:, None] * inv_freq
    cos = jnp.concatenate([jnp.cos(ar), jnp.cos(ar), jnp.cos(ac), jnp.cos(ac)], axis=-1)
    sin = jnp.concatenate([-jnp.sin(ar), jnp.sin(ar), -jnp.sin(ac), jnp.sin(ac)], axis=-1)
    cos = jnp.tile(jnp.concatenate([cos, cos], axis=-1), (batch, 1))
    sin = jnp.tile(jnp.concatenate([sin, sin], axis=-1), (batch, 1))
    cos = jnp.concatenate([cos, jnp.ones((n_ctx_rows, 2 * HEAD_DIM), F32)], axis=0)
    sin = jnp.concatenate([sin, jnp.zeros((n_ctx_rows, 2 * HEAD_DIM), F32)], axis=0)
    return cos, sin


def kernel(x, c, ctx, c_ctx, mod_w, mod_b, norm1_g, norm2_g, w_in, w_out, pool_lin, pool_scale, hy_short_w, hy_short_b, hy_f_w1, hy_f_b1, hy_f_freq1, hy_f_w2, hy_f_b2, hy_f_freq2, hy_f_w3, hy_bias, qk_norm_g, diff_lambda, subln_g, ffn_w_gate, ffn_w_up, ffn_w_down, router_w, moe_w_gate, moe_w_up, moe_w_down):
    batch, seq, d = x.shape
    ctx_len = ctx.shape[1]
    depth = mod_w.shape[0]
    pw = pool_scale.shape[1]
    hw = hy_bias.shape[1]
    nl, nc = batch * seq, batch * ctx_len
    n = nl + nc
    tm = ROW_TILE
    assert seq % tm == 0 and nc % tm == 0 and seq % (FFT_N1 // 2) == 0 and seq % GRID_W == 0
    assert d == HEADS * 2 * V_DIM and pw == hw and batch == 2
    n2 = 2 * seq // FFT_N1
    ncols = n2 * hw
    assert (n * hw) % ncols == 0
    tiles = dict(nlt=nl // tm, tpb=seq // tm, ncond=batch + 1)

    xs = jnp.concatenate([x.reshape(nl, d), ctx.reshape(nc, d)], axis=0)
    cond8 = jnp.concatenate([c, c_ctx[None, :], jnp.zeros((8 - batch - 1, d), F32)], axis=0)
    mods = _modvec(cond8, mod_w, mod_b)
    cos_t, sin_t = _rope_tables(seq, nc, batch)
    consts = _dft_consts(n2)
    small = _dft_small_consts(ctx_len)
    eye = jnp.eye(len(POOL_WINDOWS), dtype=F32)
    n_moe = moe_w_gate.shape[0]
    assert n_moe <= depth
    moe_w2d = [w.reshape(-1, w.shape[-1]) for w in (moe_w_gate, moe_w_up, moe_w_down)]
    moe_bf16 = {}
    filter_memo = {}

    def filters(i):
        if i not in filter_memo:
            filt = (hy_f_w1[i], hy_f_b1[i], hy_f_freq1[i], hy_f_w2[i], hy_f_b2[i], hy_f_freq2[i], hy_f_w3[i])
            taps, asum = _hy_filter(seq, *filt, hw)
            kr1, ki1 = _fft_first(taps.reshape(FFT_N1, ncols), consts["f1_real"], ncols)
            taps_c, asum_c = _hy_filter(ctx_len, *filt, hw)
            filter_memo[i] = (_fft_mid(kr1, ki1, consts, n2, hw), asum, taps_c, asum_c)
        return filter_memo[i]

    for i in range(depth):
        last = i == depth - 1
        lam_init = 0.8 - 0.6 * math.exp(-0.3 * i)
        mod = mods[i]
        gqk = jnp.concatenate([qk_norm_g[i], qk_norm_g[i]], axis=-1)
        zp, zh, qt, k12, vt = _in_proj(xs, norm1_g[i][None, :], mod, w_in[i].astype(BF16), cos_t, sin_t, gqk,
                                       pw=pw, hw=hw, **tiles)

        g_b = jnp.broadcast_to(subln_g[i][:, None], (V_DIM, ATT_TQ))
        att_kw = dict(batch=batch, seq=seq, ctx_len=ctx_len, lam_init=lam_init)

        cast_j = i if i < n_moe else None

        def attend(direct, lam_p=diff_lambda[i], qt=qt, k12=k12, vt=vt, g_b=g_b, att_kw=att_kw, cast_j=cast_j):
            ctx_out = _attention(lam_p, qt, k12, vt, g_b, latent=False, direct=direct, **att_kw)
            if cast_j is None:
                return (_attention(lam_p, qt, k12, vt, g_b, latent=True, direct=direct, **att_kw), ctx_out)
            if direct:
                lat_out, *wb = _attention(lam_p, qt, k12, vt, g_b, latent=True, direct=True,
                                          cast=[(w, cast_j, n_moe) for w in moe_w2d], **att_kw)
            else:
                lat_out = _attention(lam_p, qt, k12, vt, g_b, latent=True, direct=False, **att_kw)
                wb = _cast_bf16(*[w[cast_j * (w.shape[0] // n_moe):(cast_j + 1) * (w.shape[0] // n_moe)]
                                  for w in moe_w2d])
            return (lat_out, ctx_out, *wb)

        bound = (HEAD_DIM * QSCALE) * jnp.max(jnp.abs(qk_norm_g[i][0])) * jnp.max(jnp.abs(qk_norm_g[i][1]))
        ya_l, ya_c, *wb = lax.cond(bound * 1.02 < ATT_DIRECT_MAX, lambda: attend(True), lambda: attend(False))
        if cast_j is not None:
            moe_bf16[cast_j] = tuple(w.reshape((1,) + src.shape[1:])
                                     for w, src in zip(wb, (moe_w_gate, moe_w_up, moe_w_down)))

        lin_bd = (eye[:, None, :, None] * pool_lin[i][:, :, None, :]).reshape(pw, pw).astype(BF16)
        pscale = pool_scale[i][None, :]
        yp_l = _pool(zp, lin_bd, pscale, row0=0, batch=batch, seq=seq, out_rows=nl, out_row0=0)
        yp_c = _pool(zp, lin_bd, pscale, row0=nl, batch=batch, seq=ctx_len, out_rows=nc, out_row0=0)

        sw, sb = hy_short_w[i], hy_short_b[i][None, :]
        u_l, x0_l = _hy_pre(zh, sw, sb, row0=0, batch=batch, seq=seq, out_rows=n, out_row0=0)
        u_c, x0_c = _hy_pre(zh, sw, sb, row0=nl, batch=batch, seq=ctx_len, out_rows=nc, out_row0=0)
        bias = hy_bias[i][None, :]
        spectrum, asum, taps_c, asum_c = filters(i)
        ar, ai = _fft_first(u_l.reshape(-1, ncols), consts["f1_data"], ncols)
        br, bi = _fft_mid(ar, ai, consts, n2, hw, spectrum=spectrum)
        reps = min(FFT_COLS, ncols) // hw
        yh_lat = _fft_last(br.reshape(FFT_N1, ncols), bi.reshape(FFT_N1, ncols), consts["g1"],
                           u_l.reshape(-1, ncols), x0_l.reshape(-1, ncols),
                           jnp.tile(bias, (1, reps)), jnp.tile(1.0 / asum, (1, reps)), ncols)
        yh_ctx = _hy_ctx(u_c, x0_c, taps_c, bias, 1.0 / asum_c, small)

        j = i // 2
        moe = i % 2 == 1
        rw = ffn_w = None
        if moe:
            rw = jnp.concatenate([router_w[j], jnp.zeros((d, LANES - N_EXPERTS), F32)], axis=1)
        else:
            ffn_w = (ffn_w_gate[j].astype(BF16), ffn_w_up[j].astype(BF16), ffn_w_down[j].astype(BF16))
        out_rows = nl if last else n
        outs = _out_proj(xs, (yp_l, yh_lat.reshape(nl, hw), ya_l), (yp_c, yh_ctx, ya_c), w_out[i].astype(BF16),
                         norm2_g[i][None, :], mod, rw, ffn_w, out_rows=out_rows, **tiles)
        if moe:
            xs, u, route = outs
            xs = _moe_layer(xs, u, route, mod, *moe_bf16[j], 0, lambda i=i: [filters(k) for k in range(i + 1, depth)],
                            out_rows=out_rows, **tiles)
        else:
            xs, = outs
    return xs[:nl].reshape(batch, seq, d)
```

```python
import functools
import math

import numpy as np
import jax
import jax.numpy as jnp
from jax import lax
from jax.experimental import pallas as pl
from jax.experimental.pallas import tpu as pltpu

F32 = jnp.float32
BF16 = jnp.bfloat16
EPS = 1e-6

GRID_W = 64
POOL_WINDOWS = (2, 4, 8, 16)
HEADS = 4
HEAD_DIM = 64
V_DIM = 128
ROPE_FREQS = 16
ROPE_THETA = 10000.0
HY_BANDS = 16
HY_TARGET = 1e-2
HY_FAST = 0.3
HY_SLOW = 1.5
N_EXPERTS = 8

LANES = 128
ROW_TILE = 512
HALO = 32
FFT_N1 = 128
FFT_COLS = 2048
FFT_KB = 4
ATT_TQ = 1024
ATT_TK = 512
ATT_UNROLL = 16
ATT_GROUP = 8
QSCALE = (HEAD_DIM ** -0.5) * math.log2(math.e)
ATT_DIRECT_MAX = 100.0
FFN_CHUNKS = 2
CAST_STEPS = 64
MOE_TM = 512
MOE_CHUNKS = 2
VMEM_LIMIT = 56 * 1024 * 1024


def _cparams(sem):
    return pltpu.CompilerParams(dimension_semantics=sem, vmem_limit_bytes=VMEM_LIMIT)


def _dot(a, b):
    return jnp.dot(a, b, preferred_element_type=F32)


def _split(a):
    hi = a.astype(BF16)
    lo = (a - hi.astype(F32)).astype(BF16)
    return hi, lo


def _dot3(a, b):
    ah, al = _split(a)
    bh, bl = _split(b)
    return _dot(ah, bh) + _dot(ah, bl) + _dot(al, bh)


def _dot3c(ch, cl, d):
    dh, dl = _split(d)
    return _dot(ch, dh) + _dot(ch, dl) + _dot(cl, dh)


def _silu(x):
    return x / (1.0 + jnp.exp(-x))


def _normmod(x, g, shift, scale):
    ms = jnp.mean(x * x, axis=-1, keepdims=True)
    return (x * lax.rsqrt(ms + EPS) * g) * (1.0 + scale) + shift


def _cond_row(t, n_lat_tiles, tiles_per_batch, n_cond):
    return jnp.where(t >= n_lat_tiles, n_cond - 1, t // tiles_per_batch)


def _cast_kernel(*refs):
    half = len(refs) // 2
    for w_ref, o_ref in zip(refs[:half], refs[half:]):
        o_ref[...] = w_ref[...].astype(BF16)


def _cast_bf16(*ws):
    flat = [w.reshape(-1, w.shape[-1]) for w in ws]
    steps = CAST_STEPS
    specs = [pl.BlockSpec((w.shape[0] // steps, w.shape[1]), lambda t: (t, 0)) for w in flat]
    outs = pl.pallas_call(
        _cast_kernel,
        grid=(steps,),
        in_specs=specs,
        out_specs=specs,
        out_shape=[jax.ShapeDtypeStruct(w.shape, BF16) for w in flat],
        compiler_params=_cparams(("arbitrary",)),
    )(*flat)
    return [o.reshape(w.shape) for o, w in zip(outs, ws)]


def _modvec_kernel(c_ref, w_ref, b_ref, o_ref):
    s = _silu(c_ref[...])
    o_ref[0, 0] = _dot3(s, w_ref[0]) + b_ref[0]


def _modvec(cond8, mod_w, mod_b):
    depth, d, six_d = mod_w.shape
    nchunk = six_d // d
    return pl.pallas_call(
        _modvec_kernel,
        grid=(depth, nchunk),
        in_specs=[
            pl.BlockSpec((8, d), lambda i, j: (0, 0)),
            pl.BlockSpec((1, d, d), lambda i, j: (i, 0, j)),
            pl.BlockSpec((1, 1, d), lambda i, j: (i, 0, j)),
        ],
        out_specs=pl.BlockSpec((1, 1, 8, d), lambda i, j: (i, j, 0, 0)),
        out_shape=jax.ShapeDtypeStruct((depth, nchunk, 8, d), F32),
        compiler_params=_cparams(("arbitrary", "arbitrary")),
    )(cond8, mod_w, mod_b.reshape(depth, 1, six_d))


def _in_proj_kernel(x_ref, g_ref, mod_ref, w_ref, cos_ref, sin_ref, gqk_ref,
                    zp_ref, zh_ref, qt_ref, k_ref, vt_ref, zatt_ref, *, nlt, tpb, ncond, pw, hw, ntiles):
    t = pl.program_id(0)

    @pl.when(t == 0)
    def _():
        zatt_ref[...] = jnp.zeros_like(zatt_ref)

    tm = x_ref.shape[0]
    att = pw + 3 * hw
    lane = lax.broadcasted_iota(jnp.int32, (tm, 2 * HEAD_DIM), 1)
    first = lane < HEAD_DIM
    apart = (lane % (2 * ROPE_FREQS)) < ROPE_FREQS
    cos = cos_ref[...]
    sin = sin_ref[...]
    qk_w = HEADS * 2 * HEAD_DIM

    def norm_rope(v, g):
        v2 = v * v
        s_all = jnp.sum(v2, axis=-1, keepdims=True)
        s_first = jnp.sum(jnp.where(first, v2, 0.0), axis=-1, keepdims=True)
        ms = jnp.where(first, s_first, s_all - s_first) * (1.0 / HEAD_DIM)
        vn = v * lax.rsqrt(ms + EPS) * g
        swapped = jnp.where(apart, pltpu.roll(vn, 2 * HEAD_DIM - ROPE_FREQS, 1), pltpu.roll(vn, ROPE_FREQS, 1))
        return vn * cos + swapped * sin

    for h in range(HEADS):
        lo = h * 2 * HEAD_DIM
        q = norm_rope(zatt_ref[:, lo:lo + 2 * HEAD_DIM], gqk_ref[0:1, :])
        qt_ref[h] = (q * QSCALE).T.astype(BF16)
        k = norm_rope(zatt_ref[:, qk_w + lo:qk_w + lo + 2 * HEAD_DIM], gqk_ref[1:2, :])
        k_ref[h] = k.astype(BF16)
        vlo = 2 * qk_w + h * V_DIM
        vt_ref[h] = zatt_ref[:, vlo:vlo + V_DIM].T.astype(BF16)

    ci = _cond_row(jnp.minimum(t, ntiles - 1), nlt, tpb, ncond)
    shift = mod_ref[0, pl.ds(ci, 1), :]
    scale = mod_ref[1, pl.ds(ci, 1), :]
    y = _normmod(x_ref[...], g_ref[...], shift, scale).astype(BF16)
    z = _dot(y, w_ref[...])
    zp_ref[...] = z[:, :pw].astype(BF16)
    zh_ref[...] = z[:, pw:att].astype(BF16)
    zatt_ref[...] = z[:, att:]


def _in_proj(x, g, mod, w, cos_t, sin_t, gqk, *, nlt, tpb, ncond, pw, hw):
    n, d = x.shape
    wid = w.shape[1]
    tm = ROW_TILE
    nt = n // tm
    kern = functools.partial(_in_proj_kernel, nlt=nlt, tpb=tpb, ncond=ncond, pw=pw, hw=hw, ntiles=nt)
    cur = lambda t: jnp.minimum(t, nt - 1)
    prv = lambda t: jnp.maximum(t - 1, 0)
    return pl.pallas_call(
        kern,
        grid=(nt + 1,),
        in_specs=[
            pl.BlockSpec((tm, d), lambda t: (cur(t), 0)),
            pl.BlockSpec((1, d), lambda t: (0, 0)),
            pl.BlockSpec(mod.shape, lambda t: (0, 0, 0)),
            pl.BlockSpec((d, wid), lambda t: (0, 0)),
            pl.BlockSpec((tm, 2 * HEAD_DIM), lambda t: (prv(t), 0)),
            pl.BlockSpec((tm, 2 * HEAD_DIM), lambda t: (prv(t), 0)),
            pl.BlockSpec((2, 2 * HEAD_DIM), lambda t: (0, 0)),
        ],
        out_specs=[
            pl.BlockSpec((tm, pw), lambda t: (cur(t), 0)),
            pl.BlockSpec((tm, 3 * hw), lambda t: (cur(t), 0)),
            pl.BlockSpec((HEADS, 2 * HEAD_DIM, tm), lambda t: (0, 0, prv(t))),
            pl.BlockSpec((HEADS, tm, 2 * HEAD_DIM), lambda t: (0, prv(t), 0)),
            pl.BlockSpec((HEADS, V_DIM, tm), lambda t: (0, 0, prv(t))),
        ],
        scratch_shapes=[pltpu.VMEM((tm, wid - pw - 3 * hw), F32)],
        out_shape=[
            jax.ShapeDtypeStruct((n, pw), BF16),
            jax.ShapeDtypeStruct((n, 3 * hw), BF16),
            jax.ShapeDtypeStruct((HEADS, 2 * HEAD_DIM, n), BF16),
            jax.ShapeDtypeStruct((HEADS, n, 2 * HEAD_DIM), BF16),
            jax.ShapeDtypeStruct((HEADS, V_DIM, n), BF16),
        ],
        compiler_params=_cparams(("arbitrary",)),
    )(x, g, mod, w, cos_t, sin_t, gqk)


def _attn_kernel(lam_ref, qt_ref, kc_ref, vtc_ref, *rest, tk, n_lat_chunks, lam_init):
    if n_lat_chunks:
        kl_ref, vtl_ref, g_ref, o_ref, m_ref, l_ref, acc_ref = rest
    else:
        g_ref, o_ref, m_ref, l_ref, acc_ref = rest
    qt = qt_ref[0]
    row = lax.broadcasted_iota(jnp.int32, qt.shape, 0)
    zero = jnp.zeros_like(qt)
    qmaps = (jnp.where(row < HEAD_DIM, qt, zero), jnp.where(row >= HEAD_DIM, qt, zero))

    def process(k_tile, vt_tile, first):
        for mi in range(2):
            s = _dot(k_tile, qmaps[mi])
            smax = jnp.max(s, axis=0, keepdims=True)
            if first:
                m_new = smax
            else:
                m_old = m_ref[mi]
                m_new = jnp.maximum(m_old, smax)
            p = jnp.exp2(s - m_new)
            psum = jnp.sum(p, axis=0, keepdims=True)
            pv = _dot(vt_tile, p.astype(BF16))
            if first:
                l_ref[mi] = psum
                acc_ref[mi] = pv
            else:
                alpha = jnp.exp2(m_old - m_new)
                l_ref[mi] = alpha * l_ref[mi] + psum
                acc_ref[mi] = alpha * acc_ref[mi] + pv
            m_ref[mi] = m_new

    process(kc_ref[0], vtc_ref[0], True)
    if n_lat_chunks:
        def body(j, carry):
            off = pl.multiple_of(j * tk, tk)
            process(kl_ref[0, pl.ds(off, tk), :], vtl_ref[0, :, pl.ds(off, tk)], False)
            return carry
        lax.fori_loop(0, n_lat_chunks, body, 0)

    lv = lam_ref[...]
    lam = (jnp.exp(jnp.sum(lv[0:1] * lv[1:2], axis=-1, keepdims=True))
           - jnp.exp(jnp.sum(lv[2:3] * lv[3:4], axis=-1, keepdims=True)) + lam_init)
    o = acc_ref[0] * (1.0 / l_ref[0]) - lam * (acc_ref[1] * (1.0 / l_ref[1]))
    ms = jnp.mean(o * o, axis=0, keepdims=True)
    y = o * lax.rsqrt(ms + EPS) * g_ref[...] * (1.0 - lam_init)
    o_ref[...] = y.T.astype(BF16)


def _attn_direct_kernel(lam_ref, qt_ref, kc_ref, vtc_ref, *rest, tk, n_lat_chunks, lam_init, n_cast=0):
    if n_lat_chunks:
        kl_ref, vtl_ref, g_ref = rest[:3]
        rest = rest[3:]
    else:
        g_ref = rest[0]
        rest = rest[1:]
    cast_in, (o_ref, *cast_out), (p_ref, l_ref, acc_ref) = rest[:n_cast], rest[n_cast:2 * n_cast + 1], rest[2 * n_cast + 1:]
    for w_ref, c_ref in zip(cast_in, cast_out):
        c_ref[...] = w_ref[...].astype(BF16)
    qt = qt_ref[0]
    tq = qt.shape[1]
    row = lax.broadcasted_iota(jnp.int32, qt.shape, 0)
    zero = jnp.zeros_like(qt)
    qmaps = (jnp.where(row < HEAD_DIM, qt, zero), jnp.where(row >= HEAD_DIM, qt, zero))
    grp = ATT_GROUP

    def keys(first_piece, count):
        if isinstance(first_piece, int):
            return pl.ds((first_piece - 1) * tk, count * tk)
        return pl.ds(pl.multiple_of((first_piece - 1) * tk, tk), count * tk)

    def stage_ab(k_tile, pset, r):
        for mi in range(2):
            p = jnp.exp2(_dot(k_tile, qmaps[mi]))
            l_ref[mi] = l_ref[mi] + jnp.sum(p.reshape(tk // 8, 8, tq), axis=0)
            p_ref[pset, mi, r * tk:(r + 1) * tk, :] = p.astype(BF16)

    def stage_c(vt_tile, pset, lo, count):
        for mi in range(2):
            acc_ref[mi] = acc_ref[mi] + _dot(vt_tile, p_ref[pset, mi, lo * tk:(lo + count) * tk, :])

    def group_ab(g, pset):
        k_tile = kl_ref[0, keys(grp * g, grp), :]
        for mi in range(2):
            p = jnp.exp2(_dot(k_tile, qmaps[mi]))
            l_ref[mi] = l_ref[mi] + jnp.sum(p.reshape(grp * tk // 8, 8, tq), axis=0)
            p_ref[pset, mi] = p.astype(BF16)

    def group_c(g, pset):
        stage_c(vtl_ref[0, :, keys(grp * g, grp)], pset, 0, grp)

    l_ref[...] = jnp.zeros_like(l_ref)
    acc_ref[...] = jnp.zeros_like(acc_ref)
    stage_ab(kc_ref[0], 0, 0)
    if n_lat_chunks == 0:
        stage_c(vtc_ref[0], 0, 0, 1)
    else:
        n_groups = n_lat_chunks // grp
        per_step = ATT_UNROLL // grp
        for r in range(1, grp):
            stage_ab(kl_ref[0, keys(r, 1), :], 0, r)
        group_ab(1, 1)
        stage_c(vtc_ref[0], 0, 0, 1)
        stage_c(vtl_ref[0, :, keys(1, grp - 1)], 0, 1, grp - 1)
        for g in range(2, per_step):
            group_ab(g, g % 2)
            group_c(g - 1, (g - 1) % 2)

        def body(jj, carry):
            g0 = per_step * (jj + 1)
            for q in range(per_step):
                group_ab(g0 + q, q % 2)
                group_c(g0 + q - 1, (q + 1) % 2)
            return carry

        lax.fori_loop(0, n_groups // per_step - 1, body, 0)
        stage_ab(kl_ref[0, keys(n_lat_chunks, 1), :], 0, 0)
        group_c(n_groups - 1, 1)
        stage_c(vtl_ref[0, :, keys(n_lat_chunks, 1)], 0, 0, 1)

    lv = lam_ref[...]
    lam = (jnp.exp(jnp.sum(lv[0:1] * lv[1:2], axis=-1, keepdims=True))
           - jnp.exp(jnp.sum(lv[2:3] * lv[3:4], axis=-1, keepdims=True)) + lam_init)
    l1 = jnp.sum(l_ref[0], axis=0, keepdims=True)
    l2 = jnp.sum(l_ref[1], axis=0, keepdims=True)
    o = acc_ref[0] * (1.0 / l1) - lam * (acc_ref[1] * (1.0 / l2))
    ms = jnp.mean(o * o, axis=0, keepdims=True)
    y = o * lax.rsqrt(ms + EPS) * g_ref[...] * (1.0 - lam_init)
    o_ref[...] = y.T.astype(BF16)


def _attention(lam_p, qt, k12, vt, g_b, *, batch, seq, ctx_len, latent, lam_init, direct, cast=()):
    n = k12.shape[1]
    nl = batch * seq
    tq = ATT_TQ if latent else ctx_len
    nq = seq // tq if latent else 1
    tk = ctx_len if direct else ATT_TK
    assert seq % ((ATT_UNROLL if direct else 1) * tk) == 0
    qbase = 0 if latent else nl // tq

    def qrow(b, iq):
        return (b * nq + iq) if latent else (qbase + b)

    in_specs = [
        pl.BlockSpec(lam_p.shape, lambda b, h, iq: (0, 0)),
        pl.BlockSpec((1, 2 * HEAD_DIM, tq), lambda b, h, iq: (h, 0, qrow(b, iq))),
        pl.BlockSpec((1, ctx_len, 2 * HEAD_DIM), lambda b, h, iq: (h, nl // ctx_len + b, 0)),
        pl.BlockSpec((1, V_DIM, ctx_len), lambda b, h, iq: (h, 0, nl // ctx_len + b)),
    ]
    args = [lam_p, qt, k12, vt]
    if latent:
        in_specs += [
            pl.BlockSpec((1, seq, 2 * HEAD_DIM), lambda b, h, iq: (h, b, 0)),
            pl.BlockSpec((1, V_DIM, seq), lambda b, h, iq: (h, 0, b)),
        ]
        args += [k12, vt]
    in_specs.append(pl.BlockSpec((V_DIM, tq), lambda b, h, iq: (0, 0)))
    args.append(g_b[:, :tq])
    nrows = nl if latent else batch * ctx_len
    body = _attn_direct_kernel if direct else _attn_kernel
    kw = dict(tk=tk, n_lat_chunks=(seq // tk if latent else 0), lam_init=lam_init)
    out_specs = [pl.BlockSpec((tq, V_DIM), lambda b, h, iq: (b * nq + iq, h))]
    out_shape = [jax.ShapeDtypeStruct((nrows, HEADS * V_DIM), BF16)]
    if cast:
        assert direct
        kw["n_cast"] = len(cast)
        steps = batch * HEADS * nq
        for w, slab, parts in cast:
            rows = w.shape[0] // parts // steps
            in_specs.append(pl.BlockSpec(
                (rows, w.shape[1]), lambda b, h, iq, slab=slab: (slab * steps + (b * HEADS + h) * nq + iq, 0)))
            args.append(w)
            out_specs.append(pl.BlockSpec((rows, w.shape[1]), lambda b, h, iq: ((b * HEADS + h) * nq + iq, 0)))
            out_shape.append(jax.ShapeDtypeStruct((w.shape[0] // parts, w.shape[1]), BF16))
    kern = functools.partial(body, **kw)
    if direct:
        scratch = [pltpu.VMEM((2, 2, ATT_GROUP * tk, tq), BF16),
                   pltpu.VMEM((2, 8, tq), F32), pltpu.VMEM((2, V_DIM, tq), F32)]
    else:
        scratch = [pltpu.VMEM((2, 1, tq), F32), pltpu.VMEM((2, 1, tq), F32), pltpu.VMEM((2, V_DIM, tq), F32)]
    outs = pl.pallas_call(
        kern,
        grid=(batch, HEADS, nq),
        in_specs=in_specs,
        out_specs=out_specs,
        out_shape=out_shape,
        scratch_shapes=scratch,
        compiler_params=_cparams(("arbitrary", "arbitrary", "arbitrary")),
    )(*args)
    return outs if cast else outs[0]


def _halo_specs(width, *, row0, seq, tile):
    hb = tile // HALO

    def cur(b, i):
        return ((row0 + b * seq) // tile + i, 0)

    def prev(b, i):
        first = (row0 + b * seq) // HALO
        return (jnp.maximum(first + i * hb - 1, first), 0)

    def nxt(b, i):
        first = (row0 + b * seq) // HALO
        return (jnp.minimum(first + (i + 1) * hb, first + seq // HALO - 1), 0)

    return [pl.BlockSpec((HALO, width), prev), pl.BlockSpec((tile, width), cur), pl.BlockSpec((HALO, width), nxt)]


def _fill_ext(ext_ref, prev_ref, cur_ref, next_ref, *, seq, tile):
    pos0 = pl.program_id(1) * tile
    width = cur_ref.shape[1]
    hpos = lax.broadcasted_iota(jnp.int32, (HALO, width), 0)
    ext_ref[0:HALO, :] = jnp.where(pos0 - HALO + hpos >= 0, prev_ref[...].astype(F32), 0.0)
    ext_ref[HALO:HALO + tile, :] = cur_ref[...].astype(F32)
    ext_ref[HALO + tile:, :] = jnp.where(pos0 + tile + hpos < seq, next_ref[...].astype(F32), 0.0)


def _pool_kernel(prev_ref, cur_ref, next_ref, lin_ref, scale_ref, o_ref, ext_ref, s_ref, *, seq, tile):
    _fill_ext(ext_ref, prev_ref, cur_ref, next_ref, seq=seq, tile=tile)
    width = cur_ref.shape[1]
    gd = width // len(POOL_WINDOWS)
    n0 = tile + 2 * HALO
    s_ref[0, 0:n0 - 8, :] = ext_ref[0:n0 - 8, :] + ext_ref[1:n0 - 7, :]
    for k in range(1, len(POOL_WINDOWS)):
        step = 1 << k
        ln = n0 - 8 * (k + 1)
        s_ref[k, 0:ln, :] = s_ref[k - 1, 0:ln, :] + s_ref[k - 1, step:step + ln, :]
    lane = lax.broadcasted_iota(jnp.int32, (tile, width), 1)
    pos = pl.program_id(1) * tile + lax.broadcasted_iota(jnp.int32, (tile, width), 0)
    grp = lane // gd
    wsum = jnp.zeros((tile, width), F32)
    half = jnp.zeros((tile, width), jnp.int32)
    for k, win in enumerate(POOL_WINDOWS):
        start = HALO - win // 2
        wsum = jnp.where(grp == k, s_ref[k, start:start + tile, :], wsum)
        half = jnp.where(grp == k, win // 2, half)
    cnt = jnp.minimum(pos + half, seq) - jnp.maximum(pos - half, 0)
    z = ext_ref[HALO:HALO + tile, :]
    dlt = (wsum / cnt.astype(F32) - z).astype(BF16)
    o_ref[...] = (_dot(dlt, lin_ref[...]) * scale_ref[...]).astype(BF16)


def _pool(zp, lin_bd, scale, *, row0, batch, seq, out_rows, out_row0):
    width = zp.shape[1]
    tile = min(ROW_TILE, seq)
    kern = functools.partial(_pool_kernel, seq=seq, tile=tile)
    return pl.pallas_call(
        kern,
        grid=(batch, seq // tile),
        in_specs=_halo_specs(width, row0=row0, seq=seq, tile=tile) + [
            pl.BlockSpec((width, width), lambda b, i: (0, 0)),
            pl.BlockSpec((1, width), lambda b, i: (0, 0)),
        ],
        out_specs=pl.BlockSpec((tile, width), lambda b, i: ((out_row0 + b * seq) // tile + i, 0)),
        out_shape=jax.ShapeDtypeStruct((out_rows, width), BF16),
        scratch_shapes=[
            pltpu.VMEM((tile + 2 * HALO, width), F32),
            pltpu.VMEM((4, tile + 2 * HALO, width), F32),
        ],
        compiler_params=_cparams(("arbitrary", "arbitrary")),
    )(zp, zp, zp, lin_bd, scale)


def _hy_pre_kernel(prev_ref, cur_ref, next_ref, w_ref, b_ref, u_ref, x0_ref, ext_ref, *, seq, tile, hw):
    _fill_ext(ext_ref, prev_ref, cur_ref, next_ref, seq=seq, tile=tile)
    y = b_ref[...] + ext_ref[HALO - 1:HALO - 1 + tile, :] * w_ref[0:1, :]
    y = y + ext_ref[HALO:HALO + tile, :] * w_ref[1:2, :]
    y = y + ext_ref[HALO + 1:HALO + 1 + tile, :] * w_ref[2:3, :]
    x0_ref[...] = y[:, :hw]
    u_ref[...] = y[:, 2 * hw:] * y[:, hw:2 * hw]


def _hy_pre(zh, sw, sb, *, row0, batch, seq, out_rows, out_row0):
    width = zh.shape[1]
    hw = width // 3
    tile = min(ROW_TILE, seq)
    kern = functools.partial(_hy_pre_kernel, seq=seq, tile=tile, hw=hw)
    ospec = pl.BlockSpec((tile, hw), lambda b, i: ((out_row0 + b * seq) // tile + i, 0))
    return pl.pallas_call(
        kern,
        grid=(batch, seq // tile),
        in_specs=_halo_specs(width, row0=row0, seq=seq, tile=tile) + [
            pl.BlockSpec((3, width), lambda b, i: (0, 0)),
            pl.BlockSpec((1, width), lambda b, i: (0, 0)),
        ],
        out_specs=[ospec, ospec],
        out_shape=[jax.ShapeDtypeStruct((out_rows, hw), F32), jax.ShapeDtypeStruct((out_rows, hw), F32)],
        scratch_shapes=[pltpu.VMEM((tile + 2 * HALO, width), F32)],
        compiler_params=_cparams(("arbitrary", "arbitrary")),
    )(zh, zh, zh, sw, sb)


def _filter_kernel(feat_ref, w1_ref, b1_ref, f1_ref, w2_ref, b2_ref, f2_ref, w3_ref, dl_ref,
                   taps_ref, asum_ref, *, tile, hw, seq, embp):
    feat = feat_ref[...]
    h = jnp.sin(f1_ref[...] * (_dot3(feat, w1_ref[...]) + b1_ref[...]))
    h = jnp.sin(f2_ref[...] * (_dot3(h, w2_ref[...]) + b2_ref[...]))
    h = _dot3(h, w3_ref[...])
    odd = lax.broadcasted_iota(jnp.int32, (tile, 2 * hw), 1) >= hw
    tcol = jnp.where(odd, feat[:, embp:embp + 1], feat[:, 0:1])
    dec = jnp.exp(-tcol * dl_ref[...])
    row = pl.program_id(0) * tile + lax.broadcasted_iota(jnp.int32, (tile, 2 * hw), 0)
    pos = 2 * row + odd.astype(jnp.int32)
    taps = jnp.where(pos == seq, 0.0, h * dec)
    taps_ref[...] = taps
    part = jnp.sum(jnp.abs(taps), axis=0, keepdims=True)

    @pl.when(pl.program_id(0) == 0)
    def _():
        asum_ref[...] = part

    @pl.when(pl.program_id(0) != 0)
    def _():
        asum_ref[...] = asum_ref[...] + part


def _hy_filter(seq, w1, b1, f1, w2, b2, f2, w3, hw):
    ar = jnp.arange(seq, dtype=jnp.int32)
    pos = jnp.concatenate([ar, (seq - ar) % seq]).astype(F32)[:, None]
    t = pos * (1.0 / (seq - 1))
    w = (2.0 * math.pi / seq) * pos
    bands = jnp.linspace(1e-4, HY_BANDS - 1, HY_BANDS, dtype=F32)[None, :]
    emb = 1 + 2 * HY_BANDS
    embp = ((emb + 7) // 8) * 8
    feat = jnp.concatenate([t, jnp.cos(bands * w), -jnp.sin(bands * w), jnp.zeros((2 * seq, embp - emb), F32)], axis=-1)
    w1p = jnp.concatenate([w1, jnp.zeros((embp - emb, w1.shape[1]), F32)], axis=0)
    max_decay = math.log(1.0 / HY_TARGET) / HY_FAST
    min_decay = math.log(1.0 / HY_TARGET) / HY_SLOW
    deltas = jnp.linspace(min_decay, max_decay, hw, dtype=F32)[None, :]
    feat = feat.reshape(seq, 2 * embp)
    hid = w1.shape[1]

    def pair(m):
        z = jnp.zeros_like(m)
        return jnp.concatenate([jnp.concatenate([m, z], axis=1), jnp.concatenate([z, m], axis=1)], axis=0)

    twice = lambda v: jnp.concatenate([v, v])[None, :]
    w3p = jnp.concatenate([pair(w3[:, :hw]), pair(w3[:, hw:])], axis=1)
    tile = min(ROW_TILE, seq) // 2
    nfwd = seq // (2 * tile)
    full = lambda shape: pl.BlockSpec(shape, lambda i: (0,) * len(shape))
    taps, asum = pl.pallas_call(
        functools.partial(_filter_kernel, tile=tile, hw=hw, seq=seq, embp=embp),
        grid=(2 * nfwd,),
        in_specs=[
            pl.BlockSpec((tile, 2 * embp), lambda i: (i, 0)),
            full((2 * embp, 2 * hid)), full((1, 2 * hid)), full((1, 2 * hid)),
            full((2 * hid, 2 * hid)), full((1, 2 * hid)), full((1, 2 * hid)),
            pl.BlockSpec((2 * hid, 2 * hw), lambda i: (0, i // nfwd)), full((1, 2 * hw)),
        ],
        out_specs=[
            pl.BlockSpec((tile, 2 * hw), lambda i: (i, 0)),
            pl.BlockSpec((1, 2 * hw), lambda i: (0, 0)),
        ],
        out_shape=[
            jax.ShapeDtypeStruct((seq, 2 * hw), F32),
            jax.ShapeDtypeStruct((1, 2 * hw), F32),
        ],
        compiler_params=_cparams(("arbitrary",)),
    )(feat, pair(w1p), twice(b1), twice(f1), pair(w2), twice(b2), twice(f2), w3p,
      jnp.concatenate([deltas, deltas], axis=1))
    return taps.reshape(2 * seq, hw), asum[:, :hw] + asum[:, hw:]


def _np_split(a):
    a32 = jnp.asarray(a, F32)
    hi = a32.astype(BF16)
    lo = (a32 - hi.astype(F32)).astype(BF16)
    return hi, lo


def _dft_consts(n2):
    n1 = FFT_N1
    n = n1 * n2
    half = n1 // 2
    a1 = -2.0 * np.pi * np.outer(np.arange(n1), np.arange(n1)) / n1
    f1r, f1i = np.cos(a1), np.sin(a1)
    f1_data = np.block([[f1r[:, :half], -f1i[:, :half]], [f1i[:, :half], f1r[:, :half]]])
    f1_real = np.concatenate([f1r, f1i], axis=0)
    g1r, g1i = f1r / n, -f1i / n
    g1 = np.block([[g1r[:half], -g1i[:half]], [g1i[:half], g1r[:half]]])
    a2 = -2.0 * np.pi * np.outer(np.arange(n2), np.arange(n2)) / n2
    f2r, f2i = np.cos(a2), np.sin(a2)
    f2 = np.block([[f2r, -f2i], [f2i, f2r]])
    g2 = np.block([[f2r, f2i], [-f2i, f2r]])
    at = -2.0 * np.pi * np.outer(np.arange(n1), np.arange(n2)) / n
    twr = jnp.broadcast_to(jnp.asarray(np.cos(at), F32)[:, :, None], (n1, n2, LANES))
    twi = jnp.broadcast_to(jnp.asarray(np.sin(at), F32)[:, :, None], (n1, n2, LANES))
    return dict(f1_data=_np_split(f1_data), f1_real=_np_split(f1_real), g1=_np_split(g1),
                f2=_np_split(f2), g2=_np_split(g2), twr=twr, twi=twi)


def _fft1_kernel(z_ref, fh_ref, fl_ref, ar_ref, ai_ref):
    a = _dot3c(fh_ref[...], fl_ref[...], z_ref[...])
    ar_ref[...] = a[:FFT_N1]
    ai_ref[...] = a[FFT_N1:]


def _fft_first(zview, fmat, ncols):
    cb = min(FFT_COLS, ncols)
    fh, fl = fmat
    cspec = pl.BlockSpec(fh.shape, lambda j: (0, 0))
    ospec = pl.BlockSpec((FFT_N1, cb), lambda j: (0, j))
    return pl.pallas_call(
        _fft1_kernel,
        grid=(ncols // cb,),
        in_specs=[pl.BlockSpec((FFT_N1, cb), lambda j: (0, j)), cspec, cspec],
        out_specs=[ospec, ospec],
        out_shape=[jax.ShapeDtypeStruct((FFT_N1, ncols), F32)] * 2,
        compiler_params=_cparams(("arbitrary",)),
    )(zview, fh, fl)


def _fftmid_kernel(ar_ref, ai_ref, twr_ref, twi_ref, f2h_ref, f2l_ref, *rest, filter_only, n2):
    kb, _, width = ar_ref.shape
    reps = width // LANES
    lanes = lambda parts: jnp.concatenate(parts, axis=1)
    twr = lanes([t for j in range(kb) for t in [twr_ref[j]] * reps])
    twi = lanes([t for j in range(kb) for t in [twi_ref[j]] * reps])
    ar = lanes([ar_ref[j] for j in range(kb)])
    ai = lanes([ai_ref[j] for j in range(kb)])
    z = jnp.concatenate([ar * twr - ai * twi, ar * twi + ai * twr], axis=0)
    x = _dot3c(f2h_ref[...], f2l_ref[...], z)
    xr, xi = x[:n2], x[n2:]
    if filter_only:
        kr_out, ki_out = rest
        for j in range(kb):
            kr_out[j] = xr[:, j * width:(j + 1) * width]
            ki_out[j] = xi[:, j * width:(j + 1) * width]
        return
    kr_ref, ki_ref, g2h_ref, g2l_ref, br_out, bi_out = rest
    kr = lanes([kr_ref[j] for j in range(kb)])
    ki = lanes([ki_ref[j] for j in range(kb)])
    y = jnp.concatenate([xr * kr - xi * ki, xr * ki + xi * kr], axis=0)
    w = _dot3c(g2h_ref[...], g2l_ref[...], y)
    wr, wi = w[:n2], w[n2:]
    br = wr * twr + wi * twi
    bi = wi * twr - wr * twi
    for j in range(kb):
        br_out[j] = br[:, j * width:(j + 1) * width]
        bi_out[j] = bi[:, j * width:(j + 1) * width]


def _fft_mid(ar, ai, consts, n2, width, spectrum=None):
    a3r = ar.reshape(FFT_N1, n2, width)
    a3i = ai.reshape(FFT_N1, n2, width)
    blk = pl.BlockSpec((FFT_KB, n2, width), lambda k: (k, 0, 0))
    twspec = pl.BlockSpec((FFT_KB, n2, LANES), lambda k: (k, 0, 0))
    cspec = pl.BlockSpec((2 * n2, 2 * n2), lambda k: (0, 0))
    in_specs = [blk, blk, twspec, twspec, cspec, cspec]
    args = [a3r, a3i, consts["twr"], consts["twi"], *consts["f2"]]
    if spectrum is not None:
        in_specs += [blk, blk, cspec, cspec]
        args += [spectrum[0], spectrum[1], *consts["g2"]]
    return pl.pallas_call(
        functools.partial(_fftmid_kernel, filter_only=spectrum is None, n2=n2),
        grid=(FFT_N1 // FFT_KB,),
        in_specs=in_specs,
        out_specs=[blk, blk],
        out_shape=[jax.ShapeDtypeStruct((FFT_N1, n2, width), F32)] * 2,
        compiler_params=_cparams(("arbitrary",)),
    )(*args)


def _fftlast_kernel(br_ref, bi_ref, gh_ref, gl_ref, u_ref, x0_ref, bias_ref, invn_ref, o_ref):
    b = jnp.concatenate([br_ref[...], bi_ref[...]], axis=0)
    y = _dot3c(gh_ref[...], gl_ref[...], b)
    o_ref[...] = ((y * invn_ref[...] + u_ref[...] * bias_ref[...]) * x0_ref[...]).astype(BF16)


def _fft_last(br, bi, gmat, uview, x0view, bias_t, invn_t, ncols):
    cb = min(FFT_COLS, ncols)
    gh, gl = gmat
    cspec = pl.BlockSpec(gh.shape, lambda j: (0, 0))
    dspec = pl.BlockSpec((FFT_N1, cb), lambda j: (0, j))
    vspec = pl.BlockSpec((1, cb), lambda j: (0, 0))
    return pl.pallas_call(
        _fftlast_kernel,
        grid=(ncols // cb,),
        in_specs=[dspec, dspec, cspec, cspec, dspec, dspec, vspec, vspec],
        out_specs=dspec,
        out_shape=jax.ShapeDtypeStruct((FFT_N1, ncols), BF16),
        compiler_params=_cparams(("arbitrary",)),
    )(br, bi, gh, gl, uview, x0view, bias_t, invn_t)


def _dft_small_consts(seq):
    n = 2 * seq
    a = -2.0 * np.pi * np.outer(np.arange(n), np.arange(n)) / n
    fr, fi = np.cos(a), np.sin(a)
    f_data = np.block([[fr[:, :seq], -fi[:, :seq]], [fi[:, :seq], fr[:, :seq]]])
    f_real = np.concatenate([fr, fi], axis=0)
    gr, gi = fr / n, -fi / n
    g = np.block([[gr[:seq], -gi[:seq]], [gi[:seq], gr[:seq]]])
    return _np_split(f_data), _np_split(f_real), _np_split(g)


def _hy_ctx_kernel(u_ref, x0_ref, taps_ref, fdh_ref, fdl_ref, frh_ref, frl_ref, gh_ref, gl_ref,
                   bias_ref, invn_ref, o_ref):
    n = taps_ref.shape[0]
    z = u_ref[...]
    a = _dot3c(fdh_ref[...], fdl_ref[...], z)
    k = _dot3c(frh_ref[...], frl_ref[...], taps_ref[...])
    ar, ai, kr, ki = a[:n], a[n:], k[:n], k[n:]
    y = jnp.concatenate([ar * kr - ai * ki, ar * ki + ai * kr], axis=0)
    conv = _dot3c(gh_ref[...], gl_ref[...], y)
    o_ref[...] = ((conv * invn_ref[...] + z * bias_ref[...]) * x0_ref[...]).astype(BF16)


def _hy_ctx(u, x0, taps, bias, invn, small):
    rows, hw = u.shape
    full = lambda a: pl.BlockSpec(a.shape, lambda i: (0,) * a.ndim)
    args = [u, x0, taps, *small[0], *small[1], *small[2], bias, invn]
    return pl.pallas_call(
        _hy_ctx_kernel,
        grid=(1,),
        in_specs=[full(a) for a in args],
        out_specs=pl.BlockSpec((rows, hw), lambda i: (0, 0)),
        out_shape=jax.ShapeDtypeStruct((rows, hw), BF16),
        compiler_params=_cparams(("arbitrary",)),
    )(*args)


def _out_proj_kernel(x_ref, ypl_ref, yhl_ref, yal_ref, ypc_ref, yhc_ref, yac_ref, w_ref, g_ref, mod_ref, *rest,
                     nlt, tpb, ncond, pw, hw, moe, ntiles):
    if moe:
        rw_ref, xo_ref, u_ref, route_ref, un_ref = rest
    else:
        wg_ref, wu_ref, wd_ref, xo_ref = rest
    t = jnp.minimum(pl.program_id(0), ntiles - 1)
    if moe:
        @pl.when(pl.program_id(0) == 0)
        def _():
            un_ref[...] = jnp.zeros_like(un_ref)

        logits = _dot3(un_ref[...], rw_ref[...])
    ci = _cond_row(t, nlt, tpb, ncond)
    is_ctx = t >= nlt
    yp = jnp.where(is_ctx, ypc_ref[...], ypl_ref[...])
    yh = jnp.where(is_ctx, yhc_ref[...], yhl_ref[...])
    ya = jnp.where(is_ctx, yac_ref[...], yal_ref[...])
    mix = _dot(yp, w_ref[0:pw, :]) + _dot(yh, w_ref[pw:pw + hw, :]) + _dot(ya, w_ref[pw + hw:, :])
    x = x_ref[...] + mod_ref[2, pl.ds(ci, 1), :] * mix
    un = _normmod(x, g_ref[...], mod_ref[3, pl.ds(ci, 1), :], mod_ref[4, pl.ds(ci, 1), :])
    if not moe:
        u = un.astype(BF16)
        ff = wg_ref.shape[1]
        fc = ff // FFN_CHUNKS
        y = jnp.zeros(x.shape, F32)
        for c in range(FFN_CHUNKS):
            gate = _dot(u, wg_ref[:, c * fc:(c + 1) * fc])
            up = _dot(u, wu_ref[:, c * fc:(c + 1) * fc])
            y = y + _dot((_silu(gate) * up).astype(BF16), wd_ref[c * fc:(c + 1) * fc, :])
        xo_ref[...] = x + mod_ref[5, pl.ds(ci, 1), :] * y
    else:
        xo_ref[...] = x
        u_ref[...] = un.astype(BF16)
        un_ref[...] = un
        lane = lax.broadcasted_iota(jnp.int32, logits.shape, 1)
        neg = jnp.float32(-jnp.inf)
        lg = jnp.where(lane < N_EXPERTS, logits, neg)
        t1 = jnp.max(lg, axis=-1, keepdims=True)
        i1 = jnp.min(jnp.where(lg == t1, lane, LANES), axis=-1, keepdims=True)
        lg2 = jnp.where(lane == i1, neg, lg)
        t2 = jnp.max(lg2, axis=-1, keepdims=True)
        i2 = jnp.min(jnp.where(lg2 == t2, lane, LANES), axis=-1, keepdims=True)
        e2 = jnp.exp(t2 - t1)
        g1 = 1.0 / (1.0 + e2)
        g2 = e2 / (1.0 + e2)
        route_ref[...] = jnp.where(lane == 0, i1.astype(F32), jnp.where(lane == 1, i2.astype(F32),
                                   jnp.where(lane == 2, g1, jnp.where(lane == 3, g2, 0.0))))


def _out_proj(x, lat, ctx, w, g, mod, rw, ffn_w, *, nlt, tpb, ncond, out_rows):
    n, d = x.shape
    pw, hw, aw = (a.shape[1] for a in lat)
    tm = ROW_TILE
    moe = rw is not None
    nt = (n if moe else out_rows) // tm
    cur = lambda t: jnp.minimum(t, nt - 1)
    row = lambda width: pl.BlockSpec((tm, width), lambda t: (cur(t), 0))
    lrow = lambda width: pl.BlockSpec((tm, width), lambda t: (jnp.minimum(t, nlt - 1), 0))
    crow = lambda width: pl.BlockSpec((tm, width), lambda t: (jnp.maximum(cur(t) - nlt, 0), 0))
    in_specs = [row(d), lrow(pw), lrow(hw), lrow(aw), crow(pw), crow(hw), crow(aw),
                pl.BlockSpec(w.shape, lambda t: (0, 0)),
                pl.BlockSpec((1, d), lambda t: (0, 0)),
                pl.BlockSpec(mod.shape, lambda t: (0, 0, 0))]
    args = [x, *lat, *ctx, w, g, mod]
    scratch = []
    if not moe:
        const = lambda a: pl.BlockSpec(a.shape, lambda t: (0, 0), pipeline_mode=pl.Buffered(1))
        in_specs += [const(a) for a in ffn_w]
        args += list(ffn_w)
        out_specs = [row(d)]
        out_shape = [jax.ShapeDtypeStruct((out_rows, d), F32)]
    else:
        out_specs = [row(d), row(d)]
        out_shape = [jax.ShapeDtypeStruct((n, d), F32), jax.ShapeDtypeStruct((n, d), BF16)]
        in_specs.append(pl.BlockSpec(rw.shape, lambda t: (0, 0)))
        args.append(rw)
        out_specs.append(pl.BlockSpec((tm, LANES), lambda t: (jnp.maximum(t - 1, 0), 0)))
        out_shape.append(jax.ShapeDtypeStruct((n, LANES), F32))
        scratch.append(pltpu.VMEM((tm, d), F32))
    kern = functools.partial(_out_proj_kernel, nlt=nlt, tpb=tpb, ncond=ncond, pw=pw, hw=hw, moe=moe, ntiles=nt)
    return pl.pallas_call(
        kern, grid=(nt + 1 if moe else nt,), in_specs=in_specs, out_specs=out_specs, out_shape=out_shape,
        scratch_shapes=scratch,
        compiler_params=_cparams(("arbitrary",)),
    )(*args)


def _moe_kernel(te_ref, tv_ref, u_ref, wg_ref, wu_ref, wd_ref, *rest):
    o_ref = rest[-1]
    @pl.when(tv_ref[pl.program_id(0)] > 0)
    def _():
        u = u_ref[...]
        ff = wg_ref.shape[3]
        fc = ff // MOE_CHUNKS
        y = jnp.zeros(o_ref.shape, F32)
        for c in range(MOE_CHUNKS):
            sl = slice(c * fc, (c + 1) * fc)
            h = (_silu(_dot(u, wg_ref[0, 0, :, sl])) * _dot(u, wu_ref[0, 0, :, sl])).astype(BF16)
            y = y + _dot(h, wd_ref[0, 0, sl, :])
        o_ref[...] = y.astype(BF16)


def _moe_experts(ug, tile_expert, tile_valid, wg, wu, wd, j, *, tile0, total_rows, prev=None):
    p, d = ug.shape
    ff = wg.shape[3]
    tm = MOE_TM
    one = pl.Buffered(1)
    in_specs = [
        pl.BlockSpec((tm, d), lambda t, te, tv: (t, 0)),
        pl.BlockSpec((1, 1, d, ff), lambda t, te, tv: (j, te[t], 0, 0), pipeline_mode=one),
        pl.BlockSpec((1, 1, d, ff), lambda t, te, tv: (j, te[t], 0, 0), pipeline_mode=one),
        pl.BlockSpec((1, 1, ff, d), lambda t, te, tv: (j, te[t], 0, 0), pipeline_mode=one),
    ]
    args = [tile_expert, tile_valid, ug, wg, wu, wd]
    aliases = {}
    if prev is not None:
        in_specs.append(pl.BlockSpec(memory_space=pl.ANY))
        args.append(prev)
        aliases = {len(args) - 1: 0}
    grid_spec = pltpu.PrefetchScalarGridSpec(
        num_scalar_prefetch=2,
        grid=(p // tm,),
        in_specs=in_specs,
        out_specs=pl.BlockSpec((tm, d), lambda t, te, tv: (t + tile0, 0)),
    )
    return pl.pallas_call(
        _moe_kernel, grid_spec=grid_spec,
        out_shape=jax.ShapeDtypeStruct((total_rows, d), BF16),
        input_output_aliases=aliases,
        compiler_params=_cparams(("arbitrary",)),
    )(*args)


def _moe_combine_kernel(x_ref, ya_ref, yb_ref, route_ref, mod_ref, o_ref, *, nlt, tpb, ncond):
    ci = _cond_row(pl.program_id(0), nlt, tpb, ncond)
    r = route_ref[...]
    lane = lax.broadcasted_iota(jnp.int32, r.shape, 1)
    g1 = jnp.sum(jnp.where(lane == 2, r, 0.0), axis=-1, keepdims=True)
    g2 = jnp.sum(jnp.where(lane == 3, r, 0.0), axis=-1, keepdims=True)
    y = g1 * ya_ref[...].astype(F32) + g2 * yb_ref[...].astype(F32)
    o_ref[...] = x_ref[...] + mod_ref[5, pl.ds(ci, 1), :] * y


def _moe_combine(x, ya, yb, route, mod, *, nlt, tpb, ncond, out_rows):
    d = x.shape[1]
    tm = ROW_TILE
    row = lambda width: pl.BlockSpec((tm, width), lambda t: (t, 0))
    kern = functools.partial(_moe_combine_kernel, nlt=nlt, tpb=tpb, ncond=ncond)
    return pl.pallas_call(
        kern,
        grid=(out_rows // tm,),
        in_specs=[row(d), row(d), row(d), row(LANES), pl.BlockSpec(mod.shape, lambda t: (0, 0, 0))],
        out_specs=row(d),
        out_shape=jax.ShapeDtypeStruct((out_rows, d), F32),
        compiler_params=_cparams(("arbitrary",)),
    )(x, ya, yb, route, mod)


def _moe_layer(x, u, route, mod, wg, wu, wd, j, *, nlt, tpb, ncond, out_rows):
    n, d = x.shape
    tm = MOE_TM
    experts = jnp.concatenate([route[:, 0], route[:, 1]]).astype(jnp.int32)
    onehot = (experts[:, None] == jnp.arange(N_EXPERTS, dtype=jnp.int32)[None, :]).astype(jnp.int32)
    rank = jnp.sum(onehot * (jnp.cumsum(onehot, axis=0) - 1), axis=1)
    counts = jnp.sum(onehot, axis=0)
    padded = ((counts + tm - 1) // tm) * tm
    ends = jnp.cumsum(padded)
    starts = ends - padded
    dest = starts[experts] + rank
    p = 2 * n + N_EXPERTS * tm
    tokens = jnp.concatenate([jnp.arange(n, dtype=jnp.int32)] * 2)
    inb = dict(mode="promise_in_bounds")
    src = jnp.zeros((p,), jnp.int32).at[dest].set(tokens, unique_indices=True, **inb)
    tile_start = jnp.arange(p // tm, dtype=jnp.int32) * tm
    tile_expert = jnp.minimum(jnp.searchsorted(ends, tile_start, side="right"), N_EXPERTS - 1).astype(jnp.int32)
    tile_valid = (tile_start < ends[-1]).astype(jnp.int32)
    nt0 = (p // tm) // 2
    ys = None
    for lo, hi in ((0, nt0), (nt0, p // tm)):
        ug = u.at[src[lo * tm:hi * tm]].get(**inb)
        ys = _moe_experts(ug, tile_expert[lo:hi], tile_valid[lo:hi], wg, wu, wd, j,
                          tile0=lo, total_rows=p, prev=ys)
    ya = ys.at[dest[:out_rows]].get(**inb)
    yb = ys.at[dest[n:n + out_rows]].get(**inb)
    return _moe_combine(x, ya, yb, route, mod, nlt=nlt, tpb=tpb, ncond=ncond, out_rows=out_rows)


def _rope_tables(seq, n_ctx_rows, batch):
    rows = seq // GRID_W
    row = jnp.repeat(jnp.arange(rows, dtype=F32), GRID_W)
    col = jnp.broadcast_to(jnp.arange(GRID_W, dtype=F32), (rows, GRID_W)).reshape(-1)
    inv_freq = jnp.power(ROPE_THETA, -jnp.arange(ROPE_FREQS, dtype=F32) / ROPE_FREQS)
    ar = row[:, None] * inv_freq
    ac = col[:, None] * inv_freq
    cos = jnp.concatenate([jnp.cos(ar), jnp.cos(ar), jnp.cos(ac), jnp.cos(ac)], axis=-1)
    sin = jnp.concatenate([-jnp.sin(ar), jnp.sin(ar), -jnp.sin(ac), jnp.sin(ac)], axis=-1)
    cos = jnp.tile(jnp.concatenate([cos, cos], axis=-1), (batch, 1))
    sin = jnp.tile(jnp.concatenate([sin, sin], axis=-1), (batch, 1))
    cos = jnp.concatenate([cos, jnp.ones((n_ctx_rows, 2 * HEAD_DIM), F32)], axis=0)
    sin = jnp.concatenate([sin, jnp.zeros((n_ctx_rows, 2 * HEAD_DIM), F32)], axis=0)
    return cos, sin


def kernel(x, c, ctx, c_ctx, mod_w, mod_b, norm1_g, norm2_g, w_in, w_out, pool_lin, pool_scale, hy_short_w, hy_short_b, hy_f_w1, hy_f_b1, hy_f_freq1, hy_f_w2, hy_f_b2, hy_f_freq2, hy_f_w3, hy_bias, qk_norm_g, diff_lambda, subln_g, ffn_w_gate, ffn_w_up, ffn_w_down, router_w, moe_w_gate, moe_w_up, moe_w_down):
    batch, seq, d = x.shape
    ctx_len = ctx.shape[1]
    depth = mod_w.shape[0]
    pw = pool_scale.shape[1]
    hw = hy_bias.shape[1]
    nl, nc = batch * seq, batch * ctx_len
    n = nl + nc
    tm = ROW_TILE
    assert seq % tm == 0 and nc % tm == 0 and seq % (FFT_N1 // 2) == 0 and seq % GRID_W == 0
    assert d == HEADS * 2 * V_DIM and pw == hw and batch == 2
    n2 = 2 * seq // FFT_N1
    ncols = n2 * hw
    assert (n * hw) % ncols == 0
    tiles = dict(nlt=nl // tm, tpb=seq // tm, ncond=batch + 1)

    xs = jnp.concatenate([x.reshape(nl, d), ctx.reshape(nc, d)], axis=0)
    cond8 = jnp.concatenate([c, c_ctx[None, :], jnp.zeros((8 - batch - 1, d), F32)], axis=0)
    mods = _modvec(cond8, mod_w, mod_b)
    cos_t, sin_t = _rope_tables(seq, nc, batch)
    consts = _dft_consts(n2)
    small = _dft_small_consts(ctx_len)
    eye = jnp.eye(len(POOL_WINDOWS), dtype=F32)
    n_moe = moe_w_gate.shape[0]
    assert n_moe <= depth
    moe_w2d = [w.reshape(-1, w.shape[-1]) for w in (moe_w_gate, moe_w_up, moe_w_down)]
    moe_bf16 = {}

    for i in range(depth):
        last = i == depth - 1
        lam_init = 0.8 - 0.6 * math.exp(-0.3 * i)
        mod = mods[i]
        gqk = jnp.concatenate([qk_norm_g[i], qk_norm_g[i]], axis=-1)
        zp, zh, qt, k12, vt = _in_proj(xs, norm1_g[i][None, :], mod, w_in[i].astype(BF16), cos_t, sin_t, gqk,
                                       pw=pw, hw=hw, **tiles)

        g_b = jnp.broadcast_to(subln_g[i][:, None], (V_DIM, ATT_TQ))
        att_kw = dict(batch=batch, seq=seq, ctx_len=ctx_len, lam_init=lam_init)

        cast_j = i if i < n_moe else None

        def attend(direct, lam_p=diff_lambda[i], qt=qt, k12=k12, vt=vt, g_b=g_b, att_kw=att_kw, cast_j=cast_j):
            ctx_out = _attention(lam_p, qt, k12, vt, g_b, latent=False, direct=direct, **att_kw)
            if cast_j is None:
                return (_attention(lam_p, qt, k12, vt, g_b, latent=True, direct=direct, **att_kw), ctx_out)
            if direct:
                lat_out, *wb = _attention(lam_p, qt, k12, vt, g_b, latent=True, direct=True,
                                          cast=[(w, cast_j, n_moe) for w in moe_w2d], **att_kw)
            else:
                lat_out = _attention(lam_p, qt, k12, vt, g_b, latent=True, direct=False, **att_kw)
                wb = _cast_bf16(*[w[cast_j * (w.shape[0] // n_moe):(cast_j + 1) * (w.shape[0] // n_moe)]
                                  for w in moe_w2d])
            return (lat_out, ctx_out, *wb)

        bound = (HEAD_DIM * QSCALE) * jnp.max(jnp.abs(qk_norm_g[i][0])) * jnp.max(jnp.abs(qk_norm_g[i][1]))
        ya_l, ya_c, *wb = lax.cond(bound * 1.02 < ATT_DIRECT_MAX, lambda: attend(True), lambda: attend(False))
        if cast_j is not None:
            moe_bf16[cast_j] = tuple(w.reshape((1,) + src.shape[1:])
                                     for w, src in zip(wb, (moe_w_gate, moe_w_up, moe_w_down)))

        lin_bd = (eye[:, None, :, None] * pool_lin[i][:, :, None, :]).reshape(pw, pw).astype(BF16)
        pscale = pool_scale[i][None, :]
        yp_l = _pool(zp, lin_bd, pscale, row0=0, batch=batch, seq=seq, out_rows=nl, out_row0=0)
        yp_c = _pool(zp, lin_bd, pscale, row0=nl, batch=batch, seq=ctx_len, out_rows=nc, out_row0=0)

        sw, sb = hy_short_w[i], hy_short_b[i][None, :]
        u_l, x0_l = _hy_pre(zh, sw, sb, row0=0, batch=batch, seq=seq, out_rows=n, out_row0=0)
        u_c, x0_c = _hy_pre(zh, sw, sb, row0=nl, batch=batch, seq=ctx_len, out_rows=nc, out_row0=0)
        filt = (hy_f_w1[i], hy_f_b1[i], hy_f_freq1[i], hy_f_w2[i], hy_f_b2[i], hy_f_freq2[i], hy_f_w3[i])
        bias = hy_bias[i][None, :]
        taps, asum = _hy_filter(seq, *filt, hw)
        kr1, ki1 = _fft_first(taps.reshape(FFT_N1, ncols), consts["f1_real"], ncols)
        spectrum = _fft_mid(kr1, ki1, consts, n2, hw)
        ar, ai = _fft_first(u_l.reshape(-1, ncols), consts["f1_data"], ncols)
        br, bi = _fft_mid(ar, ai, consts, n2, hw, spectrum=spectrum)
        reps = min(FFT_COLS, ncols) // hw
        yh_lat = _fft_last(br.reshape(FFT_N1, ncols), bi.reshape(FFT_N1, ncols), consts["g1"],
                           u_l.reshape(-1, ncols), x0_l.reshape(-1, ncols),
                           jnp.tile(bias, (1, reps)), jnp.tile(1.0 / asum, (1, reps)), ncols)
        taps_c, asum_c = _hy_filter(ctx_len, *filt, hw)
        yh_ctx = _hy_ctx(u_c, x0_c, taps_c, bias, 1.0 / asum_c, small)

        j = i // 2
        moe = i % 2 == 1
        rw = ffn_w = None
        if moe:
            rw = jnp.concatenate([router_w[j], jnp.zeros((d, LANES - N_EXPERTS), F32)], axis=1)
        else:
            ffn_w = (ffn_w_gate[j].astype(BF16), ffn_w_up[j].astype(BF16), ffn_w_down[j].astype(BF16))
        out_rows = nl if last else n
        outs = _out_proj(xs, (yp_l, yh_lat.reshape(nl, hw), ya_l), (yp_c, yh_ctx, ya_c), w_out[i].astype(BF16),
                         norm2_g[i][None, :], mod, rw, ffn_w, out_rows=out_rows, **tiles)
        if moe:
            xs, u, route = outs
            xs = _moe_layer(xs, u, route, mod, *moe_bf16[j], 0, out_rows=out_rows, **tiles)
        else:
            xs, = outs
    return xs[:nl].reshape(batch, seq, d)
```

```python
import functools
import math

import numpy as np
import jax
import jax.numpy as jnp
from jax import lax
from jax.experimental import pallas as pl
from jax.experimental.pallas import tpu as pltpu

F32 = jnp.float32
BF16 = jnp.bfloat16
EPS = 1e-6

GRID_W = 64
POOL_WINDOWS = (2, 4, 8, 16)
HEADS = 4
HEAD_DIM = 64
V_DIM = 128
ROPE_FREQS = 16
ROPE_THETA = 10000.0
HY_BANDS = 16
HY_TARGET = 1e-2
HY_FAST = 0.3
HY_SLOW = 1.5
N_EXPERTS = 8

LANES = 128
ROW_TILE = 512
HALO = 32
FFT_N1 = 128
FFT_COLS = 2048
FFT_KB = 4
ATT_TQ = 1024
ATT_TK = 512
ATT_UNROLL = 16
ATT_GROUP = 8
QSCALE = (HEAD_DIM ** -0.5) * math.log2(math.e)
ATT_DIRECT_MAX = 100.0
FFN_CHUNKS = 2
CAST_STEPS = 64
MOE_TM = 512
MOE_PARTS = 4
MOE_CHUNKS = 2
VMEM_LIMIT = 56 * 1024 * 1024


def _cparams(sem):
    return pltpu.CompilerParams(dimension_semantics=sem, vmem_limit_bytes=VMEM_LIMIT)


def _dot(a, b):
    return jnp.dot(a, b, preferred_element_type=F32)


def _split(a):
    hi = a.astype(BF16)
    lo = (a - hi.astype(F32)).astype(BF16)
    return hi, lo


def _dot3(a, b):
    ah, al = _split(a)
    bh, bl = _split(b)
    return _dot(ah, bh) + _dot(ah, bl) + _dot(al, bh)


def _dot3c(ch, cl, d):
    dh, dl = _split(d)
    return _dot(ch, dh) + _dot(ch, dl) + _dot(cl, dh)


def _silu(x):
    return x / (1.0 + jnp.exp(-x))


def _normmod(x, g, shift, scale):
    ms = jnp.mean(x * x, axis=-1, keepdims=True)
    return (x * lax.rsqrt(ms + EPS) * g) * (1.0 + scale) + shift


def _cond_row(t, n_lat_tiles, tiles_per_batch, n_cond):
    return jnp.where(t >= n_lat_tiles, n_cond - 1, t // tiles_per_batch)


def _cast_kernel(*refs):
    half = len(refs) // 2
    for w_ref, o_ref in zip(refs[:half], refs[half:]):
        o_ref[...] = w_ref[...].astype(BF16)


def _cast_bf16(*ws):
    flat = [w.reshape(-1, w.shape[-1]) for w in ws]
    steps = CAST_STEPS
    specs = [pl.BlockSpec((w.shape[0] // steps, w.shape[1]), lambda t: (t, 0)) for w in flat]
    outs = pl.pallas_call(
        _cast_kernel,
        grid=(steps,),
        in_specs=specs,
        out_specs=specs,
        out_shape=[jax.ShapeDtypeStruct(w.shape, BF16) for w in flat],
        compiler_params=_cparams(("arbitrary",)),
    )(*flat)
    return [o.reshape(w.shape) for o, w in zip(outs, ws)]


def _modvec_kernel(c_ref, w_ref, b_ref, o_ref):
    s = _silu(c_ref[...])
    o_ref[0, 0] = _dot3(s, w_ref[0]) + b_ref[0]


def _modvec(cond8, mod_w, mod_b):
    depth, d, six_d = mod_w.shape
    nchunk = six_d // d
    return pl.pallas_call(
        _modvec_kernel,
        grid=(depth, nchunk),
        in_specs=[
            pl.BlockSpec((8, d), lambda i, j: (0, 0)),
            pl.BlockSpec((1, d, d), lambda i, j: (i, 0, j)),
            pl.BlockSpec((1, 1, d), lambda i, j: (i, 0, j)),
        ],
        out_specs=pl.BlockSpec((1, 1, 8, d), lambda i, j: (i, j, 0, 0)),
        out_shape=jax.ShapeDtypeStruct((depth, nchunk, 8, d), F32),
        compiler_params=_cparams(("arbitrary", "arbitrary")),
    )(cond8, mod_w, mod_b.reshape(depth, 1, six_d))


def _in_proj_kernel(x_ref, g_ref, mod_ref, w_ref, cos_ref, sin_ref, gqk_ref,
                    zp_ref, zh_ref, qt_ref, k_ref, vt_ref, zatt_ref, *, nlt, tpb, ncond, pw, hw, ntiles):
    t = pl.program_id(0)

    @pl.when(t == 0)
    def _():
        zatt_ref[...] = jnp.zeros_like(zatt_ref)

    tm = x_ref.shape[0]
    att = pw + 3 * hw
    lane = lax.broadcasted_iota(jnp.int32, (tm, 2 * HEAD_DIM), 1)
    first = lane < HEAD_DIM
    apart = (lane % (2 * ROPE_FREQS)) < ROPE_FREQS
    cos = cos_ref[...]
    sin = sin_ref[...]
    qk_w = HEADS * 2 * HEAD_DIM

    def norm_rope(v, g):
        v2 = v * v
        s_all = jnp.sum(v2, axis=-1, keepdims=True)
        s_first = jnp.sum(jnp.where(first, v2, 0.0), axis=-1, keepdims=True)
        ms = jnp.where(first, s_first, s_all - s_first) * (1.0 / HEAD_DIM)
        vn = v * lax.rsqrt(ms + EPS) * g
        swapped = jnp.where(apart, pltpu.roll(vn, 2 * HEAD_DIM - ROPE_FREQS, 1), pltpu.roll(vn, ROPE_FREQS, 1))
        return vn * cos + swapped * sin

    for h in range(HEADS):
        lo = h * 2 * HEAD_DIM
        q = norm_rope(zatt_ref[:, lo:lo + 2 * HEAD_DIM], gqk_ref[0:1, :])
        qt_ref[h] = (q * QSCALE).T.astype(BF16)
        k = norm_rope(zatt_ref[:, qk_w + lo:qk_w + lo + 2 * HEAD_DIM], gqk_ref[1:2, :])
        k_ref[h] = k.astype(BF16)
        vlo = 2 * qk_w + h * V_DIM
        vt_ref[h] = zatt_ref[:, vlo:vlo + V_DIM].T.astype(BF16)

    ci = _cond_row(jnp.minimum(t, ntiles - 1), nlt, tpb, ncond)
    shift = mod_ref[0, pl.ds(ci, 1), :]
    scale = mod_ref[1, pl.ds(ci, 1), :]
    y = _normmod(x_ref[...], g_ref[...], shift, scale).astype(BF16)
    z = _dot(y, w_ref[...])
    zp_ref[...] = z[:, :pw].astype(BF16)
    zh_ref[...] = z[:, pw:att].astype(BF16)
    zatt_ref[...] = z[:, att:]


def _in_proj(x, g, mod, w, cos_t, sin_t, gqk, *, nlt, tpb, ncond, pw, hw):
    n, d = x.shape
    wid = w.shape[1]
    tm = ROW_TILE
    nt = n // tm
    kern = functools.partial(_in_proj_kernel, nlt=nlt, tpb=tpb, ncond=ncond, pw=pw, hw=hw, ntiles=nt)
    cur = lambda t: jnp.minimum(t, nt - 1)
    prv = lambda t: jnp.maximum(t - 1, 0)
    return pl.pallas_call(
        kern,
        grid=(nt + 1,),
        in_specs=[
            pl.BlockSpec((tm, d), lambda t: (cur(t), 0)),
            pl.BlockSpec((1, d), lambda t: (0, 0)),
            pl.BlockSpec(mod.shape, lambda t: (0, 0, 0)),
            pl.BlockSpec((d, wid), lambda t: (0, 0)),
            pl.BlockSpec((tm, 2 * HEAD_DIM), lambda t: (prv(t), 0)),
            pl.BlockSpec((tm, 2 * HEAD_DIM), lambda t: (prv(t), 0)),
            pl.BlockSpec((2, 2 * HEAD_DIM), lambda t: (0, 0)),
        ],
        out_specs=[
            pl.BlockSpec((tm, pw), lambda t: (cur(t), 0)),
            pl.BlockSpec((tm, 3 * hw), lambda t: (cur(t), 0)),
            pl.BlockSpec((HEADS, 2 * HEAD_DIM, tm), lambda t: (0, 0, prv(t))),
            pl.BlockSpec((HEADS, tm, 2 * HEAD_DIM), lambda t: (0, prv(t), 0)),
            pl.BlockSpec((HEADS, V_DIM, tm), lambda t: (0, 0, prv(t))),
        ],
        scratch_shapes=[pltpu.VMEM((tm, wid - pw - 3 * hw), F32)],
        out_shape=[
            jax.ShapeDtypeStruct((n, pw), BF16),
            jax.ShapeDtypeStruct((n, 3 * hw), BF16),
            jax.ShapeDtypeStruct((HEADS, 2 * HEAD_DIM, n), BF16),
            jax.ShapeDtypeStruct((HEADS, n, 2 * HEAD_DIM), BF16),
            jax.ShapeDtypeStruct((HEADS, V_DIM, n), BF16),
        ],
        compiler_params=_cparams(("arbitrary",)),
    )(x, g, mod, w, cos_t, sin_t, gqk)


def _attn_kernel(lam_ref, qt_ref, kc_ref, vtc_ref, *rest, tk, n_lat_chunks, lam_init):
    if n_lat_chunks:
        kl_ref, vtl_ref, g_ref, o_ref, m_ref, l_ref, acc_ref = rest
    else:
        g_ref, o_ref, m_ref, l_ref, acc_ref = rest
    qt = qt_ref[0]
    row = lax.broadcasted_iota(jnp.int32, qt.shape, 0)
    zero = jnp.zeros_like(qt)
    qmaps = (jnp.where(row < HEAD_DIM, qt, zero), jnp.where(row >= HEAD_DIM, qt, zero))

    def process(k_tile, vt_tile, first):
        for mi in range(2):
            s = _dot(k_tile, qmaps[mi])
            smax = jnp.max(s, axis=0, keepdims=True)
            if first:
                m_new = smax
            else:
                m_old = m_ref[mi]
                m_new = jnp.maximum(m_old, smax)
            p = jnp.exp2(s - m_new)
            psum = jnp.sum(p, axis=0, keepdims=True)
            pv = _dot(vt_tile, p.astype(BF16))
            if first:
                l_ref[mi] = psum
                acc_ref[mi] = pv
            else:
                alpha = jnp.exp2(m_old - m_new)
                l_ref[mi] = alpha * l_ref[mi] + psum
                acc_ref[mi] = alpha * acc_ref[mi] + pv
            m_ref[mi] = m_new

    process(kc_ref[0], vtc_ref[0], True)
    if n_lat_chunks:
        def body(j, carry):
            off = pl.multiple_of(j * tk, tk)
            process(kl_ref[0, pl.ds(off, tk), :], vtl_ref[0, :, pl.ds(off, tk)], False)
            return carry
        lax.fori_loop(0, n_lat_chunks, body, 0)

    lv = lam_ref[...]
    lam = (jnp.exp(jnp.sum(lv[0:1] * lv[1:2], axis=-1, keepdims=True))
           - jnp.exp(jnp.sum(lv[2:3] * lv[3:4], axis=-1, keepdims=True)) + lam_init)
    o = acc_ref[0] * (1.0 / l_ref[0]) - lam * (acc_ref[1] * (1.0 / l_ref[1]))
    ms = jnp.mean(o * o, axis=0, keepdims=True)
    y = o * lax.rsqrt(ms + EPS) * g_ref[...] * (1.0 - lam_init)
    o_ref[...] = y.T.astype(BF16)


def _attn_direct_kernel(lam_ref, qt_ref, kc_ref, vtc_ref, *rest, tk, n_lat_chunks, lam_init, n_cast=0):
    if n_lat_chunks:
        kl_ref, vtl_ref, g_ref = rest[:3]
        rest = rest[3:]
    else:
        g_ref = rest[0]
        rest = rest[1:]
    cast_in, (o_ref, *cast_out), (p_ref, l_ref, acc_ref) = rest[:n_cast], rest[n_cast:2 * n_cast + 1], rest[2 * n_cast + 1:]
    for w_ref, c_ref in zip(cast_in, cast_out):
        c_ref[...] = w_ref[...].astype(BF16)
    qt = qt_ref[0]
    tq = qt.shape[1]
    row = lax.broadcasted_iota(jnp.int32, qt.shape, 0)
    zero = jnp.zeros_like(qt)
    qmaps = (jnp.where(row < HEAD_DIM, qt, zero), jnp.where(row >= HEAD_DIM, qt, zero))
    grp = ATT_GROUP

    def keys(first_piece, count):
        if isinstance(first_piece, int):
            return pl.ds((first_piece - 1) * tk, count * tk)
        return pl.ds(pl.multiple_of((first_piece - 1) * tk, tk), count * tk)

    def stage_ab(k_tile, pset, r):
        for mi in range(2):
            p = jnp.exp2(_dot(k_tile, qmaps[mi]))
            l_ref[mi] = l_ref[mi] + jnp.sum(p.reshape(tk // 8, 8, tq), axis=0)
            p_ref[pset, mi, r * tk:(r + 1) * tk, :] = p.astype(BF16)

    def stage_c(vt_tile, pset, lo, count):
        for mi in range(2):
            acc_ref[mi] = acc_ref[mi] + _dot(vt_tile, p_ref[pset, mi, lo * tk:(lo + count) * tk, :])

    def group_ab(g, pset):
        k_tile = kl_ref[0, keys(grp * g, grp), :]
        for mi in range(2):
            p = jnp.exp2(_dot(k_tile, qmaps[mi]))
            l_ref[mi] = l_ref[mi] + jnp.sum(p.reshape(grp * tk // 8, 8, tq), axis=0)
            p_ref[pset, mi] = p.astype(BF16)

    def group_c(g, pset):
        stage_c(vtl_ref[0, :, keys(grp * g, grp)], pset, 0, grp)

    l_ref[...] = jnp.zeros_like(l_ref)
    acc_ref[...] = jnp.zeros_like(acc_ref)
    stage_ab(kc_ref[0], 0, 0)
    if n_lat_chunks == 0:
        stage_c(vtc_ref[0], 0, 0, 1)
    else:
        n_groups = n_lat_chunks // grp
        per_step = ATT_UNROLL // grp
        for r in range(1, grp):
            stage_ab(kl_ref[0, keys(r, 1), :], 0, r)
        group_ab(1, 1)
        stage_c(vtc_ref[0], 0, 0, 1)
        stage_c(vtl_ref[0, :, keys(1, grp - 1)], 0, 1, grp - 1)
        for g in range(2, per_step):
            group_ab(g, g % 2)
            group_c(g - 1, (g - 1) % 2)

        def body(jj, carry):
            g0 = per_step * (jj + 1)
            for q in range(per_step):
                group_ab(g0 + q, q % 2)
                group_c(g0 + q - 1, (q + 1) % 2)
            return carry

        lax.fori_loop(0, n_groups // per_step - 1, body, 0)
        stage_ab(kl_ref[0, keys(n_lat_chunks, 1), :], 0, 0)
        group_c(n_groups - 1, 1)
        stage_c(vtl_ref[0, :, keys(n_lat_chunks, 1)], 0, 0, 1)

    lv = lam_ref[...]
    lam = (jnp.exp(jnp.sum(lv[0:1] * lv[1:2], axis=-1, keepdims=True))
           - jnp.exp(jnp.sum(lv[2:3] * lv[3:4], axis=-1, keepdims=True)) + lam_init)
    l1 = jnp.sum(l_ref[0], axis=0, keepdims=True)
    l2 = jnp.sum(l_ref[1], axis=0, keepdims=True)
    o = acc_ref[0] * (1.0 / l1) - lam * (acc_ref[1] * (1.0 / l2))
    ms = jnp.mean(o * o, axis=0, keepdims=True)
    y = o * lax.rsqrt(ms + EPS) * g_ref[...] * (1.0 - lam_init)
    o_ref[...] = y.T.astype(BF16)


def _attention(lam_p, qt, k12, vt, g_b, *, batch, seq, ctx_len, latent, lam_init, direct, cast=()):
    n = k12.shape[1]
    nl = batch * seq
    tq = ATT_TQ if latent else ctx_len
    nq = seq // tq if latent else 1
    tk = ctx_len if direct else ATT_TK
    assert seq % ((ATT_UNROLL if direct else 1) * tk) == 0
    qbase = 0 if latent else nl // tq

    def qrow(b, iq):
        return (b * nq + iq) if latent else (qbase + b)

    in_specs = [
        pl.BlockSpec(lam_p.shape, lambda b, h, iq: (0, 0)),
        pl.BlockSpec((1, 2 * HEAD_DIM, tq), lambda b, h, iq: (h, 0, qrow(b, iq))),
        pl.BlockSpec((1, ctx_len, 2 * HEAD_DIM), lambda b, h, iq: (h, nl // ctx_len + b, 0)),
        pl.BlockSpec((1, V_DIM, ctx_len), lambda b, h, iq: (h, 0, nl // ctx_len + b)),
    ]
    args = [lam_p, qt, k12, vt]
    if latent:
        in_specs += [
            pl.BlockSpec((1, seq, 2 * HEAD_DIM), lambda b, h, iq: (h, b, 0)),
            pl.BlockSpec((1, V_DIM, seq), lambda b, h, iq: (h, 0, b)),
        ]
        args += [k12, vt]
    in_specs.append(pl.BlockSpec((V_DIM, tq), lambda b, h, iq: (0, 0)))
    args.append(g_b[:, :tq])
    nrows = nl if latent else batch * ctx_len
    body = _attn_direct_kernel if direct else _attn_kernel
    kw = dict(tk=tk, n_lat_chunks=(seq // tk if latent else 0), lam_init=lam_init)
    out_specs = [pl.BlockSpec((tq, V_DIM), lambda b, h, iq: (b * nq + iq, h))]
    out_shape = [jax.ShapeDtypeStruct((nrows, HEADS * V_DIM), BF16)]
    if cast:
        assert direct
        kw["n_cast"] = len(cast)
        steps = batch * HEADS * nq
        for w, slab, parts in cast:
            rows = w.shape[0] // parts // steps
            in_specs.append(pl.BlockSpec(
                (rows, w.shape[1]), lambda b, h, iq, slab=slab: (slab * steps + (b * HEADS + h) * nq + iq, 0)))
            args.append(w)
            out_specs.append(pl.BlockSpec((rows, w.shape[1]), lambda b, h, iq: ((b * HEADS + h) * nq + iq, 0)))
            out_shape.append(jax.ShapeDtypeStruct((w.shape[0] // parts, w.shape[1]), BF16))
    kern = functools.partial(body, **kw)
    if direct:
        scratch = [pltpu.VMEM((2, 2, ATT_GROUP * tk, tq), BF16),
                   pltpu.VMEM((2, 8, tq), F32), pltpu.VMEM((2, V_DIM, tq), F32)]
    else:
        scratch = [pltpu.VMEM((2, 1, tq), F32), pltpu.VMEM((2, 1, tq), F32), pltpu.VMEM((2, V_DIM, tq), F32)]
    outs = pl.pallas_call(
        kern,
        grid=(batch, HEADS, nq),
        in_specs=in_specs,
        out_specs=out_specs,
        out_shape=out_shape,
        scratch_shapes=scratch,
        compiler_params=_cparams(("arbitrary", "arbitrary", "arbitrary")),
    )(*args)
    return outs if cast else outs[0]


def _halo_specs(width, *, row0, seq, tile):
    hb = tile // HALO

    def cur(b, i):
        return ((row0 + b * seq) // tile + i, 0)

    def prev(b, i):
        first = (row0 + b * seq) // HALO
        return (jnp.maximum(first + i * hb - 1, first), 0)

    def nxt(b, i):
        first = (row0 + b * seq) // HALO
        return (jnp.minimum(first + (i + 1) * hb, first + seq // HALO - 1), 0)

    return [pl.BlockSpec((HALO, width), prev), pl.BlockSpec((tile, width), cur), pl.BlockSpec((HALO, width), nxt)]


def _fill_ext(ext_ref, prev_ref, cur_ref, next_ref, *, seq, tile):
    pos0 = pl.program_id(1) * tile
    width = cur_ref.shape[1]
    hpos = lax.broadcasted_iota(jnp.int32, (HALO, width), 0)
    ext_ref[0:HALO, :] = jnp.where(pos0 - HALO + hpos >= 0, prev_ref[...].astype(F32), 0.0)
    ext_ref[HALO:HALO + tile, :] = cur_ref[...].astype(F32)
    ext_ref[HALO + tile:, :] = jnp.where(pos0 + tile + hpos < seq, next_ref[...].astype(F32), 0.0)


def _pool_kernel(prev_ref, cur_ref, next_ref, lin_ref, scale_ref, o_ref, ext_ref, s_ref, *, seq, tile):
    _fill_ext(ext_ref, prev_ref, cur_ref, next_ref, seq=seq, tile=tile)
    width = cur_ref.shape[1]
    gd = width // len(POOL_WINDOWS)
    n0 = tile + 2 * HALO
    s_ref[0, 0:n0 - 8, :] = ext_ref[0:n0 - 8, :] + ext_ref[1:n0 - 7, :]
    for k in range(1, len(POOL_WINDOWS)):
        step = 1 << k
        ln = n0 - 8 * (k + 1)
        s_ref[k, 0:ln, :] = s_ref[k - 1, 0:ln, :] + s_ref[k - 1, step:step + ln, :]
    lane = lax.broadcasted_iota(jnp.int32, (tile, width), 1)
    pos = pl.program_id(1) * tile + lax.broadcasted_iota(jnp.int32, (tile, width), 0)
    grp = lane // gd
    wsum = jnp.zeros((tile, width), F32)
    half = jnp.zeros((tile, width), jnp.int32)
    for k, win in enumerate(POOL_WINDOWS):
        start = HALO - win // 2
        wsum = jnp.where(grp == k, s_ref[k, start:start + tile, :], wsum)
        half = jnp.where(grp == k, win // 2, half)
    cnt = jnp.minimum(pos + half, seq) - jnp.maximum(pos - half, 0)
    z = ext_ref[HALO:HALO + tile, :]
    dlt = (wsum / cnt.astype(F32) - z).astype(BF16)
    o_ref[...] = (_dot(dlt, lin_ref[...]) * scale_ref[...]).astype(BF16)


def _pool(zp, lin_bd, scale, *, row0, batch, seq, out_rows, out_row0):
    width = zp.shape[1]
    tile = min(ROW_TILE, seq)
    kern = functools.partial(_pool_kernel, seq=seq, tile=tile)
    return pl.pallas_call(
        kern,
        grid=(batch, seq // tile),
        in_specs=_halo_specs(width, row0=row0, seq=seq, tile=tile) + [
            pl.BlockSpec((width, width), lambda b, i: (0, 0)),
            pl.BlockSpec((1, width), lambda b, i: (0, 0)),
        ],
        out_specs=pl.BlockSpec((tile, width), lambda b, i: ((out_row0 + b * seq) // tile + i, 0)),
        out_shape=jax.ShapeDtypeStruct((out_rows, width), BF16),
        scratch_shapes=[
            pltpu.VMEM((tile + 2 * HALO, width), F32),
            pltpu.VMEM((4, tile + 2 * HALO, width), F32),
        ],
        compiler_params=_cparams(("arbitrary", "arbitrary")),
    )(zp, zp, zp, lin_bd, scale)


def _hy_pre_kernel(prev_ref, cur_ref, next_ref, w_ref, b_ref, u_ref, x0_ref, ext_ref, *, seq, tile, hw):
    _fill_ext(ext_ref, prev_ref, cur_ref, next_ref, seq=seq, tile=tile)
    y = b_ref[...] + ext_ref[HALO - 1:HALO - 1 + tile, :] * w_ref[0:1, :]
    y = y + ext_ref[HALO:HALO + tile, :] * w_ref[1:2, :]
    y = y + ext_ref[HALO + 1:HALO + 1 + tile, :] * w_ref[2:3, :]
    x0_ref[...] = y[:, :hw]
    u_ref[...] = y[:, 2 * hw:] * y[:, hw:2 * hw]


def _hy_pre(zh, sw, sb, *, row0, batch, seq, out_rows, out_row0):
    width = zh.shape[1]
    hw = width // 3
    tile = min(ROW_TILE, seq)
    kern = functools.partial(_hy_pre_kernel, seq=seq, tile=tile, hw=hw)
    ospec = pl.BlockSpec((tile, hw), lambda b, i: ((out_row0 + b * seq) // tile + i, 0))
    return pl.pallas_call(
        kern,
        grid=(batch, seq // tile),
        in_specs=_halo_specs(width, row0=row0, seq=seq, tile=tile) + [
            pl.BlockSpec((3, width), lambda b, i: (0, 0)),
            pl.BlockSpec((1, width), lambda b, i: (0, 0)),
        ],
        out_specs=[ospec, ospec],
        out_shape=[jax.ShapeDtypeStruct((out_rows, hw), F32), jax.ShapeDtypeStruct((out_rows, hw), F32)],
        scratch_shapes=[pltpu.VMEM((tile + 2 * HALO, width), F32)],
        compiler_params=_cparams(("arbitrary", "arbitrary")),
    )(zh, zh, zh, sw, sb)


def _filter_kernel(feat_ref, w1_ref, b1_ref, f1_ref, w2_ref, b2_ref, f2_ref, w3_ref, dl_ref,
                   taps_ref, asum_ref, *, tile, hw, seq, embp):
    feat = feat_ref[...]
    h = jnp.sin(f1_ref[...] * (_dot3(feat, w1_ref[...]) + b1_ref[...]))
    h = jnp.sin(f2_ref[...] * (_dot3(h, w2_ref[...]) + b2_ref[...]))
    h = _dot3(h, w3_ref[...])
    odd = lax.broadcasted_iota(jnp.int32, (tile, 2 * hw), 1) >= hw
    tcol = jnp.where(odd, feat[:, embp:embp + 1], feat[:, 0:1])
    dec = jnp.exp(-tcol * dl_ref[...])
    row = pl.program_id(0) * tile + lax.broadcasted_iota(jnp.int32, (tile, 2 * hw), 0)
    pos = 2 * row + odd.astype(jnp.int32)
    taps = jnp.where(pos == seq, 0.0, h * dec)
    taps_ref[...] = taps
    part = jnp.sum(jnp.abs(taps), axis=0, keepdims=True)

    @pl.when(pl.program_id(0) == 0)
    def _():
        asum_ref[...] = part

    @pl.when(pl.program_id(0) != 0)
    def _():
        asum_ref[...] = asum_ref[...] + part


def _hy_filter(seq, w1, b1, f1, w2, b2, f2, w3, hw):
    ar = jnp.arange(seq, dtype=jnp.int32)
    pos = jnp.concatenate([ar, (seq - ar) % seq]).astype(F32)[:, None]
    t = pos * (1.0 / (seq - 1))
    w = (2.0 * math.pi / seq) * pos
    bands = jnp.linspace(1e-4, HY_BANDS - 1, HY_BANDS, dtype=F32)[None, :]
    emb = 1 + 2 * HY_BANDS
    embp = ((emb + 7) // 8) * 8
    feat = jnp.concatenate([t, jnp.cos(bands * w), -jnp.sin(bands * w), jnp.zeros((2 * seq, embp - emb), F32)], axis=-1)
    w1p = jnp.concatenate([w1, jnp.zeros((embp - emb, w1.shape[1]), F32)], axis=0)
    max_decay = math.log(1.0 / HY_TARGET) / HY_FAST
    min_decay = math.log(1.0 / HY_TARGET) / HY_SLOW
    deltas = jnp.linspace(min_decay, max_decay, hw, dtype=F32)[None, :]
    feat = feat.reshape(seq, 2 * embp)
    hid = w1.shape[1]

    def pair(m):
        z = jnp.zeros_like(m)
        return jnp.concatenate([jnp.concatenate([m, z], axis=1), jnp.concatenate([z, m], axis=1)], axis=0)

    twice = lambda v: jnp.concatenate([v, v])[None, :]
    w3p = jnp.concatenate([pair(w3[:, :hw]), pair(w3[:, hw:])], axis=1)
    tile = min(ROW_TILE, seq) // 2
    nfwd = seq // (2 * tile)
    full = lambda shape: pl.BlockSpec(shape, lambda i: (0,) * len(shape))
    taps, asum = pl.pallas_call(
        functools.partial(_filter_kernel, tile=tile, hw=hw, seq=seq, embp=embp),
        grid=(2 * nfwd,),
        in_specs=[
            pl.BlockSpec((tile, 2 * embp), lambda i: (i, 0)),
            full((2 * embp, 2 * hid)), full((1, 2 * hid)), full((1, 2 * hid)),
            full((2 * hid, 2 * hid)), full((1, 2 * hid)), full((1, 2 * hid)),
            pl.BlockSpec((2 * hid, 2 * hw), lambda i: (0, i // nfwd)), full((1, 2 * hw)),
        ],
        out_specs=[
            pl.BlockSpec((tile, 2 * hw), lambda i: (i, 0)),
            pl.BlockSpec((1, 2 * hw), lambda i: (0, 0)),
        ],
        out_shape=[
            jax.ShapeDtypeStruct((seq, 2 * hw), F32),
            jax.ShapeDtypeStruct((1, 2 * hw), F32),
        ],
        compiler_params=_cparams(("arbitrary",)),
    )(feat, pair(w1p), twice(b1), twice(f1), pair(w2), twice(b2), twice(f2), w3p,
      jnp.concatenate([deltas, deltas], axis=1))
    return taps.reshape(2 * seq, hw), asum[:, :hw] + asum[:, hw:]


def _np_split(a):
    a32 = jnp.asarray(a, F32)
    hi = a32.astype(BF16)
    lo = (a32 - hi.astype(F32)).astype(BF16)
    return hi, lo


def _dft_consts(n2):
    n1 = FFT_N1
    n = n1 * n2
    half = n1 // 2
    a1 = -2.0 * np.pi * np.outer(np.arange(n1), np.arange(n1)) / n1
    f1r, f1i = np.cos(a1), np.sin(a1)
    f1_data = np.block([[f1r[:, :half], -f1i[:, :half]], [f1i[:, :half], f1r[:, :half]]])
    f1_real = np.concatenate([f1r, f1i], axis=0)
    g1r, g1i = f1r / n, -f1i / n
    g1 = np.block([[g1r[:half], -g1i[:half]], [g1i[:half], g1r[:half]]])
    a2 = -2.0 * np.pi * np.outer(np.arange(n2), np.arange(n2)) / n2
    f2r, f2i = np.cos(a2), np.sin(a2)
    f2 = np.block([[f2r, -f2i], [f2i, f2r]])
    g2 = np.block([[f2r, f2i], [-f2i, f2r]])
    at = -2.0 * np.pi * np.outer(np.arange(n1), np.arange(n2)) / n
    twr = jnp.broadcast_to(jnp.asarray(np.cos(at), F32)[:, :, None], (n1, n2, LANES))
    twi = jnp.broadcast_to(jnp.asarray(np.sin(at), F32)[:, :, None], (n1, n2, LANES))
    return dict(f1_data=_np_split(f1_data), f1_real=_np_split(f1_real), g1=_np_split(g1),
                f2=_np_split(f2), g2=_np_split(g2), twr=twr, twi=twi)


def _fft1_kernel(z_ref, fh_ref, fl_ref, ar_ref, ai_ref):
    a = _dot3c(fh_ref[...], fl_ref[...], z_ref[...])
    ar_ref[...] = a[:FFT_N1]
    ai_ref[...] = a[FFT_N1:]


def _fft_first(zview, fmat, ncols):
    cb = min(FFT_COLS, ncols)
    fh, fl = fmat
    cspec = pl.BlockSpec(fh.shape, lambda j: (0, 0))
    ospec = pl.BlockSpec((FFT_N1, cb), lambda j: (0, j))
    return pl.pallas_call(
        _fft1_kernel,
        grid=(ncols // cb,),
        in_specs=[pl.BlockSpec((FFT_N1, cb), lambda j: (0, j)), cspec, cspec],
        out_specs=[ospec, ospec],
        out_shape=[jax.ShapeDtypeStruct((FFT_N1, ncols), F32)] * 2,
        compiler_params=_cparams(("arbitrary",)),
    )(zview, fh, fl)


def _fftmid_kernel(ar_ref, ai_ref, twr_ref, twi_ref, f2h_ref, f2l_ref, *rest, filter_only, n2):
    kb, _, width = ar_ref.shape
    reps = width // LANES
    lanes = lambda parts: jnp.concatenate(parts, axis=1)
    twr = lanes([t for j in range(kb) for t in [twr_ref[j]] * reps])
    twi = lanes([t for j in range(kb) for t in [twi_ref[j]] * reps])
    ar = lanes([ar_ref[j] for j in range(kb)])
    ai = lanes([ai_ref[j] for j in range(kb)])
    z = jnp.concatenate([ar * twr - ai * twi, ar * twi + ai * twr], axis=0)
    x = _dot3c(f2h_ref[...], f2l_ref[...], z)
    xr, xi = x[:n2], x[n2:]
    if filter_only:
        kr_out, ki_out = rest
        for j in range(kb):
            kr_out[j] = xr[:, j * width:(j + 1) * width]
            ki_out[j] = xi[:, j * width:(j + 1) * width]
        return
    kr_ref, ki_ref, g2h_ref, g2l_ref, br_out, bi_out = rest
    kr = lanes([kr_ref[j] for j in range(kb)])
    ki = lanes([ki_ref[j] for j in range(kb)])
    y = jnp.concatenate([xr * kr - xi * ki, xr * ki + xi * kr], axis=0)
    w = _dot3c(g2h_ref[...], g2l_ref[...], y)
    wr, wi = w[:n2], w[n2:]
    br = wr * twr + wi * twi
    bi = wi * twr - wr * twi
    for j in range(kb):
        br_out[j] = br[:, j * width:(j + 1) * width]
        bi_out[j] = bi[:, j * width:(j + 1) * width]


def _fft_mid(ar, ai, consts, n2, width, spectrum=None):
    a3r = ar.reshape(FFT_N1, n2, width)
    a3i = ai.reshape(FFT_N1, n2, width)
    blk = pl.BlockSpec((FFT_KB, n2, width), lambda k: (k, 0, 0))
    twspec = pl.BlockSpec((FFT_KB, n2, LANES), lambda k: (k, 0, 0))
    cspec = pl.BlockSpec((2 * n2, 2 * n2), lambda k: (0, 0))
    in_specs = [blk, blk, twspec, twspec, cspec, cspec]
    args = [a3r, a3i, consts["twr"], consts["twi"], *consts["f2"]]
    if spectrum is not None:
        in_specs += [blk, blk, cspec, cspec]
        args += [spectrum[0], spectrum[1], *consts["g2"]]
    return pl.pallas_call(
        functools.partial(_fftmid_kernel, filter_only=spectrum is None, n2=n2),
        grid=(FFT_N1 // FFT_KB,),
        in_specs=in_specs,
        out_specs=[blk, blk],
        out_shape=[jax.ShapeDtypeStruct((FFT_N1, n2, width), F32)] * 2,
        compiler_params=_cparams(("arbitrary",)),
    )(*args)


def _fftlast_kernel(br_ref, bi_ref, gh_ref, gl_ref, u_ref, x0_ref, bias_ref, invn_ref, o_ref):
    b = jnp.concatenate([br_ref[...], bi_ref[...]], axis=0)
    y = _dot3c(gh_ref[...], gl_ref[...], b)
    o_ref[...] = ((y * invn_ref[...] + u_ref[...] * bias_ref[...]) * x0_ref[...]).astype(BF16)


def _fft_last(br, bi, gmat, uview, x0view, bias_t, invn_t, ncols):
    cb = min(FFT_COLS, ncols)
    gh, gl = gmat
    cspec = pl.BlockSpec(gh.shape, lambda j: (0, 0))
    dspec = pl.BlockSpec((FFT_N1, cb), lambda j: (0, j))
    vspec = pl.BlockSpec((1, cb), lambda j: (0, 0))
    return pl.pallas_call(
        _fftlast_kernel,
        grid=(ncols // cb,),
        in_specs=[dspec, dspec, cspec, cspec, dspec, dspec, vspec, vspec],
        out_specs=dspec,
        out_shape=jax.ShapeDtypeStruct((FFT_N1, ncols), BF16),
        compiler_params=_cparams(("arbitrary",)),
    )(br, bi, gh, gl, uview, x0view, bias_t, invn_t)


def _dft_small_consts(seq):
    n = 2 * seq
    a = -2.0 * np.pi * np.outer(np.arange(n), np.arange(n)) / n
    fr, fi = np.cos(a), np.sin(a)
    f_data = np.block([[fr[:, :seq], -fi[:, :seq]], [fi[:, :seq], fr[:, :seq]]])
    f_real = np.concatenate([fr, fi], axis=0)
    gr, gi = fr / n, -fi / n
    g = np.block([[gr[:seq], -gi[:seq]], [gi[:seq], gr[:seq]]])
    return _np_split(f_data), _np_split(f_real), _np_split(g)


def _hy_ctx_kernel(u_ref, x0_ref, taps_ref, fdh_ref, fdl_ref, frh_ref, frl_ref, gh_ref, gl_ref,
                   bias_ref, invn_ref, o_ref):
    n = taps_ref.shape[0]
    z = u_ref[...]
    a = _dot3c(fdh_ref[...], fdl_ref[...], z)
    k = _dot3c(frh_ref[...], frl_ref[...], taps_ref[...])
    ar, ai, kr, ki = a[:n], a[n:], k[:n], k[n:]
    y = jnp.concatenate([ar * kr - ai * ki, ar * ki + ai * kr], axis=0)
    conv = _dot3c(gh_ref[...], gl_ref[...], y)
    o_ref[...] = ((conv * invn_ref[...] + z * bias_ref[...]) * x0_ref[...]).astype(BF16)


def _hy_ctx(u, x0, taps, bias, invn, small):
    rows, hw = u.shape
    full = lambda a: pl.BlockSpec(a.shape, lambda i: (0,) * a.ndim)
    args = [u, x0, taps, *small[0], *small[1], *small[2], bias, invn]
    return pl.pallas_call(
        _hy_ctx_kernel,
        grid=(1,),
        in_specs=[full(a) for a in args],
        out_specs=pl.BlockSpec((rows, hw), lambda i: (0, 0)),
        out_shape=jax.ShapeDtypeStruct((rows, hw), BF16),
        compiler_params=_cparams(("arbitrary",)),
    )(*args)


def _out_proj_kernel(x_ref, ypl_ref, yhl_ref, yal_ref, ypc_ref, yhc_ref, yac_ref, w_ref, g_ref, mod_ref, *rest,
                     nlt, tpb, ncond, pw, hw, moe, ntiles):
    if moe:
        rw_ref, xo_ref, u_ref, route_ref, un_ref = rest
    else:
        wg_ref, wu_ref, wd_ref, xo_ref = rest
    t = jnp.minimum(pl.program_id(0), ntiles - 1)
    if moe:
        @pl.when(pl.program_id(0) == 0)
        def _():
            un_ref[...] = jnp.zeros_like(un_ref)

        logits = _dot3(un_ref[...], rw_ref[...])
    ci = _cond_row(t, nlt, tpb, ncond)
    is_ctx = t >= nlt
    yp = jnp.where(is_ctx, ypc_ref[...], ypl_ref[...])
    yh = jnp.where(is_ctx, yhc_ref[...], yhl_ref[...])
    ya = jnp.where(is_ctx, yac_ref[...], yal_ref[...])
    mix = _dot(yp, w_ref[0:pw, :]) + _dot(yh, w_ref[pw:pw + hw, :]) + _dot(ya, w_ref[pw + hw:, :])
    x = x_ref[...] + mod_ref[2, pl.ds(ci, 1), :] * mix
    un = _normmod(x, g_ref[...], mod_ref[3, pl.ds(ci, 1), :], mod_ref[4, pl.ds(ci, 1), :])
    if not moe:
        u = un.astype(BF16)
        ff = wg_ref.shape[1]
        fc = ff // FFN_CHUNKS
        y = jnp.zeros(x.shape, F32)
        for c in range(FFN_CHUNKS):
            gate = _dot(u, wg_ref[:, c * fc:(c + 1) * fc])
            up = _dot(u, wu_ref[:, c * fc:(c + 1) * fc])
            y = y + _dot((_silu(gate) * up).astype(BF16), wd_ref[c * fc:(c + 1) * fc, :])
        xo_ref[...] = x + mod_ref[5, pl.ds(ci, 1), :] * y
    else:
        xo_ref[...] = x
        u_ref[...] = un.astype(BF16)
        un_ref[...] = un
        lane = lax.broadcasted_iota(jnp.int32, logits.shape, 1)
        neg = jnp.float32(-jnp.inf)
        lg = jnp.where(lane < N_EXPERTS, logits, neg)
        t1 = jnp.max(lg, axis=-1, keepdims=True)
        i1 = jnp.min(jnp.where(lg == t1, lane, LANES), axis=-1, keepdims=True)
        lg2 = jnp.where(lane == i1, neg, lg)
        t2 = jnp.max(lg2, axis=-1, keepdims=True)
        i2 = jnp.min(jnp.where(lg2 == t2, lane, LANES), axis=-1, keepdims=True)
        e2 = jnp.exp(t2 - t1)
        g1 = 1.0 / (1.0 + e2)
        g2 = e2 / (1.0 + e2)
        route_ref[...] = jnp.where(lane == 0, i1.astype(F32), jnp.where(lane == 1, i2.astype(F32),
                                   jnp.where(lane == 2, g1, jnp.where(lane == 3, g2, 0.0))))


def _out_proj(x, lat, ctx, w, g, mod, rw, ffn_w, *, nlt, tpb, ncond, out_rows):
    n, d = x.shape
    pw, hw, aw = (a.shape[1] for a in lat)
    tm = ROW_TILE
    moe = rw is not None
    nt = (n if moe else out_rows) // tm
    cur = lambda t: jnp.minimum(t, nt - 1)
    row = lambda width: pl.BlockSpec((tm, width), lambda t: (cur(t), 0))
    lrow = lambda width: pl.BlockSpec((tm, width), lambda t: (jnp.minimum(t, nlt - 1), 0))
    crow = lambda width: pl.BlockSpec((tm, width), lambda t: (jnp.maximum(cur(t) - nlt, 0), 0))
    in_specs = [row(d), lrow(pw), lrow(hw), lrow(aw), crow(pw), crow(hw), crow(aw),
                pl.BlockSpec(w.shape, lambda t: (0, 0)),
                pl.BlockSpec((1, d), lambda t: (0, 0)),
                pl.BlockSpec(mod.shape, lambda t: (0, 0, 0))]
    args = [x, *lat, *ctx, w, g, mod]
    scratch = []
    if not moe:
        const = lambda a: pl.BlockSpec(a.shape, lambda t: (0, 0), pipeline_mode=pl.Buffered(1))
        in_specs += [const(a) for a in ffn_w]
        args += list(ffn_w)
        out_specs = [row(d)]
        out_shape = [jax.ShapeDtypeStruct((out_rows, d), F32)]
    else:
        out_specs = [row(d), row(d)]
        out_shape = [jax.ShapeDtypeStruct((n, d), F32), jax.ShapeDtypeStruct((n, d), BF16)]
        in_specs.append(pl.BlockSpec(rw.shape, lambda t: (0, 0)))
        args.append(rw)
        out_specs.append(pl.BlockSpec((tm, LANES), lambda t: (jnp.maximum(t - 1, 0), 0)))
        out_shape.append(jax.ShapeDtypeStruct((n, LANES), F32))
        scratch.append(pltpu.VMEM((tm, d), F32))
    kern = functools.partial(_out_proj_kernel, nlt=nlt, tpb=tpb, ncond=ncond, pw=pw, hw=hw, moe=moe, ntiles=nt)
    return pl.pallas_call(
        kern, grid=(nt + 1 if moe else nt,), in_specs=in_specs, out_specs=out_specs, out_shape=out_shape,
        scratch_shapes=scratch,
        compiler_params=_cparams(("arbitrary",)),
    )(*args)


def _moe_kernel(te_ref, tv_ref, u_ref, wg_ref, wu_ref, wd_ref, *rest):
    o_ref = rest[-1]
    @pl.when(tv_ref[pl.program_id(0)] > 0)
    def _():
        u = u_ref[...]
        ff = wg_ref.shape[3]
        fc = ff // MOE_CHUNKS
        y = jnp.zeros(o_ref.shape, F32)
        for c in range(MOE_CHUNKS):
            sl = slice(c * fc, (c + 1) * fc)
            h = (_silu(_dot(u, wg_ref[0, 0, :, sl])) * _dot(u, wu_ref[0, 0, :, sl])).astype(BF16)
            y = y + _dot(h, wd_ref[0, 0, sl, :])
        o_ref[...] = y.astype(BF16)


def _moe_experts(ug, tile_expert, tile_valid, wg, wu, wd, j, *, tile0, total_rows, prev=None):
    p, d = ug.shape
    ff = wg.shape[3]
    tm = MOE_TM
    one = pl.Buffered(1)
    in_specs = [
        pl.BlockSpec((tm, d), lambda t, te, tv: (t, 0)),
        pl.BlockSpec((1, 1, d, ff), lambda t, te, tv: (j, te[t], 0, 0), pipeline_mode=one),
        pl.BlockSpec((1, 1, d, ff), lambda t, te, tv: (j, te[t], 0, 0), pipeline_mode=one),
        pl.BlockSpec((1, 1, ff, d), lambda t, te, tv: (j, te[t], 0, 0), pipeline_mode=one),
    ]
    args = [tile_expert, tile_valid, ug, wg, wu, wd]
    aliases = {}
    if prev is not None:
        in_specs.append(pl.BlockSpec(memory_space=pl.ANY))
        args.append(prev)
        aliases = {len(args) - 1: 0}
    grid_spec = pltpu.PrefetchScalarGridSpec(
        num_scalar_prefetch=2,
        grid=(p // tm,),
        in_specs=in_specs,
        out_specs=pl.BlockSpec((tm, d), lambda t, te, tv: (t + tile0, 0)),
    )
    return pl.pallas_call(
        _moe_kernel, grid_spec=grid_spec,
        out_shape=jax.ShapeDtypeStruct((total_rows, d), BF16),
        input_output_aliases=aliases,
        compiler_params=_cparams(("arbitrary",)),
    )(*args)


def _moe_combine_kernel(x_ref, ya_ref, yb_ref, route_ref, mod_ref, o_ref, *, nlt, tpb, ncond):
    ci = _cond_row(pl.program_id(0), nlt, tpb, ncond)
    r = route_ref[...]
    lane = lax.broadcasted_iota(jnp.int32, r.shape, 1)
    g1 = jnp.sum(jnp.where(lane == 2, r, 0.0), axis=-1, keepdims=True)
    g2 = jnp.sum(jnp.where(lane == 3, r, 0.0), axis=-1, keepdims=True)
    y = g1 * ya_ref[...].astype(F32) + g2 * yb_ref[...].astype(F32)
    o_ref[...] = x_ref[...] + mod_ref[5, pl.ds(ci, 1), :] * y


def _moe_combine(x, ya, yb, route, mod, *, nlt, tpb, ncond, out_rows):
    d = x.shape[1]
    tm = ROW_TILE
    row = lambda width: pl.BlockSpec((tm, width), lambda t: (t, 0))
    kern = functools.partial(_moe_combine_kernel, nlt=nlt, tpb=tpb, ncond=ncond)
    return pl.pallas_call(
        kern,
        grid=(out_rows // tm,),
        in_specs=[row(d), row(d), row(d), row(LANES), pl.BlockSpec(mod.shape, lambda t: (0, 0, 0))],
        out_specs=row(d),
        out_shape=jax.ShapeDtypeStruct((out_rows, d), F32),
        compiler_params=_cparams(("arbitrary",)),
    )(x, ya, yb, route, mod)


def _moe_layer(x, u, route, mod, wg, wu, wd, j, *, nlt, tpb, ncond, out_rows):
    n, d = x.shape
    tm = MOE_TM
    experts = jnp.concatenate([route[:, 0], route[:, 1]]).astype(jnp.int32)
    onehot = (experts[:, None] == jnp.arange(N_EXPERTS, dtype=jnp.int32)[None, :]).astype(jnp.int32)
    rank = jnp.sum(onehot * (jnp.cumsum(onehot, axis=0) - 1), axis=1)
    counts = jnp.sum(onehot, axis=0)
    padded = ((counts + tm - 1) // tm) * tm
    ends = jnp.cumsum(padded)
    starts = ends - padded
    dest = starts[experts] + rank
    p = 2 * n + N_EXPERTS * tm
    tokens = jnp.concatenate([jnp.arange(n, dtype=jnp.int32)] * 2)
    inb = dict(mode="promise_in_bounds")
    src = jnp.zeros((p,), jnp.int32).at[dest].set(tokens, unique_indices=True, **inb)
    tile_start = jnp.arange(p // tm, dtype=jnp.int32) * tm
    tile_expert = jnp.minimum(jnp.searchsorted(ends, tile_start, side="right"), N_EXPERTS - 1).astype(jnp.int32)
    tile_valid = (tile_start < ends[-1]).astype(jnp.int32)
    nt = p // tm
    cuts = [nt * k // MOE_PARTS for k in range(MOE_PARTS + 1)]
    ys = None
    for lo, hi in zip(cuts[:-1], cuts[1:]):
        ug = u.at[src[lo * tm:hi * tm]].get(**inb)
        ys = _moe_experts(ug, tile_expert[lo:hi], tile_valid[lo:hi], wg, wu, wd, j,
                          tile0=lo, total_rows=p, prev=ys)
    ya = ys.at[dest[:out_rows]].get(**inb)
    yb = ys.at[dest[n:n + out_rows]].get(**inb)
    return _moe_combine(x, ya, yb, route, mod, nlt=nlt, tpb=tpb, ncond=ncond, out_rows=out_rows)


def _rope_tables(seq, n_ctx_rows, batch):
    rows = seq // GRID_W
    row = jnp.repeat(jnp.arange(rows, dtype=F32), GRID_W)
    col = jnp.broadcast_to(jnp.arange(GRID_W, dtype=F32), (rows, GRID_W)).reshape(-1)
    inv_freq = jnp.power(ROPE_THETA, -jnp.arange(ROPE_FREQS, dtype=F32) / ROPE_FREQS)
    ar = row[:, None] * inv_freq
    ac = col[:, None] * inv_freq
    cos = jnp.concatenate([jnp.cos(ar), jnp.cos(ar), jnp.cos(ac), jnp.cos(ac)], axis=-1)
    sin = jnp.concatenate([-jnp.sin(ar), jnp.sin(ar), -jnp.sin(ac), jnp.sin(ac)], axis=-1)
    cos = jnp.tile(jnp.concatenate([cos, cos], axis=-1), (batch, 1))
    sin = jnp.tile(jnp.concatenate([sin, sin], axis=-1), (batch, 1))
    cos = jnp.concatenate([cos, jnp.ones((n_ctx_rows, 2 * HEAD_DIM), F32)], axis=0)
    sin = jnp.concatenate([sin, jnp.zeros((n_ctx_rows, 2 * HEAD_DIM), F32)], axis=0)
    return cos, sin


def kernel(x, c, ctx, c_ctx, mod_w, mod_b, norm1_g, norm2_g, w_in, w_out, pool_lin, pool_scale, hy_short_w, hy_short_b, hy_f_w1, hy_f_b1, hy_f_freq1, hy_f_w2, hy_f_b2, hy_f_freq2, hy_f_w3, hy_bias, qk_norm_g, diff_lambda, subln_g, ffn_w_gate, ffn_w_up, ffn_w_down, router_w, moe_w_gate, moe_w_up, moe_w_down):
    batch, seq, d = x.shape
    ctx_len = ctx.shape[1]
    depth = mod_w.shape[0]
    pw = pool_scale.shape[1]
    hw = hy_bias.shape[1]
    nl, nc = batch * seq, batch * ctx_len
    n = nl + nc
    tm = ROW_TILE
    assert seq % tm == 0 and nc % tm == 0 and seq % (FFT_N1 // 2) == 0 and seq % GRID_W == 0
    assert d == HEADS * 2 * V_DIM and pw == hw and batch == 2
    n2 = 2 * seq // FFT_N1
    ncols = n2 * hw
    assert (n * hw) % ncols == 0
    tiles = dict(nlt=nl // tm, tpb=seq // tm, ncond=batch + 1)

    xs = jnp.concatenate([x.reshape(nl, d), ctx.reshape(nc, d)], axis=0)
    cond8 = jnp.concatenate([c, c_ctx[None, :], jnp.zeros((8 - batch - 1, d), F32)], axis=0)
    mods = _modvec(cond8, mod_w, mod_b)
    cos_t, sin_t = _rope_tables(seq, nc, batch)
    consts = _dft_consts(n2)
    small = _dft_small_consts(ctx_len)
    eye = jnp.eye(len(POOL_WINDOWS), dtype=F32)
    n_moe = moe_w_gate.shape[0]
    assert n_moe <= depth
    moe_w2d = [w.reshape(-1, w.shape[-1]) for w in (moe_w_gate, moe_w_up, moe_w_down)]
    moe_bf16 = {}

    for i in range(depth):
        last = i == depth - 1
        lam_init = 0.8 - 0.6 * math.exp(-0.3 * i)
        mod = mods[i]
        gqk = jnp.concatenate([qk_norm_g[i], qk_norm_g[i]], axis=-1)
        zp, zh, qt, k12, vt = _in_proj(xs, norm1_g[i][None, :], mod, w_in[i].astype(BF16), cos_t, sin_t, gqk,
                                       pw=pw, hw=hw, **tiles)

        g_b = jnp.broadcast_to(subln_g[i][:, None], (V_DIM, ATT_TQ))
        att_kw = dict(batch=batch, seq=seq, ctx_len=ctx_len, lam_init=lam_init)

        cast_j = i if i < n_moe else None

        def attend(direct, lam_p=diff_lambda[i], qt=qt, k12=k12, vt=vt, g_b=g_b, att_kw=att_kw, cast_j=cast_j):
            ctx_out = _attention(lam_p, qt, k12, vt, g_b, latent=False, direct=direct, **att_kw)
            if cast_j is None:
                return (_attention(lam_p, qt, k12, vt, g_b, latent=True, direct=direct, **att_kw), ctx_out)
            if direct:
                lat_out, *wb = _attention(lam_p, qt, k12, vt, g_b, latent=True, direct=True,
                                          cast=[(w, cast_j, n_moe) for w in moe_w2d], **att_kw)
            else:
                lat_out = _attention(lam_p, qt, k12, vt, g_b, latent=True, direct=False, **att_kw)
                wb = _cast_bf16(*[w[cast_j * (w.shape[0] // n_moe):(cast_j + 1) * (w.shape[0] // n_moe)]
                                  for w in moe_w2d])
            return (lat_out, ctx_out, *wb)

        bound = (HEAD_DIM * QSCALE) * jnp.max(jnp.abs(qk_norm_g[i][0])) * jnp.max(jnp.abs(qk_norm_g[i][1]))
        ya_l, ya_c, *wb = lax.cond(bound * 1.02 < ATT_DIRECT_MAX, lambda: attend(True), lambda: attend(False))
        if cast_j is not None:
            moe_bf16[cast_j] = tuple(w.reshape((1,) + src.shape[1:])
                                     for w, src in zip(wb, (moe_w_gate, moe_w_up, moe_w_down)))

        lin_bd = (eye[:, None, :, None] * pool_lin[i][:, :, None, :]).reshape(pw, pw).astype(BF16)
        pscale = pool_scale[i][None, :]
        yp_l = _pool(zp, lin_bd, pscale, row0=0, batch=batch, seq=seq, out_rows=nl, out_row0=0)
        yp_c = _pool(zp, lin_bd, pscale, row0=nl, batch=batch, seq=ctx_len, out_rows=nc, out_row0=0)

        sw, sb = hy_short_w[i], hy_short_b[i][None, :]
        u_l, x0_l = _hy_pre(zh, sw, sb, row0=0, batch=batch, seq=seq, out_rows=n, out_row0=0)
        u_c, x0_c = _hy_pre(zh, sw, sb, row0=nl, batch=batch, seq=ctx_len, out_rows=nc, out_row0=0)
        filt = (hy_f_w1[i], hy_f_b1[i], hy_f_freq1[i], hy_f_w2[i], hy_f_b2[i], hy_f_freq2[i], hy_f_w3[i])
        bias = hy_bias[i][None, :]
        taps, asum = _hy_filter(seq, *filt, hw)
        kr1, ki1 = _fft_first(taps.reshape(FFT_N1, ncols), consts["f1_real"], ncols)
        spectrum = _fft_mid(kr1, ki1, consts, n2, hw)
        ar, ai = _fft_first(u_l.reshape(-1, ncols), consts["f1_data"], ncols)
        br, bi = _fft_mid(ar, ai, consts, n2, hw, spectrum=spectrum)
        reps = min(FFT_COLS, ncols) // hw
        yh_lat = _fft_last(br.reshape(FFT_N1, ncols), bi.reshape(FFT_N1, ncols), consts["g1"],
                           u_l.reshape(-1, ncols), x0_l.reshape(-1, ncols),
                           jnp.tile(bias, (1, reps)), jnp.tile(1.0 / asum, (1, reps)), ncols)
        taps_c, asum_c = _hy_filter(ctx_len, *filt, hw)
        yh_ctx = _hy_ctx(u_c, x0_c, taps_c, bias, 1.0 / asum_c, small)

        j = i // 2
        moe = i % 2 == 1
        rw = ffn_w = None
        if moe:
            rw = jnp.concatenate([router_w[j], jnp.zeros((d, LANES - N_EXPERTS), F32)], axis=1)
        else:
            ffn_w = (ffn_w_gate[j].astype(BF16), ffn_w_up[j].astype(BF16), ffn_w_down[j].astype(BF16))
        out_rows = nl if last else n
        outs = _out_proj(xs, (yp_l, yh_lat.reshape(nl, hw), ya_l), (yp_c, yh_ctx, ya_c), w_out[i].astype(BF16),
                         norm2_g[i][None, :], mod, rw, ffn_w, out_rows=out_rows, **tiles)
        if moe:
            xs, u, route = outs
            xs = _moe_layer(xs, u, route, mod, *moe_bf16[j], 0, out_rows=out_rows, **tiles)
        else:
            xs, = outs
    return xs[:nl].reshape(batch, seq, d)
```

```python
import functools
import math

import numpy as np
import jax
import jax.numpy as jnp
from jax import lax
from jax.experimental import pallas as pl
from jax.experimental.pallas import tpu as pltpu

F32 = jnp.float32
BF16 = jnp.bfloat16
EPS = 1e-6

GRID_W = 64
POOL_WINDOWS = (2, 4, 8, 16)
HEADS = 4
HEAD_DIM = 64
V_DIM = 128
ROPE_FREQS = 16
ROPE_THETA = 10000.0
HY_BANDS = 16
HY_TARGET = 1e-2
HY_FAST = 0.3
HY_SLOW = 1.5
N_EXPERTS = 8

LANES = 128
ROW_TILE = 512
HALO = 32
FFT_N1 = 128
SEQ_TILE = 1024
FFT_COLS = 4096
FFT_KB = 4
ATT_TQ = 1024
ATT_TK = 512
ATT_UNROLL = 16
ATT_GROUP = 8
QSCALE = (HEAD_DIM ** -0.5) * math.log2(math.e)
ATT_DIRECT_MAX = 100.0
FFN_CHUNKS = 2
CAST_STEPS = 64
MOE_TM = 512
MOE_PARTS = 4
MOE_CHUNKS = 2
VMEM_LIMIT = 56 * 1024 * 1024


def _cparams(sem):
    return pltpu.CompilerParams(dimension_semantics=sem, vmem_limit_bytes=VMEM_LIMIT)


def _dot(a, b):
    return jnp.dot(a, b, preferred_element_type=F32)


def _split(a):
    hi = a.astype(BF16)
    lo = (a - hi.astype(F32)).astype(BF16)
    return hi, lo


def _dot3(a, b):
    ah, al = _split(a)
    bh, bl = _split(b)
    return _dot(ah, bh) + _dot(ah, bl) + _dot(al, bh)


def _dot3c(ch, cl, d):
    dh, dl = _split(d)
    return _dot(ch, dh) + _dot(ch, dl) + _dot(cl, dh)


def _silu(x):
    return x / (1.0 + jnp.exp(-x))


def _normmod(x, g, shift, scale):
    ms = jnp.mean(x * x, axis=-1, keepdims=True)
    return (x * lax.rsqrt(ms + EPS) * g) * (1.0 + scale) + shift


def _cond_row(t, n_lat_tiles, tiles_per_batch, n_cond):
    return jnp.where(t >= n_lat_tiles, n_cond - 1, t // tiles_per_batch)


def _cast_kernel(*refs):
    half = len(refs) // 2
    for w_ref, o_ref in zip(refs[:half], refs[half:]):
        o_ref[...] = w_ref[...].astype(BF16)


def _cast_bf16(*ws):
    flat = [w.reshape(-1, w.shape[-1]) for w in ws]
    steps = CAST_STEPS
    specs = [pl.BlockSpec((w.shape[0] // steps, w.shape[1]), lambda t: (t, 0)) for w in flat]
    outs = pl.pallas_call(
        _cast_kernel,
        grid=(steps,),
        in_specs=specs,
        out_specs=specs,
        out_shape=[jax.ShapeDtypeStruct(w.shape, BF16) for w in flat],
        compiler_params=_cparams(("arbitrary",)),
    )(*flat)
    return [o.reshape(w.shape) for o, w in zip(outs, ws)]


def _modvec_kernel(c_ref, w_ref, b_ref, o_ref):
    s = _silu(c_ref[...])
    o_ref[0, 0] = _dot3(s, w_ref[0]) + b_ref[0]


def _modvec(cond8, mod_w, mod_b):
    depth, d, six_d = mod_w.shape
    nchunk = six_d // d
    return pl.pallas_call(
        _modvec_kernel,
        grid=(depth, nchunk),
        in_specs=[
            pl.BlockSpec((8, d), lambda i, j: (0, 0)),
            pl.BlockSpec((1, d, d), lambda i, j: (i, 0, j)),
            pl.BlockSpec((1, 1, d), lambda i, j: (i, 0, j)),
        ],
        out_specs=pl.BlockSpec((1, 1, 8, d), lambda i, j: (i, j, 0, 0)),
        out_shape=jax.ShapeDtypeStruct((depth, nchunk, 8, d), F32),
        compiler_params=_cparams(("arbitrary", "arbitrary")),
    )(cond8, mod_w, mod_b.reshape(depth, 1, six_d))


def _in_proj_kernel(x_ref, g_ref, mod_ref, w_ref, cos_ref, sin_ref, gqk_ref,
                    zp_ref, zh_ref, qt_ref, k_ref, vt_ref, zatt_ref, *, nlt, tpb, ncond, pw, hw, ntiles):
    t = pl.program_id(0)

    @pl.when(t == 0)
    def _():
        zatt_ref[...] = jnp.zeros_like(zatt_ref)

    tm = x_ref.shape[0]
    att = pw + 3 * hw
    lane = lax.broadcasted_iota(jnp.int32, (tm, 2 * HEAD_DIM), 1)
    first = lane < HEAD_DIM
    apart = (lane % (2 * ROPE_FREQS)) < ROPE_FREQS
    cos = cos_ref[...]
    sin = sin_ref[...]
    qk_w = HEADS * 2 * HEAD_DIM

    def norm_rope(v, g):
        v2 = v * v
        s_all = jnp.sum(v2, axis=-1, keepdims=True)
        s_first = jnp.sum(jnp.where(first, v2, 0.0), axis=-1, keepdims=True)
        ms = jnp.where(first, s_first, s_all - s_first) * (1.0 / HEAD_DIM)
        vn = v * lax.rsqrt(ms + EPS) * g
        swapped = jnp.where(apart, pltpu.roll(vn, 2 * HEAD_DIM - ROPE_FREQS, 1), pltpu.roll(vn, ROPE_FREQS, 1))
        return vn * cos + swapped * sin

    for h in range(HEADS):
        lo = h * 2 * HEAD_DIM
        q = norm_rope(zatt_ref[:, lo:lo + 2 * HEAD_DIM], gqk_ref[0:1, :])
        qt_ref[h] = (q * QSCALE).T.astype(BF16)
        k = norm_rope(zatt_ref[:, qk_w + lo:qk_w + lo + 2 * HEAD_DIM], gqk_ref[1:2, :])
        k_ref[h] = k.astype(BF16)
        vlo = 2 * qk_w + h * V_DIM
        vt_ref[h] = zatt_ref[:, vlo:vlo + V_DIM].T.astype(BF16)

    ci = _cond_row(jnp.minimum(t, ntiles - 1), nlt, tpb, ncond)
    shift = mod_ref[0, pl.ds(ci, 1), :]
    scale = mod_ref[1, pl.ds(ci, 1), :]
    y = _normmod(x_ref[...], g_ref[...], shift, scale).astype(BF16)
    z = _dot(y, w_ref[...])
    zp_ref[...] = z[:, :pw].astype(BF16)
    zh_ref[...] = z[:, pw:att].astype(BF16)
    zatt_ref[...] = z[:, att:]


def _in_proj(x, g, mod, w, cos_t, sin_t, gqk, *, nlt, tpb, ncond, pw, hw):
    n, d = x.shape
    wid = w.shape[1]
    tm = ROW_TILE
    nt = n // tm
    kern = functools.partial(_in_proj_kernel, nlt=nlt, tpb=tpb, ncond=ncond, pw=pw, hw=hw, ntiles=nt)
    cur = lambda t: jnp.minimum(t, nt - 1)
    prv = lambda t: jnp.maximum(t - 1, 0)
    return pl.pallas_call(
        kern,
        grid=(nt + 1,),
        in_specs=[
            pl.BlockSpec((tm, d), lambda t: (cur(t), 0)),
            pl.BlockSpec((1, d), lambda t: (0, 0)),
            pl.BlockSpec(mod.shape, lambda t: (0, 0, 0)),
            pl.BlockSpec((d, wid), lambda t: (0, 0)),
            pl.BlockSpec((tm, 2 * HEAD_DIM), lambda t: (prv(t), 0)),
            pl.BlockSpec((tm, 2 * HEAD_DIM), lambda t: (prv(t), 0)),
            pl.BlockSpec((2, 2 * HEAD_DIM), lambda t: (0, 0)),
        ],
        out_specs=[
            pl.BlockSpec((tm, pw), lambda t: (cur(t), 0)),
            pl.BlockSpec((tm, 3 * hw), lambda t: (cur(t), 0)),
            pl.BlockSpec((HEADS, 2 * HEAD_DIM, tm), lambda t: (0, 0, prv(t))),
            pl.BlockSpec((HEADS, tm, 2 * HEAD_DIM), lambda t: (0, prv(t), 0)),
            pl.BlockSpec((HEADS, V_DIM, tm), lambda t: (0, 0, prv(t))),
        ],
        scratch_shapes=[pltpu.VMEM((tm, wid - pw - 3 * hw), F32)],
        out_shape=[
            jax.ShapeDtypeStruct((n, pw), BF16),
            jax.ShapeDtypeStruct((n, 3 * hw), BF16),
            jax.ShapeDtypeStruct((HEADS, 2 * HEAD_DIM, n), BF16),
            jax.ShapeDtypeStruct((HEADS, n, 2 * HEAD_DIM), BF16),
            jax.ShapeDtypeStruct((HEADS, V_DIM, n), BF16),
        ],
        compiler_params=_cparams(("arbitrary",)),
    )(x, g, mod, w, cos_t, sin_t, gqk)


def _attn_kernel(lam_ref, qt_ref, kc_ref, vtc_ref, *rest, tk, n_lat_chunks, lam_init):
    if n_lat_chunks:
        kl_ref, vtl_ref, g_ref, o_ref, m_ref, l_ref, acc_ref = rest
    else:
        g_ref, o_ref, m_ref, l_ref, acc_ref = rest
    qt = qt_ref[0]
    row = lax.broadcasted_iota(jnp.int32, qt.shape, 0)
    zero = jnp.zeros_like(qt)
    qmaps = (jnp.where(row < HEAD_DIM, qt, zero), jnp.where(row >= HEAD_DIM, qt, zero))

    def process(k_tile, vt_tile, first):
        for mi in range(2):
            s = _dot(k_tile, qmaps[mi])
            smax = jnp.max(s, axis=0, keepdims=True)
            if first:
                m_new = smax
            else:
                m_old = m_ref[mi]
                m_new = jnp.maximum(m_old, smax)
            p = jnp.exp2(s - m_new)
            psum = jnp.sum(p, axis=0, keepdims=True)
            pv = _dot(vt_tile, p.astype(BF16))
            if first:
                l_ref[mi] = psum
                acc_ref[mi] = pv
            else:
                alpha = jnp.exp2(m_old - m_new)
                l_ref[mi] = alpha * l_ref[mi] + psum
                acc_ref[mi] = alpha * acc_ref[mi] + pv
            m_ref[mi] = m_new

    process(kc_ref[0], vtc_ref[0], True)
    if n_lat_chunks:
        def body(j, carry):
            off = pl.multiple_of(j * tk, tk)
            process(kl_ref[0, pl.ds(off, tk), :], vtl_ref[0, :, pl.ds(off, tk)], False)
            return carry
        lax.fori_loop(0, n_lat_chunks, body, 0)

    lv = lam_ref[...]
    lam = (jnp.exp(jnp.sum(lv[0:1] * lv[1:2], axis=-1, keepdims=True))
           - jnp.exp(jnp.sum(lv[2:3] * lv[3:4], axis=-1, keepdims=True)) + lam_init)
    o = acc_ref[0] * (1.0 / l_ref[0]) - lam * (acc_ref[1] * (1.0 / l_ref[1]))
    ms = jnp.mean(o * o, axis=0, keepdims=True)
    y = o * lax.rsqrt(ms + EPS) * g_ref[...] * (1.0 - lam_init)
    o_ref[...] = y.T.astype(BF16)


def _attn_direct_kernel(lam_ref, qt_ref, kc_ref, vtc_ref, *rest, tk, n_lat_chunks, lam_init, n_cast=0):
    if n_lat_chunks:
        kl_ref, vtl_ref, g_ref = rest[:3]
        rest = rest[3:]
    else:
        g_ref = rest[0]
        rest = rest[1:]
    cast_in, (o_ref, *cast_out), (p_ref, l_ref, acc_ref) = rest[:n_cast], rest[n_cast:2 * n_cast + 1], rest[2 * n_cast + 1:]
    for w_ref, c_ref in zip(cast_in, cast_out):
        c_ref[...] = w_ref[...].astype(BF16)
    qt = qt_ref[0]
    tq = qt.shape[1]
    row = lax.broadcasted_iota(jnp.int32, qt.shape, 0)
    zero = jnp.zeros_like(qt)
    qmaps = (jnp.where(row < HEAD_DIM, qt, zero), jnp.where(row >= HEAD_DIM, qt, zero))
    grp = ATT_GROUP

    def keys(first_piece, count):
        if isinstance(first_piece, int):
            return pl.ds((first_piece - 1) * tk, count * tk)
        return pl.ds(pl.multiple_of((first_piece - 1) * tk, tk), count * tk)

    def stage_ab(k_tile, pset, r):
        for mi in range(2):
            p = jnp.exp2(_dot(k_tile, qmaps[mi]))
            l_ref[mi] = l_ref[mi] + jnp.sum(p.reshape(tk // 8, 8, tq), axis=0)
            p_ref[pset, mi, r * tk:(r + 1) * tk, :] = p.astype(BF16)

    def stage_c(vt_tile, pset, lo, count):
        for mi in range(2):
            acc_ref[mi] = acc_ref[mi] + _dot(vt_tile, p_ref[pset, mi, lo * tk:(lo + count) * tk, :])

    def group_ab(g, pset):
        k_tile = kl_ref[0, keys(grp * g, grp), :]
        for mi in range(2):
            p = jnp.exp2(_dot(k_tile, qmaps[mi]))
            l_ref[mi] = l_ref[mi] + jnp.sum(p.reshape(grp * tk // 8, 8, tq), axis=0)
            p_ref[pset, mi] = p.astype(BF16)

    def group_c(g, pset):
        stage_c(vtl_ref[0, :, keys(grp * g, grp)], pset, 0, grp)

    l_ref[...] = jnp.zeros_like(l_ref)
    acc_ref[...] = jnp.zeros_like(acc_ref)
    stage_ab(kc_ref[0], 0, 0)
    if n_lat_chunks == 0:
        stage_c(vtc_ref[0], 0, 0, 1)
    else:
        n_groups = n_lat_chunks // grp
        per_step = ATT_UNROLL // grp
        for r in range(1, grp):
            stage_ab(kl_ref[0, keys(r, 1), :], 0, r)
        group_ab(1, 1)
        stage_c(vtc_ref[0], 0, 0, 1)
        stage_c(vtl_ref[0, :, keys(1, grp - 1)], 0, 1, grp - 1)
        for g in range(2, per_step):
            group_ab(g, g % 2)
            group_c(g - 1, (g - 1) % 2)

        def body(jj, carry):
            g0 = per_step * (jj + 1)
            for q in range(per_step):
                group_ab(g0 + q, q % 2)
                group_c(g0 + q - 1, (q + 1) % 2)
            return carry

        lax.fori_loop(0, n_groups // per_step - 1, body, 0)
        stage_ab(kl_ref[0, keys(n_lat_chunks, 1), :], 0, 0)
        group_c(n_groups - 1, 1)
        stage_c(vtl_ref[0, :, keys(n_lat_chunks, 1)], 0, 0, 1)

    lv = lam_ref[...]
    lam = (jnp.exp(jnp.sum(lv[0:1] * lv[1:2], axis=-1, keepdims=True))
           - jnp.exp(jnp.sum(lv[2:3] * lv[3:4], axis=-1, keepdims=True)) + lam_init)
    l1 = jnp.sum(l_ref[0], axis=0, keepdims=True)
    l2 = jnp.sum(l_ref[1], axis=0, keepdims=True)
    o = acc_ref[0] * (1.0 / l1) - lam * (acc_ref[1] * (1.0 / l2))
    ms = jnp.mean(o * o, axis=0, keepdims=True)
    y = o * lax.rsqrt(ms + EPS) * g_ref[...] * (1.0 - lam_init)
    o_ref[...] = y.T.astype(BF16)


def _attention(lam_p, qt, k12, vt, g_b, *, batch, seq, ctx_len, latent, lam_init, direct, cast=()):
    n = k12.shape[1]
    nl = batch * seq
    tq = ATT_TQ if latent else ctx_len
    nq = seq // tq if latent else 1
    tk = ctx_len if direct else ATT_TK
    assert seq % ((ATT_UNROLL if direct else 1) * tk) == 0
    qbase = 0 if latent else nl // tq

    def qrow(b, iq):
        return (b * nq + iq) if latent else (qbase + b)

    in_specs = [
        pl.BlockSpec(lam_p.shape, lambda b, h, iq: (0, 0)),
        pl.BlockSpec((1, 2 * HEAD_DIM, tq), lambda b, h, iq: (h, 0, qrow(b, iq))),
        pl.BlockSpec((1, ctx_len, 2 * HEAD_DIM), lambda b, h, iq: (h, nl // ctx_len + b, 0)),
        pl.BlockSpec((1, V_DIM, ctx_len), lambda b, h, iq: (h, 0, nl // ctx_len + b)),
    ]
    args = [lam_p, qt, k12, vt]
    if latent:
        in_specs += [
            pl.BlockSpec((1, seq, 2 * HEAD_DIM), lambda b, h, iq: (h, b, 0)),
            pl.BlockSpec((1, V_DIM, seq), lambda b, h, iq: (h, 0, b)),
        ]
        args += [k12, vt]
    in_specs.append(pl.BlockSpec((V_DIM, tq), lambda b, h, iq: (0, 0)))
    args.append(g_b[:, :tq])
    nrows = nl if latent else batch * ctx_len
    body = _attn_direct_kernel if direct else _attn_kernel
    kw = dict(tk=tk, n_lat_chunks=(seq // tk if latent else 0), lam_init=lam_init)
    out_specs = [pl.BlockSpec((tq, V_DIM), lambda b, h, iq: (b * nq + iq, h))]
    out_shape = [jax.ShapeDtypeStruct((nrows, HEADS * V_DIM), BF16)]
    if cast:
        assert direct
        kw["n_cast"] = len(cast)
        steps = batch * HEADS * nq
        for w, slab, parts in cast:
            rows = w.shape[0] // parts // steps
            in_specs.append(pl.BlockSpec(
                (rows, w.shape[1]), lambda b, h, iq, slab=slab: (slab * steps + (b * HEADS + h) * nq + iq, 0)))
            args.append(w)
            out_specs.append(pl.BlockSpec((rows, w.shape[1]), lambda b, h, iq: ((b * HEADS + h) * nq + iq, 0)))
            out_shape.append(jax.ShapeDtypeStruct((w.shape[0] // parts, w.shape[1]), BF16))
    kern = functools.partial(body, **kw)
    if direct:
        scratch = [pltpu.VMEM((2, 2, ATT_GROUP * tk, tq), BF16),
                   pltpu.VMEM((2, 8, tq), F32), pltpu.VMEM((2, V_DIM, tq), F32)]
    else:
        scratch = [pltpu.VMEM((2, 1, tq), F32), pltpu.VMEM((2, 1, tq), F32), pltpu.VMEM((2, V_DIM, tq), F32)]
    outs = pl.pallas_call(
        kern,
        grid=(batch, HEADS, nq),
        in_specs=in_specs,
        out_specs=out_specs,
        out_shape=out_shape,
        scratch_shapes=scratch,
        compiler_params=_cparams(("arbitrary", "arbitrary", "arbitrary")),
    )(*args)
    return outs if cast else outs[0]


def _halo_specs(width, *, row0, seq, tile):
    hb = tile // HALO

    def cur(b, i):
        return ((row0 + b * seq) // tile + i, 0)

    def prev(b, i):
        first = (row0 + b * seq) // HALO
        return (jnp.maximum(first + i * hb - 1, first), 0)

    def nxt(b, i):
        first = (row0 + b * seq) // HALO
        return (jnp.minimum(first + (i + 1) * hb, first + seq // HALO - 1), 0)

    return [pl.BlockSpec((HALO, width), prev), pl.BlockSpec((tile, width), cur), pl.BlockSpec((HALO, width), nxt)]


def _fill_ext(ext_ref, prev_ref, cur_ref, next_ref, *, seq, tile):
    pos0 = pl.program_id(1) * tile
    width = cur_ref.shape[1]
    hpos = lax.broadcasted_iota(jnp.int32, (HALO, width), 0)
    ext_ref[0:HALO, :] = jnp.where(pos0 - HALO + hpos >= 0, prev_ref[...].astype(F32), 0.0)
    ext_ref[HALO:HALO + tile, :] = cur_ref[...].astype(F32)
    ext_ref[HALO + tile:, :] = jnp.where(pos0 + tile + hpos < seq, next_ref[...].astype(F32), 0.0)


def _pool_kernel(prev_ref, cur_ref, next_ref, lin_ref, scale_ref, o_ref, ext_ref, s_ref, *, seq, tile):
    _fill_ext(ext_ref, prev_ref, cur_ref, next_ref, seq=seq, tile=tile)
    width = cur_ref.shape[1]
    gd = width // len(POOL_WINDOWS)
    n0 = tile + 2 * HALO
    s_ref[0, 0:n0 - 8, :] = ext_ref[0:n0 - 8, :] + ext_ref[1:n0 - 7, :]
    for k in range(1, len(POOL_WINDOWS)):
        step = 1 << k
        ln = n0 - 8 * (k + 1)
        s_ref[k, 0:ln, :] = s_ref[k - 1, 0:ln, :] + s_ref[k - 1, step:step + ln, :]
    lane = lax.broadcasted_iota(jnp.int32, (tile, width), 1)
    pos = pl.program_id(1) * tile + lax.broadcasted_iota(jnp.int32, (tile, width), 0)
    grp = lane // gd
    wsum = jnp.zeros((tile, width), F32)
    half = jnp.zeros((tile, width), jnp.int32)
    for k, win in enumerate(POOL_WINDOWS):
        start = HALO - win // 2
        wsum = jnp.where(grp == k, s_ref[k, start:start + tile, :], wsum)
        half = jnp.where(grp == k, win // 2, half)
    cnt = jnp.minimum(pos + half, seq) - jnp.maximum(pos - half, 0)
    z = ext_ref[HALO:HALO + tile, :]
    dlt = (wsum / cnt.astype(F32) - z).astype(BF16)
    o_ref[...] = (_dot(dlt, lin_ref[...]) * scale_ref[...]).astype(BF16)


def _pool(zp, lin_bd, scale, *, row0, batch, seq, out_rows, out_row0):
    width = zp.shape[1]
    tile = min(SEQ_TILE, seq)
    kern = functools.partial(_pool_kernel, seq=seq, tile=tile)
    return pl.pallas_call(
        kern,
        grid=(batch, seq // tile),
        in_specs=_halo_specs(width, row0=row0, seq=seq, tile=tile) + [
            pl.BlockSpec((width, width), lambda b, i: (0, 0)),
            pl.BlockSpec((1, width), lambda b, i: (0, 0)),
        ],
        out_specs=pl.BlockSpec((tile, width), lambda b, i: ((out_row0 + b * seq) // tile + i, 0)),
        out_shape=jax.ShapeDtypeStruct((out_rows, width), BF16),
        scratch_shapes=[
            pltpu.VMEM((tile + 2 * HALO, width), F32),
            pltpu.VMEM((4, tile + 2 * HALO, width), F32),
        ],
        compiler_params=_cparams(("arbitrary", "arbitrary")),
    )(zp, zp, zp, lin_bd, scale)


def _hy_pre_kernel(prev_ref, cur_ref, next_ref, w_ref, b_ref, u_ref, x0_ref, ext_ref, *, seq, tile, hw):
    _fill_ext(ext_ref, prev_ref, cur_ref, next_ref, seq=seq, tile=tile)
    y = b_ref[...] + ext_ref[HALO - 1:HALO - 1 + tile, :] * w_ref[0:1, :]
    y = y + ext_ref[HALO:HALO + tile, :] * w_ref[1:2, :]
    y = y + ext_ref[HALO + 1:HALO + 1 + tile, :] * w_ref[2:3, :]
    x0_ref[...] = y[:, :hw]
    u_ref[...] = y[:, 2 * hw:] * y[:, hw:2 * hw]


def _hy_pre(zh, sw, sb, *, row0, batch, seq, out_rows, out_row0):
    width = zh.shape[1]
    hw = width // 3
    tile = min(SEQ_TILE, seq)
    kern = functools.partial(_hy_pre_kernel, seq=seq, tile=tile, hw=hw)
    ospec = pl.BlockSpec((tile, hw), lambda b, i: ((out_row0 + b * seq) // tile + i, 0))
    return pl.pallas_call(
        kern,
        grid=(batch, seq // tile),
        in_specs=_halo_specs(width, row0=row0, seq=seq, tile=tile) + [
            pl.BlockSpec((3, width), lambda b, i: (0, 0)),
            pl.BlockSpec((1, width), lambda b, i: (0, 0)),
        ],
        out_specs=[ospec, ospec],
        out_shape=[jax.ShapeDtypeStruct((out_rows, hw), F32), jax.ShapeDtypeStruct((out_rows, hw), F32)],
        scratch_shapes=[pltpu.VMEM((tile + 2 * HALO, width), F32)],
        compiler_params=_cparams(("arbitrary", "arbitrary")),
    )(zh, zh, zh, sw, sb)


def _filter_kernel(feat_ref, w1_ref, b1_ref, f1_ref, w2_ref, b2_ref, f2_ref, w3_ref, dl_ref,
                   taps_ref, asum_ref, *, tile, hw, seq, embp):
    feat = feat_ref[...]
    h = jnp.sin(f1_ref[...] * (_dot3(feat, w1_ref[...]) + b1_ref[...]))
    h = jnp.sin(f2_ref[...] * (_dot3(h, w2_ref[...]) + b2_ref[...]))
    h = _dot3(h, w3_ref[...])
    odd = lax.broadcasted_iota(jnp.int32, (tile, 2 * hw), 1) >= hw
    tcol = jnp.where(odd, feat[:, embp:embp + 1], feat[:, 0:1])
    dec = jnp.exp(-tcol * dl_ref[...])
    row = pl.program_id(0) * tile + lax.broadcasted_iota(jnp.int32, (tile, 2 * hw), 0)
    pos = 2 * row + odd.astype(jnp.int32)
    taps = jnp.where(pos == seq, 0.0, h * dec)
    taps_ref[...] = taps
    part = jnp.sum(jnp.abs(taps), axis=0, keepdims=True)

    @pl.when(pl.program_id(0) == 0)
    def _():
        asum_ref[...] = part

    @pl.when(pl.program_id(0) != 0)
    def _():
        asum_ref[...] = asum_ref[...] + part


def _hy_filter(seq, w1, b1, f1, w2, b2, f2, w3, hw):
    ar = jnp.arange(seq, dtype=jnp.int32)
    pos = jnp.concatenate([ar, (seq - ar) % seq]).astype(F32)[:, None]
    t = pos * (1.0 / (seq - 1))
    w = (2.0 * math.pi / seq) * pos
    bands = jnp.linspace(1e-4, HY_BANDS - 1, HY_BANDS, dtype=F32)[None, :]
    emb = 1 + 2 * HY_BANDS
    embp = ((emb + 7) // 8) * 8
    feat = jnp.concatenate([t, jnp.cos(bands * w), -jnp.sin(bands * w), jnp.zeros((2 * seq, embp - emb), F32)], axis=-1)
    w1p = jnp.concatenate([w1, jnp.zeros((embp - emb, w1.shape[1]), F32)], axis=0)
    max_decay = math.log(1.0 / HY_TARGET) / HY_FAST
    min_decay = math.log(1.0 / HY_TARGET) / HY_SLOW
    deltas = jnp.linspace(min_decay, max_decay, hw, dtype=F32)[None, :]
    feat = feat.reshape(seq, 2 * embp)
    hid = w1.shape[1]

    def pair(m):
        z = jnp.zeros_like(m)
        return jnp.concatenate([jnp.concatenate([m, z], axis=1), jnp.concatenate([z, m], axis=1)], axis=0)

    twice = lambda v: jnp.concatenate([v, v])[None, :]
    w3p = jnp.concatenate([pair(w3[:, :hw]), pair(w3[:, hw:])], axis=1)
    tile = min(SEQ_TILE, seq) // 2
    nfwd = seq // (2 * tile)
    full = lambda shape: pl.BlockSpec(shape, lambda i: (0,) * len(shape))
    taps, asum = pl.pallas_call(
        functools.partial(_filter_kernel, tile=tile, hw=hw, seq=seq, embp=embp),
        grid=(2 * nfwd,),
        in_specs=[
            pl.BlockSpec((tile, 2 * embp), lambda i: (i, 0)),
            full((2 * embp, 2 * hid)), full((1, 2 * hid)), full((1, 2 * hid)),
            full((2 * hid, 2 * hid)), full((1, 2 * hid)), full((1, 2 * hid)),
            pl.BlockSpec((2 * hid, 2 * hw), lambda i: (0, i // nfwd)), full((1, 2 * hw)),
        ],
        out_specs=[
            pl.BlockSpec((tile, 2 * hw), lambda i: (i, 0)),
            pl.BlockSpec((1, 2 * hw), lambda i: (0, 0)),
        ],
        out_shape=[
            jax.ShapeDtypeStruct((seq, 2 * hw), F32),
            jax.ShapeDtypeStruct((1, 2 * hw), F32),
        ],
        compiler_params=_cparams(("arbitrary",)),
    )(feat, pair(w1p), twice(b1), twice(f1), pair(w2), twice(b2), twice(f2), w3p,
      jnp.concatenate([deltas, deltas], axis=1))
    return taps.reshape(2 * seq, hw), asum[:, :hw] + asum[:, hw:]


def _np_split(a):
    a32 = jnp.asarray(a, F32)
    hi = a32.astype(BF16)
    lo = (a32 - hi.astype(F32)).astype(BF16)
    return hi, lo


def _dft_consts(n2):
    n1 = FFT_N1
    n = n1 * n2
    half = n1 // 2
    a1 = -2.0 * np.pi * np.outer(np.arange(n1), np.arange(n1)) / n1
    f1r, f1i = np.cos(a1), np.sin(a1)
    f1_data = np.block([[f1r[:, :half], -f1i[:, :half]], [f1i[:, :half], f1r[:, :half]]])
    f1_real = np.concatenate([f1r, f1i], axis=0)
    g1r, g1i = f1r / n, -f1i / n
    g1 = np.block([[g1r[:half], -g1i[:half]], [g1i[:half], g1r[:half]]])
    a2 = -2.0 * np.pi * np.outer(np.arange(n2), np.arange(n2)) / n2
    f2r, f2i = np.cos(a2), np.sin(a2)
    f2 = np.block([[f2r, -f2i], [f2i, f2r]])
    g2 = np.block([[f2r, f2i], [-f2i, f2r]])
    at = -2.0 * np.pi * np.outer(np.arange(n1), np.arange(n2)) / n
    twr = jnp.broadcast_to(jnp.asarray(np.cos(at), F32)[:, :, None], (n1, n2, LANES))
    twi = jnp.broadcast_to(jnp.asarray(np.sin(at), F32)[:, :, None], (n1, n2, LANES))
    return dict(f1_data=_np_split(f1_data), f1_real=_np_split(f1_real), g1=_np_split(g1),
                f2=_np_split(f2), g2=_np_split(g2), twr=twr, twi=twi)


def _fft1_kernel(z_ref, fh_ref, fl_ref, ar_ref, ai_ref):
    a = _dot3c(fh_ref[...], fl_ref[...], z_ref[...])
    ar_ref[...] = a[:FFT_N1]
    ai_ref[...] = a[FFT_N1:]


def _fft_first(zview, fmat, ncols):
    cb = min(FFT_COLS, ncols)
    fh, fl = fmat
    cspec = pl.BlockSpec(fh.shape, lambda j: (0, 0))
    ospec = pl.BlockSpec((FFT_N1, cb), lambda j: (0, j))
    return pl.pallas_call(
        _fft1_kernel,
        grid=(ncols // cb,),
        in_specs=[pl.BlockSpec((FFT_N1, cb), lambda j: (0, j)), cspec, cspec],
        out_specs=[ospec, ospec],
        out_shape=[jax.ShapeDtypeStruct((FFT_N1, ncols), F32)] * 2,
        compiler_params=_cparams(("arbitrary",)),
    )(zview, fh, fl)


def _fftmid_kernel(ar_ref, ai_ref, twr_ref, twi_ref, f2h_ref, f2l_ref, *rest, filter_only, n2):
    kb, _, width = ar_ref.shape
    reps = width // LANES
    lanes = lambda parts: jnp.concatenate(parts, axis=1)
    twr = lanes([t for j in range(kb) for t in [twr_ref[j]] * reps])
    twi = lanes([t for j in range(kb) for t in [twi_ref[j]] * reps])
    ar = lanes([ar_ref[j] for j in range(kb)])
    ai = lanes([ai_ref[j] for j in range(kb)])
    z = jnp.concatenate([ar * twr - ai * twi, ar * twi + ai * twr], axis=0)
    x = _dot3c(f2h_ref[...], f2l_ref[...], z)
    xr, xi = x[:n2], x[n2:]
    if filter_only:
        kr_out, ki_out = rest
        for j in range(kb):
            kr_out[j] = xr[:, j * width:(j + 1) * width]
            ki_out[j] = xi[:, j * width:(j + 1) * width]
        return
    kr_ref, ki_ref, g2h_ref, g2l_ref, br_out, bi_out = rest
    kr = lanes([kr_ref[j] for j in range(kb)])
    ki = lanes([ki_ref[j] for j in range(kb)])
    y = jnp.concatenate([xr * kr - xi * ki, xr * ki + xi * kr], axis=0)
    w = _dot3c(g2h_ref[...], g2l_ref[...], y)
    wr, wi = w[:n2], w[n2:]
    br = wr * twr + wi * twi
    bi = wi * twr - wr * twi
    for j in range(kb):
        br_out[j] = br[:, j * width:(j + 1) * width]
        bi_out[j] = bi[:, j * width:(j + 1) * width]


def _fft_mid(ar, ai, consts, n2, width, spectrum=None):
    a3r = ar.reshape(FFT_N1, n2, width)
    a3i = ai.reshape(FFT_N1, n2, width)
    blk = pl.BlockSpec((FFT_KB, n2, width), lambda k: (k, 0, 0))
    twspec = pl.BlockSpec((FFT_KB, n2, LANES), lambda k: (k, 0, 0))
    cspec = pl.BlockSpec((2 * n2, 2 * n2), lambda k: (0, 0))
    in_specs = [blk, blk, twspec, twspec, cspec, cspec]
    args = [a3r, a3i, consts["twr"], consts["twi"], *consts["f2"]]
    if spectrum is not None:
        in_specs += [blk, blk, cspec, cspec]
        args += [spectrum[0], spectrum[1], *consts["g2"]]
    return pl.pallas_call(
        functools.partial(_fftmid_kernel, filter_only=spectrum is None, n2=n2),
        grid=(FFT_N1 // FFT_KB,),
        in_specs=in_specs,
        out_specs=[blk, blk],
        out_shape=[jax.ShapeDtypeStruct((FFT_N1, n2, width), F32)] * 2,
        compiler_params=_cparams(("arbitrary",)),
    )(*args)


def _fftlast_kernel(br_ref, bi_ref, gh_ref, gl_ref, u_ref, x0_ref, bias_ref, invn_ref, o_ref):
    b = jnp.concatenate([br_ref[...], bi_ref[...]], axis=0)
    y = _dot3c(gh_ref[...], gl_ref[...], b)
    o_ref[...] = ((y * invn_ref[...] + u_ref[...] * bias_ref[...]) * x0_ref[...]).astype(BF16)


def _fft_last(br, bi, gmat, uview, x0view, bias_t, invn_t, ncols):
    cb = min(FFT_COLS, ncols)
    gh, gl = gmat
    cspec = pl.BlockSpec(gh.shape, lambda j: (0, 0))
    dspec = pl.BlockSpec((FFT_N1, cb), lambda j: (0, j))
    vspec = pl.BlockSpec((1, cb), lambda j: (0, 0))
    return pl.pallas_call(
        _fftlast_kernel,
        grid=(ncols // cb,),
        in_specs=[dspec, dspec, cspec, cspec, dspec, dspec, vspec, vspec],
        out_specs=dspec,
        out_shape=jax.ShapeDtypeStruct((FFT_N1, ncols), BF16),
        compiler_params=_cparams(("arbitrary",)),
    )(br, bi, gh, gl, uview, x0view, bias_t, invn_t)


def _dft_small_consts(seq):
    n = 2 * seq
    a = -2.0 * np.pi * np.outer(np.arange(n), np.arange(n)) / n
    fr, fi = np.cos(a), np.sin(a)
    f_data = np.block([[fr[:, :seq], -fi[:, :seq]], [fi[:, :seq], fr[:, :seq]]])
    f_real = np.concatenate([fr, fi], axis=0)
    gr, gi = fr / n, -fi / n
    g = np.block([[gr[:seq], -gi[:seq]], [gi[:seq], gr[:seq]]])
    return _np_split(f_data), _np_split(f_real), _np_split(g)


def _hy_ctx_kernel(u_ref, x0_ref, taps_ref, fdh_ref, fdl_ref, frh_ref, frl_ref, gh_ref, gl_ref,
                   bias_ref, invn_ref, o_ref):
    n = taps_ref.shape[0]
    z = u_ref[...]
    a = _dot3c(fdh_ref[...], fdl_ref[...], z)
    k = _dot3c(frh_ref[...], frl_ref[...], taps_ref[...])
    ar, ai, kr, ki = a[:n], a[n:], k[:n], k[n:]
    y = jnp.concatenate([ar * kr - ai * ki, ar * ki + ai * kr], axis=0)
    conv = _dot3c(gh_ref[...], gl_ref[...], y)
    o_ref[...] = ((conv * invn_ref[...] + z * bias_ref[...]) * x0_ref[...]).astype(BF16)


def _hy_ctx(u, x0, taps, bias, invn, small):
    rows, hw = u.shape
    full = lambda a: pl.BlockSpec(a.shape, lambda i: (0,) * a.ndim)
    args = [u, x0, taps, *small[0], *small[1], *small[2], bias, invn]
    return pl.pallas_call(
        _hy_ctx_kernel,
        grid=(1,),
        in_specs=[full(a) for a in args],
        out_specs=pl.BlockSpec((rows, hw), lambda i: (0, 0)),
        out_shape=jax.ShapeDtypeStruct((rows, hw), BF16),
        compiler_params=_cparams(("arbitrary",)),
    )(*args)


def _out_proj_kernel(x_ref, ypl_ref, yhl_ref, yal_ref, ypc_ref, yhc_ref, yac_ref, w_ref, g_ref, mod_ref, *rest,
                     nlt, tpb, ncond, pw, hw, moe, ntiles):
    if moe:
        rw_ref, xo_ref, u_ref, route_ref, un_ref = rest
    else:
        wg_ref, wu_ref, wd_ref, xo_ref = rest
    t = jnp.minimum(pl.program_id(0), ntiles - 1)
    if moe:
        @pl.when(pl.program_id(0) == 0)
        def _():
            un_ref[...] = jnp.zeros_like(un_ref)

        logits = _dot3(un_ref[...], rw_ref[...])
    ci = _cond_row(t, nlt, tpb, ncond)
    is_ctx = t >= nlt
    yp = jnp.where(is_ctx, ypc_ref[...], ypl_ref[...])
    yh = jnp.where(is_ctx, yhc_ref[...], yhl_ref[...])
    ya = jnp.where(is_ctx, yac_ref[...], yal_ref[...])
    mix = _dot(yp, w_ref[0:pw, :]) + _dot(yh, w_ref[pw:pw + hw, :]) + _dot(ya, w_ref[pw + hw:, :])
    x = x_ref[...] + mod_ref[2, pl.ds(ci, 1), :] * mix
    un = _normmod(x, g_ref[...], mod_ref[3, pl.ds(ci, 1), :], mod_ref[4, pl.ds(ci, 1), :])
    if not moe:
        u = un.astype(BF16)
        ff = wg_ref.shape[1]
        fc = ff // FFN_CHUNKS
        y = jnp.zeros(x.shape, F32)
        for c in range(FFN_CHUNKS):
            gate = _dot(u, wg_ref[:, c * fc:(c + 1) * fc])
            up = _dot(u, wu_ref[:, c * fc:(c + 1) * fc])
            y = y + _dot((_silu(gate) * up).astype(BF16), wd_ref[c * fc:(c + 1) * fc, :])
        xo_ref[...] = x + mod_ref[5, pl.ds(ci, 1), :] * y
    else:
        xo_ref[...] = x
        u_ref[...] = un.astype(BF16)
        un_ref[...] = un
        lane = lax.broadcasted_iota(jnp.int32, logits.shape, 1)
        neg = jnp.float32(-jnp.inf)
        lg = jnp.where(lane < N_EXPERTS, logits, neg)
        t1 = jnp.max(lg, axis=-1, keepdims=True)
        i1 = jnp.min(jnp.where(lg == t1, lane, LANES), axis=-1, keepdims=True)
        lg2 = jnp.where(lane == i1, neg, lg)
        t2 = jnp.max(lg2, axis=-1, keepdims=True)
        i2 = jnp.min(jnp.where(lg2 == t2, lane, LANES), axis=-1, keepdims=True)
        e2 = jnp.exp(t2 - t1)
        g1 = 1.0 / (1.0 + e2)
        g2 = e2 / (1.0 + e2)
        route_ref[...] = jnp.where(lane == 0, i1.astype(F32), jnp.where(lane == 1, i2.astype(F32),
                                   jnp.where(lane == 2, g1, jnp.where(lane == 3, g2, 0.0))))


def _out_proj(x, lat, ctx, w, g, mod, rw, ffn_w, *, nlt, tpb, ncond, out_rows):
    n, d = x.shape
    pw, hw, aw = (a.shape[1] for a in lat)
    tm = ROW_TILE
    moe = rw is not None
    nt = (n if moe else out_rows) // tm
    cur = lambda t: jnp.minimum(t, nt - 1)
    row = lambda width: pl.BlockSpec((tm, width), lambda t: (cur(t), 0))
    lrow = lambda width: pl.BlockSpec((tm, width), lambda t: (jnp.minimum(t, nlt - 1), 0))
    crow = lambda width: pl.BlockSpec((tm, width), lambda t: (jnp.maximum(cur(t) - nlt, 0), 0))
    in_specs = [row(d), lrow(pw), lrow(hw), lrow(aw), crow(pw), crow(hw), crow(aw),
                pl.BlockSpec(w.shape, lambda t: (0, 0)),
                pl.BlockSpec((1, d), lambda t: (0, 0)),
                pl.BlockSpec(mod.shape, lambda t: (0, 0, 0))]
    args = [x, *lat, *ctx, w, g, mod]
    scratch = []
    if not moe:
        const = lambda a: pl.BlockSpec(a.shape, lambda t: (0, 0), pipeline_mode=pl.Buffered(1))
        in_specs += [const(a) for a in ffn_w]
        args += list(ffn_w)
        out_specs = [row(d)]
        out_shape = [jax.ShapeDtypeStruct((out_rows, d), F32)]
    else:
        out_specs = [row(d), row(d)]
        out_shape = [jax.ShapeDtypeStruct((n, d), F32), jax.ShapeDtypeStruct((n, d), BF16)]
        in_specs.append(pl.BlockSpec(rw.shape, lambda t: (0, 0)))
        args.append(rw)
        out_specs.append(pl.BlockSpec((tm, LANES), lambda t: (jnp.maximum(t - 1, 0), 0)))
        out_shape.append(jax.ShapeDtypeStruct((n, LANES), F32))
        scratch.append(pltpu.VMEM((tm, d), F32))
    kern = functools.partial(_out_proj_kernel, nlt=nlt, tpb=tpb, ncond=ncond, pw=pw, hw=hw, moe=moe, ntiles=nt)
    return pl.pallas_call(
        kern, grid=(nt + 1 if moe else nt,), in_specs=in_specs, out_specs=out_specs, out_shape=out_shape,
        scratch_shapes=scratch,
        compiler_params=_cparams(("arbitrary",)),
    )(*args)


def _moe_kernel(te_ref, tv_ref, u_ref, wg_ref, wu_ref, wd_ref, *rest):
    o_ref = rest[-1]
    @pl.when(tv_ref[pl.program_id(0)] > 0)
    def _():
        u = u_ref[...]
        ff = wg_ref.shape[3]
        fc = ff // MOE_CHUNKS
        y = jnp.zeros(o_ref.shape, F32)
        for c in range(MOE_CHUNKS):
            sl = slice(c * fc, (c + 1) * fc)
            h = (_silu(_dot(u, wg_ref[0, 0, :, sl])) * _dot(u, wu_ref[0, 0, :, sl])).astype(BF16)
            y = y + _dot(h, wd_ref[0, 0, sl, :])
        o_ref[...] = y.astype(BF16)


def _moe_experts(ug, tile_expert, tile_valid, wg, wu, wd, j, *, tile0, total_rows, prev=None):
    p, d = ug.shape
    ff = wg.shape[3]
    tm = MOE_TM
    one = pl.Buffered(1)
    in_specs = [
        pl.BlockSpec((tm, d), lambda t, te, tv: (t, 0)),
        pl.BlockSpec((1, 1, d, ff), lambda t, te, tv: (j, te[t], 0, 0), pipeline_mode=one),
        pl.BlockSpec((1, 1, d, ff), lambda t, te, tv: (j, te[t], 0, 0), pipeline_mode=one),
        pl.BlockSpec((1, 1, ff, d), lambda t, te, tv: (j, te[t], 0, 0), pipeline_mode=one),
    ]
    args = [tile_expert, tile_valid, ug, wg, wu, wd]
    aliases = {}
    if prev is not None:
        in_specs.append(pl.BlockSpec(memory_space=pl.ANY))
        args.append(prev)
        aliases = {len(args) - 1: 0}
    grid_spec = pltpu.PrefetchScalarGridSpec(
        num_scalar_prefetch=2,
        grid=(p // tm,),
        in_specs=in_specs,
        out_specs=pl.BlockSpec((tm, d), lambda t, te, tv: (t + tile0, 0)),
    )
    return pl.pallas_call(
        _moe_kernel, grid_spec=grid_spec,
        out_shape=jax.ShapeDtypeStruct((total_rows, d), BF16),
        input_output_aliases=aliases,
        compiler_params=_cparams(("arbitrary",)),
    )(*args)


def _moe_combine_kernel(x_ref, ya_ref, yb_ref, route_ref, mod_ref, o_ref, *, nlt, tpb, ncond):
    ci = _cond_row(pl.program_id(0), nlt, tpb, ncond)
    r = route_ref[...]
    lane = lax.broadcasted_iota(jnp.int32, r.shape, 1)
    g1 = jnp.sum(jnp.where(lane == 2, r, 0.0), axis=-1, keepdims=True)
    g2 = jnp.sum(jnp.where(lane == 3, r, 0.0), axis=-1, keepdims=True)
    y = g1 * ya_ref[...].astype(F32) + g2 * yb_ref[...].astype(F32)
    o_ref[...] = x_ref[...] + mod_ref[5, pl.ds(ci, 1), :] * y


def _moe_combine(x, ya, yb, route, mod, *, nlt, tpb, ncond, out_rows):
    d = x.shape[1]
    tm = ROW_TILE
    row = lambda width: pl.BlockSpec((tm, width), lambda t: (t, 0))
    kern = functools.partial(_moe_combine_kernel, nlt=nlt, tpb=tpb, ncond=ncond)
    return pl.pallas_call(
        kern,
        grid=(out_rows // tm,),
        in_specs=[row(d), row(d), row(d), row(LANES), pl.BlockSpec(mod.shape, lambda t: (0, 0, 0))],
        out_specs=row(d),
        out_shape=jax.ShapeDtypeStruct((out_rows, d), F32),
        compiler_params=_cparams(("arbitrary",)),
    )(x, ya, yb, route, mod)


def _moe_layer(x, u, route, mod, wg, wu, wd, j, *, nlt, tpb, ncond, out_rows):
    n, d = x.shape
    tm = MOE_TM
    experts = jnp.concatenate([route[:, 0], route[:, 1]]).astype(jnp.int32)
    onehot = (experts[:, None] == jnp.arange(N_EXPERTS, dtype=jnp.int32)[None, :]).astype(jnp.int32)
    rank = jnp.sum(onehot * (jnp.cumsum(onehot, axis=0) - 1), axis=1)
    counts = jnp.sum(onehot, axis=0)
    padded = ((counts + tm - 1) // tm) * tm
    ends = jnp.cumsum(padded)
    starts = ends - padded
    dest = starts[experts] + rank
    p = 2 * n + N_EXPERTS * tm
    tokens = jnp.concatenate([jnp.arange(n, dtype=jnp.int32)] * 2)
    inb = dict(mode="promise_in_bounds")
    src = jnp.zeros((p,), jnp.int32).at[dest].set(tokens, unique_indices=True, **inb)
    tile_start = jnp.arange(p // tm, dtype=jnp.int32) * tm
    tile_expert = jnp.minimum(jnp.searchsorted(ends, tile_start, side="right"), N_EXPERTS - 1).astype(jnp.int32)
    tile_valid = (tile_start < ends[-1]).astype(jnp.int32)
    nt = p // tm
    cuts = [nt * k // MOE_PARTS for k in range(MOE_PARTS + 1)]
    ys = None
    for lo, hi in zip(cuts[:-1], cuts[1:]):
        ug = u.at[src[lo * tm:hi * tm]].get(**inb)
        ys = _moe_experts(ug, tile_expert[lo:hi], tile_valid[lo:hi], wg, wu, wd, j,
                          tile0=lo, total_rows=p, prev=ys)
    ya = ys.at[dest[:out_rows]].get(**inb)
    yb = ys.at[dest[n:n + out_rows]].get(**inb)
    return _moe_combine(x, ya, yb, route, mod, nlt=nlt, tpb=tpb, ncond=ncond, out_rows=out_rows)


def _rope_tables(seq, n_ctx_rows, batch):
    rows = seq // GRID_W
    row = jnp.repeat(jnp.arange(rows, dtype=F32), GRID_W)
    col = jnp.broadcast_to(jnp.arange(GRID_W, dtype=F32), (rows, GRID_W)).reshape(-1)
    inv_freq = jnp.power(ROPE_THETA, -jnp.arange(ROPE_FREQS, dtype=F32) / ROPE_FREQS)
    ar = row[:, None] * inv_freq
    ac = col[:, None] * inv_freq
    cos = jnp.concatenate([jnp.cos(ar), jnp.cos(ar), jnp.cos(ac), jnp.cos(ac)], axis=-1)
    sin = jnp.concatenate([-jnp.sin(ar), jnp.sin(ar), -jnp.sin(ac), jnp.sin(ac)], axis=-1)
    cos = jnp.tile(jnp.concatenate([cos, cos], axis=-1), (batch, 1))
    sin = jnp.tile(jnp.concatenate([sin, sin], axis=-1), (batch, 1))
    cos = jnp.concatenate([cos, jnp.ones((n_ctx_rows, 2 * HEAD_DIM), F32)], axis=0)
    sin = jnp.concatenate([sin, jnp.zeros((n_ctx_rows, 2 * HEAD_DIM), F32)], axis=0)
    return cos, sin


def kernel(x, c, ctx, c_ctx, mod_w, mod_b, norm1_g, norm2_g, w_in, w_out, pool_lin, pool_scale, hy_short_w, hy_short_b, hy_f_w1, hy_f_b1, hy_f_freq1, hy_f_w2, hy_f_b2, hy_f_freq2, hy_f_w3, hy_bias, qk_norm_g, diff_lambda, subln_g, ffn_w_gate, ffn_w_up, ffn_w_down, router_w, moe_w_gate, moe_w_up, moe_w_down):
    batch, seq, d = x.shape
    ctx_len = ctx.shape[1]
    depth = mod_w.shape[0]
    pw = pool_scale.shape[1]
    hw = hy_bias.shape[1]
    nl, nc = batch * seq, batch * ctx_len
    n = nl + nc
    tm = ROW_TILE
    assert seq % tm == 0 and nc % tm == 0 and seq % (FFT_N1 // 2) == 0 and seq % GRID_W == 0
    assert d == HEADS * 2 * V_DIM and pw == hw and batch == 2
    n2 = 2 * seq // FFT_N1
    ncols = n2 * hw
    assert (n * hw) % ncols == 0
    tiles = dict(nlt=nl // tm, tpb=seq // tm, ncond=batch + 1)

    xs = jnp.concatenate([x.reshape(nl, d), ctx.reshape(nc, d)], axis=0)
    cond8 = jnp.concatenate([c, c_ctx[None, :], jnp.zeros((8 - batch - 1, d), F32)], axis=0)
    mods = _modvec(cond8, mod_w, mod_b)
    cos_t, sin_t = _rope_tables(seq, nc, batch)
    consts = _dft_consts(n2)
    small = _dft_small_consts(ctx_len)
    eye = jnp.eye(len(POOL_WINDOWS), dtype=F32)
    n_moe = moe_w_gate.shape[0]
    assert n_moe <= depth
    moe_w2d = [w.reshape(-1, w.shape[-1]) for w in (moe_w_gate, moe_w_up, moe_w_down)]
    moe_bf16 = {}

    for i in range(depth):
        last = i == depth - 1
        lam_init = 0.8 - 0.6 * math.exp(-0.3 * i)
        mod = mods[i]
        gqk = jnp.concatenate([qk_norm_g[i], qk_norm_g[i]], axis=-1)
        zp, zh, qt, k12, vt = _in_proj(xs, norm1_g[i][None, :], mod, w_in[i].astype(BF16), cos_t, sin_t, gqk,
                                       pw=pw, hw=hw, **tiles)

        g_b = jnp.broadcast_to(subln_g[i][:, None], (V_DIM, ATT_TQ))
        att_kw = dict(batch=batch, seq=seq, ctx_len=ctx_len, lam_init=lam_init)

        cast_j = i if i < n_moe else None

        def attend(direct, lam_p=diff_lambda[i], qt=qt, k12=k12, vt=vt, g_b=g_b, att_kw=att_kw, cast_j=cast_j):
            ctx_out = _attention(lam_p, qt, k12, vt, g_b, latent=False, direct=direct, **att_kw)
            if cast_j is None:
                return (_attention(lam_p, qt, k12, vt, g_b, latent=True, direct=direct, **att_kw), ctx_out)
            if direct:
                lat_out, *wb = _attention(lam_p, qt, k12, vt, g_b, latent=True, direct=True,
                                          cast=[(w, cast_j, n_moe) for w in moe_w2d], **att_kw)
            else:
                lat_out = _attention(lam_p, qt, k12, vt, g_b, latent=True, direct=False, **att_kw)
                wb = _cast_bf16(*[w[cast_j * (w.shape[0] // n_moe):(cast_j + 1) * (w.shape[0] // n_moe)]
                                  for w in moe_w2d])
            return (lat_out, ctx_out, *wb)

        bound = (HEAD_DIM * QSCALE) * jnp.max(jnp.abs(qk_norm_g[i][0])) * jnp.max(jnp.abs(qk_norm_g[i][1]))
        ya_l, ya_c, *wb = lax.cond(bound * 1.02 < ATT_DIRECT_MAX, lambda: attend(True), lambda: attend(False))
        if cast_j is not None:
            moe_bf16[cast_j] = tuple(w.reshape((1,) + src.shape[1:])
                                     for w, src in zip(wb, (moe_w_gate, moe_w_up, moe_w_down)))

        lin_bd = (eye[:, None, :, None] * pool_lin[i][:, :, None, :]).reshape(pw, pw).astype(BF16)
        pscale = pool_scale[i][None, :]
        yp_l = _pool(zp, lin_bd, pscale, row0=0, batch=batch, seq=seq, out_rows=nl, out_row0=0)
        yp_c = _pool(zp, lin_bd, pscale, row0=nl, batch=batch, seq=ctx_len, out_rows=nc, out_row0=0)

        sw, sb = hy_short_w[i], hy_short_b[i][None, :]
        u_l, x0_l = _hy_pre(zh, sw, sb, row0=0, batch=batch, seq=seq, out_rows=n, out_row0=0)
        u_c, x0_c = _hy_pre(zh, sw, sb, row0=nl, batch=batch, seq=ctx_len, out_rows=nc, out_row0=0)
        filt = (hy_f_w1[i], hy_f_b1[i], hy_f_freq1[i], hy_f_w2[i], hy_f_b2[i], hy_f_freq2[i], hy_f_w3[i])
        bias = hy_bias[i][None, :]
        taps, asum = _hy_filter(seq, *filt, hw)
        kr1, ki1 = _fft_first(taps.reshape(FFT_N1, ncols), consts["f1_real"], ncols)
        spectrum = _fft_mid(kr1, ki1, consts, n2, hw)
        ar, ai = _fft_first(u_l.reshape(-1, ncols), consts["f1_data"], ncols)
        br, bi = _fft_mid(ar, ai, consts, n2, hw, spectrum=spectrum)
        reps = min(FFT_COLS, ncols) // hw
        yh_lat = _fft_last(br.reshape(FFT_N1, ncols), bi.reshape(FFT_N1, ncols), consts["g1"],
                           u_l.reshape(-1, ncols), x0_l.reshape(-1, ncols),
                           jnp.tile(bias, (1, reps)), jnp.tile(1.0 / asum, (1, reps)), ncols)
        taps_c, asum_c = _hy_filter(ctx_len, *filt, hw)
        yh_ctx = _hy_ctx(u_c, x0_c, taps_c, bias, 1.0 / asum_c, small)

        j = i // 2
        moe = i % 2 == 1
        rw = ffn_w = None
        if moe:
            rw = jnp.concatenate([router_w[j], jnp.zeros((d, LANES - N_EXPERTS), F32)], axis=1)
        else:
            ffn_w = (ffn_w_gate[j].astype(BF16), ffn_w_up[j].astype(BF16), ffn_w_down[j].astype(BF16))
        out_rows = nl if last else n
        outs = _out_proj(xs, (yp_l, yh_lat.reshape(nl, hw), ya_l), (yp_c, yh_ctx, ya_c), w_out[i].astype(BF16),
                         norm2_g[i][None, :], mod, rw, ffn_w, out_rows=out_rows, **tiles)
        if moe:
            xs, u, route = outs
            xs = _moe_layer(xs, u, route, mod, *moe_bf16[j], 0, out_rows=out_rows, **tiles)
        else:
            xs, = outs
    return xs[:nl].reshape(batch, seq, d)
```

```python
import functools
import math

import numpy as np
import jax
import jax.numpy as jnp
from jax import lax
from jax.experimental import pallas as pl
from jax.experimental.pallas import tpu as pltpu

F32 = jnp.float32
BF16 = jnp.bfloat16
EPS = 1e-6

GRID_W = 64
POOL_WINDOWS = (2, 4, 8, 16)
HEADS = 4
HEAD_DIM = 64
V_DIM = 128
ROPE_FREQS = 16
ROPE_THETA = 10000.0
HY_BANDS = 16
HY_TARGET = 1e-2
HY_FAST = 0.3
HY_SLOW = 1.5
N_EXPERTS = 8

LANES = 128
ROW_TILE = 512
HALO = 32
FFT_N1 = 128
SEQ_TILE = 4096
FFT_COLS = 4096
FFT_KB = 4
ATT_TQ = 1024
ATT_TK = 512
ATT_UNROLL = 16
ATT_GROUP = 8
QSCALE = (HEAD_DIM ** -0.5) * math.log2(math.e)
ATT_DIRECT_MAX = 100.0
FFN_CHUNKS = 2
CAST_STEPS = 64
MOE_TM = 512
MOE_PARTS = 4
MOE_CHUNKS = 2
VMEM_LIMIT = 56 * 1024 * 1024


def _cparams(sem):
    return pltpu.CompilerParams(dimension_semantics=sem, vmem_limit_bytes=VMEM_LIMIT)


def _dot(a, b):
    return jnp.dot(a, b, preferred_element_type=F32)


def _split(a):
    hi = a.astype(BF16)
    lo = (a - hi.astype(F32)).astype(BF16)
    return hi, lo


def _dot3(a, b):
    ah, al = _split(a)
    bh, bl = _split(b)
    return _dot(ah, bh) + _dot(ah, bl) + _dot(al, bh)


def _dot3c(ch, cl, d):
    dh, dl = _split(d)
    return _dot(ch, dh) + _dot(ch, dl) + _dot(cl, dh)


def _silu(x):
    return x / (1.0 + jnp.exp(-x))


def _normmod(x, g, shift, scale):
    ms = jnp.mean(x * x, axis=-1, keepdims=True)
    return (x * lax.rsqrt(ms + EPS) * g) * (1.0 + scale) + shift


def _cond_row(t, n_lat_tiles, tiles_per_batch, n_cond):
    return jnp.where(t >= n_lat_tiles, n_cond - 1, t // tiles_per_batch)


def _cast_kernel(*refs):
    half = len(refs) // 2
    for w_ref, o_ref in zip(refs[:half], refs[half:]):
        o_ref[...] = w_ref[...].astype(BF16)


def _cast_bf16(*ws):
    flat = [w.reshape(-1, w.shape[-1]) for w in ws]
    steps = CAST_STEPS
    specs = [pl.BlockSpec((w.shape[0] // steps, w.shape[1]), lambda t: (t, 0)) for w in flat]
    outs = pl.pallas_call(
        _cast_kernel,
        grid=(steps,),
        in_specs=specs,
        out_specs=specs,
        out_shape=[jax.ShapeDtypeStruct(w.shape, BF16) for w in flat],
        compiler_params=_cparams(("arbitrary",)),
    )(*flat)
    return [o.reshape(w.shape) for o, w in zip(outs, ws)]


def _modvec_kernel(c_ref, w_ref, b_ref, o_ref):
    s = _silu(c_ref[...])
    o_ref[0, 0] = _dot3(s, w_ref[0]) + b_ref[0]


def _modvec(cond8, mod_w, mod_b):
    depth, d, six_d = mod_w.shape
    nchunk = six_d // d
    return pl.pallas_call(
        _modvec_kernel,
        grid=(depth, nchunk),
        in_specs=[
            pl.BlockSpec((8, d), lambda i, j: (0, 0)),
            pl.BlockSpec((1, d, d), lambda i, j: (i, 0, j)),
            pl.BlockSpec((1, 1, d), lambda i, j: (i, 0, j)),
        ],
        out_specs=pl.BlockSpec((1, 1, 8, d), lambda i, j: (i, j, 0, 0)),
        out_shape=jax.ShapeDtypeStruct((depth, nchunk, 8, d), F32),
        compiler_params=_cparams(("arbitrary", "arbitrary")),
    )(cond8, mod_w, mod_b.reshape(depth, 1, six_d))


def _in_proj_kernel(x_ref, g_ref, mod_ref, w_ref, cos_ref, sin_ref, gqk_ref,
                    zp_ref, zh_ref, qt_ref, k_ref, vt_ref, zatt_ref, *, nlt, tpb, ncond, pw, hw, ntiles):
    t = pl.program_id(0)

    @pl.when(t == 0)
    def _():
        zatt_ref[...] = jnp.zeros_like(zatt_ref)

    tm = x_ref.shape[0]
    att = pw + 3 * hw
    lane = lax.broadcasted_iota(jnp.int32, (tm, 2 * HEAD_DIM), 1)
    first = lane < HEAD_DIM
    apart = (lane % (2 * ROPE_FREQS)) < ROPE_FREQS
    cos = cos_ref[...]
    sin = sin_ref[...]
    qk_w = HEADS * 2 * HEAD_DIM

    def norm_rope(v, g):
        v2 = v * v
        s_all = jnp.sum(v2, axis=-1, keepdims=True)
        s_first = jnp.sum(jnp.where(first, v2, 0.0), axis=-1, keepdims=True)
        ms = jnp.where(first, s_first, s_all - s_first) * (1.0 / HEAD_DIM)
        vn = v * lax.rsqrt(ms + EPS) * g
        swapped = jnp.where(apart, pltpu.roll(vn, 2 * HEAD_DIM - ROPE_FREQS, 1), pltpu.roll(vn, ROPE_FREQS, 1))
        return vn * cos + swapped * sin

    for h in range(HEADS):
        lo = h * 2 * HEAD_DIM
        q = norm_rope(zatt_ref[:, lo:lo + 2 * HEAD_DIM], gqk_ref[0:1, :])
        qt_ref[h] = (q * QSCALE).T.astype(BF16)
        k = norm_rope(zatt_ref[:, qk_w + lo:qk_w + lo + 2 * HEAD_DIM], gqk_ref[1:2, :])
        k_ref[h] = k.astype(BF16)
        vlo = 2 * qk_w + h * V_DIM
        vt_ref[h] = zatt_ref[:, vlo:vlo + V_DIM].T.astype(BF16)

    ci = _cond_row(jnp.minimum(t, ntiles - 1), nlt, tpb, ncond)
    shift = mod_ref[0, pl.ds(ci, 1), :]
    scale = mod_ref[1, pl.ds(ci, 1), :]
    y = _normmod(x_ref[...], g_ref[...], shift, scale).astype(BF16)
    z = _dot(y, w_ref[...])
    zp_ref[...] = z[:, :pw].astype(BF16)
    zh_ref[...] = z[:, pw:att].astype(BF16)
    zatt_ref[...] = z[:, att:]


def _in_proj(x, g, mod, w, cos_t, sin_t, gqk, *, nlt, tpb, ncond, pw, hw):
    n, d = x.shape
    wid = w.shape[1]
    tm = ROW_TILE
    nt = n // tm
    kern = functools.partial(_in_proj_kernel, nlt=nlt, tpb=tpb, ncond=ncond, pw=pw, hw=hw, ntiles=nt)
    cur = lambda t: jnp.minimum(t, nt - 1)
    prv = lambda t: jnp.maximum(t - 1, 0)
    return pl.pallas_call(
        kern,
        grid=(nt + 1,),
        in_specs=[
            pl.BlockSpec((tm, d), lambda t: (cur(t), 0)),
            pl.BlockSpec((1, d), lambda t: (0, 0)),
            pl.BlockSpec(mod.shape, lambda t: (0, 0, 0)),
            pl.BlockSpec((d, wid), lambda t: (0, 0)),
            pl.BlockSpec((tm, 2 * HEAD_DIM), lambda t: (prv(t), 0)),
            pl.BlockSpec((tm, 2 * HEAD_DIM), lambda t: (prv(t), 0)),
            pl.BlockSpec((2, 2 * HEAD_DIM), lambda t: (0, 0)),
        ],
        out_specs=[
            pl.BlockSpec((tm, pw), lambda t: (cur(t), 0)),
            pl.BlockSpec((tm, 3 * hw), lambda t: (cur(t), 0)),
            pl.BlockSpec((HEADS, 2 * HEAD_DIM, tm), lambda t: (0, 0, prv(t))),
            pl.BlockSpec((HEADS, tm, 2 * HEAD_DIM), lambda t: (0, prv(t), 0)),
            pl.BlockSpec((HEADS, V_DIM, tm), lambda t: (0, 0, prv(t))),
        ],
        scratch_shapes=[pltpu.VMEM((tm, wid - pw - 3 * hw), F32)],
        out_shape=[
            jax.ShapeDtypeStruct((n, pw), BF16),
            jax.ShapeDtypeStruct((n, 3 * hw), BF16),
            jax.ShapeDtypeStruct((HEADS, 2 * HEAD_DIM, n), BF16),
            jax.ShapeDtypeStruct((HEADS, n, 2 * HEAD_DIM), BF16),
            jax.ShapeDtypeStruct((HEADS, V_DIM, n), BF16),
        ],
        compiler_params=_cparams(("arbitrary",)),
    )(x, g, mod, w, cos_t, sin_t, gqk)


def _attn_kernel(lam_ref, qt_ref, kc_ref, vtc_ref, *rest, tk, n_lat_chunks, lam_init):
    if n_lat_chunks:
        kl_ref, vtl_ref, g_ref, o_ref, m_ref, l_ref, acc_ref = rest
    else:
        g_ref, o_ref, m_ref, l_ref, acc_ref = rest
    qt = qt_ref[0]
    row = lax.broadcasted_iota(jnp.int32, qt.shape, 0)
    zero = jnp.zeros_like(qt)
    qmaps = (jnp.where(row < HEAD_DIM, qt, zero), jnp.where(row >= HEAD_DIM, qt, zero))

    def process(k_tile, vt_tile, first):
        for mi in range(2):
            s = _dot(k_tile, qmaps[mi])
            smax = jnp.max(s, axis=0, keepdims=True)
            if first:
                m_new = smax
            else:
                m_old = m_ref[mi]
                m_new = jnp.maximum(m_old, smax)
            p = jnp.exp2(s - m_new)
            psum = jnp.sum(p, axis=0, keepdims=True)
            pv = _dot(vt_tile, p.astype(BF16))
            if first:
                l_ref[mi] = psum
                acc_ref[mi] = pv
            else:
                alpha = jnp.exp2(m_old - m_new)
                l_ref[mi] = alpha * l_ref[mi] + psum
                acc_ref[mi] = alpha * acc_ref[mi] + pv
            m_ref[mi] = m_new

    process(kc_ref[0], vtc_ref[0], True)
    if n_lat_chunks:
        def body(j, carry):
            off = pl.multiple_of(j * tk, tk)
            process(kl_ref[0, pl.ds(off, tk), :], vtl_ref[0, :, pl.ds(off, tk)], False)
            return carry
        lax.fori_loop(0, n_lat_chunks, body, 0)

    lv = lam_ref[...]
    lam = (jnp.exp(jnp.sum(lv[0:1] * lv[1:2], axis=-1, keepdims=True))
           - jnp.exp(jnp.sum(lv[2:3] * lv[3:4], axis=-1, keepdims=True)) + lam_init)
    o = acc_ref[0] * (1.0 / l_ref[0]) - lam * (acc_ref[1] * (1.0 / l_ref[1]))
    ms = jnp.mean(o * o, axis=0, keepdims=True)
    y = o * lax.rsqrt(ms + EPS) * g_ref[...] * (1.0 - lam_init)
    o_ref[...] = y.T.astype(BF16)


def _attn_direct_kernel(lam_ref, qt_ref, kc_ref, vtc_ref, *rest, tk, n_lat_chunks, lam_init, n_cast=0):
    if n_lat_chunks:
        kl_ref, vtl_ref, g_ref = rest[:3]
        rest = rest[3:]
    else:
        g_ref = rest[0]
        rest = rest[1:]
    cast_in, (o_ref, *cast_out), (p_ref, l_ref, acc_ref) = rest[:n_cast], rest[n_cast:2 * n_cast + 1], rest[2 * n_cast + 1:]
    for w_ref, c_ref in zip(cast_in, cast_out):
        c_ref[...] = w_ref[...].astype(BF16)
    qt = qt_ref[0]
    tq = qt.shape[1]
    row = lax.broadcasted_iota(jnp.int32, qt.shape, 0)
    zero = jnp.zeros_like(qt)
    qmaps = (jnp.where(row < HEAD_DIM, qt, zero), jnp.where(row >= HEAD_DIM, qt, zero))
    grp = ATT_GROUP

    def keys(first_piece, count):
        if isinstance(first_piece, int):
            return pl.ds((first_piece - 1) * tk, count * tk)
        return pl.ds(pl.multiple_of((first_piece - 1) * tk, tk), count * tk)

    def stage_ab(k_tile, pset, r):
        for mi in range(2):
            p = jnp.exp2(_dot(k_tile, qmaps[mi]))
            l_ref[mi] = l_ref[mi] + jnp.sum(p.reshape(tk // 8, 8, tq), axis=0)
            p_ref[pset, mi, r * tk:(r + 1) * tk, :] = p.astype(BF16)

    def stage_c(vt_tile, pset, lo, count):
        for mi in range(2):
            acc_ref[mi] = acc_ref[mi] + _dot(vt_tile, p_ref[pset, mi, lo * tk:(lo + count) * tk, :])

    def group_ab(g, pset):
        k_tile = kl_ref[0, keys(grp * g, grp), :]
        for mi in range(2):
            p = jnp.exp2(_dot(k_tile, qmaps[mi]))
            l_ref[mi] = l_ref[mi] + jnp.sum(p.reshape(grp * tk // 8, 8, tq), axis=0)
            p_ref[pset, mi] = p.astype(BF16)

    def group_c(g, pset):
        stage_c(vtl_ref[0, :, keys(grp * g, grp)], pset, 0, grp)

    l_ref[...] = jnp.zeros_like(l_ref)
    acc_ref[...] = jnp.zeros_like(acc_ref)
    stage_ab(kc_ref[0], 0, 0)
    if n_lat_chunks == 0:
        stage_c(vtc_ref[0], 0, 0, 1)
    else:
        n_groups = n_lat_chunks // grp
        per_step = ATT_UNROLL // grp
        for r in range(1, grp):
            stage_ab(kl_ref[0, keys(r, 1), :], 0, r)
        group_ab(1, 1)
        stage_c(vtc_ref[0], 0, 0, 1)
        stage_c(vtl_ref[0, :, keys(1, grp - 1)], 0, 1, grp - 1)
        for g in range(2, per_step):
            group_ab(g, g % 2)
            group_c(g - 1, (g - 1) % 2)

        def body(jj, carry):
            g0 = per_step * (jj + 1)
            for q in range(per_step):
                group_ab(g0 + q, q % 2)
                group_c(g0 + q - 1, (q + 1) % 2)
            return carry

        lax.fori_loop(0, n_groups // per_step - 1, body, 0)
        stage_ab(kl_ref[0, keys(n_lat_chunks, 1), :], 0, 0)
        group_c(n_groups - 1, 1)
        stage_c(vtl_ref[0, :, keys(n_lat_chunks, 1)], 0, 0, 1)

    lv = lam_ref[...]
    lam = (jnp.exp(jnp.sum(lv[0:1] * lv[1:2], axis=-1, keepdims=True))
           - jnp.exp(jnp.sum(lv[2:3] * lv[3:4], axis=-1, keepdims=True)) + lam_init)
    l1 = jnp.sum(l_ref[0], axis=0, keepdims=True)
    l2 = jnp.sum(l_ref[1], axis=0, keepdims=True)
    o = acc_ref[0] * (1.0 / l1) - lam * (acc_ref[1] * (1.0 / l2))
    ms = jnp.mean(o * o, axis=0, keepdims=True)
    y = o * lax.rsqrt(ms + EPS) * g_ref[...] * (1.0 - lam_init)
    o_ref[...] = y.T.astype(BF16)


def _attention(lam_p, qt, k12, vt, g_b, *, batch, seq, ctx_len, latent, lam_init, direct, cast=()):
    n = k12.shape[1]
    nl = batch * seq
    tq = ATT_TQ if latent else ctx_len
    nq = seq // tq if latent else 1
    tk = ctx_len if direct else ATT_TK
    assert seq % ((ATT_UNROLL if direct else 1) * tk) == 0
    qbase = 0 if latent else nl // tq

    def qrow(b, iq):
        return (b * nq + iq) if latent else (qbase + b)

    in_specs = [
        pl.BlockSpec(lam_p.shape, lambda b, h, iq: (0, 0)),
        pl.BlockSpec((1, 2 * HEAD_DIM, tq), lambda b, h, iq: (h, 0, qrow(b, iq))),
        pl.BlockSpec((1, ctx_len, 2 * HEAD_DIM), lambda b, h, iq: (h, nl // ctx_len + b, 0)),
        pl.BlockSpec((1, V_DIM, ctx_len), lambda b, h, iq: (h, 0, nl // ctx_len + b)),
    ]
    args = [lam_p, qt, k12, vt]
    if latent:
        in_specs += [
            pl.BlockSpec((1, seq, 2 * HEAD_DIM), lambda b, h, iq: (h, b, 0)),
            pl.BlockSpec((1, V_DIM, seq), lambda b, h, iq: (h, 0, b)),
        ]
        args += [k12, vt]
    in_specs.append(pl.BlockSpec((V_DIM, tq), lambda b, h, iq: (0, 0)))
    args.append(g_b[:, :tq])
    nrows = nl if latent else batch * ctx_len
    body = _attn_direct_kernel if direct else _attn_kernel
    kw = dict(tk=tk, n_lat_chunks=(seq // tk if latent else 0), lam_init=lam_init)
    out_specs = [pl.BlockSpec((tq, V_DIM), lambda b, h, iq: (b * nq + iq, h))]
    out_shape = [jax.ShapeDtypeStruct((nrows, HEADS * V_DIM), BF16)]
    if cast:
        assert direct
        kw["n_cast"] = len(cast)
        steps = batch * HEADS * nq
        for w, slab, parts in cast:
            rows = w.shape[0] // parts // steps
            in_specs.append(pl.BlockSpec(
                (rows, w.shape[1]), lambda b, h, iq, slab=slab: (slab * steps + (b * HEADS + h) * nq + iq, 0)))
            args.append(w)
            out_specs.append(pl.BlockSpec((rows, w.shape[1]), lambda b, h, iq: ((b * HEADS + h) * nq + iq, 0)))
            out_shape.append(jax.ShapeDtypeStruct((w.shape[0] // parts, w.shape[1]), BF16))
    kern = functools.partial(body, **kw)
    if direct:
        scratch = [pltpu.VMEM((2, 2, ATT_GROUP * tk, tq), BF16),
                   pltpu.VMEM((2, 8, tq), F32), pltpu.VMEM((2, V_DIM, tq), F32)]
    else:
        scratch = [pltpu.VMEM((2, 1, tq), F32), pltpu.VMEM((2, 1, tq), F32), pltpu.VMEM((2, V_DIM, tq), F32)]
    outs = pl.pallas_call(
        kern,
        grid=(batch, HEADS, nq),
        in_specs=in_specs,
        out_specs=out_specs,
        out_shape=out_shape,
        scratch_shapes=scratch,
        compiler_params=_cparams(("arbitrary", "arbitrary", "arbitrary")),
    )(*args)
    return outs if cast else outs[0]


def _halo_specs(width, *, row0, seq, tile):
    hb = tile // HALO

    def cur(b, i):
        return ((row0 + b * seq) // tile + i, 0)

    def prev(b, i):
        first = (row0 + b * seq) // HALO
        return (jnp.maximum(first + i * hb - 1, first), 0)

    def nxt(b, i):
        first = (row0 + b * seq) // HALO
        return (jnp.minimum(first + (i + 1) * hb, first + seq // HALO - 1), 0)

    return [pl.BlockSpec((HALO, width), prev), pl.BlockSpec((tile, width), cur), pl.BlockSpec((HALO, width), nxt)]


def _fill_ext(ext_ref, prev_ref, cur_ref, next_ref, *, seq, tile):
    pos0 = pl.program_id(1) * tile
    width = cur_ref.shape[1]
    hpos = lax.broadcasted_iota(jnp.int32, (HALO, width), 0)
    ext_ref[0:HALO, :] = jnp.where(pos0 - HALO + hpos >= 0, prev_ref[...].astype(F32), 0.0)
    ext_ref[HALO:HALO + tile, :] = cur_ref[...].astype(F32)
    ext_ref[HALO + tile:, :] = jnp.where(pos0 + tile + hpos < seq, next_ref[...].astype(F32), 0.0)


def _pool_kernel(prev_ref, cur_ref, next_ref, lin_ref, scale_ref, o_ref, ext_ref, s_ref, *, seq, tile):
    _fill_ext(ext_ref, prev_ref, cur_ref, next_ref, seq=seq, tile=tile)
    width = cur_ref.shape[1]
    gd = width // len(POOL_WINDOWS)
    n0 = tile + 2 * HALO
    s_ref[0, 0:n0 - 8, :] = ext_ref[0:n0 - 8, :] + ext_ref[1:n0 - 7, :]
    for k in range(1, len(POOL_WINDOWS)):
        step = 1 << k
        ln = n0 - 8 * (k + 1)
        s_ref[k, 0:ln, :] = s_ref[k - 1, 0:ln, :] + s_ref[k - 1, step:step + ln, :]
    lane = lax.broadcasted_iota(jnp.int32, (tile, width), 1)
    pos = pl.program_id(1) * tile + lax.broadcasted_iota(jnp.int32, (tile, width), 0)
    grp = lane // gd
    wsum = jnp.zeros((tile, width), F32)
    half = jnp.zeros((tile, width), jnp.int32)
    for k, win in enumerate(POOL_WINDOWS):
        start = HALO - win // 2
        wsum = jnp.where(grp == k, s_ref[k, start:start + tile, :], wsum)
        half = jnp.where(grp == k, win // 2, half)
    cnt = jnp.minimum(pos + half, seq) - jnp.maximum(pos - half, 0)
    z = ext_ref[HALO:HALO + tile, :]
    dlt = (wsum / cnt.astype(F32) - z).astype(BF16)
    o_ref[...] = (_dot(dlt, lin_ref[...]) * scale_ref[...]).astype(BF16)


def _pool(zp, lin_bd, scale, *, row0, batch, seq, out_rows, out_row0):
    width = zp.shape[1]
    tile = min(SEQ_TILE, seq)
    kern = functools.partial(_pool_kernel, seq=seq, tile=tile)
    return pl.pallas_call(
        kern,
        grid=(batch, seq // tile),
        in_specs=_halo_specs(width, row0=row0, seq=seq, tile=tile) + [
            pl.BlockSpec((width, width), lambda b, i: (0, 0)),
            pl.BlockSpec((1, width), lambda b, i: (0, 0)),
        ],
        out_specs=pl.BlockSpec((tile, width), lambda b, i: ((out_row0 + b * seq) // tile + i, 0)),
        out_shape=jax.ShapeDtypeStruct((out_rows, width), BF16),
        scratch_shapes=[
            pltpu.VMEM((tile + 2 * HALO, width), F32),
            pltpu.VMEM((4, tile + 2 * HALO, width), F32),
        ],
        compiler_params=_cparams(("arbitrary", "arbitrary")),
    )(zp, zp, zp, lin_bd, scale)


def _hy_pre_kernel(prev_ref, cur_ref, next_ref, w_ref, b_ref, u_ref, x0_ref, ext_ref, *, seq, tile, hw):
    _fill_ext(ext_ref, prev_ref, cur_ref, next_ref, seq=seq, tile=tile)
    y = b_ref[...] + ext_ref[HALO - 1:HALO - 1 + tile, :] * w_ref[0:1, :]
    y = y + ext_ref[HALO:HALO + tile, :] * w_ref[1:2, :]
    y = y + ext_ref[HALO + 1:HALO + 1 + tile, :] * w_ref[2:3, :]
    x0_ref[...] = y[:, :hw]
    u_ref[...] = y[:, 2 * hw:] * y[:, hw:2 * hw]


def _hy_pre(zh, sw, sb, *, row0, batch, seq, out_rows, out_row0):
    width = zh.shape[1]
    hw = width // 3
    tile = min(SEQ_TILE, seq)
    kern = functools.partial(_hy_pre_kernel, seq=seq, tile=tile, hw=hw)
    ospec = pl.BlockSpec((tile, hw), lambda b, i: ((out_row0 + b * seq) // tile + i, 0))
    return pl.pallas_call(
        kern,
        grid=(batch, seq // tile),
        in_specs=_halo_specs(width, row0=row0, seq=seq, tile=tile) + [
            pl.BlockSpec((3, width), lambda b, i: (0, 0)),
            pl.BlockSpec((1, width), lambda b, i: (0, 0)),
        ],
        out_specs=[ospec, ospec],
        out_shape=[jax.ShapeDtypeStruct((out_rows, hw), F32), jax.ShapeDtypeStruct((out_rows, hw), F32)],
        scratch_shapes=[pltpu.VMEM((tile + 2 * HALO, width), F32)],
        compiler_params=_cparams(("arbitrary", "arbitrary")),
    )(zh, zh, zh, sw, sb)


def _filter_kernel(feat_ref, w1_ref, b1_ref, f1_ref, w2_ref, b2_ref, f2_ref, w3_ref, dl_ref,
                   taps_ref, asum_ref, *, tile, hw, seq, embp):
    feat = feat_ref[...]
    h = jnp.sin(f1_ref[...] * (_dot3(feat, w1_ref[...]) + b1_ref[...]))
    h = jnp.sin(f2_ref[...] * (_dot3(h, w2_ref[...]) + b2_ref[...]))
    h = _dot3(h, w3_ref[...])
    odd = lax.broadcasted_iota(jnp.int32, (tile, 2 * hw), 1) >= hw
    tcol = jnp.where(odd, feat[:, embp:embp + 1], feat[:, 0:1])
    dec = jnp.exp(-tcol * dl_ref[...])
    row = pl.program_id(0) * tile + lax.broadcasted_iota(jnp.int32, (tile, 2 * hw), 0)
    pos = 2 * row + odd.astype(jnp.int32)
    taps = jnp.where(pos == seq, 0.0, h * dec)
    taps_ref[...] = taps
    part = jnp.sum(jnp.abs(taps), axis=0, keepdims=True)

    @pl.when(pl.program_id(0) == 0)
    def _():
        asum_ref[...] = part

    @pl.when(pl.program_id(0) != 0)
    def _():
        asum_ref[...] = asum_ref[...] + part


def _hy_filter(seq, w1, b1, f1, w2, b2, f2, w3, hw):
    ar = jnp.arange(seq, dtype=jnp.int32)
    pos = jnp.concatenate([ar, (seq - ar) % seq]).astype(F32)[:, None]
    t = pos * (1.0 / (seq - 1))
    w = (2.0 * math.pi / seq) * pos
    bands = jnp.linspace(1e-4, HY_BANDS - 1, HY_BANDS, dtype=F32)[None, :]
    emb = 1 + 2 * HY_BANDS
    embp = ((emb + 7) // 8) * 8
    feat = jnp.concatenate([t, jnp.cos(bands * w), -jnp.sin(bands * w), jnp.zeros((2 * seq, embp - emb), F32)], axis=-1)
    w1p = jnp.concatenate([w1, jnp.zeros((embp - emb, w1.shape[1]), F32)], axis=0)
    max_decay = math.log(1.0 / HY_TARGET) / HY_FAST
    min_decay = math.log(1.0 / HY_TARGET) / HY_SLOW
    deltas = jnp.linspace(min_decay, max_decay, hw, dtype=F32)[None, :]
    feat = feat.reshape(seq, 2 * embp)
    hid = w1.shape[1]

    def pair(m):
        z = jnp.zeros_like(m)
        return jnp.concatenate([jnp.concatenate([m, z], axis=1), jnp.concatenate([z, m], axis=1)], axis=0)

    twice = lambda v: jnp.concatenate([v, v])[None, :]
    w3p = jnp.concatenate([pair(w3[:, :hw]), pair(w3[:, hw:])], axis=1)
    tile = min(SEQ_TILE, seq) // 2
    nfwd = seq // (2 * tile)
    full = lambda shape: pl.BlockSpec(shape, lambda i: (0,) * len(shape))
    taps, asum = pl.pallas_call(
        functools.partial(_filter_kernel, tile=tile, hw=hw, seq=seq, embp=embp),
        grid=(2 * nfwd,),
        in_specs=[
            pl.BlockSpec((tile, 2 * embp), lambda i: (i, 0)),
            full((2 * embp, 2 * hid)), full((1, 2 * hid)), full((1, 2 * hid)),
            full((2 * hid, 2 * hid)), full((1, 2 * hid)), full((1, 2 * hid)),
            pl.BlockSpec((2 * hid, 2 * hw), lambda i: (0, i // nfwd)), full((1, 2 * hw)),
        ],
        out_specs=[
            pl.BlockSpec((tile, 2 * hw), lambda i: (i, 0)),
            pl.BlockSpec((1, 2 * hw), lambda i: (0, 0)),
        ],
        out_shape=[
            jax.ShapeDtypeStruct((seq, 2 * hw), F32),
            jax.ShapeDtypeStruct((1, 2 * hw), F32),
        ],
        compiler_params=_cparams(("arbitrary",)),
    )(feat, pair(w1p), twice(b1), twice(f1), pair(w2), twice(b2), twice(f2), w3p,
      jnp.concatenate([deltas, deltas], axis=1))
    return taps.reshape(2 * seq, hw), asum[:, :hw] + asum[:, hw:]


def _np_split(a):
    a32 = jnp.asarray(a, F32)
    hi = a32.astype(BF16)
    lo = (a32 - hi.astype(F32)).astype(BF16)
    return hi, lo


def _dft_consts(n2):
    n1 = FFT_N1
    n = n1 * n2
    half = n1 // 2
    a1 = -2.0 * np.pi * np.outer(np.arange(n1), np.arange(n1)) / n1
    f1r, f1i = np.cos(a1), np.sin(a1)
    f1_data = np.block([[f1r[:, :half], -f1i[:, :half]], [f1i[:, :half], f1r[:, :half]]])
    f1_real = np.concatenate([f1r, f1i], axis=0)
    g1r, g1i = f1r / n, -f1i / n
    g1 = np.block([[g1r[:half], -g1i[:half]], [g1i[:half], g1r[:half]]])
    a2 = -2.0 * np.pi * np.outer(np.arange(n2), np.arange(n2)) / n2
    f2r, f2i = np.cos(a2), np.sin(a2)
    f2 = np.block([[f2r, -f2i], [f2i, f2r]])
    g2 = np.block([[f2r, f2i], [-f2i, f2r]])
    at = -2.0 * np.pi * np.outer(np.arange(n1), np.arange(n2)) / n
    twr = jnp.broadcast_to(jnp.asarray(np.cos(at), F32)[:, :, None], (n1, n2, LANES))
    twi = jnp.broadcast_to(jnp.asarray(np.sin(at), F32)[:, :, None], (n1, n2, LANES))
    return dict(f1_data=_np_split(f1_data), f1_real=_np_split(f1_real), g1=_np_split(g1),
                f2=_np_split(f2), g2=_np_split(g2), twr=twr, twi=twi)


def _fft1_kernel(z_ref, fh_ref, fl_ref, ar_ref, ai_ref):
    a = _dot3c(fh_ref[...], fl_ref[...], z_ref[...])
    ar_ref[...] = a[:FFT_N1]
    ai_ref[...] = a[FFT_N1:]


def _fft_first(zview, fmat, ncols):
    cb = min(FFT_COLS, ncols)
    fh, fl = fmat
    cspec = pl.BlockSpec(fh.shape, lambda j: (0, 0))
    ospec = pl.BlockSpec((FFT_N1, cb), lambda j: (0, j))
    return pl.pallas_call(
        _fft1_kernel,
        grid=(ncols // cb,),
        in_specs=[pl.BlockSpec((FFT_N1, cb), lambda j: (0, j)), cspec, cspec],
        out_specs=[ospec, ospec],
        out_shape=[jax.ShapeDtypeStruct((FFT_N1, ncols), F32)] * 2,
        compiler_params=_cparams(("arbitrary",)),
    )(zview, fh, fl)


def _fftmid_kernel(ar_ref, ai_ref, twr_ref, twi_ref, f2h_ref, f2l_ref, *rest, filter_only, n2):
    kb, _, width = ar_ref.shape
    reps = width // LANES
    lanes = lambda parts: jnp.concatenate(parts, axis=1)
    twr = lanes([t for j in range(kb) for t in [twr_ref[j]] * reps])
    twi = lanes([t for j in range(kb) for t in [twi_ref[j]] * reps])
    ar = lanes([ar_ref[j] for j in range(kb)])
    ai = lanes([ai_ref[j] for j in range(kb)])
    z = jnp.concatenate([ar * twr - ai * twi, ar * twi + ai * twr], axis=0)
    x = _dot3c(f2h_ref[...], f2l_ref[...], z)
    xr, xi = x[:n2], x[n2:]
    if filter_only:
        kr_out, ki_out = rest
        for j in range(kb):
            kr_out[j] = xr[:, j * width:(j + 1) * width]
            ki_out[j] = xi[:, j * width:(j + 1) * width]
        return
    kr_ref, ki_ref, g2h_ref, g2l_ref, br_out, bi_out = rest
    kr = lanes([kr_ref[j] for j in range(kb)])
    ki = lanes([ki_ref[j] for j in range(kb)])
    y = jnp.concatenate([xr * kr - xi * ki, xr * ki + xi * kr], axis=0)
    w = _dot3c(g2h_ref[...], g2l_ref[...], y)
    wr, wi = w[:n2], w[n2:]
    br = wr * twr + wi * twi
    bi = wi * twr - wr * twi
    for j in range(kb):
        br_out[j] = br[:, j * width:(j + 1) * width]
        bi_out[j] = bi[:, j * width:(j + 1) * width]


def _fft_mid(ar, ai, consts, n2, width, spectrum=None):
    a3r = ar.reshape(FFT_N1, n2, width)
    a3i = ai.reshape(FFT_N1, n2, width)
    blk = pl.BlockSpec((FFT_KB, n2, width), lambda k: (k, 0, 0))
    twspec = pl.BlockSpec((FFT_KB, n2, LANES), lambda k: (k, 0, 0))
    cspec = pl.BlockSpec((2 * n2, 2 * n2), lambda k: (0, 0))
    in_specs = [blk, blk, twspec, twspec, cspec, cspec]
    args = [a3r, a3i, consts["twr"], consts["twi"], *consts["f2"]]
    if spectrum is not None:
        in_specs += [blk, blk, cspec, cspec]
        args += [spectrum[0], spectrum[1], *consts["g2"]]
    return pl.pallas_call(
        functools.partial(_fftmid_kernel, filter_only=spectrum is None, n2=n2),
        grid=(FFT_N1 // FFT_KB,),
        in_specs=in_specs,
        out_specs=[blk, blk],
        out_shape=[jax.ShapeDtypeStruct((FFT_N1, n2, width), F32)] * 2,
        compiler_params=_cparams(("arbitrary",)),
    )(*args)


def _fftlast_kernel(br_ref, bi_ref, gh_ref, gl_ref, u_ref, x0_ref, bias_ref, invn_ref, o_ref):
    b = jnp.concatenate([br_ref[...], bi_ref[...]], axis=0)
    y = _dot3c(gh_ref[...], gl_ref[...], b)
    o_ref[...] = ((y * invn_ref[...] + u_ref[...] * bias_ref[...]) * x0_ref[...]).astype(BF16)


def _fft_last(br, bi, gmat, uview, x0view, bias_t, invn_t, ncols):
    cb = min(FFT_COLS, ncols)
    gh, gl = gmat
    cspec = pl.BlockSpec(gh.shape, lambda j: (0, 0))
    dspec = pl.BlockSpec((FFT_N1, cb), lambda j: (0, j))
    vspec = pl.BlockSpec((1, cb), lambda j: (0, 0))
    return pl.pallas_call(
        _fftlast_kernel,
        grid=(ncols // cb,),
        in_specs=[dspec, dspec, cspec, cspec, dspec, dspec, vspec, vspec],
        out_specs=dspec,
        out_shape=jax.ShapeDtypeStruct((FFT_N1, ncols), BF16),
        compiler_params=_cparams(("arbitrary",)),
    )(br, bi, gh, gl, uview, x0view, bias_t, invn_t)


def _dft_small_consts(seq):
    n = 2 * seq
    a = -2.0 * np.pi * np.outer(np.arange(n), np.arange(n)) / n
    fr, fi = np.cos(a), np.sin(a)
    f_data = np.block([[fr[:, :seq], -fi[:, :seq]], [fi[:, :seq], fr[:, :seq]]])
    f_real = np.concatenate([fr, fi], axis=0)
    gr, gi = fr / n, -fi / n
    g = np.block([[gr[:seq], -gi[:seq]], [gi[:seq], gr[:seq]]])
    return _np_split(f_data), _np_split(f_real), _np_split(g)


def _hy_ctx_kernel(u_ref, x0_ref, taps_ref, fdh_ref, fdl_ref, frh_ref, frl_ref, gh_ref, gl_ref,
                   bias_ref, invn_ref, o_ref):
    n = taps_ref.shape[0]
    z = u_ref[...]
    a = _dot3c(fdh_ref[...], fdl_ref[...], z)
    k = _dot3c(frh_ref[...], frl_ref[...], taps_ref[...])
    ar, ai, kr, ki = a[:n], a[n:], k[:n], k[n:]
    y = jnp.concatenate([ar * kr - ai * ki, ar * ki + ai * kr], axis=0)
    conv = _dot3c(gh_ref[...], gl_ref[...], y)
    o_ref[...] = ((conv * invn_ref[...] + z * bias_ref[...]) * x0_ref[...]).astype(BF16)


def _hy_ctx(u, x0, taps, bias, invn, small):
    rows, hw = u.shape
    full = lambda a: pl.BlockSpec(a.shape, lambda i: (0,) * a.ndim)
    args = [u, x0, taps, *small[0], *small[1], *small[2], bias, invn]
    return pl.pallas_call(
        _hy_ctx_kernel,
        grid=(1,),
        in_specs=[full(a) for a in args],
        out_specs=pl.BlockSpec((rows, hw), lambda i: (0, 0)),
        out_shape=jax.ShapeDtypeStruct((rows, hw), BF16),
        compiler_params=_cparams(("arbitrary",)),
    )(*args)


def _out_proj_kernel(x_ref, ypl_ref, yhl_ref, yal_ref, ypc_ref, yhc_ref, yac_ref, w_ref, g_ref, mod_ref, *rest,
                     nlt, tpb, ncond, pw, hw, moe, ntiles):
    if moe:
        rw_ref, xo_ref, u_ref, route_ref, un_ref = rest
    else:
        wg_ref, wu_ref, wd_ref, xo_ref = rest
    t = jnp.minimum(pl.program_id(0), ntiles - 1)
    if moe:
        @pl.when(pl.program_id(0) == 0)
        def _():
            un_ref[...] = jnp.zeros_like(un_ref)

        logits = _dot3(un_ref[...], rw_ref[...])
    ci = _cond_row(t, nlt, tpb, ncond)
    is_ctx = t >= nlt
    yp = jnp.where(is_ctx, ypc_ref[...], ypl_ref[...])
    yh = jnp.where(is_ctx, yhc_ref[...], yhl_ref[...])
    ya = jnp.where(is_ctx, yac_ref[...], yal_ref[...])
    mix = _dot(yp, w_ref[0:pw, :]) + _dot(yh, w_ref[pw:pw + hw, :]) + _dot(ya, w_ref[pw + hw:, :])
    x = x_ref[...] + mod_ref[2, pl.ds(ci, 1), :] * mix
    un = _normmod(x, g_ref[...], mod_ref[3, pl.ds(ci, 1), :], mod_ref[4, pl.ds(ci, 1), :])
    if not moe:
        u = un.astype(BF16)
        ff = wg_ref.shape[1]
        fc = ff // FFN_CHUNKS
        y = jnp.zeros(x.shape, F32)
        for c in range(FFN_CHUNKS):
            gate = _dot(u, wg_ref[:, c * fc:(c + 1) * fc])
            up = _dot(u, wu_ref[:, c * fc:(c + 1) * fc])
            y = y + _dot((_silu(gate) * up).astype(BF16), wd_ref[c * fc:(c + 1) * fc, :])
        xo_ref[...] = x + mod_ref[5, pl.ds(ci, 1), :] * y
    else:
        xo_ref[...] = x
        u_ref[...] = un.astype(BF16)
        un_ref[...] = un
        lane = lax.broadcasted_iota(jnp.int32, logits.shape, 1)
        neg = jnp.float32(-jnp.inf)
        lg = jnp.where(lane < N_EXPERTS, logits, neg)
        t1 = jnp.max(lg, axis=-1, keepdims=True)
        i1 = jnp.min(jnp.where(lg == t1, lane, LANES), axis=-1, keepdims=True)
        lg2 = jnp.where(lane == i1, neg, lg)
        t2 = jnp.max(lg2, axis=-1, keepdims=True)
        i2 = jnp.min(jnp.where(lg2 == t2, lane, LANES), axis=-1, keepdims=True)
        e2 = jnp.exp(t2 - t1)
        g1 = 1.0 / (1.0 + e2)
        g2 = e2 / (1.0 + e2)
        route_ref[...] = jnp.where(lane == 0, i1.astype(F32), jnp.where(lane == 1, i2.astype(F32),
                                   jnp.where(lane == 2, g1, jnp.where(lane == 3, g2, 0.0))))


def _out_proj(x, lat, ctx, w, g, mod, rw, ffn_w, *, nlt, tpb, ncond, out_rows):
    n, d = x.shape
    pw, hw, aw = (a.shape[1] for a in lat)
    tm = ROW_TILE
    moe = rw is not None
    nt = (n if moe else out_rows) // tm
    cur = lambda t: jnp.minimum(t, nt - 1)
    row = lambda width: pl.BlockSpec((tm, width), lambda t: (cur(t), 0))
    lrow = lambda width: pl.BlockSpec((tm, width), lambda t: (jnp.minimum(t, nlt - 1), 0))
    crow = lambda width: pl.BlockSpec((tm, width), lambda t: (jnp.maximum(cur(t) - nlt, 0), 0))
    in_specs = [row(d), lrow(pw), lrow(hw), lrow(aw), crow(pw), crow(hw), crow(aw),
                pl.BlockSpec(w.shape, lambda t: (0, 0)),
                pl.BlockSpec((1, d), lambda t: (0, 0)),
                pl.BlockSpec(mod.shape, lambda t: (0, 0, 0))]
    args = [x, *lat, *ctx, w, g, mod]
    scratch = []
    if not moe:
        const = lambda a: pl.BlockSpec(a.shape, lambda t: (0, 0), pipeline_mode=pl.Buffered(1))
        in_specs += [const(a) for a in ffn_w]
        args += list(ffn_w)
        out_specs = [row(d)]
        out_shape = [jax.ShapeDtypeStruct((out_rows, d), F32)]
    else:
        out_specs = [row(d), row(d)]
        out_shape = [jax.ShapeDtypeStruct((n, d), F32), jax.ShapeDtypeStruct((n, d), BF16)]
        in_specs.append(pl.BlockSpec(rw.shape, lambda t: (0, 0)))
        args.append(rw)
        out_specs.append(pl.BlockSpec((tm, LANES), lambda t: (jnp.maximum(t - 1, 0), 0)))
        out_shape.append(jax.ShapeDtypeStruct((n, LANES), F32))
        scratch.append(pltpu.VMEM((tm, d), F32))
    kern = functools.partial(_out_proj_kernel, nlt=nlt, tpb=tpb, ncond=ncond, pw=pw, hw=hw, moe=moe, ntiles=nt)
    return pl.pallas_call(
        kern, grid=(nt + 1 if moe else nt,), in_specs=in_specs, out_specs=out_specs, out_shape=out_shape,
        scratch_shapes=scratch,
        compiler_params=_cparams(("arbitrary",)),
    )(*args)


def _moe_kernel(te_ref, tv_ref, u_ref, wg_ref, wu_ref, wd_ref, *rest):
    o_ref = rest[-1]
    @pl.when(tv_ref[pl.program_id(0)] > 0)
    def _():
        u = u_ref[...]
        ff = wg_ref.shape[3]
        fc = ff // MOE_CHUNKS
        y = jnp.zeros(o_ref.shape, F32)
        for c in range(MOE_CHUNKS):
            sl = slice(c * fc, (c + 1) * fc)
            h = (_silu(_dot(u, wg_ref[0, 0, :, sl])) * _dot(u, wu_ref[0, 0, :, sl])).astype(BF16)
            y = y + _dot(h, wd_ref[0, 0, sl, :])
        o_ref[...] = y.astype(BF16)


def _moe_experts(ug, tile_expert, tile_valid, wg, wu, wd, j, *, tile0, total_rows, prev=None):
    p, d = ug.shape
    ff = wg.shape[3]
    tm = MOE_TM
    one = pl.Buffered(1)
    in_specs = [
        pl.BlockSpec((tm, d), lambda t, te, tv: (t, 0)),
        pl.BlockSpec((1, 1, d, ff), lambda t, te, tv: (j, te[t], 0, 0), pipeline_mode=one),
        pl.BlockSpec((1, 1, d, ff), lambda t, te, tv: (j, te[t], 0, 0), pipeline_mode=one),
        pl.BlockSpec((1, 1, ff, d), lambda t, te, tv: (j, te[t], 0, 0), pipeline_mode=one),
    ]
    args = [tile_expert, tile_valid, ug, wg, wu, wd]
    aliases = {}
    if prev is not None:
        in_specs.append(pl.BlockSpec(memory_space=pl.ANY))
        args.append(prev)
        aliases = {len(args) - 1: 0}
    grid_spec = pltpu.PrefetchScalarGridSpec(
        num_scalar_prefetch=2,
        grid=(p // tm,),
        in_specs=in_specs,
        out_specs=pl.BlockSpec((tm, d), lambda t, te, tv: (t + tile0, 0)),
    )
    return pl.pallas_call(
        _moe_kernel, grid_spec=grid_spec,
        out_shape=jax.ShapeDtypeStruct((total_rows, d), BF16),
        input_output_aliases=aliases,
        compiler_params=_cparams(("arbitrary",)),
    )(*args)


def _moe_combine_kernel(x_ref, ya_ref, yb_ref, route_ref, mod_ref, o_ref, *, nlt, tpb, ncond):
    ci = _cond_row(pl.program_id(0), nlt, tpb, ncond)
    r = route_ref[...]
    lane = lax.broadcasted_iota(jnp.int32, r.shape, 1)
    g1 = jnp.sum(jnp.where(lane == 2, r, 0.0), axis=-1, keepdims=True)
    g2 = jnp.sum(jnp.where(lane == 3, r, 0.0), axis=-1, keepdims=True)
    y = g1 * ya_ref[...].astype(F32) + g2 * yb_ref[...].astype(F32)
    o_ref[...] = x_ref[...] + mod_ref[5, pl.ds(ci, 1), :] * y


def _moe_combine(x, ya, yb, route, mod, *, nlt, tpb, ncond, out_rows):
    d = x.shape[1]
    tm = ROW_TILE
    row = lambda width: pl.BlockSpec((tm, width), lambda t: (t, 0))
    kern = functools.partial(_moe_combine_kernel, nlt=nlt, tpb=tpb, ncond=ncond)
    return pl.pallas_call(
        kern,
        grid=(out_rows // tm,),
        in_specs=[row(d), row(d), row(d), row(LANES), pl.BlockSpec(mod.shape, lambda t: (0, 0, 0))],
        out_specs=row(d),
        out_shape=jax.ShapeDtypeStruct((out_rows, d), F32),
        compiler_params=_cparams(("arbitrary",)),
    )(x, ya, yb, route, mod)


def _moe_layer(x, u, route, mod, wg, wu, wd, j, *, nlt, tpb, ncond, out_rows):
    n, d = x.shape
    tm = MOE_TM
    experts = jnp.concatenate([route[:, 0], route[:, 1]]).astype(jnp.int32)
    onehot = (experts[:, None] == jnp.arange(N_EXPERTS, dtype=jnp.int32)[None, :]).astype(jnp.int32)
    rank = jnp.sum(onehot * (jnp.cumsum(onehot, axis=0) - 1), axis=1)
    counts = jnp.sum(onehot, axis=0)
    padded = ((counts + tm - 1) // tm) * tm
    ends = jnp.cumsum(padded)
    starts = ends - padded
    dest = starts[experts] + rank
    p = 2 * n + N_EXPERTS * tm
    tokens = jnp.concatenate([jnp.arange(n, dtype=jnp.int32)] * 2)
    inb = dict(mode="promise_in_bounds")
    src = jnp.zeros((p,), jnp.int32).at[dest].set(tokens, unique_indices=True, **inb)
    tile_start = jnp.arange(p // tm, dtype=jnp.int32) * tm
    tile_expert = jnp.minimum(jnp.searchsorted(ends, tile_start, side="right"), N_EXPERTS - 1).astype(jnp.int32)
    tile_valid = (tile_start < ends[-1]).astype(jnp.int32)
    nt = p // tm
    cuts = [nt * k // MOE_PARTS for k in range(MOE_PARTS + 1)]
    ys = None
    for lo, hi in zip(cuts[:-1], cuts[1:]):
        ug = u.at[src[lo * tm:hi * tm]].get(**inb)
        ys = _moe_experts(ug, tile_expert[lo:hi], tile_valid[lo:hi], wg, wu, wd, j,
                          tile0=lo, total_rows=p, prev=ys)
    ya = ys.at[dest[:out_rows]].get(**inb)
    yb = ys.at[dest[n:n + out_rows]].get(**inb)
    return _moe_combine(x, ya, yb, route, mod, nlt=nlt, tpb=tpb, ncond=ncond, out_rows=out_rows)


def _rope_tables(seq, n_ctx_rows, batch):
    rows = seq // GRID_W
    row = jnp.repeat(jnp.arange(rows, dtype=F32), GRID_W)
    col = jnp.broadcast_to(jnp.arange(GRID_W, dtype=F32), (rows, GRID_W)).reshape(-1)
    inv_freq = jnp.power(ROPE_THETA, -jnp.arange(ROPE_FREQS, dtype=F32) / ROPE_FREQS)
    ar = row[:, None] * inv_freq
    ac = col[:, None] * inv_freq
    cos = jnp.concatenate([jnp.cos(ar), jnp.cos(ar), jnp.cos(ac), jnp.cos(ac)], axis=-1)
    sin = jnp.concatenate([-jnp.sin(ar), jnp.sin(ar), -jnp.sin(ac), jnp.sin(ac)], axis=-1)
    cos = jnp.tile(jnp.concatenate([cos, cos], axis=-1), (batch, 1))
    sin = jnp.tile(jnp.concatenate([sin, sin], axis=-1), (batch, 1))
    cos = jnp.concatenate([cos, jnp.ones((n_ctx_rows, 2 * HEAD_DIM), F32)], axis=0)
    sin = jnp.concatenate([sin, jnp.zeros((n_ctx_rows, 2 * HEAD_DIM), F32)], axis=0)
    return cos, sin


def kernel(x, c, ctx, c_ctx, mod_w, mod_b, norm1_g, norm2_g, w_in, w_out, pool_lin, pool_scale, hy_short_w, hy_short_b, hy_f_w1, hy_f_b1, hy_f_freq1, hy_f_w2, hy_f_b2, hy_f_freq2, hy_f_w3, hy_bias, qk_norm_g, diff_lambda, subln_g, ffn_w_gate, ffn_w_up, ffn_w_down, router_w, moe_w_gate, moe_w_up, moe_w_down):
    batch, seq, d = x.shape
    ctx_len = ctx.shape[1]
    depth = mod_w.shape[0]
    pw = pool_scale.shape[1]
    hw = hy_bias.shape[1]
    nl, nc = batch * seq, batch * ctx_len
    n = nl + nc
    tm = ROW_TILE
    assert seq % tm == 0 and nc % tm == 0 and seq % (FFT_N1 // 2) == 0 and seq % GRID_W == 0
    assert d == HEADS * 2 * V_DIM and pw == hw and batch == 2
    n2 = 2 * seq // FFT_N1
    ncols = n2 * hw
    assert (n * hw) % ncols == 0
    tiles = dict(nlt=nl // tm, tpb=seq // tm, ncond=batch + 1)

    xs = jnp.concatenate([x.reshape(nl, d), ctx.reshape(nc, d)], axis=0)
    cond8 = jnp.concatenate([c, c_ctx[None, :], jnp.zeros((8 - batch - 1, d), F32)], axis=0)
    mods = _modvec(cond8, mod_w, mod_b)
    cos_t, sin_t = _rope_tables(seq, nc, batch)
    consts = _dft_consts(n2)
    small = _dft_small_consts(ctx_len)
    eye = jnp.eye(len(POOL_WINDOWS), dtype=F32)
    n_moe = moe_w_gate.shape[0]
    assert n_moe <= depth
    moe_w2d = [w.reshape(-1, w.shape[-1]) for w in (moe_w_gate, moe_w_up, moe_w_down)]
    moe_bf16 = {}

    for i in range(depth):
        last = i == depth - 1
        lam_init = 0.8 - 0.6 * math.exp(-0.3 * i)
        mod = mods[i]
        gqk = jnp.concatenate([qk_norm_g[i], qk_norm_g[i]], axis=-1)
        zp, zh, qt, k12, vt = _in_proj(xs, norm1_g[i][None, :], mod, w_in[i].astype(BF16), cos_t, sin_t, gqk,
                                       pw=pw, hw=hw, **tiles)

        g_b = jnp.broadcast_to(subln_g[i][:, None], (V_DIM, ATT_TQ))
        att_kw = dict(batch=batch, seq=seq, ctx_len=ctx_len, lam_init=lam_init)

        cast_j = i if i < n_moe else None

        def attend(direct, lam_p=diff_lambda[i], qt=qt, k12=k12, vt=vt, g_b=g_b, att_kw=att_kw, cast_j=cast_j):
            ctx_out = _attention(lam_p, qt, k12, vt, g_b, latent=False, direct=direct, **att_kw)
            if cast_j is None:
                return (_attention(lam_p, qt, k12, vt, g_b, latent=True, direct=direct, **att_kw), ctx_out)
            if direct:
                lat_out, *wb = _attention(lam_p, qt, k12, vt, g_b, latent=True, direct=True,
                                          cast=[(w, cast_j, n_moe) for w in moe_w2d], **att_kw)
            else:
                lat_out = _attention(lam_p, qt, k12, vt, g_b, latent=True, direct=False, **att_kw)
                wb = _cast_bf16(*[w[cast_j * (w.shape[0] // n_moe):(cast_j + 1) * (w.shape[0] // n_moe)]
                                  for w in moe_w2d])
            return (lat_out, ctx_out, *wb)

        bound = (HEAD_DIM * QSCALE) * jnp.max(jnp.abs(qk_norm_g[i][0])) * jnp.max(jnp.abs(qk_norm_g[i][1]))
        ya_l, ya_c, *wb = lax.cond(bound * 1.02 < ATT_DIRECT_MAX, lambda: attend(True), lambda: attend(False))
        if cast_j is not None:
            moe_bf16[cast_j] = tuple(w.reshape((1,) + src.shape[1:])
                                     for w, src in zip(wb, (moe_w_gate, moe_w_up, moe_w_down)))

        lin_bd = (eye[:, None, :, None] * pool_lin[i][:, :, None, :]).reshape(pw, pw).astype(BF16)
        pscale = pool_scale[i][None, :]
        yp_l = _pool(zp, lin_bd, pscale, row0=0, batch=batch, seq=seq, out_rows=nl, out_row0=0)
        yp_c = _pool(zp, lin_bd, pscale, row0=nl, batch=batch, seq=ctx_len, out_rows=nc, out_row0=0)

        sw, sb = hy_short_w[i], hy_short_b[i][None, :]
        u_l, x0_l = _hy_pre(zh, sw, sb, row0=0, batch=batch, seq=seq, out_rows=n, out_row0=0)
        u_c, x0_c = _hy_pre(zh, sw, sb, row0=nl, batch=batch, seq=ctx_len, out_rows=nc, out_row0=0)
        filt = (hy_f_w1[i], hy_f_b1[i], hy_f_freq1[i], hy_f_w2[i], hy_f_b2[i], hy_f_freq2[i], hy_f_w3[i])
        bias = hy_bias[i][None, :]
        taps, asum = _hy_filter(seq, *filt, hw)
        kr1, ki1 = _fft_first(taps.reshape(FFT_N1, ncols), consts["f1_real"], ncols)
        spectrum = _fft_mid(kr1, ki1, consts, n2, hw)
        ar, ai = _fft_first(u_l.reshape(-1, ncols), consts["f1_data"], ncols)
        br, bi = _fft_mid(ar, ai, consts, n2, hw, spectrum=spectrum)
        reps = min(FFT_COLS, ncols) // hw
        yh_lat = _fft_last(br.reshape(FFT_N1, ncols), bi.reshape(FFT_N1, ncols), consts["g1"],
                           u_l.reshape(-1, ncols), x0_l.reshape(-1, ncols),
                           jnp.tile(bias, (1, reps)), jnp.tile(1.0 / asum, (1, reps)), ncols)
        taps_c, asum_c = _hy_filter(ctx_len, *filt, hw)
        yh_ctx = _hy_ctx(u_c, x0_c, taps_c, bias, 1.0 / asum_c, small)

        j = i // 2
        moe = i % 2 == 1
        rw = ffn_w = None
        if moe:
            rw = jnp.concatenate([router_w[j], jnp.zeros((d, LANES - N_EXPERTS), F32)], axis=1)
        else:
            ffn_w = (ffn_w_gate[j].astype(BF16), ffn_w_up[j].astype(BF16), ffn_w_down[j].astype(BF16))
        out_rows = nl if last else n
        outs = _out_proj(xs, (yp_l, yh_lat.reshape(nl, hw), ya_l), (yp_c, yh_ctx, ya_c), w_out[i].astype(BF16),
                         norm2_g[i][None, :], mod, rw, ffn_w, out_rows=out_rows, **tiles)
        if moe:
            xs, u, route = outs
            xs = _moe_layer(xs, u, route, mod, *moe_bf16[j], 0, out_rows=out_rows, **tiles)
        else:
            xs, = outs
    return xs[:nl].reshape(batch, seq, d)
```

```python
import functools
import math

import numpy as np
import jax
import jax.numpy as jnp
from jax import lax
from jax.experimental import pallas as pl
from jax.experimental.pallas import tpu as pltpu

F32 = jnp.float32
BF16 = jnp.bfloat16
EPS = 1e-6

GRID_W = 64
POOL_WINDOWS = (2, 4, 8, 16)
HEADS = 4
HEAD_DIM = 64
V_DIM = 128
ROPE_FREQS = 16
ROPE_THETA = 10000.0
HY_BANDS = 16
HY_TARGET = 1e-2
HY_FAST = 0.3
HY_SLOW = 1.5
N_EXPERTS = 8

LANES = 128
ROW_TILE = 512
HALO = 32
FFT_N1 = 128
SEQ_TILE = 4096
FFT_COLS = 4096
FFT_KB = 4
ATT_TQ = 1024
ATT_TK = 512
ATT_UNROLL = 16
ATT_GROUP = 8
QSCALE = (HEAD_DIM ** -0.5) * math.log2(math.e)
ATT_DIRECT_MAX = 100.0
FFN_CHUNKS = 2
CAST_STEPS = 64
MOE_TM = 512
MOE_PARTS = 8
MOE_CHUNKS = 2
VMEM_LIMIT = 56 * 1024 * 1024


def _cparams(sem):
    return pltpu.CompilerParams(dimension_semantics=sem, vmem_limit_bytes=VMEM_LIMIT)


def _dot(a, b):
    return jnp.dot(a, b, preferred_element_type=F32)


def _split(a):
    hi = a.astype(BF16)
    lo = (a - hi.astype(F32)).astype(BF16)
    return hi, lo


def _dot3(a, b):
    ah, al = _split(a)
    bh, bl = _split(b)
    return _dot(ah, bh) + _dot(ah, bl) + _dot(al, bh)


def _dot3c(ch, cl, d):
    dh, dl = _split(d)
    return _dot(ch, dh) + _dot(ch, dl) + _dot(cl, dh)


def _silu(x):
    return x / (1.0 + jnp.exp(-x))


def _normmod(x, g, shift, scale):
    ms = jnp.mean(x * x, axis=-1, keepdims=True)
    return (x * lax.rsqrt(ms + EPS) * g) * (1.0 + scale) + shift


def _cond_row(t, n_lat_tiles, tiles_per_batch, n_cond):
    return jnp.where(t >= n_lat_tiles, n_cond - 1, t // tiles_per_batch)


def _cast_kernel(*refs):
    half = len(refs) // 2
    for w_ref, o_ref in zip(refs[:half], refs[half:]):
        o_ref[...] = w_ref[...].astype(BF16)


def _cast_bf16(*ws):
    flat = [w.reshape(-1, w.shape[-1]) for w in ws]
    steps = CAST_STEPS
    specs = [pl.BlockSpec((w.shape[0] // steps, w.shape[1]), lambda t: (t, 0)) for w in flat]
    outs = pl.pallas_call(
        _cast_kernel,
        grid=(steps,),
        in_specs=specs,
        out_specs=specs,
        out_shape=[jax.ShapeDtypeStruct(w.shape, BF16) for w in flat],
        compiler_params=_cparams(("arbitrary",)),
    )(*flat)
    return [o.reshape(w.shape) for o, w in zip(outs, ws)]


def _modvec_kernel(c_ref, w_ref, b_ref, o_ref):
    s = _silu(c_ref[...])
    o_ref[0, 0] = _dot3(s, w_ref[0]) + b_ref[0]


def _modvec(cond8, mod_w, mod_b):
    depth, d, six_d = mod_w.shape
    nchunk = six_d // d
    return pl.pallas_call(
        _modvec_kernel,
        grid=(depth, nchunk),
        in_specs=[
            pl.BlockSpec((8, d), lambda i, j: (0, 0)),
            pl.BlockSpec((1, d, d), lambda i, j: (i, 0, j)),
            pl.BlockSpec((1, 1, d), lambda i, j: (i, 0, j)),
        ],
        out_specs=pl.BlockSpec((1, 1, 8, d), lambda i, j: (i, j, 0, 0)),
        out_shape=jax.ShapeDtypeStruct((depth, nchunk, 8, d), F32),
        compiler_params=_cparams(("arbitrary", "arbitrary")),
    )(cond8, mod_w, mod_b.reshape(depth, 1, six_d))


def _in_proj_kernel(x_ref, g_ref, mod_ref, w_ref, cos_ref, sin_ref, gqk_ref,
                    zp_ref, zh_ref, qt_ref, k_ref, vt_ref, zatt_ref, *, nlt, tpb, ncond, pw, hw, ntiles):
    t = pl.program_id(0)

    @pl.when(t == 0)
    def _():
        zatt_ref[...] = jnp.zeros_like(zatt_ref)

    tm = x_ref.shape[0]
    att = pw + 3 * hw
    lane = lax.broadcasted_iota(jnp.int32, (tm, 2 * HEAD_DIM), 1)
    first = lane < HEAD_DIM
    apart = (lane % (2 * ROPE_FREQS)) < ROPE_FREQS
    cos = cos_ref[...]
    sin = sin_ref[...]
    qk_w = HEADS * 2 * HEAD_DIM

    def norm_rope(v, g):
        v2 = v * v
        s_all = jnp.sum(v2, axis=-1, keepdims=True)
        s_first = jnp.sum(jnp.where(first, v2, 0.0), axis=-1, keepdims=True)
        ms = jnp.where(first, s_first, s_all - s_first) * (1.0 / HEAD_DIM)
        vn = v * lax.rsqrt(ms + EPS) * g
        swapped = jnp.where(apart, pltpu.roll(vn, 2 * HEAD_DIM - ROPE_FREQS, 1), pltpu.roll(vn, ROPE_FREQS, 1))
        return vn * cos + swapped * sin

    for h in range(HEADS):
        lo = h * 2 * HEAD_DIM
        q = norm_rope(zatt_ref[:, lo:lo + 2 * HEAD_DIM], gqk_ref[0:1, :])
        qt_ref[h] = (q * QSCALE).T.astype(BF16)
        k = norm_rope(zatt_ref[:, qk_w + lo:qk_w + lo + 2 * HEAD_DIM], gqk_ref[1:2, :])
        k_ref[h] = k.astype(BF16)
        vlo = 2 * qk_w + h * V_DIM
        vt_ref[h] = zatt_ref[:, vlo:vlo + V_DIM].T.astype(BF16)

    ci = _cond_row(jnp.minimum(t, ntiles - 1), nlt, tpb, ncond)
    shift = mod_ref[0, pl.ds(ci, 1), :]
    scale = mod_ref[1, pl.ds(ci, 1), :]
    y = _normmod(x_ref[...], g_ref[...], shift, scale).astype(BF16)
    z = _dot(y, w_ref[...])
    zp_ref[...] = z[:, :pw].astype(BF16)
    zh_ref[...] = z[:, pw:att].astype(BF16)
    zatt_ref[...] = z[:, att:]


def _in_proj(x, g, mod, w, cos_t, sin_t, gqk, *, nlt, tpb, ncond, pw, hw):
    n, d = x.shape
    wid = w.shape[1]
    tm = ROW_TILE
    nt = n // tm
    kern = functools.partial(_in_proj_kernel, nlt=nlt, tpb=tpb, ncond=ncond, pw=pw, hw=hw, ntiles=nt)
    cur = lambda t: jnp.minimum(t, nt - 1)
    prv = lambda t: jnp.maximum(t - 1, 0)
    return pl.pallas_call(
        kern,
        grid=(nt + 1,),
        in_specs=[
            pl.BlockSpec((tm, d), lambda t: (cur(t), 0)),
            pl.BlockSpec((1, d), lambda t: (0, 0)),
            pl.BlockSpec(mod.shape, lambda t: (0, 0, 0)),
            pl.BlockSpec((d, wid), lambda t: (0, 0)),
            pl.BlockSpec((tm, 2 * HEAD_DIM), lambda t: (prv(t), 0)),
            pl.BlockSpec((tm, 2 * HEAD_DIM), lambda t: (prv(t), 0)),
            pl.BlockSpec((2, 2 * HEAD_DIM), lambda t: (0, 0)),
        ],
        out_specs=[
            pl.BlockSpec((tm, pw), lambda t: (cur(t), 0)),
            pl.BlockSpec((tm, 3 * hw), lambda t: (cur(t), 0)),
            pl.BlockSpec((HEADS, 2 * HEAD_DIM, tm), lambda t: (0, 0, prv(t))),
            pl.BlockSpec((HEADS, tm, 2 * HEAD_DIM), lambda t: (0, prv(t), 0)),
            pl.BlockSpec((HEADS, V_DIM, tm), lambda t: (0, 0, prv(t))),
        ],
        scratch_shapes=[pltpu.VMEM((tm, wid - pw - 3 * hw), F32)],
        out_shape=[
            jax.ShapeDtypeStruct((n, pw), BF16),
            jax.ShapeDtypeStruct((n, 3 * hw), BF16),
            jax.ShapeDtypeStruct((HEADS, 2 * HEAD_DIM, n), BF16),
            jax.ShapeDtypeStruct((HEADS, n, 2 * HEAD_DIM), BF16),
            jax.ShapeDtypeStruct((HEADS, V_DIM, n), BF16),
        ],
        compiler_params=_cparams(("arbitrary",)),
    )(x, g, mod, w, cos_t, sin_t, gqk)


def _attn_kernel(lam_ref, qt_ref, kc_ref, vtc_ref, *rest, tk, n_lat_chunks, lam_init):
    if n_lat_chunks:
        kl_ref, vtl_ref, g_ref, o_ref, m_ref, l_ref, acc_ref = rest
    else:
        g_ref, o_ref, m_ref, l_ref, acc_ref = rest
    qt = qt_ref[0]
    row = lax.broadcasted_iota(jnp.int32, qt.shape, 0)
    zero = jnp.zeros_like(qt)
    qmaps = (jnp.where(row < HEAD_DIM, qt, zero), jnp.where(row >= HEAD_DIM, qt, zero))

    def process(k_tile, vt_tile, first):
        for mi in range(2):
            s = _dot(k_tile, qmaps[mi])
            smax = jnp.max(s, axis=0, keepdims=True)
            if first:
                m_new = smax
            else:
                m_old = m_ref[mi]
                m_new = jnp.maximum(m_old, smax)
            p = jnp.exp2(s - m_new)
            psum = jnp.sum(p, axis=0, keepdims=True)
            pv = _dot(vt_tile, p.astype(BF16))
            if first:
                l_ref[mi] = psum
                acc_ref[mi] = pv
            else:
                alpha = jnp.exp2(m_old - m_new)
                l_ref[mi] = alpha * l_ref[mi] + psum
                acc_ref[mi] = alpha * acc_ref[mi] + pv
            m_ref[mi] = m_new

    process(kc_ref[0], vtc_ref[0], True)
    if n_lat_chunks:
        def body(j, carry):
            off = pl.multiple_of(j * tk, tk)
            process(kl_ref[0, pl.ds(off, tk), :], vtl_ref[0, :, pl.ds(off, tk)], False)
            return carry
        lax.fori_loop(0, n_lat_chunks, body, 0)

    lv = lam_ref[...]
    lam = (jnp.exp(jnp.sum(lv[0:1] * lv[1:2], axis=-1, keepdims=True))
           - jnp.exp(jnp.sum(lv[2:3] * lv[3:4], axis=-1, keepdims=True)) + lam_init)
    o = acc_ref[0] * (1.0 / l_ref[0]) - lam * (acc_ref[1] * (1.0 / l_ref[1]))
    ms = jnp.mean(o * o, axis=0, keepdims=True)
    y = o * lax.rsqrt(ms + EPS) * g_ref[...] * (1.0 - lam_init)
    o_ref[...] = y.T.astype(BF16)


def _attn_direct_kernel(lam_ref, qt_ref, kc_ref, vtc_ref, *rest, tk, n_lat_chunks, lam_init, n_cast=0):
    if n_lat_chunks:
        kl_ref, vtl_ref, g_ref = rest[:3]
        rest = rest[3:]
    else:
        g_ref = rest[0]
        rest = rest[1:]
    cast_in, (o_ref, *cast_out), (p_ref, l_ref, acc_ref) = rest[:n_cast], rest[n_cast:2 * n_cast + 1], rest[2 * n_cast + 1:]
    for w_ref, c_ref in zip(cast_in, cast_out):
        c_ref[...] = w_ref[...].astype(BF16)
    qt = qt_ref[0]
    tq = qt.shape[1]
    row = lax.broadcasted_iota(jnp.int32, qt.shape, 0)
    zero = jnp.zeros_like(qt)
    qmaps = (jnp.where(row < HEAD_DIM, qt, zero), jnp.where(row >= HEAD_DIM, qt, zero))
    grp = ATT_GROUP

    def keys(first_piece, count):
        if isinstance(first_piece, int):
            return pl.ds((first_piece - 1) * tk, count * tk)
        return pl.ds(pl.multiple_of((first_piece - 1) * tk, tk), count * tk)

    def stage_ab(k_tile, pset, r):
        for mi in range(2):
            p = jnp.exp2(_dot(k_tile, qmaps[mi]))
            l_ref[mi] = l_ref[mi] + jnp.sum(p.reshape(tk // 8, 8, tq), axis=0)
            p_ref[pset, mi, r * tk:(r + 1) * tk, :] = p.astype(BF16)

    def stage_c(vt_tile, pset, lo, count):
        for mi in range(2):
            acc_ref[mi] = acc_ref[mi] + _dot(vt_tile, p_ref[pset, mi, lo * tk:(lo + count) * tk, :])

    def group_ab(g, pset):
        k_tile = kl_ref[0, keys(grp * g, grp), :]
        for mi in range(2):
            p = jnp.exp2(_dot(k_tile, qmaps[mi]))
            l_ref[mi] = l_ref[mi] + jnp.sum(p.reshape(grp * tk // 8, 8, tq), axis=0)
            p_ref[pset, mi] = p.astype(BF16)

    def group_c(g, pset):
        stage_c(vtl_ref[0, :, keys(grp * g, grp)], pset, 0, grp)

    l_ref[...] = jnp.zeros_like(l_ref)
    acc_ref[...] = jnp.zeros_like(acc_ref)
    stage_ab(kc_ref[0], 0, 0)
    if n_lat_chunks == 0:
        stage_c(vtc_ref[0], 0, 0, 1)
    else:
        n_groups = n_lat_chunks // grp
        per_step = ATT_UNROLL // grp
        for r in range(1, grp):
            stage_ab(kl_ref[0, keys(r, 1), :], 0, r)
        group_ab(1, 1)
        stage_c(vtc_ref[0], 0, 0, 1)
        stage_c(vtl_ref[0, :, keys(1, grp - 1)], 0, 1, grp - 1)
        for g in range(2, per_step):
            group_ab(g, g % 2)
            group_c(g - 1, (g - 1) % 2)

        def body(jj, carry):
            g0 = per_step * (jj + 1)
            for q in range(per_step):
                group_ab(g0 + q, q % 2)
                group_c(g0 + q - 1, (q + 1) % 2)
            return carry

        lax.fori_loop(0, n_groups // per_step - 1, body, 0)
        stage_ab(kl_ref[0, keys(n_lat_chunks, 1), :], 0, 0)
        group_c(n_groups - 1, 1)
        stage_c(vtl_ref[0, :, keys(n_lat_chunks, 1)], 0, 0, 1)

    lv = lam_ref[...]
    lam = (jnp.exp(jnp.sum(lv[0:1] * lv[1:2], axis=-1, keepdims=True))
           - jnp.exp(jnp.sum(lv[2:3] * lv[3:4], axis=-1, keepdims=True)) + lam_init)
    l1 = jnp.sum(l_ref[0], axis=0, keepdims=True)
    l2 = jnp.sum(l_ref[1], axis=0, keepdims=True)
    o = acc_ref[0] * (1.0 / l1) - lam * (acc_ref[1] * (1.0 / l2))
    ms = jnp.mean(o * o, axis=0, keepdims=True)
    y = o * lax.rsqrt(ms + EPS) * g_ref[...] * (1.0 - lam_init)
    o_ref[...] = y.T.astype(BF16)


def _attention(lam_p, qt, k12, vt, g_b, *, batch, seq, ctx_len, latent, lam_init, direct, cast=()):
    n = k12.shape[1]
    nl = batch * seq
    tq = ATT_TQ if latent else ctx_len
    nq = seq // tq if latent else 1
    tk = ctx_len if direct else ATT_TK
    assert seq % ((ATT_UNROLL if direct else 1) * tk) == 0
    qbase = 0 if latent else nl // tq

    def qrow(b, iq):
        return (b * nq + iq) if latent else (qbase + b)

    in_specs = [
        pl.BlockSpec(lam_p.shape, lambda b, h, iq: (0, 0)),
        pl.BlockSpec((1, 2 * HEAD_DIM, tq), lambda b, h, iq: (h, 0, qrow(b, iq))),
        pl.BlockSpec((1, ctx_len, 2 * HEAD_DIM), lambda b, h, iq: (h, nl // ctx_len + b, 0)),
        pl.BlockSpec((1, V_DIM, ctx_len), lambda b, h, iq: (h, 0, nl // ctx_len + b)),
    ]
    args = [lam_p, qt, k12, vt]
    if latent:
        in_specs += [
            pl.BlockSpec((1, seq, 2 * HEAD_DIM), lambda b, h, iq: (h, b, 0)),
            pl.BlockSpec((1, V_DIM, seq), lambda b, h, iq: (h, 0, b)),
        ]
        args += [k12, vt]
    in_specs.append(pl.BlockSpec((V_DIM, tq), lambda b, h, iq: (0, 0)))
    args.append(g_b[:, :tq])
    nrows = nl if latent else batch * ctx_len
    body = _attn_direct_kernel if direct else _attn_kernel
    kw = dict(tk=tk, n_lat_chunks=(seq // tk if latent else 0), lam_init=lam_init)
    out_specs = [pl.BlockSpec((tq, V_DIM), lambda b, h, iq: (b * nq + iq, h))]
    out_shape = [jax.ShapeDtypeStruct((nrows, HEADS * V_DIM), BF16)]
    if cast:
        assert direct
        kw["n_cast"] = len(cast)
        steps = batch * HEADS * nq
        for w, slab, parts in cast:
            rows = w.shape[0] // parts // steps
            in_specs.append(pl.BlockSpec(
                (rows, w.shape[1]), lambda b, h, iq, slab=slab: (slab * steps + (b * HEADS + h) * nq + iq, 0)))
            args.append(w)
            out_specs.append(pl.BlockSpec((rows, w.shape[1]), lambda b, h, iq: ((b * HEADS + h) * nq + iq, 0)))
            out_shape.append(jax.ShapeDtypeStruct((w.shape[0] // parts, w.shape[1]), BF16))
    kern = functools.partial(body, **kw)
    if direct:
        scratch = [pltpu.VMEM((2, 2, ATT_GROUP * tk, tq), BF16),
                   pltpu.VMEM((2, 8, tq), F32), pltpu.VMEM((2, V_DIM, tq), F32)]
    else:
        scratch = [pltpu.VMEM((2, 1, tq), F32), pltpu.VMEM((2, 1, tq), F32), pltpu.VMEM((2, V_DIM, tq), F32)]
    outs = pl.pallas_call(
        kern,
        grid=(batch, HEADS, nq),
        in_specs=in_specs,
        out_specs=out_specs,
        out_shape=out_shape,
        scratch_shapes=scratch,
        compiler_params=_cparams(("arbitrary", "arbitrary", "arbitrary")),
    )(*args)
    return outs if cast else outs[0]


def _halo_specs(width, *, row0, seq, tile):
    hb = tile // HALO

    def cur(b, i):
        return ((row0 + b * seq) // tile + i, 0)

    def prev(b, i):
        first = (row0 + b * seq) // HALO
        return (jnp.maximum(first + i * hb - 1, first), 0)

    def nxt(b, i):
        first = (row0 + b * seq) // HALO
        return (jnp.minimum(first + (i + 1) * hb, first + seq // HALO - 1), 0)

    return [pl.BlockSpec((HALO, width), prev), pl.BlockSpec((tile, width), cur), pl.BlockSpec((HALO, width), nxt)]


def _fill_ext(ext_ref, prev_ref, cur_ref, next_ref, *, seq, tile):
    pos0 = pl.program_id(1) * tile
    width = cur_ref.shape[1]
    hpos = lax.broadcasted_iota(jnp.int32, (HALO, width), 0)
    ext_ref[0:HALO, :] = jnp.where(pos0 - HALO + hpos >= 0, prev_ref[...].astype(F32), 0.0)
    ext_ref[HALO:HALO + tile, :] = cur_ref[...].astype(F32)
    ext_ref[HALO + tile:, :] = jnp.where(pos0 + tile + hpos < seq, next_ref[...].astype(F32), 0.0)


def _pool_kernel(prev_ref, cur_ref, next_ref, lin_ref, scale_ref, o_ref, ext_ref, s_ref, *, seq, tile):
    _fill_ext(ext_ref, prev_ref, cur_ref, next_ref, seq=seq, tile=tile)
    width = cur_ref.shape[1]
    gd = width // len(POOL_WINDOWS)
    n0 = tile + 2 * HALO
    s_ref[0, 0:n0 - 8, :] = ext_ref[0:n0 - 8, :] + ext_ref[1:n0 - 7, :]
    for k in range(1, len(POOL_WINDOWS)):
        step = 1 << k
        ln = n0 - 8 * (k + 1)
        s_ref[k, 0:ln, :] = s_ref[k - 1, 0:ln, :] + s_ref[k - 1, step:step + ln, :]
    lane = lax.broadcasted_iota(jnp.int32, (tile, width), 1)
    pos = pl.program_id(1) * tile + lax.broadcasted_iota(jnp.int32, (tile, width), 0)
    grp = lane // gd
    wsum = jnp.zeros((tile, width), F32)
    half = jnp.zeros((tile, width), jnp.int32)
    for k, win in enumerate(POOL_WINDOWS):
        start = HALO - win // 2
        wsum = jnp.where(grp == k, s_ref[k, start:start + tile, :], wsum)
        half = jnp.where(grp == k, win // 2, half)
    cnt = jnp.minimum(pos + half, seq) - jnp.maximum(pos - half, 0)
    z = ext_ref[HALO:HALO + tile, :]
    dlt = (wsum / cnt.astype(F32) - z).astype(BF16)
    o_ref[...] = (_dot(dlt, lin_ref[...]) * scale_ref[...]).astype(BF16)


def _pool(zp, lin_bd, scale, *, row0, batch, seq, out_rows, out_row0):
    width = zp.shape[1]
    tile = min(SEQ_TILE, seq)
    kern = functools.partial(_pool_kernel, seq=seq, tile=tile)
    return pl.pallas_call(
        kern,
        grid=(batch, seq // tile),
        in_specs=_halo_specs(width, row0=row0, seq=seq, tile=tile) + [
            pl.BlockSpec((width, width), lambda b, i: (0, 0)),
            pl.BlockSpec((1, width), lambda b, i: (0, 0)),
        ],
        out_specs=pl.BlockSpec((tile, width), lambda b, i: ((out_row0 + b * seq) // tile + i, 0)),
        out_shape=jax.ShapeDtypeStruct((out_rows, width), BF16),
        scratch_shapes=[
            pltpu.VMEM((tile + 2 * HALO, width), F32),
            pltpu.VMEM((4, tile + 2 * HALO, width), F32),
        ],
        compiler_params=_cparams(("arbitrary", "arbitrary")),
    )(zp, zp, zp, lin_bd, scale)


def _hy_pre_kernel(prev_ref, cur_ref, next_ref, w_ref, b_ref, u_ref, x0_ref, ext_ref, *, seq, tile, hw):
    _fill_ext(ext_ref, prev_ref, cur_ref, next_ref, seq=seq, tile=tile)
    y = b_ref[...] + ext_ref[HALO - 1:HALO - 1 + tile, :] * w_ref[0:1, :]
    y = y + ext_ref[HALO:HALO + tile, :] * w_ref[1:2, :]
    y = y + ext_ref[HALO + 1:HALO + 1 + tile, :] * w_ref[2:3, :]
    x0_ref[...] = y[:, :hw]
    u_ref[...] = y[:, 2 * hw:] * y[:, hw:2 * hw]


def _hy_pre(zh, sw, sb, *, row0, batch, seq, out_rows, out_row0):
    width = zh.shape[1]
    hw = width // 3
    tile = min(SEQ_TILE, seq)
    kern = functools.partial(_hy_pre_kernel, seq=seq, tile=tile, hw=hw)
    ospec = pl.BlockSpec((tile, hw), lambda b, i: ((out_row0 + b * seq) // tile + i, 0))
    return pl.pallas_call(
        kern,
        grid=(batch, seq // tile),
        in_specs=_halo_specs(width, row0=row0, seq=seq, tile=tile) + [
            pl.BlockSpec((3, width), lambda b, i: (0, 0)),
            pl.BlockSpec((1, width), lambda b, i: (0, 0)),
        ],
        out_specs=[ospec, ospec],
        out_shape=[jax.ShapeDtypeStruct((out_rows, hw), F32), jax.ShapeDtypeStruct((out_rows, hw), F32)],
        scratch_shapes=[pltpu.VMEM((tile + 2 * HALO, width), F32)],
        compiler_params=_cparams(("arbitrary", "arbitrary")),
    )(zh, zh, zh, sw, sb)


def _filter_kernel(feat_ref, w1_ref, b1_ref, f1_ref, w2_ref, b2_ref, f2_ref, w3_ref, dl_ref,
                   taps_ref, asum_ref, *, tile, hw, seq, embp):
    feat = feat_ref[...]
    h = jnp.sin(f1_ref[...] * (_dot3(feat, w1_ref[...]) + b1_ref[...]))
    h = jnp.sin(f2_ref[...] * (_dot3(h, w2_ref[...]) + b2_ref[...]))
    h = _dot3(h, w3_ref[...])
    odd = lax.broadcasted_iota(jnp.int32, (tile, 2 * hw), 1) >= hw
    tcol = jnp.where(odd, feat[:, embp:embp + 1], feat[:, 0:1])
    dec = jnp.exp(-tcol * dl_ref[...])
    row = pl.program_id(0) * tile + lax.broadcasted_iota(jnp.int32, (tile, 2 * hw), 0)
    pos = 2 * row + odd.astype(jnp.int32)
    taps = jnp.where(pos == seq, 0.0, h * dec)
    taps_ref[...] = taps
    part = jnp.sum(jnp.abs(taps), axis=0, keepdims=True)

    @pl.when(pl.program_id(0) == 0)
    def _():
        asum_ref[...] = part

    @pl.when(pl.program_id(0) != 0)
    def _():
        asum_ref[...] = asum_ref[...] + part


def _hy_filter(seq, w1, b1, f1, w2, b2, f2, w3, hw):
    ar = jnp.arange(seq, dtype=jnp.int32)
    pos = jnp.concatenate([ar, (seq - ar) % seq]).astype(F32)[:, None]
    t = pos * (1.0 / (seq - 1))
    w = (2.0 * math.pi / seq) * pos
    bands = jnp.linspace(1e-4, HY_BANDS - 1, HY_BANDS, dtype=F32)[None, :]
    emb = 1 + 2 * HY_BANDS
    embp = ((emb + 7) // 8) * 8
    feat = jnp.concatenate([t, jnp.cos(bands * w), -jnp.sin(bands * w), jnp.zeros((2 * seq, embp - emb), F32)], axis=-1)
    w1p = jnp.concatenate([w1, jnp.zeros((embp - emb, w1.shape[1]), F32)], axis=0)
    max_decay = math.log(1.0 / HY_TARGET) / HY_FAST
    min_decay = math.log(1.0 / HY_TARGET) / HY_SLOW
    deltas = jnp.linspace(min_decay, max_decay, hw, dtype=F32)[None, :]
    feat = feat.reshape(seq, 2 * embp)
    hid = w1.shape[1]

    def pair(m):
        z = jnp.zeros_like(m)
        return jnp.concatenate([jnp.concatenate([m, z], axis=1), jnp.concatenate([z, m], axis=1)], axis=0)

    twice = lambda v: jnp.concatenate([v, v])[None, :]
    w3p = jnp.concatenate([pair(w3[:, :hw]), pair(w3[:, hw:])], axis=1)
    tile = min(SEQ_TILE, seq) // 2
    nfwd = seq // (2 * tile)
    full = lambda shape: pl.BlockSpec(shape, lambda i: (0,) * len(shape))
    taps, asum = pl.pallas_call(
        functools.partial(_filter_kernel, tile=tile, hw=hw, seq=seq, embp=embp),
        grid=(2 * nfwd,),
        in_specs=[
            pl.BlockSpec((tile, 2 * embp), lambda i: (i, 0)),
            full((2 * embp, 2 * hid)), full((1, 2 * hid)), full((1, 2 * hid)),
            full((2 * hid, 2 * hid)), full((1, 2 * hid)), full((1, 2 * hid)),
            pl.BlockSpec((2 * hid, 2 * hw), lambda i: (0, i // nfwd)), full((1, 2 * hw)),
        ],
        out_specs=[
            pl.BlockSpec((tile, 2 * hw), lambda i: (i, 0)),
            pl.BlockSpec((1, 2 * hw), lambda i: (0, 0)),
        ],
        out_shape=[
            jax.ShapeDtypeStruct((seq, 2 * hw), F32),
            jax.ShapeDtypeStruct((1, 2 * hw), F32),
        ],
        compiler_params=_cparams(("arbitrary",)),
    )(feat, pair(w1p), twice(b1), twice(f1), pair(w2), twice(b2), twice(f2), w3p,
      jnp.concatenate([deltas, deltas], axis=1))
    return taps.reshape(2 * seq, hw), asum[:, :hw] + asum[:, hw:]


def _np_split(a):
    a32 = jnp.asarray(a, F32)
    hi = a32.astype(BF16)
    lo = (a32 - hi.astype(F32)).astype(BF16)
    return hi, lo


def _dft_consts(n2):
    n1 = FFT_N1
    n = n1 * n2
    half = n1 // 2
    a1 = -2.0 * np.pi * np.outer(np.arange(n1), np.arange(n1)) / n1
    f1r, f1i = np.cos(a1), np.sin(a1)
    f1_data = np.block([[f1r[:, :half], -f1i[:, :half]], [f1i[:, :half], f1r[:, :half]]])
    f1_real = np.concatenate([f1r, f1i], axis=0)
    g1r, g1i = f1r / n, -f1i / n
    g1 = np.block([[g1r[:half], -g1i[:half]], [g1i[:half], g1r[:half]]])
    a2 = -2.0 * np.pi * np.outer(np.arange(n2), np.arange(n2)) / n2
    f2r, f2i = np.cos(a2), np.sin(a2)
    f2 = np.block([[f2r, -f2i], [f2i, f2r]])
    g2 = np.block([[f2r, f2i], [-f2i, f2r]])
    at = -2.0 * np.pi * np.outer(np.arange(n1), np.arange(n2)) / n
    twr = jnp.broadcast_to(jnp.asarray(np.cos(at), F32)[:, :, None], (n1, n2, LANES))
    twi = jnp.broadcast_to(jnp.asarray(np.sin(at), F32)[:, :, None], (n1, n2, LANES))
    return dict(f1_data=_np_split(f1_data), f1_real=_np_split(f1_real), g1=_np_split(g1),
                f2=_np_split(f2), g2=_np_split(g2), twr=twr, twi=twi)


def _fft1_kernel(z_ref, fh_ref, fl_ref, ar_ref, ai_ref):
    a = _dot3c(fh_ref[...], fl_ref[...], z_ref[...])
    ar_ref[...] = a[:FFT_N1]
    ai_ref[...] = a[FFT_N1:]


def _fft_first(zview, fmat, ncols):
    cb = min(FFT_COLS, ncols)
    fh, fl = fmat
    cspec = pl.BlockSpec(fh.shape, lambda j: (0, 0))
    ospec = pl.BlockSpec((FFT_N1, cb), lambda j: (0, j))
    return pl.pallas_call(
        _fft1_kernel,
        grid=(ncols // cb,),
        in_specs=[pl.BlockSpec((FFT_N1, cb), lambda j: (0, j)), cspec, cspec],
        out_specs=[ospec, ospec],
        out_shape=[jax.ShapeDtypeStruct((FFT_N1, ncols), F32)] * 2,
        compiler_params=_cparams(("arbitrary",)),
    )(zview, fh, fl)


def _fftmid_kernel(ar_ref, ai_ref, twr_ref, twi_ref, f2h_ref, f2l_ref, *rest, filter_only, n2):
    kb, _, width = ar_ref.shape
    reps = width // LANES
    lanes = lambda parts: jnp.concatenate(parts, axis=1)
    twr = lanes([t for j in range(kb) for t in [twr_ref[j]] * reps])
    twi = lanes([t for j in range(kb) for t in [twi_ref[j]] * reps])
    ar = lanes([ar_ref[j] for j in range(kb)])
    ai = lanes([ai_ref[j] for j in range(kb)])
    z = jnp.concatenate([ar * twr - ai * twi, ar * twi + ai * twr], axis=0)
    x = _dot3c(f2h_ref[...], f2l_ref[...], z)
    xr, xi = x[:n2], x[n2:]
    if filter_only:
        kr_out, ki_out = rest
        for j in range(kb):
            kr_out[j] = xr[:, j * width:(j + 1) * width]
            ki_out[j] = xi[:, j * width:(j + 1) * width]
        return
    kr_ref, ki_ref, g2h_ref, g2l_ref, br_out, bi_out = rest
    kr = lanes([kr_ref[j] for j in range(kb)])
    ki = lanes([ki_ref[j] for j in range(kb)])
    y = jnp.concatenate([xr * kr - xi * ki, xr * ki + xi * kr], axis=0)
    w = _dot3c(g2h_ref[...], g2l_ref[...], y)
    wr, wi = w[:n2], w[n2:]
    br = wr * twr + wi * twi
    bi = wi * twr - wr * twi
    for j in range(kb):
        br_out[j] = br[:, j * width:(j + 1) * width]
        bi_out[j] = bi[:, j * width:(j + 1) * width]


def _fft_mid(ar, ai, consts, n2, width, spectrum=None):
    a3r = ar.reshape(FFT_N1, n2, width)
    a3i = ai.reshape(FFT_N1, n2, width)
    blk = pl.BlockSpec((FFT_KB, n2, width), lambda k: (k, 0, 0))
    twspec = pl.BlockSpec((FFT_KB, n2, LANES), lambda k: (k, 0, 0))
    cspec = pl.BlockSpec((2 * n2, 2 * n2), lambda k: (0, 0))
    in_specs = [blk, blk, twspec, twspec, cspec, cspec]
    args = [a3r, a3i, consts["twr"], consts["twi"], *consts["f2"]]
    if spectrum is not None:
        in_specs += [blk, blk, cspec, cspec]
        args += [spectrum[0], spectrum[1], *consts["g2"]]
    return pl.pallas_call(
        functools.partial(_fftmid_kernel, filter_only=spectrum is None, n2=n2),
        grid=(FFT_N1 // FFT_KB,),
        in_specs=in_specs,
        out_specs=[blk, blk],
        out_shape=[jax.ShapeDtypeStruct((FFT_N1, n2, width), F32)] * 2,
        compiler_params=_cparams(("arbitrary",)),
    )(*args)


def _fftlast_kernel(br_ref, bi_ref, gh_ref, gl_ref, u_ref, x0_ref, bias_ref, invn_ref, o_ref):
    b = jnp.concatenate([br_ref[...], bi_ref[...]], axis=0)
    y = _dot3c(gh_ref[...], gl_ref[...], b)
    o_ref[...] = ((y * invn_ref[...] + u_ref[...] * bias_ref[...]) * x0_ref[...]).astype(BF16)


def _fft_last(br, bi, gmat, uview, x0view, bias_t, invn_t, ncols):
    cb = min(FFT_COLS, ncols)
    gh, gl = gmat
    cspec = pl.BlockSpec(gh.shape, lambda j: (0, 0))
    dspec = pl.BlockSpec((FFT_N1, cb), lambda j: (0, j))
    vspec = pl.BlockSpec((1, cb), lambda j: (0, 0))
    return pl.pallas_call(
        _fftlast_kernel,
        grid=(ncols // cb,),
        in_specs=[dspec, dspec, cspec, cspec, dspec, dspec, vspec, vspec],
        out_specs=dspec,
        out_shape=jax.ShapeDtypeStruct((FFT_N1, ncols), BF16),
        compiler_params=_cparams(("arbitrary",)),
    )(br, bi, gh, gl, uview, x0view, bias_t, invn_t)


def _dft_small_consts(seq):
    n = 2 * seq
    a = -2.0 * np.pi * np.outer(np.arange(n), np.arange(n)) / n
    fr, fi = np.cos(a), np.sin(a)
    f_data = np.block([[fr[:, :seq], -fi[:, :seq]], [fi[:, :seq], fr[:, :seq]]])
    f_real = np.concatenate([fr, fi], axis=0)
    gr, gi = fr / n, -fi / n
    g = np.block([[gr[:seq], -gi[:seq]], [gi[:seq], gr[:seq]]])
    return _np_split(f_data), _np_split(f_real), _np_split(g)


def _hy_ctx_kernel(u_ref, x0_ref, taps_ref, fdh_ref, fdl_ref, frh_ref, frl_ref, gh_ref, gl_ref,
                   bias_ref, invn_ref, o_ref):
    n = taps_ref.shape[0]
    z = u_ref[...]
    a = _dot3c(fdh_ref[...], fdl_ref[...], z)
    k = _dot3c(frh_ref[...], frl_ref[...], taps_ref[...])
    ar, ai, kr, ki = a[:n], a[n:], k[:n], k[n:]
    y = jnp.concatenate([ar * kr - ai * ki, ar * ki + ai * kr], axis=0)
    conv = _dot3c(gh_ref[...], gl_ref[...], y)
    o_ref[...] = ((conv * invn_ref[...] + z * bias_ref[...]) * x0_ref[...]).astype(BF16)


def _hy_ctx(u, x0, taps, bias, invn, small):
    rows, hw = u.shape
    full = lambda a: pl.BlockSpec(a.shape, lambda i: (0,) * a.ndim)
    args = [u, x0, taps, *small[0], *small[1], *small[2], bias, invn]
    return pl.pallas_call(
        _hy_ctx_kernel,
        grid=(1,),
        in_specs=[full(a) for a in args],
        out_specs=pl.BlockSpec((rows, hw), lambda i: (0, 0)),
        out_shape=jax.ShapeDtypeStruct((rows, hw), BF16),
        compiler_params=_cparams(("arbitrary",)),
    )(*args)


def _out_proj_kernel(x_ref, ypl_ref, yhl_ref, yal_ref, ypc_ref, yhc_ref, yac_ref, w_ref, g_ref, mod_ref, *rest,
                     nlt, tpb, ncond, pw, hw, moe, ntiles):
    if moe:
        rw_ref, xo_ref, u_ref, route_ref, un_ref = rest
    else:
        wg_ref, wu_ref, wd_ref, xo_ref = rest
    t = jnp.minimum(pl.program_id(0), ntiles - 1)
    if moe:
        @pl.when(pl.program_id(0) == 0)
        def _():
            un_ref[...] = jnp.zeros_like(un_ref)

        logits = _dot3(un_ref[...], rw_ref[...])
    ci = _cond_row(t, nlt, tpb, ncond)
    is_ctx = t >= nlt
    yp = jnp.where(is_ctx, ypc_ref[...], ypl_ref[...])
    yh = jnp.where(is_ctx, yhc_ref[...], yhl_ref[...])
    ya = jnp.where(is_ctx, yac_ref[...], yal_ref[...])
    mix = _dot(yp, w_ref[0:pw, :]) + _dot(yh, w_ref[pw:pw + hw, :]) + _dot(ya, w_ref[pw + hw:, :])
    x = x_ref[...] + mod_ref[2, pl.ds(ci, 1), :] * mix
    un = _normmod(x, g_ref[...], mod_ref[3, pl.ds(ci, 1), :], mod_ref[4, pl.ds(ci, 1), :])
    if not moe:
        u = un.astype(BF16)
        ff = wg_ref.shape[1]
        fc = ff // FFN_CHUNKS
        y = jnp.zeros(x.shape, F32)
        for c in range(FFN_CHUNKS):
            gate = _dot(u, wg_ref[:, c * fc:(c + 1) * fc])
            up = _dot(u, wu_ref[:, c * fc:(c + 1) * fc])
            y = y + _dot((_silu(gate) * up).astype(BF16), wd_ref[c * fc:(c + 1) * fc, :])
        xo_ref[...] = x + mod_ref[5, pl.ds(ci, 1), :] * y
    else:
        xo_ref[...] = x
        u_ref[...] = un.astype(BF16)
        un_ref[...] = un
        lane = lax.broadcasted_iota(jnp.int32, logits.shape, 1)
        neg = jnp.float32(-jnp.inf)
        lg = jnp.where(lane < N_EXPERTS, logits, neg)
        t1 = jnp.max(lg, axis=-1, keepdims=True)
        i1 = jnp.min(jnp.where(lg == t1, lane, LANES), axis=-1, keepdims=True)
        lg2 = jnp.where(lane == i1, neg, lg)
        t2 = jnp.max(lg2, axis=-1, keepdims=True)
        i2 = jnp.min(jnp.where(lg2 == t2, lane, LANES), axis=-1, keepdims=True)
        e2 = jnp.exp(t2 - t1)
        g1 = 1.0 / (1.0 + e2)
        g2 = e2 / (1.0 + e2)
        route_ref[...] = jnp.where(lane == 0, i1.astype(F32), jnp.where(lane == 1, i2.astype(F32),
                                   jnp.where(lane == 2, g1, jnp.where(lane == 3, g2, 0.0))))


def _out_proj(x, lat, ctx, w, g, mod, rw, ffn_w, *, nlt, tpb, ncond, out_rows):
    n, d = x.shape
    pw, hw, aw = (a.shape[1] for a in lat)
    tm = ROW_TILE
    moe = rw is not None
    nt = (n if moe else out_rows) // tm
    cur = lambda t: jnp.minimum(t, nt - 1)
    row = lambda width: pl.BlockSpec((tm, width), lambda t: (cur(t), 0))
    lrow = lambda width: pl.BlockSpec((tm, width), lambda t: (jnp.minimum(t, nlt - 1), 0))
    crow = lambda width: pl.BlockSpec((tm, width), lambda t: (jnp.maximum(cur(t) - nlt, 0), 0))
    in_specs = [row(d), lrow(pw), lrow(hw), lrow(aw), crow(pw), crow(hw), crow(aw),
                pl.BlockSpec(w.shape, lambda t: (0, 0)),
                pl.BlockSpec((1, d), lambda t: (0, 0)),
                pl.BlockSpec(mod.shape, lambda t: (0, 0, 0))]
    args = [x, *lat, *ctx, w, g, mod]
    scratch = []
    if not moe:
        const = lambda a: pl.BlockSpec(a.shape, lambda t: (0, 0), pipeline_mode=pl.Buffered(1))
        in_specs += [const(a) for a in ffn_w]
        args += list(ffn_w)
        out_specs = [row(d)]
        out_shape = [jax.ShapeDtypeStruct((out_rows, d), F32)]
    else:
        out_specs = [row(d), row(d)]
        out_shape = [jax.ShapeDtypeStruct((n, d), F32), jax.ShapeDtypeStruct((n, d), BF16)]
        in_specs.append(pl.BlockSpec(rw.shape, lambda t: (0, 0)))
        args.append(rw)
        out_specs.append(pl.BlockSpec((tm, LANES), lambda t: (jnp.maximum(t - 1, 0), 0)))
        out_shape.append(jax.ShapeDtypeStruct((n, LANES), F32))
        scratch.append(pltpu.VMEM((tm, d), F32))
    kern = functools.partial(_out_proj_kernel, nlt=nlt, tpb=tpb, ncond=ncond, pw=pw, hw=hw, moe=moe, ntiles=nt)
    return pl.pallas_call(
        kern, grid=(nt + 1 if moe else nt,), in_specs=in_specs, out_specs=out_specs, out_shape=out_shape,
        scratch_shapes=scratch,
        compiler_params=_cparams(("arbitrary",)),
    )(*args)


def _moe_kernel(te_ref, tv_ref, u_ref, wg_ref, wu_ref, wd_ref, *rest):
    o_ref = rest[-1]
    @pl.when(tv_ref[pl.program_id(0)] > 0)
    def _():
        u = u_ref[...]
        ff = wg_ref.shape[3]
        fc = ff // MOE_CHUNKS
        y = jnp.zeros(o_ref.shape, F32)
        for c in range(MOE_CHUNKS):
            sl = slice(c * fc, (c + 1) * fc)
            h = (_silu(_dot(u, wg_ref[0, 0, :, sl])) * _dot(u, wu_ref[0, 0, :, sl])).astype(BF16)
            y = y + _dot(h, wd_ref[0, 0, sl, :])
        o_ref[...] = y.astype(BF16)


def _moe_experts(ug, tile_expert, tile_valid, wg, wu, wd, j, *, tile0, total_rows, prev=None):
    p, d = ug.shape
    ff = wg.shape[3]
    tm = MOE_TM
    one = pl.Buffered(1)
    in_specs = [
        pl.BlockSpec((tm, d), lambda t, te, tv: (t, 0)),
        pl.BlockSpec((1, 1, d, ff), lambda t, te, tv: (j, te[t], 0, 0), pipeline_mode=one),
        pl.BlockSpec((1, 1, d, ff), lambda t, te, tv: (j, te[t], 0, 0), pipeline_mode=one),
        pl.BlockSpec((1, 1, ff, d), lambda t, te, tv: (j, te[t], 0, 0), pipeline_mode=one),
    ]
    args = [tile_expert, tile_valid, ug, wg, wu, wd]
    aliases = {}
    if prev is not None:
        in_specs.append(pl.BlockSpec(memory_space=pl.ANY))
        args.append(prev)
        aliases = {len(args) - 1: 0}
    grid_spec = pltpu.PrefetchScalarGridSpec(
        num_scalar_prefetch=2,
        grid=(p // tm,),
        in_specs=in_specs,
        out_specs=pl.BlockSpec((tm, d), lambda t, te, tv: (t + tile0, 0)),
    )
    return pl.pallas_call(
        _moe_kernel, grid_spec=grid_spec,
        out_shape=jax.ShapeDtypeStruct((total_rows, d), BF16),
        input_output_aliases=aliases,
        compiler_params=_cparams(("arbitrary",)),
    )(*args)


def _moe_combine_kernel(x_ref, ya_ref, yb_ref, route_ref, mod_ref, o_ref, *, nlt, tpb, ncond):
    ci = _cond_row(pl.program_id(0), nlt, tpb, ncond)
    r = route_ref[...]
    lane = lax.broadcasted_iota(jnp.int32, r.shape, 1)
    g1 = jnp.sum(jnp.where(lane == 2, r, 0.0), axis=-1, keepdims=True)
    g2 = jnp.sum(jnp.where(lane == 3, r, 0.0), axis=-1, keepdims=True)
    y = g1 * ya_ref[...].astype(F32) + g2 * yb_ref[...].astype(F32)
    o_ref[...] = x_ref[...] + mod_ref[5, pl.ds(ci, 1), :] * y


def _moe_combine(x, ya, yb, route, mod, *, nlt, tpb, ncond, out_rows):
    d = x.shape[1]
    tm = ROW_TILE
    row = lambda width: pl.BlockSpec((tm, width), lambda t: (t, 0))
    kern = functools.partial(_moe_combine_kernel, nlt=nlt, tpb=tpb, ncond=ncond)
    return pl.pallas_call(
        kern,
        grid=(out_rows // tm,),
        in_specs=[row(d), row(d), row(d), row(LANES), pl.BlockSpec(mod.shape, lambda t: (0, 0, 0))],
        out_specs=row(d),
        out_shape=jax.ShapeDtypeStruct((out_rows, d), F32),
        compiler_params=_cparams(("arbitrary",)),
    )(x, ya, yb, route, mod)


def _moe_layer(x, u, route, mod, wg, wu, wd, j, *, nlt, tpb, ncond, out_rows):
    n, d = x.shape
    tm = MOE_TM
    experts = jnp.concatenate([route[:, 0], route[:, 1]]).astype(jnp.int32)
    onehot = (experts[:, None] == jnp.arange(N_EXPERTS, dtype=jnp.int32)[None, :]).astype(jnp.int32)
    rank = jnp.sum(onehot * (jnp.cumsum(onehot, axis=0) - 1), axis=1)
    counts = jnp.sum(onehot, axis=0)
    padded = ((counts + tm - 1) // tm) * tm
    ends = jnp.cumsum(padded)
    starts = ends - padded
    dest = starts[experts] + rank
    p = 2 * n + N_EXPERTS * tm
    tokens = jnp.concatenate([jnp.arange(n, dtype=jnp.int32)] * 2)
    inb = dict(mode="promise_in_bounds")
    src = jnp.zeros((p,), jnp.int32).at[dest].set(tokens, unique_indices=True, **inb)
    tile_start = jnp.arange(p // tm, dtype=jnp.int32) * tm
    tile_expert = jnp.minimum(jnp.searchsorted(ends, tile_start, side="right"), N_EXPERTS - 1).astype(jnp.int32)
    tile_valid = (tile_start < ends[-1]).astype(jnp.int32)
    nt = p // tm
    cuts = [nt * k // MOE_PARTS for k in range(MOE_PARTS + 1)]
    ys = None
    for lo, hi in zip(cuts[:-1], cuts[1:]):
        ug = u.at[src[lo * tm:hi * tm]].get(**inb)
        ys = _moe_experts(ug, tile_expert[lo:hi], tile_valid[lo:hi], wg, wu, wd, j,
                          tile0=lo, total_rows=p, prev=ys)
    ya = ys.at[dest[:out_rows]].get(**inb)
    yb = ys.at[dest[n:n + out_rows]].get(**inb)
    return _moe_combine(x, ya, yb, route, mod, nlt=nlt, tpb=tpb, ncond=ncond, out_rows=out_rows)


def _rope_tables(seq, n_ctx_rows, batch):
    rows = seq // GRID_W
    row = jnp.repeat(jnp.arange(rows, dtype=F32), GRID_W)
    col = jnp.broadcast_to(jnp.arange(GRID_W, dtype=F32), (rows, GRID_W)).reshape(-1)
    inv_freq = jnp.power(ROPE_THETA, -jnp.arange(ROPE_FREQS, dtype=F32) / ROPE_FREQS)
    ar = row[:, None] * inv_freq
    ac = col[:, None] * inv_freq
    cos = jnp.concatenate([jnp.cos(ar), jnp.cos(ar), jnp.cos(ac), jnp.cos(ac)], axis=-1)
    sin = jnp.concatenate([-jnp.sin(ar), jnp.sin(ar), -jnp.sin(ac), jnp.sin(ac)], axis=-1)
    cos = jnp.tile(jnp.concatenate([cos, cos], axis=-1), (batch, 1))
    sin = jnp.tile(jnp.concatenate([sin, sin], axis=-1), (batch, 1))
    cos = jnp.concatenate([cos, jnp.ones((n_ctx_rows, 2 * HEAD_DIM), F32)], axis=0)
    sin = jnp.concatenate([sin, jnp.zeros((n_ctx_rows, 2 * HEAD_DIM), F32)], axis=0)
    return cos, sin


def kernel(x, c, ctx, c_ctx, mod_w, mod_b, norm1_g, norm2_g, w_in, w_out, pool_lin, pool_scale, hy_short_w, hy_short_b, hy_f_w1, hy_f_b1, hy_f_freq1, hy_f_w2, hy_f_b2, hy_f_freq2, hy_f_w3, hy_bias, qk_norm_g, diff_lambda, subln_g, ffn_w_gate, ffn_w_up, ffn_w_down, router_w, moe_w_gate, moe_w_up, moe_w_down):
    batch, seq, d = x.shape
    ctx_len = ctx.shape[1]
    depth = mod_w.shape[0]
    pw = pool_scale.shape[1]
    hw = hy_bias.shape[1]
    nl, nc = batch * seq, batch * ctx_len
    n = nl + nc
    tm = ROW_TILE
    assert seq % tm == 0 and nc % tm == 0 and seq % (FFT_N1 // 2) == 0 and seq % GRID_W == 0
    assert d == HEADS * 2 * V_DIM and pw == hw and batch == 2
    n2 = 2 * seq // FFT_N1
    ncols = n2 * hw
    assert (n * hw) % ncols == 0
    tiles = dict(nlt=nl // tm, tpb=seq // tm, ncond=batch + 1)

    xs = jnp.concatenate([x.reshape(nl, d), ctx.reshape(nc, d)], axis=0)
    cond8 = jnp.concatenate([c, c_ctx[None, :], jnp.zeros((8 - batch - 1, d), F32)], axis=0)
    mods = _modvec(cond8, mod_w, mod_b)
    cos_t, sin_t = _rope_tables(seq, nc, batch)
    consts = _dft_consts(n2)
    small = _dft_small_consts(ctx_len)
    eye = jnp.eye(len(POOL_WINDOWS), dtype=F32)
    n_moe = moe_w_gate.shape[0]
    assert n_moe <= depth
    moe_w2d = [w.reshape(-1, w.shape[-1]) for w in (moe_w_gate, moe_w_up, moe_w_down)]
    moe_bf16 = {}

    for i in range(depth):
        last = i == depth - 1
        lam_init = 0.8 - 0.6 * math.exp(-0.3 * i)
        mod = mods[i]
        gqk = jnp.concatenate([qk_norm_g[i], qk_norm_g[i]], axis=-1)
        zp, zh, qt, k12, vt = _in_proj(xs, norm1_g[i][None, :], mod, w_in[i].astype(BF16), cos_t, sin_t, gqk,
                                       pw=pw, hw=hw, **tiles)

        g_b = jnp.broadcast_to(subln_g[i][:, None], (V_DIM, ATT_TQ))
        att_kw = dict(batch=batch, seq=seq, ctx_len=ctx_len, lam_init=lam_init)

        cast_j = i if i < n_moe else None

        def attend(direct, lam_p=diff_lambda[i], qt=qt, k12=k12, vt=vt, g_b=g_b, att_kw=att_kw, cast_j=cast_j):
            ctx_out = _attention(lam_p, qt, k12, vt, g_b, latent=False, direct=direct, **att_kw)
            if cast_j is None:
                return (_attention(lam_p, qt, k12, vt, g_b, latent=True, direct=direct, **att_kw), ctx_out)
            if direct:
                lat_out, *wb = _attention(lam_p, qt, k12, vt, g_b, latent=True, direct=True,
                                          cast=[(w, cast_j, n_moe) for w in moe_w2d], **att_kw)
            else:
                lat_out = _attention(lam_p, qt, k12, vt, g_b, latent=True, direct=False, **att_kw)
                wb = _cast_bf16(*[w[cast_j * (w.shape[0] // n_moe):(cast_j + 1) * (w.shape[0] // n_moe)]
                                  for w in moe_w2d])
            return (lat_out, ctx_out, *wb)

        bound = (HEAD_DIM * QSCALE) * jnp.max(jnp.abs(qk_norm_g[i][0])) * jnp.max(jnp.abs(qk_norm_g[i][1]))
        ya_l, ya_c, *wb = lax.cond(bound * 1.02 < ATT_DIRECT_MAX, lambda: attend(True), lambda: attend(False))
        if cast_j is not None:
            moe_bf16[cast_j] = tuple(w.reshape((1,) + src.shape[1:])
                                     for w, src in zip(wb, (moe_w_gate, moe_w_up, moe_w_down)))

        lin_bd = (eye[:, None, :, None] * pool_lin[i][:, :, None, :]).reshape(pw, pw).astype(BF16)
        pscale = pool_scale[i][None, :]
        yp_l = _pool(zp, lin_bd, pscale, row0=0, batch=batch, seq=seq, out_rows=nl, out_row0=0)
        yp_c = _pool(zp, lin_bd, pscale, row0=nl, batch=batch, seq=ctx_len, out_rows=nc, out_row0=0)

        sw, sb = hy_short_w[i], hy_short_b[i][None, :]
        u_l, x0_l = _hy_pre(zh, sw, sb, row0=0, batch=batch, seq=seq, out_rows=n, out_row0=0)
        u_c, x0_c = _hy_pre(zh, sw, sb, row0=nl, batch=batch, seq=ctx_len, out_rows=nc, out_row0=0)
        filt = (hy_f_w1[i], hy_f_b1[i], hy_f_freq1[i], hy_f_w2[i], hy_f_b2[i], hy_f_freq2[i], hy_f_w3[i])
        bias = hy_bias[i][None, :]
        taps, asum = _hy_filter(seq, *filt, hw)
        kr1, ki1 = _fft_first(taps.reshape(FFT_N1, ncols), consts["f1_real"], ncols)
        spectrum = _fft_mid(kr1, ki1, consts, n2, hw)
        ar, ai = _fft_first(u_l.reshape(-1, ncols), consts["f1_data"], ncols)
        br, bi = _fft_mid(ar, ai, consts, n2, hw, spectrum=spectrum)
        reps = min(FFT_COLS, ncols) // hw
        yh_lat = _fft_last(br.reshape(FFT_N1, ncols), bi.reshape(FFT_N1, ncols), consts["g1"],
                           u_l.reshape(-1, ncols), x0_l.reshape(-1, ncols),
                           jnp.tile(bias, (1, reps)), jnp.tile(1.0 / asum, (1, reps)), ncols)
        taps_c, asum_c = _hy_filter(ctx_len, *filt, hw)
        yh_ctx = _hy_ctx(u_c, x0_c, taps_c, bias, 1.0 / asum_c, small)

        j = i // 2
        moe = i % 2 == 1
        rw = ffn_w = None
        if moe:
            rw = jnp.concatenate([router_w[j], jnp.zeros((d, LANES - N_EXPERTS), F32)], axis=1)
        else:
            ffn_w = (ffn_w_gate[j].astype(BF16), ffn_w_up[j].astype(BF16), ffn_w_down[j].astype(BF16))
        out_rows = nl if last else n
        outs = _out_proj(xs, (yp_l, yh_lat.reshape(nl, hw), ya_l), (yp_c, yh_ctx, ya_c), w_out[i].astype(BF16),
                         norm2_g[i][None, :], mod, rw, ffn_w, out_rows=out_rows, **tiles)
        if moe:
            xs, u, route = outs
            xs = _moe_layer(xs, u, route, mod, *moe_bf16[j], 0, out_rows=out_rows, **tiles)
        else:
            xs, = outs
    return xs[:nl].reshape(batch, seq, d)
```
